```python
import jax, jax.numpy as jnp
from jax import lax
import numpy as np

D_MODEL = 1024
BATCH = 8
SEQ = 4096
DEPTH = 2

HEAD_DIM = 64
SGU_WIDTH = 3 * D_MODEL // 8
CONV_WIDTH = 3 * D_MODEL // 8
POOL_WIDTH = D_MODEL - SGU_WIDTH - CONV_WIDTH
SGU_HEADS = SGU_WIDTH // HEAD_DIM
CHUNK = 128
CONV_K = 31
POOL_WINDOWS = (2, 4, 8, 16)
POOL_GROUPS = len(POOL_WINDOWS)
POOL_GDIM = POOL_WIDTH // POOL_GROUPS
IN_WIDTH = 2 * SGU_WIDTH + 2 * CONV_WIDTH + POOL_WIDTH
D_FF = ((8 * D_MODEL // 3 + 127) // 128) * 128
FFN_CONV_K = 3
N_MOD = 6
EPS = 1e-6
MOD_INIT_STD = 0.02

kernel_name = "hybrid_sgu_conformer_pool_convffn_block"


def rms_norm(x, g):
    xf = x.astype(jnp.float32)
    y = xf * lax.rsqrt(jnp.mean(xf * xf, axis=-1, keepdims=True) + EPS)
    return (y * g.astype(jnp.float32)).astype(x.dtype)


def layer_norm(x, g, b):
    xf = x.astype(jnp.float32)
    mu = jnp.mean(xf, axis=-1, keepdims=True)
    var = jnp.mean(jnp.square(xf - mu), axis=-1, keepdims=True)
    y = (xf - mu) * lax.rsqrt(var + EPS)
    return (y * g.astype(jnp.float32) + b.astype(jnp.float32)).astype(x.dtype)


def causal_depthwise_conv(x, w, b):
    k, ch = w.shape
    y = lax.conv_general_dilated(
        x, w[:, None, :].astype(x.dtype), window_strides=(1,), padding=[(k - 1, 0)],
        dimension_numbers=("NWC", "WIO", "NWC"), feature_group_count=ch)
    return y + b


def spatial_gating(z, norm_g, norm_b, w_s, b_s):
    bsz, s, _ = z.shape
    u, v = jnp.split(z, 2, axis=-1)
    v = v.reshape(bsz, s // CHUNK, CHUNK, SGU_HEADS, HEAD_DIM)
    v = layer_norm(v, norm_g.reshape(SGU_HEADS, HEAD_DIM), norm_b.reshape(SGU_HEADS, HEAD_DIM))
    mask = jnp.tril(jnp.ones((CHUNK, CHUNK), dtype=bool))
    w = jnp.where(mask[None], w_s, jnp.zeros_like(w_s))
    f = jnp.einsum("hts,bnshd->bnthd", w, v) + b_s.T[None, None, :, :, None]
    return u * f.reshape(bsz, s, SGU_WIDTH)


def conformer_conv(z, conv_w, conv_b, norm_g, norm_b):
    a, g = jnp.split(z, 2, axis=-1)
    h = a * jax.nn.sigmoid(g)
    h = causal_depthwise_conv(h, conv_w, conv_b)
    return jax.nn.silu(layer_norm(h, norm_g, norm_b))


def multiscale_pool(z, pool_w, pool_scale):
    bsz, s, _ = z.shape
    zf = z.astype(jnp.float32)
    cs = jnp.pad(jnp.cumsum(zf, axis=1), ((0, 0), (1, 0), (0, 0)))
    pos1 = jnp.arange(1, s + 1, dtype=jnp.int32)
    means = []
    for gi, win in enumerate(POOL_WINDOWS):
        sl = slice(gi * POOL_GDIM, (gi + 1) * POOL_GDIM)
        hi = cs[:, 1:, sl]
        lo = jnp.pad(cs[:, : s + 1 - win, sl], ((0, 0), (win - 1, 0), (0, 0)))
        count = jnp.minimum(pos1, win).astype(jnp.float32)[None, :, None]
        means.append((hi - lo) / count)
    d = (jnp.concatenate(means, axis=-1) - zf).astype(z.dtype)
    d = d.reshape(bsz, s, POOL_GROUPS, POOL_GDIM)
    y = jnp.einsum("bsgc,gcd->bsgd", d, pool_w).reshape(bsz, s, POOL_WIDTH)
    return y * pool_scale


def _fwd_setup_inputs(seed: int = 0) -> dict:
    key = jax.random.key(seed)
    ks = jax.random.split(key, 32)
    L, D = DEPTH, D_MODEL
    nrm = lambda k, shape, std: (jax.random.normal(k, shape, jnp.float32) * std)
    gain = lambda k, shape: 1.0 + 0.05 * jax.random.normal(k, shape, jnp.float32)
    return {
        "x": jax.random.normal(ks[0], (BATCH, SEQ, D), jnp.float32),
        "c": jax.random.normal(ks[1], (BATCH, D), jnp.float32),
        "mod_w": nrm(ks[2], (L, D, N_MOD * D), MOD_INIT_STD),
        "mod_b": nrm(ks[3], (L, N_MOD * D), 0.01),
        "mix_pre_g": gain(ks[4], (L, D)),
        "mix_post_g": gain(ks[5], (L, D)),
        "w_in": nrm(ks[6], (L, D, IN_WIDTH), D ** -0.5),
        "sgu_norm_g": gain(ks[7], (L, SGU_WIDTH)),
        "sgu_norm_b": nrm(ks[8], (L, SGU_WIDTH), 0.02),
        "sgu_w": nrm(ks[9], (L, SGU_HEADS, CHUNK, CHUNK), CHUNK ** -0.5),
        "sgu_b": gain(ks[10], (L, SGU_HEADS, CHUNK)),
        "conv_w": nrm(ks[11], (L, CONV_K, CONV_WIDTH), CONV_K ** -0.5),
        "conv_b": nrm(ks[12], (L, CONV_WIDTH), 0.02),
        "conv_norm_g": gain(ks[13], (L, CONV_WIDTH)),
        "conv_norm_b": nrm(ks[14], (L, CONV_WIDTH), 0.02),
        "pool_w": nrm(ks[15], (L, POOL_GROUPS, POOL_GDIM, POOL_GDIM), POOL_GDIM ** -0.5),
        "pool_scale": gain(ks[16], (L, POOL_WIDTH)),
        "branch_g": gain(ks[17], (L, D)),
        "w_out": nrm(ks[18], (L, D, D), D ** -0.5),
        "ffn_pre_g": gain(ks[19], (L, D)),
        "ffn_post_g": gain(ks[20], (L, D)),
        "ffn_up": nrm(ks[21], (L, D, 2 * D_FF), D ** -0.5),
        "ffn_conv_w": nrm(ks[22], (L, FFN_CONV_K, 2 * D_FF), FFN_CONV_K ** -0.5),
        "ffn_conv_b": nrm(ks[23], (L, 2 * D_FF), 0.02),
        "ffn_down": nrm(ks[24], (L, D_FF, D), D_FF ** -0.5),
    }


def _fwd_reference(x, c, mod_w, mod_b, mix_pre_g, mix_post_g, w_in, sgu_norm_g, sgu_norm_b, sgu_w, sgu_b,
              conv_w, conv_b, conv_norm_g, conv_norm_b, pool_w, pool_scale, branch_g, w_out,
              ffn_pre_g, ffn_post_g, ffn_up, ffn_conv_w, ffn_conv_b, ffn_down):
    sc = jax.nn.silu(c)
    for l in range(DEPTH):
        mod = sc @ mod_w[l] + mod_b[l]
        sh1, sc1, g1, sh2, sc2, g2 = [m[:, None, :] for m in jnp.split(mod, N_MOD, axis=-1)]

        h = rms_norm(x, mix_pre_g[l]) * (1.0 + sc1) + sh1
        z = h @ w_in[l]
        z_a, z_b, z_c = jnp.split(z, [2 * SGU_WIDTH, 2 * SGU_WIDTH + 2 * CONV_WIDTH], axis=-1)
        y_a = spatial_gating(jax.nn.gelu(z_a), sgu_norm_g[l], sgu_norm_b[l], sgu_w[l], sgu_b[l])
        y_b = conformer_conv(z_b, conv_w[l], conv_b[l], conv_norm_g[l], conv_norm_b[l])
        y_c = multiscale_pool(z_c, pool_w[l], pool_scale[l])
        ga, gb, gc = jnp.split(branch_g[l], [SGU_WIDTH, SGU_WIDTH + CONV_WIDTH])
        y = jnp.concatenate([rms_norm(y_a, ga), rms_norm(y_b, gb), rms_norm(y_c, gc)], axis=-1)
        y = y @ w_out[l]
        x = x + g1 * rms_norm(y, mix_post_g[l])

        h = rms_norm(x, ffn_pre_g[l]) * (1.0 + sc2) + sh2
        u = causal_depthwise_conv(h @ ffn_up[l], ffn_conv_w[l], ffn_conv_b[l])
        ug, uv = jnp.split(u, 2, axis=-1)
        y = (jax.nn.gelu(ug) * uv) @ ffn_down[l]
        x = x + g2 * rms_norm(y, ffn_post_g[l])
    return x


import jax as _jax
import jax.numpy as _jnp

TWIN_FORMAT = 'train_step'
FWD_PARAMS = ['x', 'c', 'mod_w', 'mod_b', 'mix_pre_g', 'mix_post_g', 'w_in', 'sgu_norm_g', 'sgu_norm_b', 'sgu_w', 'sgu_b', 'conv_w', 'conv_b', 'conv_norm_g', 'conv_norm_b', 'pool_w', 'pool_scale', 'branch_g', 'w_out', 'ffn_pre_g', 'ffn_post_g', 'ffn_up', 'ffn_conv_w', 'ffn_conv_b', 'ffn_down']
TWIN_WEIGHTS = ['mod_w', 'mod_b', 'mix_pre_g', 'mix_post_g', 'w_in', 'sgu_norm_g', 'sgu_norm_b', 'sgu_w', 'sgu_b', 'conv_w', 'conv_b', 'conv_norm_g', 'conv_norm_b', 'pool_w', 'pool_scale', 'branch_g', 'w_out', 'ffn_pre_g', 'ffn_post_g', 'ffn_up', 'ffn_conv_w', 'ffn_conv_b', 'ffn_down']
TWIN_DIFF_INPUT = 'x'
TWIN_INPUTS = ['x', 'c', 'mod_w', 'mod_b', 'mix_pre_g', 'mix_post_g', 'w_in', 'sgu_norm_g', 'sgu_norm_b', 'sgu_w', 'sgu_b', 'conv_w', 'conv_b', 'conv_norm_g', 'conv_norm_b', 'pool_w', 'pool_scale', 'branch_g', 'w_out', 'ffn_pre_g', 'ffn_post_g', 'ffn_up', 'ffn_conv_w', 'ffn_conv_b', 'ffn_down', 'loss_target', 'm_mod_w', 'm_mod_b', 'm_mix_pre_g', 'm_mix_post_g', 'm_w_in', 'm_sgu_norm_g', 'm_sgu_norm_b', 'm_sgu_w', 'm_sgu_b', 'm_conv_w', 'm_conv_b', 'm_conv_norm_g', 'm_conv_norm_b', 'm_pool_w', 'm_pool_scale', 'm_branch_g', 'm_w_out', 'm_ffn_pre_g', 'm_ffn_post_g', 'm_ffn_up', 'm_ffn_conv_w', 'm_ffn_conv_b', 'm_ffn_down', 'v_mod_w', 'v_mod_b', 'v_mix_pre_g', 'v_mix_post_g', 'v_w_in', 'v_sgu_norm_g', 'v_sgu_norm_b', 'v_sgu_w', 'v_sgu_b', 'v_conv_w', 'v_conv_b', 'v_conv_norm_g', 'v_conv_norm_b', 'v_pool_w', 'v_pool_scale', 'v_branch_g', 'v_w_out', 'v_ffn_pre_g', 'v_ffn_post_g', 'v_ffn_up', 'v_ffn_conv_w', 'v_ffn_conv_b', 'v_ffn_down']
TWIN_OUTPUTS = ['loss', 'grad_x', 'grad_mod_w', 'grad_mod_b', 'grad_mix_pre_g', 'grad_mix_post_g', 'grad_w_in', 'grad_sgu_norm_g', 'grad_sgu_norm_b', 'grad_sgu_w', 'grad_sgu_b', 'grad_conv_w', 'grad_conv_b', 'grad_conv_norm_g', 'grad_conv_norm_b', 'grad_pool_w', 'grad_pool_scale', 'grad_branch_g', 'grad_w_out', 'grad_ffn_pre_g', 'grad_ffn_post_g', 'grad_ffn_up', 'grad_ffn_conv_w', 'grad_ffn_conv_b', 'grad_ffn_down', 'delta_mod_w', 'delta_mod_b', 'delta_mix_pre_g', 'delta_mix_post_g', 'delta_w_in', 'delta_sgu_norm_g', 'delta_sgu_norm_b', 'delta_sgu_w', 'delta_sgu_b', 'delta_conv_w', 'delta_conv_b', 'delta_conv_norm_g', 'delta_conv_norm_b', 'delta_pool_w', 'delta_pool_scale', 'delta_branch_g', 'delta_w_out', 'delta_ffn_pre_g', 'delta_ffn_post_g', 'delta_ffn_up', 'delta_ffn_conv_w', 'delta_ffn_conv_b', 'delta_ffn_down', 'new_m_mod_w', 'new_m_mod_b', 'new_m_mix_pre_g', 'new_m_mix_post_g', 'new_m_w_in', 'new_m_sgu_norm_g', 'new_m_sgu_norm_b', 'new_m_sgu_w', 'new_m_sgu_b', 'new_m_conv_w', 'new_m_conv_b', 'new_m_conv_norm_g', 'new_m_conv_norm_b', 'new_m_pool_w', 'new_m_pool_scale', 'new_m_branch_g', 'new_m_w_out', 'new_m_ffn_pre_g', 'new_m_ffn_post_g', 'new_m_ffn_up', 'new_m_ffn_conv_w', 'new_m_ffn_conv_b', 'new_m_ffn_down', 'new_v_mod_w', 'new_v_mod_b', 'new_v_mix_pre_g', 'new_v_mix_post_g', 'new_v_w_in', 'new_v_sgu_norm_g', 'new_v_sgu_norm_b', 'new_v_sgu_w', 'new_v_sgu_b', 'new_v_conv_w', 'new_v_conv_b', 'new_v_conv_norm_g', 'new_v_conv_norm_b', 'new_v_pool_w', 'new_v_pool_scale', 'new_v_branch_g', 'new_v_w_out', 'new_v_ffn_pre_g', 'new_v_ffn_post_g', 'new_v_ffn_up', 'new_v_ffn_conv_w', 'new_v_ffn_conv_b', 'new_v_ffn_down']
TWIN_LEAF_KINDS = {'loss': 'loss', 'grad_x': 'grad_x', 'grad_mod_w': 'grad_w', 'grad_mod_b': 'grad_w', 'grad_mix_pre_g': 'grad_w', 'grad_mix_post_g': 'grad_w', 'grad_w_in': 'grad_w', 'grad_sgu_norm_g': 'grad_w', 'grad_sgu_norm_b': 'grad_w', 'grad_sgu_w': 'grad_w', 'grad_sgu_b': 'grad_w', 'grad_conv_w': 'grad_w', 'grad_conv_b': 'grad_w', 'grad_conv_norm_g': 'grad_w', 'grad_conv_norm_b': 'grad_w', 'grad_pool_w': 'grad_w', 'grad_pool_scale': 'grad_w', 'grad_branch_g': 'grad_w', 'grad_w_out': 'grad_w', 'grad_ffn_pre_g': 'grad_w', 'grad_ffn_post_g': 'grad_w', 'grad_ffn_up': 'grad_w', 'grad_ffn_conv_w': 'grad_w', 'grad_ffn_conv_b': 'grad_w', 'grad_ffn_down': 'grad_w', 'delta_mod_w': 'delta_w', 'delta_mod_b': 'delta_w', 'delta_mix_pre_g': 'delta_w', 'delta_mix_post_g': 'delta_w', 'delta_w_in': 'delta_w', 'delta_sgu_norm_g': 'delta_w', 'delta_sgu_norm_b': 'delta_w', 'delta_sgu_w': 'delta_w', 'delta_sgu_b': 'delta_w', 'delta_conv_w': 'delta_w', 'delta_conv_b': 'delta_w', 'delta_conv_norm_g': 'delta_w', 'delta_conv_norm_b': 'delta_w', 'delta_pool_w': 'delta_w', 'delta_pool_scale': 'delta_w', 'delta_branch_g': 'delta_w', 'delta_w_out': 'delta_w', 'delta_ffn_pre_g': 'delta_w', 'delta_ffn_post_g': 'delta_w', 'delta_ffn_up': 'delta_w', 'delta_ffn_conv_w': 'delta_w', 'delta_ffn_conv_b': 'delta_w', 'delta_ffn_down': 'delta_w', 'new_m_mod_w': 'new_m', 'new_m_mod_b': 'new_m', 'new_m_mix_pre_g': 'new_m', 'new_m_mix_post_g': 'new_m', 'new_m_w_in': 'new_m', 'new_m_sgu_norm_g': 'new_m', 'new_m_sgu_norm_b': 'new_m', 'new_m_sgu_w': 'new_m', 'new_m_sgu_b': 'new_m', 'new_m_conv_w': 'new_m', 'new_m_conv_b': 'new_m', 'new_m_conv_norm_g': 'new_m', 'new_m_conv_norm_b': 'new_m', 'new_m_pool_w': 'new_m', 'new_m_pool_scale': 'new_m', 'new_m_branch_g': 'new_m', 'new_m_w_out': 'new_m', 'new_m_ffn_pre_g': 'new_m', 'new_m_ffn_post_g': 'new_m', 'new_m_ffn_up': 'new_m', 'new_m_ffn_conv_w': 'new_m', 'new_m_ffn_conv_b': 'new_m', 'new_m_ffn_down': 'new_m', 'new_v_mod_w': 'new_v', 'new_v_mod_b': 'new_v', 'new_v_mix_pre_g': 'new_v', 'new_v_mix_post_g': 'new_v', 'new_v_w_in': 'new_v', 'new_v_sgu_norm_g': 'new_v', 'new_v_sgu_norm_b': 'new_v', 'new_v_sgu_w': 'new_v', 'new_v_sgu_b': 'new_v', 'new_v_conv_w': 'new_v', 'new_v_conv_b': 'new_v', 'new_v_conv_norm_g': 'new_v', 'new_v_conv_norm_b': 'new_v', 'new_v_pool_w': 'new_v', 'new_v_pool_scale': 'new_v', 'new_v_branch_g': 'new_v', 'new_v_w_out': 'new_v', 'new_v_ffn_pre_g': 'new_v', 'new_v_ffn_post_g': 'new_v', 'new_v_ffn_up': 'new_v', 'new_v_ffn_conv_w': 'new_v', 'new_v_ffn_conv_b': 'new_v', 'new_v_ffn_down': 'new_v'}


def _forward(args):
    return _fwd_reference(*[args[k] for k in FWD_PARAMS])


def _output_shape():
    out = _jax.eval_shape(lambda: _forward(_fwd_setup_inputs(0)))
    return out.shape, out.dtype

N_MICROBATCH = 1
ADAM_LR = 0.001
ADAM_B1 = 0.9
ADAM_B2 = 0.999
ADAM_EPS = 1e-08
ADAM_WD = 0.01
ADAM_STEP = 10
PER_EXAMPLE_BATCH_AXIS = {'x': 0, 'c': 0, 'loss_target': 0}
SHARED_INPUTS = []
_WEIGHT_DTYPES = {'mod_w': _jnp.float32, 'mod_b': _jnp.float32, 'mix_pre_g': _jnp.float32, 'mix_post_g': _jnp.float32, 'w_in': _jnp.float32, 'sgu_norm_g': _jnp.float32, 'sgu_norm_b': _jnp.float32, 'sgu_w': _jnp.float32, 'sgu_b': _jnp.float32, 'conv_w': _jnp.float32, 'conv_b': _jnp.float32, 'conv_norm_g': _jnp.float32, 'conv_norm_b': _jnp.float32, 'pool_w': _jnp.float32, 'pool_scale': _jnp.float32, 'branch_g': _jnp.float32, 'w_out': _jnp.float32, 'ffn_pre_g': _jnp.float32, 'ffn_post_g': _jnp.float32, 'ffn_up': _jnp.float32, 'ffn_conv_w': _jnp.float32, 'ffn_conv_b': _jnp.float32, 'ffn_down': _jnp.float32}
MOMENT_SCALE = {'mod_w': 2.318746e+00, 'mod_b': 4.295440e+00, 'mix_pre_g': 2.008964e-01, 'mix_post_g': 6.318420e+00, 'w_in': 2.650040e-01, 'sgu_norm_g': 8.426669e-02, 'sgu_norm_b': 8.053279e-02, 'sgu_w': 5.520223e-02, 'sgu_b': 8.202013e-02, 'conv_w': 4.383236e-01, 'conv_b': 2.069762e+00, 'conv_norm_g': 1.104591e+00, 'conv_norm_b': 1.423903e+00, 'pool_w': 2.181944e-01, 'pool_scale': 2.479624e-01, 'branch_g': 6.551154e-01, 'w_out': 5.991619e-01, 'ffn_pre_g': 2.099040e-01, 'ffn_post_g': 6.105813e+00, 'ffn_up': 1.369049e-01, 'ffn_conv_w': 1.598538e-01, 'ffn_conv_b': 3.538696e-01, 'ffn_down': 3.025907e-01}


def _to_microbatches(a, axis):
    t = _jnp.moveaxis(a, axis, 0)
    t = t.reshape((N_MICROBATCH, t.shape[0] // N_MICROBATCH) + t.shape[1:])
    return _jnp.moveaxis(t, 1, axis + 1)


def setup_inputs(seed: int = 0) -> dict:
    inp = _fwd_setup_inputs(seed)
    key = _jax.random.fold_in(_jax.random.key(seed), 7919)
    shape, _ = _output_shape()
    out = dict(inp)
    out["loss_target"] = _jax.random.normal(_jax.random.fold_in(key, 0), shape, _jnp.float32)
    for i, name in enumerate(TWIN_WEIGHTS):
        w = inp[name].astype(_jnp.float32)
        if MOMENT_SCALE is None:
            s = _jnp.sqrt(_jnp.mean(_jnp.square(w)) + 1e-30)
        else:
            s = MOMENT_SCALE[name]
        km, kv = _jax.random.split(_jax.random.fold_in(key, i + 1))
        out[name] = w
        out["m_" + name] = s * _jax.random.normal(km, w.shape, _jnp.float32)
        out["v_" + name] = (s * s) * _jax.random.uniform(kv, w.shape, _jnp.float32, 0.5, 1.5)
    if N_MICROBATCH > 1:
        for name, axis in PER_EXAMPLE_BATCH_AXIS.items():
            out[name] = _to_microbatches(out[name], axis)
    return {'x': out['x'], 'c': out['c'], 'mod_w': out['mod_w'], 'mod_b': out['mod_b'], 'mix_pre_g': out['mix_pre_g'], 'mix_post_g': out['mix_post_g'], 'w_in': out['w_in'], 'sgu_norm_g': out['sgu_norm_g'], 'sgu_norm_b': out['sgu_norm_b'], 'sgu_w': out['sgu_w'], 'sgu_b': out['sgu_b'], 'conv_w': out['conv_w'], 'conv_b': out['conv_b'], 'conv_norm_g': out['conv_norm_g'], 'conv_norm_b': out['conv_norm_b'], 'pool_w': out['pool_w'], 'pool_scale': out['pool_scale'], 'branch_g': out['branch_g'], 'w_out': out['w_out'], 'ffn_pre_g': out['ffn_pre_g'], 'ffn_post_g': out['ffn_post_g'], 'ffn_up': out['ffn_up'], 'ffn_conv_w': out['ffn_conv_w'], 'ffn_conv_b': out['ffn_conv_b'], 'ffn_down': out['ffn_down'], 'loss_target': out['loss_target'], 'm_mod_w': out['m_mod_w'], 'm_mod_b': out['m_mod_b'], 'm_mix_pre_g': out['m_mix_pre_g'], 'm_mix_post_g': out['m_mix_post_g'], 'm_w_in': out['m_w_in'], 'm_sgu_norm_g': out['m_sgu_norm_g'], 'm_sgu_norm_b': out['m_sgu_norm_b'], 'm_sgu_w': out['m_sgu_w'], 'm_sgu_b': out['m_sgu_b'], 'm_conv_w': out['m_conv_w'], 'm_conv_b': out['m_conv_b'], 'm_conv_norm_g': out['m_conv_norm_g'], 'm_conv_norm_b': out['m_conv_norm_b'], 'm_pool_w': out['m_pool_w'], 'm_pool_scale': out['m_pool_scale'], 'm_branch_g': out['m_branch_g'], 'm_w_out': out['m_w_out'], 'm_ffn_pre_g': out['m_ffn_pre_g'], 'm_ffn_post_g': out['m_ffn_post_g'], 'm_ffn_up': out['m_ffn_up'], 'm_ffn_conv_w': out['m_ffn_conv_w'], 'm_ffn_conv_b': out['m_ffn_conv_b'], 'm_ffn_down': out['m_ffn_down'], 'v_mod_w': out['v_mod_w'], 'v_mod_b': out['v_mod_b'], 'v_mix_pre_g': out['v_mix_pre_g'], 'v_mix_post_g': out['v_mix_post_g'], 'v_w_in': out['v_w_in'], 'v_sgu_norm_g': out['v_sgu_norm_g'], 'v_sgu_norm_b': out['v_sgu_norm_b'], 'v_sgu_w': out['v_sgu_w'], 'v_sgu_b': out['v_sgu_b'], 'v_conv_w': out['v_conv_w'], 'v_conv_b': out['v_conv_b'], 'v_conv_norm_g': out['v_conv_norm_g'], 'v_conv_norm_b': out['v_conv_norm_b'], 'v_pool_w': out['v_pool_w'], 'v_pool_scale': out['v_pool_scale'], 'v_branch_g': out['v_branch_g'], 'v_w_out': out['v_w_out'], 'v_ffn_pre_g': out['v_ffn_pre_g'], 'v_ffn_post_g': out['v_ffn_post_g'], 'v_ffn_up': out['v_ffn_up'], 'v_ffn_conv_w': out['v_ffn_conv_w'], 'v_ffn_conv_b': out['v_ffn_conv_b'], 'v_ffn_down': out['v_ffn_down']}


def _loss(weights, diff, rest, loss_target):
    with _jax.named_scope("forward"):
        args = {**rest, TWIN_DIFF_INPUT: diff, **{k: w.astype(_WEIGHT_DTYPES[k]) for k, w in weights.items()}}
        y = _forward(args)
    with _jax.named_scope("loss_head"):
        err = _jnp.square(y.astype(_jnp.float32) - loss_target)
        return 0.5 * _jnp.sum(_jnp.mean(err, axis=-1)) if err.ndim else 0.5 * err


def _adamw(w, g, m, v):
    m = ADAM_B1 * m + (1.0 - ADAM_B1) * g
    v = ADAM_B2 * v + (1.0 - ADAM_B2) * _jnp.square(g)
    m_hat = m / (1.0 - ADAM_B1 ** ADAM_STEP)
    v_hat = v / (1.0 - ADAM_B2 ** ADAM_STEP)
    delta = -ADAM_LR * (m_hat / (_jnp.sqrt(v_hat) + ADAM_EPS) + ADAM_WD * w)
    return delta, m, v


def reference(x, c, mod_w, mod_b, mix_pre_g, mix_post_g, w_in, sgu_norm_g, sgu_norm_b, sgu_w, sgu_b, conv_w, conv_b, conv_norm_g, conv_norm_b, pool_w, pool_scale, branch_g, w_out, ffn_pre_g, ffn_post_g, ffn_up, ffn_conv_w, ffn_conv_b, ffn_down, loss_target, m_mod_w, m_mod_b, m_mix_pre_g, m_mix_post_g, m_w_in, m_sgu_norm_g, m_sgu_norm_b, m_sgu_w, m_sgu_b, m_conv_w, m_conv_b, m_conv_norm_g, m_conv_norm_b, m_pool_w, m_pool_scale, m_branch_g, m_w_out, m_ffn_pre_g, m_ffn_post_g, m_ffn_up, m_ffn_conv_w, m_ffn_conv_b, m_ffn_down, v_mod_w, v_mod_b, v_mix_pre_g, v_mix_post_g, v_w_in, v_sgu_norm_g, v_sgu_norm_b, v_sgu_w, v_sgu_b, v_conv_w, v_conv_b, v_conv_norm_g, v_conv_norm_b, v_pool_w, v_pool_scale, v_branch_g, v_w_out, v_ffn_pre_g, v_ffn_post_g, v_ffn_up, v_ffn_conv_w, v_ffn_conv_b, v_ffn_down):
    given = dict(x=x, c=c, mod_w=mod_w, mod_b=mod_b, mix_pre_g=mix_pre_g, mix_post_g=mix_post_g, w_in=w_in, sgu_norm_g=sgu_norm_g, sgu_norm_b=sgu_norm_b, sgu_w=sgu_w, sgu_b=sgu_b, conv_w=conv_w, conv_b=conv_b, conv_norm_g=conv_norm_g, conv_norm_b=conv_norm_b, pool_w=pool_w, pool_scale=pool_scale, branch_g=branch_g, w_out=w_out, ffn_pre_g=ffn_pre_g, ffn_post_g=ffn_post_g, ffn_up=ffn_up, ffn_conv_w=ffn_conv_w, ffn_conv_b=ffn_conv_b, ffn_down=ffn_down, loss_target=loss_target, m_mod_w=m_mod_w, m_mod_b=m_mod_b, m_mix_pre_g=m_mix_pre_g, m_mix_post_g=m_mix_post_g, m_w_in=m_w_in, m_sgu_norm_g=m_sgu_norm_g, m_sgu_norm_b=m_sgu_norm_b, m_sgu_w=m_sgu_w, m_sgu_b=m_sgu_b, m_conv_w=m_conv_w, m_conv_b=m_conv_b, m_conv_norm_g=m_conv_norm_g, m_conv_norm_b=m_conv_norm_b, m_pool_w=m_pool_w, m_pool_scale=m_pool_scale, m_branch_g=m_branch_g, m_w_out=m_w_out, m_ffn_pre_g=m_ffn_pre_g, m_ffn_post_g=m_ffn_post_g, m_ffn_up=m_ffn_up, m_ffn_conv_w=m_ffn_conv_w, m_ffn_conv_b=m_ffn_conv_b, m_ffn_down=m_ffn_down, v_mod_w=v_mod_w, v_mod_b=v_mod_b, v_mix_pre_g=v_mix_pre_g, v_mix_post_g=v_mix_post_g, v_w_in=v_w_in, v_sgu_norm_g=v_sgu_norm_g, v_sgu_norm_b=v_sgu_norm_b, v_sgu_w=v_sgu_w, v_sgu_b=v_sgu_b, v_conv_w=v_conv_w, v_conv_b=v_conv_b, v_conv_norm_g=v_conv_norm_g, v_conv_norm_b=v_conv_norm_b, v_pool_w=v_pool_w, v_pool_scale=v_pool_scale, v_branch_g=v_branch_g, v_w_out=v_w_out, v_ffn_pre_g=v_ffn_pre_g, v_ffn_post_g=v_ffn_post_g, v_ffn_up=v_ffn_up, v_ffn_conv_w=v_ffn_conv_w, v_ffn_conv_b=v_ffn_conv_b, v_ffn_down=v_ffn_down)
    weights = {n: given[n] for n in TWIN_WEIGHTS}
    shared = {n: given[n] for n in SHARED_INPUTS}
    per_example = {n: given[n] for n in ['x', 'c']}
    grad_fn = _jax.value_and_grad(_loss, argnums=(0, 1))

    def one_microbatch(ex, loss_target):
        ex = dict(ex)
        diff = ex.pop(TWIN_DIFF_INPUT)
        return grad_fn(weights, diff, {**shared, **ex}, loss_target)

    if N_MICROBATCH == 1:
        loss, (grad_w, grad_x) = one_microbatch(per_example, given["loss_target"])
    else:
        def body(carry, xs):
            loss_sum, grad_sum = carry
            l_k, (gw_k, gx_k) = one_microbatch(xs[0], xs[1])
            with _jax.named_scope("update"):
                return (loss_sum + l_k, _jax.tree.map(_jnp.add, grad_sum, gw_k)), gx_k

        init = (_jnp.zeros((), _jnp.float32), _jax.tree.map(_jnp.zeros_like, weights))
        (loss, grad_w), grad_x = _jax.lax.scan(body, init, (per_example, given["loss_target"]))
    with _jax.named_scope("update"):
        delta_w, new_m, new_v = {}, {}, {}
        for n in TWIN_WEIGHTS:
            delta_w[n], new_m[n], new_v[n] = _adamw(weights[n], grad_w[n], given["m_" + n], given["v_" + n])
    return (loss, grad_x, *[grad_w[n] for n in TWIN_WEIGHTS], *[delta_w[n] for n in TWIN_WEIGHTS],
            *[new_m[n] for n in TWIN_WEIGHTS], *[new_v[n] for n in TWIN_WEIGHTS])
```

```python
import functools
import math

import jax
import jax.numpy as jnp
from jax import lax
from jax.experimental import pallas as pl
from jax.experimental.pallas import tpu as pltpu

F32 = jnp.float32
BF16 = jnp.bfloat16
MESH = pl.DeviceIdType.MESH

EPS = 1e-6
HEAD_DIM = 64
CHUNK = 128
SGU_WIDTH = 384
CONV_WIDTH = 384
POOL_WIDTH = 256
POOL_WINDOWS = (2, 4, 8, 16)
CONV_K = 31
FFN_CONV_K = 3
N_MOD = 6
N_CHIPS = 4
N_DEV = 8

ADAM_LR = 0.001
ADAM_B1 = 0.9
ADAM_B2 = 0.999
ADAM_EPS = 1e-08
ADAM_WD = 0.01
ADAM_STEP = 10

MIX_HALO = 32
FFN_HALO = 8
CONV_ROWS = 32
SMALL_COLS = 1024
VMEM_BYTES_V7X = 64 * 1024 * 1024


def _vmem_limit(estimate_bytes):
    return int(min(max(estimate_bytes, 16 * 1024 * 1024), VMEM_BYTES_V7X - 8 * 1024 * 1024))


def _row_tile(s, want):
    return want if s % want == 0 else math.gcd(s, want)


def _rsum(v):
    return jnp.sum(v, axis=0, keepdims=True)


def _rmean(v):
    return jnp.mean(v, axis=-1, keepdims=True)


def _gelu(v):
    k = math.sqrt(2.0 / math.pi)
    return 0.5 * v * (1.0 + jnp.tanh(k * (v + 0.044715 * v * v * v)))


def _gelu_grad(v):
    k = math.sqrt(2.0 / math.pi)
    t = jnp.tanh(k * (v + 0.044715 * v * v * v))
    return 0.5 * (1.0 + t) + 0.5 * v * (1.0 - t * t) * (k * (1.0 + 3.0 * 0.044715 * v * v))


def _dot(a, b):
    return jnp.dot(a, b, preferred_element_type=F32)


def _dot_nt(a, b):
    return lax.dot_general(a, b, (((1,), (1,)), ((), ())), preferred_element_type=F32)


def _dot_tn(a, b):
    return lax.dot_general(a, b, (((0,), (0,)), ((), ())), preferred_element_type=F32)


def _group_mean(v, bd):
    hi = v.astype(BF16)
    lo = (v - hi.astype(F32)).astype(BF16)
    return (_dot(hi, bd) + _dot(lo, bd)) * (1.0 / HEAD_DIM)


def _const_spec(shape):
    nd = len(shape)
    return pl.BlockSpec(shape, lambda *_: (0,) * nd)


def _norm_mod_matmul(x, gain, s1p, shift, w, name):
    s, d = x.shape
    nb, _, tn = w.shape
    ts = _row_tile(s, 512)

    def body(x_ref, g_ref, s_ref, b_ref, w_ref, z_ref, h_ref):
        @pl.when(pl.program_id(1) == 0)
        def _():
            xv = x_ref[...]
            r = lax.rsqrt(_rmean(xv * xv) + EPS)
            h_ref[...] = ((xv * r) * g_ref[...] * s_ref[...] + b_ref[...]).astype(BF16)

        z_ref[...] = _dot(h_ref[...], w_ref[...])

    vec = pl.BlockSpec((1, d), lambda i, j: (0, 0))
    return pl.pallas_call(
        body, name=name,
        grid=(s // ts, nb),
        in_specs=[pl.BlockSpec((ts, d), lambda i, j: (i, 0)), vec, vec, vec,
                  pl.BlockSpec((None, d, tn), lambda i, j: (j, 0, 0))],
        out_specs=[pl.BlockSpec((ts, tn), lambda i, j: (i, j)), pl.BlockSpec((ts, d), lambda i, j: (i, 0))],
        out_shape=[jax.ShapeDtypeStruct((s, nb * tn), F32), jax.ShapeDtypeStruct((s, d), BF16)],
        compiler_params=pltpu.CompilerParams(
            dimension_semantics=("arbitrary", "arbitrary"),
            vmem_limit_bytes=_vmem_limit(2 * (ts * d * 4 + d * tn * 2 + ts * tn * 4 + ts * d * 2) + 4 * ts * d * 4)),
    )(x, gain, s1p, shift, w)


def _matmul_norm_resid(a, w, xres, gate, gpost, name):
    s, k = a.shape
    d = w.shape[1]
    ts = _row_tile(s, 512)

    def body(a_ref, w_ref, x_ref, gate_ref, gp_ref, o_ref, xn_ref):
        o = _dot(a_ref[...], w_ref[...])
        o_ref[...] = o
        r = lax.rsqrt(_rmean(o * o) + EPS)
        xn_ref[...] = x_ref[...] + gate_ref[...] * ((o * r) * gp_ref[...])

    vec = pl.BlockSpec((1, d), lambda i: (0, 0))
    row = pl.BlockSpec((ts, d), lambda i: (i, 0))
    return pl.pallas_call(
        body, name=name,
        grid=(s // ts,),
        in_specs=[pl.BlockSpec((ts, k), lambda i: (i, 0)),
                  pl.BlockSpec((k, d), lambda i: (0, 0), pipeline_mode=pl.Buffered(1)), row, vec, vec],
        out_specs=[row, row],
        out_shape=[jax.ShapeDtypeStruct((s, d), F32)] * 2,
        compiler_params=pltpu.CompilerParams(
            dimension_semantics=("arbitrary",),
            vmem_limit_bytes=_vmem_limit(2 * (ts * k * 2 + 3 * ts * d * 4) + k * d * 2 + 4 * ts * d * 4)),
    )(a, w, xres, gate, gpost)


def _wgrad(a, b, acols, bcols, out_struct, out_index, out_block, prev, name):
    s = a.shape[0]
    ts = _row_tile(s, 512)
    aw, afn = acols
    bw, bfn = bcols
    nj = out_index[0]
    oidx = out_index[1]

    def body(*refs):
        a_ref, b_ref = refs[0], refs[1]
        o_ref, acc = refs[-2], refs[-1]
        i = pl.program_id(1)

        @pl.when(i == 0)
        def _():
            acc[...] = jnp.zeros_like(acc)

        acc[...] += _dot_tn(a_ref[...], b_ref[...])

        @pl.when(i == pl.num_programs(1) - 1)
        def _():
            o_ref[...] = acc[...].astype(o_ref.dtype)

    in_specs = [pl.BlockSpec((ts, aw), lambda j, i: (i, afn(j))), pl.BlockSpec((ts, bw), lambda j, i: (i, bfn(j)))]
    args = [a, b]
    aliases = {}
    if prev is not None:
        in_specs.append(pl.BlockSpec(memory_space=pl.ANY))
        args.append(prev)
        aliases = {2: 0}
    return pl.pallas_call(
        body, name=name,
        grid=(nj, s // ts),
        in_specs=in_specs,
        out_specs=pl.BlockSpec(out_block, lambda j, i: oidx(j)),
        out_shape=out_struct,
        scratch_shapes=[pltpu.VMEM((aw, bw), F32)],
        input_output_aliases=aliases,
        compiler_params=pltpu.CompilerParams(
            dimension_semantics=("arbitrary", "arbitrary"),
            vmem_limit_bytes=_vmem_limit(2 * (ts * aw * 2 + ts * bw * 2) + 3 * aw * bw * 4 + ts * aw * 4)),
    )(*args)


def _loss_head(xo, tgt):
    s, d = xo.shape
    ts = _row_tile(s, 512)

    def body(x_ref, t_ref, dx_ref, l_ref, acc):
        i = pl.program_id(0)

        @pl.when(i == 0)
        def _():
            acc[...] = jnp.zeros_like(acc)

        e = x_ref[...] - t_ref[...]
        dx_ref[...] = e * (1.0 / d)
        acc[...] += _rsum(e * e)

        @pl.when(i == pl.num_programs(0) - 1)
        def _():
            tot = jnp.sum(acc[...], axis=-1, keepdims=True) * (0.5 / d)
            l_ref[...] = jnp.broadcast_to(tot, l_ref.shape)

    row = pl.BlockSpec((ts, d), lambda i: (i, 0))
    return pl.pallas_call(
        body, name="loss_head",
        grid=(s // ts,),
        in_specs=[row, row],
        out_specs=[row, pl.BlockSpec((1, SMALL_COLS), lambda i: (0, 0))],
        out_shape=[jax.ShapeDtypeStruct((s, d), F32), jax.ShapeDtypeStruct((1, SMALL_COLS), F32)],
        scratch_shapes=[pltpu.VMEM((1, d), F32)],
        compiler_params=pltpu.CompilerParams(dimension_semantics=("arbitrary",)),
    )(xo, tgt)


def _resid_bwd_matmul(dxn, o, gate, gpost, wt, name):
    s, d = dxn.shape
    k = wt.shape[1]
    ts = _row_tile(s, 512)

    def body(dx_ref, o_ref, gate_ref, gp_ref, wt_ref, do_ref, da_ref, dgate_ref, dgp_ref):
        i = pl.program_id(0)

        @pl.when(i == 0)
        def _():
            dgate_ref[...] = jnp.zeros_like(dgate_ref)
            dgp_ref[...] = jnp.zeros_like(dgp_ref)

        dx = dx_ref[...]
        o = o_ref[...]
        r = lax.rsqrt(_rmean(o * o) + EPS)
        on = o * r
        dgate_ref[...] += _rsum(dx * (on * gp_ref[...]))
        don = dx * gate_ref[...]
        dgp_ref[...] += _rsum(don * on)
        t = don * gp_ref[...]
        do = (r * (t - on * _rmean(t * on))).astype(BF16)
        do_ref[...] = do
        da_ref[...] = _dot(do, wt_ref[...])

    vec = pl.BlockSpec((1, d), lambda i: (0, 0))
    row = pl.BlockSpec((ts, d), lambda i: (i, 0))
    return pl.pallas_call(
        body, name=name,
        grid=(s // ts,),
        in_specs=[row, row, vec, vec, pl.BlockSpec((d, k), lambda i: (0, 0), pipeline_mode=pl.Buffered(1))],
        out_specs=[row, pl.BlockSpec((ts, k), lambda i: (i, 0)), vec, vec],
        out_shape=[jax.ShapeDtypeStruct((s, d), BF16), jax.ShapeDtypeStruct((s, k), F32),
                   jax.ShapeDtypeStruct((1, d), F32), jax.ShapeDtypeStruct((1, d), F32)],
        compiler_params=pltpu.CompilerParams(
            dimension_semantics=("arbitrary",),
            vmem_limit_bytes=_vmem_limit(2 * (2 * ts * d * 4 + ts * d * 2 + ts * k * 4) + d * k * 2 + 6 * ts * d * 4)),
    )(dxn, o, gate, gpost, wt)


def _norm_mod_bwd(dh, xv, gain, s1p, dres):
    r = lax.rsqrt(_rmean(xv * xv) + EPS)
    xn = xv * r
    dshift = _rsum(dh)
    t = dh * xn
    dscale = _rsum(t * gain)
    dgain = _rsum(t * s1p)
    dxn = dh * (gain * s1p)
    dx = r * (dxn - xn * _rmean(dxn * xn)) + dres
    return dx, dshift, dscale, dgain


def _lane_lt(shape, bound):
    return lax.broadcasted_iota(jnp.int32, shape, 1) < bound


def _sgu_forward(z_ref, bd_ref, ng_ref, nb_ref, wm_ref, bias_ref, ts, ya_s, f_s):
    u = _gelu(z_ref[:, 0:SGU_WIDTH])
    v = _gelu(z_ref[:, SGU_WIDTH:2 * SGU_WIDTH])
    bd = bd_ref[...]
    vc = v - _group_mean(v, bd)
    rstd = lax.rsqrt(_group_mean(vc * vc, bd) + EPS)
    vhat = vc * rstd
    vn = (vhat * ng_ref[...] + nb_ref[...]).astype(BF16)
    left = _lane_lt((CHUNK, CHUNK), HEAD_DIM)
    for n in range(ts // CHUNK):
        rows = slice(n * CHUNK, (n + 1) * CHUNK)
        for p in range(SGU_WIDTH // CHUNK):
            cols = slice(p * CHUNK, (p + 1) * CHUNK)
            blk = vn[rows, cols]
            f = jnp.where(left, _dot(wm_ref[2 * p], blk), _dot(wm_ref[2 * p + 1], blk)) + bias_ref[:, cols]
            if f_s is not None:
                f_s[rows, cols] = f
            ya_s[rows, cols] = u[rows, cols] * f
    return u, vhat, rstd, vn


def _conv31_forward(z_ref, zh_ref, first, cw_ref, cb_ref, ts, ext_b, cbs):
    a = z_ref[:, 2 * SGU_WIDTH:2 * SGU_WIDTH + CONV_WIDTH]
    g = z_ref[:, 2 * SGU_WIDTH + CONV_WIDTH:2 * SGU_WIDTH + 2 * CONV_WIDTH]
    ah = zh_ref[:, 2 * SGU_WIDTH:2 * SGU_WIDTH + CONV_WIDTH]
    gh = zh_ref[:, 2 * SGU_WIDTH + CONV_WIDTH:2 * SGU_WIDTH + 2 * CONV_WIDTH]
    ext_b[pl.ds(0, MIX_HALO), :] = jnp.where(first, 0.0, ah * jax.nn.sigmoid(gh))
    ext_b[pl.ds(MIX_HALO, ts), :] = a * jax.nn.sigmoid(g)
    for r in range(ts // CONV_ROWS):
        acc = jnp.broadcast_to(cb_ref[...], (CONV_ROWS, CONV_WIDTH))
        for k in range(CONV_K):
            acc = acc + cw_ref[k:k + 1, :] * ext_b[pl.ds(MIX_HALO - (CONV_K - 1) + k + r * CONV_ROWS, CONV_ROWS), :]
        cbs[pl.ds(r * CONV_ROWS, CONV_ROWS), :] = acc


def _pool_counts(i, ts):
    pos1 = (i * ts + 1 + lax.broadcasted_iota(jnp.int32, (ts, POOL_WIDTH), 0)).astype(F32)
    lane = lax.broadcasted_iota(jnp.int32, (ts, POOL_WIDTH), 1)
    gdim = POOL_WIDTH // len(POOL_WINDOWS)
    win = jnp.where(lane < gdim, float(POOL_WINDOWS[0]),
                    jnp.where(lane < 2 * gdim, float(POOL_WINDOWS[1]),
                              jnp.where(lane < 3 * gdim, float(POOL_WINDOWS[2]), float(POOL_WINDOWS[3]))))
    return jnp.minimum(pos1, win)


def _window_sums(ext, base, ts, sign):
    lane = lax.broadcasted_iota(jnp.int32, (ts, POOL_WIDTH), 1)
    gdim = POOL_WIDTH // len(POOL_WINDOWS)
    run = jnp.zeros((ts, POOL_WIDTH), F32)
    out = jnp.zeros((ts, POOL_WIDTH), F32)
    for m in range(POOL_WINDOWS[-1]):
        run = run + ext[pl.ds(base + sign * m, ts), :]
        for gi, win in enumerate(POOL_WINDOWS):
            if m == win - 1:
                out = jnp.where((lane >= gi * gdim) & (lane < (gi + 1) * gdim), run, out)
    return out


def _pool_forward(z_ref, zh_ref, first, i, ts, ext_c):
    c0 = 2 * SGU_WIDTH + 2 * CONV_WIDTH
    zc = z_ref[:, c0:c0 + POOL_WIDTH]
    ext_c[pl.ds(0, MIX_HALO), :] = jnp.where(first, 0.0, zh_ref[:, c0:c0 + POOL_WIDTH])
    ext_c[pl.ds(MIX_HALO, ts), :] = zc
    sums = _window_sums(ext_c, MIX_HALO, ts, -1)
    return sums / _pool_counts(i, ts) - zc


def _layer_norm_rows(v):
    mu = _rmean(v)
    vc = v - mu
    rstd = lax.rsqrt(_rmean(vc * vc) + EPS)
    return vc * rstd, rstd


def _mixer_specs(s, ts, width):
    nbh = ts // MIX_HALO
    tile = pl.BlockSpec((ts, width), lambda i: (i, 0))
    prev = pl.BlockSpec((MIX_HALO, width), lambda i: (jnp.maximum(i * nbh - 1, 0), 0))
    nxt = pl.BlockSpec((MIX_HALO, width), lambda i: (jnp.minimum((i + 1) * nbh, s // MIX_HALO - 1), 0))
    return tile, prev, nxt


def _mixer_fwd(z, mp, name):
    s, inw = z.shape
    d = SGU_WIDTH + CONV_WIDTH + POOL_WIDTH
    ts = _row_tile(s, 256)

    def body(z_ref, zh_ref, bd_ref, ng_ref, nb_ref, wm_ref, bias_ref, cw_ref, cb_ref, cng_ref, cnb_ref,
             pw_ref, ps_ref, bg_ref, y_ref, ya_s, ext_b, cbs, ext_c):
        i = pl.program_id(0)
        first = i == 0
        _sgu_forward(z_ref, bd_ref, ng_ref, nb_ref, wm_ref, bias_ref, ts, ya_s, None)
        ya = ya_s[...]
        ra = lax.rsqrt(_rmean(ya * ya) + EPS)
        y_ref[:, 0:SGU_WIDTH] = ((ya * ra) * bg_ref[:, 0:SGU_WIDTH]).astype(BF16)

        _conv31_forward(z_ref, zh_ref, first, cw_ref, cb_ref, ts, ext_b, cbs)
        chat, _ = _layer_norm_rows(cbs[...])
        lin = chat * cng_ref[...] + cnb_ref[...]
        yb = lin * jax.nn.sigmoid(lin)
        rb = lax.rsqrt(_rmean(yb * yb) + EPS)
        y_ref[:, SGU_WIDTH:SGU_WIDTH + CONV_WIDTH] = ((yb * rb) * bg_ref[:, SGU_WIDTH:SGU_WIDTH + CONV_WIDTH]).astype(BF16)

        dpool = _pool_forward(z_ref, zh_ref, first, i, ts, ext_c)
        yc = _dot(dpool.astype(BF16), pw_ref[...]) * ps_ref[...]
        rc = lax.rsqrt(_rmean(yc * yc) + EPS)
        y_ref[:, SGU_WIDTH + CONV_WIDTH:d] = ((yc * rc) * bg_ref[:, SGU_WIDTH + CONV_WIDTH:d]).astype(BF16)

    tile, prev, _ = _mixer_specs(s, ts, inw)
    consts = [mp["bd"], mp["ng"], mp["nb"], mp["wm"], mp["bias"], mp["cw"], mp["cb"], mp["cng"], mp["cnb"],
              mp["pw"], mp["ps"], mp["bg"]]
    return pl.pallas_call(
        body, name=name,
        grid=(s // ts,),
        in_specs=[tile, prev] + [_const_spec(c.shape) for c in consts],
        out_specs=pl.BlockSpec((ts, d), lambda i: (i, 0)),
        out_shape=jax.ShapeDtypeStruct((s, d), BF16),
        scratch_shapes=[pltpu.VMEM((ts, SGU_WIDTH), F32), pltpu.VMEM((ts + MIX_HALO, CONV_WIDTH), F32),
                        pltpu.VMEM((ts, CONV_WIDTH), F32), pltpu.VMEM((ts + MIX_HALO, POOL_WIDTH), F32)],
        compiler_params=pltpu.CompilerParams(dimension_semantics=("arbitrary",),
                                             vmem_limit_bytes=_vmem_limit(16 * ts * inw * 4)),
    )(z, z, *consts)


def _mixer_bwd_a(z, dy, mp, name):
    s, inw = z.shape
    d = SGU_WIDTH + CONV_WIDTH + POOL_WIDTH
    ts = _row_tile(s, 256)
    nchunk = ts // CHUNK

    def rms_bwd(dyn, y, g):
        r = lax.rsqrt(_rmean(y * y) + EPS)
        yn = y * r
        dg = _rsum(dyn * yn)
        t = dyn * g
        return r * (t - yn * _rmean(t * yn)), dg

    def body(z_ref, zh_ref, dy_ref, bd_ref, ng_ref, nb_ref, wm_ref, wmt_ref, bias_ref, cw_ref, cb_ref, cng_ref, cnb_ref,
             pw_ref, pwt_ref, ps_ref, bg_ref,
             dza_ref, dcb_ref, dd_ref, dbg_ref, dwm_ref, dbias_ref, dng_ref, dnb_ref, dcng_ref, dcnb_ref, dps_ref, dpw_ref,
             ya_s, f_s, dvn_s, ext_b, cbs, ext_c):
        i = pl.program_id(0)
        first = i == 0

        @pl.when(first)
        def _():
            for ref in (dwm_ref, dbias_ref, dng_ref, dnb_ref, dcng_ref, dcnb_ref, dps_ref, dpw_ref):
                ref[...] = jnp.zeros_like(ref)

        u, vhat, rstd, vn = _sgu_forward(z_ref, bd_ref, ng_ref, nb_ref, wm_ref, bias_ref, ts, ya_s, f_s)
        dya, dbg_a = rms_bwd(dy_ref[:, 0:SGU_WIDTH], ya_s[...], bg_ref[:, 0:SGU_WIDTH])
        du = dya * f_s[...]
        df = dya * u
        dfb = df.astype(BF16)
        left = _lane_lt((CHUNK, CHUNK), HEAD_DIM)
        zero = jnp.zeros((CHUNK, CHUNK), BF16)
        dbias = jnp.zeros((CHUNK, SGU_WIDTH), F32)
        for n in range(nchunk):
            rows = slice(n * CHUNK, (n + 1) * CHUNK)
            dbias = dbias + df[rows, :]
            for p in range(SGU_WIDTH // CHUNK):
                cols = slice(p * CHUNK, (p + 1) * CHUNK)
                dblk = dfb[rows, cols]
                vblk = vn[rows, cols]
                dwm_ref[2 * p] += _dot_nt(jnp.where(left, dblk, zero), vblk)
                dwm_ref[2 * p + 1] += _dot_nt(jnp.where(left, zero, dblk), vblk)
                dvn_s[rows, cols] = jnp.where(left, _dot(wmt_ref[2 * p], dblk), _dot(wmt_ref[2 * p + 1], dblk))
        dbias_ref[...] += dbias
        dvn = dvn_s[...]
        dng_ref[...] += _rsum(dvn * vhat)
        dnb_ref[...] += _rsum(dvn)
        dvh = dvn * ng_ref[...]
        bd = bd_ref[...]
        dv = rstd * (dvh - _group_mean(dvh, bd) - vhat * _group_mean(dvh * vhat, bd))
        dza_ref[:, 0:SGU_WIDTH] = (du * _gelu_grad(z_ref[:, 0:SGU_WIDTH])).astype(BF16)
        dza_ref[:, SGU_WIDTH:2 * SGU_WIDTH] = (dv * _gelu_grad(z_ref[:, SGU_WIDTH:2 * SGU_WIDTH])).astype(BF16)

        _conv31_forward(z_ref, zh_ref, first, cw_ref, cb_ref, ts, ext_b, cbs)
        chat, crstd = _layer_norm_rows(cbs[...])
        lin = chat * cng_ref[...] + cnb_ref[...]
        sl = jax.nn.sigmoid(lin)
        dyb, dbg_b = rms_bwd(dy_ref[:, SGU_WIDTH:SGU_WIDTH + CONV_WIDTH], lin * sl, bg_ref[:, SGU_WIDTH:SGU_WIDTH + CONV_WIDTH])
        dlin = dyb * (sl * (1.0 + lin * (1.0 - sl)))
        dcng_ref[...] += _rsum(dlin * chat)
        dcnb_ref[...] += _rsum(dlin)
        dch = dlin * cng_ref[...]
        dcb_ref[...] = crstd * (dch - _rmean(dch) - chat * _rmean(dch * chat))

        dpool = _pool_forward(z_ref, zh_ref, first, i, ts, ext_c)
        dpb = dpool.astype(BF16)
        ycp = _dot(dpb, pw_ref[...])
        dyc, dbg_c = rms_bwd(dy_ref[:, SGU_WIDTH + CONV_WIDTH:d], ycp * ps_ref[...], bg_ref[:, SGU_WIDTH + CONV_WIDTH:d])
        dps_ref[...] += _rsum(dyc * ycp)
        dycp = (dyc * ps_ref[...]).astype(BF16)
        dpw_ref[...] += _dot_tn(dpb, dycp)
        dd_ref[...] = _dot(dycp, pwt_ref[...])

        @pl.when(first)
        def _():
            dbg_ref[...] = jnp.zeros_like(dbg_ref)

        dbg_ref[:, 0:SGU_WIDTH] += dbg_a
        dbg_ref[:, SGU_WIDTH:SGU_WIDTH + CONV_WIDTH] += dbg_b
        dbg_ref[:, SGU_WIDTH + CONV_WIDTH:d] += dbg_c

    tile, prev, _ = _mixer_specs(s, ts, inw)
    consts = [mp["bd"], mp["ng"], mp["nb"], mp["wm"], mp["wmt"], mp["bias"], mp["cw"], mp["cb"], mp["cng"], mp["cnb"],
              mp["pw"], mp["pwt"], mp["ps"], mp["bg"]]
    acc_shapes = [(1, d), (2 * (SGU_WIDTH // CHUNK), CHUNK, CHUNK), (CHUNK, SGU_WIDTH), (1, SGU_WIDTH), (1, SGU_WIDTH),
                  (1, CONV_WIDTH), (1, CONV_WIDTH), (1, POOL_WIDTH), (POOL_WIDTH, POOL_WIDTH)]
    return pl.pallas_call(
        body, name=name,
        grid=(s // ts,),
        in_specs=[tile, prev, pl.BlockSpec((ts, d), lambda i: (i, 0))] + [_const_spec(c.shape) for c in consts],
        out_specs=[pl.BlockSpec((ts, 2 * SGU_WIDTH), lambda i: (i, 0)), pl.BlockSpec((ts, CONV_WIDTH), lambda i: (i, 0)),
                   pl.BlockSpec((ts, POOL_WIDTH), lambda i: (i, 0))] + [_const_spec(a) for a in acc_shapes],
        out_shape=[jax.ShapeDtypeStruct((s, 2 * SGU_WIDTH), BF16), jax.ShapeDtypeStruct((s, CONV_WIDTH), F32),
                   jax.ShapeDtypeStruct((s, POOL_WIDTH), F32)] + [jax.ShapeDtypeStruct(a, F32) for a in acc_shapes],
        scratch_shapes=[pltpu.VMEM((ts, SGU_WIDTH), F32), pltpu.VMEM((ts, SGU_WIDTH), F32), pltpu.VMEM((ts, SGU_WIDTH), F32),
                        pltpu.VMEM((ts + MIX_HALO, CONV_WIDTH), F32), pltpu.VMEM((ts, CONV_WIDTH), F32),
                        pltpu.VMEM((ts + MIX_HALO, POOL_WIDTH), F32)],
        compiler_params=pltpu.CompilerParams(dimension_semantics=("arbitrary",),
                                             vmem_limit_bytes=_vmem_limit(24 * ts * inw * 4)),
    )(z, z, dy, *consts)


def _mixer_bwd_b(z, dza, dcb, dd, x, dres, gain, s1p, cw, wt, name):
    s, inw = z.shape
    d = x.shape[1]
    ts = _row_tile(s, 256)
    c0 = 2 * SGU_WIDTH
    c1 = c0 + 2 * CONV_WIDTH

    def body(z_ref, zh_ref, dza_ref, dcb_ref, dcbn_ref, dd_ref, ddn_ref, x_ref, dres_ref, g_ref, s_ref, cw_ref, wt_ref,
             dx_ref, dz_ref, dsh_ref, dsc_ref, dg_ref, dcw_ref, dcbias_ref, ext_b, ext_n, ext_e):
        i = pl.program_id(0)
        first = i == 0
        last = i == pl.num_programs(0) - 1

        @pl.when(first)
        def _():
            for ref in (dsh_ref, dsc_ref, dg_ref, dcw_ref, dcbias_ref):
                ref[...] = jnp.zeros_like(ref)

        a = z_ref[:, c0:c0 + CONV_WIDTH]
        sg = jax.nn.sigmoid(z_ref[:, c0 + CONV_WIDTH:c1])
        ah = zh_ref[:, c0:c0 + CONV_WIDTH]
        gh = zh_ref[:, c0 + CONV_WIDTH:c1]
        ext_b[pl.ds(0, MIX_HALO), :] = jnp.where(first, 0.0, ah * jax.nn.sigmoid(gh))
        ext_b[pl.ds(MIX_HALO, ts), :] = a * sg
        dcbv = dcb_ref[...]
        dcbias_ref[...] += _rsum(dcbv)
        for k in range(CONV_K):
            dcw_ref[k:k + 1, :] += _rsum(dcbv * ext_b[pl.ds(MIX_HALO - (CONV_K - 1) + k, ts), :])

        ext_n[pl.ds(0, ts), :] = dcbv
        ext_n[pl.ds(ts, MIX_HALO), :] = jnp.where(last, 0.0, dcbn_ref[...])
        for r in range(ts // CONV_ROWS):
            acc = jnp.zeros((CONV_ROWS, CONV_WIDTH), F32)
            for k in range(CONV_K):
                acc = acc + cw_ref[k:k + 1, :] * ext_n[pl.ds(CONV_K - 1 - k + r * CONV_ROWS, CONV_ROWS), :]
            rows = pl.ds(r * CONV_ROWS, CONV_ROWS)
            ar = z_ref[rows, c0:c0 + CONV_WIDTH]
            sr = jax.nn.sigmoid(z_ref[rows, c0 + CONV_WIDTH:c1])
            dz_ref[rows, c0:c0 + CONV_WIDTH] = (acc * sr).astype(BF16)
            dz_ref[rows, c0 + CONV_WIDTH:c1] = (acc * ar * sr * (1.0 - sr)).astype(BF16)

        ddv = dd_ref[...]
        ext_e[pl.ds(0, ts), :] = ddv / _pool_counts(i, ts)
        nh = (i + 1) * ts + lax.broadcasted_iota(jnp.int32, (MIX_HALO, POOL_WIDTH), 0)
        lane = lax.broadcasted_iota(jnp.int32, (MIX_HALO, POOL_WIDTH), 1)
        gdim = POOL_WIDTH // len(POOL_WINDOWS)
        winh = jnp.where(lane < gdim, float(POOL_WINDOWS[0]),
                         jnp.where(lane < 2 * gdim, float(POOL_WINDOWS[1]),
                                   jnp.where(lane < 3 * gdim, float(POOL_WINDOWS[2]), float(POOL_WINDOWS[3]))))
        cnth = jnp.minimum((nh + 1).astype(F32), winh)
        ext_e[pl.ds(ts, MIX_HALO), :] = jnp.where(last, 0.0, ddn_ref[...] / cnth)
        dz_ref[:, c1:inw] = (_window_sums(ext_e, 0, ts, 1) - ddv).astype(BF16)
        dz_ref[:, 0:c0] = dza_ref[...]

        dh = _dot(dz_ref[...], wt_ref[...])
        dx, dsh, dsc, dg = _norm_mod_bwd(dh, x_ref[...], g_ref[...], s_ref[...], dres_ref[...])
        dx_ref[...] = dx
        dsh_ref[...] += dsh
        dsc_ref[...] += dsc
        dg_ref[...] += dg

    tile, prev, _ = _mixer_specs(s, ts, inw)
    _, _, nxt_b = _mixer_specs(s, ts, CONV_WIDTH)
    _, _, nxt_c = _mixer_specs(s, ts, POOL_WIDTH)
    row = pl.BlockSpec((ts, d), lambda i: (i, 0))
    vec = pl.BlockSpec((1, d), lambda i: (0, 0))
    return pl.pallas_call(
        body, name=name,
        grid=(s // ts,),
        in_specs=[tile, prev, pl.BlockSpec((ts, c0), lambda i: (i, 0)),
                  pl.BlockSpec((ts, CONV_WIDTH), lambda i: (i, 0)), nxt_b,
                  pl.BlockSpec((ts, POOL_WIDTH), lambda i: (i, 0)), nxt_c,
                  row, row, vec, vec, _const_spec(cw.shape),
                  pl.BlockSpec(wt.shape, lambda i: (0, 0), pipeline_mode=pl.Buffered(1))],
        out_specs=[row, pl.BlockSpec((ts, inw), lambda i: (i, 0)), vec, vec, vec,
                   _const_spec((CONV_K, CONV_WIDTH)), _const_spec((1, CONV_WIDTH))],
        out_shape=[jax.ShapeDtypeStruct((s, d), F32), jax.ShapeDtypeStruct((s, inw), BF16)]
        + [jax.ShapeDtypeStruct((1, d), F32)] * 3
        + [jax.ShapeDtypeStruct((CONV_K, CONV_WIDTH), F32), jax.ShapeDtypeStruct((1, CONV_WIDTH), F32)],
        scratch_shapes=[pltpu.VMEM((ts + MIX_HALO, CONV_WIDTH), F32), pltpu.VMEM((ts + MIX_HALO, CONV_WIDTH), F32),
                        pltpu.VMEM((ts + MIX_HALO, POOL_WIDTH), F32)],
        compiler_params=pltpu.CompilerParams(dimension_semantics=("arbitrary",),
                                             vmem_limit_bytes=_vmem_limit(16 * ts * inw * 4 + inw * d * 2)),
    )(z, z, dza, dcb, dcb, dd, dd, x, dres, gain, s1p, cw, wt)


def _ffn_specs(s, ts, tc, half_blocks):
    nbh = ts // FFN_HALO

    def tile(off):
        return pl.BlockSpec((ts, tc), lambda j, i: (i, j + off))

    def prev(off):
        return pl.BlockSpec((FFN_HALO, tc), lambda j, i: (jnp.maximum(i * nbh - 1, 0), j + off))

    def vec(rows, off):
        return pl.BlockSpec((rows, tc), lambda j, i: (0, j + off))

    return tile, prev, vec


def _conv3_tile(p_ref, ph_ref, first, w_ref, b_ref, ts, ext):
    ext[pl.ds(0, FFN_HALO), :] = jnp.where(first, 0.0, ph_ref[...])
    ext[pl.ds(FFN_HALO, ts), :] = p_ref[...]
    acc = b_ref[...] + w_ref[FFN_CONV_K - 1:FFN_CONV_K, :] * p_ref[...]
    for k in range(FFN_CONV_K - 1):
        acc = acc + w_ref[k:k + 1, :] * ext[pl.ds(FFN_HALO - (FFN_CONV_K - 1) + k, ts), :]
    return acc


def _ffn_act_fwd(p, cw, cb, name):
    s, f2 = p.shape
    f = f2 // 2
    tc = f // 2
    hb = f // tc
    ts = _row_tile(s, 256)

    def body(pg_ref, pgh_ref, pv_ref, pvh_ref, wg_ref, wv_ref, bg_ref, bv_ref, act_ref, ext_g, ext_v):
        first = pl.program_id(1) == 0
        ug = _conv3_tile(pg_ref, pgh_ref, first, wg_ref, bg_ref, ts, ext_g)
        uv = _conv3_tile(pv_ref, pvh_ref, first, wv_ref, bv_ref, ts, ext_v)
        act_ref[...] = (_gelu(ug) * uv).astype(BF16)

    tile, prev, vec = _ffn_specs(s, ts, tc, hb)
    return pl.pallas_call(
        body, name=name,
        grid=(hb, s // ts),
        in_specs=[tile(0), prev(0), tile(hb), prev(hb), vec(FFN_CONV_K, 0), vec(FFN_CONV_K, hb), vec(1, 0), vec(1, hb)],
        out_specs=pl.BlockSpec((ts, tc), lambda j, i: (i, j)),
        out_shape=jax.ShapeDtypeStruct((s, f), BF16),
        scratch_shapes=[pltpu.VMEM((ts + FFN_HALO, tc), F32)] * 2,
        compiler_params=pltpu.CompilerParams(dimension_semantics=("arbitrary", "arbitrary"),
                                             vmem_limit_bytes=_vmem_limit(16 * ts * tc * 4)),
    )(p, p, p, p, cw, cw, cb, cb)


def _ffn_act_bwd(p, dact, cw, cb, name):
    s, f2 = p.shape
    f = f2 // 2
    tc = f // 2
    hb = f // tc
    ts = _row_tile(s, 256)

    def body(pg_ref, pgh_ref, pv_ref, pvh_ref, da_ref, wg_ref, wv_ref, bg_ref, bv_ref,
             dug_ref, duv_ref, dwg_ref, dwv_ref, dbg_ref, dbv_ref, ext_g, ext_v):
        first = pl.program_id(1) == 0

        @pl.when(first)
        def _():
            for ref in (dwg_ref, dwv_ref, dbg_ref, dbv_ref):
                ref[...] = jnp.zeros_like(ref)

        ug = _conv3_tile(pg_ref, pgh_ref, first, wg_ref, bg_ref, ts, ext_g)
        uv = _conv3_tile(pv_ref, pvh_ref, first, wv_ref, bv_ref, ts, ext_v)
        da = da_ref[...]
        dug = da * uv * _gelu_grad(ug)
        duv = da * _gelu(ug)
        dug_ref[...] = dug
        duv_ref[...] = duv
        dbg_ref[...] += _rsum(dug)
        dbv_ref[...] += _rsum(duv)
        for k in range(FFN_CONV_K):
            off = FFN_HALO - (FFN_CONV_K - 1) + k
            dwg_ref[k:k + 1, :] += _rsum(dug * ext_g[pl.ds(off, ts), :])
            dwv_ref[k:k + 1, :] += _rsum(duv * ext_v[pl.ds(off, ts), :])

    tile, prev, vec = _ffn_specs(s, ts, tc, hb)
    half = pl.BlockSpec((ts, tc), lambda j, i: (i, j))
    wacc = pl.BlockSpec((FFN_CONV_K, tc), lambda j, i: (0, j))
    bacc = pl.BlockSpec((1, tc), lambda j, i: (0, j))
    return pl.pallas_call(
        body, name=name,
        grid=(hb, s // ts),
        in_specs=[tile(0), prev(0), tile(hb), prev(hb), half, vec(FFN_CONV_K, 0), vec(FFN_CONV_K, hb), vec(1, 0), vec(1, hb)],
        out_specs=[half, half, wacc, wacc, bacc, bacc],
        out_shape=[jax.ShapeDtypeStruct((s, f), F32)] * 2 + [jax.ShapeDtypeStruct((FFN_CONV_K, f), F32)] * 2
        + [jax.ShapeDtypeStruct((1, f), F32)] * 2,
        scratch_shapes=[pltpu.VMEM((ts + FFN_HALO, tc), F32)] * 2,
        compiler_params=pltpu.CompilerParams(dimension_semantics=("arbitrary", "arbitrary"),
                                             vmem_limit_bytes=_vmem_limit(24 * ts * tc * 4)),
    )(p, p, p, p, dact, cw, cw, cb, cb)


def _ffn_in_bwd(dug, duv, cw, wt, x, dres, gain, s1p, name):
    s, f = dug.shape
    d = x.shape[1]
    ts = _row_tile(s, 256)
    tc = f // 2
    nbh = ts // FFN_HALO

    def body(dug_ref, dugn_ref, duv_ref, duvn_ref, cw_ref, wt_ref, x_ref, dres_ref, g_ref, s_ref,
             dx_ref, dp_ref, dsh_ref, dsc_ref, dg_ref, ext):
        i = pl.program_id(0)
        last = i == pl.num_programs(0) - 1

        @pl.when(i == 0)
        def _():
            for ref in (dsh_ref, dsc_ref, dg_ref):
                ref[...] = jnp.zeros_like(ref)

        dh = jnp.zeros((ts, d), F32)
        for half, (t_ref, n_ref) in enumerate(((dug_ref, dugn_ref), (duv_ref, duvn_ref))):
            for cb in range(f // tc):
                cols = slice(cb * tc, (cb + 1) * tc)
                wcols = slice(half * f + cb * tc, half * f + (cb + 1) * tc)
                ext[pl.ds(0, ts), :] = t_ref[:, cols]
                ext[pl.ds(ts, FFN_HALO), :] = jnp.where(last, 0.0, n_ref[:, cols])
                acc = cw_ref[FFN_CONV_K - 1:FFN_CONV_K, wcols] * t_ref[:, cols]
                for k in range(FFN_CONV_K - 1):
                    acc = acc + cw_ref[k:k + 1, wcols] * ext[pl.ds(FFN_CONV_K - 1 - k, ts), :]
                dpb = acc.astype(BF16)
                dp_ref[:, wcols] = dpb
                dh = dh + _dot(dpb, wt_ref[wcols, :])
        dx, dsh, dsc, dg = _norm_mod_bwd(dh, x_ref[...], g_ref[...], s_ref[...], dres_ref[...])
        dx_ref[...] = dx
        dsh_ref[...] += dsh
        dsc_ref[...] += dsc
        dg_ref[...] += dg

    tile = pl.BlockSpec((ts, f), lambda i: (i, 0))
    nxt = pl.BlockSpec((FFN_HALO, f), lambda i: (jnp.minimum((i + 1) * nbh, s // FFN_HALO - 1), 0))
    row = pl.BlockSpec((ts, d), lambda i: (i, 0))
    vec = pl.BlockSpec((1, d), lambda i: (0, 0))
    return pl.pallas_call(
        body, name=name,
        grid=(s // ts,),
        in_specs=[tile, nxt, tile, nxt, _const_spec(cw.shape),
                  pl.BlockSpec(wt.shape, lambda i: (0, 0), pipeline_mode=pl.Buffered(1)), row, row, vec, vec],
        out_specs=[row, pl.BlockSpec((ts, 2 * f), lambda i: (i, 0)), vec, vec, vec],
        out_shape=[jax.ShapeDtypeStruct((s, d), F32), jax.ShapeDtypeStruct((s, 2 * f), BF16)] + [jax.ShapeDtypeStruct((1, d), F32)] * 3,
        scratch_shapes=[pltpu.VMEM((ts + FFN_HALO, tc), F32)],
        compiler_params=pltpu.CompilerParams(
            dimension_semantics=("arbitrary",),
            vmem_limit_bytes=_vmem_limit(4 * ts * f * 4 + 2 * f * d * 2 + 2 * ts * 2 * f * 2 + 12 * ts * d * 4 + 6 * ts * tc * 4)),
    )(dug, dug, duv, duv, cw, wt, x, dres, gain, s1p)


def _adamw_math(w, g, m, v):
    m = ADAM_B1 * m + (1.0 - ADAM_B1) * g
    v = ADAM_B2 * v + (1.0 - ADAM_B2) * (g * g)
    m_hat = m / (1.0 - ADAM_B1 ** ADAM_STEP)
    v_hat = v / (1.0 - ADAM_B2 ** ADAM_STEP)
    delta = -ADAM_LR * (m_hat / (jnp.sqrt(v_hat) + ADAM_EPS) + ADAM_WD * w)
    return delta, m, v


def _adam_rows(rows, cols):
    want = max(8, (2 * 1024 * 1024 // (cols * 4)) // 8 * 8)
    tr = min(rows, want)
    while rows % tr:
        tr -= 8
    return tr


def _adamw(w, m, v, g_parts, name):
    shape = w.shape
    c = shape[-1]
    r = math.prod(shape[:-1])
    tr = _adam_rows(r, c)
    ng = len(g_parts)

    def body(*refs):
        w_ref, m_ref, v_ref = refs[0:3]
        g_refs = refs[3:3 + ng]
        g_out, d_out, m_out, v_out = refs[3 + ng:]
        g = g_refs[0][...]
        for gr in g_refs[1:]:
            g = g + gr[...]
        delta, mn, vn = _adamw_math(w_ref[...], g, m_ref[...], v_ref[...])
        g_out[...] = g
        d_out[...] = delta
        m_out[...] = mn
        v_out[...] = vn

    blk = pl.BlockSpec((tr, c), lambda i: (i, 0))
    outs = pl.pallas_call(
        body, name=name,
        grid=(r // tr,),
        in_specs=[blk] * (3 + ng),
        out_specs=[blk] * 4,
        out_shape=[jax.ShapeDtypeStruct((r, c), F32)] * 4,
        compiler_params=pltpu.CompilerParams(dimension_semantics=("arbitrary",),
                                             vmem_limit_bytes=_vmem_limit(2 * (7 + ng) * tr * max(c, 128) * 4 + (8 << 20))),
    )(w.reshape(r, c), m.reshape(r, c), v.reshape(r, c), *[g.reshape(r, c) for g in g_parts])
    return [o.reshape(shape) for o in outs]


def _modw_adamw(sct, dmod, w, m, v, name):
    nl, d, n = w.shape
    tr = _row_tile(d, 256)

    def body(sct_ref, dm_ref, w_ref, m_ref, v_ref, g_out, d_out, m_out, v_out):
        sc = sct_ref[...].astype(BF16).astype(F32)
        dm = dm_ref[...].astype(BF16).astype(F32)
        g = sc[:, 0:1] * dm[0:1, :]
        for b in range(1, N_DEV):
            g = g + sc[:, b:b + 1] * dm[b:b + 1, :]
        delta, mn, vn = _adamw_math(w_ref[...], g, m_ref[...], v_ref[...])
        g_out[...] = g
        d_out[...] = delta
        m_out[...] = mn
        v_out[...] = vn

    blk = pl.BlockSpec((None, tr, n), lambda l, i: (l, i, 0))
    return pl.pallas_call(
        body, name=name,
        grid=(nl, d // tr),
        in_specs=[pl.BlockSpec((tr, N_DEV), lambda l, i: (i, 0)), pl.BlockSpec((None, N_DEV, n), lambda l, i: (l, 0, 0)),
                  blk, blk, blk],
        out_specs=[blk] * 4,
        out_shape=[jax.ShapeDtypeStruct((nl, d, n), F32)] * 4,
        compiler_params=pltpu.CompilerParams(dimension_semantics=("arbitrary", "arbitrary"),
                                             vmem_limit_bytes=_vmem_limit(2 * 8 * tr * n * 4 + (8 << 20))),
    )(sct, dmod, w, m, v)


def _reduce4(recv, name):
    shape = recv.shape[1:]
    c = shape[-1]
    r = math.prod(shape[:-1])
    tr = _adam_rows(r, c)

    def body(r_ref, o_ref):
        acc = r_ref[0].astype(F32)
        for k in range(1, N_CHIPS):
            acc = acc + r_ref[k].astype(F32)
        o_ref[...] = acc

    return pl.pallas_call(
        body, name=name,
        grid=(r // tr,),
        in_specs=[pl.BlockSpec((N_CHIPS, tr, c), lambda i: (0, i, 0))],
        out_specs=pl.BlockSpec((tr, c), lambda i: (i, 0)),
        out_shape=jax.ShapeDtypeStruct((r, c), F32),
        compiler_params=pltpu.CompilerParams(dimension_semantics=("arbitrary",),
                                             vmem_limit_bytes=_vmem_limit(2 * 8 * tr * max(c, 128) * 4 + (8 << 20))),
    )(recv.reshape(N_CHIPS, r, c)).reshape(shape)


def _my_place():
    return lax.axis_index("x"), lax.axis_index("y"), lax.axis_index("c")


def _chip_coords(j):
    return j // 2, j % 2


def _mod_forward(c, mod_w, mod_b4):
    nl, d, n = mod_w.shape
    kc = 256

    def body(c_ref, w_ref, b_ref, mod_ref, sc_ref, cbuf, stage, s1, r1, s2, r2):
        mx, my, mc = _my_place()
        me = 4 * mx + 2 * my + mc
        q = 2 * mx + my
        cv = c_ref[...]
        cbuf[me] = jnp.broadcast_to(cv * jax.nn.sigmoid(cv), (8, d))
        sends = []
        for t in range(N_DEV):
            tx, ty = _chip_coords(t // 2)
            cp = pltpu.make_async_remote_copy(src_ref=cbuf.at[me], dst_ref=cbuf.at[me], send_sem=s1.at[t], recv_sem=r1.at[me],
                                              device_id=(tx, ty, t % 2), device_id_type=MESH)

            @pl.when(t != me)
            def _():
                cp.start()

            sends.append((t, cp))
        for t in range(N_DEV):
            @pl.when(t != me)
            def _():
                pltpu.make_async_remote_copy(src_ref=cbuf.at[t], dst_ref=cbuf.at[t], send_sem=s1.at[t], recv_sem=r1.at[t],
                                             device_id=(mx, my, mc), device_id_type=MESH).wait_recv()
        for t, cp in sends:
            @pl.when(t != me)
            def _():
                cp.wait_send()

        row = lax.broadcasted_iota(jnp.int32, (8, d), 0)
        sc_all = jnp.zeros((8, d), F32)
        for t in range(N_DEV):
            sc_all = sc_all + jnp.where(row == t, cbuf[t], 0.0)
        sc_ref[...] = sc_all
        rown = lax.broadcasted_iota(jnp.int32, (8, n), 0)
        for l in range(nl):
            acc = jnp.zeros((8, n), F32)
            for k0 in range(0, d, kc):
                acc = acc + _dot(sc_all[:, k0:k0 + kc].astype(BF16), w_ref[l, k0:k0 + kc, :].astype(BF16))
            acc = acc + b_ref[l, q]
            for j in range(N_CHIPS):
                jx, jy = _chip_coords(j)
                bdest = 4 * jx + 2 * jy + mc
                rowv = jnp.sum(jnp.where(rown == bdest, acc, 0.0), axis=0, keepdims=True)
                stage[j, l] = jnp.broadcast_to(rowv, (8, n))
        sends2 = []
        for j in range(N_CHIPS):
            jx, jy = _chip_coords(j)
            cp = pltpu.make_async_remote_copy(src_ref=stage.at[j], dst_ref=mod_ref.at[:, q], send_sem=s2.at[j], recv_sem=r2.at[q],
                                              device_id=(jx, jy, mc), device_id_type=MESH)

            @pl.when(j != q)
            def _():
                cp.start()

            @pl.when(j == q)
            def _():
                for l in range(nl):
                    mod_ref[l, j] = stage[j, l]

            sends2.append((j, cp))
        for j in range(N_CHIPS):
            @pl.when(j != q)
            def _():
                pltpu.make_async_remote_copy(src_ref=stage.at[j], dst_ref=mod_ref.at[:, j], send_sem=s2.at[j], recv_sem=r2.at[j],
                                             device_id=(mx, my, mc), device_id_type=MESH).wait_recv()
        for j, cp in sends2:
            @pl.when(j != q)
            def _():
                cp.wait_send()

    vm = pl.BlockSpec(memory_space=pltpu.VMEM)
    return pl.pallas_call(
        body, name="mod_forward",
        in_specs=[vm, vm, vm],
        out_specs=[vm, vm],
        out_shape=[jax.ShapeDtypeStruct((nl, N_CHIPS, 8, n), F32), jax.ShapeDtypeStruct((8, d), F32)],
        scratch_shapes=[pltpu.VMEM((N_DEV, 8, d), F32), pltpu.VMEM((N_CHIPS, nl, 8, n), F32),
                        pltpu.SemaphoreType.DMA((N_DEV,)), pltpu.SemaphoreType.DMA((N_DEV,)),
                        pltpu.SemaphoreType.DMA((N_CHIPS,)), pltpu.SemaphoreType.DMA((N_CHIPS,))],
        compiler_params=pltpu.CompilerParams(vmem_limit_bytes=_vmem_limit(2 * nl * d * n * 4 + (8 << 20))),
    )(c, mod_w, mod_b4)


def _gather_shards(shards):
    na = len(shards)

    def body(*refs):
        ins, outs = refs[:na], refs[na:2 * na]
        ssem, rsem, lsem = refs[2 * na:]
        mx, my, mc = _my_place()
        q = 2 * mx + my
        local = [pltpu.make_async_copy(ins[a], outs[a].at[q], lsem.at[a]) for a in range(na)]
        for cp in local:
            cp.start()
        sends = []
        for j in range(N_CHIPS):
            jx, jy = _chip_coords(j)
            for a in range(na):
                cp = pltpu.make_async_remote_copy(src_ref=ins[a], dst_ref=outs[a].at[q], send_sem=ssem.at[a, j],
                                                  recv_sem=rsem.at[a, q], device_id=(jx, jy, mc), device_id_type=MESH)

                @pl.when(j != q)
                def _():
                    cp.start()

                sends.append((j, cp))
        for j in range(N_CHIPS):
            for a in range(na):
                @pl.when(j != q)
                def _():
                    pltpu.make_async_remote_copy(src_ref=ins[a], dst_ref=outs[a].at[j], send_sem=ssem.at[a, j],
                                                 recv_sem=rsem.at[a, j], device_id=(mx, my, mc), device_id_type=MESH).wait_recv()
        for j, cp in sends:
            @pl.when(j != q)
            def _():
                cp.wait_send()
        for cp in local:
            cp.wait()

    hbm = pl.BlockSpec(memory_space=pl.ANY)
    return pl.pallas_call(
        body, name="gather_weights",
        in_specs=[hbm] * na,
        out_specs=[hbm] * na,
        out_shape=[jax.ShapeDtypeStruct((N_CHIPS,) + a.shape, a.dtype) for a in shards],
        scratch_shapes=[pltpu.SemaphoreType.DMA((na, N_CHIPS)), pltpu.SemaphoreType.DMA((na, N_CHIPS)),
                        pltpu.SemaphoreType.DMA((na,))],
    )(*shards)


def _scatter_partials(parts, slot_axes):
    na = len(parts)
    slot_shapes = [p.shape[:ax] + p.shape[ax + 1:] for p, ax in zip(parts, slot_axes)]
    slot_fns = [(lambda r, j: r.at[j]) if ax == 0 else (lambda r, j: r.at[:, j]) for ax in slot_axes]

    def body(*refs):
        ins, outs = refs[:na], refs[na:2 * na]
        ssem, rsem, lsem = refs[2 * na:]
        mx, my, mc = _my_place()
        q = 2 * mx + my
        sends = []
        for j in range(N_CHIPS):
            jx, jy = _chip_coords(j)
            for a in range(na):
                src = slot_fns[a](ins[a], j)
                cp = pltpu.make_async_remote_copy(src_ref=src, dst_ref=outs[a].at[q], send_sem=ssem.at[a, j],
                                                  recv_sem=rsem.at[a, q], device_id=(jx, jy, mc), device_id_type=MESH)
                lc = pltpu.make_async_copy(src, outs[a].at[j], lsem.at[a])

                @pl.when(j != q)
                def _():
                    cp.start()

                @pl.when(j == q)
                def _():
                    lc.start()

                sends.append((j, cp, lc))
        for j in range(N_CHIPS):
            for a in range(na):
                @pl.when(j != q)
                def _():
                    pltpu.make_async_remote_copy(src_ref=slot_fns[a](ins[a], j), dst_ref=outs[a].at[j], send_sem=ssem.at[a, j],
                                                 recv_sem=rsem.at[a, j], device_id=(mx, my, mc), device_id_type=MESH).wait_recv()
        for j, cp, lc in sends:
            @pl.when(j != q)
            def _():
                cp.wait_send()

            @pl.when(j == q)
            def _():
                lc.wait()

    hbm = pl.BlockSpec(memory_space=pl.ANY)
    return pl.pallas_call(
        body, name="scatter_grads",
        in_specs=[hbm] * na,
        out_specs=[hbm] * na,
        out_shape=[jax.ShapeDtypeStruct((N_CHIPS,) + shp, p.dtype) for shp, p in zip(slot_shapes, parts)],
        scratch_shapes=[pltpu.SemaphoreType.DMA((na, N_CHIPS)), pltpu.SemaphoreType.DMA((na, N_CHIPS)),
                        pltpu.SemaphoreType.DMA((na,))],
    )(*parts)


def _swap_with_sibling(arrs):
    na = len(arrs)

    def body(*refs):
        ins, outs = refs[:na], refs[na:2 * na]
        ssem, rsem = refs[2 * na:]
        mx, my, mc = _my_place()
        cps = [pltpu.make_async_remote_copy(src_ref=ins[a], dst_ref=outs[a], send_sem=ssem.at[a], recv_sem=rsem.at[a],
                                            device_id=(mx, my, 1 - mc), device_id_type=MESH) for a in range(na)]
        for cp in cps:
            cp.start()
        for cp in cps:
            cp.wait()

    hbm = pl.BlockSpec(memory_space=pl.ANY)
    return pl.pallas_call(
        body, name="swap_sibling",
        in_specs=[hbm] * na,
        out_specs=[hbm] * na,
        out_shape=[jax.ShapeDtypeStruct(a.shape, a.dtype) for a in arrs],
        scratch_shapes=[pltpu.SemaphoreType.DMA((na,)), pltpu.SemaphoreType.DMA((na,))],
    )(*arrs)


def _allgather_sum(packed):
    r, c = packed.shape

    def body(p_ref, all_ref, sum_ref, ssem, rsem):
        mx, my, mc = _my_place()
        me = 4 * mx + 2 * my + mc
        all_ref[me] = p_ref[...]
        sends = []
        for t in range(N_DEV):
            tx, ty = _chip_coords(t // 2)
            cp = pltpu.make_async_remote_copy(src_ref=p_ref, dst_ref=all_ref.at[me], send_sem=ssem.at[t], recv_sem=rsem.at[me],
                                              device_id=(tx, ty, t % 2), device_id_type=MESH)

            @pl.when(t != me)
            def _():
                cp.start()

            sends.append((t, cp))
        for t in range(N_DEV):
            @pl.when(t != me)
            def _():
                pltpu.make_async_remote_copy(src_ref=p_ref, dst_ref=all_ref.at[t], send_sem=ssem.at[t], recv_sem=rsem.at[t],
                                             device_id=(mx, my, mc), device_id_type=MESH).wait_recv()
        for t, cp in sends:
            @pl.when(t != me)
            def _():
                cp.wait_send()
        acc = all_ref[0]
        for t in range(1, N_DEV):
            acc = acc + all_ref[t]
        sum_ref[...] = acc

    vm = pl.BlockSpec(memory_space=pltpu.VMEM)
    return pl.pallas_call(
        body, name="allgather_small",
        in_specs=[vm],
        out_specs=[vm, vm],
        out_shape=[jax.ShapeDtypeStruct((N_DEV, r, c), F32), jax.ShapeDtypeStruct((r, c), F32)],
        scratch_shapes=[pltpu.SemaphoreType.DMA((N_DEV,)), pltpu.SemaphoreType.DMA((N_DEV,))],
        compiler_params=pltpu.CompilerParams(vmem_limit_bytes=_vmem_limit((N_DEV + 4) * r * c * 4 + (4 << 20))),
    )(packed)


def _pack(arrs):
    rows, layout, at = [], [], 0
    for a in arrs:
        n = a.size
        nr = -(-n // SMALL_COLS)
        flat = a.reshape(-1)
        if nr * SMALL_COLS != n:
            flat = jnp.concatenate([flat, jnp.zeros((nr * SMALL_COLS - n,), F32)])
        rows.append(flat.reshape(nr, SMALL_COLS))
        layout.append((at, nr, a.shape))
        at += nr
    pad = -at % 8
    if pad:
        rows.append(jnp.zeros((pad, SMALL_COLS), F32))
    return jnp.concatenate(rows, axis=0), layout


def _unpack(buf, layout):
    out = []
    for at, nr, shape in layout:
        n = math.prod(shape)
        out.append(buf[at:at + nr].reshape(-1)[:n].reshape(shape))
    return out


SMALL_NAMES = ("mod_b", "mix_pre_g", "mix_post_g", "sgu_norm_g", "sgu_norm_b", "sgu_w", "sgu_b", "conv_b", "conv_norm_g",
               "conv_norm_b", "pool_w", "pool_scale", "branch_g", "ffn_pre_g", "ffn_post_g", "ffn_conv_b")
SHARDED_SMALL = ("conv_w", "ffn_conv_w")
WEIGHT_ORDER = ("mod_w", "mod_b", "mix_pre_g", "mix_post_g", "w_in", "sgu_norm_g", "sgu_norm_b", "sgu_w", "sgu_b", "conv_w",
                "conv_b", "conv_norm_g", "conv_norm_b", "pool_w", "pool_scale", "branch_g", "w_out", "ffn_pre_g", "ffn_post_g",
                "ffn_up", "ffn_conv_w", "ffn_conv_b", "ffn_down")


def _block_diag(blocks):
    n, a, b = blocks.shape
    eye = jnp.eye(n, dtype=blocks.dtype)
    return (eye[:, None, :, None] * blocks[:, :, None, :]).reshape(n * a, n * b)


def _diag_blocks(mat, n):
    a = mat.shape[0] // n
    return jnp.stack([mat[g * a:(g + 1) * a, g * a:(g + 1) * a] for g in range(n)])


def _step(x, c, loss_target, w, m, v):
    nl = w["mod_w"].shape[0]
    s, d = x.shape[1], x.shape[2]
    heads = SGU_WIDTH // HEAD_DIM
    groups = len(POOL_WINDOWS)
    mx, my, _ = _my_place()
    q = 2 * mx + my
    x0 = x.reshape(s, d)
    tgt = loss_target.reshape(s, d)

    nmod = w["mod_w"].shape[2]
    mod4, sc_all = _mod_forward(c, w["mod_w"], w["mod_b"].reshape(nl, N_CHIPS, 1, nmod))
    mod = mod4[:, :, 0, :].reshape(nl, N_MOD, 1, d)
    g_win, g_wout, g_up, g_down, g_cw, g_fcw = _gather_shards(
        [w["w_in"].astype(BF16), w["w_out"].astype(BF16), w["ffn_up"].astype(BF16), w["ffn_down"].astype(BF16),
         w["conv_w"], w["ffn_conv_w"]])
    inw = g_win.shape[3] * N_CHIPS
    f2 = g_up.shape[3] * N_CHIPS
    f = f2 // 2
    w_in = jnp.transpose(g_win, (1, 2, 0, 3)).reshape(nl, d, inw)
    w_in_t = jnp.transpose(g_win, (1, 0, 3, 2)).reshape(nl, inw, d)
    w_out = jnp.transpose(g_wout, (1, 0, 2, 3)).reshape(nl, d, d)
    w_out_t = jnp.transpose(g_wout, (1, 3, 0, 2)).reshape(nl, d, d)
    up_t = jnp.transpose(g_up, (1, 0, 3, 2)).reshape(nl, f2, d)
    down = jnp.transpose(g_down, (1, 0, 2, 3)).reshape(nl, f, d)
    down_t = jnp.transpose(g_down, (1, 3, 0, 2)).reshape(nl, d, f)
    conv_w = jnp.transpose(g_cw, (1, 2, 0, 3)).reshape(nl, CONV_K, CONV_WIDTH)
    ffn_cw = jnp.transpose(g_fcw, (1, 2, 0, 3)).reshape(nl, FFN_CONV_K, f2)

    tril = jnp.tril(jnp.ones((CHUNK, CHUNK), bool))
    bd = _block_diag(jnp.ones((heads, HEAD_DIM, HEAD_DIM), BF16))

    def mixer_params(l):
        wm = jnp.where(tril[None], w["sgu_w"][l], 0.0)
        pw = _block_diag(w["pool_w"][l])
        return dict(
            bd=bd, ng=w["sgu_norm_g"][l][None], nb=w["sgu_norm_b"][l][None],
            wm=wm.astype(BF16), wmt=jnp.swapaxes(wm, 1, 2).astype(BF16),
            bias=jnp.repeat(w["sgu_b"][l].T, HEAD_DIM, axis=1),
            cw=conv_w[l], cb=w["conv_b"][l][None], cng=w["conv_norm_g"][l][None], cnb=w["conv_norm_b"][l][None],
            pw=pw.astype(BF16), pwt=pw.T.astype(BF16), ps=w["pool_scale"][l][None], bg=w["branch_g"][l][None])

    saved = []
    xl = x0
    for l in range(nl):
        sh1, sc1, g1, sh2, sc2, g2 = [mod[l, k] for k in range(N_MOD)]
        mp = mixer_params(l)
        gpre1, gpost1 = w["mix_pre_g"][l][None], w["mix_post_g"][l][None]
        gpre2, gpost2 = w["ffn_pre_g"][l][None], w["ffn_post_g"][l][None]
        z, h1 = _norm_mod_matmul(xl, gpre1, 1.0 + sc1, sh1, w_in[l][None], f"mix_in_{l}")
        ycat = _mixer_fwd(z, mp, f"mixer_fwd_{l}")
        o, x1 = _matmul_norm_resid(ycat, w_out[l], xl, g1, gpost1, f"mix_out_{l}")
        p, h2 = _norm_mod_matmul(x1, gpre2, 1.0 + sc2, sh2, g_up[:, l], f"ffn_in_{l}")
        fcb = w["ffn_conv_b"][l][None]
        act = _ffn_act_fwd(p, ffn_cw[l], fcb, f"ffn_act_{l}")
        qo, x2 = _matmul_norm_resid(act, down[l], x1, g2, gpost2, f"ffn_out_{l}")
        saved.append(dict(x=xl, z=z, h1=h1, ycat=ycat, o=o, x1=x1, p=p, h2=h2, act=act, qo=qo, mp=mp, fcb=fcb,
                          mods=(sh1, sc1, g1, sh2, sc2, g2), gains=(gpre1, gpost1, gpre2, gpost2)))
        xl = x2

    dx, loss_row = _loss_head(xl, tgt)

    small = {n: [None] * nl for n in SMALL_NAMES + SHARDED_SMALL}
    dmods = [None] * nl
    gbuf = dict(w_in=None, w_out=None, ffn_up=None, ffn_down=None)
    for l in reversed(range(nl)):
        sv = saved[l]
        sh1, sc1, g1, sh2, sc2, g2 = sv["mods"]
        gpre1, gpost1, gpre2, gpost2 = sv["gains"]
        dq, dact, dg2, dgpost2 = _resid_bwd_matmul(dx, sv["qo"], g2, gpost2, down_t[l], f"ffn_out_bwd_{l}")
        gbuf["ffn_down"] = _wgrad(
            sv["act"], dq, (f, lambda j: 0), (d, lambda j: 0), jax.ShapeDtypeStruct((nl, f, d), BF16),
            (1, lambda j, l=l: (l, 0, 0)), (None, f, d), gbuf["ffn_down"], f"wgrad_ffn_down_{l}")
        dug, duv, dfwg, dfwv, dfbg, dfbv = _ffn_act_bwd(sv["p"], dact, ffn_cw[l], sv["fcb"], f"ffn_act_bwd_{l}")
        dx1, dp, dsh2, dsc2, dgpre2 = _ffn_in_bwd(dug, duv, ffn_cw[l], up_t[l], sv["x1"], dx, gpre2, 1.0 + sc2, f"ffn_in_bwd_{l}")
        tn = f2 // N_CHIPS
        gbuf["ffn_up"] = _wgrad(
            sv["h2"], dp, (d, lambda j: 0), (tn, lambda j: j), jax.ShapeDtypeStruct((N_CHIPS, nl, d, tn), BF16),
            (N_CHIPS, lambda j, l=l: (j, l, 0, 0)), (None, None, d, tn), gbuf["ffn_up"], f"wgrad_ffn_up_{l}")
        do, dycat, dg1, dgpost1 = _resid_bwd_matmul(dx1, sv["o"], g1, gpost1, w_out_t[l], f"mix_out_bwd_{l}")
        gbuf["w_out"] = _wgrad(
            sv["ycat"], do, (d, lambda j: 0), (d, lambda j: 0), jax.ShapeDtypeStruct((nl, d, d), BF16),
            (1, lambda j, l=l: (l, 0, 0)), (None, d, d), gbuf["w_out"], f"wgrad_w_out_{l}")
        (dza, dcb, dd, dbg, dwm, dbias, dng, dnb, dcng, dcnb, dps, dpw) = _mixer_bwd_a(sv["z"], dycat, sv["mp"], f"mixer_bwd_a_{l}")
        dx, dz, dsh1, dsc1, dgpre1, dcw, dcbias = _mixer_bwd_b(
            sv["z"], dza, dcb, dd, sv["x"], dx1, gpre1, 1.0 + sc1, sv["mp"]["cw"], w_in_t[l], f"mixer_bwd_b_{l}")
        gbuf["w_in"] = _wgrad(
            sv["h1"], dz, (d, lambda j: 0), (inw, lambda j: 0), jax.ShapeDtypeStruct((nl, d, inw), BF16),
            (1, lambda j, l=l: (l, 0, 0)), (None, d, inw), gbuf["w_in"], f"wgrad_w_in_{l}")

        dmods[l] = jnp.concatenate([dsh1, dsc1, dg1, dsh2, dsc2, dg2], axis=0)
        small["mix_pre_g"][l], small["mix_post_g"][l] = dgpre1[0], dgpost1[0]
        small["ffn_pre_g"][l], small["ffn_post_g"][l] = dgpre2[0], dgpost2[0]
        small["sgu_norm_g"][l], small["sgu_norm_b"][l] = dng[0], dnb[0]
        small["sgu_w"][l] = jnp.where(tril[None], dwm, 0.0)
        small["sgu_b"][l] = dbias.reshape(CHUNK, heads, HEAD_DIM).sum(-1).T
        small["conv_b"][l], small["conv_norm_g"][l], small["conv_norm_b"][l] = dcbias[0], dcng[0], dcnb[0]
        small["pool_w"][l], small["pool_scale"][l], small["branch_g"][l] = _diag_blocks(dpw, groups), dps[0], dbg[0]
        small["ffn_conv_b"][l] = jnp.concatenate([dfbg[0], dfbv[0]])
        small["conv_w"][l] = dcw
        small["ffn_conv_w"][l] = jnp.concatenate([dfwg, dfwv], axis=1)

    names = [n for n in SMALL_NAMES if n != "mod_b"] + list(SHARDED_SMALL)
    packed, layout = _pack([jnp.stack(dmods), loss_row] + [jnp.stack(small[n]) for n in names])
    gathered, summed = _allgather_sum(packed)
    parts = _unpack(summed, layout)
    loss = parts[1][0, 0]
    gsmall = dict(zip(names, parts[2:]))
    gsmall["mod_b"] = parts[0].reshape(nl, N_MOD * d)
    at0, nr0, _ = layout[0]
    dmod_all = gathered[:, at0:at0 + nr0].reshape(N_DEV, nl, N_MOD * d)
    dmod_mine = jnp.transpose(lax.dynamic_slice_in_dim(dmod_all, q * nmod, nmod, axis=2), (1, 0, 2))

    grads, deltas, new_m, new_v = {}, {}, {}, {}

    def put(name, res):
        grads[name], deltas[name], new_m[name], new_v[name] = res

    put("mod_w", _modw_adamw(sc_all.T, dmod_mine, w["mod_w"], m["mod_w"], v["mod_w"], "adamw_mod_w"))

    pw_, lay = _pack([w[n] for n in SMALL_NAMES])
    pm_, _ = _pack([m[n] for n in SMALL_NAMES])
    pv_, _ = _pack([v[n] for n in SMALL_NAMES])
    pg_, _ = _pack([gsmall[n] for n in SMALL_NAMES])
    res = _adamw(pw_, pm_, pv_, [pg_], "adamw_small")
    for n, g_, d_, m_, v_ in zip(SMALL_NAMES, *[_unpack(r_, lay) for r_ in res]):
        put(n, (g_, d_, m_, v_))

    gsh = {"conv_w": lax.dynamic_slice_in_dim(gsmall["conv_w"], q * (CONV_WIDTH // N_CHIPS), CONV_WIDTH // N_CHIPS, axis=2),
           "ffn_conv_w": lax.dynamic_slice_in_dim(gsmall["ffn_conv_w"], q * (f2 // N_CHIPS), f2 // N_CHIPS, axis=2)}
    pw_, lay = _pack([w[n] for n in SHARDED_SMALL])
    pm_, _ = _pack([m[n] for n in SHARDED_SMALL])
    pv_, _ = _pack([v[n] for n in SHARDED_SMALL])
    pg_, _ = _pack([gsh[n] for n in SHARDED_SMALL])
    res = _adamw(pw_, pm_, pv_, [pg_], "adamw_sharded_small")
    for n, g_, d_, m_, v_ in zip(SHARDED_SMALL, *[_unpack(r_, lay) for r_ in res]):
        put(n, (g_, d_, m_, v_))

    kin = inw // N_CHIPS
    win_parts = jnp.transpose(gbuf["w_in"].reshape(nl, d, N_CHIPS, kin), (2, 0, 1, 3))
    recv = _scatter_partials(
        [win_parts, gbuf["w_out"].reshape(nl, N_CHIPS, d // N_CHIPS, d), gbuf["ffn_up"],
         gbuf["ffn_down"].reshape(nl, N_CHIPS, f // N_CHIPS, d)],
        [0, 1, 0, 1])
    big = ("w_in", "w_out", "ffn_up", "ffn_down")
    mine = [_reduce4(r, f"reduce4_{n}") for r, n in zip(recv, big)]
    theirs = _swap_with_sibling(mine)
    for n, a, b in zip(big, mine, theirs):
        put(n, _adamw(w[n], m[n], v[n], [a, b], f"adamw_{n}"))

    return (loss, dx.reshape(1, s, d), *[grads[n] for n in WEIGHT_ORDER], *[deltas[n] for n in WEIGHT_ORDER],
            *[new_m[n] for n in WEIGHT_ORDER], *[new_v[n] for n in WEIGHT_ORDER])


def kernel(x, c, mod_w, mod_b, mix_pre_g, mix_post_g, w_in, sgu_norm_g, sgu_norm_b, sgu_w, sgu_b, conv_w, conv_b, conv_norm_g, conv_norm_b, pool_w, pool_scale, branch_g, w_out, ffn_pre_g, ffn_post_g, ffn_up, ffn_conv_w, ffn_conv_b, ffn_down, loss_target, m_mod_w, m_mod_b, m_mix_pre_g, m_mix_post_g, m_w_in, m_sgu_norm_g, m_sgu_norm_b, m_sgu_w, m_sgu_b, m_conv_w, m_conv_b, m_conv_norm_g, m_conv_norm_b, m_pool_w, m_pool_scale, m_branch_g, m_w_out, m_ffn_pre_g, m_ffn_post_g, m_ffn_up, m_ffn_conv_w, m_ffn_conv_b, m_ffn_down, v_mod_w, v_mod_b, v_mix_pre_g, v_mix_post_g, v_w_in, v_sgu_norm_g, v_sgu_norm_b, v_sgu_w, v_sgu_b, v_conv_w, v_conv_b, v_conv_norm_g, v_conv_norm_b, v_pool_w, v_pool_scale, v_branch_g, v_w_out, v_ffn_pre_g, v_ffn_post_g, v_ffn_up, v_ffn_conv_w, v_ffn_conv_b, v_ffn_down):
    w = dict(mod_w=mod_w, mod_b=mod_b, mix_pre_g=mix_pre_g, mix_post_g=mix_post_g, w_in=w_in, sgu_norm_g=sgu_norm_g,
             sgu_norm_b=sgu_norm_b, sgu_w=sgu_w, sgu_b=sgu_b, conv_w=conv_w, conv_b=conv_b, conv_norm_g=conv_norm_g,
             conv_norm_b=conv_norm_b, pool_w=pool_w, pool_scale=pool_scale, branch_g=branch_g, w_out=w_out,
             ffn_pre_g=ffn_pre_g, ffn_post_g=ffn_post_g, ffn_up=ffn_up, ffn_conv_w=ffn_conv_w, ffn_conv_b=ffn_conv_b,
             ffn_down=ffn_down)
    m = dict(mod_w=m_mod_w, mod_b=m_mod_b, mix_pre_g=m_mix_pre_g, mix_post_g=m_mix_post_g, w_in=m_w_in,
             sgu_norm_g=m_sgu_norm_g, sgu_norm_b=m_sgu_norm_b, sgu_w=m_sgu_w, sgu_b=m_sgu_b, conv_w=m_conv_w,
             conv_b=m_conv_b, conv_norm_g=m_conv_norm_g, conv_norm_b=m_conv_norm_b, pool_w=m_pool_w,
             pool_scale=m_pool_scale, branch_g=m_branch_g, w_out=m_w_out, ffn_pre_g=m_ffn_pre_g, ffn_post_g=m_ffn_post_g,
             ffn_up=m_ffn_up, ffn_conv_w=m_ffn_conv_w, ffn_conv_b=m_ffn_conv_b, ffn_down=m_ffn_down)
    v = dict(mod_w=v_mod_w, mod_b=v_mod_b, mix_pre_g=v_mix_pre_g, mix_post_g=v_mix_post_g, w_in=v_w_in,
             sgu_norm_g=v_sgu_norm_g, sgu_norm_b=v_sgu_norm_b, sgu_w=v_sgu_w, sgu_b=v_sgu_b, conv_w=v_conv_w,
             conv_b=v_conv_b, conv_norm_g=v_conv_norm_g, conv_norm_b=v_conv_norm_b, pool_w=v_pool_w,
             pool_scale=v_pool_scale, branch_g=v_branch_g, w_out=v_w_out, ffn_pre_g=v_ffn_pre_g, ffn_post_g=v_ffn_post_g,
             ffn_up=v_ffn_up, ffn_conv_w=v_ffn_conv_w, ffn_conv_b=v_ffn_conv_b, ffn_down=v_ffn_down)
    return _step(x, c, loss_target, w, m, v)
```

```python
import functools
import math

import jax
import jax.numpy as jnp
from jax import lax
from jax.experimental import pallas as pl
from jax.experimental.pallas import tpu as pltpu

F32 = jnp.float32
BF16 = jnp.bfloat16
MESH = pl.DeviceIdType.MESH

EPS = 1e-6
HEAD_DIM = 64
CHUNK = 128
SGU_WIDTH = 384
CONV_WIDTH = 384
POOL_WIDTH = 256
POOL_WINDOWS = (2, 4, 8, 16)
CONV_K = 31
FFN_CONV_K = 3
N_MOD = 6
N_CHIPS = 4
N_DEV = 8

ADAM_LR = 0.001
ADAM_B1 = 0.9
ADAM_B2 = 0.999
ADAM_EPS = 1e-08
ADAM_WD = 0.01
ADAM_STEP = 10

MIX_HALO = 32
FFN_HALO = 8
CONV_ROWS = 32
SMALL_COLS = 1024
VMEM_BYTES_V7X = 64 * 1024 * 1024


def _vmem_limit(estimate_bytes):
    return int(min(max(estimate_bytes, 16 * 1024 * 1024), VMEM_BYTES_V7X - 8 * 1024 * 1024))


def _row_tile(s, want):
    return want if s % want == 0 else math.gcd(s, want)


def _rsum(v):
    return jnp.sum(v, axis=0, keepdims=True)


def _rmean(v):
    return jnp.mean(v, axis=-1, keepdims=True)


def _gelu(v):
    k = math.sqrt(2.0 / math.pi)
    return 0.5 * v * (1.0 + jnp.tanh(k * (v + 0.044715 * v * v * v)))


def _gelu_grad(v):
    k = math.sqrt(2.0 / math.pi)
    t = jnp.tanh(k * (v + 0.044715 * v * v * v))
    return 0.5 * (1.0 + t) + 0.5 * v * (1.0 - t * t) * (k * (1.0 + 3.0 * 0.044715 * v * v))


def _dot(a, b):
    return jnp.dot(a, b, preferred_element_type=F32)


def _dot_nt(a, b):
    return lax.dot_general(a, b, (((1,), (1,)), ((), ())), preferred_element_type=F32)


def _dot_tn(a, b):
    return lax.dot_general(a, b, (((0,), (0,)), ((), ())), preferred_element_type=F32)


def _group_mean(v, bd):
    hi = v.astype(BF16)
    lo = (v - hi.astype(F32)).astype(BF16)
    return (_dot(hi, bd) + _dot(lo, bd)) * (1.0 / HEAD_DIM)


def _const_spec(shape):
    nd = len(shape)
    return pl.BlockSpec(shape, lambda *_: (0,) * nd)


def _norm_mod_matmul(x, gain, s1p, shift, w, name):
    s, d = x.shape
    nb, _, tn = w.shape
    ts = _row_tile(s, 512 if nb * tn <= 2048 else 256)

    def body(x_ref, g_ref, s_ref, b_ref, w_ref, z_ref, h_ref):
        xv = x_ref[...]
        r = lax.rsqrt(_rmean(xv * xv) + EPS)
        h = ((xv * r) * g_ref[...] * s_ref[...] + b_ref[...]).astype(BF16)
        h_ref[...] = h
        for j in range(nb):
            z_ref[:, j * tn:(j + 1) * tn] = _dot(h, w_ref[j])

    vec = pl.BlockSpec((1, d), lambda i: (0, 0))
    return pl.pallas_call(
        body, name=name,
        grid=(s // ts,),
        in_specs=[pl.BlockSpec((ts, d), lambda i: (i, 0)), vec, vec, vec,
                  pl.BlockSpec((nb, d, tn), lambda i: (0, 0, 0), pipeline_mode=pl.Buffered(1))],
        out_specs=[pl.BlockSpec((ts, nb * tn), lambda i: (i, 0)), pl.BlockSpec((ts, d), lambda i: (i, 0))],
        out_shape=[jax.ShapeDtypeStruct((s, nb * tn), F32), jax.ShapeDtypeStruct((s, d), BF16)],
        compiler_params=pltpu.CompilerParams(
            dimension_semantics=("arbitrary",),
            vmem_limit_bytes=_vmem_limit(2 * (ts * d * 4 + ts * nb * tn * 4 + ts * d * 2) + nb * d * tn * 2 + 4 * ts * d * 4)),
    )(x, gain, s1p, shift, w)


def _matmul_norm_resid(a, w, xres, gate, gpost, name):
    s, k = a.shape
    d = w.shape[1]
    ts = _row_tile(s, 512)

    def body(a_ref, w_ref, x_ref, gate_ref, gp_ref, o_ref, xn_ref):
        o = _dot(a_ref[...], w_ref[...])
        o_ref[...] = o
        r = lax.rsqrt(_rmean(o * o) + EPS)
        xn_ref[...] = x_ref[...] + gate_ref[...] * ((o * r) * gp_ref[...])

    vec = pl.BlockSpec((1, d), lambda i: (0, 0))
    row = pl.BlockSpec((ts, d), lambda i: (i, 0))
    return pl.pallas_call(
        body, name=name,
        grid=(s // ts,),
        in_specs=[pl.BlockSpec((ts, k), lambda i: (i, 0)),
                  pl.BlockSpec((k, d), lambda i: (0, 0), pipeline_mode=pl.Buffered(1)), row, vec, vec],
        out_specs=[row, row],
        out_shape=[jax.ShapeDtypeStruct((s, d), F32)] * 2,
        compiler_params=pltpu.CompilerParams(
            dimension_semantics=("arbitrary",),
            vmem_limit_bytes=_vmem_limit(2 * (ts * k * 2 + 3 * ts * d * 4) + k * d * 2 + 4 * ts * d * 4)),
    )(a, w, xres, gate, gpost)


def _wgrad(a, b, acols, bcols, out_struct, out_index, out_block, name):
    s = a.shape[0]
    ts = _row_tile(s, 512)
    aw, afn = acols
    bw, bfn = bcols
    nj = out_index[0]
    oidx = out_index[1]

    def body(a_ref, b_ref, o_ref, acc):
        i = pl.program_id(1)

        @pl.when(i == 0)
        def _():
            acc[...] = jnp.zeros_like(acc)

        acc[...] += _dot_tn(a_ref[...], b_ref[...])

        @pl.when(i == pl.num_programs(1) - 1)
        def _():
            o_ref[...] = acc[...].astype(o_ref.dtype)

    return pl.pallas_call(
        body, name=name,
        grid=(nj, s // ts),
        in_specs=[pl.BlockSpec((ts, aw), lambda j, i: (i, afn(j))), pl.BlockSpec((ts, bw), lambda j, i: (i, bfn(j)))],
        out_specs=pl.BlockSpec(out_block, lambda j, i: oidx(j)),
        out_shape=out_struct,
        scratch_shapes=[pltpu.VMEM((aw, bw), F32)],
        compiler_params=pltpu.CompilerParams(
            dimension_semantics=("arbitrary", "arbitrary"),
            vmem_limit_bytes=_vmem_limit(2 * (ts * aw * 2 + ts * bw * 2) + 3 * aw * bw * 4 + ts * aw * 4)),
    )(a, b)


def _loss_head(xo, tgt):
    s, d = xo.shape
    ts = _row_tile(s, 512)

    def body(x_ref, t_ref, dx_ref, l_ref, acc):
        i = pl.program_id(0)

        @pl.when(i == 0)
        def _():
            acc[...] = jnp.zeros_like(acc)

        e = x_ref[...] - t_ref[...]
        dx_ref[...] = e * (1.0 / d)
        acc[...] += _rsum(e * e)

        @pl.when(i == pl.num_programs(0) - 1)
        def _():
            tot = jnp.sum(acc[...], axis=-1, keepdims=True) * (0.5 / d)
            l_ref[...] = jnp.broadcast_to(tot, l_ref.shape)

    row = pl.BlockSpec((ts, d), lambda i: (i, 0))
    return pl.pallas_call(
        body, name="loss_head",
        grid=(s // ts,),
        in_specs=[row, row],
        out_specs=[row, pl.BlockSpec((1, SMALL_COLS), lambda i: (0, 0))],
        out_shape=[jax.ShapeDtypeStruct((s, d), F32), jax.ShapeDtypeStruct((1, SMALL_COLS), F32)],
        scratch_shapes=[pltpu.VMEM((1, d), F32)],
        compiler_params=pltpu.CompilerParams(dimension_semantics=("arbitrary",)),
    )(xo, tgt)


def _resid_bwd_matmul(dxn, o, gate, gpost, w, name):
    s, d = dxn.shape
    k = w.shape[0]
    ts = _row_tile(s, 512)

    def body(dx_ref, o_ref, gate_ref, gp_ref, w_ref, do_ref, da_ref, dgate_ref, dgp_ref):
        i = pl.program_id(0)

        @pl.when(i == 0)
        def _():
            dgate_ref[...] = jnp.zeros_like(dgate_ref)
            dgp_ref[...] = jnp.zeros_like(dgp_ref)

        dx = dx_ref[...]
        o = o_ref[...]
        r = lax.rsqrt(_rmean(o * o) + EPS)
        on = o * r
        dgate_ref[...] += _rsum(dx * (on * gp_ref[...]))
        don = dx * gate_ref[...]
        dgp_ref[...] += _rsum(don * on)
        t = don * gp_ref[...]
        do = (r * (t - on * _rmean(t * on))).astype(BF16)
        do_ref[...] = do
        da_ref[...] = _dot_nt(do, w_ref[...])

    vec = pl.BlockSpec((1, d), lambda i: (0, 0))
    row = pl.BlockSpec((ts, d), lambda i: (i, 0))
    return pl.pallas_call(
        body, name=name,
        grid=(s // ts,),
        in_specs=[row, row, vec, vec, pl.BlockSpec((k, d), lambda i: (0, 0), pipeline_mode=pl.Buffered(1))],
        out_specs=[row, pl.BlockSpec((ts, k), lambda i: (i, 0)), vec, vec],
        out_shape=[jax.ShapeDtypeStruct((s, d), BF16), jax.ShapeDtypeStruct((s, k), F32),
                   jax.ShapeDtypeStruct((1, d), F32), jax.ShapeDtypeStruct((1, d), F32)],
        compiler_params=pltpu.CompilerParams(
            dimension_semantics=("arbitrary",),
            vmem_limit_bytes=_vmem_limit(2 * (2 * ts * d * 4 + ts * d * 2 + ts * k * 4) + d * k * 2 + 6 * ts * d * 4)),
    )(dxn, o, gate, gpost, w)


def _norm_mod_bwd(dh, xv, gain, s1p, dres):
    r = lax.rsqrt(_rmean(xv * xv) + EPS)
    xn = xv * r
    dshift = _rsum(dh)
    t = dh * xn
    dscale = _rsum(t * gain)
    dgain = _rsum(t * s1p)
    dxn = dh * (gain * s1p)
    dx = r * (dxn - xn * _rmean(dxn * xn)) + dres
    return dx, dshift, dscale, dgain


def _lane_lt(shape, bound):
    return lax.broadcasted_iota(jnp.int32, shape, 1) < bound


def _sgu_forward(z_ref, bd_ref, ng_ref, nb_ref, wm_ref, bias_ref, ts, ya_s, f_s):
    u = _gelu(z_ref[:, 0:SGU_WIDTH])
    v = _gelu(z_ref[:, SGU_WIDTH:2 * SGU_WIDTH])
    bd = bd_ref[...]
    vc = v - _group_mean(v, bd)
    rstd = lax.rsqrt(_group_mean(vc * vc, bd) + EPS)
    vhat = vc * rstd
    vn = (vhat * ng_ref[...] + nb_ref[...]).astype(BF16)
    left = _lane_lt((CHUNK, CHUNK), HEAD_DIM)
    for n in range(ts // CHUNK):
        rows = slice(n * CHUNK, (n + 1) * CHUNK)
        for p in range(SGU_WIDTH // CHUNK):
            cols = slice(p * CHUNK, (p + 1) * CHUNK)
            blk = vn[rows, cols]
            f = jnp.where(left, _dot(wm_ref[2 * p], blk), _dot(wm_ref[2 * p + 1], blk)) + bias_ref[:, cols]
            if f_s is not None:
                f_s[rows, cols] = f
            ya_s[rows, cols] = u[rows, cols] * f
    return u, vhat, rstd, vn


def _conv31_forward(z_ref, zh_ref, first, cw_ref, cb_ref, ts, ext_b, cbs):
    a = z_ref[:, 2 * SGU_WIDTH:2 * SGU_WIDTH + CONV_WIDTH]
    g = z_ref[:, 2 * SGU_WIDTH + CONV_WIDTH:2 * SGU_WIDTH + 2 * CONV_WIDTH]
    ah = zh_ref[:, 2 * SGU_WIDTH:2 * SGU_WIDTH + CONV_WIDTH]
    gh = zh_ref[:, 2 * SGU_WIDTH + CONV_WIDTH:2 * SGU_WIDTH + 2 * CONV_WIDTH]
    ext_b[pl.ds(0, MIX_HALO), :] = jnp.where(first, 0.0, ah * jax.nn.sigmoid(gh))
    ext_b[pl.ds(MIX_HALO, ts), :] = a * jax.nn.sigmoid(g)
    for r in range(ts // CONV_ROWS):
        acc = jnp.broadcast_to(cb_ref[...], (CONV_ROWS, CONV_WIDTH))
        for k in range(CONV_K):
            acc = acc + cw_ref[k:k + 1, :] * ext_b[pl.ds(MIX_HALO - (CONV_K - 1) + k + r * CONV_ROWS, CONV_ROWS), :]
        cbs[pl.ds(r * CONV_ROWS, CONV_ROWS), :] = acc


def _pool_counts(i, ts):
    pos1 = (i * ts + 1 + lax.broadcasted_iota(jnp.int32, (ts, POOL_WIDTH), 0)).astype(F32)
    lane = lax.broadcasted_iota(jnp.int32, (ts, POOL_WIDTH), 1)
    gdim = POOL_WIDTH // len(POOL_WINDOWS)
    win = jnp.where(lane < gdim, float(POOL_WINDOWS[0]),
                    jnp.where(lane < 2 * gdim, float(POOL_WINDOWS[1]),
                              jnp.where(lane < 3 * gdim, float(POOL_WINDOWS[2]), float(POOL_WINDOWS[3]))))
    return jnp.minimum(pos1, win)


def _window_sums(ext, base, ts, sign):
    lane = lax.broadcasted_iota(jnp.int32, (ts, POOL_WIDTH), 1)
    gdim = POOL_WIDTH // len(POOL_WINDOWS)
    run = jnp.zeros((ts, POOL_WIDTH), F32)
    out = jnp.zeros((ts, POOL_WIDTH), F32)
    for m in range(POOL_WINDOWS[-1]):
        run = run + ext[pl.ds(base + sign * m, ts), :]
        for gi, win in enumerate(POOL_WINDOWS):
            if m == win - 1:
                out = jnp.where((lane >= gi * gdim) & (lane < (gi + 1) * gdim), run, out)
    return out


def _pool_forward(z_ref, zh_ref, first, i, ts, ext_c):
    c0 = 2 * SGU_WIDTH + 2 * CONV_WIDTH
    zc = z_ref[:, c0:c0 + POOL_WIDTH]
    ext_c[pl.ds(0, MIX_HALO), :] = jnp.where(first, 0.0, zh_ref[:, c0:c0 + POOL_WIDTH])
    ext_c[pl.ds(MIX_HALO, ts), :] = zc
    sums = _window_sums(ext_c, MIX_HALO, ts, -1)
    return sums / _pool_counts(i, ts) - zc


def _layer_norm_rows(v):
    mu = _rmean(v)
    vc = v - mu
    rstd = lax.rsqrt(_rmean(vc * vc) + EPS)
    return vc * rstd, rstd


def _mixer_specs(s, ts, width):
    nbh = ts // MIX_HALO
    tile = pl.BlockSpec((ts, width), lambda i: (i, 0))
    prev = pl.BlockSpec((MIX_HALO, width), lambda i: (jnp.maximum(i * nbh - 1, 0), 0))
    nxt = pl.BlockSpec((MIX_HALO, width), lambda i: (jnp.minimum((i + 1) * nbh, s // MIX_HALO - 1), 0))
    return tile, prev, nxt


def _mixer_fwd(z, mp, name):
    s, inw = z.shape
    d = SGU_WIDTH + CONV_WIDTH + POOL_WIDTH
    ts = _row_tile(s, 256)

    def body(z_ref, zh_ref, bd_ref, ng_ref, nb_ref, wm_ref, bias_ref, cw_ref, cb_ref, cng_ref, cnb_ref,
             pw_ref, ps_ref, bg_ref, y_ref, ya_s, ext_b, cbs, ext_c):
        i = pl.program_id(0)
        first = i == 0
        _sgu_forward(z_ref, bd_ref, ng_ref, nb_ref, wm_ref, bias_ref, ts, ya_s, None)
        ya = ya_s[...]
        ra = lax.rsqrt(_rmean(ya * ya) + EPS)
        y_ref[:, 0:SGU_WIDTH] = ((ya * ra) * bg_ref[:, 0:SGU_WIDTH]).astype(BF16)

        _conv31_forward(z_ref, zh_ref, first, cw_ref, cb_ref, ts, ext_b, cbs)
        chat, _ = _layer_norm_rows(cbs[...])
        lin = chat * cng_ref[...] + cnb_ref[...]
        yb = lin * jax.nn.sigmoid(lin)
        rb = lax.rsqrt(_rmean(yb * yb) + EPS)
        y_ref[:, SGU_WIDTH:SGU_WIDTH + CONV_WIDTH] = ((yb * rb) * bg_ref[:, SGU_WIDTH:SGU_WIDTH + CONV_WIDTH]).astype(BF16)

        dpool = _pool_forward(z_ref, zh_ref, first, i, ts, ext_c)
        yc = _dot(dpool.astype(BF16), pw_ref[...]) * ps_ref[...]
        rc = lax.rsqrt(_rmean(yc * yc) + EPS)
        y_ref[:, SGU_WIDTH + CONV_WIDTH:d] = ((yc * rc) * bg_ref[:, SGU_WIDTH + CONV_WIDTH:d]).astype(BF16)

    tile, prev, _ = _mixer_specs(s, ts, inw)
    consts = [mp["bd"], mp["ng"], mp["nb"], mp["wm"], mp["bias"], mp["cw"], mp["cb"], mp["cng"], mp["cnb"],
              mp["pw"], mp["ps"], mp["bg"]]
    return pl.pallas_call(
        body, name=name,
        grid=(s // ts,),
        in_specs=[tile, prev] + [_const_spec(c.shape) for c in consts],
        out_specs=pl.BlockSpec((ts, d), lambda i: (i, 0)),
        out_shape=jax.ShapeDtypeStruct((s, d), BF16),
        scratch_shapes=[pltpu.VMEM((ts, SGU_WIDTH), F32), pltpu.VMEM((ts + MIX_HALO, CONV_WIDTH), F32),
                        pltpu.VMEM((ts, CONV_WIDTH), F32), pltpu.VMEM((ts + MIX_HALO, POOL_WIDTH), F32)],
        compiler_params=pltpu.CompilerParams(dimension_semantics=("arbitrary",),
                                             vmem_limit_bytes=_vmem_limit(16 * ts * inw * 4)),
    )(z, z, *consts)


def _mixer_bwd_a(z, dy, mp, name):
    s, inw = z.shape
    d = SGU_WIDTH + CONV_WIDTH + POOL_WIDTH
    ts = _row_tile(s, 256)
    nchunk = ts // CHUNK

    def rms_bwd(dyn, y, g):
        r = lax.rsqrt(_rmean(y * y) + EPS)
        yn = y * r
        dg = _rsum(dyn * yn)
        t = dyn * g
        return r * (t - yn * _rmean(t * yn)), dg

    def body(z_ref, zh_ref, dy_ref, bd_ref, ng_ref, nb_ref, wm_ref, wmt_ref, bias_ref, cw_ref, cb_ref, cng_ref, cnb_ref,
             pw_ref, pwt_ref, ps_ref, bg_ref,
             dza_ref, dcb_ref, dd_ref, dbg_ref, dwm_ref, dbias_ref, dng_ref, dnb_ref, dcng_ref, dcnb_ref, dps_ref, dpw_ref,
             ya_s, f_s, dvn_s, ext_b, cbs, ext_c):
        i = pl.program_id(0)
        first = i == 0

        @pl.when(first)
        def _():
            for ref in (dwm_ref, dbias_ref, dng_ref, dnb_ref, dcng_ref, dcnb_ref, dps_ref, dpw_ref):
                ref[...] = jnp.zeros_like(ref)

        u, vhat, rstd, vn = _sgu_forward(z_ref, bd_ref, ng_ref, nb_ref, wm_ref, bias_ref, ts, ya_s, f_s)
        dya, dbg_a = rms_bwd(dy_ref[:, 0:SGU_WIDTH], ya_s[...], bg_ref[:, 0:SGU_WIDTH])
        du = dya * f_s[...]
        df = dya * u
        dfb = df.astype(BF16)
        left = _lane_lt((CHUNK, CHUNK), HEAD_DIM)
        zero = jnp.zeros((CHUNK, CHUNK), BF16)
        dbias = jnp.zeros((CHUNK, SGU_WIDTH), F32)
        for n in range(nchunk):
            rows = slice(n * CHUNK, (n + 1) * CHUNK)
            dbias = dbias + df[rows, :]
            for p in range(SGU_WIDTH // CHUNK):
                cols = slice(p * CHUNK, (p + 1) * CHUNK)
                dblk = dfb[rows, cols]
                vblk = vn[rows, cols]
                dwm_ref[2 * p] += _dot_nt(jnp.where(left, dblk, zero), vblk)
                dwm_ref[2 * p + 1] += _dot_nt(jnp.where(left, zero, dblk), vblk)
                dvn_s[rows, cols] = jnp.where(left, _dot(wmt_ref[2 * p], dblk), _dot(wmt_ref[2 * p + 1], dblk))
        dbias_ref[...] += dbias
        dvn = dvn_s[...]
        dng_ref[...] += _rsum(dvn * vhat)
        dnb_ref[...] += _rsum(dvn)
        dvh = dvn * ng_ref[...]
        bd = bd_ref[...]
        dv = rstd * (dvh - _group_mean(dvh, bd) - vhat * _group_mean(dvh * vhat, bd))
        dza_ref[:, 0:SGU_WIDTH] = (du * _gelu_grad(z_ref[:, 0:SGU_WIDTH])).astype(BF16)
        dza_ref[:, SGU_WIDTH:2 * SGU_WIDTH] = (dv * _gelu_grad(z_ref[:, SGU_WIDTH:2 * SGU_WIDTH])).astype(BF16)

        _conv31_forward(z_ref, zh_ref, first, cw_ref, cb_ref, ts, ext_b, cbs)
        chat, crstd = _layer_norm_rows(cbs[...])
        lin = chat * cng_ref[...] + cnb_ref[...]
        sl = jax.nn.sigmoid(lin)
        dyb, dbg_b = rms_bwd(dy_ref[:, SGU_WIDTH:SGU_WIDTH + CONV_WIDTH], lin * sl, bg_ref[:, SGU_WIDTH:SGU_WIDTH + CONV_WIDTH])
        dlin = dyb * (sl * (1.0 + lin * (1.0 - sl)))
        dcng_ref[...] += _rsum(dlin * chat)
        dcnb_ref[...] += _rsum(dlin)
        dch = dlin * cng_ref[...]
        dcb_ref[...] = crstd * (dch - _rmean(dch) - chat * _rmean(dch * chat))

        dpool = _pool_forward(z_ref, zh_ref, first, i, ts, ext_c)
        dpb = dpool.astype(BF16)
        ycp = _dot(dpb, pw_ref[...])
        dyc, dbg_c = rms_bwd(dy_ref[:, SGU_WIDTH + CONV_WIDTH:d], ycp * ps_ref[...], bg_ref[:, SGU_WIDTH + CONV_WIDTH:d])
        dps_ref[...] += _rsum(dyc * ycp)
        dycp = (dyc * ps_ref[...]).astype(BF16)
        dpw_ref[...] += _dot_tn(dpb, dycp)
        dd_ref[...] = _dot(dycp, pwt_ref[...])

        @pl.when(first)
        def _():
            dbg_ref[...] = jnp.zeros_like(dbg_ref)

        dbg_ref[:, 0:SGU_WIDTH] += dbg_a
        dbg_ref[:, SGU_WIDTH:SGU_WIDTH + CONV_WIDTH] += dbg_b
        dbg_ref[:, SGU_WIDTH + CONV_WIDTH:d] += dbg_c

    tile, prev, _ = _mixer_specs(s, ts, inw)
    consts = [mp["bd"], mp["ng"], mp["nb"], mp["wm"], mp["wmt"], mp["bias"], mp["cw"], mp["cb"], mp["cng"], mp["cnb"],
              mp["pw"], mp["pwt"], mp["ps"], mp["bg"]]
    acc_shapes = [(1, d), (2 * (SGU_WIDTH // CHUNK), CHUNK, CHUNK), (CHUNK, SGU_WIDTH), (1, SGU_WIDTH), (1, SGU_WIDTH),
                  (1, CONV_WIDTH), (1, CONV_WIDTH), (1, POOL_WIDTH), (POOL_WIDTH, POOL_WIDTH)]
    return pl.pallas_call(
        body, name=name,
        grid=(s // ts,),
        in_specs=[tile, prev, pl.BlockSpec((ts, d), lambda i: (i, 0))] + [_const_spec(c.shape) for c in consts],
        out_specs=[pl.BlockSpec((ts, 2 * SGU_WIDTH), lambda i: (i, 0)), pl.BlockSpec((ts, CONV_WIDTH), lambda i: (i, 0)),
                   pl.BlockSpec((ts, POOL_WIDTH), lambda i: (i, 0))] + [_const_spec(a) for a in acc_shapes],
        out_shape=[jax.ShapeDtypeStruct((s, 2 * SGU_WIDTH), BF16), jax.ShapeDtypeStruct((s, CONV_WIDTH), F32),
                   jax.ShapeDtypeStruct((s, POOL_WIDTH), F32)] + [jax.ShapeDtypeStruct(a, F32) for a in acc_shapes],
        scratch_shapes=[pltpu.VMEM((ts, SGU_WIDTH), F32), pltpu.VMEM((ts, SGU_WIDTH), F32), pltpu.VMEM((ts, SGU_WIDTH), F32),
                        pltpu.VMEM((ts + MIX_HALO, CONV_WIDTH), F32), pltpu.VMEM((ts, CONV_WIDTH), F32),
                        pltpu.VMEM((ts + MIX_HALO, POOL_WIDTH), F32)],
        compiler_params=pltpu.CompilerParams(dimension_semantics=("arbitrary",),
                                             vmem_limit_bytes=_vmem_limit(24 * ts * inw * 4)),
    )(z, z, dy, *consts)


def _mixer_bwd_b(z, dza, dcb, dd, x, dres, gain, s1p, cw, w, name):
    s, inw = z.shape
    d = x.shape[1]
    ts = _row_tile(s, 256)
    c0 = 2 * SGU_WIDTH
    c1 = c0 + 2 * CONV_WIDTH

    def body(z_ref, zh_ref, dza_ref, dcb_ref, dcbn_ref, dd_ref, ddn_ref, x_ref, dres_ref, g_ref, s_ref, cw_ref, w_ref,
             dx_ref, dz_ref, dsh_ref, dsc_ref, dg_ref, dcw_ref, dcbias_ref, ext_b, ext_n, ext_e):
        i = pl.program_id(0)
        first = i == 0
        last = i == pl.num_programs(0) - 1

        @pl.when(first)
        def _():
            for ref in (dsh_ref, dsc_ref, dg_ref, dcw_ref, dcbias_ref):
                ref[...] = jnp.zeros_like(ref)

        a = z_ref[:, c0:c0 + CONV_WIDTH]
        sg = jax.nn.sigmoid(z_ref[:, c0 + CONV_WIDTH:c1])
        ah = zh_ref[:, c0:c0 + CONV_WIDTH]
        gh = zh_ref[:, c0 + CONV_WIDTH:c1]
        ext_b[pl.ds(0, MIX_HALO), :] = jnp.where(first, 0.0, ah * jax.nn.sigmoid(gh))
        ext_b[pl.ds(MIX_HALO, ts), :] = a * sg
        dcbv = dcb_ref[...]
        dcbias_ref[...] += _rsum(dcbv)
        for k in range(CONV_K):
            dcw_ref[k:k + 1, :] += _rsum(dcbv * ext_b[pl.ds(MIX_HALO - (CONV_K - 1) + k, ts), :])

        ext_n[pl.ds(0, ts), :] = dcbv
        ext_n[pl.ds(ts, MIX_HALO), :] = jnp.where(last, 0.0, dcbn_ref[...])
        for r in range(ts // CONV_ROWS):
            acc = jnp.zeros((CONV_ROWS, CONV_WIDTH), F32)
            for k in range(CONV_K):
                acc = acc + cw_ref[k:k + 1, :] * ext_n[pl.ds(CONV_K - 1 - k + r * CONV_ROWS, CONV_ROWS), :]
            rows = pl.ds(r * CONV_ROWS, CONV_ROWS)
            ar = z_ref[rows, c0:c0 + CONV_WIDTH]
            sr = jax.nn.sigmoid(z_ref[rows, c0 + CONV_WIDTH:c1])
            dz_ref[rows, c0:c0 + CONV_WIDTH] = (acc * sr).astype(BF16)
            dz_ref[rows, c0 + CONV_WIDTH:c1] = (acc * ar * sr * (1.0 - sr)).astype(BF16)

        ddv = dd_ref[...]
        ext_e[pl.ds(0, ts), :] = ddv / _pool_counts(i, ts)
        nh = (i + 1) * ts + lax.broadcasted_iota(jnp.int32, (MIX_HALO, POOL_WIDTH), 0)
        lane = lax.broadcasted_iota(jnp.int32, (MIX_HALO, POOL_WIDTH), 1)
        gdim = POOL_WIDTH // len(POOL_WINDOWS)
        winh = jnp.where(lane < gdim, float(POOL_WINDOWS[0]),
                         jnp.where(lane < 2 * gdim, float(POOL_WINDOWS[1]),
                                   jnp.where(lane < 3 * gdim, float(POOL_WINDOWS[2]), float(POOL_WINDOWS[3]))))
        cnth = jnp.minimum((nh + 1).astype(F32), winh)
        ext_e[pl.ds(ts, MIX_HALO), :] = jnp.where(last, 0.0, ddn_ref[...] / cnth)
        dz_ref[:, c1:inw] = (_window_sums(ext_e, 0, ts, 1) - ddv).astype(BF16)
        dz_ref[:, 0:c0] = dza_ref[...]

        dh = _dot_nt(dz_ref[...], w_ref[...])
        dx, dsh, dsc, dg = _norm_mod_bwd(dh, x_ref[...], g_ref[...], s_ref[...], dres_ref[...])
        dx_ref[...] = dx
        dsh_ref[...] += dsh
        dsc_ref[...] += dsc
        dg_ref[...] += dg

    tile, prev, _ = _mixer_specs(s, ts, inw)
    _, _, nxt_b = _mixer_specs(s, ts, CONV_WIDTH)
    _, _, nxt_c = _mixer_specs(s, ts, POOL_WIDTH)
    row = pl.BlockSpec((ts, d), lambda i: (i, 0))
    vec = pl.BlockSpec((1, d), lambda i: (0, 0))
    return pl.pallas_call(
        body, name=name,
        grid=(s // ts,),
        in_specs=[tile, prev, pl.BlockSpec((ts, c0), lambda i: (i, 0)),
                  pl.BlockSpec((ts, CONV_WIDTH), lambda i: (i, 0)), nxt_b,
                  pl.BlockSpec((ts, POOL_WIDTH), lambda i: (i, 0)), nxt_c,
                  row, row, vec, vec, _const_spec(cw.shape),
                  pl.BlockSpec(w.shape, lambda i: (0, 0), pipeline_mode=pl.Buffered(1))],
        out_specs=[row, pl.BlockSpec((ts, inw), lambda i: (i, 0)), vec, vec, vec,
                   _const_spec((CONV_K, CONV_WIDTH)), _const_spec((1, CONV_WIDTH))],
        out_shape=[jax.ShapeDtypeStruct((s, d), F32), jax.ShapeDtypeStruct((s, inw), BF16)]
        + [jax.ShapeDtypeStruct((1, d), F32)] * 3
        + [jax.ShapeDtypeStruct((CONV_K, CONV_WIDTH), F32), jax.ShapeDtypeStruct((1, CONV_WIDTH), F32)],
        scratch_shapes=[pltpu.VMEM((ts + MIX_HALO, CONV_WIDTH), F32), pltpu.VMEM((ts + MIX_HALO, CONV_WIDTH), F32),
                        pltpu.VMEM((ts + MIX_HALO, POOL_WIDTH), F32)],
        compiler_params=pltpu.CompilerParams(dimension_semantics=("arbitrary",),
                                             vmem_limit_bytes=_vmem_limit(16 * ts * inw * 4 + inw * d * 2)),
    )(z, z, dza, dcb, dcb, dd, dd, x, dres, gain, s1p, cw, w)


def _ffn_specs(s, ts, tc, half_blocks):
    nbh = ts // FFN_HALO

    def tile(off):
        return pl.BlockSpec((ts, tc), lambda j, i: (i, j + off))

    def prev(off):
        return pl.BlockSpec((FFN_HALO, tc), lambda j, i: (jnp.maximum(i * nbh - 1, 0), j + off))

    def vec(rows, off):
        return pl.BlockSpec((rows, tc), lambda j, i: (0, j + off))

    return tile, prev, vec


def _conv3_tile(p_ref, ph_ref, first, w_ref, b_ref, ts, ext):
    ext[pl.ds(0, FFN_HALO), :] = jnp.where(first, 0.0, ph_ref[...])
    ext[pl.ds(FFN_HALO, ts), :] = p_ref[...]
    acc = b_ref[...] + w_ref[FFN_CONV_K - 1:FFN_CONV_K, :] * p_ref[...]
    for k in range(FFN_CONV_K - 1):
        acc = acc + w_ref[k:k + 1, :] * ext[pl.ds(FFN_HALO - (FFN_CONV_K - 1) + k, ts), :]
    return acc


def _ffn_act_fwd(p, cw, cb, name):
    s, f2 = p.shape
    f = f2 // 2
    tc = f // 2
    hb = f // tc
    ts = _row_tile(s, 256)

    def body(pg_ref, pgh_ref, pv_ref, pvh_ref, wg_ref, wv_ref, bg_ref, bv_ref, act_ref, ext_g, ext_v):
        first = pl.program_id(1) == 0
        ug = _conv3_tile(pg_ref, pgh_ref, first, wg_ref, bg_ref, ts, ext_g)
        uv = _conv3_tile(pv_ref, pvh_ref, first, wv_ref, bv_ref, ts, ext_v)
        act_ref[...] = (_gelu(ug) * uv).astype(BF16)

    tile, prev, vec = _ffn_specs(s, ts, tc, hb)
    return pl.pallas_call(
        body, name=name,
        grid=(hb, s // ts),
        in_specs=[tile(0), prev(0), tile(hb), prev(hb), vec(FFN_CONV_K, 0), vec(FFN_CONV_K, hb), vec(1, 0), vec(1, hb)],
        out_specs=pl.BlockSpec((ts, tc), lambda j, i: (i, j)),
        out_shape=jax.ShapeDtypeStruct((s, f), BF16),
        scratch_shapes=[pltpu.VMEM((ts + FFN_HALO, tc), F32)] * 2,
        compiler_params=pltpu.CompilerParams(dimension_semantics=("arbitrary", "arbitrary"),
                                             vmem_limit_bytes=_vmem_limit(16 * ts * tc * 4)),
    )(p, p, p, p, cw, cw, cb, cb)


def _ffn_act_bwd(p, dact, cw, cb, name):
    s, f2 = p.shape
    f = f2 // 2
    tc = f // 2
    hb = f // tc
    ts = _row_tile(s, 256)

    def body(pg_ref, pgh_ref, pv_ref, pvh_ref, da_ref, wg_ref, wv_ref, bg_ref, bv_ref,
             dug_ref, duv_ref, dwg_ref, dwv_ref, dbg_ref, dbv_ref, ext_g, ext_v):
        first = pl.program_id(1) == 0

        @pl.when(first)
        def _():
            for ref in (dwg_ref, dwv_ref, dbg_ref, dbv_ref):
                ref[...] = jnp.zeros_like(ref)

        ug = _conv3_tile(pg_ref, pgh_ref, first, wg_ref, bg_ref, ts, ext_g)
        uv = _conv3_tile(pv_ref, pvh_ref, first, wv_ref, bv_ref, ts, ext_v)
        da = da_ref[...]
        dug = da * uv * _gelu_grad(ug)
        duv = da * _gelu(ug)
        dug_ref[...] = dug
        duv_ref[...] = duv
        dbg_ref[...] += _rsum(dug)
        dbv_ref[...] += _rsum(duv)
        for k in range(FFN_CONV_K):
            off = FFN_HALO - (FFN_CONV_K - 1) + k
            dwg_ref[k:k + 1, :] += _rsum(dug * ext_g[pl.ds(off, ts), :])
            dwv_ref[k:k + 1, :] += _rsum(duv * ext_v[pl.ds(off, ts), :])

    tile, prev, vec = _ffn_specs(s, ts, tc, hb)
    half = pl.BlockSpec((ts, tc), lambda j, i: (i, j))
    wacc = pl.BlockSpec((FFN_CONV_K, tc), lambda j, i: (0, j))
    bacc = pl.BlockSpec((1, tc), lambda j, i: (0, j))
    return pl.pallas_call(
        body, name=name,
        grid=(hb, s // ts),
        in_specs=[tile(0), prev(0), tile(hb), prev(hb), half, vec(FFN_CONV_K, 0), vec(FFN_CONV_K, hb), vec(1, 0), vec(1, hb)],
        out_specs=[half, half, wacc, wacc, bacc, bacc],
        out_shape=[jax.ShapeDtypeStruct((s, f), F32)] * 2 + [jax.ShapeDtypeStruct((FFN_CONV_K, f), F32)] * 2
        + [jax.ShapeDtypeStruct((1, f), F32)] * 2,
        scratch_shapes=[pltpu.VMEM((ts + FFN_HALO, tc), F32)] * 2,
        compiler_params=pltpu.CompilerParams(dimension_semantics=("arbitrary", "arbitrary"),
                                             vmem_limit_bytes=_vmem_limit(24 * ts * tc * 4)),
    )(p, p, p, p, dact, cw, cw, cb, cb)


def _ffn_in_bwd(dug, duv, cw, w, x, dres, gain, s1p, name):
    s, f = dug.shape
    d = x.shape[1]
    ts = _row_tile(s, 256)
    tc = w.shape[2]
    assert f % tc == 0 and w.shape[0] * tc == 2 * f
    nbh = ts // FFN_HALO

    def body(dug_ref, dugn_ref, duv_ref, duvn_ref, cw_ref, w_ref, x_ref, dres_ref, g_ref, s_ref,
             dx_ref, dp_ref, dsh_ref, dsc_ref, dg_ref, ext):
        i = pl.program_id(0)
        last = i == pl.num_programs(0) - 1

        @pl.when(i == 0)
        def _():
            for ref in (dsh_ref, dsc_ref, dg_ref):
                ref[...] = jnp.zeros_like(ref)

        dh = jnp.zeros((ts, d), F32)
        for half, (t_ref, n_ref) in enumerate(((dug_ref, dugn_ref), (duv_ref, duvn_ref))):
            for cb in range(f // tc):
                cols = slice(cb * tc, (cb + 1) * tc)
                wcols = slice(half * f + cb * tc, half * f + (cb + 1) * tc)
                ext[pl.ds(0, ts), :] = t_ref[:, cols]
                ext[pl.ds(ts, FFN_HALO), :] = jnp.where(last, 0.0, n_ref[:, cols])
                acc = cw_ref[FFN_CONV_K - 1:FFN_CONV_K, wcols] * t_ref[:, cols]
                for k in range(FFN_CONV_K - 1):
                    acc = acc + cw_ref[k:k + 1, wcols] * ext[pl.ds(FFN_CONV_K - 1 - k, ts), :]
                dpb = acc.astype(BF16)
                dp_ref[:, wcols] = dpb
                dh = dh + _dot_nt(dpb, w_ref[half * (f // tc) + cb])
        dx, dsh, dsc, dg = _norm_mod_bwd(dh, x_ref[...], g_ref[...], s_ref[...], dres_ref[...])
        dx_ref[...] = dx
        dsh_ref[...] += dsh
        dsc_ref[...] += dsc
        dg_ref[...] += dg

    tile = pl.BlockSpec((ts, f), lambda i: (i, 0))
    nxt = pl.BlockSpec((FFN_HALO, f), lambda i: (jnp.minimum((i + 1) * nbh, s // FFN_HALO - 1), 0))
    row = pl.BlockSpec((ts, d), lambda i: (i, 0))
    vec = pl.BlockSpec((1, d), lambda i: (0, 0))
    return pl.pallas_call(
        body, name=name,
        grid=(s // ts,),
        in_specs=[tile, nxt, tile, nxt, _const_spec(cw.shape),
                  pl.BlockSpec(w.shape, lambda i: (0, 0, 0), pipeline_mode=pl.Buffered(1)), row, row, vec, vec],
        out_specs=[row, pl.BlockSpec((ts, 2 * f), lambda i: (i, 0)), vec, vec, vec],
        out_shape=[jax.ShapeDtypeStruct((s, d), F32), jax.ShapeDtypeStruct((s, 2 * f), BF16)] + [jax.ShapeDtypeStruct((1, d), F32)] * 3,
        scratch_shapes=[pltpu.VMEM((ts + FFN_HALO, tc), F32)],
        compiler_params=pltpu.CompilerParams(
            dimension_semantics=("arbitrary",),
            vmem_limit_bytes=_vmem_limit(4 * ts * f * 4 + 2 * f * d * 2 + 2 * ts * 2 * f * 2 + 12 * ts * d * 4 + 6 * ts * tc * 4)),
    )(dug, dug, duv, duv, cw, w, x, dres, gain, s1p)


def _adamw_math(w, g, m, v):
    m = ADAM_B1 * m + (1.0 - ADAM_B1) * g
    v = ADAM_B2 * v + (1.0 - ADAM_B2) * (g * g)
    m_hat = m / (1.0 - ADAM_B1 ** ADAM_STEP)
    v_hat = v / (1.0 - ADAM_B2 ** ADAM_STEP)
    delta = -ADAM_LR * (m_hat / (jnp.sqrt(v_hat) + ADAM_EPS) + ADAM_WD * w)
    return delta, m, v


def _adam_rows(rows, cols):
    want = max(8, (2 * 1024 * 1024 // (cols * 4)) // 8 * 8)
    tr = min(rows, want)
    while rows % tr:
        tr -= 8
    return tr


def _adamw(w, m, v, g_parts, name):
    shape = w.shape
    c = shape[-1]
    r = math.prod(shape[:-1])
    tr = _adam_rows(r, c)
    ng = len(g_parts)

    def body(*refs):
        w_ref, m_ref, v_ref = refs[0:3]
        g_refs = refs[3:3 + ng]
        g_out, d_out, m_out, v_out = refs[3 + ng:]
        g = g_refs[0][...]
        for gr in g_refs[1:]:
            g = g + gr[...]
        delta, mn, vn = _adamw_math(w_ref[...], g, m_ref[...], v_ref[...])
        g_out[...] = g
        d_out[...] = delta
        m_out[...] = mn
        v_out[...] = vn

    blk = pl.BlockSpec((tr, c), lambda i: (i, 0))
    outs = pl.pallas_call(
        body, name=name,
        grid=(r // tr,),
        in_specs=[blk] * (3 + ng),
        out_specs=[blk] * 4,
        out_shape=[jax.ShapeDtypeStruct((r, c), F32)] * 4,
        compiler_params=pltpu.CompilerParams(dimension_semantics=("arbitrary",),
                                             vmem_limit_bytes=_vmem_limit(2 * (7 + ng) * tr * max(c, 128) * 4 + (8 << 20))),
    )(w.reshape(r, c), m.reshape(r, c), v.reshape(r, c), *[g.reshape(r, c) for g in g_parts])
    return [o.reshape(shape) for o in outs]


def _modw_adamw(sct, dmod, w, m, v, name):
    nl, d, n = w.shape
    tr = _row_tile(d, 256)

    def body(sct_ref, dm_ref, w_ref, m_ref, v_ref, g_out, d_out, m_out, v_out):
        sc = sct_ref[...].astype(BF16).astype(F32)
        dm = dm_ref[...].astype(BF16).astype(F32)
        g = sc[:, 0:1] * dm[0:1, :]
        for b in range(1, N_DEV):
            g = g + sc[:, b:b + 1] * dm[b:b + 1, :]
        delta, mn, vn = _adamw_math(w_ref[...], g, m_ref[...], v_ref[...])
        g_out[...] = g
        d_out[...] = delta
        m_out[...] = mn
        v_out[...] = vn

    blk = pl.BlockSpec((None, tr, n), lambda l, i: (l, i, 0))
    return pl.pallas_call(
        body, name=name,
        grid=(nl, d // tr),
        in_specs=[pl.BlockSpec((tr, N_DEV), lambda l, i: (i, 0)), pl.BlockSpec((None, N_DEV, n), lambda l, i: (l, 0, 0)),
                  blk, blk, blk],
        out_specs=[blk] * 4,
        out_shape=[jax.ShapeDtypeStruct((nl, d, n), F32)] * 4,
        compiler_params=pltpu.CompilerParams(dimension_semantics=("arbitrary", "arbitrary"),
                                             vmem_limit_bytes=_vmem_limit(2 * 8 * tr * n * 4 + (8 << 20))),
    )(sct, dmod, w, m, v)


def _reduce4(recvs, name):
    nl = len(recvs)
    shape = recvs[0].shape[1:]
    c = shape[-1]
    r = math.prod(shape[:-1])
    tr = _adam_rows(r, c)
    nt = r // tr

    def body(*refs):
        o_ref = refs[nl]
        for l in range(nl):
            @pl.when(pl.program_id(0) == l)
            def _():
                acc = refs[l][0].astype(F32)
                for k in range(1, N_CHIPS):
                    acc = acc + refs[l][k].astype(F32)
                o_ref[...] = acc

    def in_map(l):
        return lambda ll, i: (0, jnp.where(ll < l, 0, jnp.where(ll > l, nt - 1, i)), 0)

    return pl.pallas_call(
        body, name=name,
        grid=(nl, nt),
        in_specs=[pl.BlockSpec((N_CHIPS, tr, c), in_map(l)) for l in range(nl)],
        out_specs=pl.BlockSpec((None, tr, c), lambda ll, i: (ll, i, 0)),
        out_shape=jax.ShapeDtypeStruct((nl, r, c), F32),
        compiler_params=pltpu.CompilerParams(dimension_semantics=("arbitrary", "arbitrary"),
                                             vmem_limit_bytes=_vmem_limit(2 * 8 * nl * tr * max(c, 128) * 4 + (8 << 20))),
    )(*[rv.reshape(N_CHIPS, r, c) for rv in recvs]).reshape((nl,) + shape)


def _my_place():
    return lax.axis_index("x"), lax.axis_index("y"), lax.axis_index("c")


def _chip_coords(j):
    return j // 2, j % 2


def _mod_forward(c, mod_w, mod_b4):
    nl, d, n = mod_w.shape
    kc = 256

    def body(c_ref, w_ref, b_ref, mod_ref, sc_ref, cbuf, stage, s1, r1, s2, r2):
        mx, my, mc = _my_place()
        me = 4 * mx + 2 * my + mc
        q = 2 * mx + my
        cv = c_ref[...]
        cbuf[me] = jnp.broadcast_to(cv * jax.nn.sigmoid(cv), (8, d))
        sends = []
        for t in range(N_DEV):
            tx, ty = _chip_coords(t // 2)
            cp = pltpu.make_async_remote_copy(src_ref=cbuf.at[me], dst_ref=cbuf.at[me], send_sem=s1.at[t], recv_sem=r1.at[me],
                                              device_id=(tx, ty, t % 2), device_id_type=MESH)

            @pl.when(t != me)
            def _():
                cp.start()

            sends.append((t, cp))
        for t in range(N_DEV):
            @pl.when(t != me)
            def _():
                pltpu.make_async_remote_copy(src_ref=cbuf.at[t], dst_ref=cbuf.at[t], send_sem=s1.at[t], recv_sem=r1.at[t],
                                             device_id=(mx, my, mc), device_id_type=MESH).wait_recv()
        for t, cp in sends:
            @pl.when(t != me)
            def _():
                cp.wait_send()

        row = lax.broadcasted_iota(jnp.int32, (8, d), 0)
        sc_all = jnp.zeros((8, d), F32)
        for t in range(N_DEV):
            sc_all = sc_all + jnp.where(row == t, cbuf[t], 0.0)
        sc_ref[...] = sc_all
        rown = lax.broadcasted_iota(jnp.int32, (8, n), 0)
        for l in range(nl):
            acc = jnp.zeros((8, n), F32)
            for k0 in range(0, d, kc):
                acc = acc + _dot(sc_all[:, k0:k0 + kc].astype(BF16), w_ref[l, k0:k0 + kc, :].astype(BF16))
            acc = acc + b_ref[l, q]
            for j in range(N_CHIPS):
                jx, jy = _chip_coords(j)
                bdest = 4 * jx + 2 * jy + mc
                rowv = jnp.sum(jnp.where(rown == bdest, acc, 0.0), axis=0, keepdims=True)
                stage[j, l] = jnp.broadcast_to(rowv, (8, n))
        sends2 = []
        for j in range(N_CHIPS):
            jx, jy = _chip_coords(j)
            cp = pltpu.make_async_remote_copy(src_ref=stage.at[j], dst_ref=mod_ref.at[:, q], send_sem=s2.at[j], recv_sem=r2.at[q],
                                              device_id=(jx, jy, mc), device_id_type=MESH)

            @pl.when(j != q)
            def _():
                cp.start()

            @pl.when(j == q)
            def _():
                for l in range(nl):
                    mod_ref[l, j] = stage[j, l]

            sends2.append((j, cp))
        for j in range(N_CHIPS):
            @pl.when(j != q)
            def _():
                pltpu.make_async_remote_copy(src_ref=stage.at[j], dst_ref=mod_ref.at[:, j], send_sem=s2.at[j], recv_sem=r2.at[j],
                                             device_id=(mx, my, mc), device_id_type=MESH).wait_recv()
        for j, cp in sends2:
            @pl.when(j != q)
            def _():
                cp.wait_send()

    vm = pl.BlockSpec(memory_space=pltpu.VMEM)
    return pl.pallas_call(
        body, name="mod_forward",
        in_specs=[vm, vm, vm],
        out_specs=[vm, vm],
        out_shape=[jax.ShapeDtypeStruct((nl, N_CHIPS, 8, n), F32), jax.ShapeDtypeStruct((8, d), F32)],
        scratch_shapes=[pltpu.VMEM((N_DEV, 8, d), F32), pltpu.VMEM((N_CHIPS, nl, 8, n), F32),
                        pltpu.SemaphoreType.DMA((N_DEV,)), pltpu.SemaphoreType.DMA((N_DEV,)),
                        pltpu.SemaphoreType.DMA((N_CHIPS,)), pltpu.SemaphoreType.DMA((N_CHIPS,))],
        compiler_params=pltpu.CompilerParams(vmem_limit_bytes=_vmem_limit(2 * nl * d * n * 4 + (8 << 20))),
    )(c, mod_w, mod_b4)


_HBM_SPEC = pl.BlockSpec(memory_space=pltpu.HBM)
_SEM_SPEC = pl.BlockSpec(memory_space=pltpu.SEMAPHORE)
_DATAFLOW = pltpu.SideEffectType.DATAFLOW_SIDE_EFFECTING


def _slot(ref, scatter, j):
    return ref.at[j] if scatter else ref


def _exchange_start(groups, scatter, name):
    flat = [a for g in groups for a in g]
    na = len(flat)
    ng = len(groups)
    sizes = [len(g) for g in groups]
    first = [sum(sizes[:g]) for g in range(ng)]
    where = [(g, k) for g in range(ng) for k in range(sizes[g])]
    lands = [lax.empty((N_CHIPS,) + (a.shape[1:] if scatter else a.shape), a.dtype) for a in flat]

    def body(*refs):
        ins, lnd = refs[:na], refs[na:2 * na]
        ssems, rsems = refs[2 * na:2 * na + ng], refs[2 * na + ng:2 * na + 2 * ng]
        token, lsem = refs[-2], refs[-1]
        mx, my, mc = _my_place()
        q = 2 * mx + my
        for j in range(N_CHIPS):
            jx, jy = _chip_coords(j)
            for a in range(na):
                g, k = where[a]

                @pl.when(j != q)
                def _():
                    pltpu.make_async_remote_copy(src_ref=_slot(ins[a], scatter, j), dst_ref=lnd[a].at[q],
                                                 send_sem=ssems[g].at[k * N_CHIPS + j], recv_sem=rsems[g].at[k * N_CHIPS + q],
                                                 device_id=(jx, jy, mc), device_id_type=MESH).start()
        local = [pltpu.make_async_copy(_slot(ins[a], scatter, q), lnd[a].at[q], lsem.at[a]) for a in range(na)]
        for cp in local:
            cp.start()
        token[...] = jnp.zeros_like(token)
        for cp in local:
            cp.wait()

    sem_shapes = [pltpu.SemaphoreType.DMA((n * N_CHIPS,)) for n in sizes]
    outs = pl.pallas_call(
        body, name=name,
        in_specs=[_HBM_SPEC] * (2 * na),
        out_specs=[_SEM_SPEC] * (2 * ng) + [_HBM_SPEC] * (2 * na) + [pl.BlockSpec(memory_space=pltpu.VMEM)],
        out_shape=sem_shapes + sem_shapes + [pltpu.HBM(a.shape, a.dtype) for a in flat + lands]
        + [jax.ShapeDtypeStruct((8, 128), F32)],
        scratch_shapes=[pltpu.SemaphoreType.DMA((na,))],
        input_output_aliases={i: 2 * ng + i for i in range(2 * na)},
        compiler_params=pltpu.CompilerParams(has_side_effects=_DATAFLOW),
    )(*[pltpu.with_memory_space_constraint(a, pltpu.HBM) for a in flat + lands])
    ssems, rsems = outs[:ng], outs[ng:2 * ng]
    src_thru, land_thru = outs[2 * ng:2 * ng + na], outs[2 * ng + na:2 * ng + 2 * na]
    states = [(src_thru[first[g]:first[g] + sizes[g]], land_thru[first[g]:first[g] + sizes[g]], ssems[g], rsems[g])
              for g in range(ng)]
    return states, outs[-1]


def _exchange_wait(state, scatter, after, name):
    srcs, lands, ssem, rsem = state
    na = len(srcs)

    def body(*refs):
        ins, lnd = refs[:na], refs[na:2 * na]
        ssem_ref, rsem_ref = refs[2 * na], refs[2 * na + 1]
        mx, my, mc = _my_place()
        q = 2 * mx + my
        for j in range(N_CHIPS):
            for a in range(na):
                @pl.when(j != q)
                def _():
                    cp = pltpu.make_async_remote_copy(src_ref=_slot(ins[a], scatter, j), dst_ref=lnd[a].at[j],
                                                      send_sem=ssem_ref.at[a * N_CHIPS + j], recv_sem=rsem_ref.at[a * N_CHIPS + j],
                                                      device_id=(mx, my, mc), device_id_type=MESH)
                    cp.wait_send()
                    cp.wait_recv()

    outs = pl.pallas_call(
        body, name=name,
        in_specs=[_HBM_SPEC] * (2 * na) + [_SEM_SPEC, _SEM_SPEC, pl.BlockSpec(memory_space=pl.ANY)],
        out_specs=[_HBM_SPEC] * (2 * na),
        out_shape=[pltpu.HBM(a.shape, a.dtype) for a in list(srcs) + list(lands)],
        input_output_aliases={i: i for i in range(2 * na)},
        compiler_params=pltpu.CompilerParams(has_side_effects=_DATAFLOW),
    )(*srcs, *lands, ssem, rsem, after)
    return outs[na:]


def _swap_with_sibling(arrs):
    na = len(arrs)

    def body(*refs):
        ins, outs = refs[:na], refs[na:2 * na]
        ssem, rsem = refs[2 * na:]
        mx, my, mc = _my_place()
        cps = [pltpu.make_async_remote_copy(src_ref=ins[a], dst_ref=outs[a], send_sem=ssem.at[a], recv_sem=rsem.at[a],
                                            device_id=(mx, my, 1 - mc), device_id_type=MESH) for a in range(na)]
        for cp in cps:
            cp.start()
        for cp in cps:
            cp.wait()

    hbm = pl.BlockSpec(memory_space=pl.ANY)
    return pl.pallas_call(
        body, name="swap_sibling",
        in_specs=[hbm] * na,
        out_specs=[hbm] * na,
        out_shape=[jax.ShapeDtypeStruct(a.shape, a.dtype) for a in arrs],
        scratch_shapes=[pltpu.SemaphoreType.DMA((na,)), pltpu.SemaphoreType.DMA((na,))],
    )(*arrs)


def _allgather_sum(packed):
    r, c = packed.shape

    def body(p_ref, all_ref, sum_ref, ssem, rsem):
        mx, my, mc = _my_place()
        me = 4 * mx + 2 * my + mc
        all_ref[me] = p_ref[...]
        sends = []
        for t in range(N_DEV):
            tx, ty = _chip_coords(t // 2)
            cp = pltpu.make_async_remote_copy(src_ref=p_ref, dst_ref=all_ref.at[me], send_sem=ssem.at[t], recv_sem=rsem.at[me],
                                              device_id=(tx, ty, t % 2), device_id_type=MESH)

            @pl.when(t != me)
            def _():
                cp.start()

            sends.append((t, cp))
        for t in range(N_DEV):
            @pl.when(t != me)
            def _():
                pltpu.make_async_remote_copy(src_ref=p_ref, dst_ref=all_ref.at[t], send_sem=ssem.at[t], recv_sem=rsem.at[t],
                                             device_id=(mx, my, mc), device_id_type=MESH).wait_recv()
        for t, cp in sends:
            @pl.when(t != me)
            def _():
                cp.wait_send()
        acc = all_ref[0]
        for t in range(1, N_DEV):
            acc = acc + all_ref[t]
        sum_ref[...] = acc

    vm = pl.BlockSpec(memory_space=pltpu.VMEM)
    return pl.pallas_call(
        body, name="allgather_small",
        in_specs=[vm],
        out_specs=[vm, vm],
        out_shape=[jax.ShapeDtypeStruct((N_DEV, r, c), F32), jax.ShapeDtypeStruct((r, c), F32)],
        scratch_shapes=[pltpu.SemaphoreType.DMA((N_DEV,)), pltpu.SemaphoreType.DMA((N_DEV,))],
        compiler_params=pltpu.CompilerParams(vmem_limit_bytes=_vmem_limit((N_DEV + 4) * r * c * 4 + (4 << 20))),
    )(packed)


def _pack(arrs):
    rows, layout, at = [], [], 0
    for a in arrs:
        n = a.size
        nr = -(-n // SMALL_COLS)
        flat = a.reshape(-1)
        if nr * SMALL_COLS != n:
            flat = jnp.concatenate([flat, jnp.zeros((nr * SMALL_COLS - n,), F32)])
        rows.append(flat.reshape(nr, SMALL_COLS))
        layout.append((at, nr, a.shape))
        at += nr
    pad = -at % 8
    if pad:
        rows.append(jnp.zeros((pad, SMALL_COLS), F32))
    return jnp.concatenate(rows, axis=0), layout


def _unpack(buf, layout):
    out = []
    for at, nr, shape in layout:
        n = math.prod(shape)
        out.append(buf[at:at + nr].reshape(-1)[:n].reshape(shape))
    return out


SMALL_NAMES = ("mod_b", "mix_pre_g", "mix_post_g", "sgu_norm_g", "sgu_norm_b", "sgu_w", "sgu_b", "conv_b", "conv_norm_g",
               "conv_norm_b", "pool_w", "pool_scale", "branch_g", "ffn_pre_g", "ffn_post_g", "ffn_conv_b")
SHARDED_SMALL = ("conv_w", "ffn_conv_w")
WEIGHT_ORDER = ("mod_w", "mod_b", "mix_pre_g", "mix_post_g", "w_in", "sgu_norm_g", "sgu_norm_b", "sgu_w", "sgu_b", "conv_w",
                "conv_b", "conv_norm_g", "conv_norm_b", "pool_w", "pool_scale", "branch_g", "w_out", "ffn_pre_g", "ffn_post_g",
                "ffn_up", "ffn_conv_w", "ffn_conv_b", "ffn_down")


def _block_diag(blocks):
    n, a, b = blocks.shape
    eye = jnp.eye(n, dtype=blocks.dtype)
    return (eye[:, None, :, None] * blocks[:, :, None, :]).reshape(n * a, n * b)


def _diag_blocks(mat, n):
    a = mat.shape[0] // n
    return jnp.stack([mat[g * a:(g + 1) * a, g * a:(g + 1) * a] for g in range(n)])


def _step(x, c, loss_target, w, m, v):
    nl = w["mod_w"].shape[0]
    s, d = x.shape[1], x.shape[2]
    heads = SGU_WIDTH // HEAD_DIM
    groups = len(POOL_WINDOWS)
    mx, my, _ = _my_place()
    q = 2 * mx + my
    x0 = x.reshape(s, d)
    tgt = loss_target.reshape(s, d)

    nmod = w["mod_w"].shape[2]
    kin = w["w_in"].shape[2]
    inw = kin * N_CHIPS
    f2 = w["ffn_up"].shape[2] * N_CHIPS
    f = f2 // 2
    wgroups = []
    for l in range(nl):
        wgroups += [[w["w_in"][l].astype(BF16), w["conv_w"][l], w["ffn_conv_w"][l]], [w["w_out"][l].astype(BF16)],
                    [w["ffn_up"][l].astype(BF16)], [w["ffn_down"][l].astype(BF16)]]
    gstates, gtoken = _exchange_start(wgroups, False, "gather_start")
    mod4, sc_all = _mod_forward(c + gtoken[0:1, 0:1], w["mod_w"], w["mod_b"].reshape(nl, N_CHIPS, 1, nmod))
    mod = mod4[:, :, 0, :].reshape(nl, N_MOD, 1, d)

    tril = jnp.tril(jnp.ones((CHUNK, CHUNK), bool))
    bd = _block_diag(jnp.ones((heads, HEAD_DIM, HEAD_DIM), BF16))

    def mixer_params(l, conv_w):
        wm = jnp.where(tril[None], w["sgu_w"][l], 0.0)
        pw = _block_diag(w["pool_w"][l])
        return dict(
            bd=bd, ng=w["sgu_norm_g"][l][None], nb=w["sgu_norm_b"][l][None],
            wm=wm.astype(BF16), wmt=jnp.swapaxes(wm, 1, 2).astype(BF16),
            bias=jnp.repeat(w["sgu_b"][l].T, HEAD_DIM, axis=1),
            cw=conv_w, cb=w["conv_b"][l][None], cng=w["conv_norm_g"][l][None], cnb=w["conv_norm_b"][l][None],
            pw=pw.astype(BF16), pwt=pw.T.astype(BF16), ps=w["pool_scale"][l][None], bg=w["branch_g"][l][None])

    saved = []
    xl = x0
    prev_done = mod4
    for l in range(nl):
        sh1, sc1, g1, sh2, sc2, g2 = [mod[l, k] for k in range(N_MOD)]
        gpre1, gpost1 = w["mix_pre_g"][l][None], w["mix_post_g"][l][None]
        gpre2, gpost2 = w["ffn_pre_g"][l][None], w["ffn_post_g"][l][None]
        g_win, g_cw, g_fcw = _exchange_wait(gstates[4 * l], False, prev_done, f"gather_wait_in_{l}")
        w_in = jnp.transpose(g_win, (1, 0, 2)).reshape(d, inw)
        conv_w = jnp.transpose(g_cw, (1, 0, 2)).reshape(CONV_K, CONV_WIDTH)
        ffn_cw = jnp.transpose(g_fcw, (1, 0, 2)).reshape(FFN_CONV_K, f2)
        mp = mixer_params(l, conv_w)
        z, h1 = _norm_mod_matmul(xl, gpre1, 1.0 + sc1, sh1, w_in[None], f"mix_in_{l}")
        (g_wout,) = _exchange_wait(gstates[4 * l + 1], False, z, f"gather_wait_out_{l}")
        w_out = g_wout.reshape(d, d)
        ycat = _mixer_fwd(z, mp, f"mixer_fwd_{l}")
        (up,) = _exchange_wait(gstates[4 * l + 2], False, ycat, f"gather_wait_up_{l}")
        o, x1 = _matmul_norm_resid(ycat, w_out, xl, g1, gpost1, f"mix_out_{l}")
        (g_down,) = _exchange_wait(gstates[4 * l + 3], False, x1, f"gather_wait_down_{l}")
        down = g_down.reshape(f, d)
        p, h2 = _norm_mod_matmul(x1, gpre2, 1.0 + sc2, sh2, up, f"ffn_in_{l}")
        fcb = w["ffn_conv_b"][l][None]
        act = _ffn_act_fwd(p, ffn_cw, fcb, f"ffn_act_{l}")
        qo, x2 = _matmul_norm_resid(act, down, x1, g2, gpost2, f"ffn_out_{l}")
        saved.append(dict(x=xl, z=z, h1=h1, ycat=ycat, o=o, x1=x1, p=p, h2=h2, act=act, qo=qo, mp=mp, fcb=fcb,
                          w_in=w_in, w_out=w_out, up=up, down=down, ffn_cw=ffn_cw,
                          mods=(sh1, sc1, g1, sh2, sc2, g2), gains=(gpre1, gpost1, gpre2, gpost2)))
        xl = x2
        prev_done = p

    dx, loss_row = _loss_head(xl, tgt)

    small = {n: [None] * nl for n in SMALL_NAMES + SHARDED_SMALL}
    dmods = [None] * nl
    tn = f2 // N_CHIPS
    sstates = {}
    token = None
    for l in reversed(range(nl)):
        sv = saved[l]
        sh1, sc1, g1, sh2, sc2, g2 = sv["mods"]
        gpre1, gpost1, gpre2, gpost2 = sv["gains"]
        if token is not None:
            g2 = g2 + token[0:1, 0:1]
        dq, dact, dg2, dgpost2 = _resid_bwd_matmul(dx, sv["qo"], g2, gpost2, sv["down"], f"ffn_out_bwd_{l}")
        g_down = _wgrad(sv["act"], dq, (f, lambda j: 0), (d, lambda j: 0), jax.ShapeDtypeStruct((f, d), BF16),
                        (1, lambda j: (0, 0)), (f, d), f"wgrad_ffn_down_{l}")
        dug, duv, dfwg, dfwv, dfbg, dfbv = _ffn_act_bwd(sv["p"], dact, sv["ffn_cw"], sv["fcb"], f"ffn_act_bwd_{l}")
        dx1, dp, dsh2, dsc2, dgpre2 = _ffn_in_bwd(dug, duv, sv["ffn_cw"], sv["up"], sv["x1"], dx, gpre2, 1.0 + sc2,
                                                  f"ffn_in_bwd_{l}")
        g_up = _wgrad(sv["h2"], dp, (d, lambda j: 0), (tn, lambda j: j), jax.ShapeDtypeStruct((N_CHIPS, d, tn), BF16),
                      (N_CHIPS, lambda j: (j, 0, 0)), (None, d, tn), f"wgrad_ffn_up_{l}")
        (sstates[l, 0],), token = _exchange_start([[g_down.reshape(N_CHIPS, f // N_CHIPS, d), g_up]], True,
                                                  f"scatter_start_ffn_{l}")
        do, dycat, dg1, dgpost1 = _resid_bwd_matmul(dx1, sv["o"], g1 + token[0:1, 0:1], gpost1, sv["w_out"],
                                                    f"mix_out_bwd_{l}")
        g_out = _wgrad(sv["ycat"], do, (d, lambda j: 0), (d, lambda j: 0), jax.ShapeDtypeStruct((d, d), BF16),
                       (1, lambda j: (0, 0)), (d, d), f"wgrad_w_out_{l}")
        (dza, dcb, dd, dbg, dwm, dbias, dng, dnb, dcng, dcnb, dps, dpw) = _mixer_bwd_a(sv["z"], dycat, sv["mp"], f"mixer_bwd_a_{l}")
        dx, dz, dsh1, dsc1, dgpre1, dcw, dcbias = _mixer_bwd_b(
            sv["z"], dza, dcb, dd, sv["x"], dx1, gpre1, 1.0 + sc1, sv["mp"]["cw"], sv["w_in"], f"mixer_bwd_b_{l}")
        g_in = _wgrad(sv["h1"], dz, (d, lambda j: 0), (inw, lambda j: 0), jax.ShapeDtypeStruct((d, inw), BF16),
                      (1, lambda j: (0, 0)), (d, inw), f"wgrad_w_in_{l}")
        (sstates[l, 1],), token = _exchange_start(
            [[g_out.reshape(N_CHIPS, d // N_CHIPS, d), jnp.transpose(g_in.reshape(d, N_CHIPS, kin), (1, 0, 2))]], True,
            f"scatter_start_mix_{l}")

        dmods[l] = jnp.concatenate([dsh1, dsc1, dg1, dsh2, dsc2, dg2], axis=0)
        small["mix_pre_g"][l], small["mix_post_g"][l] = dgpre1[0], dgpost1[0]
        small["ffn_pre_g"][l], small["ffn_post_g"][l] = dgpre2[0], dgpost2[0]
        small["sgu_norm_g"][l], small["sgu_norm_b"][l] = dng[0], dnb[0]
        small["sgu_w"][l] = jnp.where(tril[None], dwm, 0.0)
        small["sgu_b"][l] = dbias.reshape(CHUNK, heads, HEAD_DIM).sum(-1).T
        small["conv_b"][l], small["conv_norm_g"][l], small["conv_norm_b"][l] = dcbias[0], dcng[0], dcnb[0]
        small["pool_w"][l], small["pool_scale"][l], small["branch_g"][l] = _diag_blocks(dpw, groups), dps[0], dbg[0]
        small["ffn_conv_b"][l] = jnp.concatenate([dfbg[0], dfbv[0]])
        small["conv_w"][l] = dcw
        small["ffn_conv_w"][l] = jnp.concatenate([dfwg, dfwv], axis=1)

    names = [n for n in SMALL_NAMES if n != "mod_b"] + list(SHARDED_SMALL)
    packed, layout = _pack([jnp.stack(dmods), loss_row + token[0:1, 0:1]] + [jnp.stack(small[n]) for n in names])
    gathered, summed = _allgather_sum(packed)
    parts = _unpack(summed, layout)
    loss = parts[1][0, 0]
    gsmall = dict(zip(names, parts[2:]))
    gsmall["mod_b"] = parts[0].reshape(nl, N_MOD * d)
    at0, nr0, _ = layout[0]
    dmod_all = gathered[:, at0:at0 + nr0].reshape(N_DEV, nl, N_MOD * d)
    dmod_mine = jnp.transpose(lax.dynamic_slice_in_dim(dmod_all, q * nmod, nmod, axis=2), (1, 0, 2))

    grads, deltas, new_m, new_v = {}, {}, {}, {}

    def put(name, res):
        grads[name], deltas[name], new_m[name], new_v[name] = res

    put("mod_w", _modw_adamw(sc_all.T, dmod_mine, w["mod_w"], m["mod_w"], v["mod_w"], "adamw_mod_w"))

    pw_, lay = _pack([w[n] for n in SMALL_NAMES])
    pm_, _ = _pack([m[n] for n in SMALL_NAMES])
    pv_, _ = _pack([v[n] for n in SMALL_NAMES])
    pg_, _ = _pack([gsmall[n] for n in SMALL_NAMES])
    res = _adamw(pw_, pm_, pv_, [pg_], "adamw_small")
    for n, g_, d_, m_, v_ in zip(SMALL_NAMES, *[_unpack(r_, lay) for r_ in res]):
        put(n, (g_, d_, m_, v_))

    gsh = {"conv_w": lax.dynamic_slice_in_dim(gsmall["conv_w"], q * (CONV_WIDTH // N_CHIPS), CONV_WIDTH // N_CHIPS, axis=2),
           "ffn_conv_w": lax.dynamic_slice_in_dim(gsmall["ffn_conv_w"], q * (f2 // N_CHIPS), f2 // N_CHIPS, axis=2)}
    pw_, lay = _pack([w[n] for n in SHARDED_SMALL])
    pm_, _ = _pack([m[n] for n in SHARDED_SMALL])
    pv_, _ = _pack([v[n] for n in SHARDED_SMALL])
    pg_, _ = _pack([gsh[n] for n in SHARDED_SMALL])
    res = _adamw(pw_, pm_, pv_, [pg_], "adamw_sharded_small")
    for n, g_, d_, m_, v_ in zip(SHARDED_SMALL, *[_unpack(r_, lay) for r_ in res]):
        put(n, (g_, d_, m_, v_))

    recv = dict(w_in=[None] * nl, w_out=[None] * nl, ffn_up=[None] * nl, ffn_down=[None] * nl)
    done = grads["mod_w"]
    for l in reversed(range(nl)):
        recv["ffn_down"][l], recv["ffn_up"][l] = _exchange_wait(sstates[l, 0], True, done, f"scatter_wait_ffn_{l}")
        recv["w_out"][l], recv["w_in"][l] = _exchange_wait(sstates[l, 1], True, recv["ffn_up"][l], f"scatter_wait_mix_{l}")
        done = recv["w_in"][l]
    big = ("w_in", "w_out", "ffn_up", "ffn_down")
    mine = [_reduce4(recv[n], f"reduce4_{n}") for n in big]
    theirs = _swap_with_sibling(mine)
    for n, a, b in zip(big, mine, theirs):
        put(n, _adamw(w[n], m[n], v[n], [a, b], f"adamw_{n}"))

    return (loss, dx.reshape(1, s, d), *[grads[n] for n in WEIGHT_ORDER], *[deltas[n] for n in WEIGHT_ORDER],
            *[new_m[n] for n in WEIGHT_ORDER], *[new_v[n] for n in WEIGHT_ORDER])


def kernel(x, c, mod_w, mod_b, mix_pre_g, mix_post_g, w_in, sgu_norm_g, sgu_norm_b, sgu_w, sgu_b, conv_w, conv_b, conv_norm_g, conv_norm_b, pool_w, pool_scale, branch_g, w_out, ffn_pre_g, ffn_post_g, ffn_up, ffn_conv_w, ffn_conv_b, ffn_down, loss_target, m_mod_w, m_mod_b, m_mix_pre_g, m_mix_post_g, m_w_in, m_sgu_norm_g, m_sgu_norm_b, m_sgu_w, m_sgu_b, m_conv_w, m_conv_b, m_conv_norm_g, m_conv_norm_b, m_pool_w, m_pool_scale, m_branch_g, m_w_out, m_ffn_pre_g, m_ffn_post_g, m_ffn_up, m_ffn_conv_w, m_ffn_conv_b, m_ffn_down, v_mod_w, v_mod_b, v_mix_pre_g, v_mix_post_g, v_w_in, v_sgu_norm_g, v_sgu_norm_b, v_sgu_w, v_sgu_b, v_conv_w, v_conv_b, v_conv_norm_g, v_conv_norm_b, v_pool_w, v_pool_scale, v_branch_g, v_w_out, v_ffn_pre_g, v_ffn_post_g, v_ffn_up, v_ffn_conv_w, v_ffn_conv_b, v_ffn_down):
    w = dict(mod_w=mod_w, mod_b=mod_b, mix_pre_g=mix_pre_g, mix_post_g=mix_post_g, w_in=w_in, sgu_norm_g=sgu_norm_g,
             sgu_norm_b=sgu_norm_b, sgu_w=sgu_w, sgu_b=sgu_b, conv_w=conv_w, conv_b=conv_b, conv_norm_g=conv_norm_g,
             conv_norm_b=conv_norm_b, pool_w=pool_w, pool_scale=pool_scale, branch_g=branch_g, w_out=w_out,
             ffn_pre_g=ffn_pre_g, ffn_post_g=ffn_post_g, ffn_up=ffn_up, ffn_conv_w=ffn_conv_w, ffn_conv_b=ffn_conv_b,
             ffn_down=ffn_down)
    m = dict(mod_w=m_mod_w, mod_b=m_mod_b, mix_pre_g=m_mix_pre_g, mix_post_g=m_mix_post_g, w_in=m_w_in,
             sgu_norm_g=m_sgu_norm_g, sgu_norm_b=m_sgu_norm_b, sgu_w=m_sgu_w, sgu_b=m_sgu_b, conv_w=m_conv_w,
             conv_b=m_conv_b, conv_norm_g=m_conv_norm_g, conv_norm_b=m_conv_norm_b, pool_w=m_pool_w,
             pool_scale=m_pool_scale, branch_g=m_branch_g, w_out=m_w_out, ffn_pre_g=m_ffn_pre_g, ffn_post_g=m_ffn_post_g,
             ffn_up=m_ffn_up, ffn_conv_w=m_ffn_conv_w, ffn_conv_b=m_ffn_conv_b, ffn_down=m_ffn_down)
    v = dict(mod_w=v_mod_w, mod_b=v_mod_b, mix_pre_g=v_mix_pre_g, mix_post_g=v_mix_post_g, w_in=v_w_in,
             sgu_norm_g=v_sgu_norm_g, sgu_norm_b=v_sgu_norm_b, sgu_w=v_sgu_w, sgu_b=v_sgu_b, conv_w=v_conv_w,
             conv_b=v_conv_b, conv_norm_g=v_conv_norm_g, conv_norm_b=v_conv_norm_b, pool_w=v_pool_w,
             pool_scale=v_pool_scale, branch_g=v_branch_g, w_out=v_w_out, ffn_pre_g=v_ffn_pre_g, ffn_post_g=v_ffn_post_g,
             ffn_up=v_ffn_up, ffn_conv_w=v_ffn_conv_w, ffn_conv_b=v_ffn_conv_b, ffn_down=v_ffn_down)
    return _step(x, c, loss_target, w, m, v)
```

```python
import functools
import math

import jax
import jax.numpy as jnp
from jax import lax
from jax.experimental import pallas as pl
from jax.experimental.pallas import tpu as pltpu

F32 = jnp.float32
BF16 = jnp.bfloat16
MESH = pl.DeviceIdType.MESH

EPS = 1e-6
HEAD_DIM = 64
CHUNK = 128
SGU_WIDTH = 384
CONV_WIDTH = 384
POOL_WIDTH = 256
POOL_WINDOWS = (2, 4, 8, 16)
CONV_K = 31
FFN_CONV_K = 3
N_MOD = 6
N_CHIPS = 4
N_DEV = 8

ADAM_LR = 0.001
ADAM_B1 = 0.9
ADAM_B2 = 0.999
ADAM_EPS = 1e-08
ADAM_WD = 0.01
ADAM_STEP = 10

MIX_HALO = 32
FFN_HALO = 8
CONV_ROWS = 32
SMALL_COLS = 1024
VMEM_BYTES_V7X = 64 * 1024 * 1024


def _vmem_limit(estimate_bytes):
    return int(min(max(estimate_bytes, 16 * 1024 * 1024), VMEM_BYTES_V7X - 8 * 1024 * 1024))


def _row_tile(s, want):
    return want if s % want == 0 else math.gcd(s, want)


def _rsum(v):
    return jnp.sum(v, axis=0, keepdims=True)


def _rmean(v):
    return jnp.mean(v, axis=-1, keepdims=True)


def _gelu(v):
    k = math.sqrt(2.0 / math.pi)
    return 0.5 * v * (1.0 + jnp.tanh(k * (v + 0.044715 * v * v * v)))


def _gelu_grad(v):
    k = math.sqrt(2.0 / math.pi)
    t = jnp.tanh(k * (v + 0.044715 * v * v * v))
    return 0.5 * (1.0 + t) + 0.5 * v * (1.0 - t * t) * (k * (1.0 + 3.0 * 0.044715 * v * v))


def _dot(a, b):
    return jnp.dot(a, b, preferred_element_type=F32)


def _dot_nt(a, b):
    return lax.dot_general(a, b, (((1,), (1,)), ((), ())), preferred_element_type=F32)


def _dot_tn(a, b):
    return lax.dot_general(a, b, (((0,), (0,)), ((), ())), preferred_element_type=F32)


def _group_mean(v, bd):
    hi = v.astype(BF16)
    lo = (v - hi.astype(F32)).astype(BF16)
    return (_dot(hi, bd) + _dot(lo, bd)) * (1.0 / HEAD_DIM)


def _const_spec(shape):
    nd = len(shape)
    return pl.BlockSpec(shape, lambda *_: (0,) * nd)


def _norm_mod_matmul(x, gain, s1p, shift, w, name):
    s, d = x.shape
    nb, _, tn = w.shape
    ts = _row_tile(s, 512 if nb * tn <= 2048 else 256)

    def body(x_ref, g_ref, s_ref, b_ref, w_ref, z_ref, h_ref):
        xv = x_ref[...]
        r = lax.rsqrt(_rmean(xv * xv) + EPS)
        h = ((xv * r) * g_ref[...] * s_ref[...] + b_ref[...]).astype(BF16)
        h_ref[...] = h
        for j in range(nb):
            z_ref[:, j * tn:(j + 1) * tn] = _dot(h, w_ref[j])

    vec = pl.BlockSpec((1, d), lambda i: (0, 0))
    return pl.pallas_call(
        body, name=name,
        grid=(s // ts,),
        in_specs=[pl.BlockSpec((ts, d), lambda i: (i, 0)), vec, vec, vec,
                  pl.BlockSpec((nb, d, tn), lambda i: (0, 0, 0), pipeline_mode=pl.Buffered(1))],
        out_specs=[pl.BlockSpec((ts, nb * tn), lambda i: (i, 0)), pl.BlockSpec((ts, d), lambda i: (i, 0))],
        out_shape=[jax.ShapeDtypeStruct((s, nb * tn), F32), jax.ShapeDtypeStruct((s, d), BF16)],
        compiler_params=pltpu.CompilerParams(
            dimension_semantics=("arbitrary",),
            vmem_limit_bytes=_vmem_limit(2 * (ts * d * 4 + ts * nb * tn * 4 + ts * d * 2) + nb * d * tn * 2 + 4 * ts * d * 4)),
    )(x, gain, s1p, shift, w)


def _matmul_norm_resid(a, w, xres, gate, gpost, name):
    s, k = a.shape
    d = w.shape[1]
    ts = _row_tile(s, 512)

    def body(a_ref, w_ref, x_ref, gate_ref, gp_ref, o_ref, xn_ref):
        o = _dot(a_ref[...], w_ref[...])
        o_ref[...] = o
        r = lax.rsqrt(_rmean(o * o) + EPS)
        xn_ref[...] = x_ref[...] + gate_ref[...] * ((o * r) * gp_ref[...])

    vec = pl.BlockSpec((1, d), lambda i: (0, 0))
    row = pl.BlockSpec((ts, d), lambda i: (i, 0))
    return pl.pallas_call(
        body, name=name,
        grid=(s // ts,),
        in_specs=[pl.BlockSpec((ts, k), lambda i: (i, 0)),
                  pl.BlockSpec((k, d), lambda i: (0, 0), pipeline_mode=pl.Buffered(1)), row, vec, vec],
        out_specs=[row, row],
        out_shape=[jax.ShapeDtypeStruct((s, d), F32)] * 2,
        compiler_params=pltpu.CompilerParams(
            dimension_semantics=("arbitrary",),
            vmem_limit_bytes=_vmem_limit(2 * (ts * k * 2 + 3 * ts * d * 4) + k * d * 2 + 4 * ts * d * 4)),
    )(a, w, xres, gate, gpost)


def _wgrad(a, b, acols, bcols, out_struct, out_index, out_block, name):
    s = a.shape[0]
    ts = _row_tile(s, 512)
    aw, afn = acols
    bw, bfn = bcols
    nj = out_index[0]
    oidx = out_index[1]

    def body(a_ref, b_ref, o_ref, acc):
        i = pl.program_id(1)

        @pl.when(i == 0)
        def _():
            acc[...] = jnp.zeros_like(acc)

        acc[...] += _dot_tn(a_ref[...], b_ref[...])

        @pl.when(i == pl.num_programs(1) - 1)
        def _():
            o_ref[...] = acc[...].astype(o_ref.dtype)

    return pl.pallas_call(
        body, name=name,
        grid=(nj, s // ts),
        in_specs=[pl.BlockSpec((ts, aw), lambda j, i: (i, afn(j))), pl.BlockSpec((ts, bw), lambda j, i: (i, bfn(j)))],
        out_specs=pl.BlockSpec(out_block, lambda j, i: oidx(j)),
        out_shape=out_struct,
        scratch_shapes=[pltpu.VMEM((aw, bw), F32)],
        compiler_params=pltpu.CompilerParams(
            dimension_semantics=("arbitrary", "arbitrary"),
            vmem_limit_bytes=_vmem_limit(2 * (ts * aw * 2 + ts * bw * 2) + 3 * aw * bw * 4 + ts * aw * 4)),
    )(a, b)


def _loss_head(xo, tgt):
    s, d = xo.shape
    ts = _row_tile(s, 512)

    def body(x_ref, t_ref, dx_ref, l_ref, acc):
        i = pl.program_id(0)

        @pl.when(i == 0)
        def _():
            acc[...] = jnp.zeros_like(acc)

        e = x_ref[...] - t_ref[...]
        dx_ref[...] = e * (1.0 / d)
        acc[...] += _rsum(e * e)

        @pl.when(i == pl.num_programs(0) - 1)
        def _():
            tot = jnp.sum(acc[...], axis=-1, keepdims=True) * (0.5 / d)
            l_ref[...] = jnp.broadcast_to(tot, l_ref.shape)

    row = pl.BlockSpec((ts, d), lambda i: (i, 0))
    return pl.pallas_call(
        body, name="loss_head",
        grid=(s // ts,),
        in_specs=[row, row],
        out_specs=[row, pl.BlockSpec((1, SMALL_COLS), lambda i: (0, 0))],
        out_shape=[jax.ShapeDtypeStruct((s, d), F32), jax.ShapeDtypeStruct((1, SMALL_COLS), F32)],
        scratch_shapes=[pltpu.VMEM((1, d), F32)],
        compiler_params=pltpu.CompilerParams(dimension_semantics=("arbitrary",)),
    )(xo, tgt)


def _resid_bwd_matmul(dxn, o, gate, gpost, w, name):
    s, d = dxn.shape
    k = w.shape[0]
    ts = _row_tile(s, 512)

    def body(dx_ref, o_ref, gate_ref, gp_ref, w_ref, do_ref, da_ref, dgate_ref, dgp_ref):
        i = pl.program_id(0)

        @pl.when(i == 0)
        def _():
            dgate_ref[...] = jnp.zeros_like(dgate_ref)
            dgp_ref[...] = jnp.zeros_like(dgp_ref)

        dx = dx_ref[...]
        o = o_ref[...]
        r = lax.rsqrt(_rmean(o * o) + EPS)
        on = o * r
        dgate_ref[...] += _rsum(dx * (on * gp_ref[...]))
        don = dx * gate_ref[...]
        dgp_ref[...] += _rsum(don * on)
        t = don * gp_ref[...]
        do = (r * (t - on * _rmean(t * on))).astype(BF16)
        do_ref[...] = do
        da_ref[...] = _dot_nt(do, w_ref[...])

    vec = pl.BlockSpec((1, d), lambda i: (0, 0))
    row = pl.BlockSpec((ts, d), lambda i: (i, 0))
    return pl.pallas_call(
        body, name=name,
        grid=(s // ts,),
        in_specs=[row, row, vec, vec, pl.BlockSpec((k, d), lambda i: (0, 0), pipeline_mode=pl.Buffered(1))],
        out_specs=[row, pl.BlockSpec((ts, k), lambda i: (i, 0)), vec, vec],
        out_shape=[jax.ShapeDtypeStruct((s, d), BF16), jax.ShapeDtypeStruct((s, k), F32),
                   jax.ShapeDtypeStruct((1, d), F32), jax.ShapeDtypeStruct((1, d), F32)],
        compiler_params=pltpu.CompilerParams(
            dimension_semantics=("arbitrary",),
            vmem_limit_bytes=_vmem_limit(2 * (2 * ts * d * 4 + ts * d * 2 + ts * k * 4) + d * k * 2 + 6 * ts * d * 4)),
    )(dxn, o, gate, gpost, w)


def _norm_mod_bwd(dh, xv, gain, s1p, dres):
    r = lax.rsqrt(_rmean(xv * xv) + EPS)
    xn = xv * r
    dshift = _rsum(dh)
    t = dh * xn
    dscale = _rsum(t * gain)
    dgain = _rsum(t * s1p)
    dxn = dh * (gain * s1p)
    dx = r * (dxn - xn * _rmean(dxn * xn)) + dres
    return dx, dshift, dscale, dgain


def _lane_lt(shape, bound):
    return lax.broadcasted_iota(jnp.int32, shape, 1) < bound


def _sgu_forward(z_ref, bd_ref, ng_ref, nb_ref, wm_ref, bias_ref, ts, ya_s, f_s):
    u = _gelu(z_ref[:, 0:SGU_WIDTH])
    v = _gelu(z_ref[:, SGU_WIDTH:2 * SGU_WIDTH])
    bd = bd_ref[...]
    vc = v - _group_mean(v, bd)
    rstd = lax.rsqrt(_group_mean(vc * vc, bd) + EPS)
    vhat = vc * rstd
    vn = (vhat * ng_ref[...] + nb_ref[...]).astype(BF16)
    left = _lane_lt((CHUNK, CHUNK), HEAD_DIM)
    for n in range(ts // CHUNK):
        rows = slice(n * CHUNK, (n + 1) * CHUNK)
        for p in range(SGU_WIDTH // CHUNK):
            cols = slice(p * CHUNK, (p + 1) * CHUNK)
            blk = vn[rows, cols]
            f = jnp.where(left, _dot(wm_ref[2 * p], blk), _dot(wm_ref[2 * p + 1], blk)) + bias_ref[:, cols]
            if f_s is not None:
                f_s[rows, cols] = f
            ya_s[rows, cols] = u[rows, cols] * f
    return u, vhat, rstd, vn


def _conv31_forward(z_ref, zh_ref, first, cw_ref, cb_ref, ts, ext_b, cbs):
    a = z_ref[:, 2 * SGU_WIDTH:2 * SGU_WIDTH + CONV_WIDTH]
    g = z_ref[:, 2 * SGU_WIDTH + CONV_WIDTH:2 * SGU_WIDTH + 2 * CONV_WIDTH]
    ah = zh_ref[:, 2 * SGU_WIDTH:2 * SGU_WIDTH + CONV_WIDTH]
    gh = zh_ref[:, 2 * SGU_WIDTH + CONV_WIDTH:2 * SGU_WIDTH + 2 * CONV_WIDTH]
    ext_b[pl.ds(0, MIX_HALO), :] = jnp.where(first, 0.0, ah * jax.nn.sigmoid(gh))
    ext_b[pl.ds(MIX_HALO, ts), :] = a * jax.nn.sigmoid(g)
    for r in range(ts // CONV_ROWS):
        acc = jnp.broadcast_to(cb_ref[...], (CONV_ROWS, CONV_WIDTH))
        for k in range(CONV_K):
            acc = acc + cw_ref[k:k + 1, :] * ext_b[pl.ds(MIX_HALO - (CONV_K - 1) + k + r * CONV_ROWS, CONV_ROWS), :]
        cbs[pl.ds(r * CONV_ROWS, CONV_ROWS), :] = acc


def _pool_counts(i, ts):
    pos1 = (i * ts + 1 + lax.broadcasted_iota(jnp.int32, (ts, POOL_WIDTH), 0)).astype(F32)
    lane = lax.broadcasted_iota(jnp.int32, (ts, POOL_WIDTH), 1)
    gdim = POOL_WIDTH // len(POOL_WINDOWS)
    win = jnp.where(lane < gdim, float(POOL_WINDOWS[0]),
                    jnp.where(lane < 2 * gdim, float(POOL_WINDOWS[1]),
                              jnp.where(lane < 3 * gdim, float(POOL_WINDOWS[2]), float(POOL_WINDOWS[3]))))
    return jnp.minimum(pos1, win)


def _window_sums(ext, base, ts, sign):
    lane = lax.broadcasted_iota(jnp.int32, (ts, POOL_WIDTH), 1)
    gdim = POOL_WIDTH // len(POOL_WINDOWS)
    run = jnp.zeros((ts, POOL_WIDTH), F32)
    out = jnp.zeros((ts, POOL_WIDTH), F32)
    for m in range(POOL_WINDOWS[-1]):
        run = run + ext[pl.ds(base + sign * m, ts), :]
        for gi, win in enumerate(POOL_WINDOWS):
            if m == win - 1:
                out = jnp.where((lane >= gi * gdim) & (lane < (gi + 1) * gdim), run, out)
    return out


def _pool_forward(z_ref, zh_ref, first, i, ts, ext_c):
    c0 = 2 * SGU_WIDTH + 2 * CONV_WIDTH
    zc = z_ref[:, c0:c0 + POOL_WIDTH]
    ext_c[pl.ds(0, MIX_HALO), :] = jnp.where(first, 0.0, zh_ref[:, c0:c0 + POOL_WIDTH])
    ext_c[pl.ds(MIX_HALO, ts), :] = zc
    sums = _window_sums(ext_c, MIX_HALO, ts, -1)
    return sums / _pool_counts(i, ts) - zc


def _layer_norm_rows(v):
    mu = _rmean(v)
    vc = v - mu
    rstd = lax.rsqrt(_rmean(vc * vc) + EPS)
    return vc * rstd, rstd


def _mixer_specs(s, ts, width):
    nbh = ts // MIX_HALO
    tile = pl.BlockSpec((ts, width), lambda i: (i, 0))
    prev = pl.BlockSpec((MIX_HALO, width), lambda i: (jnp.maximum(i * nbh - 1, 0), 0))
    nxt = pl.BlockSpec((MIX_HALO, width), lambda i: (jnp.minimum((i + 1) * nbh, s // MIX_HALO - 1), 0))
    return tile, prev, nxt


def _mixer_fwd(z, mp, name):
    s, inw = z.shape
    d = SGU_WIDTH + CONV_WIDTH + POOL_WIDTH
    ts = _row_tile(s, 256)

    def body(z_ref, zh_ref, bd_ref, ng_ref, nb_ref, wm_ref, bias_ref, cw_ref, cb_ref, cng_ref, cnb_ref,
             pw_ref, ps_ref, bg_ref, y_ref, ya_s, ext_b, cbs, ext_c):
        i = pl.program_id(0)
        first = i == 0
        _sgu_forward(z_ref, bd_ref, ng_ref, nb_ref, wm_ref, bias_ref, ts, ya_s, None)
        ya = ya_s[...]
        ra = lax.rsqrt(_rmean(ya * ya) + EPS)
        y_ref[:, 0:SGU_WIDTH] = ((ya * ra) * bg_ref[:, 0:SGU_WIDTH]).astype(BF16)

        _conv31_forward(z_ref, zh_ref, first, cw_ref, cb_ref, ts, ext_b, cbs)
        chat, _ = _layer_norm_rows(cbs[...])
        lin = chat * cng_ref[...] + cnb_ref[...]
        yb = lin * jax.nn.sigmoid(lin)
        rb = lax.rsqrt(_rmean(yb * yb) + EPS)
        y_ref[:, SGU_WIDTH:SGU_WIDTH + CONV_WIDTH] = ((yb * rb) * bg_ref[:, SGU_WIDTH:SGU_WIDTH + CONV_WIDTH]).astype(BF16)

        dpool = _pool_forward(z_ref, zh_ref, first, i, ts, ext_c)
        yc = _dot(dpool.astype(BF16), pw_ref[...]) * ps_ref[...]
        rc = lax.rsqrt(_rmean(yc * yc) + EPS)
        y_ref[:, SGU_WIDTH + CONV_WIDTH:d] = ((yc * rc) * bg_ref[:, SGU_WIDTH + CONV_WIDTH:d]).astype(BF16)

    tile, prev, _ = _mixer_specs(s, ts, inw)
    consts = [mp["bd"], mp["ng"], mp["nb"], mp["wm"], mp["bias"], mp["cw"], mp["cb"], mp["cng"], mp["cnb"],
              mp["pw"], mp["ps"], mp["bg"]]
    return pl.pallas_call(
        body, name=name,
        grid=(s // ts,),
        in_specs=[tile, prev] + [_const_spec(c.shape) for c in consts],
        out_specs=pl.BlockSpec((ts, d), lambda i: (i, 0)),
        out_shape=jax.ShapeDtypeStruct((s, d), BF16),
        scratch_shapes=[pltpu.VMEM((ts, SGU_WIDTH), F32), pltpu.VMEM((ts + MIX_HALO, CONV_WIDTH), F32),
                        pltpu.VMEM((ts, CONV_WIDTH), F32), pltpu.VMEM((ts + MIX_HALO, POOL_WIDTH), F32)],
        compiler_params=pltpu.CompilerParams(dimension_semantics=("arbitrary",),
                                             vmem_limit_bytes=_vmem_limit(16 * ts * inw * 4)),
    )(z, z, *consts)


def _mixer_bwd_a(z, dy, mp, name):
    s, inw = z.shape
    d = SGU_WIDTH + CONV_WIDTH + POOL_WIDTH
    ts = _row_tile(s, 256)
    nchunk = ts // CHUNK

    def rms_bwd(dyn, y, g):
        r = lax.rsqrt(_rmean(y * y) + EPS)
        yn = y * r
        dg = _rsum(dyn * yn)
        t = dyn * g
        return r * (t - yn * _rmean(t * yn)), dg

    def body(z_ref, zh_ref, dy_ref, bd_ref, ng_ref, nb_ref, wm_ref, wmt_ref, bias_ref, cw_ref, cb_ref, cng_ref, cnb_ref,
             pw_ref, pwt_ref, ps_ref, bg_ref,
             dza_ref, dcb_ref, dd_ref, dbg_ref, dwm_ref, dbias_ref, dng_ref, dnb_ref, dcng_ref, dcnb_ref, dps_ref, dpw_ref,
             ya_s, f_s, dvn_s, ext_b, cbs, ext_c):
        i = pl.program_id(0)
        first = i == 0

        @pl.when(first)
        def _():
            for ref in (dwm_ref, dbias_ref, dng_ref, dnb_ref, dcng_ref, dcnb_ref, dps_ref, dpw_ref):
                ref[...] = jnp.zeros_like(ref)

        u, vhat, rstd, vn = _sgu_forward(z_ref, bd_ref, ng_ref, nb_ref, wm_ref, bias_ref, ts, ya_s, f_s)
        dya, dbg_a = rms_bwd(dy_ref[:, 0:SGU_WIDTH], ya_s[...], bg_ref[:, 0:SGU_WIDTH])
        du = dya * f_s[...]
        df = dya * u
        dfb = df.astype(BF16)
        left = _lane_lt((CHUNK, CHUNK), HEAD_DIM)
        zero = jnp.zeros((CHUNK, CHUNK), BF16)
        dbias = jnp.zeros((CHUNK, SGU_WIDTH), F32)
        for n in range(nchunk):
            rows = slice(n * CHUNK, (n + 1) * CHUNK)
            dbias = dbias + df[rows, :]
            for p in range(SGU_WIDTH // CHUNK):
                cols = slice(p * CHUNK, (p + 1) * CHUNK)
                dblk = dfb[rows, cols]
                vblk = vn[rows, cols]
                dwm_ref[2 * p] += _dot_nt(jnp.where(left, dblk, zero), vblk)
                dwm_ref[2 * p + 1] += _dot_nt(jnp.where(left, zero, dblk), vblk)
                dvn_s[rows, cols] = jnp.where(left, _dot(wmt_ref[2 * p], dblk), _dot(wmt_ref[2 * p + 1], dblk))
        dbias_ref[...] += dbias
        dvn = dvn_s[...]
        dng_ref[...] += _rsum(dvn * vhat)
        dnb_ref[...] += _rsum(dvn)
        dvh = dvn * ng_ref[...]
        bd = bd_ref[...]
        dv = rstd * (dvh - _group_mean(dvh, bd) - vhat * _group_mean(dvh * vhat, bd))
        dza_ref[:, 0:SGU_WIDTH] = (du * _gelu_grad(z_ref[:, 0:SGU_WIDTH])).astype(BF16)
        dza_ref[:, SGU_WIDTH:2 * SGU_WIDTH] = (dv * _gelu_grad(z_ref[:, SGU_WIDTH:2 * SGU_WIDTH])).astype(BF16)

        _conv31_forward(z_ref, zh_ref, first, cw_ref, cb_ref, ts, ext_b, cbs)
        chat, crstd = _layer_norm_rows(cbs[...])
        lin = chat * cng_ref[...] + cnb_ref[...]
        sl = jax.nn.sigmoid(lin)
        dyb, dbg_b = rms_bwd(dy_ref[:, SGU_WIDTH:SGU_WIDTH + CONV_WIDTH], lin * sl, bg_ref[:, SGU_WIDTH:SGU_WIDTH + CONV_WIDTH])
        dlin = dyb * (sl * (1.0 + lin * (1.0 - sl)))
        dcng_ref[...] += _rsum(dlin * chat)
        dcnb_ref[...] += _rsum(dlin)
        dch = dlin * cng_ref[...]
        dcb_ref[...] = crstd * (dch - _rmean(dch) - chat * _rmean(dch * chat))

        dpool = _pool_forward(z_ref, zh_ref, first, i, ts, ext_c)
        dpb = dpool.astype(BF16)
        ycp = _dot(dpb, pw_ref[...])
        dyc, dbg_c = rms_bwd(dy_ref[:, SGU_WIDTH + CONV_WIDTH:d], ycp * ps_ref[...], bg_ref[:, SGU_WIDTH + CONV_WIDTH:d])
        dps_ref[...] += _rsum(dyc * ycp)
        dycp = (dyc * ps_ref[...]).astype(BF16)
        dpw_ref[...] += _dot_tn(dpb, dycp)
        dd_ref[...] = _dot(dycp, pwt_ref[...])

        @pl.when(first)
        def _():
            dbg_ref[...] = jnp.zeros_like(dbg_ref)

        dbg_ref[:, 0:SGU_WIDTH] += dbg_a
        dbg_ref[:, SGU_WIDTH:SGU_WIDTH + CONV_WIDTH] += dbg_b
        dbg_ref[:, SGU_WIDTH + CONV_WIDTH:d] += dbg_c

    tile, prev, _ = _mixer_specs(s, ts, inw)
    consts = [mp["bd"], mp["ng"], mp["nb"], mp["wm"], mp["wmt"], mp["bias"], mp["cw"], mp["cb"], mp["cng"], mp["cnb"],
              mp["pw"], mp["pwt"], mp["ps"], mp["bg"]]
    acc_shapes = [(1, d), (2 * (SGU_WIDTH // CHUNK), CHUNK, CHUNK), (CHUNK, SGU_WIDTH), (1, SGU_WIDTH), (1, SGU_WIDTH),
                  (1, CONV_WIDTH), (1, CONV_WIDTH), (1, POOL_WIDTH), (POOL_WIDTH, POOL_WIDTH)]
    return pl.pallas_call(
        body, name=name,
        grid=(s // ts,),
        in_specs=[tile, prev, pl.BlockSpec((ts, d), lambda i: (i, 0))] + [_const_spec(c.shape) for c in consts],
        out_specs=[pl.BlockSpec((ts, 2 * SGU_WIDTH), lambda i: (i, 0)), pl.BlockSpec((ts, CONV_WIDTH), lambda i: (i, 0)),
                   pl.BlockSpec((ts, POOL_WIDTH), lambda i: (i, 0))] + [_const_spec(a) for a in acc_shapes],
        out_shape=[jax.ShapeDtypeStruct((s, 2 * SGU_WIDTH), BF16), jax.ShapeDtypeStruct((s, CONV_WIDTH), F32),
                   jax.ShapeDtypeStruct((s, POOL_WIDTH), F32)] + [jax.ShapeDtypeStruct(a, F32) for a in acc_shapes],
        scratch_shapes=[pltpu.VMEM((ts, SGU_WIDTH), F32), pltpu.VMEM((ts, SGU_WIDTH), F32), pltpu.VMEM((ts, SGU_WIDTH), F32),
                        pltpu.VMEM((ts + MIX_HALO, CONV_WIDTH), F32), pltpu.VMEM((ts, CONV_WIDTH), F32),
                        pltpu.VMEM((ts + MIX_HALO, POOL_WIDTH), F32)],
        compiler_params=pltpu.CompilerParams(dimension_semantics=("arbitrary",),
                                             vmem_limit_bytes=_vmem_limit(24 * ts * inw * 4)),
    )(z, z, dy, *consts)


def _mixer_bwd_b(z, dza, dcb, dd, x, dres, gain, s1p, cw, w, name):
    s, inw = z.shape
    d = x.shape[1]
    ts = _row_tile(s, 256)
    c0 = 2 * SGU_WIDTH
    c1 = c0 + 2 * CONV_WIDTH

    def body(z_ref, zh_ref, dza_ref, dcb_ref, dcbn_ref, dd_ref, ddn_ref, x_ref, dres_ref, g_ref, s_ref, cw_ref, w_ref,
             dx_ref, dz_ref, dsh_ref, dsc_ref, dg_ref, dcw_ref, dcbias_ref, ext_b, ext_n, ext_e):
        i = pl.program_id(0)
        first = i == 0
        last = i == pl.num_programs(0) - 1

        @pl.when(first)
        def _():
            for ref in (dsh_ref, dsc_ref, dg_ref, dcw_ref, dcbias_ref):
                ref[...] = jnp.zeros_like(ref)

        a = z_ref[:, c0:c0 + CONV_WIDTH]
        sg = jax.nn.sigmoid(z_ref[:, c0 + CONV_WIDTH:c1])
        ah = zh_ref[:, c0:c0 + CONV_WIDTH]
        gh = zh_ref[:, c0 + CONV_WIDTH:c1]
        ext_b[pl.ds(0, MIX_HALO), :] = jnp.where(first, 0.0, ah * jax.nn.sigmoid(gh))
        ext_b[pl.ds(MIX_HALO, ts), :] = a * sg
        dcbv = dcb_ref[...]
        dcbias_ref[...] += _rsum(dcbv)
        for k in range(CONV_K):
            dcw_ref[k:k + 1, :] += _rsum(dcbv * ext_b[pl.ds(MIX_HALO - (CONV_K - 1) + k, ts), :])

        ext_n[pl.ds(0, ts), :] = dcbv
        ext_n[pl.ds(ts, MIX_HALO), :] = jnp.where(last, 0.0, dcbn_ref[...])
        for r in range(ts // CONV_ROWS):
            acc = jnp.zeros((CONV_ROWS, CONV_WIDTH), F32)
            for k in range(CONV_K):
                acc = acc + cw_ref[k:k + 1, :] * ext_n[pl.ds(CONV_K - 1 - k + r * CONV_ROWS, CONV_ROWS), :]
            rows = pl.ds(r * CONV_ROWS, CONV_ROWS)
            ar = z_ref[rows, c0:c0 + CONV_WIDTH]
            sr = jax.nn.sigmoid(z_ref[rows, c0 + CONV_WIDTH:c1])
            dz_ref[rows, c0:c0 + CONV_WIDTH] = (acc * sr).astype(BF16)
            dz_ref[rows, c0 + CONV_WIDTH:c1] = (acc * ar * sr * (1.0 - sr)).astype(BF16)

        ddv = dd_ref[...]
        ext_e[pl.ds(0, ts), :] = ddv / _pool_counts(i, ts)
        nh = (i + 1) * ts + lax.broadcasted_iota(jnp.int32, (MIX_HALO, POOL_WIDTH), 0)
        lane = lax.broadcasted_iota(jnp.int32, (MIX_HALO, POOL_WIDTH), 1)
        gdim = POOL_WIDTH // len(POOL_WINDOWS)
        winh = jnp.where(lane < gdim, float(POOL_WINDOWS[0]),
                         jnp.where(lane < 2 * gdim, float(POOL_WINDOWS[1]),
                                   jnp.where(lane < 3 * gdim, float(POOL_WINDOWS[2]), float(POOL_WINDOWS[3]))))
        cnth = jnp.minimum((nh + 1).astype(F32), winh)
        ext_e[pl.ds(ts, MIX_HALO), :] = jnp.where(last, 0.0, ddn_ref[...] / cnth)
        dz_ref[:, c1:inw] = (_window_sums(ext_e, 0, ts, 1) - ddv).astype(BF16)
        dz_ref[:, 0:c0] = dza_ref[...]

        dh = _dot_nt(dz_ref[...], w_ref[...])
        dx, dsh, dsc, dg = _norm_mod_bwd(dh, x_ref[...], g_ref[...], s_ref[...], dres_ref[...])
        dx_ref[...] = dx
        dsh_ref[...] += dsh
        dsc_ref[...] += dsc
        dg_ref[...] += dg

    tile, prev, _ = _mixer_specs(s, ts, inw)
    _, _, nxt_b = _mixer_specs(s, ts, CONV_WIDTH)
    _, _, nxt_c = _mixer_specs(s, ts, POOL_WIDTH)
    row = pl.BlockSpec((ts, d), lambda i: (i, 0))
    vec = pl.BlockSpec((1, d), lambda i: (0, 0))
    return pl.pallas_call(
        body, name=name,
        grid=(s // ts,),
        in_specs=[tile, prev, pl.BlockSpec((ts, c0), lambda i: (i, 0)),
                  pl.BlockSpec((ts, CONV_WIDTH), lambda i: (i, 0)), nxt_b,
                  pl.BlockSpec((ts, POOL_WIDTH), lambda i: (i, 0)), nxt_c,
                  row, row, vec, vec, _const_spec(cw.shape),
                  pl.BlockSpec(w.shape, lambda i: (0, 0), pipeline_mode=pl.Buffered(1))],
        out_specs=[row, pl.BlockSpec((ts, inw), lambda i: (i, 0)), vec, vec, vec,
                   _const_spec((CONV_K, CONV_WIDTH)), _const_spec((1, CONV_WIDTH))],
        out_shape=[jax.ShapeDtypeStruct((s, d), F32), jax.ShapeDtypeStruct((s, inw), BF16)]
        + [jax.ShapeDtypeStruct((1, d), F32)] * 3
        + [jax.ShapeDtypeStruct((CONV_K, CONV_WIDTH), F32), jax.ShapeDtypeStruct((1, CONV_WIDTH), F32)],
        scratch_shapes=[pltpu.VMEM((ts + MIX_HALO, CONV_WIDTH), F32), pltpu.VMEM((ts + MIX_HALO, CONV_WIDTH), F32),
                        pltpu.VMEM((ts + MIX_HALO, POOL_WIDTH), F32)],
        compiler_params=pltpu.CompilerParams(dimension_semantics=("arbitrary",),
                                             vmem_limit_bytes=_vmem_limit(16 * ts * inw * 4 + inw * d * 2)),
    )(z, z, dza, dcb, dcb, dd, dd, x, dres, gain, s1p, cw, w)


def _ffn_specs(s, ts, tc, half_blocks):
    nbh = ts // FFN_HALO

    def tile(off):
        return pl.BlockSpec((ts, tc), lambda j, i: (i, j + off))

    def prev(off):
        return pl.BlockSpec((FFN_HALO, tc), lambda j, i: (jnp.maximum(i * nbh - 1, 0), j + off))

    def vec(rows, off):
        return pl.BlockSpec((rows, tc), lambda j, i: (0, j + off))

    return tile, prev, vec


def _conv3_tile(p_ref, ph_ref, first, w_ref, b_ref, ts, ext):
    ext[pl.ds(0, FFN_HALO), :] = jnp.where(first, 0.0, ph_ref[...])
    ext[pl.ds(FFN_HALO, ts), :] = p_ref[...]
    acc = b_ref[...] + w_ref[FFN_CONV_K - 1:FFN_CONV_K, :] * p_ref[...]
    for k in range(FFN_CONV_K - 1):
        acc = acc + w_ref[k:k + 1, :] * ext[pl.ds(FFN_HALO - (FFN_CONV_K - 1) + k, ts), :]
    return acc


def _ffn_act_fwd(p, cw, cb, name):
    s, f2 = p.shape
    f = f2 // 2
    tc = f // 2
    hb = f // tc
    ts = _row_tile(s, 256)

    def body(pg_ref, pgh_ref, pv_ref, pvh_ref, wg_ref, wv_ref, bg_ref, bv_ref, act_ref, ext_g, ext_v):
        first = pl.program_id(1) == 0
        ug = _conv3_tile(pg_ref, pgh_ref, first, wg_ref, bg_ref, ts, ext_g)
        uv = _conv3_tile(pv_ref, pvh_ref, first, wv_ref, bv_ref, ts, ext_v)
        act_ref[...] = (_gelu(ug) * uv).astype(BF16)

    tile, prev, vec = _ffn_specs(s, ts, tc, hb)
    return pl.pallas_call(
        body, name=name,
        grid=(hb, s // ts),
        in_specs=[tile(0), prev(0), tile(hb), prev(hb), vec(FFN_CONV_K, 0), vec(FFN_CONV_K, hb), vec(1, 0), vec(1, hb)],
        out_specs=pl.BlockSpec((ts, tc), lambda j, i: (i, j)),
        out_shape=jax.ShapeDtypeStruct((s, f), BF16),
        scratch_shapes=[pltpu.VMEM((ts + FFN_HALO, tc), F32)] * 2,
        compiler_params=pltpu.CompilerParams(dimension_semantics=("arbitrary", "arbitrary"),
                                             vmem_limit_bytes=_vmem_limit(16 * ts * tc * 4)),
    )(p, p, p, p, cw, cw, cb, cb)


def _ffn_act_bwd(p, dact, cw, cb, name):
    s, f2 = p.shape
    f = f2 // 2
    tc = f // 2
    hb = f // tc
    ts = _row_tile(s, 256)

    def body(pg_ref, pgh_ref, pv_ref, pvh_ref, da_ref, wg_ref, wv_ref, bg_ref, bv_ref,
             dug_ref, duv_ref, dwg_ref, dwv_ref, dbg_ref, dbv_ref, ext_g, ext_v):
        first = pl.program_id(1) == 0

        @pl.when(first)
        def _():
            for ref in (dwg_ref, dwv_ref, dbg_ref, dbv_ref):
                ref[...] = jnp.zeros_like(ref)

        ug = _conv3_tile(pg_ref, pgh_ref, first, wg_ref, bg_ref, ts, ext_g)
        uv = _conv3_tile(pv_ref, pvh_ref, first, wv_ref, bv_ref, ts, ext_v)
        da = da_ref[...]
        dug = da * uv * _gelu_grad(ug)
        duv = da * _gelu(ug)
        dug_ref[...] = dug
        duv_ref[...] = duv
        dbg_ref[...] += _rsum(dug)
        dbv_ref[...] += _rsum(duv)
        for k in range(FFN_CONV_K):
            off = FFN_HALO - (FFN_CONV_K - 1) + k
            dwg_ref[k:k + 1, :] += _rsum(dug * ext_g[pl.ds(off, ts), :])
            dwv_ref[k:k + 1, :] += _rsum(duv * ext_v[pl.ds(off, ts), :])

    tile, prev, vec = _ffn_specs(s, ts, tc, hb)
    half = pl.BlockSpec((ts, tc), lambda j, i: (i, j))
    wacc = pl.BlockSpec((FFN_CONV_K, tc), lambda j, i: (0, j))
    bacc = pl.BlockSpec((1, tc), lambda j, i: (0, j))
    return pl.pallas_call(
        body, name=name,
        grid=(hb, s // ts),
        in_specs=[tile(0), prev(0), tile(hb), prev(hb), half, vec(FFN_CONV_K, 0), vec(FFN_CONV_K, hb), vec(1, 0), vec(1, hb)],
        out_specs=[half, half, wacc, wacc, bacc, bacc],
        out_shape=[jax.ShapeDtypeStruct((s, f), F32)] * 2 + [jax.ShapeDtypeStruct((FFN_CONV_K, f), F32)] * 2
        + [jax.ShapeDtypeStruct((1, f), F32)] * 2,
        scratch_shapes=[pltpu.VMEM((ts + FFN_HALO, tc), F32)] * 2,
        compiler_params=pltpu.CompilerParams(dimension_semantics=("arbitrary", "arbitrary"),
                                             vmem_limit_bytes=_vmem_limit(24 * ts * tc * 4)),
    )(p, p, p, p, dact, cw, cw, cb, cb)


def _ffn_in_bwd(dug, duv, cw, w, x, dres, gain, s1p, name):
    s, f = dug.shape
    d = x.shape[1]
    ts = _row_tile(s, 256)
    tc = w.shape[2]
    assert f % tc == 0 and w.shape[0] * tc == 2 * f
    nbh = ts // FFN_HALO

    def body(dug_ref, dugn_ref, duv_ref, duvn_ref, cw_ref, w_ref, x_ref, dres_ref, g_ref, s_ref,
             dx_ref, dp_ref, dsh_ref, dsc_ref, dg_ref, ext):
        i = pl.program_id(0)
        last = i == pl.num_programs(0) - 1

        @pl.when(i == 0)
        def _():
            for ref in (dsh_ref, dsc_ref, dg_ref):
                ref[...] = jnp.zeros_like(ref)

        dh = jnp.zeros((ts, d), F32)
        for half, (t_ref, n_ref) in enumerate(((dug_ref, dugn_ref), (duv_ref, duvn_ref))):
            for cb in range(f // tc):
                cols = slice(cb * tc, (cb + 1) * tc)
                wcols = slice(half * f + cb * tc, half * f + (cb + 1) * tc)
                ext[pl.ds(0, ts), :] = t_ref[:, cols]
                ext[pl.ds(ts, FFN_HALO), :] = jnp.where(last, 0.0, n_ref[:, cols])
                acc = cw_ref[FFN_CONV_K - 1:FFN_CONV_K, wcols] * t_ref[:, cols]
                for k in range(FFN_CONV_K - 1):
                    acc = acc + cw_ref[k:k + 1, wcols] * ext[pl.ds(FFN_CONV_K - 1 - k, ts), :]
                dpb = acc.astype(BF16)
                dp_ref[:, wcols] = dpb
                dh = dh + _dot_nt(dpb, w_ref[half * (f // tc) + cb])
        dx, dsh, dsc, dg = _norm_mod_bwd(dh, x_ref[...], g_ref[...], s_ref[...], dres_ref[...])
        dx_ref[...] = dx
        dsh_ref[...] += dsh
        dsc_ref[...] += dsc
        dg_ref[...] += dg

    tile = pl.BlockSpec((ts, f), lambda i: (i, 0))
    nxt = pl.BlockSpec((FFN_HALO, f), lambda i: (jnp.minimum((i + 1) * nbh, s // FFN_HALO - 1), 0))
    row = pl.BlockSpec((ts, d), lambda i: (i, 0))
    vec = pl.BlockSpec((1, d), lambda i: (0, 0))
    return pl.pallas_call(
        body, name=name,
        grid=(s // ts,),
        in_specs=[tile, nxt, tile, nxt, _const_spec(cw.shape),
                  pl.BlockSpec(w.shape, lambda i: (0, 0, 0), pipeline_mode=pl.Buffered(1)), row, row, vec, vec],
        out_specs=[row, pl.BlockSpec((ts, 2 * f), lambda i: (i, 0)), vec, vec, vec],
        out_shape=[jax.ShapeDtypeStruct((s, d), F32), jax.ShapeDtypeStruct((s, 2 * f), BF16)] + [jax.ShapeDtypeStruct((1, d), F32)] * 3,
        scratch_shapes=[pltpu.VMEM((ts + FFN_HALO, tc), F32)],
        compiler_params=pltpu.CompilerParams(
            dimension_semantics=("arbitrary",),
            vmem_limit_bytes=_vmem_limit(4 * ts * f * 4 + 2 * f * d * 2 + 2 * ts * 2 * f * 2 + 12 * ts * d * 4 + 6 * ts * tc * 4)),
    )(dug, dug, duv, duv, cw, w, x, dres, gain, s1p)


def _adamw_math(w, g, m, v):
    m = ADAM_B1 * m + (1.0 - ADAM_B1) * g
    v = ADAM_B2 * v + (1.0 - ADAM_B2) * (g * g)
    m_hat = m / (1.0 - ADAM_B1 ** ADAM_STEP)
    v_hat = v / (1.0 - ADAM_B2 ** ADAM_STEP)
    delta = -ADAM_LR * (m_hat / (jnp.sqrt(v_hat) + ADAM_EPS) + ADAM_WD * w)
    return delta, m, v


def _adam_rows(rows, cols):
    want = max(8, (2 * 1024 * 1024 // (cols * 4)) // 8 * 8)
    tr = min(rows, want)
    while rows % tr:
        tr -= 8
    return tr


def _adamw(w, m, v, g_parts, name):
    shape = w.shape
    c = shape[-1]
    r = math.prod(shape[:-1])
    tr = _adam_rows(r, c)
    ng = len(g_parts)

    def body(*refs):
        w_ref, m_ref, v_ref = refs[0:3]
        g_refs = refs[3:3 + ng]
        g_out, d_out, m_out, v_out = refs[3 + ng:]
        g = g_refs[0][...]
        for gr in g_refs[1:]:
            g = g + gr[...]
        delta, mn, vn = _adamw_math(w_ref[...], g, m_ref[...], v_ref[...])
        g_out[...] = g
        d_out[...] = delta
        m_out[...] = mn
        v_out[...] = vn

    blk = pl.BlockSpec((tr, c), lambda i: (i, 0))
    outs = pl.pallas_call(
        body, name=name,
        grid=(r // tr,),
        in_specs=[blk] * (3 + ng),
        out_specs=[blk] * 4,
        out_shape=[jax.ShapeDtypeStruct((r, c), F32)] * 4,
        compiler_params=pltpu.CompilerParams(dimension_semantics=("arbitrary",),
                                             vmem_limit_bytes=_vmem_limit(2 * (7 + ng) * tr * max(c, 128) * 4 + (8 << 20))),
    )(w.reshape(r, c), m.reshape(r, c), v.reshape(r, c), *[g.reshape(r, c) for g in g_parts])
    return [o.reshape(shape) for o in outs]


def _modw_adamw(sct, dmod, w, m, v, name):
    nl, d, n = w.shape
    tr = _row_tile(d, 256)

    def body(sct_ref, dm_ref, w_ref, m_ref, v_ref, g_out, d_out, m_out, v_out):
        sc = sct_ref[...].astype(BF16).astype(F32)
        dm = dm_ref[...].astype(BF16).astype(F32)
        g = sc[:, 0:1] * dm[0:1, :]
        for b in range(1, N_DEV):
            g = g + sc[:, b:b + 1] * dm[b:b + 1, :]
        delta, mn, vn = _adamw_math(w_ref[...], g, m_ref[...], v_ref[...])
        g_out[...] = g
        d_out[...] = delta
        m_out[...] = mn
        v_out[...] = vn

    blk = pl.BlockSpec((None, tr, n), lambda l, i: (l, i, 0))
    return pl.pallas_call(
        body, name=name,
        grid=(nl, d // tr),
        in_specs=[pl.BlockSpec((tr, N_DEV), lambda l, i: (i, 0)), pl.BlockSpec((None, N_DEV, n), lambda l, i: (l, 0, 0)),
                  blk, blk, blk],
        out_specs=[blk] * 4,
        out_shape=[jax.ShapeDtypeStruct((nl, d, n), F32)] * 4,
        compiler_params=pltpu.CompilerParams(dimension_semantics=("arbitrary", "arbitrary"),
                                             vmem_limit_bytes=_vmem_limit(2 * 8 * tr * n * 4 + (8 << 20))),
    )(sct, dmod, w, m, v)


def _reduce4(recvs, name):
    nl = len(recvs)
    shape = recvs[0].shape[1:]
    c = shape[-1]
    r = math.prod(shape[:-1])
    tr = _adam_rows(r, c)
    nt = r // tr

    def body(*refs):
        o_ref = refs[nl]
        for l in range(nl):
            @pl.when(pl.program_id(0) == l)
            def _():
                acc = refs[l][0].astype(F32)
                for k in range(1, N_CHIPS):
                    acc = acc + refs[l][k].astype(F32)
                o_ref[...] = acc

    def in_map(l):
        return lambda ll, i: (0, jnp.where(ll < l, 0, jnp.where(ll > l, nt - 1, i)), 0)

    return pl.pallas_call(
        body, name=name,
        grid=(nl, nt),
        in_specs=[pl.BlockSpec((N_CHIPS, tr, c), in_map(l)) for l in range(nl)],
        out_specs=pl.BlockSpec((None, tr, c), lambda ll, i: (ll, i, 0)),
        out_shape=jax.ShapeDtypeStruct((nl, r, c), F32),
        compiler_params=pltpu.CompilerParams(dimension_semantics=("arbitrary", "arbitrary"),
                                             vmem_limit_bytes=_vmem_limit(2 * 8 * nl * tr * max(c, 128) * 4 + (8 << 20))),
    )(*[rv.reshape(N_CHIPS, r, c) for rv in recvs]).reshape((nl,) + shape)


def _my_place():
    return lax.axis_index("x"), lax.axis_index("y"), lax.axis_index("c")


def _chip_coords(j):
    return j // 2, j % 2


def _mod_forward(c, mod_w, mod_b4):
    nl, d, n = mod_w.shape
    kc = 256

    def body(c_ref, w_ref, b_ref, mod_ref, sc_ref, cbuf, stage, s1, r1, s2, r2):
        mx, my, mc = _my_place()
        me = 4 * mx + 2 * my + mc
        q = 2 * mx + my
        cv = c_ref[...]
        cbuf[me] = jnp.broadcast_to(cv * jax.nn.sigmoid(cv), (8, d))
        sends = []
        for t in range(N_DEV):
            tx, ty = _chip_coords(t // 2)
            cp = pltpu.make_async_remote_copy(src_ref=cbuf.at[me], dst_ref=cbuf.at[me], send_sem=s1.at[t], recv_sem=r1.at[me],
                                              device_id=(tx, ty, t % 2), device_id_type=MESH)

            @pl.when(t != me)
            def _():
                cp.start()

            sends.append((t, cp))
        for t in range(N_DEV):
            @pl.when(t != me)
            def _():
                pltpu.make_async_remote_copy(src_ref=cbuf.at[t], dst_ref=cbuf.at[t], send_sem=s1.at[t], recv_sem=r1.at[t],
                                             device_id=(mx, my, mc), device_id_type=MESH).wait_recv()
        for t, cp in sends:
            @pl.when(t != me)
            def _():
                cp.wait_send()

        row = lax.broadcasted_iota(jnp.int32, (8, d), 0)
        sc_all = jnp.zeros((8, d), F32)
        for t in range(N_DEV):
            sc_all = sc_all + jnp.where(row == t, cbuf[t], 0.0)
        sc_ref[...] = sc_all
        rown = lax.broadcasted_iota(jnp.int32, (8, n), 0)
        for l in range(nl):
            acc = jnp.zeros((8, n), F32)
            for k0 in range(0, d, kc):
                acc = acc + _dot(sc_all[:, k0:k0 + kc].astype(BF16), w_ref[l, k0:k0 + kc, :].astype(BF16))
            acc = acc + b_ref[l, q]
            for j in range(N_CHIPS):
                jx, jy = _chip_coords(j)
                bdest = 4 * jx + 2 * jy + mc
                rowv = jnp.sum(jnp.where(rown == bdest, acc, 0.0), axis=0, keepdims=True)
                stage[j, l] = jnp.broadcast_to(rowv, (8, n))
        sends2 = []
        for j in range(N_CHIPS):
            jx, jy = _chip_coords(j)
            cp = pltpu.make_async_remote_copy(src_ref=stage.at[j], dst_ref=mod_ref.at[:, q], send_sem=s2.at[j], recv_sem=r2.at[q],
                                              device_id=(jx, jy, mc), device_id_type=MESH)

            @pl.when(j != q)
            def _():
                cp.start()

            @pl.when(j == q)
            def _():
                for l in range(nl):
                    mod_ref[l, j] = stage[j, l]

            sends2.append((j, cp))
        for j in range(N_CHIPS):
            @pl.when(j != q)
            def _():
                pltpu.make_async_remote_copy(src_ref=stage.at[j], dst_ref=mod_ref.at[:, j], send_sem=s2.at[j], recv_sem=r2.at[j],
                                             device_id=(mx, my, mc), device_id_type=MESH).wait_recv()
        for j, cp in sends2:
            @pl.when(j != q)
            def _():
                cp.wait_send()

    vm = pl.BlockSpec(memory_space=pltpu.VMEM)
    return pl.pallas_call(
        body, name="mod_forward",
        in_specs=[vm, vm, vm],
        out_specs=[vm, vm],
        out_shape=[jax.ShapeDtypeStruct((nl, N_CHIPS, 8, n), F32), jax.ShapeDtypeStruct((8, d), F32)],
        scratch_shapes=[pltpu.VMEM((N_DEV, 8, d), F32), pltpu.VMEM((N_CHIPS, nl, 8, n), F32),
                        pltpu.SemaphoreType.DMA((N_DEV,)), pltpu.SemaphoreType.DMA((N_DEV,)),
                        pltpu.SemaphoreType.DMA((N_CHIPS,)), pltpu.SemaphoreType.DMA((N_CHIPS,))],
        compiler_params=pltpu.CompilerParams(vmem_limit_bytes=_vmem_limit(2 * nl * d * n * 4 + (8 << 20))),
    )(c, mod_w, mod_b4)


_HBM_SPEC = pl.BlockSpec(memory_space=pltpu.HBM)
_SEM_SPEC = pl.BlockSpec(memory_space=pltpu.SEMAPHORE)
_DATAFLOW = pltpu.SideEffectType.DATAFLOW_SIDE_EFFECTING


def _slot(ref, scatter, j):
    return ref.at[j] if scatter else ref


def _exchange_start(groups, scatter, after, name):
    flat = [a for g in groups for a in g]
    na = len(flat)
    ng = len(groups)
    sizes = [len(g) for g in groups]
    first = [sum(sizes[:g]) for g in range(ng)]
    where = [(g, k) for g in range(ng) for k in range(sizes[g])]
    mx, my, _ = _my_place()
    qo = 2 * mx + my
    lands = []
    for a in flat:
        own = lax.dynamic_index_in_dim(a, qo, 0, keepdims=False) if scatter else a
        lands.append(lax.dynamic_update_index_in_dim(lax.empty((N_CHIPS,) + own.shape, a.dtype), own, qo, 0))

    def body(*refs):
        ins, lnd = refs[:na], refs[na:2 * na]
        ssems, rsems = refs[2 * na + 1:2 * na + 1 + ng], refs[2 * na + 1 + ng:2 * na + 1 + 2 * ng]
        token = refs[-1]
        mx, my, mc = _my_place()
        q = 2 * mx + my
        for j in range(N_CHIPS):
            jx, jy = _chip_coords(j)
            for a in range(na):
                g, k = where[a]

                @pl.when(j != q)
                def _():
                    pltpu.make_async_remote_copy(src_ref=_slot(ins[a], scatter, j), dst_ref=lnd[a].at[q],
                                                 send_sem=ssems[g].at[k * N_CHIPS + j], recv_sem=rsems[g].at[k * N_CHIPS + q],
                                                 device_id=(jx, jy, mc), device_id_type=MESH).start()
        token[...] = jnp.zeros_like(token)

    sem_shapes = [pltpu.SemaphoreType.DMA((n * N_CHIPS,)) for n in sizes]
    outs = pl.pallas_call(
        body, name=name,
        in_specs=[_HBM_SPEC] * (2 * na) + [pl.BlockSpec(memory_space=pl.ANY)],
        out_specs=[_SEM_SPEC] * (2 * ng) + [_HBM_SPEC] * (2 * na) + [pl.BlockSpec(memory_space=pltpu.VMEM)],
        out_shape=sem_shapes + sem_shapes + [pltpu.HBM(a.shape, a.dtype) for a in flat + lands]
        + [jax.ShapeDtypeStruct((8, 128), F32)],
        input_output_aliases={i: 2 * ng + i for i in range(2 * na)},
        compiler_params=pltpu.CompilerParams(has_side_effects=_DATAFLOW),
    )(*[pltpu.with_memory_space_constraint(a, pltpu.HBM) for a in flat + lands], after)
    ssems, rsems = outs[:ng], outs[ng:2 * ng]
    src_thru, land_thru = outs[2 * ng:2 * ng + na], outs[2 * ng + na:2 * ng + 2 * na]
    states = [(src_thru[first[g]:first[g] + sizes[g]], land_thru[first[g]:first[g] + sizes[g]], ssems[g], rsems[g])
              for g in range(ng)]
    return states, outs[-1]


def _exchange_wait(state, scatter, after, name):
    srcs, lands, ssem, rsem = state
    na = len(srcs)

    def body(*refs):
        ins, lnd = refs[:na], refs[na:2 * na]
        ssem_ref, rsem_ref = refs[2 * na], refs[2 * na + 1]
        mx, my, mc = _my_place()
        q = 2 * mx + my
        for j in range(N_CHIPS):
            for a in range(na):
                @pl.when(j != q)
                def _():
                    cp = pltpu.make_async_remote_copy(src_ref=_slot(ins[a], scatter, j), dst_ref=lnd[a].at[j],
                                                      send_sem=ssem_ref.at[a * N_CHIPS + j], recv_sem=rsem_ref.at[a * N_CHIPS + j],
                                                      device_id=(mx, my, mc), device_id_type=MESH)
                    cp.wait_send()
                    cp.wait_recv()

    outs = pl.pallas_call(
        body, name=name,
        in_specs=[_HBM_SPEC] * (2 * na) + [_SEM_SPEC, _SEM_SPEC, pl.BlockSpec(memory_space=pl.ANY)],
        out_specs=[_HBM_SPEC] * (2 * na),
        out_shape=[pltpu.HBM(a.shape, a.dtype) for a in list(srcs) + list(lands)],
        input_output_aliases={i: i for i in range(2 * na)},
        compiler_params=pltpu.CompilerParams(has_side_effects=_DATAFLOW),
    )(*srcs, *lands, ssem, rsem, after)
    return outs[na:]


def _swap_with_sibling(arrs):
    na = len(arrs)

    def body(*refs):
        ins, outs = refs[:na], refs[na:2 * na]
        ssem, rsem = refs[2 * na:]
        mx, my, mc = _my_place()
        cps = [pltpu.make_async_remote_copy(src_ref=ins[a], dst_ref=outs[a], send_sem=ssem.at[a], recv_sem=rsem.at[a],
                                            device_id=(mx, my, 1 - mc), device_id_type=MESH) for a in range(na)]
        for cp in cps:
            cp.start()
        for cp in cps:
            cp.wait()

    hbm = pl.BlockSpec(memory_space=pl.ANY)
    return pl.pallas_call(
        body, name="swap_sibling",
        in_specs=[hbm] * na,
        out_specs=[hbm] * na,
        out_shape=[jax.ShapeDtypeStruct(a.shape, a.dtype) for a in arrs],
        scratch_shapes=[pltpu.SemaphoreType.DMA((na,)), pltpu.SemaphoreType.DMA((na,))],
    )(*arrs)


def _allreduce_small(rows_all, rows_sum):
    ra, c = rows_all.shape
    r = rows_sum.shape[0]
    ch = r // N_DEV
    assert ch % 8 == 0 and ch * N_DEV == r

    def body(a_ref, s_ref, all_ref, sum_ref, rbuf, red, sa, rva, sb, rvb, sc, rvc):
        mx, my, mc = _my_place()
        me = 4 * mx + 2 * my + mc
        mine = pl.ds(pl.multiple_of(me * ch, 8), ch)
        all_ref[me] = a_ref[...]
        rbuf[me] = s_ref[mine, :]

        def dev(t):
            tx, ty = _chip_coords(t // 2)
            return (tx, ty, t % 2)

        def everyone_else(fn):
            for t in range(N_DEV):
                @pl.when(t != me)
                def _():
                    fn(t)

        def copy_a(t, slot):
            return pltpu.make_async_remote_copy(src_ref=a_ref, dst_ref=all_ref.at[slot], send_sem=sa.at[t], recv_sem=rva.at[slot],
                                                device_id=dev(t), device_id_type=MESH)

        def copy_b(t, slot):
            return pltpu.make_async_remote_copy(src_ref=s_ref.at[pl.ds(t * ch, ch), :], dst_ref=rbuf.at[slot], send_sem=sb.at[t],
                                                recv_sem=rvb.at[slot], device_id=dev(t), device_id_type=MESH)

        def copy_c(t, chunk_start, slot):
            return pltpu.make_async_remote_copy(src_ref=red, dst_ref=sum_ref.at[pl.ds(chunk_start, ch), :], send_sem=sc.at[t],
                                                recv_sem=rvc.at[slot], device_id=dev(t), device_id_type=MESH)

        everyone_else(lambda t: (copy_a(t, me).start(), copy_b(t, me).start()))
        everyone_else(lambda t: (copy_a(t, t).wait_recv(), copy_b(t, t).wait_recv()))
        everyone_else(lambda t: (copy_a(t, me).wait_send(), copy_b(t, me).wait_send()))
        acc = rbuf[0]
        for t in range(1, N_DEV):
            acc = acc + rbuf[t]
        red[...] = acc
        sum_ref[mine, :] = acc
        everyone_else(lambda t: copy_c(t, pl.multiple_of(me * ch, 8), me).start())
        everyone_else(lambda t: copy_c(t, t * ch, t).wait_recv())
        everyone_else(lambda t: copy_c(t, pl.multiple_of(me * ch, 8), me).wait_send())

    vm = pl.BlockSpec(memory_space=pltpu.VMEM)
    return pl.pallas_call(
        body, name="allreduce_small",
        in_specs=[vm, vm],
        out_specs=[vm, vm],
        out_shape=[jax.ShapeDtypeStruct((N_DEV, ra, c), F32), jax.ShapeDtypeStruct((r, c), F32)],
        scratch_shapes=[pltpu.VMEM((N_DEV, ch, c), F32), pltpu.VMEM((ch, c), F32)] + [pltpu.SemaphoreType.DMA((N_DEV,))] * 6,
        compiler_params=pltpu.CompilerParams(vmem_limit_bytes=_vmem_limit((3 * r + 2 * N_DEV * ra) * c * 4 + (4 << 20))),
    )(rows_all, rows_sum)


def _pack(arrs, row_multiple=8):
    rows, layout, at = [], [], 0
    for a in arrs:
        n = a.size
        nr = -(-n // (8 * SMALL_COLS)) * 8
        flat = a.reshape(-1)
        if nr * SMALL_COLS != n:
            flat = jnp.pad(flat, (0, nr * SMALL_COLS - n))
        rows.append(flat.reshape(nr, SMALL_COLS))
        layout.append((at, nr, a.shape))
        at += nr
    pad = -at % row_multiple
    if pad:
        rows.append(jnp.zeros((pad, SMALL_COLS), F32))
    return jnp.concatenate(rows, axis=0), layout


def _unpack(buf, layout):
    out = []
    for at, nr, shape in layout:
        n = math.prod(shape)
        out.append(buf[at:at + nr].reshape(-1)[:n].reshape(shape))
    return out


SMALL_NAMES = ("mod_b", "mix_pre_g", "mix_post_g", "sgu_norm_g", "sgu_norm_b", "sgu_w", "sgu_b", "conv_b", "conv_norm_g",
               "conv_norm_b", "pool_w", "pool_scale", "branch_g", "ffn_pre_g", "ffn_post_g", "ffn_conv_b")
SHARDED_SMALL = ("conv_w", "ffn_conv_w")
WEIGHT_ORDER = ("mod_w", "mod_b", "mix_pre_g", "mix_post_g", "w_in", "sgu_norm_g", "sgu_norm_b", "sgu_w", "sgu_b", "conv_w",
                "conv_b", "conv_norm_g", "conv_norm_b", "pool_w", "pool_scale", "branch_g", "w_out", "ffn_pre_g", "ffn_post_g",
                "ffn_up", "ffn_conv_w", "ffn_conv_b", "ffn_down")


def _block_diag(blocks):
    n, a, b = blocks.shape
    eye = jnp.eye(n, dtype=blocks.dtype)
    return (eye[:, None, :, None] * blocks[:, :, None, :]).reshape(n * a, n * b)


def _diag_blocks(mat, n):
    a = mat.shape[0] // n
    return jnp.stack([mat[g * a:(g + 1) * a, g * a:(g + 1) * a] for g in range(n)])


def _step(x, c, loss_target, w, m, v):
    nl = w["mod_w"].shape[0]
    s, d = x.shape[1], x.shape[2]
    heads = SGU_WIDTH // HEAD_DIM
    groups = len(POOL_WINDOWS)
    mx, my, _ = _my_place()
    q = 2 * mx + my
    x0 = x.reshape(s, d)
    tgt = loss_target.reshape(s, d)

    nmod = w["mod_w"].shape[2]
    kin = w["w_in"].shape[2]
    inw = kin * N_CHIPS
    f2 = w["ffn_up"].shape[2] * N_CHIPS
    f = f2 // 2
    wgroups = []
    for l in range(nl):
        wgroups += [[w["w_in"][l].astype(BF16), w["conv_w"][l], w["ffn_conv_w"][l]], [w["w_out"][l].astype(BF16)],
                    [w["ffn_up"][l].astype(BF16)], [w["ffn_down"][l].astype(BF16)]]
    mod4, sc_all = _mod_forward(c, w["mod_w"], w["mod_b"].reshape(nl, N_CHIPS, 1, nmod))
    mod = mod4[:, :, 0, :].reshape(nl, N_MOD, 1, d)
    gstates, gtoken = _exchange_start(wgroups, False, mod4, "gather_start")

    tril = jnp.tril(jnp.ones((CHUNK, CHUNK), bool))
    bd = _block_diag(jnp.ones((heads, HEAD_DIM, HEAD_DIM), BF16))

    def mixer_params(l, conv_w):
        wm = jnp.where(tril[None], w["sgu_w"][l], 0.0)
        pw = _block_diag(w["pool_w"][l])
        return dict(
            bd=bd, ng=w["sgu_norm_g"][l][None], nb=w["sgu_norm_b"][l][None],
            wm=wm.astype(BF16), wmt=jnp.swapaxes(wm, 1, 2).astype(BF16),
            bias=jnp.repeat(w["sgu_b"][l].T, HEAD_DIM, axis=1),
            cw=conv_w, cb=w["conv_b"][l][None], cng=w["conv_norm_g"][l][None], cnb=w["conv_norm_b"][l][None],
            pw=pw.astype(BF16), pwt=pw.T.astype(BF16), ps=w["pool_scale"][l][None], bg=w["branch_g"][l][None])

    saved = []
    xl = x0
    prev_done = gtoken
    for l in range(nl):
        sh1, sc1, g1, sh2, sc2, g2 = [mod[l, k] for k in range(N_MOD)]
        gpre1, gpost1 = w["mix_pre_g"][l][None], w["mix_post_g"][l][None]
        gpre2, gpost2 = w["ffn_pre_g"][l][None], w["ffn_post_g"][l][None]
        g_win, g_cw, g_fcw = _exchange_wait(gstates[4 * l], False, prev_done, f"gather_wait_in_{l}")
        w_in = jnp.transpose(g_win, (1, 0, 2)).reshape(d, inw)
        conv_w = jnp.transpose(g_cw, (1, 0, 2)).reshape(CONV_K, CONV_WIDTH)
        ffn_cw = jnp.transpose(g_fcw, (1, 0, 2)).reshape(FFN_CONV_K, f2)
        mp = mixer_params(l, conv_w)
        z, h1 = _norm_mod_matmul(xl, gpre1, 1.0 + sc1, sh1, w_in[None], f"mix_in_{l}")
        (g_wout,) = _exchange_wait(gstates[4 * l + 1], False, z, f"gather_wait_out_{l}")
        w_out = g_wout.reshape(d, d)
        ycat = _mixer_fwd(z, mp, f"mixer_fwd_{l}")
        (up,) = _exchange_wait(gstates[4 * l + 2], False, ycat, f"gather_wait_up_{l}")
        o, x1 = _matmul_norm_resid(ycat, w_out, xl, g1, gpost1, f"mix_out_{l}")
        (g_down,) = _exchange_wait(gstates[4 * l + 3], False, x1, f"gather_wait_down_{l}")
        down = g_down.reshape(f, d)
        p, h2 = _norm_mod_matmul(x1, gpre2, 1.0 + sc2, sh2, up, f"ffn_in_{l}")
        fcb = w["ffn_conv_b"][l][None]
        act = _ffn_act_fwd(p, ffn_cw, fcb, f"ffn_act_{l}")
        qo, x2 = _matmul_norm_resid(act, down, x1, g2, gpost2, f"ffn_out_{l}")
        saved.append(dict(x=xl, z=z, h1=h1, ycat=ycat, o=o, x1=x1, p=p, h2=h2, act=act, qo=qo, mp=mp, fcb=fcb,
                          w_in=w_in, w_out=w_out, up=up, down=down, ffn_cw=ffn_cw,
                          mods=(sh1, sc1, g1, sh2, sc2, g2), gains=(gpre1, gpost1, gpre2, gpost2)))
        xl = x2
        prev_done = p

    dx, loss_row = _loss_head(xl, tgt)

    small = {n: [None] * nl for n in SMALL_NAMES + SHARDED_SMALL}
    dmods = [None] * nl
    tn = f2 // N_CHIPS
    sstates = {}
    token = None
    for l in reversed(range(nl)):
        sv = saved[l]
        sh1, sc1, g1, sh2, sc2, g2 = sv["mods"]
        gpre1, gpost1, gpre2, gpost2 = sv["gains"]
        if token is not None:
            g2 = g2 + token[0:1, 0:1]
        dq, dact, dg2, dgpost2 = _resid_bwd_matmul(dx, sv["qo"], g2, gpost2, sv["down"], f"ffn_out_bwd_{l}")
        g_down = _wgrad(sv["act"], dq, (f, lambda j: 0), (d, lambda j: 0), jax.ShapeDtypeStruct((f, d), BF16),
                        (1, lambda j: (0, 0)), (f, d), f"wgrad_ffn_down_{l}")
        dug, duv, dfwg, dfwv, dfbg, dfbv = _ffn_act_bwd(sv["p"], dact, sv["ffn_cw"], sv["fcb"], f"ffn_act_bwd_{l}")
        dx1, dp, dsh2, dsc2, dgpre2 = _ffn_in_bwd(dug, duv, sv["ffn_cw"], sv["up"], sv["x1"], dx, gpre2, 1.0 + sc2,
                                                  f"ffn_in_bwd_{l}")
        g_up = _wgrad(sv["h2"], dp, (d, lambda j: 0), (tn, lambda j: j), jax.ShapeDtypeStruct((N_CHIPS, d, tn), BF16),
                      (N_CHIPS, lambda j: (j, 0, 0)), (None, d, tn), f"wgrad_ffn_up_{l}")
        (sstates[l, 0],), token = _exchange_start([[g_down.reshape(N_CHIPS, f // N_CHIPS, d), g_up]], True, g_up,
                                                  f"scatter_start_ffn_{l}")
        do, dycat, dg1, dgpost1 = _resid_bwd_matmul(dx1, sv["o"], g1 + token[0:1, 0:1], gpost1, sv["w_out"],
                                                    f"mix_out_bwd_{l}")
        g_out = _wgrad(sv["ycat"], do, (d, lambda j: 0), (d, lambda j: 0), jax.ShapeDtypeStruct((d, d), BF16),
                       (1, lambda j: (0, 0)), (d, d), f"wgrad_w_out_{l}")
        (sstates[l, 1],), token = _exchange_start([[g_out.reshape(N_CHIPS, d // N_CHIPS, d)]], True, g_out,
                                                  f"scatter_start_out_{l}")
        mp_after = dict(sv["mp"], bg=sv["mp"]["bg"] + token[0:1, 0:1])
        (dza, dcb, dd, dbg, dwm, dbias, dng, dnb, dcng, dcnb, dps, dpw) = _mixer_bwd_a(sv["z"], dycat, mp_after, f"mixer_bwd_a_{l}")
        dx, dz, dsh1, dsc1, dgpre1, dcw, dcbias = _mixer_bwd_b(
            sv["z"], dza, dcb, dd, sv["x"], dx1, gpre1, 1.0 + sc1, sv["mp"]["cw"], sv["w_in"], f"mixer_bwd_b_{l}")
        g_in = _wgrad(sv["h1"], dz, (d, lambda j: 0), (inw, lambda j: 0), jax.ShapeDtypeStruct((d, inw), BF16),
                      (1, lambda j: (0, 0)), (d, inw), f"wgrad_w_in_{l}")
        g_in_parts = jnp.transpose(g_in.reshape(d, N_CHIPS, kin), (1, 0, 2))
        if l > 0:
            (sstates[l, 2],), token = _exchange_start([[g_in_parts]], True, g_in_parts, f"scatter_start_in_{l}")

        dmods[l] = jnp.concatenate([dsh1, dsc1, dg1, dsh2, dsc2, dg2], axis=0)
        small["mix_pre_g"][l], small["mix_post_g"][l] = dgpre1[0], dgpost1[0]
        small["ffn_pre_g"][l], small["ffn_post_g"][l] = dgpre2[0], dgpost2[0]
        small["sgu_norm_g"][l], small["sgu_norm_b"][l] = dng[0], dnb[0]
        small["sgu_w"][l] = jnp.where(tril[None], dwm, 0.0)
        small["sgu_b"][l] = dbias.reshape(CHUNK, heads, HEAD_DIM).sum(-1).T
        small["conv_b"][l], small["conv_norm_g"][l], small["conv_norm_b"][l] = dcbias[0], dcng[0], dcnb[0]
        small["pool_w"][l], small["pool_scale"][l], small["branch_g"][l] = _diag_blocks(dpw, groups), dps[0], dbg[0]
        small["ffn_conv_b"][l] = jnp.concatenate([dfbg[0], dfbv[0]])
        small["conv_w"][l] = dcw
        small["ffn_conv_w"][l] = jnp.concatenate([dfwg, dfwv], axis=1)

    names = [n for n in SMALL_NAMES if n != "mod_b"] + list(SHARDED_SMALL)
    dmod_rows, _ = _pack([jnp.stack(dmods)])
    packed, layout = _pack([jnp.stack(dmods), loss_row] + [jnp.stack(small[n]) for n in names], 8 * N_DEV)
    gathered, summed = _allreduce_small(dmod_rows, packed)
    (sstates[0, 2],), token = _exchange_start([[g_in_parts]], True, summed, "scatter_start_in_0")
    parts = _unpack(summed, layout)
    loss = parts[1][0, 0]
    gsmall = dict(zip(names, parts[2:]))
    gsmall["mod_b"] = parts[0].reshape(nl, N_MOD * d)
    dmod_all = gathered[:, :nl * N_MOD].reshape(N_DEV, nl, N_MOD * d)
    dmod_mine = jnp.transpose(lax.dynamic_slice_in_dim(dmod_all, q * nmod, nmod, axis=2), (1, 0, 2))

    grads, deltas, new_m, new_v = {}, {}, {}, {}

    def put(name, res):
        grads[name], deltas[name], new_m[name], new_v[name] = res

    put("mod_w", _modw_adamw(sc_all.T + token[0:1, 0:1], dmod_mine, w["mod_w"], m["mod_w"], v["mod_w"], "adamw_mod_w"))

    pw_, lay = _pack([w[n] for n in SMALL_NAMES])
    pm_, _ = _pack([m[n] for n in SMALL_NAMES])
    pv_, _ = _pack([v[n] for n in SMALL_NAMES])
    pg_, _ = _pack([gsmall[n] for n in SMALL_NAMES])
    res = _adamw(pw_, pm_, pv_, [pg_], "adamw_small")
    for n, g_, d_, m_, v_ in zip(SMALL_NAMES, *[_unpack(r_, lay) for r_ in res]):
        put(n, (g_, d_, m_, v_))

    gsh = {"conv_w": lax.dynamic_slice_in_dim(gsmall["conv_w"], q * (CONV_WIDTH // N_CHIPS), CONV_WIDTH // N_CHIPS, axis=2),
           "ffn_conv_w": lax.dynamic_slice_in_dim(gsmall["ffn_conv_w"], q * (f2 // N_CHIPS), f2 // N_CHIPS, axis=2)}
    pw_, lay = _pack([w[n] for n in SHARDED_SMALL])
    pm_, _ = _pack([m[n] for n in SHARDED_SMALL])
    pv_, _ = _pack([v[n] for n in SHARDED_SMALL])
    pg_, _ = _pack([gsh[n] for n in SHARDED_SMALL])
    res = _adamw(pw_, pm_, pv_, [pg_], "adamw_sharded_small")
    for n, g_, d_, m_, v_ in zip(SHARDED_SMALL, *[_unpack(r_, lay) for r_ in res]):
        put(n, (g_, d_, m_, v_))

    recv = dict(w_in=[None] * nl, w_out=[None] * nl, ffn_up=[None] * nl, ffn_down=[None] * nl)
    done = grads["mod_w"]
    for l in reversed(range(nl)):
        recv["ffn_down"][l], recv["ffn_up"][l] = _exchange_wait(sstates[l, 0], True, done, f"scatter_wait_ffn_{l}")
        (recv["w_out"][l],) = _exchange_wait(sstates[l, 1], True, recv["ffn_up"][l], f"scatter_wait_out_{l}")
        (recv["w_in"][l],) = _exchange_wait(sstates[l, 2], True, recv["w_out"][l], f"scatter_wait_in_{l}")
        done = recv["w_in"][l]
    big = ("w_in", "w_out", "ffn_up", "ffn_down")
    mine = [_reduce4(recv[n], f"reduce4_{n}") for n in big]
    theirs = _swap_with_sibling(mine)
    for n, a, b in zip(big, mine, theirs):
        put(n, _adamw(w[n], m[n], v[n], [a, b], f"adamw_{n}"))

    return (loss, dx.reshape(1, s, d), *[grads[n] for n in WEIGHT_ORDER], *[deltas[n] for n in WEIGHT_ORDER],
            *[new_m[n] for n in WEIGHT_ORDER], *[new_v[n] for n in WEIGHT_ORDER])


def kernel(x, c, mod_w, mod_b, mix_pre_g, mix_post_g, w_in, sgu_norm_g, sgu_norm_b, sgu_w, sgu_b, conv_w, conv_b, conv_norm_g, conv_norm_b, pool_w, pool_scale, branch_g, w_out, ffn_pre_g, ffn_post_g, ffn_up, ffn_conv_w, ffn_conv_b, ffn_down, loss_target, m_mod_w, m_mod_b, m_mix_pre_g, m_mix_post_g, m_w_in, m_sgu_norm_g, m_sgu_norm_b, m_sgu_w, m_sgu_b, m_conv_w, m_conv_b, m_conv_norm_g, m_conv_norm_b, m_pool_w, m_pool_scale, m_branch_g, m_w_out, m_ffn_pre_g, m_ffn_post_g, m_ffn_up, m_ffn_conv_w, m_ffn_conv_b, m_ffn_down, v_mod_w, v_mod_b, v_mix_pre_g, v_mix_post_g, v_w_in, v_sgu_norm_g, v_sgu_norm_b, v_sgu_w, v_sgu_b, v_conv_w, v_conv_b, v_conv_norm_g, v_conv_norm_b, v_pool_w, v_pool_scale, v_branch_g, v_w_out, v_ffn_pre_g, v_ffn_post_g, v_ffn_up, v_ffn_conv_w, v_ffn_conv_b, v_ffn_down):
    w = dict(mod_w=mod_w, mod_b=mod_b, mix_pre_g=mix_pre_g, mix_post_g=mix_post_g, w_in=w_in, sgu_norm_g=sgu_norm_g,
             sgu_norm_b=sgu_norm_b, sgu_w=sgu_w, sgu_b=sgu_b, conv_w=conv_w, conv_b=conv_b, conv_norm_g=conv_norm_g,
             conv_norm_b=conv_norm_b, pool_w=pool_w, pool_scale=pool_scale, branch_g=branch_g, w_out=w_out,
             ffn_pre_g=ffn_pre_g, ffn_post_g=ffn_post_g, ffn_up=ffn_up, ffn_conv_w=ffn_conv_w, ffn_conv_b=ffn_conv_b,
             ffn_down=ffn_down)
    m = dict(mod_w=m_mod_w, mod_b=m_mod_b, mix_pre_g=m_mix_pre_g, mix_post_g=m_mix_post_g, w_in=m_w_in,
             sgu_norm_g=m_sgu_norm_g, sgu_norm_b=m_sgu_norm_b, sgu_w=m_sgu_w, sgu_b=m_sgu_b, conv_w=m_conv_w,
             conv_b=m_conv_b, conv_norm_g=m_conv_norm_g, conv_norm_b=m_conv_norm_b, pool_w=m_pool_w,
             pool_scale=m_pool_scale, branch_g=m_branch_g, w_out=m_w_out, ffn_pre_g=m_ffn_pre_g, ffn_post_g=m_ffn_post_g,
             ffn_up=m_ffn_up, ffn_conv_w=m_ffn_conv_w, ffn_conv_b=m_ffn_conv_b, ffn_down=m_ffn_down)
    v = dict(mod_w=v_mod_w, mod_b=v_mod_b, mix_pre_g=v_mix_pre_g, mix_post_g=v_mix_post_g, w_in=v_w_in,
             sgu_norm_g=v_sgu_norm_g, sgu_norm_b=v_sgu_norm_b, sgu_w=v_sgu_w, sgu_b=v_sgu_b, conv_w=v_conv_w,
             conv_b=v_conv_b, conv_norm_g=v_conv_norm_g, conv_norm_b=v_conv_norm_b, pool_w=v_pool_w,
             pool_scale=v_pool_scale, branch_g=v_branch_g, w_out=v_w_out, ffn_pre_g=v_ffn_pre_g, ffn_post_g=v_ffn_post_g,
             ffn_up=v_ffn_up, ffn_conv_w=v_ffn_conv_w, ffn_conv_b=v_ffn_conv_b, ffn_down=v_ffn_down)
    return _step(x, c, loss_target, w, m, v)
```

```python
import functools
import math

import jax
import jax.numpy as jnp
from jax import lax
from jax.experimental import pallas as pl
from jax.experimental.pallas import tpu as pltpu

F32 = jnp.float32
BF16 = jnp.bfloat16
MESH = pl.DeviceIdType.MESH

EPS = 1e-6
HEAD_DIM = 64
CHUNK = 128
SGU_WIDTH = 384
CONV_WIDTH = 384
POOL_WIDTH = 256
POOL_WINDOWS = (2, 4, 8, 16)
CONV_K = 31
FFN_CONV_K = 3
N_MOD = 6
N_CHIPS = 4
N_DEV = 8

ADAM_LR = 0.001
ADAM_B1 = 0.9
ADAM_B2 = 0.999
ADAM_EPS = 1e-08
ADAM_WD = 0.01
ADAM_STEP = 10

MIX_HALO = 32
FFN_HALO = 8
CONV_ROWS = 32
SMALL_COLS = 1024
VMEM_BYTES_V7X = 64 * 1024 * 1024


def _vmem_limit(estimate_bytes):
    return int(min(max(estimate_bytes, 16 * 1024 * 1024), VMEM_BYTES_V7X - 8 * 1024 * 1024))


def _row_tile(s, want):
    return want if s % want == 0 else math.gcd(s, want)


def _rsum(v):
    return jnp.sum(v, axis=0, keepdims=True)


def _rmean(v):
    return jnp.mean(v, axis=-1, keepdims=True)


def _gelu(v):
    k = math.sqrt(2.0 / math.pi)
    return 0.5 * v * (1.0 + jnp.tanh(k * (v + 0.044715 * v * v * v)))


def _gelu_grad(v):
    k = math.sqrt(2.0 / math.pi)
    t = jnp.tanh(k * (v + 0.044715 * v * v * v))
    return 0.5 * (1.0 + t) + 0.5 * v * (1.0 - t * t) * (k * (1.0 + 3.0 * 0.044715 * v * v))


def _dot(a, b):
    return jnp.dot(a, b, preferred_element_type=F32)


def _dot_nt(a, b):
    return lax.dot_general(a, b, (((1,), (1,)), ((), ())), preferred_element_type=F32)


def _dot_tn(a, b):
    return lax.dot_general(a, b, (((0,), (0,)), ((), ())), preferred_element_type=F32)


def _group_mean(v, bd):
    hi = v.astype(BF16)
    lo = (v - hi.astype(F32)).astype(BF16)
    return (_dot(hi, bd) + _dot(lo, bd)) * (1.0 / HEAD_DIM)


def _const_spec(shape):
    nd = len(shape)
    return pl.BlockSpec(shape, lambda *_: (0,) * nd)


def _hbm_call(body, **kw):
    call = pl.pallas_call(body, **kw)
    return lambda *args: call(*[pltpu.with_memory_space_constraint(a, pltpu.HBM) for a in args])


def _norm_mod_matmul(x, gain, s1p, shift, w, name):
    s, d = x.shape
    nb, _, tn = w.shape
    ts = _row_tile(s, 512 if nb * tn <= 2048 else 256)

    def body(x_ref, g_ref, s_ref, b_ref, w_ref, z_ref, h_ref):
        xv = x_ref[...]
        r = lax.rsqrt(_rmean(xv * xv) + EPS)
        h = ((xv * r) * g_ref[...] * s_ref[...] + b_ref[...]).astype(BF16)
        h_ref[...] = h
        for j in range(nb):
            z_ref[:, j * tn:(j + 1) * tn] = _dot(h, w_ref[j])

    vec = pl.BlockSpec((1, d), lambda i: (0, 0))
    return _hbm_call(
        body, name=name,
        grid=(s // ts,),
        in_specs=[pl.BlockSpec((ts, d), lambda i: (i, 0)), vec, vec, vec,
                  pl.BlockSpec((nb, d, tn), lambda i: (0, 0, 0), pipeline_mode=pl.Buffered(1))],
        out_specs=[pl.BlockSpec((ts, nb * tn), lambda i: (i, 0)), pl.BlockSpec((ts, d), lambda i: (i, 0))],
        out_shape=[jax.ShapeDtypeStruct((s, nb * tn), F32), jax.ShapeDtypeStruct((s, d), BF16)],
        compiler_params=pltpu.CompilerParams(
            dimension_semantics=("arbitrary",),
            vmem_limit_bytes=_vmem_limit(2 * (ts * d * 4 + ts * nb * tn * 4 + ts * d * 2) + nb * d * tn * 2 + 4 * ts * d * 4)),
    )(x, gain, s1p, shift, w)


def _matmul_norm_resid(a, w, xres, gate, gpost, name):
    s, k = a.shape
    d = w.shape[1]
    ts = _row_tile(s, 512)

    def body(a_ref, w_ref, x_ref, gate_ref, gp_ref, o_ref, xn_ref):
        o = _dot(a_ref[...], w_ref[...])
        o_ref[...] = o
        r = lax.rsqrt(_rmean(o * o) + EPS)
        xn_ref[...] = x_ref[...] + gate_ref[...] * ((o * r) * gp_ref[...])

    vec = pl.BlockSpec((1, d), lambda i: (0, 0))
    row = pl.BlockSpec((ts, d), lambda i: (i, 0))
    return _hbm_call(
        body, name=name,
        grid=(s // ts,),
        in_specs=[pl.BlockSpec((ts, k), lambda i: (i, 0)),
                  pl.BlockSpec((k, d), lambda i: (0, 0), pipeline_mode=pl.Buffered(1)), row, vec, vec],
        out_specs=[row, row],
        out_shape=[jax.ShapeDtypeStruct((s, d), F32)] * 2,
        compiler_params=pltpu.CompilerParams(
            dimension_semantics=("arbitrary",),
            vmem_limit_bytes=_vmem_limit(2 * (ts * k * 2 + 3 * ts * d * 4) + k * d * 2 + 4 * ts * d * 4)),
    )(a, w, xres, gate, gpost)


def _wgrad(a, b, acols, bcols, out_struct, out_index, out_block, name):
    s = a.shape[0]
    ts = _row_tile(s, 512)
    aw, afn = acols
    bw, bfn = bcols
    nj = out_index[0]
    oidx = out_index[1]

    def body(a_ref, b_ref, o_ref, acc):
        i = pl.program_id(1)

        @pl.when(i == 0)
        def _():
            acc[...] = jnp.zeros_like(acc)

        acc[...] += _dot_tn(a_ref[...], b_ref[...])

        @pl.when(i == pl.num_programs(1) - 1)
        def _():
            o_ref[...] = acc[...].astype(o_ref.dtype)

    return _hbm_call(
        body, name=name,
        grid=(nj, s // ts),
        in_specs=[pl.BlockSpec((ts, aw), lambda j, i: (i, afn(j))), pl.BlockSpec((ts, bw), lambda j, i: (i, bfn(j)))],
        out_specs=pl.BlockSpec(out_block, lambda j, i: oidx(j)),
        out_shape=out_struct,
        scratch_shapes=[pltpu.VMEM((aw, bw), F32)],
        compiler_params=pltpu.CompilerParams(
            dimension_semantics=("arbitrary", "arbitrary"),
            vmem_limit_bytes=_vmem_limit(2 * (ts * aw * 2 + ts * bw * 2) + 3 * aw * bw * 4 + ts * aw * 4)),
    )(a, b)


def _loss_head(xo, tgt):
    s, d = xo.shape
    ts = _row_tile(s, 512)

    def body(x_ref, t_ref, dx_ref, l_ref, acc):
        i = pl.program_id(0)

        @pl.when(i == 0)
        def _():
            acc[...] = jnp.zeros_like(acc)

        e = x_ref[...] - t_ref[...]
        dx_ref[...] = e * (1.0 / d)
        acc[...] += _rsum(e * e)

        @pl.when(i == pl.num_programs(0) - 1)
        def _():
            tot = jnp.sum(acc[...], axis=-1, keepdims=True) * (0.5 / d)
            l_ref[...] = jnp.broadcast_to(tot, l_ref.shape)

    row = pl.BlockSpec((ts, d), lambda i: (i, 0))
    return _hbm_call(
        body, name="loss_head",
        grid=(s // ts,),
        in_specs=[row, row],
        out_specs=[row, pl.BlockSpec((1, SMALL_COLS), lambda i: (0, 0))],
        out_shape=[jax.ShapeDtypeStruct((s, d), F32), jax.ShapeDtypeStruct((1, SMALL_COLS), F32)],
        scratch_shapes=[pltpu.VMEM((1, d), F32)],
        compiler_params=pltpu.CompilerParams(dimension_semantics=("arbitrary",)),
    )(xo, tgt)


def _resid_bwd_matmul(dxn, o, gate, gpost, w, name):
    s, d = dxn.shape
    k = w.shape[0]
    ts = _row_tile(s, 512)

    def body(dx_ref, o_ref, gate_ref, gp_ref, w_ref, do_ref, da_ref, dgate_ref, dgp_ref):
        i = pl.program_id(0)

        @pl.when(i == 0)
        def _():
            dgate_ref[...] = jnp.zeros_like(dgate_ref)
            dgp_ref[...] = jnp.zeros_like(dgp_ref)

        dx = dx_ref[...]
        o = o_ref[...]
        r = lax.rsqrt(_rmean(o * o) + EPS)
        on = o * r
        dgate_ref[...] += _rsum(dx * (on * gp_ref[...]))
        don = dx * gate_ref[...]
        dgp_ref[...] += _rsum(don * on)
        t = don * gp_ref[...]
        do = (r * (t - on * _rmean(t * on))).astype(BF16)
        do_ref[...] = do
        da_ref[...] = _dot_nt(do, w_ref[...])

    vec = pl.BlockSpec((1, d), lambda i: (0, 0))
    row = pl.BlockSpec((ts, d), lambda i: (i, 0))
    return _hbm_call(
        body, name=name,
        grid=(s // ts,),
        in_specs=[row, row, vec, vec, pl.BlockSpec((k, d), lambda i: (0, 0), pipeline_mode=pl.Buffered(1))],
        out_specs=[row, pl.BlockSpec((ts, k), lambda i: (i, 0)), vec, vec],
        out_shape=[jax.ShapeDtypeStruct((s, d), BF16), jax.ShapeDtypeStruct((s, k), F32),
                   jax.ShapeDtypeStruct((1, d), F32), jax.ShapeDtypeStruct((1, d), F32)],
        compiler_params=pltpu.CompilerParams(
            dimension_semantics=("arbitrary",),
            vmem_limit_bytes=_vmem_limit(2 * (2 * ts * d * 4 + ts * d * 2 + ts * k * 4) + d * k * 2 + 6 * ts * d * 4)),
    )(dxn, o, gate, gpost, w)


def _norm_mod_bwd(dh, xv, gain, s1p, dres):
    r = lax.rsqrt(_rmean(xv * xv) + EPS)
    xn = xv * r
    dshift = _rsum(dh)
    t = dh * xn
    dscale = _rsum(t * gain)
    dgain = _rsum(t * s1p)
    dxn = dh * (gain * s1p)
    dx = r * (dxn - xn * _rmean(dxn * xn)) + dres
    return dx, dshift, dscale, dgain


def _lane_lt(shape, bound):
    return lax.broadcasted_iota(jnp.int32, shape, 1) < bound


def _sgu_forward(z_ref, bd_ref, ng_ref, nb_ref, wm_ref, bias_ref, ts, ya_s, f_s):
    u = _gelu(z_ref[:, 0:SGU_WIDTH])
    v = _gelu(z_ref[:, SGU_WIDTH:2 * SGU_WIDTH])
    bd = bd_ref[...]
    vc = v - _group_mean(v, bd)
    rstd = lax.rsqrt(_group_mean(vc * vc, bd) + EPS)
    vhat = vc * rstd
    vn = (vhat * ng_ref[...] + nb_ref[...]).astype(BF16)
    left = _lane_lt((CHUNK, CHUNK), HEAD_DIM)
    for n in range(ts // CHUNK):
        rows = slice(n * CHUNK, (n + 1) * CHUNK)
        for p in range(SGU_WIDTH // CHUNK):
            cols = slice(p * CHUNK, (p + 1) * CHUNK)
            blk = vn[rows, cols]
            f = jnp.where(left, _dot(wm_ref[2 * p], blk), _dot(wm_ref[2 * p + 1], blk)) + bias_ref[:, cols]
            if f_s is not None:
                f_s[rows, cols] = f
            ya_s[rows, cols] = u[rows, cols] * f
    return u, vhat, rstd, vn


def _conv31_forward(z_ref, zh_ref, first, cw_ref, cb_ref, ts, ext_b, cbs):
    a = z_ref[:, 2 * SGU_WIDTH:2 * SGU_WIDTH + CONV_WIDTH]
    g = z_ref[:, 2 * SGU_WIDTH + CONV_WIDTH:2 * SGU_WIDTH + 2 * CONV_WIDTH]
    ah = zh_ref[:, 2 * SGU_WIDTH:2 * SGU_WIDTH + CONV_WIDTH]
    gh = zh_ref[:, 2 * SGU_WIDTH + CONV_WIDTH:2 * SGU_WIDTH + 2 * CONV_WIDTH]
    ext_b[pl.ds(0, MIX_HALO), :] = jnp.where(first, 0.0, ah * jax.nn.sigmoid(gh))
    ext_b[pl.ds(MIX_HALO, ts), :] = a * jax.nn.sigmoid(g)
    for r in range(ts // CONV_ROWS):
        acc = jnp.broadcast_to(cb_ref[...], (CONV_ROWS, CONV_WIDTH))
        for k in range(CONV_K):
            acc = acc + cw_ref[k:k + 1, :] * ext_b[pl.ds(MIX_HALO - (CONV_K - 1) + k + r * CONV_ROWS, CONV_ROWS), :]
        cbs[pl.ds(r * CONV_ROWS, CONV_ROWS), :] = acc


def _pool_counts(i, ts):
    pos1 = (i * ts + 1 + lax.broadcasted_iota(jnp.int32, (ts, POOL_WIDTH), 0)).astype(F32)
    lane = lax.broadcasted_iota(jnp.int32, (ts, POOL_WIDTH), 1)
    gdim = POOL_WIDTH // len(POOL_WINDOWS)
    win = jnp.where(lane < gdim, float(POOL_WINDOWS[0]),
                    jnp.where(lane < 2 * gdim, float(POOL_WINDOWS[1]),
                              jnp.where(lane < 3 * gdim, float(POOL_WINDOWS[2]), float(POOL_WINDOWS[3]))))
    return jnp.minimum(pos1, win)


def _window_sums(ext, base, ts, sign):
    lane = lax.broadcasted_iota(jnp.int32, (ts, POOL_WIDTH), 1)
    gdim = POOL_WIDTH // len(POOL_WINDOWS)
    run = jnp.zeros((ts, POOL_WIDTH), F32)
    out = jnp.zeros((ts, POOL_WIDTH), F32)
    for m in range(POOL_WINDOWS[-1]):
        run = run + ext[pl.ds(base + sign * m, ts), :]
        for gi, win in enumerate(POOL_WINDOWS):
            if m == win - 1:
                out = jnp.where((lane >= gi * gdim) & (lane < (gi + 1) * gdim), run, out)
    return out


def _pool_forward(z_ref, zh_ref, first, i, ts, ext_c):
    c0 = 2 * SGU_WIDTH + 2 * CONV_WIDTH
    zc = z_ref[:, c0:c0 + POOL_WIDTH]
    ext_c[pl.ds(0, MIX_HALO), :] = jnp.where(first, 0.0, zh_ref[:, c0:c0 + POOL_WIDTH])
    ext_c[pl.ds(MIX_HALO, ts), :] = zc
    sums = _window_sums(ext_c, MIX_HALO, ts, -1)
    return sums / _pool_counts(i, ts) - zc


def _layer_norm_rows(v):
    mu = _rmean(v)
    vc = v - mu
    rstd = lax.rsqrt(_rmean(vc * vc) + EPS)
    return vc * rstd, rstd


def _mixer_specs(s, ts, width):
    nbh = ts // MIX_HALO
    tile = pl.BlockSpec((ts, width), lambda i: (i, 0))
    prev = pl.BlockSpec((MIX_HALO, width), lambda i: (jnp.maximum(i * nbh - 1, 0), 0))
    nxt = pl.BlockSpec((MIX_HALO, width), lambda i: (jnp.minimum((i + 1) * nbh, s // MIX_HALO - 1), 0))
    return tile, prev, nxt


def _mixer_fwd(z, mp, name):
    s, inw = z.shape
    d = SGU_WIDTH + CONV_WIDTH + POOL_WIDTH
    ts = _row_tile(s, 256)

    def body(z_ref, zh_ref, bd_ref, ng_ref, nb_ref, wm_ref, bias_ref, cw_ref, cb_ref, cng_ref, cnb_ref,
             pw_ref, ps_ref, bg_ref, y_ref, ya_s, ext_b, cbs, ext_c):
        i = pl.program_id(0)
        first = i == 0
        _sgu_forward(z_ref, bd_ref, ng_ref, nb_ref, wm_ref, bias_ref, ts, ya_s, None)
        ya = ya_s[...]
        ra = lax.rsqrt(_rmean(ya * ya) + EPS)
        y_ref[:, 0:SGU_WIDTH] = ((ya * ra) * bg_ref[:, 0:SGU_WIDTH]).astype(BF16)

        _conv31_forward(z_ref, zh_ref, first, cw_ref, cb_ref, ts, ext_b, cbs)
        chat, _ = _layer_norm_rows(cbs[...])
        lin = chat * cng_ref[...] + cnb_ref[...]
        yb = lin * jax.nn.sigmoid(lin)
        rb = lax.rsqrt(_rmean(yb * yb) + EPS)
        y_ref[:, SGU_WIDTH:SGU_WIDTH + CONV_WIDTH] = ((yb * rb) * bg_ref[:, SGU_WIDTH:SGU_WIDTH + CONV_WIDTH]).astype(BF16)

        dpool = _pool_forward(z_ref, zh_ref, first, i, ts, ext_c)
        yc = _dot(dpool.astype(BF16), pw_ref[...]) * ps_ref[...]
        rc = lax.rsqrt(_rmean(yc * yc) + EPS)
        y_ref[:, SGU_WIDTH + CONV_WIDTH:d] = ((yc * rc) * bg_ref[:, SGU_WIDTH + CONV_WIDTH:d]).astype(BF16)

    tile, prev, _ = _mixer_specs(s, ts, inw)
    consts = [mp["bd"], mp["ng"], mp["nb"], mp["wm"], mp["bias"], mp["cw"], mp["cb"], mp["cng"], mp["cnb"],
              mp["pw"], mp["ps"], mp["bg"]]
    return _hbm_call(
        body, name=name,
        grid=(s // ts,),
        in_specs=[tile, prev] + [_const_spec(c.shape) for c in consts],
        out_specs=pl.BlockSpec((ts, d), lambda i: (i, 0)),
        out_shape=jax.ShapeDtypeStruct((s, d), BF16),
        scratch_shapes=[pltpu.VMEM((ts, SGU_WIDTH), F32), pltpu.VMEM((ts + MIX_HALO, CONV_WIDTH), F32),
                        pltpu.VMEM((ts, CONV_WIDTH), F32), pltpu.VMEM((ts + MIX_HALO, POOL_WIDTH), F32)],
        compiler_params=pltpu.CompilerParams(dimension_semantics=("arbitrary",),
                                             vmem_limit_bytes=_vmem_limit(16 * ts * inw * 4)),
    )(z, z, *consts)


def _mixer_bwd_a(z, dy, mp, name):
    s, inw = z.shape
    d = SGU_WIDTH + CONV_WIDTH + POOL_WIDTH
    ts = _row_tile(s, 256)
    nchunk = ts // CHUNK

    def rms_bwd(dyn, y, g):
        r = lax.rsqrt(_rmean(y * y) + EPS)
        yn = y * r
        dg = _rsum(dyn * yn)
        t = dyn * g
        return r * (t - yn * _rmean(t * yn)), dg

    def body(z_ref, zh_ref, dy_ref, bd_ref, ng_ref, nb_ref, wm_ref, wmt_ref, bias_ref, cw_ref, cb_ref, cng_ref, cnb_ref,
             pw_ref, pwt_ref, ps_ref, bg_ref,
             dza_ref, dcb_ref, dd_ref, dbg_ref, dwm_ref, dbias_ref, dng_ref, dnb_ref, dcng_ref, dcnb_ref, dps_ref, dpw_ref,
             ya_s, f_s, dvn_s, ext_b, cbs, ext_c):
        i = pl.program_id(0)
        first = i == 0

        @pl.when(first)
        def _():
            for ref in (dwm_ref, dbias_ref, dng_ref, dnb_ref, dcng_ref, dcnb_ref, dps_ref, dpw_ref):
                ref[...] = jnp.zeros_like(ref)

        u, vhat, rstd, vn = _sgu_forward(z_ref, bd_ref, ng_ref, nb_ref, wm_ref, bias_ref, ts, ya_s, f_s)
        dya, dbg_a = rms_bwd(dy_ref[:, 0:SGU_WIDTH], ya_s[...], bg_ref[:, 0:SGU_WIDTH])
        du = dya * f_s[...]
        df = dya * u
        dfb = df.astype(BF16)
        left = _lane_lt((CHUNK, CHUNK), HEAD_DIM)
        zero = jnp.zeros((CHUNK, CHUNK), BF16)
        dbias = jnp.zeros((CHUNK, SGU_WIDTH), F32)
        for n in range(nchunk):
            rows = slice(n * CHUNK, (n + 1) * CHUNK)
            dbias = dbias + df[rows, :]
            for p in range(SGU_WIDTH // CHUNK):
                cols = slice(p * CHUNK, (p + 1) * CHUNK)
                dblk = dfb[rows, cols]
                vblk = vn[rows, cols]
                dwm_ref[2 * p] += _dot_nt(jnp.where(left, dblk, zero), vblk)
                dwm_ref[2 * p + 1] += _dot_nt(jnp.where(left, zero, dblk), vblk)
                dvn_s[rows, cols] = jnp.where(left, _dot(wmt_ref[2 * p], dblk), _dot(wmt_ref[2 * p + 1], dblk))
        dbias_ref[...] += dbias
        dvn = dvn_s[...]
        dng_ref[...] += _rsum(dvn * vhat)
        dnb_ref[...] += _rsum(dvn)
        dvh = dvn * ng_ref[...]
        bd = bd_ref[...]
        dv = rstd * (dvh - _group_mean(dvh, bd) - vhat * _group_mean(dvh * vhat, bd))
        dza_ref[:, 0:SGU_WIDTH] = (du * _gelu_grad(z_ref[:, 0:SGU_WIDTH])).astype(BF16)
        dza_ref[:, SGU_WIDTH:2 * SGU_WIDTH] = (dv * _gelu_grad(z_ref[:, SGU_WIDTH:2 * SGU_WIDTH])).astype(BF16)

        _conv31_forward(z_ref, zh_ref, first, cw_ref, cb_ref, ts, ext_b, cbs)
        chat, crstd = _layer_norm_rows(cbs[...])
        lin = chat * cng_ref[...] + cnb_ref[...]
        sl = jax.nn.sigmoid(lin)
        dyb, dbg_b = rms_bwd(dy_ref[:, SGU_WIDTH:SGU_WIDTH + CONV_WIDTH], lin * sl, bg_ref[:, SGU_WIDTH:SGU_WIDTH + CONV_WIDTH])
        dlin = dyb * (sl * (1.0 + lin * (1.0 - sl)))
        dcng_ref[...] += _rsum(dlin * chat)
        dcnb_ref[...] += _rsum(dlin)
        dch = dlin * cng_ref[...]
        dcb_ref[...] = crstd * (dch - _rmean(dch) - chat * _rmean(dch * chat))

        dpool = _pool_forward(z_ref, zh_ref, first, i, ts, ext_c)
        dpb = dpool.astype(BF16)
        ycp = _dot(dpb, pw_ref[...])
        dyc, dbg_c = rms_bwd(dy_ref[:, SGU_WIDTH + CONV_WIDTH:d], ycp * ps_ref[...], bg_ref[:, SGU_WIDTH + CONV_WIDTH:d])
        dps_ref[...] += _rsum(dyc * ycp)
        dycp = (dyc * ps_ref[...]).astype(BF16)
        dpw_ref[...] += _dot_tn(dpb, dycp)
        dd_ref[...] = _dot(dycp, pwt_ref[...])

        @pl.when(first)
        def _():
            dbg_ref[...] = jnp.zeros_like(dbg_ref)

        dbg_ref[:, 0:SGU_WIDTH] += dbg_a
        dbg_ref[:, SGU_WIDTH:SGU_WIDTH + CONV_WIDTH] += dbg_b
        dbg_ref[:, SGU_WIDTH + CONV_WIDTH:d] += dbg_c

    tile, prev, _ = _mixer_specs(s, ts, inw)
    consts = [mp["bd"], mp["ng"], mp["nb"], mp["wm"], mp["wmt"], mp["bias"], mp["cw"], mp["cb"], mp["cng"], mp["cnb"],
              mp["pw"], mp["pwt"], mp["ps"], mp["bg"]]
    acc_shapes = [(1, d), (2 * (SGU_WIDTH // CHUNK), CHUNK, CHUNK), (CHUNK, SGU_WIDTH), (1, SGU_WIDTH), (1, SGU_WIDTH),
                  (1, CONV_WIDTH), (1, CONV_WIDTH), (1, POOL_WIDTH), (POOL_WIDTH, POOL_WIDTH)]
    return _hbm_call(
        body, name=name,
        grid=(s // ts,),
        in_specs=[tile, prev, pl.BlockSpec((ts, d), lambda i: (i, 0))] + [_const_spec(c.shape) for c in consts],
        out_specs=[pl.BlockSpec((ts, 2 * SGU_WIDTH), lambda i: (i, 0)), pl.BlockSpec((ts, CONV_WIDTH), lambda i: (i, 0)),
                   pl.BlockSpec((ts, POOL_WIDTH), lambda i: (i, 0))] + [_const_spec(a) for a in acc_shapes],
        out_shape=[jax.ShapeDtypeStruct((s, 2 * SGU_WIDTH), BF16), jax.ShapeDtypeStruct((s, CONV_WIDTH), F32),
                   jax.ShapeDtypeStruct((s, POOL_WIDTH), F32)] + [jax.ShapeDtypeStruct(a, F32) for a in acc_shapes],
        scratch_shapes=[pltpu.VMEM((ts, SGU_WIDTH), F32), pltpu.VMEM((ts, SGU_WIDTH), F32), pltpu.VMEM((ts, SGU_WIDTH), F32),
                        pltpu.VMEM((ts + MIX_HALO, CONV_WIDTH), F32), pltpu.VMEM((ts, CONV_WIDTH), F32),
                        pltpu.VMEM((ts + MIX_HALO, POOL_WIDTH), F32)],
        compiler_params=pltpu.CompilerParams(dimension_semantics=("arbitrary",),
                                             vmem_limit_bytes=_vmem_limit(24 * ts * inw * 4)),
    )(z, z, dy, *consts)


def _mixer_bwd_b(z, dza, dcb, dd, x, dres, gain, s1p, cw, w, name):
    s, inw = z.shape
    d = x.shape[1]
    ts = _row_tile(s, 256)
    c0 = 2 * SGU_WIDTH
    c1 = c0 + 2 * CONV_WIDTH

    def body(z_ref, zh_ref, dza_ref, dcb_ref, dcbn_ref, dd_ref, ddn_ref, x_ref, dres_ref, g_ref, s_ref, cw_ref, w_ref,
             dx_ref, dz_ref, dsh_ref, dsc_ref, dg_ref, dcw_ref, dcbias_ref, ext_b, ext_n, ext_e):
        i = pl.program_id(0)
        first = i == 0
        last = i == pl.num_programs(0) - 1

        @pl.when(first)
        def _():
            for ref in (dsh_ref, dsc_ref, dg_ref, dcw_ref, dcbias_ref):
                ref[...] = jnp.zeros_like(ref)

        a = z_ref[:, c0:c0 + CONV_WIDTH]
        sg = jax.nn.sigmoid(z_ref[:, c0 + CONV_WIDTH:c1])
        ah = zh_ref[:, c0:c0 + CONV_WIDTH]
        gh = zh_ref[:, c0 + CONV_WIDTH:c1]
        ext_b[pl.ds(0, MIX_HALO), :] = jnp.where(first, 0.0, ah * jax.nn.sigmoid(gh))
        ext_b[pl.ds(MIX_HALO, ts), :] = a * sg
        dcbv = dcb_ref[...]
        dcbias_ref[...] += _rsum(dcbv)
        for k in range(CONV_K):
            dcw_ref[k:k + 1, :] += _rsum(dcbv * ext_b[pl.ds(MIX_HALO - (CONV_K - 1) + k, ts), :])

        ext_n[pl.ds(0, ts), :] = dcbv
        ext_n[pl.ds(ts, MIX_HALO), :] = jnp.where(last, 0.0, dcbn_ref[...])
        for r in range(ts // CONV_ROWS):
            acc = jnp.zeros((CONV_ROWS, CONV_WIDTH), F32)
            for k in range(CONV_K):
                acc = acc + cw_ref[k:k + 1, :] * ext_n[pl.ds(CONV_K - 1 - k + r * CONV_ROWS, CONV_ROWS), :]
            rows = pl.ds(r * CONV_ROWS, CONV_ROWS)
            ar = z_ref[rows, c0:c0 + CONV_WIDTH]
            sr = jax.nn.sigmoid(z_ref[rows, c0 + CONV_WIDTH:c1])
            dz_ref[rows, c0:c0 + CONV_WIDTH] = (acc * sr).astype(BF16)
            dz_ref[rows, c0 + CONV_WIDTH:c1] = (acc * ar * sr * (1.0 - sr)).astype(BF16)

        ddv = dd_ref[...]
        ext_e[pl.ds(0, ts), :] = ddv / _pool_counts(i, ts)
        nh = (i + 1) * ts + lax.broadcasted_iota(jnp.int32, (MIX_HALO, POOL_WIDTH), 0)
        lane = lax.broadcasted_iota(jnp.int32, (MIX_HALO, POOL_WIDTH), 1)
        gdim = POOL_WIDTH // len(POOL_WINDOWS)
        winh = jnp.where(lane < gdim, float(POOL_WINDOWS[0]),
                         jnp.where(lane < 2 * gdim, float(POOL_WINDOWS[1]),
                                   jnp.where(lane < 3 * gdim, float(POOL_WINDOWS[2]), float(POOL_WINDOWS[3]))))
        cnth = jnp.minimum((nh + 1).astype(F32), winh)
        ext_e[pl.ds(ts, MIX_HALO), :] = jnp.where(last, 0.0, ddn_ref[...] / cnth)
        dz_ref[:, c1:inw] = (_window_sums(ext_e, 0, ts, 1) - ddv).astype(BF16)
        dz_ref[:, 0:c0] = dza_ref[...]

        dh = _dot_nt(dz_ref[...], w_ref[...])
        dx, dsh, dsc, dg = _norm_mod_bwd(dh, x_ref[...], g_ref[...], s_ref[...], dres_ref[...])
        dx_ref[...] = dx
        dsh_ref[...] += dsh
        dsc_ref[...] += dsc
        dg_ref[...] += dg

    tile, prev, _ = _mixer_specs(s, ts, inw)
    _, _, nxt_b = _mixer_specs(s, ts, CONV_WIDTH)
    _, _, nxt_c = _mixer_specs(s, ts, POOL_WIDTH)
    row = pl.BlockSpec((ts, d), lambda i: (i, 0))
    vec = pl.BlockSpec((1, d), lambda i: (0, 0))
    return _hbm_call(
        body, name=name,
        grid=(s // ts,),
        in_specs=[tile, prev, pl.BlockSpec((ts, c0), lambda i: (i, 0)),
                  pl.BlockSpec((ts, CONV_WIDTH), lambda i: (i, 0)), nxt_b,
                  pl.BlockSpec((ts, POOL_WIDTH), lambda i: (i, 0)), nxt_c,
                  row, row, vec, vec, _const_spec(cw.shape),
                  pl.BlockSpec(w.shape, lambda i: (0, 0), pipeline_mode=pl.Buffered(1))],
        out_specs=[row, pl.BlockSpec((ts, inw), lambda i: (i, 0)), vec, vec, vec,
                   _const_spec((CONV_K, CONV_WIDTH)), _const_spec((1, CONV_WIDTH))],
        out_shape=[jax.ShapeDtypeStruct((s, d), F32), jax.ShapeDtypeStruct((s, inw), BF16)]
        + [jax.ShapeDtypeStruct((1, d), F32)] * 3
        + [jax.ShapeDtypeStruct((CONV_K, CONV_WIDTH), F32), jax.ShapeDtypeStruct((1, CONV_WIDTH), F32)],
        scratch_shapes=[pltpu.VMEM((ts + MIX_HALO, CONV_WIDTH), F32), pltpu.VMEM((ts + MIX_HALO, CONV_WIDTH), F32),
                        pltpu.VMEM((ts + MIX_HALO, POOL_WIDTH), F32)],
        compiler_params=pltpu.CompilerParams(dimension_semantics=("arbitrary",),
                                             vmem_limit_bytes=_vmem_limit(16 * ts * inw * 4 + inw * d * 2)),
    )(z, z, dza, dcb, dcb, dd, dd, x, dres, gain, s1p, cw, w)


def _ffn_specs(s, ts, tc, half_blocks):
    nbh = ts // FFN_HALO

    def tile(off):
        return pl.BlockSpec((ts, tc), lambda j, i: (i, j + off))

    def prev(off):
        return pl.BlockSpec((FFN_HALO, tc), lambda j, i: (jnp.maximum(i * nbh - 1, 0), j + off))

    def vec(rows, off):
        return pl.BlockSpec((rows, tc), lambda j, i: (0, j + off))

    return tile, prev, vec


def _conv3_tile(p_ref, ph_ref, first, w_ref, b_ref, ts, ext):
    ext[pl.ds(0, FFN_HALO), :] = jnp.where(first, 0.0, ph_ref[...])
    ext[pl.ds(FFN_HALO, ts), :] = p_ref[...]
    acc = b_ref[...] + w_ref[FFN_CONV_K - 1:FFN_CONV_K, :] * p_ref[...]
    for k in range(FFN_CONV_K - 1):
        acc = acc + w_ref[k:k + 1, :] * ext[pl.ds(FFN_HALO - (FFN_CONV_K - 1) + k, ts), :]
    return acc


def _ffn_act_fwd(p, cw, cb, name):
    s, f2 = p.shape
    f = f2 // 2
    tc = f // 2
    hb = f // tc
    ts = _row_tile(s, 256)

    def body(pg_ref, pgh_ref, pv_ref, pvh_ref, wg_ref, wv_ref, bg_ref, bv_ref, act_ref, ext_g, ext_v):
        first = pl.program_id(1) == 0
        ug = _conv3_tile(pg_ref, pgh_ref, first, wg_ref, bg_ref, ts, ext_g)
        uv = _conv3_tile(pv_ref, pvh_ref, first, wv_ref, bv_ref, ts, ext_v)
        act_ref[...] = (_gelu(ug) * uv).astype(BF16)

    tile, prev, vec = _ffn_specs(s, ts, tc, hb)
    return _hbm_call(
        body, name=name,
        grid=(hb, s // ts),
        in_specs=[tile(0), prev(0), tile(hb), prev(hb), vec(FFN_CONV_K, 0), vec(FFN_CONV_K, hb), vec(1, 0), vec(1, hb)],
        out_specs=pl.BlockSpec((ts, tc), lambda j, i: (i, j)),
        out_shape=jax.ShapeDtypeStruct((s, f), BF16),
        scratch_shapes=[pltpu.VMEM((ts + FFN_HALO, tc), F32)] * 2,
        compiler_params=pltpu.CompilerParams(dimension_semantics=("arbitrary", "arbitrary"),
                                             vmem_limit_bytes=_vmem_limit(16 * ts * tc * 4)),
    )(p, p, p, p, cw, cw, cb, cb)


def _ffn_act_bwd(p, dact, cw, cb, name):
    s, f2 = p.shape
    f = f2 // 2
    tc = f // 2
    hb = f // tc
    ts = _row_tile(s, 256)

    def body(pg_ref, pgh_ref, pv_ref, pvh_ref, da_ref, wg_ref, wv_ref, bg_ref, bv_ref,
             dug_ref, duv_ref, dwg_ref, dwv_ref, dbg_ref, dbv_ref, ext_g, ext_v):
        first = pl.program_id(1) == 0

        @pl.when(first)
        def _():
            for ref in (dwg_ref, dwv_ref, dbg_ref, dbv_ref):
                ref[...] = jnp.zeros_like(ref)

        ug = _conv3_tile(pg_ref, pgh_ref, first, wg_ref, bg_ref, ts, ext_g)
        uv = _conv3_tile(pv_ref, pvh_ref, first, wv_ref, bv_ref, ts, ext_v)
        da = da_ref[...]
        dug = da * uv * _gelu_grad(ug)
        duv = da * _gelu(ug)
        dug_ref[...] = dug
        duv_ref[...] = duv
        dbg_ref[...] += _rsum(dug)
        dbv_ref[...] += _rsum(duv)
        for k in range(FFN_CONV_K):
            off = FFN_HALO - (FFN_CONV_K - 1) + k
            dwg_ref[k:k + 1, :] += _rsum(dug * ext_g[pl.ds(off, ts), :])
            dwv_ref[k:k + 1, :] += _rsum(duv * ext_v[pl.ds(off, ts), :])

    tile, prev, vec = _ffn_specs(s, ts, tc, hb)
    half = pl.BlockSpec((ts, tc), lambda j, i: (i, j))
    wacc = pl.BlockSpec((FFN_CONV_K, tc), lambda j, i: (0, j))
    bacc = pl.BlockSpec((1, tc), lambda j, i: (0, j))
    return _hbm_call(
        body, name=name,
        grid=(hb, s // ts),
        in_specs=[tile(0), prev(0), tile(hb), prev(hb), half, vec(FFN_CONV_K, 0), vec(FFN_CONV_K, hb), vec(1, 0), vec(1, hb)],
        out_specs=[half, half, wacc, wacc, bacc, bacc],
        out_shape=[jax.ShapeDtypeStruct((s, f), F32)] * 2 + [jax.ShapeDtypeStruct((FFN_CONV_K, f), F32)] * 2
        + [jax.ShapeDtypeStruct((1, f), F32)] * 2,
        scratch_shapes=[pltpu.VMEM((ts + FFN_HALO, tc), F32)] * 2,
        compiler_params=pltpu.CompilerParams(dimension_semantics=("arbitrary", "arbitrary"),
                                             vmem_limit_bytes=_vmem_limit(24 * ts * tc * 4)),
    )(p, p, p, p, dact, cw, cw, cb, cb)


def _ffn_in_bwd(dug, duv, cw, w, x, dres, gain, s1p, name):
    s, f = dug.shape
    d = x.shape[1]
    ts = _row_tile(s, 256)
    tc = w.shape[2]
    assert f % tc == 0 and w.shape[0] * tc == 2 * f
    nbh = ts // FFN_HALO

    def body(dug_ref, dugn_ref, duv_ref, duvn_ref, cw_ref, w_ref, x_ref, dres_ref, g_ref, s_ref,
             dx_ref, dp_ref, dsh_ref, dsc_ref, dg_ref, ext):
        i = pl.program_id(0)
        last = i == pl.num_programs(0) - 1

        @pl.when(i == 0)
        def _():
            for ref in (dsh_ref, dsc_ref, dg_ref):
                ref[...] = jnp.zeros_like(ref)

        dh = jnp.zeros((ts, d), F32)
        for half, (t_ref, n_ref) in enumerate(((dug_ref, dugn_ref), (duv_ref, duvn_ref))):
            for cb in range(f // tc):
                cols = slice(cb * tc, (cb + 1) * tc)
                wcols = slice(half * f + cb * tc, half * f + (cb + 1) * tc)
                ext[pl.ds(0, ts), :] = t_ref[:, cols]
                ext[pl.ds(ts, FFN_HALO), :] = jnp.where(last, 0.0, n_ref[:, cols])
                acc = cw_ref[FFN_CONV_K - 1:FFN_CONV_K, wcols] * t_ref[:, cols]
                for k in range(FFN_CONV_K - 1):
                    acc = acc + cw_ref[k:k + 1, wcols] * ext[pl.ds(FFN_CONV_K - 1 - k, ts), :]
                dpb = acc.astype(BF16)
                dp_ref[:, wcols] = dpb
                dh = dh + _dot_nt(dpb, w_ref[half * (f // tc) + cb])
        dx, dsh, dsc, dg = _norm_mod_bwd(dh, x_ref[...], g_ref[...], s_ref[...], dres_ref[...])
        dx_ref[...] = dx
        dsh_ref[...] += dsh
        dsc_ref[...] += dsc
        dg_ref[...] += dg

    tile = pl.BlockSpec((ts, f), lambda i: (i, 0))
    nxt = pl.BlockSpec((FFN_HALO, f), lambda i: (jnp.minimum((i + 1) * nbh, s // FFN_HALO - 1), 0))
    row = pl.BlockSpec((ts, d), lambda i: (i, 0))
    vec = pl.BlockSpec((1, d), lambda i: (0, 0))
    return _hbm_call(
        body, name=name,
        grid=(s // ts,),
        in_specs=[tile, nxt, tile, nxt, _const_spec(cw.shape),
                  pl.BlockSpec(w.shape, lambda i: (0, 0, 0), pipeline_mode=pl.Buffered(1)), row, row, vec, vec],
        out_specs=[row, pl.BlockSpec((ts, 2 * f), lambda i: (i, 0)), vec, vec, vec],
        out_shape=[jax.ShapeDtypeStruct((s, d), F32), jax.ShapeDtypeStruct((s, 2 * f), BF16)] + [jax.ShapeDtypeStruct((1, d), F32)] * 3,
        scratch_shapes=[pltpu.VMEM((ts + FFN_HALO, tc), F32)],
        compiler_params=pltpu.CompilerParams(
            dimension_semantics=("arbitrary",),
            vmem_limit_bytes=_vmem_limit(4 * ts * f * 4 + 2 * f * d * 2 + 2 * ts * 2 * f * 2 + 12 * ts * d * 4 + 6 * ts * tc * 4)),
    )(dug, dug, duv, duv, cw, w, x, dres, gain, s1p)


def _adamw_math(w, g, m, v):
    m = ADAM_B1 * m + (1.0 - ADAM_B1) * g
    v = ADAM_B2 * v + (1.0 - ADAM_B2) * (g * g)
    m_hat = m / (1.0 - ADAM_B1 ** ADAM_STEP)
    v_hat = v / (1.0 - ADAM_B2 ** ADAM_STEP)
    delta = -ADAM_LR * (m_hat / (jnp.sqrt(v_hat) + ADAM_EPS) + ADAM_WD * w)
    return delta, m, v


def _adam_rows(rows, cols):
    want = max(8, (2 * 1024 * 1024 // (cols * 4)) // 8 * 8)
    tr = min(rows, want)
    while rows % tr:
        tr -= 8
    return tr


def _adamw(w, m, v, g_parts, name):
    shape = w.shape
    nl = shape[0] if w.ndim == 3 else 1
    r, c = shape[-2], shape[-1]
    tr = _adam_rows(r, c)
    ng = len(g_parts)

    def body(*refs):
        w_ref, m_ref, v_ref = refs[0:3]
        g_refs = refs[3:3 + ng]
        g_out, d_out, m_out, v_out = refs[3 + ng:]
        g = g_refs[0][...]
        for gr in g_refs[1:]:
            g = g + gr[...]
        delta, mn, vn = _adamw_math(w_ref[...], g, m_ref[...], v_ref[...])
        g_out[...] = g
        d_out[...] = delta
        m_out[...] = mn
        v_out[...] = vn

    blk = pl.BlockSpec((None, tr, c), lambda l, i: (l, i, 0))
    outs = _hbm_call(
        body, name=name,
        grid=(nl, r // tr),
        in_specs=[blk] * (3 + ng),
        out_specs=[blk] * 4,
        out_shape=[jax.ShapeDtypeStruct((nl, r, c), F32)] * 4,
        compiler_params=pltpu.CompilerParams(dimension_semantics=("arbitrary", "arbitrary"),
                                             vmem_limit_bytes=_vmem_limit(2 * (7 + ng) * tr * max(c, 128) * 4 + (8 << 20))),
    )(*[a.reshape(nl, r, c) for a in (w, m, v, *g_parts)])
    return [o.reshape(shape) for o in outs]


def _modw_adamw(sct, dmod, w, m, v, name):
    nl, d, n = w.shape
    tr = _row_tile(d, 128)

    def body(sct_ref, dm_ref, w_ref, m_ref, v_ref, g_out, d_out, m_out, v_out):
        sc = sct_ref[...].astype(BF16).astype(F32)
        dm = dm_ref[...].astype(BF16).astype(F32)
        g = sc[:, 0:1] * dm[0:1, :]
        for b in range(1, N_DEV):
            g = g + sc[:, b:b + 1] * dm[b:b + 1, :]
        delta, mn, vn = _adamw_math(w_ref[...], g, m_ref[...], v_ref[...])
        g_out[...] = g
        d_out[...] = delta
        m_out[...] = mn
        v_out[...] = vn

    blk = pl.BlockSpec((None, tr, n), lambda l, i: (l, i, 0))
    return _hbm_call(
        body, name=name,
        grid=(nl, d // tr),
        in_specs=[pl.BlockSpec((tr, N_DEV), lambda l, i: (i, 0)), pl.BlockSpec((None, N_DEV, n), lambda l, i: (l, 0, 0)),
                  blk, blk, blk],
        out_specs=[blk] * 4,
        out_shape=[jax.ShapeDtypeStruct((nl, d, n), F32)] * 4,
        compiler_params=pltpu.CompilerParams(dimension_semantics=("arbitrary", "arbitrary"),
                                             vmem_limit_bytes=_vmem_limit(2 * 8 * tr * n * 4 + (8 << 20))),
    )(sct, dmod, w, m, v)


def _reduce4(recvs, name):
    nl = len(recvs)
    shape = recvs[0].shape[1:]
    c = shape[-1]
    r = math.prod(shape[:-1])
    tr = _adam_rows(r, c)
    nt = r // tr

    def body(*refs):
        o_ref = refs[nl]
        for l in range(nl):
            @pl.when(pl.program_id(0) == l)
            def _():
                acc = refs[l][0].astype(F32)
                for k in range(1, N_CHIPS):
                    acc = acc + refs[l][k].astype(F32)
                o_ref[...] = acc

    def in_map(l):
        return lambda ll, i: (0, jnp.where(ll < l, 0, jnp.where(ll > l, nt - 1, i)), 0)

    return _hbm_call(
        body, name=name,
        grid=(nl, nt),
        in_specs=[pl.BlockSpec((N_CHIPS, tr, c), in_map(l)) for l in range(nl)],
        out_specs=pl.BlockSpec((None, tr, c), lambda ll, i: (ll, i, 0)),
        out_shape=jax.ShapeDtypeStruct((nl, r, c), F32),
        compiler_params=pltpu.CompilerParams(dimension_semantics=("arbitrary", "arbitrary"),
                                             vmem_limit_bytes=_vmem_limit(2 * 8 * nl * tr * max(c, 128) * 4 + (8 << 20))),
    )(*[rv.reshape(N_CHIPS, r, c) for rv in recvs]).reshape((nl,) + shape)


def _my_place():
    return lax.axis_index("x"), lax.axis_index("y"), lax.axis_index("c")


def _chip_coords(j):
    return j // 2, j % 2


def _mod_forward(c, mod_w, mod_b4):
    nl, d, n = mod_w.shape
    kc = 256

    def body(c_ref, w_ref, b_ref, mod_ref, sc_ref, cbuf, stage, s1, r1, s2, r2):
        mx, my, mc = _my_place()
        me = 4 * mx + 2 * my + mc
        q = 2 * mx + my
        cv = c_ref[...]
        cbuf[me] = jnp.broadcast_to(cv * jax.nn.sigmoid(cv), (8, d))
        sends = []
        for t in range(N_DEV):
            tx, ty = _chip_coords(t // 2)
            cp = pltpu.make_async_remote_copy(src_ref=cbuf.at[me], dst_ref=cbuf.at[me], send_sem=s1.at[t], recv_sem=r1.at[me],
                                              device_id=(tx, ty, t % 2), device_id_type=MESH)

            @pl.when(t != me)
            def _():
                cp.start()

            sends.append((t, cp))
        for t in range(N_DEV):
            @pl.when(t != me)
            def _():
                pltpu.make_async_remote_copy(src_ref=cbuf.at[t], dst_ref=cbuf.at[t], send_sem=s1.at[t], recv_sem=r1.at[t],
                                             device_id=(mx, my, mc), device_id_type=MESH).wait_recv()
        for t, cp in sends:
            @pl.when(t != me)
            def _():
                cp.wait_send()

        row = lax.broadcasted_iota(jnp.int32, (8, d), 0)
        sc_all = jnp.zeros((8, d), F32)
        for t in range(N_DEV):
            sc_all = sc_all + jnp.where(row == t, cbuf[t], 0.0)
        sc_ref[...] = sc_all
        rown = lax.broadcasted_iota(jnp.int32, (8, n), 0)
        for l in range(nl):
            acc = jnp.zeros((8, n), F32)
            for k0 in range(0, d, kc):
                acc = acc + _dot(sc_all[:, k0:k0 + kc].astype(BF16), w_ref[l, k0:k0 + kc, :].astype(BF16))
            acc = acc + b_ref[l, q]
            for j in range(N_CHIPS):
                jx, jy = _chip_coords(j)
                bdest = 4 * jx + 2 * jy + mc
                rowv = jnp.sum(jnp.where(rown == bdest, acc, 0.0), axis=0, keepdims=True)
                stage[j, l] = jnp.broadcast_to(rowv, (8, n))
        sends2 = []
        for j in range(N_CHIPS):
            jx, jy = _chip_coords(j)
            cp = pltpu.make_async_remote_copy(src_ref=stage.at[j], dst_ref=mod_ref.at[:, q], send_sem=s2.at[j], recv_sem=r2.at[q],
                                              device_id=(jx, jy, mc), device_id_type=MESH)

            @pl.when(j != q)
            def _():
                cp.start()

            @pl.when(j == q)
            def _():
                for l in range(nl):
                    mod_ref[l, j] = stage[j, l]

            sends2.append((j, cp))
        for j in range(N_CHIPS):
            @pl.when(j != q)
            def _():
                pltpu.make_async_remote_copy(src_ref=stage.at[j], dst_ref=mod_ref.at[:, j], send_sem=s2.at[j], recv_sem=r2.at[j],
                                             device_id=(mx, my, mc), device_id_type=MESH).wait_recv()
        for j, cp in sends2:
            @pl.when(j != q)
            def _():
                cp.wait_send()

    vm = pl.BlockSpec(memory_space=pltpu.VMEM)
    return pl.pallas_call(
        body, name="mod_forward",
        in_specs=[vm, vm, vm],
        out_specs=[vm, vm],
        out_shape=[jax.ShapeDtypeStruct((nl, N_CHIPS, 8, n), F32), jax.ShapeDtypeStruct((8, d), F32)],
        scratch_shapes=[pltpu.VMEM((N_DEV, 8, d), F32), pltpu.VMEM((N_CHIPS, nl, 8, n), F32),
                        pltpu.SemaphoreType.DMA((N_DEV,)), pltpu.SemaphoreType.DMA((N_DEV,)),
                        pltpu.SemaphoreType.DMA((N_CHIPS,)), pltpu.SemaphoreType.DMA((N_CHIPS,))],
        compiler_params=pltpu.CompilerParams(vmem_limit_bytes=_vmem_limit(2 * nl * d * n * 4 + (8 << 20))),
    )(c, mod_w, mod_b4)


_HBM_SPEC = pl.BlockSpec(memory_space=pltpu.HBM)
_SEM_SPEC = pl.BlockSpec(memory_space=pltpu.SEMAPHORE)
_DATAFLOW = pltpu.SideEffectType.DATAFLOW_SIDE_EFFECTING


def _slot(ref, scatter, j):
    return ref.at[j] if scatter else ref


def _exchange_start(groups, scatter, after, name):
    flat = [a for g in groups for a in g]
    na = len(flat)
    ng = len(groups)
    sizes = [len(g) for g in groups]
    first = [sum(sizes[:g]) for g in range(ng)]
    where = [(g, k) for g in range(ng) for k in range(sizes[g])]
    mx, my, _ = _my_place()
    qo = 2 * mx + my
    lands = []
    for a in flat:
        own = lax.dynamic_index_in_dim(a, qo, 0, keepdims=False) if scatter else a
        lands.append(lax.dynamic_update_index_in_dim(lax.empty((N_CHIPS,) + own.shape, a.dtype), own, qo, 0))

    def body(*refs):
        ins, lnd = refs[:na], refs[na:2 * na]
        ssems, rsems = refs[2 * na + 1:2 * na + 1 + ng], refs[2 * na + 1 + ng:2 * na + 1 + 2 * ng]
        token = refs[-1]
        mx, my, mc = _my_place()
        q = 2 * mx + my
        for j in range(N_CHIPS):
            jx, jy = _chip_coords(j)
            for a in range(na):
                g, k = where[a]

                @pl.when(j != q)
                def _():
                    pltpu.make_async_remote_copy(src_ref=_slot(ins[a], scatter, j), dst_ref=lnd[a].at[q],
                                                 send_sem=ssems[g].at[k * N_CHIPS + j], recv_sem=rsems[g].at[k * N_CHIPS + q],
                                                 device_id=(jx, jy, mc), device_id_type=MESH).start()
        token[...] = jnp.zeros_like(token)

    sem_shapes = [pltpu.SemaphoreType.DMA((n * N_CHIPS,)) for n in sizes]
    outs = pl.pallas_call(
        body, name=name,
        in_specs=[_HBM_SPEC] * (2 * na) + [pl.BlockSpec(memory_space=pl.ANY)],
        out_specs=[_SEM_SPEC] * (2 * ng) + [_HBM_SPEC] * (2 * na) + [pl.BlockSpec(memory_space=pltpu.VMEM)],
        out_shape=sem_shapes + sem_shapes + [pltpu.HBM(a.shape, a.dtype) for a in flat + lands]
        + [jax.ShapeDtypeStruct((8, 128), F32)],
        input_output_aliases={i: 2 * ng + i for i in range(2 * na)},
        compiler_params=pltpu.CompilerParams(has_side_effects=_DATAFLOW),
    )(*[pltpu.with_memory_space_constraint(a, pltpu.HBM) for a in flat + lands], after)
    ssems, rsems = outs[:ng], outs[ng:2 * ng]
    src_thru, land_thru = outs[2 * ng:2 * ng + na], outs[2 * ng + na:2 * ng + 2 * na]
    states = [(src_thru[first[g]:first[g] + sizes[g]], land_thru[first[g]:first[g] + sizes[g]], ssems[g], rsems[g])
              for g in range(ng)]
    return states, outs[-1]


def _exchange_wait(state, scatter, after, name):
    srcs, lands, ssem, rsem = state
    na = len(srcs)

    def body(*refs):
        ins, lnd = refs[:na], refs[na:2 * na]
        ssem_ref, rsem_ref = refs[2 * na], refs[2 * na + 1]
        mx, my, mc = _my_place()
        q = 2 * mx + my
        for j in range(N_CHIPS):
            for a in range(na):
                @pl.when(j != q)
                def _():
                    cp = pltpu.make_async_remote_copy(src_ref=_slot(ins[a], scatter, j), dst_ref=lnd[a].at[j],
                                                      send_sem=ssem_ref.at[a * N_CHIPS + j], recv_sem=rsem_ref.at[a * N_CHIPS + j],
                                                      device_id=(mx, my, mc), device_id_type=MESH)
                    cp.wait_send()
                    cp.wait_recv()

    outs = pl.pallas_call(
        body, name=name,
        in_specs=[_HBM_SPEC] * (2 * na) + [_SEM_SPEC, _SEM_SPEC, pl.BlockSpec(memory_space=pl.ANY)],
        out_specs=[_HBM_SPEC] * (2 * na),
        out_shape=[pltpu.HBM(a.shape, a.dtype) for a in list(srcs) + list(lands)],
        input_output_aliases={i: i for i in range(2 * na)},
        compiler_params=pltpu.CompilerParams(has_side_effects=_DATAFLOW),
    )(*srcs, *lands, ssem, rsem, after)
    return outs[na:]


def _swap_with_sibling(arrs):
    na = len(arrs)

    def body(*refs):
        ins, outs = refs[:na], refs[na:2 * na]
        ssem, rsem = refs[2 * na:]
        mx, my, mc = _my_place()
        cps = [pltpu.make_async_remote_copy(src_ref=ins[a], dst_ref=outs[a], send_sem=ssem.at[a], recv_sem=rsem.at[a],
                                            device_id=(mx, my, 1 - mc), device_id_type=MESH) for a in range(na)]
        for cp in cps:
            cp.start()
        for cp in cps:
            cp.wait()

    hbm = pl.BlockSpec(memory_space=pl.ANY)
    return _hbm_call(
        body, name="swap_sibling",
        in_specs=[hbm] * na,
        out_specs=[hbm] * na,
        out_shape=[jax.ShapeDtypeStruct(a.shape, a.dtype) for a in arrs],
        scratch_shapes=[pltpu.SemaphoreType.DMA((na,)), pltpu.SemaphoreType.DMA((na,))],
    )(*arrs)


def _allreduce_small(rows_all, rows_sum):
    ra, c = rows_all.shape
    r = rows_sum.shape[0]
    ch = r // N_DEV
    assert ch % 8 == 0 and ch * N_DEV == r

    def body(a_ref, s_ref, all_ref, sum_ref, rbuf, red, sa, rva, sb, rvb, sc, rvc):
        mx, my, mc = _my_place()
        me = 4 * mx + 2 * my + mc
        mine = pl.ds(pl.multiple_of(me * ch, 8), ch)
        all_ref[me] = a_ref[...]
        rbuf[me] = s_ref[mine, :]

        def dev(t):
            tx, ty = _chip_coords(t // 2)
            return (tx, ty, t % 2)

        def everyone_else(fn):
            for t in range(N_DEV):
                @pl.when(t != me)
                def _():
                    fn(t)

        def copy_a(t, slot):
            return pltpu.make_async_remote_copy(src_ref=a_ref, dst_ref=all_ref.at[slot], send_sem=sa.at[t], recv_sem=rva.at[slot],
                                                device_id=dev(t), device_id_type=MESH)

        def copy_b(t, slot):
            return pltpu.make_async_remote_copy(src_ref=s_ref.at[pl.ds(t * ch, ch), :], dst_ref=rbuf.at[slot], send_sem=sb.at[t],
                                                recv_sem=rvb.at[slot], device_id=dev(t), device_id_type=MESH)

        def copy_c(t, chunk_start, slot):
            return pltpu.make_async_remote_copy(src_ref=red, dst_ref=sum_ref.at[pl.ds(chunk_start, ch), :], send_sem=sc.at[t],
                                                recv_sem=rvc.at[slot], device_id=dev(t), device_id_type=MESH)

        everyone_else(lambda t: (copy_a(t, me).start(), copy_b(t, me).start()))
        everyone_else(lambda t: (copy_a(t, t).wait_recv(), copy_b(t, t).wait_recv()))
        everyone_else(lambda t: (copy_a(t, me).wait_send(), copy_b(t, me).wait_send()))
        acc = rbuf[0]
        for t in range(1, N_DEV):
            acc = acc + rbuf[t]
        red[...] = acc
        sum_ref[mine, :] = acc
        everyone_else(lambda t: copy_c(t, pl.multiple_of(me * ch, 8), me).start())
        everyone_else(lambda t: copy_c(t, t * ch, t).wait_recv())
        everyone_else(lambda t: copy_c(t, pl.multiple_of(me * ch, 8), me).wait_send())

    vm = pl.BlockSpec(memory_space=pltpu.VMEM)
    return pl.pallas_call(
        body, name="allreduce_small",
        in_specs=[vm, vm],
        out_specs=[vm, vm],
        out_shape=[jax.ShapeDtypeStruct((N_DEV, ra, c), F32), jax.ShapeDtypeStruct((r, c), F32)],
        scratch_shapes=[pltpu.VMEM((N_DEV, ch, c), F32), pltpu.VMEM((ch, c), F32)] + [pltpu.SemaphoreType.DMA((N_DEV,))] * 6,
        compiler_params=pltpu.CompilerParams(vmem_limit_bytes=_vmem_limit((3 * r + 2 * N_DEV * ra) * c * 4 + (4 << 20))),
    )(rows_all, rows_sum)


def _pack(arrs, row_multiple=8):
    rows, layout, at = [], [], 0
    for a in arrs:
        n = a.size
        nr = -(-n // (8 * SMALL_COLS)) * 8
        flat = a.reshape(-1)
        if nr * SMALL_COLS != n:
            flat = jnp.pad(flat, (0, nr * SMALL_COLS - n))
        rows.append(flat.reshape(nr, SMALL_COLS))
        layout.append((at, nr, a.shape))
        at += nr
    pad = -at % row_multiple
    if pad:
        rows.append(jnp.zeros((pad, SMALL_COLS), F32))
    return jnp.concatenate(rows, axis=0), layout


def _unpack(buf, layout):
    out = []
    for at, nr, shape in layout:
        n = math.prod(shape)
        out.append(buf[at:at + nr].reshape(-1)[:n].reshape(shape))
    return out


SMALL_NAMES = ("mod_b", "mix_pre_g", "mix_post_g", "sgu_norm_g", "sgu_norm_b", "sgu_w", "sgu_b", "conv_b", "conv_norm_g",
               "conv_norm_b", "pool_w", "pool_scale", "branch_g", "ffn_pre_g", "ffn_post_g", "ffn_conv_b")
SHARDED_SMALL = ("conv_w", "ffn_conv_w")
WEIGHT_ORDER = ("mod_w", "mod_b", "mix_pre_g", "mix_post_g", "w_in", "sgu_norm_g", "sgu_norm_b", "sgu_w", "sgu_b", "conv_w",
                "conv_b", "conv_norm_g", "conv_norm_b", "pool_w", "pool_scale", "branch_g", "w_out", "ffn_pre_g", "ffn_post_g",
                "ffn_up", "ffn_conv_w", "ffn_conv_b", "ffn_down")


def _block_diag(blocks):
    n, a, b = blocks.shape
    eye = jnp.eye(n, dtype=blocks.dtype)
    return (eye[:, None, :, None] * blocks[:, :, None, :]).reshape(n * a, n * b)


def _diag_blocks(mat, n):
    a = mat.shape[0] // n
    return jnp.stack([mat[g * a:(g + 1) * a, g * a:(g + 1) * a] for g in range(n)])


def _step(x, c, loss_target, w, m, v):
    nl = w["mod_w"].shape[0]
    s, d = x.shape[1], x.shape[2]
    heads = SGU_WIDTH // HEAD_DIM
    groups = len(POOL_WINDOWS)
    mx, my, _ = _my_place()
    q = 2 * mx + my
    x0 = x.reshape(s, d)
    tgt = loss_target.reshape(s, d)

    nmod = w["mod_w"].shape[2]
    kin = w["w_in"].shape[2]
    inw = kin * N_CHIPS
    f2 = w["ffn_up"].shape[2] * N_CHIPS
    f = f2 // 2

    def wgroups(l):
        return [[w["w_in"][l].astype(BF16), w["conv_w"][l], w["ffn_conv_w"][l]], [w["w_out"][l].astype(BF16)],
                [w["ffn_up"][l].astype(BF16)], [w["ffn_down"][l].astype(BF16)]]

    mod4, sc_all = _mod_forward(c, w["mod_w"], w["mod_b"].reshape(nl, N_CHIPS, 1, nmod))
    mod = mod4[:, :, 0, :].reshape(nl, N_MOD, 1, d)
    gstates = {}
    (gstates[0, 0],), gtoken = _exchange_start(wgroups(0)[:1], False, mod4, "gather_start_in_0")

    tril = jnp.tril(jnp.ones((CHUNK, CHUNK), bool))
    bd = _block_diag(jnp.ones((heads, HEAD_DIM, HEAD_DIM), BF16))

    def mixer_params(l, conv_w):
        wm = jnp.where(tril[None], w["sgu_w"][l], 0.0)
        pw = _block_diag(w["pool_w"][l])
        return dict(
            bd=bd, ng=w["sgu_norm_g"][l][None], nb=w["sgu_norm_b"][l][None],
            wm=wm.astype(BF16), wmt=jnp.swapaxes(wm, 1, 2).astype(BF16),
            bias=jnp.repeat(w["sgu_b"][l].T, HEAD_DIM, axis=1),
            cw=conv_w, cb=w["conv_b"][l][None], cng=w["conv_norm_g"][l][None], cnb=w["conv_norm_b"][l][None],
            pw=pw.astype(BF16), pwt=pw.T.astype(BF16), ps=w["pool_scale"][l][None], bg=w["branch_g"][l][None])

    saved = []
    xl = x0
    prev_done = gtoken
    for l in range(nl):
        sh1, sc1, g1, sh2, sc2, g2 = [mod[l, k] for k in range(N_MOD)]
        gpre1, gpost1 = w["mix_pre_g"][l][None], w["mix_post_g"][l][None]
        gpre2, gpost2 = w["ffn_pre_g"][l][None], w["ffn_post_g"][l][None]
        g_win, g_cw, g_fcw = _exchange_wait(gstates[l, 0], False, prev_done, f"gather_wait_in_{l}")
        sh1_after = sh1
        if l == 0:
            (gstates[0, 1], gstates[0, 2], gstates[0, 3]), tok = _exchange_start(wgroups(0)[1:], False, g_win, "gather_start_rest_0")
            sh1_after = sh1 + tok[0:1, 0:1]
        w_in = jnp.transpose(g_win, (1, 0, 2)).reshape(d, inw)
        conv_w = jnp.transpose(g_cw, (1, 0, 2)).reshape(CONV_K, CONV_WIDTH)
        ffn_cw = jnp.transpose(g_fcw, (1, 0, 2)).reshape(FFN_CONV_K, f2)
        mp = mixer_params(l, conv_w)
        z, h1 = _norm_mod_matmul(xl, gpre1, 1.0 + sc1, sh1_after, w_in[None], f"mix_in_{l}")
        (g_wout,) = _exchange_wait(gstates[l, 1], False, z, f"gather_wait_out_{l}")
        w_out = g_wout.reshape(d, d)
        ycat = _mixer_fwd(z, mp, f"mixer_fwd_{l}")
        (up,) = _exchange_wait(gstates[l, 2], False, ycat, f"gather_wait_up_{l}")
        g1_after = g1
        if l + 1 < nl:
            (gstates[l + 1, 0], gstates[l + 1, 1], gstates[l + 1, 2], gstates[l + 1, 3]), tok = _exchange_start(
                wgroups(l + 1), False, up, f"gather_start_{l + 1}")
            g1_after = g1 + tok[0:1, 0:1]
        o, x1 = _matmul_norm_resid(ycat, w_out, xl, g1_after, gpost1, f"mix_out_{l}")
        (g_down,) = _exchange_wait(gstates[l, 3], False, x1, f"gather_wait_down_{l}")
        down = g_down.reshape(f, d)
        p, h2 = _norm_mod_matmul(x1, gpre2, 1.0 + sc2, sh2, up, f"ffn_in_{l}")
        fcb = w["ffn_conv_b"][l][None]
        act = _ffn_act_fwd(p, ffn_cw, fcb, f"ffn_act_{l}")
        qo, x2 = _matmul_norm_resid(act, down, x1, g2, gpost2, f"ffn_out_{l}")
        saved.append(dict(x=xl, z=z, h1=h1, ycat=ycat, o=o, x1=x1, p=p, h2=h2, act=act, qo=qo, mp=mp, fcb=fcb,
                          w_in=w_in, w_out=w_out, up=up, down=down, ffn_cw=ffn_cw,
                          mods=(sh1, sc1, g1, sh2, sc2, g2), gains=(gpre1, gpost1, gpre2, gpost2)))
        xl = x2
        prev_done = p

    dx, loss_row = _loss_head(xl, tgt)

    small = {n: [None] * nl for n in SMALL_NAMES + SHARDED_SMALL}
    dmods = [None] * nl
    tn = f2 // N_CHIPS
    sstates = {}
    token = None
    for l in reversed(range(nl)):
        sv = saved[l]
        sh1, sc1, g1, sh2, sc2, g2 = sv["mods"]
        gpre1, gpost1, gpre2, gpost2 = sv["gains"]
        if token is not None:
            g2 = g2 + token[0:1, 0:1]
        dq, dact, dg2, dgpost2 = _resid_bwd_matmul(dx, sv["qo"], g2, gpost2, sv["down"], f"ffn_out_bwd_{l}")
        g_down = _wgrad(sv["act"], dq, (f, lambda j: 0), (d, lambda j: 0), jax.ShapeDtypeStruct((f, d), BF16),
                        (1, lambda j: (0, 0)), (f, d), f"wgrad_ffn_down_{l}")
        dug, duv, dfwg, dfwv, dfbg, dfbv = _ffn_act_bwd(sv["p"], dact, sv["ffn_cw"], sv["fcb"], f"ffn_act_bwd_{l}")
        dx1, dp, dsh2, dsc2, dgpre2 = _ffn_in_bwd(dug, duv, sv["ffn_cw"], sv["up"], sv["x1"], dx, gpre2, 1.0 + sc2,
                                                  f"ffn_in_bwd_{l}")
        g_up = _wgrad(sv["h2"], dp, (d, lambda j: 0), (tn, lambda j: j), jax.ShapeDtypeStruct((N_CHIPS, d, tn), BF16),
                      (N_CHIPS, lambda j: (j, 0, 0)), (None, d, tn), f"wgrad_ffn_up_{l}")
        (sstates[l, 0],), token = _exchange_start([[g_down.reshape(N_CHIPS, f // N_CHIPS, d), g_up]], True, g_up,
                                                  f"scatter_start_ffn_{l}")
        do, dycat, dg1, dgpost1 = _resid_bwd_matmul(dx1, sv["o"], g1 + token[0:1, 0:1], gpost1, sv["w_out"],
                                                    f"mix_out_bwd_{l}")
        g_out = _wgrad(sv["ycat"], do, (d, lambda j: 0), (d, lambda j: 0), jax.ShapeDtypeStruct((d, d), BF16),
                       (1, lambda j: (0, 0)), (d, d), f"wgrad_w_out_{l}")
        (sstates[l, 1],), token = _exchange_start([[g_out.reshape(N_CHIPS, d // N_CHIPS, d)]], True, g_out,
                                                  f"scatter_start_out_{l}")
        mp_after = dict(sv["mp"], bg=sv["mp"]["bg"] + token[0:1, 0:1])
        (dza, dcb, dd, dbg, dwm, dbias, dng, dnb, dcng, dcnb, dps, dpw) = _mixer_bwd_a(sv["z"], dycat, mp_after, f"mixer_bwd_a_{l}")
        dx, dz, dsh1, dsc1, dgpre1, dcw, dcbias = _mixer_bwd_b(
            sv["z"], dza, dcb, dd, sv["x"], dx1, gpre1, 1.0 + sc1, sv["mp"]["cw"], sv["w_in"], f"mixer_bwd_b_{l}")
        g_in = _wgrad(sv["h1"], dz, (d, lambda j: 0), (inw, lambda j: 0), jax.ShapeDtypeStruct((d, inw), BF16),
                      (1, lambda j: (0, 0)), (d, inw), f"wgrad_w_in_{l}")
        g_in_parts = jnp.transpose(g_in.reshape(d, N_CHIPS, kin), (1, 0, 2))
        if l > 0:
            (sstates[l, 2],), token = _exchange_start([[g_in_parts]], True, g_in_parts, f"scatter_start_in_{l}")

        dmods[l] = jnp.concatenate([dsh1, dsc1, dg1, dsh2, dsc2, dg2], axis=0)
        small["mix_pre_g"][l], small["mix_post_g"][l] = dgpre1[0], dgpost1[0]
        small["ffn_pre_g"][l], small["ffn_post_g"][l] = dgpre2[0], dgpost2[0]
        small["sgu_norm_g"][l], small["sgu_norm_b"][l] = dng[0], dnb[0]
        small["sgu_w"][l] = jnp.where(tril[None], dwm, 0.0)
        small["sgu_b"][l] = dbias.reshape(CHUNK, heads, HEAD_DIM).sum(-1).T
        small["conv_b"][l], small["conv_norm_g"][l], small["conv_norm_b"][l] = dcbias[0], dcng[0], dcnb[0]
        small["pool_w"][l], small["pool_scale"][l], small["branch_g"][l] = _diag_blocks(dpw, groups), dps[0], dbg[0]
        small["ffn_conv_b"][l] = jnp.concatenate([dfbg[0], dfbv[0]])
        small["conv_w"][l] = dcw
        small["ffn_conv_w"][l] = jnp.concatenate([dfwg, dfwv], axis=1)

    names = [n for n in SMALL_NAMES if n != "mod_b"] + list(SHARDED_SMALL)
    dmod_rows, _ = _pack([jnp.stack(dmods)])
    packed, layout = _pack([jnp.stack(dmods), loss_row] + [jnp.stack(small[n]) for n in names], 8 * N_DEV)
    gathered, summed = _allreduce_small(dmod_rows, packed)
    (sstates[0, 2],), token = _exchange_start([[g_in_parts]], True, summed, "scatter_start_in_0")
    parts = _unpack(summed, layout)
    loss = parts[1][0, 0]
    gsmall = dict(zip(names, parts[2:]))
    gsmall["mod_b"] = parts[0].reshape(nl, N_MOD * d)
    dmod_all = gathered[:, :nl * N_MOD].reshape(N_DEV, nl, N_MOD * d)
    dmod_mine = jnp.transpose(lax.dynamic_slice_in_dim(dmod_all, q * nmod, nmod, axis=2), (1, 0, 2))

    grads, deltas, new_m, new_v = {}, {}, {}, {}

    def put(name, res):
        grads[name], deltas[name], new_m[name], new_v[name] = res

    put("mod_w", _modw_adamw(sc_all.T + token[0:1, 0:1], dmod_mine, w["mod_w"], m["mod_w"], v["mod_w"], "adamw_mod_w"))

    pw_, lay = _pack([w[n] for n in SMALL_NAMES])
    pm_, _ = _pack([m[n] for n in SMALL_NAMES])
    pv_, _ = _pack([v[n] for n in SMALL_NAMES])
    pg_, _ = _pack([gsmall[n] for n in SMALL_NAMES])
    res = _adamw(pw_, pm_, pv_, [pg_], "adamw_small")
    for n, g_, d_, m_, v_ in zip(SMALL_NAMES, *[_unpack(r_, lay) for r_ in res]):
        put(n, (g_, d_, m_, v_))

    gsh = {"conv_w": lax.dynamic_slice_in_dim(gsmall["conv_w"], q * (CONV_WIDTH // N_CHIPS), CONV_WIDTH // N_CHIPS, axis=2),
           "ffn_conv_w": lax.dynamic_slice_in_dim(gsmall["ffn_conv_w"], q * (f2 // N_CHIPS), f2 // N_CHIPS, axis=2)}
    pw_, lay = _pack([w[n] for n in SHARDED_SMALL])
    pm_, _ = _pack([m[n] for n in SHARDED_SMALL])
    pv_, _ = _pack([v[n] for n in SHARDED_SMALL])
    pg_, _ = _pack([gsh[n] for n in SHARDED_SMALL])
    res = _adamw(pw_, pm_, pv_, [pg_], "adamw_sharded_small")
    for n, g_, d_, m_, v_ in zip(SHARDED_SMALL, *[_unpack(r_, lay) for r_ in res]):
        put(n, (g_, d_, m_, v_))

    recv = dict(w_in=[None] * nl, w_out=[None] * nl, ffn_up=[None] * nl, ffn_down=[None] * nl)
    done = grads["mod_w"]
    for l in reversed(range(nl)):
        recv["ffn_down"][l], recv["ffn_up"][l] = _exchange_wait(sstates[l, 0], True, done, f"scatter_wait_ffn_{l}")
        (recv["w_out"][l],) = _exchange_wait(sstates[l, 1], True, recv["ffn_up"][l], f"scatter_wait_out_{l}")
        (recv["w_in"][l],) = _exchange_wait(sstates[l, 2], True, recv["w_out"][l], f"scatter_wait_in_{l}")
        done = recv["w_in"][l]
    big = ("w_in", "w_out", "ffn_up", "ffn_down")
    mine = [_reduce4(recv[n], f"reduce4_{n}") for n in big]
    theirs = _swap_with_sibling(mine)
    for n, a, b in zip(big, mine, theirs):
        put(n, _adamw(w[n], m[n], v[n], [a, b], f"adamw_{n}"))

    return (loss, dx.reshape(1, s, d), *[grads[n] for n in WEIGHT_ORDER], *[deltas[n] for n in WEIGHT_ORDER],
            *[new_m[n] for n in WEIGHT_ORDER], *[new_v[n] for n in WEIGHT_ORDER])


def kernel(x, c, mod_w, mod_b, mix_pre_g, mix_post_g, w_in, sgu_norm_g, sgu_norm_b, sgu_w, sgu_b, conv_w, conv_b, conv_norm_g, conv_norm_b, pool_w, pool_scale, branch_g, w_out, ffn_pre_g, ffn_post_g, ffn_up, ffn_conv_w, ffn_conv_b, ffn_down, loss_target, m_mod_w, m_mod_b, m_mix_pre_g, m_mix_post_g, m_w_in, m_sgu_norm_g, m_sgu_norm_b, m_sgu_w, m_sgu_b, m_conv_w, m_conv_b, m_conv_norm_g, m_conv_norm_b, m_pool_w, m_pool_scale, m_branch_g, m_w_out, m_ffn_pre_g, m_ffn_post_g, m_ffn_up, m_ffn_conv_w, m_ffn_conv_b, m_ffn_down, v_mod_w, v_mod_b, v_mix_pre_g, v_mix_post_g, v_w_in, v_sgu_norm_g, v_sgu_norm_b, v_sgu_w, v_sgu_b, v_conv_w, v_conv_b, v_conv_norm_g, v_conv_norm_b, v_pool_w, v_pool_scale, v_branch_g, v_w_out, v_ffn_pre_g, v_ffn_post_g, v_ffn_up, v_ffn_conv_w, v_ffn_conv_b, v_ffn_down):
    w = dict(mod_w=mod_w, mod_b=mod_b, mix_pre_g=mix_pre_g, mix_post_g=mix_post_g, w_in=w_in, sgu_norm_g=sgu_norm_g,
             sgu_norm_b=sgu_norm_b, sgu_w=sgu_w, sgu_b=sgu_b, conv_w=conv_w, conv_b=conv_b, conv_norm_g=conv_norm_g,
             conv_norm_b=conv_norm_b, pool_w=pool_w, pool_scale=pool_scale, branch_g=branch_g, w_out=w_out,
             ffn_pre_g=ffn_pre_g, ffn_post_g=ffn_post_g, ffn_up=ffn_up, ffn_conv_w=ffn_conv_w, ffn_conv_b=ffn_conv_b,
             ffn_down=ffn_down)
    m = dict(mod_w=m_mod_w, mod_b=m_mod_b, mix_pre_g=m_mix_pre_g, mix_post_g=m_mix_post_g, w_in=m_w_in,
             sgu_norm_g=m_sgu_norm_g, sgu_norm_b=m_sgu_norm_b, sgu_w=m_sgu_w, sgu_b=m_sgu_b, conv_w=m_conv_w,
             conv_b=m_conv_b, conv_norm_g=m_conv_norm_g, conv_norm_b=m_conv_norm_b, pool_w=m_pool_w,
             pool_scale=m_pool_scale, branch_g=m_branch_g, w_out=m_w_out, ffn_pre_g=m_ffn_pre_g, ffn_post_g=m_ffn_post_g,
             ffn_up=m_ffn_up, ffn_conv_w=m_ffn_conv_w, ffn_conv_b=m_ffn_conv_b, ffn_down=m_ffn_down)
    v = dict(mod_w=v_mod_w, mod_b=v_mod_b, mix_pre_g=v_mix_pre_g, mix_post_g=v_mix_post_g, w_in=v_w_in,
             sgu_norm_g=v_sgu_norm_g, sgu_norm_b=v_sgu_norm_b, sgu_w=v_sgu_w, sgu_b=v_sgu_b, conv_w=v_conv_w,
             conv_b=v_conv_b, conv_norm_g=v_conv_norm_g, conv_norm_b=v_conv_norm_b, pool_w=v_pool_w,
             pool_scale=v_pool_scale, branch_g=v_branch_g, w_out=v_w_out, ffn_pre_g=v_ffn_pre_g, ffn_post_g=v_ffn_post_g,
             ffn_up=v_ffn_up, ffn_conv_w=v_ffn_conv_w, ffn_conv_b=v_ffn_conv_b, ffn_down=v_ffn_down)
    return _step(x, c, loss_target, w, m, v)
```

```python
import functools
import math

import jax
import jax.numpy as jnp
from jax import lax
from jax.experimental import pallas as pl
from jax.experimental.pallas import tpu as pltpu

F32 = jnp.float32
BF16 = jnp.bfloat16
MESH = pl.DeviceIdType.MESH

EPS = 1e-6
HEAD_DIM = 64
CHUNK = 128
SGU_WIDTH = 384
CONV_WIDTH = 384
POOL_WIDTH = 256
POOL_WINDOWS = (2, 4, 8, 16)
CONV_K = 31
FFN_CONV_K = 3
N_MOD = 6
N_CHIPS = 4
N_DEV = 8

ADAM_LR = 0.001
ADAM_B1 = 0.9
ADAM_B2 = 0.999
ADAM_EPS = 1e-08
ADAM_WD = 0.01
ADAM_STEP = 10

MIX_HALO = 32
FFN_HALO = 8
FFN_ROWS = 16
LANES = 128
CONV_ROWS = 32
SMALL_COLS = 1024
VMEM_BYTES_V7X = 64 * 1024 * 1024


def _vmem_limit(estimate_bytes):
    return int(min(max(estimate_bytes, 16 * 1024 * 1024), VMEM_BYTES_V7X - 8 * 1024 * 1024))


def _row_tile(s, want):
    return want if s % want == 0 else math.gcd(s, want)


def _rsum(v):
    return jnp.sum(v, axis=0, keepdims=True)


def _rmean(v):
    return jnp.mean(v, axis=-1, keepdims=True)


def _gelu(v):
    k = math.sqrt(2.0 / math.pi)
    return 0.5 * v * (1.0 + jnp.tanh(k * (v + 0.044715 * v * v * v)))


def _gelu_grad(v):
    k = math.sqrt(2.0 / math.pi)
    t = jnp.tanh(k * (v + 0.044715 * v * v * v))
    return 0.5 * (1.0 + t) + 0.5 * v * (1.0 - t * t) * (k * (1.0 + 3.0 * 0.044715 * v * v))


def _dot(a, b):
    return jnp.dot(a, b, preferred_element_type=F32)


def _dot_nt(a, b):
    return lax.dot_general(a, b, (((1,), (1,)), ((), ())), preferred_element_type=F32)


def _dot_tn(a, b):
    return lax.dot_general(a, b, (((0,), (0,)), ((), ())), preferred_element_type=F32)


def _group_mean(v, bd):
    hi = v.astype(BF16)
    lo = (v - hi.astype(F32)).astype(BF16)
    return (_dot(hi, bd) + _dot(lo, bd)) * (1.0 / HEAD_DIM)


def _const_spec(shape):
    nd = len(shape)
    return pl.BlockSpec(shape, lambda *_: (0,) * nd)


def _hbm_call(body, **kw):
    call = pl.pallas_call(body, **kw)
    return lambda *args: call(*[pltpu.with_memory_space_constraint(a, pltpu.HBM) for a in args])


def _norm_mod_matmul(x, gain, s1p, shift, w, name):
    s, d = x.shape
    nb, _, tn = w.shape
    ts = _row_tile(s, 512 if nb * tn <= 2048 else 256)

    def body(x_ref, g_ref, s_ref, b_ref, w_ref, z_ref, h_ref):
        xv = x_ref[...]
        r = lax.rsqrt(_rmean(xv * xv) + EPS)
        h = ((xv * r) * g_ref[...] * s_ref[...] + b_ref[...]).astype(BF16)
        h_ref[...] = h
        for j in range(nb):
            z_ref[:, j * tn:(j + 1) * tn] = _dot(h, w_ref[j])

    vec = pl.BlockSpec((1, d), lambda i: (0, 0))
    return _hbm_call(
        body, name=name,
        grid=(s // ts,),
        in_specs=[pl.BlockSpec((ts, d), lambda i: (i, 0)), vec, vec, vec,
                  pl.BlockSpec((nb, d, tn), lambda i: (0, 0, 0), pipeline_mode=pl.Buffered(1))],
        out_specs=[pl.BlockSpec((ts, nb * tn), lambda i: (i, 0)), pl.BlockSpec((ts, d), lambda i: (i, 0))],
        out_shape=[jax.ShapeDtypeStruct((s, nb * tn), F32), jax.ShapeDtypeStruct((s, d), BF16)],
        compiler_params=pltpu.CompilerParams(
            dimension_semantics=("arbitrary",),
            vmem_limit_bytes=_vmem_limit(2 * (ts * d * 4 + ts * nb * tn * 4 + ts * d * 2) + nb * d * tn * 2 + 4 * ts * d * 4)),
    )(x, gain, s1p, shift, w)


def _matmul_norm_resid(a, w, xres, gate, gpost, name):
    s, k = a.shape
    d = w.shape[1]
    ts = _row_tile(s, 512)

    def body(a_ref, w_ref, x_ref, gate_ref, gp_ref, o_ref, xn_ref):
        o = _dot(a_ref[...], w_ref[...])
        o_ref[...] = o
        r = lax.rsqrt(_rmean(o * o) + EPS)
        xn_ref[...] = x_ref[...] + gate_ref[...] * ((o * r) * gp_ref[...])

    vec = pl.BlockSpec((1, d), lambda i: (0, 0))
    row = pl.BlockSpec((ts, d), lambda i: (i, 0))
    return _hbm_call(
        body, name=name,
        grid=(s // ts,),
        in_specs=[pl.BlockSpec((ts, k), lambda i: (i, 0)),
                  pl.BlockSpec((k, d), lambda i: (0, 0), pipeline_mode=pl.Buffered(1)), row, vec, vec],
        out_specs=[row, row],
        out_shape=[jax.ShapeDtypeStruct((s, d), F32)] * 2,
        compiler_params=pltpu.CompilerParams(
            dimension_semantics=("arbitrary",),
            vmem_limit_bytes=_vmem_limit(2 * (ts * k * 2 + 3 * ts * d * 4) + k * d * 2 + 4 * ts * d * 4)),
    )(a, w, xres, gate, gpost)


def _wgrad(a, b, acols, bcols, out_struct, out_index, out_block, name):
    s = a.shape[0]
    ts = _row_tile(s, 512)
    aw, afn = acols
    bw, bfn = bcols
    nj = out_index[0]
    oidx = out_index[1]

    def body(a_ref, b_ref, o_ref, acc):
        i = pl.program_id(1)

        @pl.when(i == 0)
        def _():
            acc[...] = jnp.zeros_like(acc)

        acc[...] += _dot_tn(a_ref[...], b_ref[...])

        @pl.when(i == pl.num_programs(1) - 1)
        def _():
            o_ref[...] = acc[...].astype(o_ref.dtype)

    return _hbm_call(
        body, name=name,
        grid=(nj, s // ts),
        in_specs=[pl.BlockSpec((ts, aw), lambda j, i: (i, afn(j))), pl.BlockSpec((ts, bw), lambda j, i: (i, bfn(j)))],
        out_specs=pl.BlockSpec(out_block, lambda j, i: oidx(j)),
        out_shape=out_struct,
        scratch_shapes=[pltpu.VMEM((aw, bw), F32)],
        compiler_params=pltpu.CompilerParams(
            dimension_semantics=("arbitrary", "arbitrary"),
            vmem_limit_bytes=_vmem_limit(2 * (ts * aw * 2 + ts * bw * 2) + 3 * aw * bw * 4 + ts * aw * 4)),
    )(a, b)


def _loss_head(xo, tgt):
    s, d = xo.shape
    ts = _row_tile(s, 512)

    def body(x_ref, t_ref, dx_ref, l_ref, acc):
        i = pl.program_id(0)

        @pl.when(i == 0)
        def _():
            acc[...] = jnp.zeros_like(acc)

        e = x_ref[...] - t_ref[...]
        dx_ref[...] = e * (1.0 / d)
        acc[...] += _rsum(e * e)

        @pl.when(i == pl.num_programs(0) - 1)
        def _():
            tot = jnp.sum(acc[...], axis=-1, keepdims=True) * (0.5 / d)
            l_ref[...] = jnp.broadcast_to(tot, l_ref.shape)

    row = pl.BlockSpec((ts, d), lambda i: (i, 0))
    return _hbm_call(
        body, name="loss_head",
        grid=(s // ts,),
        in_specs=[row, row],
        out_specs=[row, pl.BlockSpec((1, SMALL_COLS), lambda i: (0, 0))],
        out_shape=[jax.ShapeDtypeStruct((s, d), F32), jax.ShapeDtypeStruct((1, SMALL_COLS), F32)],
        scratch_shapes=[pltpu.VMEM((1, d), F32)],
        compiler_params=pltpu.CompilerParams(dimension_semantics=("arbitrary",)),
    )(xo, tgt)


def _resid_bwd_matmul(dxn, o, gate, gpost, w, name):
    s, d = dxn.shape
    k = w.shape[0]
    ts = _row_tile(s, 512)

    def body(dx_ref, o_ref, gate_ref, gp_ref, w_ref, do_ref, da_ref, dgate_ref, dgp_ref):
        i = pl.program_id(0)

        @pl.when(i == 0)
        def _():
            dgate_ref[...] = jnp.zeros_like(dgate_ref)
            dgp_ref[...] = jnp.zeros_like(dgp_ref)

        dx = dx_ref[...]
        o = o_ref[...]
        r = lax.rsqrt(_rmean(o * o) + EPS)
        on = o * r
        dgate_ref[...] += _rsum(dx * (on * gp_ref[...]))
        don = dx * gate_ref[...]
        dgp_ref[...] += _rsum(don * on)
        t = don * gp_ref[...]
        do = (r * (t - on * _rmean(t * on))).astype(BF16)
        do_ref[...] = do
        da_ref[...] = _dot_nt(do, w_ref[...])

    vec = pl.BlockSpec((1, d), lambda i: (0, 0))
    row = pl.BlockSpec((ts, d), lambda i: (i, 0))
    return _hbm_call(
        body, name=name,
        grid=(s // ts,),
        in_specs=[row, row, vec, vec, pl.BlockSpec((k, d), lambda i: (0, 0), pipeline_mode=pl.Buffered(1))],
        out_specs=[row, pl.BlockSpec((ts, k), lambda i: (i, 0)), vec, vec],
        out_shape=[jax.ShapeDtypeStruct((s, d), BF16), jax.ShapeDtypeStruct((s, k), F32),
                   jax.ShapeDtypeStruct((1, d), F32), jax.ShapeDtypeStruct((1, d), F32)],
        compiler_params=pltpu.CompilerParams(
            dimension_semantics=("arbitrary",),
            vmem_limit_bytes=_vmem_limit(2 * (2 * ts * d * 4 + ts * d * 2 + ts * k * 4) + d * k * 2 + 6 * ts * d * 4)),
    )(dxn, o, gate, gpost, w)


def _norm_mod_bwd(dh, xv, gain, s1p, dres):
    r = lax.rsqrt(_rmean(xv * xv) + EPS)
    xn = xv * r
    dshift = _rsum(dh)
    t = dh * xn
    dscale = _rsum(t * gain)
    dgain = _rsum(t * s1p)
    dxn = dh * (gain * s1p)
    dx = r * (dxn - xn * _rmean(dxn * xn)) + dres
    return dx, dshift, dscale, dgain


def _lane_lt(shape, bound):
    return lax.broadcasted_iota(jnp.int32, shape, 1) < bound


def _sgu_forward(z_ref, bd_ref, ng_ref, nb_ref, wm_ref, bias_ref, ts, ya_s, f_s):
    u = _gelu(z_ref[:, 0:SGU_WIDTH])
    v = _gelu(z_ref[:, SGU_WIDTH:2 * SGU_WIDTH])
    bd = bd_ref[...]
    vc = v - _group_mean(v, bd)
    rstd = lax.rsqrt(_group_mean(vc * vc, bd) + EPS)
    vhat = vc * rstd
    vn = (vhat * ng_ref[...] + nb_ref[...]).astype(BF16)
    left = _lane_lt((CHUNK, CHUNK), HEAD_DIM)
    for n in range(ts // CHUNK):
        rows = slice(n * CHUNK, (n + 1) * CHUNK)
        for p in range(SGU_WIDTH // CHUNK):
            cols = slice(p * CHUNK, (p + 1) * CHUNK)
            blk = vn[rows, cols]
            f = jnp.where(left, _dot(wm_ref[2 * p], blk), _dot(wm_ref[2 * p + 1], blk)) + bias_ref[:, cols]
            if f_s is not None:
                f_s[rows, cols] = f
            ya_s[rows, cols] = u[rows, cols] * f
    return u, vhat, rstd, vn


def _conv31_forward(z_ref, zh_ref, first, cw_ref, cb_ref, ts, ext_b, cbs):
    a = z_ref[:, 2 * SGU_WIDTH:2 * SGU_WIDTH + CONV_WIDTH]
    g = z_ref[:, 2 * SGU_WIDTH + CONV_WIDTH:2 * SGU_WIDTH + 2 * CONV_WIDTH]
    ah = zh_ref[:, 2 * SGU_WIDTH:2 * SGU_WIDTH + CONV_WIDTH]
    gh = zh_ref[:, 2 * SGU_WIDTH + CONV_WIDTH:2 * SGU_WIDTH + 2 * CONV_WIDTH]
    ext_b[pl.ds(0, MIX_HALO), :] = jnp.where(first, 0.0, ah * jax.nn.sigmoid(gh))
    ext_b[pl.ds(MIX_HALO, ts), :] = a * jax.nn.sigmoid(g)
    for r in range(ts // CONV_ROWS):
        acc = jnp.broadcast_to(cb_ref[...], (CONV_ROWS, CONV_WIDTH))
        for k in range(CONV_K):
            acc = acc + cw_ref[k:k + 1, :] * ext_b[pl.ds(MIX_HALO - (CONV_K - 1) + k + r * CONV_ROWS, CONV_ROWS), :]
        cbs[pl.ds(r * CONV_ROWS, CONV_ROWS), :] = acc


def _pool_counts(i, ts):
    pos1 = (i * ts + 1 + lax.broadcasted_iota(jnp.int32, (ts, POOL_WIDTH), 0)).astype(F32)
    lane = lax.broadcasted_iota(jnp.int32, (ts, POOL_WIDTH), 1)
    gdim = POOL_WIDTH // len(POOL_WINDOWS)
    win = jnp.where(lane < gdim, float(POOL_WINDOWS[0]),
                    jnp.where(lane < 2 * gdim, float(POOL_WINDOWS[1]),
                              jnp.where(lane < 3 * gdim, float(POOL_WINDOWS[2]), float(POOL_WINDOWS[3]))))
    return jnp.minimum(pos1, win)


def _window_sums(ext, base, ts, sign):
    lane = lax.broadcasted_iota(jnp.int32, (ts, POOL_WIDTH), 1)
    gdim = POOL_WIDTH // len(POOL_WINDOWS)
    run = jnp.zeros((ts, POOL_WIDTH), F32)
    out = jnp.zeros((ts, POOL_WIDTH), F32)
    for m in range(POOL_WINDOWS[-1]):
        run = run + ext[pl.ds(base + sign * m, ts), :]
        for gi, win in enumerate(POOL_WINDOWS):
            if m == win - 1:
                out = jnp.where((lane >= gi * gdim) & (lane < (gi + 1) * gdim), run, out)
    return out


def _pool_forward(z_ref, zh_ref, first, i, ts, ext_c):
    c0 = 2 * SGU_WIDTH + 2 * CONV_WIDTH
    zc = z_ref[:, c0:c0 + POOL_WIDTH]
    ext_c[pl.ds(0, MIX_HALO), :] = jnp.where(first, 0.0, zh_ref[:, c0:c0 + POOL_WIDTH])
    ext_c[pl.ds(MIX_HALO, ts), :] = zc
    sums = _window_sums(ext_c, MIX_HALO, ts, -1)
    return sums / _pool_counts(i, ts) - zc


def _layer_norm_rows(v):
    mu = _rmean(v)
    vc = v - mu
    rstd = lax.rsqrt(_rmean(vc * vc) + EPS)
    return vc * rstd, rstd


def _mixer_specs(s, ts, width):
    nbh = ts // MIX_HALO
    tile = pl.BlockSpec((ts, width), lambda i: (i, 0))
    prev = pl.BlockSpec((MIX_HALO, width), lambda i: (jnp.maximum(i * nbh - 1, 0), 0))
    nxt = pl.BlockSpec((MIX_HALO, width), lambda i: (jnp.minimum((i + 1) * nbh, s // MIX_HALO - 1), 0))
    return tile, prev, nxt


def _mixer_fwd(z, mp, name):
    s, inw = z.shape
    d = SGU_WIDTH + CONV_WIDTH + POOL_WIDTH
    ts = _row_tile(s, 256)

    def body(z_ref, zh_ref, bd_ref, ng_ref, nb_ref, wm_ref, bias_ref, cw_ref, cb_ref, cng_ref, cnb_ref,
             pw_ref, ps_ref, bg_ref, y_ref, ya_s, ext_b, cbs, ext_c):
        i = pl.program_id(0)
        first = i == 0
        _sgu_forward(z_ref, bd_ref, ng_ref, nb_ref, wm_ref, bias_ref, ts, ya_s, None)
        ya = ya_s[...]
        ra = lax.rsqrt(_rmean(ya * ya) + EPS)
        y_ref[:, 0:SGU_WIDTH] = ((ya * ra) * bg_ref[:, 0:SGU_WIDTH]).astype(BF16)

        _conv31_forward(z_ref, zh_ref, first, cw_ref, cb_ref, ts, ext_b, cbs)
        chat, _ = _layer_norm_rows(cbs[...])
        lin = chat * cng_ref[...] + cnb_ref[...]
        yb = lin * jax.nn.sigmoid(lin)
        rb = lax.rsqrt(_rmean(yb * yb) + EPS)
        y_ref[:, SGU_WIDTH:SGU_WIDTH + CONV_WIDTH] = ((yb * rb) * bg_ref[:, SGU_WIDTH:SGU_WIDTH + CONV_WIDTH]).astype(BF16)

        dpool = _pool_forward(z_ref, zh_ref, first, i, ts, ext_c)
        yc = _dot(dpool.astype(BF16), pw_ref[...]) * ps_ref[...]
        rc = lax.rsqrt(_rmean(yc * yc) + EPS)
        y_ref[:, SGU_WIDTH + CONV_WIDTH:d] = ((yc * rc) * bg_ref[:, SGU_WIDTH + CONV_WIDTH:d]).astype(BF16)

    tile, prev, _ = _mixer_specs(s, ts, inw)
    consts = [mp["bd"], mp["ng"], mp["nb"], mp["wm"], mp["bias"], mp["cw"], mp["cb"], mp["cng"], mp["cnb"],
              mp["pw"], mp["ps"], mp["bg"]]
    return _hbm_call(
        body, name=name,
        grid=(s // ts,),
        in_specs=[tile, prev] + [_const_spec(c.shape) for c in consts],
        out_specs=pl.BlockSpec((ts, d), lambda i: (i, 0)),
        out_shape=jax.ShapeDtypeStruct((s, d), BF16),
        scratch_shapes=[pltpu.VMEM((ts, SGU_WIDTH), F32), pltpu.VMEM((ts + MIX_HALO, CONV_WIDTH), F32),
                        pltpu.VMEM((ts, CONV_WIDTH), F32), pltpu.VMEM((ts + MIX_HALO, POOL_WIDTH), F32)],
        compiler_params=pltpu.CompilerParams(dimension_semantics=("arbitrary",),
                                             vmem_limit_bytes=_vmem_limit(16 * ts * inw * 4)),
    )(z, z, *consts)


def _mixer_bwd_a(z, dy, mp, name):
    s, inw = z.shape
    d = SGU_WIDTH + CONV_WIDTH + POOL_WIDTH
    ts = _row_tile(s, 256)
    nchunk = ts // CHUNK

    def rms_bwd(dyn, y, g):
        r = lax.rsqrt(_rmean(y * y) + EPS)
        yn = y * r
        dg = _rsum(dyn * yn)
        t = dyn * g
        return r * (t - yn * _rmean(t * yn)), dg

    def body(z_ref, zh_ref, dy_ref, bd_ref, ng_ref, nb_ref, wm_ref, wmt_ref, bias_ref, cw_ref, cb_ref, cng_ref, cnb_ref,
             pw_ref, pwt_ref, ps_ref, bg_ref,
             dza_ref, dcb_ref, dd_ref, dbg_ref, dwm_ref, dbias_ref, dng_ref, dnb_ref, dcng_ref, dcnb_ref, dps_ref, dpw_ref,
             ya_s, f_s, dvn_s, ext_b, cbs, ext_c):
        i = pl.program_id(0)
        first = i == 0

        @pl.when(first)
        def _():
            for ref in (dwm_ref, dbias_ref, dng_ref, dnb_ref, dcng_ref, dcnb_ref, dps_ref, dpw_ref):
                ref[...] = jnp.zeros_like(ref)

        u, vhat, rstd, vn = _sgu_forward(z_ref, bd_ref, ng_ref, nb_ref, wm_ref, bias_ref, ts, ya_s, f_s)
        dya, dbg_a = rms_bwd(dy_ref[:, 0:SGU_WIDTH], ya_s[...], bg_ref[:, 0:SGU_WIDTH])
        du = dya * f_s[...]
        df = dya * u
        dfb = df.astype(BF16)
        left = _lane_lt((CHUNK, CHUNK), HEAD_DIM)
        zero = jnp.zeros((CHUNK, CHUNK), BF16)
        dbias = jnp.zeros((CHUNK, SGU_WIDTH), F32)
        for n in range(nchunk):
            rows = slice(n * CHUNK, (n + 1) * CHUNK)
            dbias = dbias + df[rows, :]
            for p in range(SGU_WIDTH // CHUNK):
                cols = slice(p * CHUNK, (p + 1) * CHUNK)
                dblk = dfb[rows, cols]
                vblk = vn[rows, cols]
                dwm_ref[2 * p] += _dot_nt(jnp.where(left, dblk, zero), vblk)
                dwm_ref[2 * p + 1] += _dot_nt(jnp.where(left, zero, dblk), vblk)
                dvn_s[rows, cols] = jnp.where(left, _dot(wmt_ref[2 * p], dblk), _dot(wmt_ref[2 * p + 1], dblk))
        dbias_ref[...] += dbias
        dvn = dvn_s[...]
        dng_ref[...] += _rsum(dvn * vhat)
        dnb_ref[...] += _rsum(dvn)
        dvh = dvn * ng_ref[...]
        bd = bd_ref[...]
        dv = rstd * (dvh - _group_mean(dvh, bd) - vhat * _group_mean(dvh * vhat, bd))
        dza_ref[:, 0:SGU_WIDTH] = (du * _gelu_grad(z_ref[:, 0:SGU_WIDTH])).astype(BF16)
        dza_ref[:, SGU_WIDTH:2 * SGU_WIDTH] = (dv * _gelu_grad(z_ref[:, SGU_WIDTH:2 * SGU_WIDTH])).astype(BF16)

        _conv31_forward(z_ref, zh_ref, first, cw_ref, cb_ref, ts, ext_b, cbs)
        chat, crstd = _layer_norm_rows(cbs[...])
        lin = chat * cng_ref[...] + cnb_ref[...]
        sl = jax.nn.sigmoid(lin)
        dyb, dbg_b = rms_bwd(dy_ref[:, SGU_WIDTH:SGU_WIDTH + CONV_WIDTH], lin * sl, bg_ref[:, SGU_WIDTH:SGU_WIDTH + CONV_WIDTH])
        dlin = dyb * (sl * (1.0 + lin * (1.0 - sl)))
        dcng_ref[...] += _rsum(dlin * chat)
        dcnb_ref[...] += _rsum(dlin)
        dch = dlin * cng_ref[...]
        dcb_ref[...] = crstd * (dch - _rmean(dch) - chat * _rmean(dch * chat))

        dpool = _pool_forward(z_ref, zh_ref, first, i, ts, ext_c)
        dpb = dpool.astype(BF16)
        ycp = _dot(dpb, pw_ref[...])
        dyc, dbg_c = rms_bwd(dy_ref[:, SGU_WIDTH + CONV_WIDTH:d], ycp * ps_ref[...], bg_ref[:, SGU_WIDTH + CONV_WIDTH:d])
        dps_ref[...] += _rsum(dyc * ycp)
        dycp = (dyc * ps_ref[...]).astype(BF16)
        dpw_ref[...] += _dot_tn(dpb, dycp)
        dd_ref[...] = _dot(dycp, pwt_ref[...])

        @pl.when(first)
        def _():
            dbg_ref[...] = jnp.zeros_like(dbg_ref)

        dbg_ref[:, 0:SGU_WIDTH] += dbg_a
        dbg_ref[:, SGU_WIDTH:SGU_WIDTH + CONV_WIDTH] += dbg_b
        dbg_ref[:, SGU_WIDTH + CONV_WIDTH:d] += dbg_c

    tile, prev, _ = _mixer_specs(s, ts, inw)
    consts = [mp["bd"], mp["ng"], mp["nb"], mp["wm"], mp["wmt"], mp["bias"], mp["cw"], mp["cb"], mp["cng"], mp["cnb"],
              mp["pw"], mp["pwt"], mp["ps"], mp["bg"]]
    acc_shapes = [(1, d), (2 * (SGU_WIDTH // CHUNK), CHUNK, CHUNK), (CHUNK, SGU_WIDTH), (1, SGU_WIDTH), (1, SGU_WIDTH),
                  (1, CONV_WIDTH), (1, CONV_WIDTH), (1, POOL_WIDTH), (POOL_WIDTH, POOL_WIDTH)]
    return _hbm_call(
        body, name=name,
        grid=(s // ts,),
        in_specs=[tile, prev, pl.BlockSpec((ts, d), lambda i: (i, 0))] + [_const_spec(c.shape) for c in consts],
        out_specs=[pl.BlockSpec((ts, 2 * SGU_WIDTH), lambda i: (i, 0)), pl.BlockSpec((ts, CONV_WIDTH), lambda i: (i, 0)),
                   pl.BlockSpec((ts, POOL_WIDTH), lambda i: (i, 0))] + [_const_spec(a) for a in acc_shapes],
        out_shape=[jax.ShapeDtypeStruct((s, 2 * SGU_WIDTH), BF16), jax.ShapeDtypeStruct((s, CONV_WIDTH), F32),
                   jax.ShapeDtypeStruct((s, POOL_WIDTH), F32)] + [jax.ShapeDtypeStruct(a, F32) for a in acc_shapes],
        scratch_shapes=[pltpu.VMEM((ts, SGU_WIDTH), F32), pltpu.VMEM((ts, SGU_WIDTH), F32), pltpu.VMEM((ts, SGU_WIDTH), F32),
                        pltpu.VMEM((ts + MIX_HALO, CONV_WIDTH), F32), pltpu.VMEM((ts, CONV_WIDTH), F32),
                        pltpu.VMEM((ts + MIX_HALO, POOL_WIDTH), F32)],
        compiler_params=pltpu.CompilerParams(dimension_semantics=("arbitrary",),
                                             vmem_limit_bytes=_vmem_limit(24 * ts * inw * 4)),
    )(z, z, dy, *consts)


def _mixer_bwd_b(z, dza, dcb, dd, x, dres, gain, s1p, cw, w, name):
    s, inw = z.shape
    d = x.shape[1]
    ts = _row_tile(s, 256)
    c0 = 2 * SGU_WIDTH
    c1 = c0 + 2 * CONV_WIDTH

    def body(z_ref, zh_ref, dza_ref, dcb_ref, dcbn_ref, dd_ref, ddn_ref, x_ref, dres_ref, g_ref, s_ref, cw_ref, w_ref,
             dx_ref, dz_ref, dsh_ref, dsc_ref, dg_ref, dcw_ref, dcbias_ref, ext_b, ext_n, ext_e):
        i = pl.program_id(0)
        first = i == 0
        last = i == pl.num_programs(0) - 1

        @pl.when(first)
        def _():
            for ref in (dsh_ref, dsc_ref, dg_ref, dcw_ref, dcbias_ref):
                ref[...] = jnp.zeros_like(ref)

        a = z_ref[:, c0:c0 + CONV_WIDTH]
        sg = jax.nn.sigmoid(z_ref[:, c0 + CONV_WIDTH:c1])
        ah = zh_ref[:, c0:c0 + CONV_WIDTH]
        gh = zh_ref[:, c0 + CONV_WIDTH:c1]
        ext_b[pl.ds(0, MIX_HALO), :] = jnp.where(first, 0.0, ah * jax.nn.sigmoid(gh))
        ext_b[pl.ds(MIX_HALO, ts), :] = a * sg
        dcbv = dcb_ref[...]
        dcbias_ref[...] += _rsum(dcbv)
        for k in range(CONV_K):
            dcw_ref[k:k + 1, :] += _rsum(dcbv * ext_b[pl.ds(MIX_HALO - (CONV_K - 1) + k, ts), :])

        ext_n[pl.ds(0, ts), :] = dcbv
        ext_n[pl.ds(ts, MIX_HALO), :] = jnp.where(last, 0.0, dcbn_ref[...])
        for r in range(ts // CONV_ROWS):
            acc = jnp.zeros((CONV_ROWS, CONV_WIDTH), F32)
            for k in range(CONV_K):
                acc = acc + cw_ref[k:k + 1, :] * ext_n[pl.ds(CONV_K - 1 - k + r * CONV_ROWS, CONV_ROWS), :]
            rows = pl.ds(r * CONV_ROWS, CONV_ROWS)
            ar = z_ref[rows, c0:c0 + CONV_WIDTH]
            sr = jax.nn.sigmoid(z_ref[rows, c0 + CONV_WIDTH:c1])
            dz_ref[rows, c0:c0 + CONV_WIDTH] = (acc * sr).astype(BF16)
            dz_ref[rows, c0 + CONV_WIDTH:c1] = (acc * ar * sr * (1.0 - sr)).astype(BF16)

        ddv = dd_ref[...]
        ext_e[pl.ds(0, ts), :] = ddv / _pool_counts(i, ts)
        nh = (i + 1) * ts + lax.broadcasted_iota(jnp.int32, (MIX_HALO, POOL_WIDTH), 0)
        lane = lax.broadcasted_iota(jnp.int32, (MIX_HALO, POOL_WIDTH), 1)
        gdim = POOL_WIDTH // len(POOL_WINDOWS)
        winh = jnp.where(lane < gdim, float(POOL_WINDOWS[0]),
                         jnp.where(lane < 2 * gdim, float(POOL_WINDOWS[1]),
                                   jnp.where(lane < 3 * gdim, float(POOL_WINDOWS[2]), float(POOL_WINDOWS[3]))))
        cnth = jnp.minimum((nh + 1).astype(F32), winh)
        ext_e[pl.ds(ts, MIX_HALO), :] = jnp.where(last, 0.0, ddn_ref[...] / cnth)
        dz_ref[:, c1:inw] = (_window_sums(ext_e, 0, ts, 1) - ddv).astype(BF16)
        dz_ref[:, 0:c0] = dza_ref[...]

        dh = _dot_nt(dz_ref[...], w_ref[...])
        dx, dsh, dsc, dg = _norm_mod_bwd(dh, x_ref[...], g_ref[...], s_ref[...], dres_ref[...])
        dx_ref[...] = dx
        dsh_ref[...] += dsh
        dsc_ref[...] += dsc
        dg_ref[...] += dg

    tile, prev, _ = _mixer_specs(s, ts, inw)
    _, _, nxt_b = _mixer_specs(s, ts, CONV_WIDTH)
    _, _, nxt_c = _mixer_specs(s, ts, POOL_WIDTH)
    row = pl.BlockSpec((ts, d), lambda i: (i, 0))
    vec = pl.BlockSpec((1, d), lambda i: (0, 0))
    return _hbm_call(
        body, name=name,
        grid=(s // ts,),
        in_specs=[tile, prev, pl.BlockSpec((ts, c0), lambda i: (i, 0)),
                  pl.BlockSpec((ts, CONV_WIDTH), lambda i: (i, 0)), nxt_b,
                  pl.BlockSpec((ts, POOL_WIDTH), lambda i: (i, 0)), nxt_c,
                  row, row, vec, vec, _const_spec(cw.shape),
                  pl.BlockSpec(w.shape, lambda i: (0, 0), pipeline_mode=pl.Buffered(1))],
        out_specs=[row, pl.BlockSpec((ts, inw), lambda i: (i, 0)), vec, vec, vec,
                   _const_spec((CONV_K, CONV_WIDTH)), _const_spec((1, CONV_WIDTH))],
        out_shape=[jax.ShapeDtypeStruct((s, d), F32), jax.ShapeDtypeStruct((s, inw), BF16)]
        + [jax.ShapeDtypeStruct((1, d), F32)] * 3
        + [jax.ShapeDtypeStruct((CONV_K, CONV_WIDTH), F32), jax.ShapeDtypeStruct((1, CONV_WIDTH), F32)],
        scratch_shapes=[pltpu.VMEM((ts + MIX_HALO, CONV_WIDTH), F32), pltpu.VMEM((ts + MIX_HALO, CONV_WIDTH), F32),
                        pltpu.VMEM((ts + MIX_HALO, POOL_WIDTH), F32)],
        compiler_params=pltpu.CompilerParams(dimension_semantics=("arbitrary",),
                                             vmem_limit_bytes=_vmem_limit(16 * ts * inw * 4 + inw * d * 2)),
    )(z, z, dza, dcb, dcb, dd, dd, x, dres, gain, s1p, cw, w)


def _ffn_specs(s, ts, tc, half_blocks):
    nbh = ts // FFN_HALO

    def tile(off):
        return pl.BlockSpec((ts, tc), lambda j, i: (i, j + off))

    def prev(off):
        return pl.BlockSpec((FFN_HALO, tc), lambda j, i: (jnp.maximum(i * nbh - 1, 0), j + off))

    def vec(rows, off):
        return pl.BlockSpec((rows, tc), lambda j, i: (0, j + off))

    return tile, prev, vec


def _rows_before(cur, prev, k):
    row = lax.broadcasted_iota(jnp.int32, cur.shape, 0)
    return jnp.where(row >= k, pltpu.roll(cur, k, 0), pltpu.roll(prev, k, 0))


def _conv3_rows(cur, prev, w_ref, b_ref, cols):
    x1 = _rows_before(cur, prev, 1)
    x2 = _rows_before(cur, prev, 2)
    u = b_ref[:, cols] + w_ref[2:3, cols] * cur + w_ref[1:2, cols] * x1 + w_ref[0:1, cols] * x2
    return u, x2, x1


def _halo_chunk(h_ref, cols, first):
    h = jnp.where(first, 0.0, h_ref[:, cols])
    return jnp.concatenate([h] * (FFN_ROWS // FFN_HALO), axis=0)


def _ffn_act_fwd(p, cw, cb, name):
    s, f2 = p.shape
    f = f2 // 2
    tc = f // 2
    hb = f // tc
    ts = _row_tile(s, 256)

    def body(pg_ref, pgh_ref, pv_ref, pvh_ref, wg_ref, wv_ref, bg_ref, bv_ref, act_ref):
        first = pl.program_id(1) == 0
        for c in range(tc // LANES):
            cols = slice(c * LANES, (c + 1) * LANES)

            def step(r, carry, cols=cols):
                pg_prev, pv_prev = carry
                rows = pl.ds(pl.multiple_of(r * FFN_ROWS, FFN_ROWS), FFN_ROWS)
                pg = pg_ref[rows, cols]
                pv = pv_ref[rows, cols]
                ug, _, _ = _conv3_rows(pg, pg_prev, wg_ref, bg_ref, cols)
                uv, _, _ = _conv3_rows(pv, pv_prev, wv_ref, bv_ref, cols)
                act_ref[rows, cols] = (_gelu(ug) * uv).astype(BF16)
                return pg, pv

            lax.fori_loop(0, ts // FFN_ROWS, step,
                          (_halo_chunk(pgh_ref, cols, first), _halo_chunk(pvh_ref, cols, first)), unroll=2)

    tile, prev, vec = _ffn_specs(s, ts, tc, hb)
    return _hbm_call(
        body, name=name,
        grid=(hb, s // ts),
        in_specs=[tile(0), prev(0), tile(hb), prev(hb), vec(FFN_CONV_K, 0), vec(FFN_CONV_K, hb), vec(1, 0), vec(1, hb)],
        out_specs=pl.BlockSpec((ts, tc), lambda j, i: (i, j)),
        out_shape=jax.ShapeDtypeStruct((s, f), BF16),
        compiler_params=pltpu.CompilerParams(dimension_semantics=("arbitrary", "arbitrary"),
                                             vmem_limit_bytes=_vmem_limit(8 * ts * tc * 4)),
    )(p, p, p, p, cw, cw, cb, cb)


def _ffn_act_bwd(p, dact, cw, cb, name):
    s, f2 = p.shape
    f = f2 // 2
    tc = f // 2
    hb = f // tc
    ts = _row_tile(s, 256)

    def body(pg_ref, pgh_ref, pv_ref, pvh_ref, da_ref, wg_ref, wv_ref, bg_ref, bv_ref,
             dug_ref, duv_ref, dwg_ref, dwv_ref, dbg_ref, dbv_ref):
        first = pl.program_id(1) == 0

        @pl.when(first)
        def _():
            for ref in (dwg_ref, dwv_ref, dbg_ref, dbv_ref):
                ref[...] = jnp.zeros_like(ref)

        zero = jnp.zeros((FFN_ROWS, LANES), F32)
        for c in range(tc // LANES):
            cols = slice(c * LANES, (c + 1) * LANES)

            def step(r, carry, cols=cols):
                pg_prev, pv_prev, ag0, ag1, ag2, av0, av1, av2, sg, sv = carry
                rows = pl.ds(pl.multiple_of(r * FFN_ROWS, FFN_ROWS), FFN_ROWS)
                pg = pg_ref[rows, cols]
                pv = pv_ref[rows, cols]
                ug, pg2, pg1 = _conv3_rows(pg, pg_prev, wg_ref, bg_ref, cols)
                uv, pv2, pv1 = _conv3_rows(pv, pv_prev, wv_ref, bv_ref, cols)
                da = da_ref[rows, cols]
                dug = da * uv * _gelu_grad(ug)
                duv = da * _gelu(ug)
                dug_ref[rows, cols] = dug
                duv_ref[rows, cols] = duv
                return (pg, pv, ag0 + dug * pg2, ag1 + dug * pg1, ag2 + dug * pg,
                        av0 + duv * pv2, av1 + duv * pv1, av2 + duv * pv, sg + dug, sv + duv)

            out = lax.fori_loop(0, ts // FFN_ROWS, step,
                                (_halo_chunk(pgh_ref, cols, first), _halo_chunk(pvh_ref, cols, first)) + (zero,) * 8, unroll=2)
            for k in range(FFN_CONV_K):
                dwg_ref[k:k + 1, cols] += _rsum(out[2 + k])
                dwv_ref[k:k + 1, cols] += _rsum(out[5 + k])
            dbg_ref[:, cols] += _rsum(out[8])
            dbv_ref[:, cols] += _rsum(out[9])

    tile, prev, vec = _ffn_specs(s, ts, tc, hb)
    half = pl.BlockSpec((ts, tc), lambda j, i: (i, j))
    wacc = pl.BlockSpec((FFN_CONV_K, tc), lambda j, i: (0, j))
    bacc = pl.BlockSpec((1, tc), lambda j, i: (0, j))
    return _hbm_call(
        body, name=name,
        grid=(hb, s // ts),
        in_specs=[tile(0), prev(0), tile(hb), prev(hb), half, vec(FFN_CONV_K, 0), vec(FFN_CONV_K, hb), vec(1, 0), vec(1, hb)],
        out_specs=[half, half, wacc, wacc, bacc, bacc],
        out_shape=[jax.ShapeDtypeStruct((s, f), F32)] * 2 + [jax.ShapeDtypeStruct((FFN_CONV_K, f), F32)] * 2
        + [jax.ShapeDtypeStruct((1, f), F32)] * 2,
        compiler_params=pltpu.CompilerParams(dimension_semantics=("arbitrary", "arbitrary"),
                                             vmem_limit_bytes=_vmem_limit(12 * ts * tc * 4)),
    )(p, p, p, p, dact, cw, cw, cb, cb)


def _ffn_in_bwd(dug, duv, cw, w, x, dres, gain, s1p, name):
    s, f = dug.shape
    d = x.shape[1]
    ts = _row_tile(s, 256)
    tc = w.shape[2]
    assert f % tc == 0 and w.shape[0] * tc == 2 * f
    nbh = ts // FFN_HALO

    def body(dug_ref, dugn_ref, duv_ref, duvn_ref, cw_ref, w_ref, x_ref, dres_ref, g_ref, s_ref,
             dx_ref, dp_ref, dsh_ref, dsc_ref, dg_ref, ext):
        i = pl.program_id(0)
        last = i == pl.num_programs(0) - 1

        @pl.when(i == 0)
        def _():
            for ref in (dsh_ref, dsc_ref, dg_ref):
                ref[...] = jnp.zeros_like(ref)

        dh = jnp.zeros((ts, d), F32)
        for half, (t_ref, n_ref) in enumerate(((dug_ref, dugn_ref), (duv_ref, duvn_ref))):
            for cb in range(f // tc):
                cols = slice(cb * tc, (cb + 1) * tc)
                wcols = slice(half * f + cb * tc, half * f + (cb + 1) * tc)
                ext[pl.ds(0, ts), :] = t_ref[:, cols]
                ext[pl.ds(ts, FFN_HALO), :] = jnp.where(last, 0.0, n_ref[:, cols])
                acc = cw_ref[FFN_CONV_K - 1:FFN_CONV_K, wcols] * t_ref[:, cols]
                for k in range(FFN_CONV_K - 1):
                    acc = acc + cw_ref[k:k + 1, wcols] * ext[pl.ds(FFN_CONV_K - 1 - k, ts), :]
                dpb = acc.astype(BF16)
                dp_ref[:, wcols] = dpb
                dh = dh + _dot_nt(dpb, w_ref[half * (f // tc) + cb])
        dx, dsh, dsc, dg = _norm_mod_bwd(dh, x_ref[...], g_ref[...], s_ref[...], dres_ref[...])
        dx_ref[...] = dx
        dsh_ref[...] += dsh
        dsc_ref[...] += dsc
        dg_ref[...] += dg

    tile = pl.BlockSpec((ts, f), lambda i: (i, 0))
    nxt = pl.BlockSpec((FFN_HALO, f), lambda i: (jnp.minimum((i + 1) * nbh, s // FFN_HALO - 1), 0))
    row = pl.BlockSpec((ts, d), lambda i: (i, 0))
    vec = pl.BlockSpec((1, d), lambda i: (0, 0))
    return _hbm_call(
        body, name=name,
        grid=(s // ts,),
        in_specs=[tile, nxt, tile, nxt, _const_spec(cw.shape),
                  pl.BlockSpec(w.shape, lambda i: (0, 0, 0), pipeline_mode=pl.Buffered(1)), row, row, vec, vec],
        out_specs=[row, pl.BlockSpec((ts, 2 * f), lambda i: (i, 0)), vec, vec, vec],
        out_shape=[jax.ShapeDtypeStruct((s, d), F32), jax.ShapeDtypeStruct((s, 2 * f), BF16)] + [jax.ShapeDtypeStruct((1, d), F32)] * 3,
        scratch_shapes=[pltpu.VMEM((ts + FFN_HALO, tc), F32)],
        compiler_params=pltpu.CompilerParams(
            dimension_semantics=("arbitrary",),
            vmem_limit_bytes=_vmem_limit(4 * ts * f * 4 + 2 * f * d * 2 + 2 * ts * 2 * f * 2 + 12 * ts * d * 4 + 6 * ts * tc * 4)),
    )(dug, dug, duv, duv, cw, w, x, dres, gain, s1p)


def _adamw_math(w, g, m, v):
    m = ADAM_B1 * m + (1.0 - ADAM_B1) * g
    v = ADAM_B2 * v + (1.0 - ADAM_B2) * (g * g)
    m_hat = m / (1.0 - ADAM_B1 ** ADAM_STEP)
    v_hat = v / (1.0 - ADAM_B2 ** ADAM_STEP)
    delta = -ADAM_LR * (m_hat / (jnp.sqrt(v_hat) + ADAM_EPS) + ADAM_WD * w)
    return delta, m, v


def _adam_rows(rows, cols):
    want = max(8, (2 * 1024 * 1024 // (cols * 4)) // 8 * 8)
    tr = min(rows, want)
    while rows % tr:
        tr -= 8
    return tr


def _adamw(w, m, v, g_parts, name):
    shape = w.shape
    nl = shape[0] if w.ndim == 3 else 1
    r, c = shape[-2], shape[-1]
    tr = _adam_rows(r, c)
    ng = len(g_parts)

    def body(*refs):
        w_ref, m_ref, v_ref = refs[0:3]
        g_refs = refs[3:3 + ng]
        g_out, d_out, m_out, v_out = refs[3 + ng:]
        g = g_refs[0][...]
        for gr in g_refs[1:]:
            g = g + gr[...]
        delta, mn, vn = _adamw_math(w_ref[...], g, m_ref[...], v_ref[...])
        g_out[...] = g
        d_out[...] = delta
        m_out[...] = mn
        v_out[...] = vn

    blk = pl.BlockSpec((None, tr, c), lambda l, i: (l, i, 0))
    outs = _hbm_call(
        body, name=name,
        grid=(nl, r // tr),
        in_specs=[blk] * (3 + ng),
        out_specs=[blk] * 4,
        out_shape=[jax.ShapeDtypeStruct((nl, r, c), F32)] * 4,
        compiler_params=pltpu.CompilerParams(dimension_semantics=("arbitrary", "arbitrary"),
                                             vmem_limit_bytes=_vmem_limit(2 * (7 + ng) * tr * max(c, 128) * 4 + (8 << 20))),
    )(*[a.reshape(nl, r, c) for a in (w, m, v, *g_parts)])
    return [o.reshape(shape) for o in outs]


def _modw_adamw(sct, dmod, w, m, v, name):
    nl, d, n = w.shape
    tr = _row_tile(d, 128)

    def body(sct_ref, dm_ref, w_ref, m_ref, v_ref, g_out, d_out, m_out, v_out):
        sc = sct_ref[...].astype(BF16).astype(F32)
        dm = dm_ref[...].astype(BF16).astype(F32)
        g = sc[:, 0:1] * dm[0:1, :]
        for b in range(1, N_DEV):
            g = g + sc[:, b:b + 1] * dm[b:b + 1, :]
        delta, mn, vn = _adamw_math(w_ref[...], g, m_ref[...], v_ref[...])
        g_out[...] = g
        d_out[...] = delta
        m_out[...] = mn
        v_out[...] = vn

    blk = pl.BlockSpec((None, tr, n), lambda l, i: (l, i, 0))
    return _hbm_call(
        body, name=name,
        grid=(nl, d // tr),
        in_specs=[pl.BlockSpec((tr, N_DEV), lambda l, i: (i, 0)), pl.BlockSpec((None, N_DEV, n), lambda l, i: (l, 0, 0)),
                  blk, blk, blk],
        out_specs=[blk] * 4,
        out_shape=[jax.ShapeDtypeStruct((nl, d, n), F32)] * 4,
        compiler_params=pltpu.CompilerParams(dimension_semantics=("arbitrary", "arbitrary"),
                                             vmem_limit_bytes=_vmem_limit(2 * 8 * tr * n * 4 + (8 << 20))),
    )(sct, dmod, w, m, v)


def _reduce4(recvs, name):
    nl = len(recvs)
    shape = recvs[0].shape[1:]
    c = shape[-1]
    r = math.prod(shape[:-1])
    tr = _adam_rows(r, c)
    nt = r // tr

    def body(*refs):
        o_ref = refs[nl]
        for l in range(nl):
            @pl.when(pl.program_id(0) == l)
            def _():
                acc = refs[l][0].astype(F32)
                for k in range(1, N_CHIPS):
                    acc = acc + refs[l][k].astype(F32)
                o_ref[...] = acc

    def in_map(l):
        return lambda ll, i: (0, jnp.where(ll < l, 0, jnp.where(ll > l, nt - 1, i)), 0)

    return _hbm_call(
        body, name=name,
        grid=(nl, nt),
        in_specs=[pl.BlockSpec((N_CHIPS, tr, c), in_map(l)) for l in range(nl)],
        out_specs=pl.BlockSpec((None, tr, c), lambda ll, i: (ll, i, 0)),
        out_shape=jax.ShapeDtypeStruct((nl, r, c), F32),
        compiler_params=pltpu.CompilerParams(dimension_semantics=("arbitrary", "arbitrary"),
                                             vmem_limit_bytes=_vmem_limit(2 * 8 * nl * tr * max(c, 128) * 4 + (8 << 20))),
    )(*[rv.reshape(N_CHIPS, r, c) for rv in recvs]).reshape((nl,) + shape)


def _my_place():
    return lax.axis_index("x"), lax.axis_index("y"), lax.axis_index("c")


def _chip_coords(j):
    return j // 2, j % 2


def _mod_forward(c, mod_w, mod_b4):
    nl, d, n = mod_w.shape
    kc = 256

    def body(c_ref, w_ref, b_ref, mod_ref, sc_ref, cbuf, stage, s1, r1, s2, r2):
        mx, my, mc = _my_place()
        me = 4 * mx + 2 * my + mc
        q = 2 * mx + my
        cv = c_ref[...]
        cbuf[me] = jnp.broadcast_to(cv * jax.nn.sigmoid(cv), (8, d))
        sends = []
        for t in range(N_DEV):
            tx, ty = _chip_coords(t // 2)
            cp = pltpu.make_async_remote_copy(src_ref=cbuf.at[me], dst_ref=cbuf.at[me], send_sem=s1.at[t], recv_sem=r1.at[me],
                                              device_id=(tx, ty, t % 2), device_id_type=MESH)

            @pl.when(t != me)
            def _():
                cp.start()

            sends.append((t, cp))
        for t in range(N_DEV):
            @pl.when(t != me)
            def _():
                pltpu.make_async_remote_copy(src_ref=cbuf.at[t], dst_ref=cbuf.at[t], send_sem=s1.at[t], recv_sem=r1.at[t],
                                             device_id=(mx, my, mc), device_id_type=MESH).wait_recv()
        for t, cp in sends:
            @pl.when(t != me)
            def _():
                cp.wait_send()

        row = lax.broadcasted_iota(jnp.int32, (8, d), 0)
        sc_all = jnp.zeros((8, d), F32)
        for t in range(N_DEV):
            sc_all = sc_all + jnp.where(row == t, cbuf[t], 0.0)
        sc_ref[...] = sc_all
        rown = lax.broadcasted_iota(jnp.int32, (8, n), 0)
        for l in range(nl):
            acc = jnp.zeros((8, n), F32)
            for k0 in range(0, d, kc):
                acc = acc + _dot(sc_all[:, k0:k0 + kc].astype(BF16), w_ref[l, k0:k0 + kc, :].astype(BF16))
            acc = acc + b_ref[l, q]
            for j in range(N_CHIPS):
                jx, jy = _chip_coords(j)
                bdest = 4 * jx + 2 * jy + mc
                rowv = jnp.sum(jnp.where(rown == bdest, acc, 0.0), axis=0, keepdims=True)
                stage[j, l] = jnp.broadcast_to(rowv, (8, n))
        sends2 = []
        for j in range(N_CHIPS):
            jx, jy = _chip_coords(j)
            cp = pltpu.make_async_remote_copy(src_ref=stage.at[j], dst_ref=mod_ref.at[:, q], send_sem=s2.at[j], recv_sem=r2.at[q],
                                              device_id=(jx, jy, mc), device_id_type=MESH)

            @pl.when(j != q)
            def _():
                cp.start()

            @pl.when(j == q)
            def _():
                for l in range(nl):
                    mod_ref[l, j] = stage[j, l]

            sends2.append((j, cp))
        for j in range(N_CHIPS):
            @pl.when(j != q)
            def _():
                pltpu.make_async_remote_copy(src_ref=stage.at[j], dst_ref=mod_ref.at[:, j], send_sem=s2.at[j], recv_sem=r2.at[j],
                                             device_id=(mx, my, mc), device_id_type=MESH).wait_recv()
        for j, cp in sends2:
            @pl.when(j != q)
            def _():
                cp.wait_send()

    vm = pl.BlockSpec(memory_space=pltpu.VMEM)
    return pl.pallas_call(
        body, name="mod_forward",
        in_specs=[vm, vm, vm],
        out_specs=[vm, vm],
        out_shape=[jax.ShapeDtypeStruct((nl, N_CHIPS, 8, n), F32), jax.ShapeDtypeStruct((8, d), F32)],
        scratch_shapes=[pltpu.VMEM((N_DEV, 8, d), F32), pltpu.VMEM((N_CHIPS, nl, 8, n), F32),
                        pltpu.SemaphoreType.DMA((N_DEV,)), pltpu.SemaphoreType.DMA((N_DEV,)),
                        pltpu.SemaphoreType.DMA((N_CHIPS,)), pltpu.SemaphoreType.DMA((N_CHIPS,))],
        compiler_params=pltpu.CompilerParams(vmem_limit_bytes=_vmem_limit(2 * nl * d * n * 4 + (8 << 20))),
    )(c, mod_w, mod_b4)


_HBM_SPEC = pl.BlockSpec(memory_space=pltpu.HBM)
_SEM_SPEC = pl.BlockSpec(memory_space=pltpu.SEMAPHORE)
_DATAFLOW = pltpu.SideEffectType.DATAFLOW_SIDE_EFFECTING


def _slot(ref, scatter, j):
    return ref.at[j] if scatter else ref


def _exchange_start(groups, scatter, after, name):
    flat = [a for g in groups for a in g]
    na = len(flat)
    ng = len(groups)
    sizes = [len(g) for g in groups]
    first = [sum(sizes[:g]) for g in range(ng)]
    where = [(g, k) for g in range(ng) for k in range(sizes[g])]
    mx, my, _ = _my_place()
    qo = 2 * mx + my
    lands = []
    for a in flat:
        own = lax.dynamic_index_in_dim(a, qo, 0, keepdims=False) if scatter else a
        lands.append(lax.dynamic_update_index_in_dim(lax.empty((N_CHIPS,) + own.shape, a.dtype), own, qo, 0))

    def body(*refs):
        ins, lnd = refs[:na], refs[na:2 * na]
        ssems, rsems = refs[2 * na + 1:2 * na + 1 + ng], refs[2 * na + 1 + ng:2 * na + 1 + 2 * ng]
        token = refs[-1]
        mx, my, mc = _my_place()
        q = 2 * mx + my
        for j in range(N_CHIPS):
            jx, jy = _chip_coords(j)
            for a in range(na):
                g, k = where[a]

                @pl.when(j != q)
                def _():
                    pltpu.make_async_remote_copy(src_ref=_slot(ins[a], scatter, j), dst_ref=lnd[a].at[q],
                                                 send_sem=ssems[g].at[k * N_CHIPS + j], recv_sem=rsems[g].at[k * N_CHIPS + q],
                                                 device_id=(jx, jy, mc), device_id_type=MESH).start()
        token[...] = jnp.zeros_like(token)

    sem_shapes = [pltpu.SemaphoreType.DMA((n * N_CHIPS,)) for n in sizes]
    outs = pl.pallas_call(
        body, name=name,
        in_specs=[_HBM_SPEC] * (2 * na) + [pl.BlockSpec(memory_space=pl.ANY)],
        out_specs=[_SEM_SPEC] * (2 * ng) + [_HBM_SPEC] * (2 * na) + [pl.BlockSpec(memory_space=pltpu.VMEM)],
        out_shape=sem_shapes + sem_shapes + [pltpu.HBM(a.shape, a.dtype) for a in flat + lands]
        + [jax.ShapeDtypeStruct((8, 128), F32)],
        input_output_aliases={i: 2 * ng + i for i in range(2 * na)},
        compiler_params=pltpu.CompilerParams(has_side_effects=_DATAFLOW),
    )(*[pltpu.with_memory_space_constraint(a, pltpu.HBM) for a in flat + lands], after)
    ssems, rsems = outs[:ng], outs[ng:2 * ng]
    src_thru, land_thru = outs[2 * ng:2 * ng + na], outs[2 * ng + na:2 * ng + 2 * na]
    states = [(src_thru[first[g]:first[g] + sizes[g]], land_thru[first[g]:first[g] + sizes[g]], ssems[g], rsems[g])
              for g in range(ng)]
    return states, outs[-1]


def _exchange_wait(state, scatter, after, name):
    srcs, lands, ssem, rsem = state
    na = len(srcs)

    def body(*refs):
        ins, lnd = refs[:na], refs[na:2 * na]
        ssem_ref, rsem_ref = refs[2 * na], refs[2 * na + 1]
        mx, my, mc = _my_place()
        q = 2 * mx + my
        for j in range(N_CHIPS):
            for a in range(na):
                @pl.when(j != q)
                def _():
                    cp = pltpu.make_async_remote_copy(src_ref=_slot(ins[a], scatter, j), dst_ref=lnd[a].at[j],
                                                      send_sem=ssem_ref.at[a * N_CHIPS + j], recv_sem=rsem_ref.at[a * N_CHIPS + j],
                                                      device_id=(mx, my, mc), device_id_type=MESH)
                    cp.wait_send()
                    cp.wait_recv()

    outs = pl.pallas_call(
        body, name=name,
        in_specs=[_HBM_SPEC] * (2 * na) + [_SEM_SPEC, _SEM_SPEC, pl.BlockSpec(memory_space=pl.ANY)],
        out_specs=[_HBM_SPEC] * (2 * na),
        out_shape=[pltpu.HBM(a.shape, a.dtype) for a in list(srcs) + list(lands)],
        input_output_aliases={i: i for i in range(2 * na)},
        compiler_params=pltpu.CompilerParams(has_side_effects=_DATAFLOW),
    )(*srcs, *lands, ssem, rsem, after)
    return outs[na:]


def _swap_with_sibling(arrs):
    na = len(arrs)

    def body(*refs):
        ins, outs = refs[:na], refs[na:2 * na]
        ssem, rsem = refs[2 * na:]
        mx, my, mc = _my_place()
        cps = [pltpu.make_async_remote_copy(src_ref=ins[a], dst_ref=outs[a], send_sem=ssem.at[a], recv_sem=rsem.at[a],
                                            device_id=(mx, my, 1 - mc), device_id_type=MESH) for a in range(na)]
        for cp in cps:
            cp.start()
        for cp in cps:
            cp.wait()

    hbm = pl.BlockSpec(memory_space=pl.ANY)
    return _hbm_call(
        body, name="swap_sibling",
        in_specs=[hbm] * na,
        out_specs=[hbm] * na,
        out_shape=[jax.ShapeDtypeStruct(a.shape, a.dtype) for a in arrs],
        scratch_shapes=[pltpu.SemaphoreType.DMA((na,)), pltpu.SemaphoreType.DMA((na,))],
    )(*arrs)


def _allreduce_small(rows_all, rows_sum):
    ra, c = rows_all.shape
    r = rows_sum.shape[0]
    ch = r // N_DEV
    assert ch % 8 == 0 and ch * N_DEV == r

    def body(a_ref, s_ref, all_ref, sum_ref, rbuf, red, sa, rva, sb, rvb, sc, rvc):
        mx, my, mc = _my_place()
        me = 4 * mx + 2 * my + mc
        mine = pl.ds(pl.multiple_of(me * ch, 8), ch)
        all_ref[me] = a_ref[...]
        rbuf[me] = s_ref[mine, :]

        def dev(t):
            tx, ty = _chip_coords(t // 2)
            return (tx, ty, t % 2)

        def everyone_else(fn):
            for t in range(N_DEV):
                @pl.when(t != me)
                def _():
                    fn(t)

        def copy_a(t, slot):
            return pltpu.make_async_remote_copy(src_ref=a_ref, dst_ref=all_ref.at[slot], send_sem=sa.at[t], recv_sem=rva.at[slot],
                                                device_id=dev(t), device_id_type=MESH)

        def copy_b(t, slot):
            return pltpu.make_async_remote_copy(src_ref=s_ref.at[pl.ds(t * ch, ch), :], dst_ref=rbuf.at[slot], send_sem=sb.at[t],
                                                recv_sem=rvb.at[slot], device_id=dev(t), device_id_type=MESH)

        def copy_c(t, chunk_start, slot):
            return pltpu.make_async_remote_copy(src_ref=red, dst_ref=sum_ref.at[pl.ds(chunk_start, ch), :], send_sem=sc.at[t],
                                                recv_sem=rvc.at[slot], device_id=dev(t), device_id_type=MESH)

        everyone_else(lambda t: (copy_a(t, me).start(), copy_b(t, me).start()))
        everyone_else(lambda t: (copy_a(t, t).wait_recv(), copy_b(t, t).wait_recv()))
        everyone_else(lambda t: (copy_a(t, me).wait_send(), copy_b(t, me).wait_send()))
        acc = rbuf[0]
        for t in range(1, N_DEV):
            acc = acc + rbuf[t]
        red[...] = acc
        sum_ref[mine, :] = acc
        everyone_else(lambda t: copy_c(t, pl.multiple_of(me * ch, 8), me).start())
        everyone_else(lambda t: copy_c(t, t * ch, t).wait_recv())
        everyone_else(lambda t: copy_c(t, pl.multiple_of(me * ch, 8), me).wait_send())

    vm = pl.BlockSpec(memory_space=pltpu.VMEM)
    return pl.pallas_call(
        body, name="allreduce_small",
        in_specs=[vm, vm],
        out_specs=[vm, vm],
        out_shape=[jax.ShapeDtypeStruct((N_DEV, ra, c), F32), jax.ShapeDtypeStruct((r, c), F32)],
        scratch_shapes=[pltpu.VMEM((N_DEV, ch, c), F32), pltpu.VMEM((ch, c), F32)] + [pltpu.SemaphoreType.DMA((N_DEV,))] * 6,
        compiler_params=pltpu.CompilerParams(vmem_limit_bytes=_vmem_limit((3 * r + 2 * N_DEV * ra) * c * 4 + (4 << 20))),
    )(rows_all, rows_sum)


def _pack(arrs, row_multiple=8):
    rows, layout, at = [], [], 0
    for a in arrs:
        n = a.size
        nr = -(-n // (8 * SMALL_COLS)) * 8
        flat = a.reshape(-1)
        if nr * SMALL_COLS != n:
            flat = jnp.pad(flat, (0, nr * SMALL_COLS - n))
        rows.append(flat.reshape(nr, SMALL_COLS))
        layout.append((at, nr, a.shape))
        at += nr
    pad = -at % row_multiple
    if pad:
        rows.append(jnp.zeros((pad, SMALL_COLS), F32))
    return jnp.concatenate(rows, axis=0), layout


def _unpack(buf, layout):
    out = []
    for at, nr, shape in layout:
        n = math.prod(shape)
        out.append(buf[at:at + nr].reshape(-1)[:n].reshape(shape))
    return out


SMALL_NAMES = ("mod_b", "mix_pre_g", "mix_post_g", "sgu_norm_g", "sgu_norm_b", "sgu_w", "sgu_b", "conv_b", "conv_norm_g",
               "conv_norm_b", "pool_w", "pool_scale", "branch_g", "ffn_pre_g", "ffn_post_g", "ffn_conv_b")
SHARDED_SMALL = ("conv_w", "ffn_conv_w")
WEIGHT_ORDER = ("mod_w", "mod_b", "mix_pre_g", "mix_post_g", "w_in", "sgu_norm_g", "sgu_norm_b", "sgu_w", "sgu_b", "conv_w",
                "conv_b", "conv_norm_g", "conv_norm_b", "pool_w", "pool_scale", "branch_g", "w_out", "ffn_pre_g", "ffn_post_g",
                "ffn_up", "ffn_conv_w", "ffn_conv_b", "ffn_down")


def _block_diag(blocks):
    n, a, b = blocks.shape
    eye = jnp.eye(n, dtype=blocks.dtype)
    return (eye[:, None, :, None] * blocks[:, :, None, :]).reshape(n * a, n * b)


def _diag_blocks(mat, n):
    a = mat.shape[0] // n
    return jnp.stack([mat[g * a:(g + 1) * a, g * a:(g + 1) * a] for g in range(n)])


def _step(x, c, loss_target, w, m, v):
    nl = w["mod_w"].shape[0]
    s, d = x.shape[1], x.shape[2]
    heads = SGU_WIDTH // HEAD_DIM
    groups = len(POOL_WINDOWS)
    mx, my, _ = _my_place()
    q = 2 * mx + my
    x0 = x.reshape(s, d)
    tgt = loss_target.reshape(s, d)

    nmod = w["mod_w"].shape[2]
    kin = w["w_in"].shape[2]
    inw = kin * N_CHIPS
    f2 = w["ffn_up"].shape[2] * N_CHIPS
    f = f2 // 2

    def wgroups(l):
        return [[w["w_in"][l].astype(BF16), w["conv_w"][l], w["ffn_conv_w"][l]], [w["w_out"][l].astype(BF16)],
                [w["ffn_up"][l].astype(BF16)], [w["ffn_down"][l].astype(BF16)]]

    mod4, sc_all = _mod_forward(c, w["mod_w"], w["mod_b"].reshape(nl, N_CHIPS, 1, nmod))
    mod = mod4[:, :, 0, :].reshape(nl, N_MOD, 1, d)
    gstates = {}
    (gstates[0, 0],), gtoken = _exchange_start(wgroups(0)[:1], False, mod4, "gather_start_in_0")

    tril = jnp.tril(jnp.ones((CHUNK, CHUNK), bool))
    bd = _block_diag(jnp.ones((heads, HEAD_DIM, HEAD_DIM), BF16))

    def mixer_params(l, conv_w):
        wm = jnp.where(tril[None], w["sgu_w"][l], 0.0)
        pw = _block_diag(w["pool_w"][l])
        return dict(
            bd=bd, ng=w["sgu_norm_g"][l][None], nb=w["sgu_norm_b"][l][None],
            wm=wm.astype(BF16), wmt=jnp.swapaxes(wm, 1, 2).astype(BF16),
            bias=jnp.repeat(w["sgu_b"][l].T, HEAD_DIM, axis=1),
            cw=conv_w, cb=w["conv_b"][l][None], cng=w["conv_norm_g"][l][None], cnb=w["conv_norm_b"][l][None],
            pw=pw.astype(BF16), pwt=pw.T.astype(BF16), ps=w["pool_scale"][l][None], bg=w["branch_g"][l][None])

    saved = []
    xl = x0
    arrived = {0: _exchange_wait(gstates[0, 0], False, gtoken, "gather_wait_in_0")}
    for l in range(nl):
        sh1, sc1, g1, sh2, sc2, g2 = [mod[l, k] for k in range(N_MOD)]
        gpre1, gpost1 = w["mix_pre_g"][l][None], w["mix_post_g"][l][None]
        gpre2, gpost2 = w["ffn_pre_g"][l][None], w["ffn_post_g"][l][None]
        fcb = w["ffn_conv_b"][l][None]
        sh1_after, bg_after, g1_after, sh2_after, fcb_after = sh1, w["branch_g"][l][None], g1, sh2, fcb
        g_win, g_cw, g_fcw = arrived[l][:3]
        if l == 0:
            (gstates[0, 1],), tok = _exchange_start(wgroups(0)[1:2], False, g_win, "gather_start_out_0")
            sh1_after = sh1 + tok[0:1, 0:1]
        w_in = jnp.transpose(g_win, (1, 0, 2)).reshape(d, inw)
        conv_w = jnp.transpose(g_cw, (1, 0, 2)).reshape(CONV_K, CONV_WIDTH)
        ffn_cw = jnp.transpose(g_fcw, (1, 0, 2)).reshape(FFN_CONV_K, f2)
        mp = mixer_params(l, conv_w)
        z, h1 = _norm_mod_matmul(xl, gpre1, 1.0 + sc1, sh1_after, w_in[None], f"mix_in_{l}")
        if l == 0:
            (g_wout,) = _exchange_wait(gstates[0, 1], False, z, "gather_wait_out_0")
            (gstates[0, 2],), tok = _exchange_start(wgroups(0)[2:3], False, g_wout, "gather_start_up_0")
            bg_after = bg_after + tok[0:1, 0:1]
        else:
            g_wout = arrived[l][3]
        w_out = g_wout.reshape(d, d)
        ycat = _mixer_fwd(z, dict(mp, bg=bg_after), f"mixer_fwd_{l}")
        (up,) = _exchange_wait(gstates[l, 2], False, ycat, f"gather_wait_up_{l}")
        if l == 0:
            (gstates[0, 3],), tok = _exchange_start(wgroups(0)[3:4], False, up, "gather_start_down_0")
            g1_after = g1 + tok[0:1, 0:1]
        o, x1 = _matmul_norm_resid(ycat, w_out, xl, g1_after, gpost1, f"mix_out_{l}")
        (g_down,) = _exchange_wait(gstates[l, 3], False, x1, f"gather_wait_down_{l}")
        down = g_down.reshape(f, d)
        if l + 1 < nl:
            (gstates[l + 1, 0], gstates[l + 1, 1]), tok = _exchange_start(wgroups(l + 1)[0:2], False, g_down,
                                                                        f"gather_start_in_{l + 1}")
            sh2_after = sh2 + tok[0:1, 0:1]
        p, h2 = _norm_mod_matmul(x1, gpre2, 1.0 + sc2, sh2_after, up, f"ffn_in_{l}")
        if l + 1 < nl:
            nxt = _exchange_wait(gstates[l + 1, 0], False, p, f"gather_wait_in_{l + 1}")
            nxt_out = _exchange_wait(gstates[l + 1, 1], False, nxt[0], f"gather_wait_out_{l + 1}")
            arrived[l + 1] = list(nxt) + list(nxt_out)
            (gstates[l + 1, 2], gstates[l + 1, 3]), tok = _exchange_start(wgroups(l + 1)[2:4], False, nxt_out[0],
                                                                        f"gather_start_up_{l + 1}")
            fcb_after = fcb + tok[0:1, 0:1]
        act = _ffn_act_fwd(p, ffn_cw, fcb_after, f"ffn_act_{l}")
        qo, x2 = _matmul_norm_resid(act, down, x1, g2, gpost2, f"ffn_out_{l}")
        saved.append(dict(x=xl, z=z, h1=h1, ycat=ycat, o=o, x1=x1, p=p, h2=h2, act=act, qo=qo, mp=mp, fcb=fcb,
                          w_in=w_in, w_out=w_out, up=up, down=down, ffn_cw=ffn_cw,
                          mods=(sh1, sc1, g1, sh2, sc2, g2), gains=(gpre1, gpost1, gpre2, gpost2)))
        xl = x2

    dx, loss_row = _loss_head(xl, tgt)

    small = {n: [None] * nl for n in SMALL_NAMES + SHARDED_SMALL}
    dmods = [None] * nl
    tn = f2 // N_CHIPS
    sstates = {}
    token = None
    for l in reversed(range(nl)):
        sv = saved[l]
        sh1, sc1, g1, sh2, sc2, g2 = sv["mods"]
        gpre1, gpost1, gpre2, gpost2 = sv["gains"]
        if token is not None:
            g2 = g2 + token[0:1, 0:1]
        dq, dact, dg2, dgpost2 = _resid_bwd_matmul(dx, sv["qo"], g2, gpost2, sv["down"], f"ffn_out_bwd_{l}")
        g_down = _wgrad(sv["act"], dq, (f, lambda j: 0), (d, lambda j: 0), jax.ShapeDtypeStruct((f, d), BF16),
                        (1, lambda j: (0, 0)), (f, d), f"wgrad_ffn_down_{l}")
        dug, duv, dfwg, dfwv, dfbg, dfbv = _ffn_act_bwd(sv["p"], dact, sv["ffn_cw"], sv["fcb"], f"ffn_act_bwd_{l}")
        dx1, dp, dsh2, dsc2, dgpre2 = _ffn_in_bwd(dug, duv, sv["ffn_cw"], sv["up"], sv["x1"], dx, gpre2, 1.0 + sc2,
                                                  f"ffn_in_bwd_{l}")
        g_up = _wgrad(sv["h2"], dp, (d, lambda j: 0), (tn, lambda j: j), jax.ShapeDtypeStruct((N_CHIPS, d, tn), BF16),
                      (N_CHIPS, lambda j: (j, 0, 0)), (None, d, tn), f"wgrad_ffn_up_{l}")
        (sstates[l, 0],), token = _exchange_start([[g_down.reshape(N_CHIPS, f // N_CHIPS, d), g_up]], True, g_up,
                                                  f"scatter_start_ffn_{l}")
        do, dycat, dg1, dgpost1 = _resid_bwd_matmul(dx1, sv["o"], g1 + token[0:1, 0:1], gpost1, sv["w_out"],
                                                    f"mix_out_bwd_{l}")
        g_out = _wgrad(sv["ycat"], do, (d, lambda j: 0), (d, lambda j: 0), jax.ShapeDtypeStruct((d, d), BF16),
                       (1, lambda j: (0, 0)), (d, d), f"wgrad_w_out_{l}")
        (sstates[l, 1],), token = _exchange_start([[g_out.reshape(N_CHIPS, d // N_CHIPS, d)]], True, g_out,
                                                  f"scatter_start_out_{l}")
        mp_after = dict(sv["mp"], bg=sv["mp"]["bg"] + token[0:1, 0:1])
        (dza, dcb, dd, dbg, dwm, dbias, dng, dnb, dcng, dcnb, dps, dpw) = _mixer_bwd_a(sv["z"], dycat, mp_after, f"mixer_bwd_a_{l}")
        dx, dz, dsh1, dsc1, dgpre1, dcw, dcbias = _mixer_bwd_b(
            sv["z"], dza, dcb, dd, sv["x"], dx1, gpre1, 1.0 + sc1, sv["mp"]["cw"], sv["w_in"], f"mixer_bwd_b_{l}")
        g_in = _wgrad(sv["h1"], dz, (d, lambda j: 0), (inw, lambda j: 0), jax.ShapeDtypeStruct((d, inw), BF16),
                      (1, lambda j: (0, 0)), (d, inw), f"wgrad_w_in_{l}")
        g_in_parts = jnp.transpose(g_in.reshape(d, N_CHIPS, kin), (1, 0, 2))
        if l > 0:
            (sstates[l, 2],), token = _exchange_start([[g_in_parts]], True, g_in_parts, f"scatter_start_in_{l}")

        dmods[l] = jnp.concatenate([dsh1, dsc1, dg1, dsh2, dsc2, dg2], axis=0)
        small["mix_pre_g"][l], small["mix_post_g"][l] = dgpre1[0], dgpost1[0]
        small["ffn_pre_g"][l], small["ffn_post_g"][l] = dgpre2[0], dgpost2[0]
        small["sgu_norm_g"][l], small["sgu_norm_b"][l] = dng[0], dnb[0]
        small["sgu_w"][l] = jnp.where(tril[None], dwm, 0.0)
        small["sgu_b"][l] = dbias.reshape(CHUNK, heads, HEAD_DIM).sum(-1).T
        small["conv_b"][l], small["conv_norm_g"][l], small["conv_norm_b"][l] = dcbias[0], dcng[0], dcnb[0]
        small["pool_w"][l], small["pool_scale"][l], small["branch_g"][l] = _diag_blocks(dpw, groups), dps[0], dbg[0]
        small["ffn_conv_b"][l] = jnp.concatenate([dfbg[0], dfbv[0]])
        small["conv_w"][l] = dcw
        small["ffn_conv_w"][l] = jnp.concatenate([dfwg, dfwv], axis=1)

    names = [n for n in SMALL_NAMES if n != "mod_b"] + list(SHARDED_SMALL)
    dmod_rows, _ = _pack([jnp.stack(dmods)])
    packed, layout = _pack([jnp.stack(dmods), loss_row] + [jnp.stack(small[n]) for n in names], 8 * N_DEV)
    gathered, summed = _allreduce_small(dmod_rows, packed)
    (sstates[0, 2],), token = _exchange_start([[g_in_parts]], True, summed, "scatter_start_in_0")
    parts = _unpack(summed, layout)
    loss = parts[1][0, 0]
    gsmall = dict(zip(names, parts[2:]))
    gsmall["mod_b"] = parts[0].reshape(nl, N_MOD * d)
    dmod_all = gathered[:, :nl * N_MOD].reshape(N_DEV, nl, N_MOD * d)
    dmod_mine = jnp.transpose(lax.dynamic_slice_in_dim(dmod_all, q * nmod, nmod, axis=2), (1, 0, 2))

    grads, deltas, new_m, new_v = {}, {}, {}, {}

    def put(name, res):
        grads[name], deltas[name], new_m[name], new_v[name] = res

    put("mod_w", _modw_adamw(sc_all.T + token[0:1, 0:1], dmod_mine, w["mod_w"], m["mod_w"], v["mod_w"], "adamw_mod_w"))

    pw_, lay = _pack([w[n] for n in SMALL_NAMES])
    pm_, _ = _pack([m[n] for n in SMALL_NAMES])
    pv_, _ = _pack([v[n] for n in SMALL_NAMES])
    pg_, _ = _pack([gsmall[n] for n in SMALL_NAMES])
    res = _adamw(pw_, pm_, pv_, [pg_], "adamw_small")
    for n, g_, d_, m_, v_ in zip(SMALL_NAMES, *[_unpack(r_, lay) for r_ in res]):
        put(n, (g_, d_, m_, v_))

    gsh = {"conv_w": lax.dynamic_slice_in_dim(gsmall["conv_w"], q * (CONV_WIDTH // N_CHIPS), CONV_WIDTH // N_CHIPS, axis=2),
           "ffn_conv_w": lax.dynamic_slice_in_dim(gsmall["ffn_conv_w"], q * (f2 // N_CHIPS), f2 // N_CHIPS, axis=2)}
    pw_, lay = _pack([w[n] for n in SHARDED_SMALL])
    pm_, _ = _pack([m[n] for n in SHARDED_SMALL])
    pv_, _ = _pack([v[n] for n in SHARDED_SMALL])
    pg_, _ = _pack([gsh[n] for n in SHARDED_SMALL])
    res = _adamw(pw_, pm_, pv_, [pg_], "adamw_sharded_small")
    for n, g_, d_, m_, v_ in zip(SHARDED_SMALL, *[_unpack(r_, lay) for r_ in res]):
        put(n, (g_, d_, m_, v_))

    recv = dict(w_in=[None] * nl, w_out=[None] * nl, ffn_up=[None] * nl, ffn_down=[None] * nl)
    done = grads["mod_w"]
    for l in reversed(range(nl)):
        recv["ffn_down"][l], recv["ffn_up"][l] = _exchange_wait(sstates[l, 0], True, done, f"scatter_wait_ffn_{l}")
        (recv["w_out"][l],) = _exchange_wait(sstates[l, 1], True, recv["ffn_up"][l], f"scatter_wait_out_{l}")
        (recv["w_in"][l],) = _exchange_wait(sstates[l, 2], True, recv["w_out"][l], f"scatter_wait_in_{l}")
        done = recv["w_in"][l]
    big = ("w_in", "w_out", "ffn_up", "ffn_down")
    mine = [_reduce4(recv[n], f"reduce4_{n}") for n in big]
    theirs = _swap_with_sibling(mine)
    for n, a, b in zip(big, mine, theirs):
        put(n, _adamw(w[n], m[n], v[n], [a, b], f"adamw_{n}"))

    return (loss, dx.reshape(1, s, d), *[grads[n] for n in WEIGHT_ORDER], *[deltas[n] for n in WEIGHT_ORDER],
            *[new_m[n] for n in WEIGHT_ORDER], *[new_v[n] for n in WEIGHT_ORDER])


def kernel(x, c, mod_w, mod_b, mix_pre_g, mix_post_g, w_in, sgu_norm_g, sgu_norm_b, sgu_w, sgu_b, conv_w, conv_b, conv_norm_g, conv_norm_b, pool_w, pool_scale, branch_g, w_out, ffn_pre_g, ffn_post_g, ffn_up, ffn_conv_w, ffn_conv_b, ffn_down, loss_target, m_mod_w, m_mod_b, m_mix_pre_g, m_mix_post_g, m_w_in, m_sgu_norm_g, m_sgu_norm_b, m_sgu_w, m_sgu_b, m_conv_w, m_conv_b, m_conv_norm_g, m_conv_norm_b, m_pool_w, m_pool_scale, m_branch_g, m_w_out, m_ffn_pre_g, m_ffn_post_g, m_ffn_up, m_ffn_conv_w, m_ffn_conv_b, m_ffn_down, v_mod_w, v_mod_b, v_mix_pre_g, v_mix_post_g, v_w_in, v_sgu_norm_g, v_sgu_norm_b, v_sgu_w, v_sgu_b, v_conv_w, v_conv_b, v_conv_norm_g, v_conv_norm_b, v_pool_w, v_pool_scale, v_branch_g, v_w_out, v_ffn_pre_g, v_ffn_post_g, v_ffn_up, v_ffn_conv_w, v_ffn_conv_b, v_ffn_down):
    w = dict(mod_w=mod_w, mod_b=mod_b, mix_pre_g=mix_pre_g, mix_post_g=mix_post_g, w_in=w_in, sgu_norm_g=sgu_norm_g,
             sgu_norm_b=sgu_norm_b, sgu_w=sgu_w, sgu_b=sgu_b, conv_w=conv_w, conv_b=conv_b, conv_norm_g=conv_norm_g,
             conv_norm_b=conv_norm_b, pool_w=pool_w, pool_scale=pool_scale, branch_g=branch_g, w_out=w_out,
             ffn_pre_g=ffn_pre_g, ffn_post_g=ffn_post_g, ffn_up=ffn_up, ffn_conv_w=ffn_conv_w, ffn_conv_b=ffn_conv_b,
             ffn_down=ffn_down)
    m = dict(mod_w=m_mod_w, mod_b=m_mod_b, mix_pre_g=m_mix_pre_g, mix_post_g=m_mix_post_g, w_in=m_w_in,
             sgu_norm_g=m_sgu_norm_g, sgu_norm_b=m_sgu_norm_b, sgu_w=m_sgu_w, sgu_b=m_sgu_b, conv_w=m_conv_w,
             conv_b=m_conv_b, conv_norm_g=m_conv_norm_g, conv_norm_b=m_conv_norm_b, pool_w=m_pool_w,
             pool_scale=m_pool_scale, branch_g=m_branch_g, w_out=m_w_out, ffn_pre_g=m_ffn_pre_g, ffn_post_g=m_ffn_post_g,
             ffn_up=m_ffn_up, ffn_conv_w=m_ffn_conv_w, ffn_conv_b=m_ffn_conv_b, ffn_down=m_ffn_down)
    v = dict(mod_w=v_mod_w, mod_b=v_mod_b, mix_pre_g=v_mix_pre_g, mix_post_g=v_mix_post_g, w_in=v_w_in,
             sgu_norm_g=v_sgu_norm_g, sgu_norm_b=v_sgu_norm_b, sgu_w=v_sgu_w, sgu_b=v_sgu_b, conv_w=v_conv_w,
             conv_b=v_conv_b, conv_norm_g=v_conv_norm_g, conv_norm_b=v_conv_norm_b, pool_w=v_pool_w,
             pool_scale=v_pool_scale, branch_g=v_branch_g, w_out=v_w_out, ffn_pre_g=v_ffn_pre_g, ffn_post_g=v_ffn_post_g,
             ffn_up=v_ffn_up, ffn_conv_w=v_ffn_conv_w, ffn_conv_b=v_ffn_conv_b, ffn_down=v_ffn_down)
    return _step(x, c, loss_target, w, m, v)
```

```python
import functools
import math

import jax
import jax.numpy as jnp
from jax import lax
from jax.experimental import pallas as pl
from jax.experimental.pallas import tpu as pltpu

F32 = jnp.float32
BF16 = jnp.bfloat16
MESH = pl.DeviceIdType.MESH

EPS = 1e-6
HEAD_DIM = 64
CHUNK = 128
SGU_WIDTH = 384
CONV_WIDTH = 384
POOL_WIDTH = 256
POOL_WINDOWS = (2, 4, 8, 16)
CONV_K = 31
FFN_CONV_K = 3
N_MOD = 6
N_CHIPS = 4
N_DEV = 8

ADAM_LR = 0.001
ADAM_B1 = 0.9
ADAM_B2 = 0.999
ADAM_EPS = 1e-08
ADAM_WD = 0.01
ADAM_STEP = 10

MIX_HALO = 32
FFN_HALO = 8
FFN_ROWS = 16
FFN_UNROLL = 8
LANES = 128
CONV_ROWS = 32
SMALL_COLS = 1024
VMEM_BYTES_V7X = 64 * 1024 * 1024


def _vmem_limit(estimate_bytes):
    return int(min(max(estimate_bytes, 16 * 1024 * 1024), VMEM_BYTES_V7X - 8 * 1024 * 1024))


def _row_tile(s, want):
    return want if s % want == 0 else math.gcd(s, want)


def _rsum(v):
    return jnp.sum(v, axis=0, keepdims=True)


def _rmean(v):
    return jnp.mean(v, axis=-1, keepdims=True)


def _gelu(v):
    k = math.sqrt(2.0 / math.pi)
    return 0.5 * v * (1.0 + jnp.tanh(k * (v + 0.044715 * v * v * v)))


def _gelu_grad(v):
    k = math.sqrt(2.0 / math.pi)
    t = jnp.tanh(k * (v + 0.044715 * v * v * v))
    return 0.5 * (1.0 + t) + 0.5 * v * (1.0 - t * t) * (k * (1.0 + 3.0 * 0.044715 * v * v))


def _dot(a, b):
    return jnp.dot(a, b, preferred_element_type=F32)


def _dot_nt(a, b):
    return lax.dot_general(a, b, (((1,), (1,)), ((), ())), preferred_element_type=F32)


def _dot_tn(a, b):
    return lax.dot_general(a, b, (((0,), (0,)), ((), ())), preferred_element_type=F32)


def _group_mean(v, bd):
    hi = v.astype(BF16)
    lo = (v - hi.astype(F32)).astype(BF16)
    return (_dot(hi, bd) + _dot(lo, bd)) * (1.0 / HEAD_DIM)


def _const_spec(shape):
    nd = len(shape)
    return pl.BlockSpec(shape, lambda *_: (0,) * nd)


def _hbm_call(body, **kw):
    call = pl.pallas_call(body, **kw)
    return lambda *args: call(*[pltpu.with_memory_space_constraint(a, pltpu.HBM) for a in args])


def _norm_mod_matmul(x, gain, s1p, shift, w, name):
    s, d = x.shape
    nb, _, tn = w.shape
    ts = _row_tile(s, 512 if nb * tn <= 2048 else 256)

    def body(x_ref, g_ref, s_ref, b_ref, w_ref, z_ref, h_ref):
        xv = x_ref[...]
        r = lax.rsqrt(_rmean(xv * xv) + EPS)
        h = ((xv * r) * g_ref[...] * s_ref[...] + b_ref[...]).astype(BF16)
        h_ref[...] = h
        for j in range(nb):
            z_ref[:, j * tn:(j + 1) * tn] = _dot(h, w_ref[j])

    vec = pl.BlockSpec((1, d), lambda i: (0, 0))
    return _hbm_call(
        body, name=name,
        grid=(s // ts,),
        in_specs=[pl.BlockSpec((ts, d), lambda i: (i, 0)), vec, vec, vec,
                  pl.BlockSpec((nb, d, tn), lambda i: (0, 0, 0), pipeline_mode=pl.Buffered(1))],
        out_specs=[pl.BlockSpec((ts, nb * tn), lambda i: (i, 0)), pl.BlockSpec((ts, d), lambda i: (i, 0))],
        out_shape=[jax.ShapeDtypeStruct((s, nb * tn), F32), jax.ShapeDtypeStruct((s, d), BF16)],
        compiler_params=pltpu.CompilerParams(
            dimension_semantics=("arbitrary",),
            vmem_limit_bytes=_vmem_limit(2 * (ts * d * 4 + ts * nb * tn * 4 + ts * d * 2) + nb * d * tn * 2 + 4 * ts * d * 4)),
    )(x, gain, s1p, shift, w)


def _matmul_norm_resid(a, w, xres, gate, gpost, name):
    s, k = a.shape
    d = w.shape[1]
    ts = _row_tile(s, 512)

    def body(a_ref, w_ref, x_ref, gate_ref, gp_ref, o_ref, xn_ref):
        o = _dot(a_ref[...], w_ref[...])
        o_ref[...] = o
        r = lax.rsqrt(_rmean(o * o) + EPS)
        xn_ref[...] = x_ref[...] + gate_ref[...] * ((o * r) * gp_ref[...])

    vec = pl.BlockSpec((1, d), lambda i: (0, 0))
    row = pl.BlockSpec((ts, d), lambda i: (i, 0))
    return _hbm_call(
        body, name=name,
        grid=(s // ts,),
        in_specs=[pl.BlockSpec((ts, k), lambda i: (i, 0)),
                  pl.BlockSpec((k, d), lambda i: (0, 0), pipeline_mode=pl.Buffered(1)), row, vec, vec],
        out_specs=[row, row],
        out_shape=[jax.ShapeDtypeStruct((s, d), F32)] * 2,
        compiler_params=pltpu.CompilerParams(
            dimension_semantics=("arbitrary",),
            vmem_limit_bytes=_vmem_limit(2 * (ts * k * 2 + 3 * ts * d * 4) + k * d * 2 + 4 * ts * d * 4)),
    )(a, w, xres, gate, gpost)


def _wgrad(a, b, acols, bcols, out_struct, out_index, out_block, name):
    s = a.shape[0]
    ts = _row_tile(s, 512)
    aw, afn = acols
    bw, bfn = bcols
    nj = out_index[0]
    oidx = out_index[1]

    def body(a_ref, b_ref, o_ref, acc):
        i = pl.program_id(1)

        @pl.when(i == 0)
        def _():
            acc[...] = jnp.zeros_like(acc)

        acc[...] += _dot_tn(a_ref[...], b_ref[...])

        @pl.when(i == pl.num_programs(1) - 1)
        def _():
            o_ref[...] = acc[...].astype(o_ref.dtype)

    return _hbm_call(
        body, name=name,
        grid=(nj, s // ts),
        in_specs=[pl.BlockSpec((ts, aw), lambda j, i: (i, afn(j))), pl.BlockSpec((ts, bw), lambda j, i: (i, bfn(j)))],
        out_specs=pl.BlockSpec(out_block, lambda j, i: oidx(j)),
        out_shape=out_struct,
        scratch_shapes=[pltpu.VMEM((aw, bw), F32)],
        compiler_params=pltpu.CompilerParams(
            dimension_semantics=("arbitrary", "arbitrary"),
            vmem_limit_bytes=_vmem_limit(2 * (ts * aw * 2 + ts * bw * 2) + 3 * aw * bw * 4 + ts * aw * 4)),
    )(a, b)


def _loss_head(xo, tgt):
    s, d = xo.shape
    ts = _row_tile(s, 512)

    def body(x_ref, t_ref, dx_ref, l_ref, acc):
        i = pl.program_id(0)

        @pl.when(i == 0)
        def _():
            acc[...] = jnp.zeros_like(acc)

        e = x_ref[...] - t_ref[...]
        dx_ref[...] = e * (1.0 / d)
        acc[...] += _rsum(e * e)

        @pl.when(i == pl.num_programs(0) - 1)
        def _():
            tot = jnp.sum(acc[...], axis=-1, keepdims=True) * (0.5 / d)
            l_ref[...] = jnp.broadcast_to(tot, l_ref.shape)

    row = pl.BlockSpec((ts, d), lambda i: (i, 0))
    return _hbm_call(
        body, name="loss_head",
        grid=(s // ts,),
        in_specs=[row, row],
        out_specs=[row, pl.BlockSpec((1, SMALL_COLS), lambda i: (0, 0))],
        out_shape=[jax.ShapeDtypeStruct((s, d), F32), jax.ShapeDtypeStruct((1, SMALL_COLS), F32)],
        scratch_shapes=[pltpu.VMEM((1, d), F32)],
        compiler_params=pltpu.CompilerParams(dimension_semantics=("arbitrary",)),
    )(xo, tgt)


def _resid_bwd_matmul(dxn, o, gate, gpost, w, name):
    s, d = dxn.shape
    k = w.shape[0]
    ts = _row_tile(s, 512)

    def body(dx_ref, o_ref, gate_ref, gp_ref, w_ref, do_ref, da_ref, dgate_ref, dgp_ref):
        i = pl.program_id(0)

        @pl.when(i == 0)
        def _():
            dgate_ref[...] = jnp.zeros_like(dgate_ref)
            dgp_ref[...] = jnp.zeros_like(dgp_ref)

        dx = dx_ref[...]
        o = o_ref[...]
        r = lax.rsqrt(_rmean(o * o) + EPS)
        on = o * r
        dgate_ref[...] += _rsum(dx * (on * gp_ref[...]))
        don = dx * gate_ref[...]
        dgp_ref[...] += _rsum(don * on)
        t = don * gp_ref[...]
        do = (r * (t - on * _rmean(t * on))).astype(BF16)
        do_ref[...] = do
        da_ref[...] = _dot_nt(do, w_ref[...])

    vec = pl.BlockSpec((1, d), lambda i: (0, 0))
    row = pl.BlockSpec((ts, d), lambda i: (i, 0))
    return _hbm_call(
        body, name=name,
        grid=(s // ts,),
        in_specs=[row, row, vec, vec, pl.BlockSpec((k, d), lambda i: (0, 0), pipeline_mode=pl.Buffered(1))],
        out_specs=[row, pl.BlockSpec((ts, k), lambda i: (i, 0)), vec, vec],
        out_shape=[jax.ShapeDtypeStruct((s, d), BF16), jax.ShapeDtypeStruct((s, k), F32),
                   jax.ShapeDtypeStruct((1, d), F32), jax.ShapeDtypeStruct((1, d), F32)],
        compiler_params=pltpu.CompilerParams(
            dimension_semantics=("arbitrary",),
            vmem_limit_bytes=_vmem_limit(2 * (2 * ts * d * 4 + ts * d * 2 + ts * k * 4) + d * k * 2 + 6 * ts * d * 4)),
    )(dxn, o, gate, gpost, w)


def _norm_mod_bwd(dh, xv, gain, s1p, dres):
    r = lax.rsqrt(_rmean(xv * xv) + EPS)
    xn = xv * r
    dshift = _rsum(dh)
    t = dh * xn
    dscale = _rsum(t * gain)
    dgain = _rsum(t * s1p)
    dxn = dh * (gain * s1p)
    dx = r * (dxn - xn * _rmean(dxn * xn)) + dres
    return dx, dshift, dscale, dgain


def _lane_lt(shape, bound):
    return lax.broadcasted_iota(jnp.int32, shape, 1) < bound


def _sgu_forward(z_ref, bd_ref, ng_ref, nb_ref, wm_ref, bias_ref, ts, ya_s, f_s):
    u = _gelu(z_ref[:, 0:SGU_WIDTH])
    v = _gelu(z_ref[:, SGU_WIDTH:2 * SGU_WIDTH])
    bd = bd_ref[...]
    vc = v - _group_mean(v, bd)
    rstd = lax.rsqrt(_group_mean(vc * vc, bd) + EPS)
    vhat = vc * rstd
    vn = (vhat * ng_ref[...] + nb_ref[...]).astype(BF16)
    left = _lane_lt((CHUNK, CHUNK), HEAD_DIM)
    for n in range(ts // CHUNK):
        rows = slice(n * CHUNK, (n + 1) * CHUNK)
        for p in range(SGU_WIDTH // CHUNK):
            cols = slice(p * CHUNK, (p + 1) * CHUNK)
            blk = vn[rows, cols]
            f = jnp.where(left, _dot(wm_ref[2 * p], blk), _dot(wm_ref[2 * p + 1], blk)) + bias_ref[:, cols]
            if f_s is not None:
                f_s[rows, cols] = f
            ya_s[rows, cols] = u[rows, cols] * f
    return u, vhat, rstd, vn


def _conv31_forward(z_ref, zh_ref, first, cw_ref, cb_ref, ts, ext_b, cbs):
    a = z_ref[:, 2 * SGU_WIDTH:2 * SGU_WIDTH + CONV_WIDTH]
    g = z_ref[:, 2 * SGU_WIDTH + CONV_WIDTH:2 * SGU_WIDTH + 2 * CONV_WIDTH]
    ah = zh_ref[:, 2 * SGU_WIDTH:2 * SGU_WIDTH + CONV_WIDTH]
    gh = zh_ref[:, 2 * SGU_WIDTH + CONV_WIDTH:2 * SGU_WIDTH + 2 * CONV_WIDTH]
    ext_b[pl.ds(0, MIX_HALO), :] = jnp.where(first, 0.0, ah * jax.nn.sigmoid(gh))
    ext_b[pl.ds(MIX_HALO, ts), :] = a * jax.nn.sigmoid(g)
    for r in range(ts // CONV_ROWS):
        acc = jnp.broadcast_to(cb_ref[...], (CONV_ROWS, CONV_WIDTH))
        for k in range(CONV_K):
            acc = acc + cw_ref[k:k + 1, :] * ext_b[pl.ds(MIX_HALO - (CONV_K - 1) + k + r * CONV_ROWS, CONV_ROWS), :]
        cbs[pl.ds(r * CONV_ROWS, CONV_ROWS), :] = acc


def _pool_counts(i, ts):
    pos1 = (i * ts + 1 + lax.broadcasted_iota(jnp.int32, (ts, POOL_WIDTH), 0)).astype(F32)
    lane = lax.broadcasted_iota(jnp.int32, (ts, POOL_WIDTH), 1)
    gdim = POOL_WIDTH // len(POOL_WINDOWS)
    win = jnp.where(lane < gdim, float(POOL_WINDOWS[0]),
                    jnp.where(lane < 2 * gdim, float(POOL_WINDOWS[1]),
                              jnp.where(lane < 3 * gdim, float(POOL_WINDOWS[2]), float(POOL_WINDOWS[3]))))
    return jnp.minimum(pos1, win)


def _window_sums(ext, base, ts, sign):
    lane = lax.broadcasted_iota(jnp.int32, (ts, POOL_WIDTH), 1)
    gdim = POOL_WIDTH // len(POOL_WINDOWS)
    run = jnp.zeros((ts, POOL_WIDTH), F32)
    out = jnp.zeros((ts, POOL_WIDTH), F32)
    for m in range(POOL_WINDOWS[-1]):
        run = run + ext[pl.ds(base + sign * m, ts), :]
        for gi, win in enumerate(POOL_WINDOWS):
            if m == win - 1:
                out = jnp.where((lane >= gi * gdim) & (lane < (gi + 1) * gdim), run, out)
    return out


def _pool_forward(z_ref, zh_ref, first, i, ts, ext_c):
    c0 = 2 * SGU_WIDTH + 2 * CONV_WIDTH
    zc = z_ref[:, c0:c0 + POOL_WIDTH]
    ext_c[pl.ds(0, MIX_HALO), :] = jnp.where(first, 0.0, zh_ref[:, c0:c0 + POOL_WIDTH])
    ext_c[pl.ds(MIX_HALO, ts), :] = zc
    sums = _window_sums(ext_c, MIX_HALO, ts, -1)
    return sums / _pool_counts(i, ts) - zc


def _layer_norm_rows(v):
    mu = _rmean(v)
    vc = v - mu
    rstd = lax.rsqrt(_rmean(vc * vc) + EPS)
    return vc * rstd, rstd


def _mixer_specs(s, ts, width):
    nbh = ts // MIX_HALO
    tile = pl.BlockSpec((ts, width), lambda i: (i, 0))
    prev = pl.BlockSpec((MIX_HALO, width), lambda i: (jnp.maximum(i * nbh - 1, 0), 0))
    nxt = pl.BlockSpec((MIX_HALO, width), lambda i: (jnp.minimum((i + 1) * nbh, s // MIX_HALO - 1), 0))
    return tile, prev, nxt


def _mixer_fwd(z, mp, name):
    s, inw = z.shape
    d = SGU_WIDTH + CONV_WIDTH + POOL_WIDTH
    ts = _row_tile(s, 256)

    def body(z_ref, zh_ref, bd_ref, ng_ref, nb_ref, wm_ref, bias_ref, cw_ref, cb_ref, cng_ref, cnb_ref,
             pw_ref, ps_ref, bg_ref, y_ref, ya_s, ext_b, cbs, ext_c):
        i = pl.program_id(0)
        first = i == 0
        _sgu_forward(z_ref, bd_ref, ng_ref, nb_ref, wm_ref, bias_ref, ts, ya_s, None)
        ya = ya_s[...]
        ra = lax.rsqrt(_rmean(ya * ya) + EPS)
        y_ref[:, 0:SGU_WIDTH] = ((ya * ra) * bg_ref[:, 0:SGU_WIDTH]).astype(BF16)

        _conv31_forward(z_ref, zh_ref, first, cw_ref, cb_ref, ts, ext_b, cbs)
        chat, _ = _layer_norm_rows(cbs[...])
        lin = chat * cng_ref[...] + cnb_ref[...]
        yb = lin * jax.nn.sigmoid(lin)
        rb = lax.rsqrt(_rmean(yb * yb) + EPS)
        y_ref[:, SGU_WIDTH:SGU_WIDTH + CONV_WIDTH] = ((yb * rb) * bg_ref[:, SGU_WIDTH:SGU_WIDTH + CONV_WIDTH]).astype(BF16)

        dpool = _pool_forward(z_ref, zh_ref, first, i, ts, ext_c)
        yc = _dot(dpool.astype(BF16), pw_ref[...]) * ps_ref[...]
        rc = lax.rsqrt(_rmean(yc * yc) + EPS)
        y_ref[:, SGU_WIDTH + CONV_WIDTH:d] = ((yc * rc) * bg_ref[:, SGU_WIDTH + CONV_WIDTH:d]).astype(BF16)

    tile, prev, _ = _mixer_specs(s, ts, inw)
    consts = [mp["bd"], mp["ng"], mp["nb"], mp["wm"], mp["bias"], mp["cw"], mp["cb"], mp["cng"], mp["cnb"],
              mp["pw"], mp["ps"], mp["bg"]]
    return _hbm_call(
        body, name=name,
        grid=(s // ts,),
        in_specs=[tile, prev] + [_const_spec(c.shape) for c in consts],
        out_specs=pl.BlockSpec((ts, d), lambda i: (i, 0)),
        out_shape=jax.ShapeDtypeStruct((s, d), BF16),
        scratch_shapes=[pltpu.VMEM((ts, SGU_WIDTH), F32), pltpu.VMEM((ts + MIX_HALO, CONV_WIDTH), F32),
                        pltpu.VMEM((ts, CONV_WIDTH), F32), pltpu.VMEM((ts + MIX_HALO, POOL_WIDTH), F32)],
        compiler_params=pltpu.CompilerParams(dimension_semantics=("arbitrary",),
                                             vmem_limit_bytes=_vmem_limit(16 * ts * inw * 4)),
    )(z, z, *consts)


def _mixer_bwd_a(z, dy, mp, name):
    s, inw = z.shape
    d = SGU_WIDTH + CONV_WIDTH + POOL_WIDTH
    ts = _row_tile(s, 256)
    nchunk = ts // CHUNK

    def rms_bwd(dyn, y, g):
        r = lax.rsqrt(_rmean(y * y) + EPS)
        yn = y * r
        dg = _rsum(dyn * yn)
        t = dyn * g
        return r * (t - yn * _rmean(t * yn)), dg

    def body(z_ref, zh_ref, dy_ref, bd_ref, ng_ref, nb_ref, wm_ref, wmt_ref, bias_ref, cw_ref, cb_ref, cng_ref, cnb_ref,
             pw_ref, pwt_ref, ps_ref, bg_ref,
             dza_ref, dcb_ref, dd_ref, dbg_ref, dwm_ref, dbias_ref, dng_ref, dnb_ref, dcng_ref, dcnb_ref, dps_ref, dpw_ref,
             ya_s, f_s, dvn_s, ext_b, cbs, ext_c):
        i = pl.program_id(0)
        first = i == 0

        @pl.when(first)
        def _():
            for ref in (dwm_ref, dbias_ref, dng_ref, dnb_ref, dcng_ref, dcnb_ref, dps_ref, dpw_ref):
                ref[...] = jnp.zeros_like(ref)

        u, vhat, rstd, vn = _sgu_forward(z_ref, bd_ref, ng_ref, nb_ref, wm_ref, bias_ref, ts, ya_s, f_s)
        dya, dbg_a = rms_bwd(dy_ref[:, 0:SGU_WIDTH], ya_s[...], bg_ref[:, 0:SGU_WIDTH])
        du = dya * f_s[...]
        df = dya * u
        dfb = df.astype(BF16)
        left = _lane_lt((CHUNK, CHUNK), HEAD_DIM)
        zero = jnp.zeros((CHUNK, CHUNK), BF16)
        dbias = jnp.zeros((CHUNK, SGU_WIDTH), F32)
        for n in range(nchunk):
            rows = slice(n * CHUNK, (n + 1) * CHUNK)
            dbias = dbias + df[rows, :]
            for p in range(SGU_WIDTH // CHUNK):
                cols = slice(p * CHUNK, (p + 1) * CHUNK)
                dblk = dfb[rows, cols]
                vblk = vn[rows, cols]
                dwm_ref[2 * p] += _dot_nt(jnp.where(left, dblk, zero), vblk)
                dwm_ref[2 * p + 1] += _dot_nt(jnp.where(left, zero, dblk), vblk)
                dvn_s[rows, cols] = jnp.where(left, _dot(wmt_ref[2 * p], dblk), _dot(wmt_ref[2 * p + 1], dblk))
        dbias_ref[...] += dbias
        dvn = dvn_s[...]
        dng_ref[...] += _rsum(dvn * vhat)
        dnb_ref[...] += _rsum(dvn)
        dvh = dvn * ng_ref[...]
        bd = bd_ref[...]
        dv = rstd * (dvh - _group_mean(dvh, bd) - vhat * _group_mean(dvh * vhat, bd))
        dza_ref[:, 0:SGU_WIDTH] = (du * _gelu_grad(z_ref[:, 0:SGU_WIDTH])).astype(BF16)
        dza_ref[:, SGU_WIDTH:2 * SGU_WIDTH] = (dv * _gelu_grad(z_ref[:, SGU_WIDTH:2 * SGU_WIDTH])).astype(BF16)

        _conv31_forward(z_ref, zh_ref, first, cw_ref, cb_ref, ts, ext_b, cbs)
        chat, crstd = _layer_norm_rows(cbs[...])
        lin = chat * cng_ref[...] + cnb_ref[...]
        sl = jax.nn.sigmoid(lin)
        dyb, dbg_b = rms_bwd(dy_ref[:, SGU_WIDTH:SGU_WIDTH + CONV_WIDTH], lin * sl, bg_ref[:, SGU_WIDTH:SGU_WIDTH + CONV_WIDTH])
        dlin = dyb * (sl * (1.0 + lin * (1.0 - sl)))
        dcng_ref[...] += _rsum(dlin * chat)
        dcnb_ref[...] += _rsum(dlin)
        dch = dlin * cng_ref[...]
        dcb_ref[...] = crstd * (dch - _rmean(dch) - chat * _rmean(dch * chat))

        dpool = _pool_forward(z_ref, zh_ref, first, i, ts, ext_c)
        dpb = dpool.astype(BF16)
        ycp = _dot(dpb, pw_ref[...])
        dyc, dbg_c = rms_bwd(dy_ref[:, SGU_WIDTH + CONV_WIDTH:d], ycp * ps_ref[...], bg_ref[:, SGU_WIDTH + CONV_WIDTH:d])
        dps_ref[...] += _rsum(dyc * ycp)
        dycp = (dyc * ps_ref[...]).astype(BF16)
        dpw_ref[...] += _dot_tn(dpb, dycp)
        dd_ref[...] = _dot(dycp, pwt_ref[...])

        @pl.when(first)
        def _():
            dbg_ref[...] = jnp.zeros_like(dbg_ref)

        dbg_ref[:, 0:SGU_WIDTH] += dbg_a
        dbg_ref[:, SGU_WIDTH:SGU_WIDTH + CONV_WIDTH] += dbg_b
        dbg_ref[:, SGU_WIDTH + CONV_WIDTH:d] += dbg_c

    tile, prev, _ = _mixer_specs(s, ts, inw)
    consts = [mp["bd"], mp["ng"], mp["nb"], mp["wm"], mp["wmt"], mp["bias"], mp["cw"], mp["cb"], mp["cng"], mp["cnb"],
              mp["pw"], mp["pwt"], mp["ps"], mp["bg"]]
    acc_shapes = [(1, d), (2 * (SGU_WIDTH // CHUNK), CHUNK, CHUNK), (CHUNK, SGU_WIDTH), (1, SGU_WIDTH), (1, SGU_WIDTH),
                  (1, CONV_WIDTH), (1, CONV_WIDTH), (1, POOL_WIDTH), (POOL_WIDTH, POOL_WIDTH)]
    return _hbm_call(
        body, name=name,
        grid=(s // ts,),
        in_specs=[tile, prev, pl.BlockSpec((ts, d), lambda i: (i, 0))] + [_const_spec(c.shape) for c in consts],
        out_specs=[pl.BlockSpec((ts, 2 * SGU_WIDTH), lambda i: (i, 0)), pl.BlockSpec((ts, CONV_WIDTH), lambda i: (i, 0)),
                   pl.BlockSpec((ts, POOL_WIDTH), lambda i: (i, 0))] + [_const_spec(a) for a in acc_shapes],
        out_shape=[jax.ShapeDtypeStruct((s, 2 * SGU_WIDTH), BF16), jax.ShapeDtypeStruct((s, CONV_WIDTH), F32),
                   jax.ShapeDtypeStruct((s, POOL_WIDTH), F32)] + [jax.ShapeDtypeStruct(a, F32) for a in acc_shapes],
        scratch_shapes=[pltpu.VMEM((ts, SGU_WIDTH), F32), pltpu.VMEM((ts, SGU_WIDTH), F32), pltpu.VMEM((ts, SGU_WIDTH), F32),
                        pltpu.VMEM((ts + MIX_HALO, CONV_WIDTH), F32), pltpu.VMEM((ts, CONV_WIDTH), F32),
                        pltpu.VMEM((ts + MIX_HALO, POOL_WIDTH), F32)],
        compiler_params=pltpu.CompilerParams(dimension_semantics=("arbitrary",),
                                             vmem_limit_bytes=_vmem_limit(24 * ts * inw * 4)),
    )(z, z, dy, *consts)


def _mixer_bwd_b(z, dza, dcb, dd, x, dres, gain, s1p, cw, w, name):
    s, inw = z.shape
    d = x.shape[1]
    ts = _row_tile(s, 256)
    c0 = 2 * SGU_WIDTH
    c1 = c0 + 2 * CONV_WIDTH

    def body(z_ref, zh_ref, dza_ref, dcb_ref, dcbn_ref, dd_ref, ddn_ref, x_ref, dres_ref, g_ref, s_ref, cw_ref, w_ref,
             dx_ref, dz_ref, dsh_ref, dsc_ref, dg_ref, dcw_ref, dcbias_ref, ext_b, ext_n, ext_e):
        i = pl.program_id(0)
        first = i == 0
        last = i == pl.num_programs(0) - 1

        @pl.when(first)
        def _():
            for ref in (dsh_ref, dsc_ref, dg_ref, dcw_ref, dcbias_ref):
                ref[...] = jnp.zeros_like(ref)

        a = z_ref[:, c0:c0 + CONV_WIDTH]
        sg = jax.nn.sigmoid(z_ref[:, c0 + CONV_WIDTH:c1])
        ah = zh_ref[:, c0:c0 + CONV_WIDTH]
        gh = zh_ref[:, c0 + CONV_WIDTH:c1]
        ext_b[pl.ds(0, MIX_HALO), :] = jnp.where(first, 0.0, ah * jax.nn.sigmoid(gh))
        ext_b[pl.ds(MIX_HALO, ts), :] = a * sg
        dcbv = dcb_ref[...]
        dcbias_ref[...] += _rsum(dcbv)
        for k in range(CONV_K):
            dcw_ref[k:k + 1, :] += _rsum(dcbv * ext_b[pl.ds(MIX_HALO - (CONV_K - 1) + k, ts), :])

        ext_n[pl.ds(0, ts), :] = dcbv
        ext_n[pl.ds(ts, MIX_HALO), :] = jnp.where(last, 0.0, dcbn_ref[...])
        for r in range(ts // CONV_ROWS):
            acc = jnp.zeros((CONV_ROWS, CONV_WIDTH), F32)
            for k in range(CONV_K):
                acc = acc + cw_ref[k:k + 1, :] * ext_n[pl.ds(CONV_K - 1 - k + r * CONV_ROWS, CONV_ROWS), :]
            rows = pl.ds(r * CONV_ROWS, CONV_ROWS)
            ar = z_ref[rows, c0:c0 + CONV_WIDTH]
            sr = jax.nn.sigmoid(z_ref[rows, c0 + CONV_WIDTH:c1])
            dz_ref[rows, c0:c0 + CONV_WIDTH] = (acc * sr).astype(BF16)
            dz_ref[rows, c0 + CONV_WIDTH:c1] = (acc * ar * sr * (1.0 - sr)).astype(BF16)

        ddv = dd_ref[...]
        ext_e[pl.ds(0, ts), :] = ddv / _pool_counts(i, ts)
        nh = (i + 1) * ts + lax.broadcasted_iota(jnp.int32, (MIX_HALO, POOL_WIDTH), 0)
        lane = lax.broadcasted_iota(jnp.int32, (MIX_HALO, POOL_WIDTH), 1)
        gdim = POOL_WIDTH // len(POOL_WINDOWS)
        winh = jnp.where(lane < gdim, float(POOL_WINDOWS[0]),
                         jnp.where(lane < 2 * gdim, float(POOL_WINDOWS[1]),
                                   jnp.where(lane < 3 * gdim, float(POOL_WINDOWS[2]), float(POOL_WINDOWS[3]))))
        cnth = jnp.minimum((nh + 1).astype(F32), winh)
        ext_e[pl.ds(ts, MIX_HALO), :] = jnp.where(last, 0.0, ddn_ref[...] / cnth)
        dz_ref[:, c1:inw] = (_window_sums(ext_e, 0, ts, 1) - ddv).astype(BF16)
        dz_ref[:, 0:c0] = dza_ref[...]

        dh = _dot_nt(dz_ref[...], w_ref[...])
        dx, dsh, dsc, dg = _norm_mod_bwd(dh, x_ref[...], g_ref[...], s_ref[...], dres_ref[...])
        dx_ref[...] = dx
        dsh_ref[...] += dsh
        dsc_ref[...] += dsc
        dg_ref[...] += dg

    tile, prev, _ = _mixer_specs(s, ts, inw)
    _, _, nxt_b = _mixer_specs(s, ts, CONV_WIDTH)
    _, _, nxt_c = _mixer_specs(s, ts, POOL_WIDTH)
    row = pl.BlockSpec((ts, d), lambda i: (i, 0))
    vec = pl.BlockSpec((1, d), lambda i: (0, 0))
    return _hbm_call(
        body, name=name,
        grid=(s // ts,),
        in_specs=[tile, prev, pl.BlockSpec((ts, c0), lambda i: (i, 0)),
                  pl.BlockSpec((ts, CONV_WIDTH), lambda i: (i, 0)), nxt_b,
                  pl.BlockSpec((ts, POOL_WIDTH), lambda i: (i, 0)), nxt_c,
                  row, row, vec, vec, _const_spec(cw.shape),
                  pl.BlockSpec(w.shape, lambda i: (0, 0), pipeline_mode=pl.Buffered(1))],
        out_specs=[row, pl.BlockSpec((ts, inw), lambda i: (i, 0)), vec, vec, vec,
                   _const_spec((CONV_K, CONV_WIDTH)), _const_spec((1, CONV_WIDTH))],
        out_shape=[jax.ShapeDtypeStruct((s, d), F32), jax.ShapeDtypeStruct((s, inw), BF16)]
        + [jax.ShapeDtypeStruct((1, d), F32)] * 3
        + [jax.ShapeDtypeStruct((CONV_K, CONV_WIDTH), F32), jax.ShapeDtypeStruct((1, CONV_WIDTH), F32)],
        scratch_shapes=[pltpu.VMEM((ts + MIX_HALO, CONV_WIDTH), F32), pltpu.VMEM((ts + MIX_HALO, CONV_WIDTH), F32),
                        pltpu.VMEM((ts + MIX_HALO, POOL_WIDTH), F32)],
        compiler_params=pltpu.CompilerParams(dimension_semantics=("arbitrary",),
                                             vmem_limit_bytes=_vmem_limit(16 * ts * inw * 4 + inw * d * 2)),
    )(z, z, dza, dcb, dcb, dd, dd, x, dres, gain, s1p, cw, w)


def _ffn_specs(s, ts, tc, half_blocks):
    nbh = ts // FFN_HALO

    def tile(off):
        return pl.BlockSpec((ts, tc), lambda j, i: (i, j + off))

    def prev(off):
        return pl.BlockSpec((FFN_HALO, tc), lambda j, i: (jnp.maximum(i * nbh - 1, 0), j + off))

    def vec(rows, off):
        return pl.BlockSpec((rows, tc), lambda j, i: (0, j + off))

    return tile, prev, vec


def _rows_before(cur, prev, k):
    row = lax.broadcasted_iota(jnp.int32, cur.shape, 0)
    return jnp.where(row >= k, pltpu.roll(cur, k, 0), pltpu.roll(prev, k, 0))


def _conv3_rows(cur, prev, w_ref, b_ref, cols):
    x1 = _rows_before(cur, prev, 1)
    x2 = _rows_before(cur, prev, 2)
    u = b_ref[:, cols] + w_ref[2:3, cols] * cur + w_ref[1:2, cols] * x1 + w_ref[0:1, cols] * x2
    return u, x2, x1


def _halo_chunk(h_ref, cols, first):
    h = jnp.where(first, 0.0, h_ref[:, cols])
    return jnp.concatenate([h] * (FFN_ROWS // FFN_HALO), axis=0)


def _ffn_act_fwd(p, cw, cb, name):
    s, f2 = p.shape
    f = f2 // 2
    tc = f // 2
    hb = f // tc
    ts = _row_tile(s, 256)

    def body(pg_ref, pgh_ref, pv_ref, pvh_ref, wg_ref, wv_ref, bg_ref, bv_ref, act_ref):
        first = pl.program_id(1) == 0
        for c in range(tc // LANES):
            cols = slice(c * LANES, (c + 1) * LANES)

            def chunk(r, carry, cols=cols):
                pg_prev, pv_prev = carry
                rows = pl.ds(pl.multiple_of(r * FFN_ROWS, FFN_ROWS), FFN_ROWS)
                pg = pg_ref[rows, cols]
                pv = pv_ref[rows, cols]
                ug, _, _ = _conv3_rows(pg, pg_prev, wg_ref, bg_ref, cols)
                uv, _, _ = _conv3_rows(pv, pv_prev, wv_ref, bv_ref, cols)
                act_ref[rows, cols] = (_gelu(ug) * uv).astype(BF16)
                return pg, pv

            def step(r, carry, chunk=chunk):
                for u in range(FFN_UNROLL):
                    carry = chunk(r * FFN_UNROLL + u, carry)
                return carry

            lax.fori_loop(0, ts // (FFN_ROWS * FFN_UNROLL), step,
                          (_halo_chunk(pgh_ref, cols, first), _halo_chunk(pvh_ref, cols, first)))

    tile, prev, vec = _ffn_specs(s, ts, tc, hb)
    return _hbm_call(
        body, name=name,
        grid=(hb, s // ts),
        in_specs=[tile(0), prev(0), tile(hb), prev(hb), vec(FFN_CONV_K, 0), vec(FFN_CONV_K, hb), vec(1, 0), vec(1, hb)],
        out_specs=pl.BlockSpec((ts, tc), lambda j, i: (i, j)),
        out_shape=jax.ShapeDtypeStruct((s, f), BF16),
        compiler_params=pltpu.CompilerParams(dimension_semantics=("arbitrary", "arbitrary"),
                                             vmem_limit_bytes=_vmem_limit(8 * ts * tc * 4)),
    )(p, p, p, p, cw, cw, cb, cb)


def _ffn_act_bwd(p, dact, cw, cb, name):
    s, f2 = p.shape
    f = f2 // 2
    tc = f // 2
    hb = f // tc
    ts = _row_tile(s, 256)

    def body(pg_ref, pgh_ref, pv_ref, pvh_ref, da_ref, wg_ref, wv_ref, bg_ref, bv_ref,
             dug_ref, duv_ref, dwg_ref, dwv_ref, dbg_ref, dbv_ref):
        first = pl.program_id(1) == 0

        @pl.when(first)
        def _():
            for ref in (dwg_ref, dwv_ref, dbg_ref, dbv_ref):
                ref[...] = jnp.zeros_like(ref)

        zero = jnp.zeros((FFN_ROWS, LANES), F32)
        for c in range(tc // LANES):
            cols = slice(c * LANES, (c + 1) * LANES)

            def chunk(r, carry, cols=cols):
                pg_prev, pv_prev, ag0, ag1, ag2, av0, av1, av2, sg, sv = carry
                rows = pl.ds(pl.multiple_of(r * FFN_ROWS, FFN_ROWS), FFN_ROWS)
                pg = pg_ref[rows, cols]
                pv = pv_ref[rows, cols]
                ug, pg2, pg1 = _conv3_rows(pg, pg_prev, wg_ref, bg_ref, cols)
                uv, pv2, pv1 = _conv3_rows(pv, pv_prev, wv_ref, bv_ref, cols)
                da = da_ref[rows, cols]
                dug = da * uv * _gelu_grad(ug)
                duv = da * _gelu(ug)
                dug_ref[rows, cols] = dug
                duv_ref[rows, cols] = duv
                return (pg, pv, ag0 + dug * pg2, ag1 + dug * pg1, ag2 + dug * pg,
                        av0 + duv * pv2, av1 + duv * pv1, av2 + duv * pv, sg + dug, sv + duv)

            def step(r, carry, chunk=chunk):
                for u in range(FFN_UNROLL):
                    carry = chunk(r * FFN_UNROLL + u, carry)
                return carry

            out = lax.fori_loop(0, ts // (FFN_ROWS * FFN_UNROLL), step,
                                (_halo_chunk(pgh_ref, cols, first), _halo_chunk(pvh_ref, cols, first)) + (zero,) * 8)
            for k in range(FFN_CONV_K):
                dwg_ref[k:k + 1, cols] += _rsum(out[2 + k])
                dwv_ref[k:k + 1, cols] += _rsum(out[5 + k])
            dbg_ref[:, cols] += _rsum(out[8])
            dbv_ref[:, cols] += _rsum(out[9])

    tile, prev, vec = _ffn_specs(s, ts, tc, hb)
    half = pl.BlockSpec((ts, tc), lambda j, i: (i, j))
    wacc = pl.BlockSpec((FFN_CONV_K, tc), lambda j, i: (0, j))
    bacc = pl.BlockSpec((1, tc), lambda j, i: (0, j))
    return _hbm_call(
        body, name=name,
        grid=(hb, s // ts),
        in_specs=[tile(0), prev(0), tile(hb), prev(hb), half, vec(FFN_CONV_K, 0), vec(FFN_CONV_K, hb), vec(1, 0), vec(1, hb)],
        out_specs=[half, half, wacc, wacc, bacc, bacc],
        out_shape=[jax.ShapeDtypeStruct((s, f), F32)] * 2 + [jax.ShapeDtypeStruct((FFN_CONV_K, f), F32)] * 2
        + [jax.ShapeDtypeStruct((1, f), F32)] * 2,
        compiler_params=pltpu.CompilerParams(dimension_semantics=("arbitrary", "arbitrary"),
                                             vmem_limit_bytes=_vmem_limit(12 * ts * tc * 4)),
    )(p, p, p, p, dact, cw, cw, cb, cb)


def _ffn_in_bwd(dug, duv, cw, w, x, dres, gain, s1p, name):
    s, f = dug.shape
    d = x.shape[1]
    ts = _row_tile(s, 256)
    tc = w.shape[2]
    assert f % tc == 0 and w.shape[0] * tc == 2 * f
    nbh = ts // FFN_HALO

    def body(dug_ref, dugn_ref, duv_ref, duvn_ref, cw_ref, w_ref, x_ref, dres_ref, g_ref, s_ref,
             dx_ref, dp_ref, dsh_ref, dsc_ref, dg_ref, ext):
        i = pl.program_id(0)
        last = i == pl.num_programs(0) - 1

        @pl.when(i == 0)
        def _():
            for ref in (dsh_ref, dsc_ref, dg_ref):
                ref[...] = jnp.zeros_like(ref)

        dh = jnp.zeros((ts, d), F32)
        for half, (t_ref, n_ref) in enumerate(((dug_ref, dugn_ref), (duv_ref, duvn_ref))):
            for cb in range(f // tc):
                cols = slice(cb * tc, (cb + 1) * tc)
                wcols = slice(half * f + cb * tc, half * f + (cb + 1) * tc)
                ext[pl.ds(0, ts), :] = t_ref[:, cols]
                ext[pl.ds(ts, FFN_HALO), :] = jnp.where(last, 0.0, n_ref[:, cols])
                acc = cw_ref[FFN_CONV_K - 1:FFN_CONV_K, wcols] * t_ref[:, cols]
                for k in range(FFN_CONV_K - 1):
                    acc = acc + cw_ref[k:k + 1, wcols] * ext[pl.ds(FFN_CONV_K - 1 - k, ts), :]
                dpb = acc.astype(BF16)
                dp_ref[:, wcols] = dpb
                dh = dh + _dot_nt(dpb, w_ref[half * (f // tc) + cb])
        dx, dsh, dsc, dg = _norm_mod_bwd(dh, x_ref[...], g_ref[...], s_ref[...], dres_ref[...])
        dx_ref[...] = dx
        dsh_ref[...] += dsh
        dsc_ref[...] += dsc
        dg_ref[...] += dg

    tile = pl.BlockSpec((ts, f), lambda i: (i, 0))
    nxt = pl.BlockSpec((FFN_HALO, f), lambda i: (jnp.minimum((i + 1) * nbh, s // FFN_HALO - 1), 0))
    row = pl.BlockSpec((ts, d), lambda i: (i, 0))
    vec = pl.BlockSpec((1, d), lambda i: (0, 0))
    return _hbm_call(
        body, name=name,
        grid=(s // ts,),
        in_specs=[tile, nxt, tile, nxt, _const_spec(cw.shape),
                  pl.BlockSpec(w.shape, lambda i: (0, 0, 0), pipeline_mode=pl.Buffered(1)), row, row, vec, vec],
        out_specs=[row, pl.BlockSpec((ts, 2 * f), lambda i: (i, 0)), vec, vec, vec],
        out_shape=[jax.ShapeDtypeStruct((s, d), F32), jax.ShapeDtypeStruct((s, 2 * f), BF16)] + [jax.ShapeDtypeStruct((1, d), F32)] * 3,
        scratch_shapes=[pltpu.VMEM((ts + FFN_HALO, tc), F32)],
        compiler_params=pltpu.CompilerParams(
            dimension_semantics=("arbitrary",),
            vmem_limit_bytes=_vmem_limit(4 * ts * f * 4 + 2 * f * d * 2 + 2 * ts * 2 * f * 2 + 12 * ts * d * 4 + 6 * ts * tc * 4)),
    )(dug, dug, duv, duv, cw, w, x, dres, gain, s1p)


def _adamw_math(w, g, m, v):
    m = ADAM_B1 * m + (1.0 - ADAM_B1) * g
    v = ADAM_B2 * v + (1.0 - ADAM_B2) * (g * g)
    m_hat = m / (1.0 - ADAM_B1 ** ADAM_STEP)
    v_hat = v / (1.0 - ADAM_B2 ** ADAM_STEP)
    delta = -ADAM_LR * (m_hat / (jnp.sqrt(v_hat) + ADAM_EPS) + ADAM_WD * w)
    return delta, m, v


def _adam_rows(rows, cols):
    want = max(8, (2 * 1024 * 1024 // (cols * 4)) // 8 * 8)
    tr = min(rows, want)
    while rows % tr:
        tr -= 8
    return tr


def _adamw(w, m, v, g_parts, name):
    shape = w.shape
    nl = shape[0] if w.ndim == 3 else 1
    r, c = shape[-2], shape[-1]
    tr = _adam_rows(r, c)
    ng = len(g_parts)

    def body(*refs):
        w_ref, m_ref, v_ref = refs[0:3]
        g_refs = refs[3:3 + ng]
        g_out, d_out, m_out, v_out = refs[3 + ng:]
        g = g_refs[0][...]
        for gr in g_refs[1:]:
            g = g + gr[...]
        delta, mn, vn = _adamw_math(w_ref[...], g, m_ref[...], v_ref[...])
        g_out[...] = g
        d_out[...] = delta
        m_out[...] = mn
        v_out[...] = vn

    blk = pl.BlockSpec((None, tr, c), lambda l, i: (l, i, 0))
    outs = _hbm_call(
        body, name=name,
        grid=(nl, r // tr),
        in_specs=[blk] * (3 + ng),
        out_specs=[blk] * 4,
        out_shape=[jax.ShapeDtypeStruct((nl, r, c), F32)] * 4,
        compiler_params=pltpu.CompilerParams(dimension_semantics=("arbitrary", "arbitrary"),
                                             vmem_limit_bytes=_vmem_limit(2 * (7 + ng) * tr * max(c, 128) * 4 + (8 << 20))),
    )(*[a.reshape(nl, r, c) for a in (w, m, v, *g_parts)])
    return [o.reshape(shape) for o in outs]


def _modw_adamw(sct, dmod, w, m, v, name):
    nl, d, n = w.shape
    tr = _row_tile(d, 128)

    def body(sct_ref, dm_ref, w_ref, m_ref, v_ref, g_out, d_out, m_out, v_out):
        sc = sct_ref[...].astype(BF16).astype(F32)
        dm = dm_ref[...].astype(BF16).astype(F32)
        g = sc[:, 0:1] * dm[0:1, :]
        for b in range(1, N_DEV):
            g = g + sc[:, b:b + 1] * dm[b:b + 1, :]
        delta, mn, vn = _adamw_math(w_ref[...], g, m_ref[...], v_ref[...])
        g_out[...] = g
        d_out[...] = delta
        m_out[...] = mn
        v_out[...] = vn

    blk = pl.BlockSpec((None, tr, n), lambda l, i: (l, i, 0))
    return _hbm_call(
        body, name=name,
        grid=(nl, d // tr),
        in_specs=[pl.BlockSpec((tr, N_DEV), lambda l, i: (i, 0)), pl.BlockSpec((None, N_DEV, n), lambda l, i: (l, 0, 0)),
                  blk, blk, blk],
        out_specs=[blk] * 4,
        out_shape=[jax.ShapeDtypeStruct((nl, d, n), F32)] * 4,
        compiler_params=pltpu.CompilerParams(dimension_semantics=("arbitrary", "arbitrary"),
                                             vmem_limit_bytes=_vmem_limit(2 * 8 * tr * n * 4 + (8 << 20))),
    )(sct, dmod, w, m, v)


def _reduce4(recvs, name):
    nl = len(recvs)
    shape = recvs[0].shape[1:]
    c = shape[-1]
    r = math.prod(shape[:-1])
    tr = _adam_rows(r, c)
    nt = r // tr

    def body(*refs):
        o_ref = refs[nl]
        for l in range(nl):
            @pl.when(pl.program_id(0) == l)
            def _():
                acc = refs[l][0].astype(F32)
                for k in range(1, N_CHIPS):
                    acc = acc + refs[l][k].astype(F32)
                o_ref[...] = acc

    def in_map(l):
        return lambda ll, i: (0, jnp.where(ll < l, 0, jnp.where(ll > l, nt - 1, i)), 0)

    return _hbm_call(
        body, name=name,
        grid=(nl, nt),
        in_specs=[pl.BlockSpec((N_CHIPS, tr, c), in_map(l)) for l in range(nl)],
        out_specs=pl.BlockSpec((None, tr, c), lambda ll, i: (ll, i, 0)),
        out_shape=jax.ShapeDtypeStruct((nl, r, c), F32),
        compiler_params=pltpu.CompilerParams(dimension_semantics=("arbitrary", "arbitrary"),
                                             vmem_limit_bytes=_vmem_limit(2 * 8 * nl * tr * max(c, 128) * 4 + (8 << 20))),
    )(*[rv.reshape(N_CHIPS, r, c) for rv in recvs]).reshape((nl,) + shape)


def _my_place():
    return lax.axis_index("x"), lax.axis_index("y"), lax.axis_index("c")


def _chip_coords(j):
    return j // 2, j % 2


def _mod_forward(c, mod_w, mod_b4):
    nl, d, n = mod_w.shape
    kc = 256

    def body(c_ref, w_ref, b_ref, mod_ref, sc_ref, cbuf, stage, s1, r1, s2, r2):
        mx, my, mc = _my_place()
        me = 4 * mx + 2 * my + mc
        q = 2 * mx + my
        cv = c_ref[...]
        cbuf[me] = jnp.broadcast_to(cv * jax.nn.sigmoid(cv), (8, d))
        sends = []
        for t in range(N_DEV):
            tx, ty = _chip_coords(t // 2)
            cp = pltpu.make_async_remote_copy(src_ref=cbuf.at[me], dst_ref=cbuf.at[me], send_sem=s1.at[t], recv_sem=r1.at[me],
                                              device_id=(tx, ty, t % 2), device_id_type=MESH)

            @pl.when(t != me)
            def _():
                cp.start()

            sends.append((t, cp))
        for t in range(N_DEV):
            @pl.when(t != me)
            def _():
                pltpu.make_async_remote_copy(src_ref=cbuf.at[t], dst_ref=cbuf.at[t], send_sem=s1.at[t], recv_sem=r1.at[t],
                                             device_id=(mx, my, mc), device_id_type=MESH).wait_recv()
        for t, cp in sends:
            @pl.when(t != me)
            def _():
                cp.wait_send()

        row = lax.broadcasted_iota(jnp.int32, (8, d), 0)
        sc_all = jnp.zeros((8, d), F32)
        for t in range(N_DEV):
            sc_all = sc_all + jnp.where(row == t, cbuf[t], 0.0)
        sc_ref[...] = sc_all
        rown = lax.broadcasted_iota(jnp.int32, (8, n), 0)
        for l in range(nl):
            acc = jnp.zeros((8, n), F32)
            for k0 in range(0, d, kc):
                acc = acc + _dot(sc_all[:, k0:k0 + kc].astype(BF16), w_ref[l, k0:k0 + kc, :].astype(BF16))
            acc = acc + b_ref[l, q]
            for j in range(N_CHIPS):
                jx, jy = _chip_coords(j)
                bdest = 4 * jx + 2 * jy + mc
                rowv = jnp.sum(jnp.where(rown == bdest, acc, 0.0), axis=0, keepdims=True)
                stage[j, l] = jnp.broadcast_to(rowv, (8, n))
        sends2 = []
        for j in range(N_CHIPS):
            jx, jy = _chip_coords(j)
            cp = pltpu.make_async_remote_copy(src_ref=stage.at[j], dst_ref=mod_ref.at[:, q], send_sem=s2.at[j], recv_sem=r2.at[q],
                                              device_id=(jx, jy, mc), device_id_type=MESH)

            @pl.when(j != q)
            def _():
                cp.start()

            @pl.when(j == q)
            def _():
                for l in range(nl):
                    mod_ref[l, j] = stage[j, l]

            sends2.append((j, cp))
        for j in range(N_CHIPS):
            @pl.when(j != q)
            def _():
                pltpu.make_async_remote_copy(src_ref=stage.at[j], dst_ref=mod_ref.at[:, j], send_sem=s2.at[j], recv_sem=r2.at[j],
                                             device_id=(mx, my, mc), device_id_type=MESH).wait_recv()
        for j, cp in sends2:
            @pl.when(j != q)
            def _():
                cp.wait_send()

    vm = pl.BlockSpec(memory_space=pltpu.VMEM)
    return pl.pallas_call(
        body, name="mod_forward",
        in_specs=[vm, vm, vm],
        out_specs=[vm, vm],
        out_shape=[jax.ShapeDtypeStruct((nl, N_CHIPS, 8, n), F32), jax.ShapeDtypeStruct((8, d), F32)],
        scratch_shapes=[pltpu.VMEM((N_DEV, 8, d), F32), pltpu.VMEM((N_CHIPS, nl, 8, n), F32),
                        pltpu.SemaphoreType.DMA((N_DEV,)), pltpu.SemaphoreType.DMA((N_DEV,)),
                        pltpu.SemaphoreType.DMA((N_CHIPS,)), pltpu.SemaphoreType.DMA((N_CHIPS,))],
        compiler_params=pltpu.CompilerParams(vmem_limit_bytes=_vmem_limit(2 * nl * d * n * 4 + (8 << 20))),
    )(c, mod_w, mod_b4)


_HBM_SPEC = pl.BlockSpec(memory_space=pltpu.HBM)
_SEM_SPEC = pl.BlockSpec(memory_space=pltpu.SEMAPHORE)
_DATAFLOW = pltpu.SideEffectType.DATAFLOW_SIDE_EFFECTING


def _slot(ref, scatter, j):
    return ref.at[j] if scatter else ref


def _exchange_start(groups, scatter, after, name):
    flat = [a for g in groups for a in g]
    na = len(flat)
    ng = len(groups)
    sizes = [len(g) for g in groups]
    first = [sum(sizes[:g]) for g in range(ng)]
    where = [(g, k) for g in range(ng) for k in range(sizes[g])]
    mx, my, _ = _my_place()
    qo = 2 * mx + my
    lands = []
    for a in flat:
        own = lax.dynamic_index_in_dim(a, qo, 0, keepdims=False) if scatter else a
        lands.append(lax.dynamic_update_index_in_dim(lax.empty((N_CHIPS,) + own.shape, a.dtype), own, qo, 0))

    def body(*refs):
        ins, lnd = refs[:na], refs[na:2 * na]
        ssems, rsems = refs[2 * na + 1:2 * na + 1 + ng], refs[2 * na + 1 + ng:2 * na + 1 + 2 * ng]
        token = refs[-1]
        mx, my, mc = _my_place()
        q = 2 * mx + my
        for j in range(N_CHIPS):
            jx, jy = _chip_coords(j)
            for a in range(na):
                g, k = where[a]

                @pl.when(j != q)
                def _():
                    pltpu.make_async_remote_copy(src_ref=_slot(ins[a], scatter, j), dst_ref=lnd[a].at[q],
                                                 send_sem=ssems[g].at[k * N_CHIPS + j], recv_sem=rsems[g].at[k * N_CHIPS + q],
                                                 device_id=(jx, jy, mc), device_id_type=MESH).start()
        token[...] = jnp.zeros_like(token)

    sem_shapes = [pltpu.SemaphoreType.DMA((n * N_CHIPS,)) for n in sizes]
    outs = pl.pallas_call(
        body, name=name,
        in_specs=[_HBM_SPEC] * (2 * na) + [pl.BlockSpec(memory_space=pl.ANY)],
        out_specs=[_SEM_SPEC] * (2 * ng) + [_HBM_SPEC] * (2 * na) + [pl.BlockSpec(memory_space=pltpu.VMEM)],
        out_shape=sem_shapes + sem_shapes + [pltpu.HBM(a.shape, a.dtype) for a in flat + lands]
        + [jax.ShapeDtypeStruct((8, 128), F32)],
        input_output_aliases={i: 2 * ng + i for i in range(2 * na)},
        compiler_params=pltpu.CompilerParams(has_side_effects=_DATAFLOW),
    )(*[pltpu.with_memory_space_constraint(a, pltpu.HBM) for a in flat + lands], after)
    ssems, rsems = outs[:ng], outs[ng:2 * ng]
    src_thru, land_thru = outs[2 * ng:2 * ng + na], outs[2 * ng + na:2 * ng + 2 * na]
    states = [(src_thru[first[g]:first[g] + sizes[g]], land_thru[first[g]:first[g] + sizes[g]], ssems[g], rsems[g])
              for g in range(ng)]
    return states, outs[-1]


def _exchange_wait(state, scatter, after, name):
    srcs, lands, ssem, rsem = state
    na = len(srcs)

    def body(*refs):
        ins, lnd = refs[:na], refs[na:2 * na]
        ssem_ref, rsem_ref = refs[2 * na], refs[2 * na + 1]
        mx, my, mc = _my_place()
        q = 2 * mx + my
        for j in range(N_CHIPS):
            for a in range(na):
                @pl.when(j != q)
                def _():
                    cp = pltpu.make_async_remote_copy(src_ref=_slot(ins[a], scatter, j), dst_ref=lnd[a].at[j],
                                                      send_sem=ssem_ref.at[a * N_CHIPS + j], recv_sem=rsem_ref.at[a * N_CHIPS + j],
                                                      device_id=(mx, my, mc), device_id_type=MESH)
                    cp.wait_send()
                    cp.wait_recv()

    outs = pl.pallas_call(
        body, name=name,
        in_specs=[_HBM_SPEC] * (2 * na) + [_SEM_SPEC, _SEM_SPEC, pl.BlockSpec(memory_space=pl.ANY)],
        out_specs=[_HBM_SPEC] * (2 * na),
        out_shape=[pltpu.HBM(a.shape, a.dtype) for a in list(srcs) + list(lands)],
        input_output_aliases={i: i for i in range(2 * na)},
        compiler_params=pltpu.CompilerParams(has_side_effects=_DATAFLOW),
    )(*srcs, *lands, ssem, rsem, after)
    return outs[na:]


def _swap_with_sibling(arrs):
    na = len(arrs)

    def body(*refs):
        ins, outs = refs[:na], refs[na:2 * na]
        ssem, rsem = refs[2 * na:]
        mx, my, mc = _my_place()
        cps = [pltpu.make_async_remote_copy(src_ref=ins[a], dst_ref=outs[a], send_sem=ssem.at[a], recv_sem=rsem.at[a],
                                            device_id=(mx, my, 1 - mc), device_id_type=MESH) for a in range(na)]
        for cp in cps:
            cp.start()
        for cp in cps:
            cp.wait()

    hbm = pl.BlockSpec(memory_space=pl.ANY)
    return _hbm_call(
        body, name="swap_sibling",
        in_specs=[hbm] * na,
        out_specs=[hbm] * na,
        out_shape=[jax.ShapeDtypeStruct(a.shape, a.dtype) for a in arrs],
        scratch_shapes=[pltpu.SemaphoreType.DMA((na,)), pltpu.SemaphoreType.DMA((na,))],
    )(*arrs)


def _allreduce_small(rows_all, rows_sum):
    ra, c = rows_all.shape
    r = rows_sum.shape[0]
    ch = r // N_DEV
    assert ch % 8 == 0 and ch * N_DEV == r

    def body(a_ref, s_ref, all_ref, sum_ref, rbuf, red, sa, rva, sb, rvb, sc, rvc):
        mx, my, mc = _my_place()
        me = 4 * mx + 2 * my + mc
        mine = pl.ds(pl.multiple_of(me * ch, 8), ch)
        all_ref[me] = a_ref[...]
        rbuf[me] = s_ref[mine, :]

        def dev(t):
            tx, ty = _chip_coords(t // 2)
            return (tx, ty, t % 2)

        def everyone_else(fn):
            for t in range(N_DEV):
                @pl.when(t != me)
                def _():
                    fn(t)

        def copy_a(t, slot):
            return pltpu.make_async_remote_copy(src_ref=a_ref, dst_ref=all_ref.at[slot], send_sem=sa.at[t], recv_sem=rva.at[slot],
                                                device_id=dev(t), device_id_type=MESH)

        def copy_b(t, slot):
            return pltpu.make_async_remote_copy(src_ref=s_ref.at[pl.ds(t * ch, ch), :], dst_ref=rbuf.at[slot], send_sem=sb.at[t],
                                                recv_sem=rvb.at[slot], device_id=dev(t), device_id_type=MESH)

        def copy_c(t, chunk_start, slot):
            return pltpu.make_async_remote_copy(src_ref=red, dst_ref=sum_ref.at[pl.ds(chunk_start, ch), :], send_sem=sc.at[t],
                                                recv_sem=rvc.at[slot], device_id=dev(t), device_id_type=MESH)

        everyone_else(lambda t: (copy_a(t, me).start(), copy_b(t, me).start()))
        everyone_else(lambda t: (copy_a(t, t).wait_recv(), copy_b(t, t).wait_recv()))
        everyone_else(lambda t: (copy_a(t, me).wait_send(), copy_b(t, me).wait_send()))
        acc = rbuf[0]
        for t in range(1, N_DEV):
            acc = acc + rbuf[t]
        red[...] = acc
        sum_ref[mine, :] = acc
        everyone_else(lambda t: copy_c(t, pl.multiple_of(me * ch, 8), me).start())
        everyone_else(lambda t: copy_c(t, t * ch, t).wait_recv())
        everyone_else(lambda t: copy_c(t, pl.multiple_of(me * ch, 8), me).wait_send())

    vm = pl.BlockSpec(memory_space=pltpu.VMEM)
    return pl.pallas_call(
        body, name="allreduce_small",
        in_specs=[vm, vm],
        out_specs=[vm, vm],
        out_shape=[jax.ShapeDtypeStruct((N_DEV, ra, c), F32), jax.ShapeDtypeStruct((r, c), F32)],
        scratch_shapes=[pltpu.VMEM((N_DEV, ch, c), F32), pltpu.VMEM((ch, c), F32)] + [pltpu.SemaphoreType.DMA((N_DEV,))] * 6,
        compiler_params=pltpu.CompilerParams(vmem_limit_bytes=_vmem_limit((3 * r + 2 * N_DEV * ra) * c * 4 + (4 << 20))),
    )(rows_all, rows_sum)


def _pack(arrs, row_multiple=8):
    rows, layout, at = [], [], 0
    for a in arrs:
        n = a.size
        nr = -(-n // (8 * SMALL_COLS)) * 8
        flat = a.reshape(-1)
        if nr * SMALL_COLS != n:
            flat = jnp.pad(flat, (0, nr * SMALL_COLS - n))
        rows.append(flat.reshape(nr, SMALL_COLS))
        layout.append((at, nr, a.shape))
        at += nr
    pad = -at % row_multiple
    if pad:
        rows.append(jnp.zeros((pad, SMALL_COLS), F32))
    return jnp.concatenate(rows, axis=0), layout


def _unpack(buf, layout):
    out = []
    for at, nr, shape in layout:
        n = math.prod(shape)
        out.append(buf[at:at + nr].reshape(-1)[:n].reshape(shape))
    return out


SMALL_NAMES = ("mod_b", "mix_pre_g", "mix_post_g", "sgu_norm_g", "sgu_norm_b", "sgu_w", "sgu_b", "conv_b", "conv_norm_g",
               "conv_norm_b", "pool_w", "pool_scale", "branch_g", "ffn_pre_g", "ffn_post_g", "ffn_conv_b")
SHARDED_SMALL = ("conv_w", "ffn_conv_w")
WEIGHT_ORDER = ("mod_w", "mod_b", "mix_pre_g", "mix_post_g", "w_in", "sgu_norm_g", "sgu_norm_b", "sgu_w", "sgu_b", "conv_w",
                "conv_b", "conv_norm_g", "conv_norm_b", "pool_w", "pool_scale", "branch_g", "w_out", "ffn_pre_g", "ffn_post_g",
                "ffn_up", "ffn_conv_w", "ffn_conv_b", "ffn_down")


def _block_diag(blocks):
    n, a, b = blocks.shape
    eye = jnp.eye(n, dtype=blocks.dtype)
    return (eye[:, None, :, None] * blocks[:, :, None, :]).reshape(n * a, n * b)


def _diag_blocks(mat, n):
    a = mat.shape[0] // n
    return jnp.stack([mat[g * a:(g + 1) * a, g * a:(g + 1) * a] for g in range(n)])


def _step(x, c, loss_target, w, m, v):
    nl = w["mod_w"].shape[0]
    s, d = x.shape[1], x.shape[2]
    heads = SGU_WIDTH // HEAD_DIM
    groups = len(POOL_WINDOWS)
    mx, my, _ = _my_place()
    q = 2 * mx + my
    x0 = x.reshape(s, d)
    tgt = loss_target.reshape(s, d)

    nmod = w["mod_w"].shape[2]
    kin = w["w_in"].shape[2]
    inw = kin * N_CHIPS
    f2 = w["ffn_up"].shape[2] * N_CHIPS
    f = f2 // 2

    def wgroups(l):
        return [[w["w_in"][l].astype(BF16), w["conv_w"][l], w["ffn_conv_w"][l]], [w["w_out"][l].astype(BF16)],
                [w["ffn_up"][l].astype(BF16)], [w["ffn_down"][l].astype(BF16)]]

    gstates = {}
    (gstates[0, 0],), gtoken = _exchange_start(wgroups(0)[:1], False, c, "gather_start_in_0")
    mod4, sc_all = _mod_forward(c + gtoken[0:1, 0:1], w["mod_w"], w["mod_b"].reshape(nl, N_CHIPS, 1, nmod))
    mod = mod4[:, :, 0, :].reshape(nl, N_MOD, 1, d)

    tril = jnp.tril(jnp.ones((CHUNK, CHUNK), bool))
    bd = _block_diag(jnp.ones((heads, HEAD_DIM, HEAD_DIM), BF16))

    def mixer_params(l, conv_w):
        wm = jnp.where(tril[None], w["sgu_w"][l], 0.0)
        pw = _block_diag(w["pool_w"][l])
        return dict(
            bd=bd, ng=w["sgu_norm_g"][l][None], nb=w["sgu_norm_b"][l][None],
            wm=wm.astype(BF16), wmt=jnp.swapaxes(wm, 1, 2).astype(BF16),
            bias=jnp.repeat(w["sgu_b"][l].T, HEAD_DIM, axis=1),
            cw=conv_w, cb=w["conv_b"][l][None], cng=w["conv_norm_g"][l][None], cnb=w["conv_norm_b"][l][None],
            pw=pw.astype(BF16), pwt=pw.T.astype(BF16), ps=w["pool_scale"][l][None], bg=w["branch_g"][l][None])

    saved = []
    xl = x0
    arrived = {0: _exchange_wait(gstates[0, 0], False, mod4, "gather_wait_in_0")}
    for l in range(nl):
        sh1, sc1, g1, sh2, sc2, g2 = [mod[l, k] for k in range(N_MOD)]
        gpre1, gpost1 = w["mix_pre_g"][l][None], w["mix_post_g"][l][None]
        gpre2, gpost2 = w["ffn_pre_g"][l][None], w["ffn_post_g"][l][None]
        fcb = w["ffn_conv_b"][l][None]
        sh1_after, bg_after, g1_after, sh2_after, fcb_after = sh1, w["branch_g"][l][None], g1, sh2, fcb
        g_win, g_cw, g_fcw = arrived[l][:3]
        if l == 0:
            (gstates[0, 1], gstates[0, 2]), tok = _exchange_start(wgroups(0)[1:3], False, g_win, "gather_start_out_0")
            sh1_after = sh1 + tok[0:1, 0:1]
        w_in = jnp.transpose(g_win, (1, 0, 2)).reshape(d, inw)
        conv_w = jnp.transpose(g_cw, (1, 0, 2)).reshape(CONV_K, CONV_WIDTH)
        ffn_cw = jnp.transpose(g_fcw, (1, 0, 2)).reshape(FFN_CONV_K, f2)
        mp = mixer_params(l, conv_w)
        z, h1 = _norm_mod_matmul(xl, gpre1, 1.0 + sc1, sh1_after, w_in[None], f"mix_in_{l}")
        if l == 0:
            (g_wout,) = _exchange_wait(gstates[0, 1], False, z, "gather_wait_out_0")
        else:
            g_wout = arrived[l][3]
        w_out = g_wout.reshape(d, d)
        ycat = _mixer_fwd(z, dict(mp, bg=bg_after), f"mixer_fwd_{l}")
        (up,) = _exchange_wait(gstates[l, 2], False, ycat, f"gather_wait_up_{l}")
        if l == 0:
            (gstates[0, 3],), tok = _exchange_start(wgroups(0)[3:4], False, up, "gather_start_down_0")
            g1_after = g1 + tok[0:1, 0:1]
        o, x1 = _matmul_norm_resid(ycat, w_out, xl, g1_after, gpost1, f"mix_out_{l}")
        (g_down,) = _exchange_wait(gstates[l, 3], False, x1, f"gather_wait_down_{l}")
        down = g_down.reshape(f, d)
        if l + 1 < nl:
            (gstates[l + 1, 0], gstates[l + 1, 1]), tok = _exchange_start(wgroups(l + 1)[0:2], False, g_down,
                                                                        f"gather_start_in_{l + 1}")
            sh2_after = sh2 + tok[0:1, 0:1]
        p, h2 = _norm_mod_matmul(x1, gpre2, 1.0 + sc2, sh2_after, up, f"ffn_in_{l}")
        if l + 1 < nl:
            nxt = _exchange_wait(gstates[l + 1, 0], False, p, f"gather_wait_in_{l + 1}")
            nxt_out = _exchange_wait(gstates[l + 1, 1], False, nxt[0], f"gather_wait_out_{l + 1}")
            arrived[l + 1] = list(nxt) + list(nxt_out)
            (gstates[l + 1, 2], gstates[l + 1, 3]), tok = _exchange_start(wgroups(l + 1)[2:4], False, nxt_out[0],
                                                                        f"gather_start_up_{l + 1}")
            fcb_after = fcb + tok[0:1, 0:1]
        act = _ffn_act_fwd(p, ffn_cw, fcb_after, f"ffn_act_{l}")
        qo, x2 = _matmul_norm_resid(act, down, x1, g2, gpost2, f"ffn_out_{l}")
        saved.append(dict(x=xl, z=z, h1=h1, ycat=ycat, o=o, x1=x1, p=p, h2=h2, act=act, qo=qo, mp=mp, fcb=fcb,
                          w_in=w_in, w_out=w_out, up=up, down=down, ffn_cw=ffn_cw,
                          mods=(sh1, sc1, g1, sh2, sc2, g2), gains=(gpre1, gpost1, gpre2, gpost2)))
        xl = x2

    dx, loss_row = _loss_head(xl, tgt)

    small = {n: [None] * nl for n in SMALL_NAMES + SHARDED_SMALL}
    dmods = [None] * nl
    tn = f2 // N_CHIPS
    sstates = {}
    token = None
    for l in reversed(range(nl)):
        sv = saved[l]
        sh1, sc1, g1, sh2, sc2, g2 = sv["mods"]
        gpre1, gpost1, gpre2, gpost2 = sv["gains"]
        if token is not None:
            g2 = g2 + token[0:1, 0:1]
        dq, dact, dg2, dgpost2 = _resid_bwd_matmul(dx, sv["qo"], g2, gpost2, sv["down"], f"ffn_out_bwd_{l}")
        g_down = _wgrad(sv["act"], dq, (f, lambda j: 0), (d, lambda j: 0), jax.ShapeDtypeStruct((f, d), BF16),
                        (1, lambda j: (0, 0)), (f, d), f"wgrad_ffn_down_{l}")
        dug, duv, dfwg, dfwv, dfbg, dfbv = _ffn_act_bwd(sv["p"], dact, sv["ffn_cw"], sv["fcb"], f"ffn_act_bwd_{l}")
        dx1, dp, dsh2, dsc2, dgpre2 = _ffn_in_bwd(dug, duv, sv["ffn_cw"], sv["up"], sv["x1"], dx, gpre2, 1.0 + sc2,
                                                  f"ffn_in_bwd_{l}")
        g_up = _wgrad(sv["h2"], dp, (d, lambda j: 0), (tn, lambda j: j), jax.ShapeDtypeStruct((N_CHIPS, d, tn), BF16),
                      (N_CHIPS, lambda j: (j, 0, 0)), (None, d, tn), f"wgrad_ffn_up_{l}")
        (sstates[l, 0],), token = _exchange_start([[g_down.reshape(N_CHIPS, f // N_CHIPS, d), g_up]], True, g_up,
                                                  f"scatter_start_ffn_{l}")
        do, dycat, dg1, dgpost1 = _resid_bwd_matmul(dx1, sv["o"], g1 + token[0:1, 0:1], gpost1, sv["w_out"],
                                                    f"mix_out_bwd_{l}")
        g_out = _wgrad(sv["ycat"], do, (d, lambda j: 0), (d, lambda j: 0), jax.ShapeDtypeStruct((d, d), BF16),
                       (1, lambda j: (0, 0)), (d, d), f"wgrad_w_out_{l}")
        (sstates[l, 1],), token = _exchange_start([[g_out.reshape(N_CHIPS, d // N_CHIPS, d)]], True, g_out,
                                                  f"scatter_start_out_{l}")
        mp_after = dict(sv["mp"], bg=sv["mp"]["bg"] + token[0:1, 0:1])
        (dza, dcb, dd, dbg, dwm, dbias, dng, dnb, dcng, dcnb, dps, dpw) = _mixer_bwd_a(sv["z"], dycat, mp_after, f"mixer_bwd_a_{l}")
        dx, dz, dsh1, dsc1, dgpre1, dcw, dcbias = _mixer_bwd_b(
            sv["z"], dza, dcb, dd, sv["x"], dx1, gpre1, 1.0 + sc1, sv["mp"]["cw"], sv["w_in"], f"mixer_bwd_b_{l}")
        g_in = _wgrad(sv["h1"], dz, (d, lambda j: 0), (inw, lambda j: 0), jax.ShapeDtypeStruct((d, inw), BF16),
                      (1, lambda j: (0, 0)), (d, inw), f"wgrad_w_in_{l}")
        g_in_parts = jnp.transpose(g_in.reshape(d, N_CHIPS, kin), (1, 0, 2))
        if l > 0:
            (sstates[l, 2],), token = _exchange_start([[g_in_parts]], True, g_in_parts, f"scatter_start_in_{l}")

        dmods[l] = jnp.concatenate([dsh1, dsc1, dg1, dsh2, dsc2, dg2], axis=0)
        small["mix_pre_g"][l], small["mix_post_g"][l] = dgpre1[0], dgpost1[0]
        small["ffn_pre_g"][l], small["ffn_post_g"][l] = dgpre2[0], dgpost2[0]
        small["sgu_norm_g"][l], small["sgu_norm_b"][l] = dng[0], dnb[0]
        small["sgu_w"][l] = jnp.where(tril[None], dwm, 0.0)
        small["sgu_b"][l] = dbias.reshape(CHUNK, heads, HEAD_DIM).sum(-1).T
        small["conv_b"][l], small["conv_norm_g"][l], small["conv_norm_b"][l] = dcbias[0], dcng[0], dcnb[0]
        small["pool_w"][l], small["pool_scale"][l], small["branch_g"][l] = _diag_blocks(dpw, groups), dps[0], dbg[0]
        small["ffn_conv_b"][l] = jnp.concatenate([dfbg[0], dfbv[0]])
        small["conv_w"][l] = dcw
        small["ffn_conv_w"][l] = jnp.concatenate([dfwg, dfwv], axis=1)

    names = [n for n in SMALL_NAMES if n != "mod_b"] + list(SHARDED_SMALL)
    dmod_rows, _ = _pack([jnp.stack(dmods)])
    packed, layout = _pack([jnp.stack(dmods), loss_row] + [jnp.stack(small[n]) for n in names], 8 * N_DEV)
    gathered, summed = _allreduce_small(dmod_rows, packed)
    (sstates[0, 2],), token = _exchange_start([[g_in_parts]], True, summed, "scatter_start_in_0")
    parts = _unpack(summed, layout)
    loss = parts[1][0, 0]
    gsmall = dict(zip(names, parts[2:]))
    gsmall["mod_b"] = parts[0].reshape(nl, N_MOD * d)
    dmod_all = gathered[:, :nl * N_MOD].reshape(N_DEV, nl, N_MOD * d)
    dmod_mine = jnp.transpose(lax.dynamic_slice_in_dim(dmod_all, q * nmod, nmod, axis=2), (1, 0, 2))

    grads, deltas, new_m, new_v = {}, {}, {}, {}

    def put(name, res):
        grads[name], deltas[name], new_m[name], new_v[name] = res

    put("mod_w", _modw_adamw(sc_all.T + token[0:1, 0:1], dmod_mine, w["mod_w"], m["mod_w"], v["mod_w"], "adamw_mod_w"))

    pw_, lay = _pack([w[n] for n in SMALL_NAMES])
    pm_, _ = _pack([m[n] for n in SMALL_NAMES])
    pv_, _ = _pack([v[n] for n in SMALL_NAMES])
    pg_, _ = _pack([gsmall[n] for n in SMALL_NAMES])
    res = _adamw(pw_, pm_, pv_, [pg_], "adamw_small")
    for n, g_, d_, m_, v_ in zip(SMALL_NAMES, *[_unpack(r_, lay) for r_ in res]):
        put(n, (g_, d_, m_, v_))

    gsh = {"conv_w": lax.dynamic_slice_in_dim(gsmall["conv_w"], q * (CONV_WIDTH // N_CHIPS), CONV_WIDTH // N_CHIPS, axis=2),
           "ffn_conv_w": lax.dynamic_slice_in_dim(gsmall["ffn_conv_w"], q * (f2 // N_CHIPS), f2 // N_CHIPS, axis=2)}
    pw_, lay = _pack([w[n] for n in SHARDED_SMALL])
    pm_, _ = _pack([m[n] for n in SHARDED_SMALL])
    pv_, _ = _pack([v[n] for n in SHARDED_SMALL])
    pg_, _ = _pack([gsh[n] for n in SHARDED_SMALL])
    res = _adamw(pw_, pm_, pv_, [pg_], "adamw_sharded_small")
    for n, g_, d_, m_, v_ in zip(SHARDED_SMALL, *[_unpack(r_, lay) for r_ in res]):
        put(n, (g_, d_, m_, v_))

    recv = dict(w_in=[None] * nl, w_out=[None] * nl, ffn_up=[None] * nl, ffn_down=[None] * nl)
    done = grads["mod_w"]
    for l in reversed(range(nl)):
        recv["ffn_down"][l], recv["ffn_up"][l] = _exchange_wait(sstates[l, 0], True, done, f"scatter_wait_ffn_{l}")
        (recv["w_out"][l],) = _exchange_wait(sstates[l, 1], True, recv["ffn_up"][l], f"scatter_wait_out_{l}")
        (recv["w_in"][l],) = _exchange_wait(sstates[l, 2], True, recv["w_out"][l], f"scatter_wait_in_{l}")
        done = recv["w_in"][l]
    big = ("w_in", "w_out", "ffn_up", "ffn_down")
    mine = [_reduce4(recv[n], f"reduce4_{n}") for n in big]
    theirs = _swap_with_sibling(mine)
    for n, a, b in zip(big, mine, theirs):
        put(n, _adamw(w[n], m[n], v[n], [a, b], f"adamw_{n}"))

    return (loss, dx.reshape(1, s, d), *[grads[n] for n in WEIGHT_ORDER], *[deltas[n] for n in WEIGHT_ORDER],
            *[new_m[n] for n in WEIGHT_ORDER], *[new_v[n] for n in WEIGHT_ORDER])


def kernel(x, c, mod_w, mod_b, mix_pre_g, mix_post_g, w_in, sgu_norm_g, sgu_norm_b, sgu_w, sgu_b, conv_w, conv_b, conv_norm_g, conv_norm_b, pool_w, pool_scale, branch_g, w_out, ffn_pre_g, ffn_post_g, ffn_up, ffn_conv_w, ffn_conv_b, ffn_down, loss_target, m_mod_w, m_mod_b, m_mix_pre_g, m_mix_post_g, m_w_in, m_sgu_norm_g, m_sgu_norm_b, m_sgu_w, m_sgu_b, m_conv_w, m_conv_b, m_conv_norm_g, m_conv_norm_b, m_pool_w, m_pool_scale, m_branch_g, m_w_out, m_ffn_pre_g, m_ffn_post_g, m_ffn_up, m_ffn_conv_w, m_ffn_conv_b, m_ffn_down, v_mod_w, v_mod_b, v_mix_pre_g, v_mix_post_g, v_w_in, v_sgu_norm_g, v_sgu_norm_b, v_sgu_w, v_sgu_b, v_conv_w, v_conv_b, v_conv_norm_g, v_conv_norm_b, v_pool_w, v_pool_scale, v_branch_g, v_w_out, v_ffn_pre_g, v_ffn_post_g, v_ffn_up, v_ffn_conv_w, v_ffn_conv_b, v_ffn_down):
    w = dict(mod_w=mod_w, mod_b=mod_b, mix_pre_g=mix_pre_g, mix_post_g=mix_post_g, w_in=w_in, sgu_norm_g=sgu_norm_g,
             sgu_norm_b=sgu_norm_b, sgu_w=sgu_w, sgu_b=sgu_b, conv_w=conv_w, conv_b=conv_b, conv_norm_g=conv_norm_g,
             conv_norm_b=conv_norm_b, pool_w=pool_w, pool_scale=pool_scale, branch_g=branch_g, w_out=w_out,
             ffn_pre_g=ffn_pre_g, ffn_post_g=ffn_post_g, ffn_up=ffn_up, ffn_conv_w=ffn_conv_w, ffn_conv_b=ffn_conv_b,
             ffn_down=ffn_down)
    m = dict(mod_w=m_mod_w, mod_b=m_mod_b, mix_pre_g=m_mix_pre_g, mix_post_g=m_mix_post_g, w_in=m_w_in,
             sgu_norm_g=m_sgu_norm_g, sgu_norm_b=m_sgu_norm_b, sgu_w=m_sgu_w, sgu_b=m_sgu_b, conv_w=m_conv_w,
             conv_b=m_conv_b, conv_norm_g=m_conv_norm_g, conv_norm_b=m_conv_norm_b, pool_w=m_pool_w,
             pool_scale=m_pool_scale, branch_g=m_branch_g, w_out=m_w_out, ffn_pre_g=m_ffn_pre_g, ffn_post_g=m_ffn_post_g,
             ffn_up=m_ffn_up, ffn_conv_w=m_ffn_conv_w, ffn_conv_b=m_ffn_conv_b, ffn_down=m_ffn_down)
    v = dict(mod_w=v_mod_w, mod_b=v_mod_b, mix_pre_g=v_mix_pre_g, mix_post_g=v_mix_post_g, w_in=v_w_in,
             sgu_norm_g=v_sgu_norm_g, sgu_norm_b=v_sgu_norm_b, sgu_w=v_sgu_w, sgu_b=v_sgu_b, conv_w=v_conv_w,
             conv_b=v_conv_b, conv_norm_g=v_conv_norm_g, conv_norm_b=v_conv_norm_b, pool_w=v_pool_w,
             pool_scale=v_pool_scale, branch_g=v_branch_g, w_out=v_w_out, ffn_pre_g=v_ffn_pre_g, ffn_post_g=v_ffn_post_g,
             ffn_up=v_ffn_up, ffn_conv_w=v_ffn_conv_w, ffn_conv_b=v_ffn_conv_b, ffn_down=v_ffn_down)
    return _step(x, c, loss_target, w, m, v)
```

```python
import functools
import math

import jax
import jax.numpy as jnp
from jax import lax
from jax.experimental import pallas as pl
from jax.experimental.pallas import tpu as pltpu

F32 = jnp.float32
BF16 = jnp.bfloat16
MESH = pl.DeviceIdType.MESH

EPS = 1e-6
HEAD_DIM = 64
CHUNK = 128
SGU_WIDTH = 384
CONV_WIDTH = 384
POOL_WIDTH = 256
POOL_WINDOWS = (2, 4, 8, 16)
CONV_K = 31
FFN_CONV_K = 3
N_MOD = 6
N_CHIPS = 4
N_DEV = 8

ADAM_LR = 0.001
ADAM_B1 = 0.9
ADAM_B2 = 0.999
ADAM_EPS = 1e-08
ADAM_WD = 0.01
ADAM_STEP = 10

MIX_HALO = 32
FFN_HALO = 8
FFN_ROWS = 16
FFN_UNROLL = 8
LANES = 128
CONV_ROWS = 32
SMALL_COLS = 1024
VMEM_BYTES_V7X = 64 * 1024 * 1024


def _vmem_limit(estimate_bytes):
    return int(min(max(estimate_bytes, 16 * 1024 * 1024), VMEM_BYTES_V7X - 8 * 1024 * 1024))


def _row_tile(s, want):
    return want if s % want == 0 else math.gcd(s, want)


def _rsum(v):
    return jnp.sum(v, axis=0, keepdims=True)


def _rmean(v):
    return jnp.mean(v, axis=-1, keepdims=True)


def _gelu(v):
    k = math.sqrt(2.0 / math.pi)
    return 0.5 * v * (1.0 + jnp.tanh(k * (v + 0.044715 * v * v * v)))


def _gelu_grad(v):
    k = math.sqrt(2.0 / math.pi)
    t = jnp.tanh(k * (v + 0.044715 * v * v * v))
    return 0.5 * (1.0 + t) + 0.5 * v * (1.0 - t * t) * (k * (1.0 + 3.0 * 0.044715 * v * v))


def _dot(a, b):
    return jnp.dot(a, b, preferred_element_type=F32)


def _dot_nt(a, b):
    return lax.dot_general(a, b, (((1,), (1,)), ((), ())), preferred_element_type=F32)


def _dot_tn(a, b):
    return lax.dot_general(a, b, (((0,), (0,)), ((), ())), preferred_element_type=F32)


def _group_mean(v, bd):
    hi = v.astype(BF16)
    lo = (v - hi.astype(F32)).astype(BF16)
    return (_dot(hi, bd) + _dot(lo, bd)) * (1.0 / HEAD_DIM)


def _const_spec(shape):
    nd = len(shape)
    return pl.BlockSpec(shape, lambda *_: (0,) * nd)


def _hbm_call(body, **kw):
    call = pl.pallas_call(body, **kw)
    return lambda *args: call(*[pltpu.with_memory_space_constraint(a, pltpu.HBM) for a in args])


def _norm_mod_matmul(x, gain, s1p, shift, w, name):
    s, d = x.shape
    nb, _, tn = w.shape
    ts = _row_tile(s, 512 if nb * tn <= 2048 else 256)

    def body(x_ref, g_ref, s_ref, b_ref, w_ref, z_ref, h_ref):
        xv = x_ref[...]
        r = lax.rsqrt(_rmean(xv * xv) + EPS)
        h = ((xv * r) * g_ref[...] * s_ref[...] + b_ref[...]).astype(BF16)
        h_ref[...] = h
        for j in range(nb):
            z_ref[:, j * tn:(j + 1) * tn] = _dot(h, w_ref[j])

    vec = pl.BlockSpec((1, d), lambda i: (0, 0))
    return _hbm_call(
        body, name=name,
        grid=(s // ts,),
        in_specs=[pl.BlockSpec((ts, d), lambda i: (i, 0)), vec, vec, vec,
                  pl.BlockSpec((nb, d, tn), lambda i: (0, 0, 0), pipeline_mode=pl.Buffered(1))],
        out_specs=[pl.BlockSpec((ts, nb * tn), lambda i: (i, 0)), pl.BlockSpec((ts, d), lambda i: (i, 0))],
        out_shape=[jax.ShapeDtypeStruct((s, nb * tn), F32), jax.ShapeDtypeStruct((s, d), BF16)],
        compiler_params=pltpu.CompilerParams(
            dimension_semantics=("arbitrary",),
            vmem_limit_bytes=_vmem_limit(2 * (ts * d * 4 + ts * nb * tn * 4 + ts * d * 2) + nb * d * tn * 2 + 4 * ts * d * 4)),
    )(x, gain, s1p, shift, w)


def _matmul_norm_resid(a, w, xres, gate, gpost, name):
    s, k = a.shape
    d = w.shape[1]
    ts = _row_tile(s, 512)

    def body(a_ref, w_ref, x_ref, gate_ref, gp_ref, o_ref, xn_ref):
        o = _dot(a_ref[...], w_ref[...])
        o_ref[...] = o
        r = lax.rsqrt(_rmean(o * o) + EPS)
        xn_ref[...] = x_ref[...] + gate_ref[...] * ((o * r) * gp_ref[...])

    vec = pl.BlockSpec((1, d), lambda i: (0, 0))
    row = pl.BlockSpec((ts, d), lambda i: (i, 0))
    return _hbm_call(
        body, name=name,
        grid=(s // ts,),
        in_specs=[pl.BlockSpec((ts, k), lambda i: (i, 0)),
                  pl.BlockSpec((k, d), lambda i: (0, 0), pipeline_mode=pl.Buffered(1)), row, vec, vec],
        out_specs=[row, row],
        out_shape=[jax.ShapeDtypeStruct((s, d), F32)] * 2,
        compiler_params=pltpu.CompilerParams(
            dimension_semantics=("arbitrary",),
            vmem_limit_bytes=_vmem_limit(2 * (ts * k * 2 + 3 * ts * d * 4) + k * d * 2 + 4 * ts * d * 4)),
    )(a, w, xres, gate, gpost)


def _wgrad(a, b, acols, bcols, out_struct, out_index, out_block, name):
    s = a.shape[0]
    ts = _row_tile(s, 1024)
    aw, afn = acols
    bw, bfn = bcols
    nj = out_index[0]
    oidx = out_index[1]

    def body(a_ref, b_ref, o_ref, acc):
        i = pl.program_id(1)

        @pl.when(i == 0)
        def _():
            acc[...] = jnp.zeros_like(acc)

        acc[...] += _dot_tn(a_ref[...], b_ref[...])

        @pl.when(i == pl.num_programs(1) - 1)
        def _():
            o_ref[...] = acc[...].astype(o_ref.dtype)

    return _hbm_call(
        body, name=name,
        grid=(nj, s // ts),
        in_specs=[pl.BlockSpec((ts, aw), lambda j, i: (i, afn(j))), pl.BlockSpec((ts, bw), lambda j, i: (i, bfn(j)))],
        out_specs=pl.BlockSpec(out_block, lambda j, i: oidx(j)),
        out_shape=out_struct,
        scratch_shapes=[pltpu.VMEM((aw, bw), F32)],
        compiler_params=pltpu.CompilerParams(
            dimension_semantics=("arbitrary", "arbitrary"),
            vmem_limit_bytes=_vmem_limit(2 * (ts * aw * 2 + ts * bw * 2) + 3 * aw * bw * 4 + ts * aw * 4)),
    )(a, b)


def _loss_head(xo, tgt):
    s, d = xo.shape
    ts = _row_tile(s, 512)

    def body(x_ref, t_ref, dx_ref, l_ref, acc):
        i = pl.program_id(0)

        @pl.when(i == 0)
        def _():
            acc[...] = jnp.zeros_like(acc)

        e = x_ref[...] - t_ref[...]
        dx_ref[...] = e * (1.0 / d)
        acc[...] += _rsum(e * e)

        @pl.when(i == pl.num_programs(0) - 1)
        def _():
            tot = jnp.sum(acc[...], axis=-1, keepdims=True) * (0.5 / d)
            l_ref[...] = jnp.broadcast_to(tot, l_ref.shape)

    row = pl.BlockSpec((ts, d), lambda i: (i, 0))
    return _hbm_call(
        body, name="loss_head",
        grid=(s // ts,),
        in_specs=[row, row],
        out_specs=[row, pl.BlockSpec((1, SMALL_COLS), lambda i: (0, 0))],
        out_shape=[jax.ShapeDtypeStruct((s, d), F32), jax.ShapeDtypeStruct((1, SMALL_COLS), F32)],
        scratch_shapes=[pltpu.VMEM((1, d), F32)],
        compiler_params=pltpu.CompilerParams(dimension_semantics=("arbitrary",)),
    )(xo, tgt)


def _resid_bwd_matmul(dxn, o, gate, gpost, w, name):
    s, d = dxn.shape
    k = w.shape[0]
    ts = _row_tile(s, 512)

    def body(dx_ref, o_ref, gate_ref, gp_ref, w_ref, do_ref, da_ref, dgate_ref, dgp_ref):
        i = pl.program_id(0)

        @pl.when(i == 0)
        def _():
            dgate_ref[...] = jnp.zeros_like(dgate_ref)
            dgp_ref[...] = jnp.zeros_like(dgp_ref)

        dx = dx_ref[...]
        o = o_ref[...]
        r = lax.rsqrt(_rmean(o * o) + EPS)
        on = o * r
        dgate_ref[...] += _rsum(dx * (on * gp_ref[...]))
        don = dx * gate_ref[...]
        dgp_ref[...] += _rsum(don * on)
        t = don * gp_ref[...]
        do = (r * (t - on * _rmean(t * on))).astype(BF16)
        do_ref[...] = do
        da_ref[...] = _dot_nt(do, w_ref[...])

    vec = pl.BlockSpec((1, d), lambda i: (0, 0))
    row = pl.BlockSpec((ts, d), lambda i: (i, 0))
    return _hbm_call(
        body, name=name,
        grid=(s // ts,),
        in_specs=[row, row, vec, vec, pl.BlockSpec((k, d), lambda i: (0, 0), pipeline_mode=pl.Buffered(1))],
        out_specs=[row, pl.BlockSpec((ts, k), lambda i: (i, 0)), vec, vec],
        out_shape=[jax.ShapeDtypeStruct((s, d), BF16), jax.ShapeDtypeStruct((s, k), F32),
                   jax.ShapeDtypeStruct((1, d), F32), jax.ShapeDtypeStruct((1, d), F32)],
        compiler_params=pltpu.CompilerParams(
            dimension_semantics=("arbitrary",),
            vmem_limit_bytes=_vmem_limit(2 * (2 * ts * d * 4 + ts * d * 2 + ts * k * 4) + d * k * 2 + 6 * ts * d * 4)),
    )(dxn, o, gate, gpost, w)


def _norm_mod_bwd(dh, xv, gain, s1p, dres):
    r = lax.rsqrt(_rmean(xv * xv) + EPS)
    xn = xv * r
    dshift = _rsum(dh)
    t = dh * xn
    dscale = _rsum(t * gain)
    dgain = _rsum(t * s1p)
    dxn = dh * (gain * s1p)
    dx = r * (dxn - xn * _rmean(dxn * xn)) + dres
    return dx, dshift, dscale, dgain


def _lane_lt(shape, bound):
    return lax.broadcasted_iota(jnp.int32, shape, 1) < bound


def _sgu_forward(z_ref, bd_ref, ng_ref, nb_ref, wm_ref, bias_ref, ts, ya_s, f_s):
    u = _gelu(z_ref[:, 0:SGU_WIDTH])
    v = _gelu(z_ref[:, SGU_WIDTH:2 * SGU_WIDTH])
    bd = bd_ref[...]
    vc = v - _group_mean(v, bd)
    rstd = lax.rsqrt(_group_mean(vc * vc, bd) + EPS)
    vhat = vc * rstd
    vn = (vhat * ng_ref[...] + nb_ref[...]).astype(BF16)
    left = _lane_lt((CHUNK, CHUNK), HEAD_DIM)
    for n in range(ts // CHUNK):
        rows = slice(n * CHUNK, (n + 1) * CHUNK)
        for p in range(SGU_WIDTH // CHUNK):
            cols = slice(p * CHUNK, (p + 1) * CHUNK)
            blk = vn[rows, cols]
            f = jnp.where(left, _dot(wm_ref[2 * p], blk), _dot(wm_ref[2 * p + 1], blk)) + bias_ref[:, cols]
            if f_s is not None:
                f_s[rows, cols] = f
            ya_s[rows, cols] = u[rows, cols] * f
    return u, vhat, rstd, vn


def _conv31_forward(z_ref, zh_ref, first, cw_ref, cb_ref, ts, ext_b, cbs):
    a = z_ref[:, 2 * SGU_WIDTH:2 * SGU_WIDTH + CONV_WIDTH]
    g = z_ref[:, 2 * SGU_WIDTH + CONV_WIDTH:2 * SGU_WIDTH + 2 * CONV_WIDTH]
    ah = zh_ref[:, 2 * SGU_WIDTH:2 * SGU_WIDTH + CONV_WIDTH]
    gh = zh_ref[:, 2 * SGU_WIDTH + CONV_WIDTH:2 * SGU_WIDTH + 2 * CONV_WIDTH]
    ext_b[pl.ds(0, MIX_HALO), :] = jnp.where(first, 0.0, ah * jax.nn.sigmoid(gh))
    ext_b[pl.ds(MIX_HALO, ts), :] = a * jax.nn.sigmoid(g)
    for r in range(ts // CONV_ROWS):
        acc = jnp.broadcast_to(cb_ref[...], (CONV_ROWS, CONV_WIDTH))
        for k in range(CONV_K):
            acc = acc + cw_ref[k:k + 1, :] * ext_b[pl.ds(MIX_HALO - (CONV_K - 1) + k + r * CONV_ROWS, CONV_ROWS), :]
        cbs[pl.ds(r * CONV_ROWS, CONV_ROWS), :] = acc


def _pool_counts(i, ts):
    pos1 = (i * ts + 1 + lax.broadcasted_iota(jnp.int32, (ts, POOL_WIDTH), 0)).astype(F32)
    lane = lax.broadcasted_iota(jnp.int32, (ts, POOL_WIDTH), 1)
    gdim = POOL_WIDTH // len(POOL_WINDOWS)
    win = jnp.where(lane < gdim, float(POOL_WINDOWS[0]),
                    jnp.where(lane < 2 * gdim, float(POOL_WINDOWS[1]),
                              jnp.where(lane < 3 * gdim, float(POOL_WINDOWS[2]), float(POOL_WINDOWS[3]))))
    return jnp.minimum(pos1, win)


def _window_sums(ext, base, ts, sign):
    lane = lax.broadcasted_iota(jnp.int32, (ts, POOL_WIDTH), 1)
    gdim = POOL_WIDTH // len(POOL_WINDOWS)
    run = jnp.zeros((ts, POOL_WIDTH), F32)
    out = jnp.zeros((ts, POOL_WIDTH), F32)
    for m in range(POOL_WINDOWS[-1]):
        run = run + ext[pl.ds(base + sign * m, ts), :]
        for gi, win in enumerate(POOL_WINDOWS):
            if m == win - 1:
                out = jnp.where((lane >= gi * gdim) & (lane < (gi + 1) * gdim), run, out)
    return out


def _pool_forward(z_ref, zh_ref, first, i, ts, ext_c):
    c0 = 2 * SGU_WIDTH + 2 * CONV_WIDTH
    zc = z_ref[:, c0:c0 + POOL_WIDTH]
    ext_c[pl.ds(0, MIX_HALO), :] = jnp.where(first, 0.0, zh_ref[:, c0:c0 + POOL_WIDTH])
    ext_c[pl.ds(MIX_HALO, ts), :] = zc
    sums = _window_sums(ext_c, MIX_HALO, ts, -1)
    return sums / _pool_counts(i, ts) - zc


def _layer_norm_rows(v):
    mu = _rmean(v)
    vc = v - mu
    rstd = lax.rsqrt(_rmean(vc * vc) + EPS)
    return vc * rstd, rstd


def _mixer_specs(s, ts, width):
    nbh = ts // MIX_HALO
    tile = pl.BlockSpec((ts, width), lambda i: (i, 0))
    prev = pl.BlockSpec((MIX_HALO, width), lambda i: (jnp.maximum(i * nbh - 1, 0), 0))
    nxt = pl.BlockSpec((MIX_HALO, width), lambda i: (jnp.minimum((i + 1) * nbh, s // MIX_HALO - 1), 0))
    return tile, prev, nxt


def _mixer_fwd(z, mp, name):
    s, inw = z.shape
    d = SGU_WIDTH + CONV_WIDTH + POOL_WIDTH
    ts = _row_tile(s, 256)

    def body(z_ref, zh_ref, bd_ref, ng_ref, nb_ref, wm_ref, bias_ref, cw_ref, cb_ref, cng_ref, cnb_ref,
             pw_ref, ps_ref, bg_ref, y_ref, cbs, dpool_ref, ya_s, ext_b, ext_c):
        i = pl.program_id(0)
        first = i == 0
        _sgu_forward(z_ref, bd_ref, ng_ref, nb_ref, wm_ref, bias_ref, ts, ya_s, None)
        ya = ya_s[...]
        ra = lax.rsqrt(_rmean(ya * ya) + EPS)
        y_ref[:, 0:SGU_WIDTH] = ((ya * ra) * bg_ref[:, 0:SGU_WIDTH]).astype(BF16)

        _conv31_forward(z_ref, zh_ref, first, cw_ref, cb_ref, ts, ext_b, cbs)
        chat, _ = _layer_norm_rows(cbs[...])
        lin = chat * cng_ref[...] + cnb_ref[...]
        yb = lin * jax.nn.sigmoid(lin)
        rb = lax.rsqrt(_rmean(yb * yb) + EPS)
        y_ref[:, SGU_WIDTH:SGU_WIDTH + CONV_WIDTH] = ((yb * rb) * bg_ref[:, SGU_WIDTH:SGU_WIDTH + CONV_WIDTH]).astype(BF16)

        dpool = _pool_forward(z_ref, zh_ref, first, i, ts, ext_c)
        dpool_ref[...] = dpool
        yc = _dot(dpool.astype(BF16), pw_ref[...]) * ps_ref[...]
        rc = lax.rsqrt(_rmean(yc * yc) + EPS)
        y_ref[:, SGU_WIDTH + CONV_WIDTH:d] = ((yc * rc) * bg_ref[:, SGU_WIDTH + CONV_WIDTH:d]).astype(BF16)

    tile, prev, _ = _mixer_specs(s, ts, inw)
    consts = [mp["bd"], mp["ng"], mp["nb"], mp["wm"], mp["bias"], mp["cw"], mp["cb"], mp["cng"], mp["cnb"],
              mp["pw"], mp["ps"], mp["bg"]]
    return _hbm_call(
        body, name=name,
        grid=(s // ts,),
        in_specs=[tile, prev] + [_const_spec(c.shape) for c in consts],
        out_specs=[pl.BlockSpec((ts, d), lambda i: (i, 0)), pl.BlockSpec((ts, CONV_WIDTH), lambda i: (i, 0)),
                   pl.BlockSpec((ts, POOL_WIDTH), lambda i: (i, 0))],
        out_shape=[jax.ShapeDtypeStruct((s, d), BF16), jax.ShapeDtypeStruct((s, CONV_WIDTH), F32),
                   jax.ShapeDtypeStruct((s, POOL_WIDTH), F32)],
        scratch_shapes=[pltpu.VMEM((ts, SGU_WIDTH), F32), pltpu.VMEM((ts + MIX_HALO, CONV_WIDTH), F32),
                        pltpu.VMEM((ts + MIX_HALO, POOL_WIDTH), F32)],
        compiler_params=pltpu.CompilerParams(dimension_semantics=("arbitrary",),
                                             vmem_limit_bytes=_vmem_limit(16 * ts * inw * 4)),
    )(z, z, *consts)


def _mixer_bwd_a(z, cb, dpool, dy, mp, name):
    s, inw = z.shape
    d = SGU_WIDTH + CONV_WIDTH + POOL_WIDTH
    ts = _row_tile(s, 256)
    nchunk = ts // CHUNK

    def rms_bwd(dyn, y, g):
        r = lax.rsqrt(_rmean(y * y) + EPS)
        yn = y * r
        dg = _rsum(dyn * yn)
        t = dyn * g
        return r * (t - yn * _rmean(t * yn)), dg

    def body(z_ref, cbs, dpool_ref, dy_ref, bd_ref, ng_ref, nb_ref, wm_ref, wmt_ref, bias_ref, cng_ref, cnb_ref,
             pw_ref, pwt_ref, ps_ref, bg_ref,
             dza_ref, dcb_ref, dd_ref, dbg_ref, dwm_ref, dbias_ref, dng_ref, dnb_ref, dcng_ref, dcnb_ref, dps_ref, dpw_ref,
             ya_s, f_s, dvn_s):
        i = pl.program_id(0)
        first = i == 0

        @pl.when(first)
        def _():
            for ref in (dwm_ref, dbias_ref, dng_ref, dnb_ref, dcng_ref, dcnb_ref, dps_ref, dpw_ref):
                ref[...] = jnp.zeros_like(ref)

        u, vhat, rstd, vn = _sgu_forward(z_ref, bd_ref, ng_ref, nb_ref, wm_ref, bias_ref, ts, ya_s, f_s)
        dya, dbg_a = rms_bwd(dy_ref[:, 0:SGU_WIDTH], ya_s[...], bg_ref[:, 0:SGU_WIDTH])
        du = dya * f_s[...]
        df = dya * u
        dfb = df.astype(BF16)
        left = _lane_lt((CHUNK, CHUNK), HEAD_DIM)
        zero = jnp.zeros((CHUNK, CHUNK), BF16)
        dbias = jnp.zeros((CHUNK, SGU_WIDTH), F32)
        for n in range(nchunk):
            rows = slice(n * CHUNK, (n + 1) * CHUNK)
            dbias = dbias + df[rows, :]
            for p in range(SGU_WIDTH // CHUNK):
                cols = slice(p * CHUNK, (p + 1) * CHUNK)
                dblk = dfb[rows, cols]
                vblk = vn[rows, cols]
                dwm_ref[2 * p] += _dot_nt(jnp.where(left, dblk, zero), vblk)
                dwm_ref[2 * p + 1] += _dot_nt(jnp.where(left, zero, dblk), vblk)
                dvn_s[rows, cols] = jnp.where(left, _dot(wmt_ref[2 * p], dblk), _dot(wmt_ref[2 * p + 1], dblk))
        dbias_ref[...] += dbias
        dvn = dvn_s[...]
        dng_ref[...] += _rsum(dvn * vhat)
        dnb_ref[...] += _rsum(dvn)
        dvh = dvn * ng_ref[...]
        bd = bd_ref[...]
        dv = rstd * (dvh - _group_mean(dvh, bd) - vhat * _group_mean(dvh * vhat, bd))
        dza_ref[:, 0:SGU_WIDTH] = (du * _gelu_grad(z_ref[:, 0:SGU_WIDTH])).astype(BF16)
        dza_ref[:, SGU_WIDTH:2 * SGU_WIDTH] = (dv * _gelu_grad(z_ref[:, SGU_WIDTH:2 * SGU_WIDTH])).astype(BF16)

        chat, crstd = _layer_norm_rows(cbs[...])
        lin = chat * cng_ref[...] + cnb_ref[...]
        sl = jax.nn.sigmoid(lin)
        dyb, dbg_b = rms_bwd(dy_ref[:, SGU_WIDTH:SGU_WIDTH + CONV_WIDTH], lin * sl, bg_ref[:, SGU_WIDTH:SGU_WIDTH + CONV_WIDTH])
        dlin = dyb * (sl * (1.0 + lin * (1.0 - sl)))
        dcng_ref[...] += _rsum(dlin * chat)
        dcnb_ref[...] += _rsum(dlin)
        dch = dlin * cng_ref[...]
        dcb_ref[...] = crstd * (dch - _rmean(dch) - chat * _rmean(dch * chat))

        dpb = dpool_ref[...].astype(BF16)
        ycp = _dot(dpb, pw_ref[...])
        dyc, dbg_c = rms_bwd(dy_ref[:, SGU_WIDTH + CONV_WIDTH:d], ycp * ps_ref[...], bg_ref[:, SGU_WIDTH + CONV_WIDTH:d])
        dps_ref[...] += _rsum(dyc * ycp)
        dycp = (dyc * ps_ref[...]).astype(BF16)
        dpw_ref[...] += _dot_tn(dpb, dycp)
        dd_ref[...] = _dot(dycp, pwt_ref[...])

        @pl.when(first)
        def _():
            dbg_ref[...] = jnp.zeros_like(dbg_ref)

        dbg_ref[:, 0:SGU_WIDTH] += dbg_a
        dbg_ref[:, SGU_WIDTH:SGU_WIDTH + CONV_WIDTH] += dbg_b
        dbg_ref[:, SGU_WIDTH + CONV_WIDTH:d] += dbg_c

    tile, _, _ = _mixer_specs(s, ts, inw)
    consts = [mp["bd"], mp["ng"], mp["nb"], mp["wm"], mp["wmt"], mp["bias"], mp["cng"], mp["cnb"],
              mp["pw"], mp["pwt"], mp["ps"], mp["bg"]]
    acc_shapes = [(1, d), (2 * (SGU_WIDTH // CHUNK), CHUNK, CHUNK), (CHUNK, SGU_WIDTH), (1, SGU_WIDTH), (1, SGU_WIDTH),
                  (1, CONV_WIDTH), (1, CONV_WIDTH), (1, POOL_WIDTH), (POOL_WIDTH, POOL_WIDTH)]
    return _hbm_call(
        body, name=name,
        grid=(s // ts,),
        in_specs=[tile, pl.BlockSpec((ts, CONV_WIDTH), lambda i: (i, 0)), pl.BlockSpec((ts, POOL_WIDTH), lambda i: (i, 0)),
                  pl.BlockSpec((ts, d), lambda i: (i, 0))] + [_const_spec(c.shape) for c in consts],
        out_specs=[pl.BlockSpec((ts, 2 * SGU_WIDTH), lambda i: (i, 0)), pl.BlockSpec((ts, CONV_WIDTH), lambda i: (i, 0)),
                   pl.BlockSpec((ts, POOL_WIDTH), lambda i: (i, 0))] + [_const_spec(a) for a in acc_shapes],
        out_shape=[jax.ShapeDtypeStruct((s, 2 * SGU_WIDTH), BF16), jax.ShapeDtypeStruct((s, CONV_WIDTH), F32),
                   jax.ShapeDtypeStruct((s, POOL_WIDTH), F32)] + [jax.ShapeDtypeStruct(a, F32) for a in acc_shapes],
        scratch_shapes=[pltpu.VMEM((ts, SGU_WIDTH), F32), pltpu.VMEM((ts, SGU_WIDTH), F32), pltpu.VMEM((ts, SGU_WIDTH), F32)],
        compiler_params=pltpu.CompilerParams(dimension_semantics=("arbitrary",),
                                             vmem_limit_bytes=_vmem_limit(24 * ts * inw * 4)),
    )(z, cb, dpool, dy, *consts)


def _mixer_bwd_b(z, dza, dcb, dd, x, dres, gain, s1p, cw, w, name):
    s, inw = z.shape
    d = x.shape[1]
    ts = _row_tile(s, 256)
    c0 = 2 * SGU_WIDTH
    c1 = c0 + 2 * CONV_WIDTH

    def body(z_ref, zh_ref, dza_ref, dcb_ref, dcbn_ref, dd_ref, ddn_ref, x_ref, dres_ref, g_ref, s_ref, cw_ref, w_ref,
             dx_ref, dz_ref, dsh_ref, dsc_ref, dg_ref, dcw_ref, dcbias_ref, ext_b, ext_n, ext_e):
        i = pl.program_id(0)
        first = i == 0
        last = i == pl.num_programs(0) - 1

        @pl.when(first)
        def _():
            for ref in (dsh_ref, dsc_ref, dg_ref, dcw_ref, dcbias_ref):
                ref[...] = jnp.zeros_like(ref)

        a = z_ref[:, c0:c0 + CONV_WIDTH]
        sg = jax.nn.sigmoid(z_ref[:, c0 + CONV_WIDTH:c1])
        ah = zh_ref[:, c0:c0 + CONV_WIDTH]
        gh = zh_ref[:, c0 + CONV_WIDTH:c1]
        ext_b[pl.ds(0, MIX_HALO), :] = jnp.where(first, 0.0, ah * jax.nn.sigmoid(gh))
        ext_b[pl.ds(MIX_HALO, ts), :] = a * sg
        dcbv = dcb_ref[...]
        dcbias_ref[...] += _rsum(dcbv)
        for k in range(CONV_K):
            dcw_ref[k:k + 1, :] += _rsum(dcbv * ext_b[pl.ds(MIX_HALO - (CONV_K - 1) + k, ts), :])

        ext_n[pl.ds(0, ts), :] = dcbv
        ext_n[pl.ds(ts, MIX_HALO), :] = jnp.where(last, 0.0, dcbn_ref[...])
        for r in range(ts // CONV_ROWS):
            acc = jnp.zeros((CONV_ROWS, CONV_WIDTH), F32)
            for k in range(CONV_K):
                acc = acc + cw_ref[k:k + 1, :] * ext_n[pl.ds(CONV_K - 1 - k + r * CONV_ROWS, CONV_ROWS), :]
            rows = pl.ds(r * CONV_ROWS, CONV_ROWS)
            ar = z_ref[rows, c0:c0 + CONV_WIDTH]
            sr = jax.nn.sigmoid(z_ref[rows, c0 + CONV_WIDTH:c1])
            dz_ref[rows, c0:c0 + CONV_WIDTH] = (acc * sr).astype(BF16)
            dz_ref[rows, c0 + CONV_WIDTH:c1] = (acc * ar * sr * (1.0 - sr)).astype(BF16)

        ddv = dd_ref[...]
        ext_e[pl.ds(0, ts), :] = ddv / _pool_counts(i, ts)
        nh = (i + 1) * ts + lax.broadcasted_iota(jnp.int32, (MIX_HALO, POOL_WIDTH), 0)
        lane = lax.broadcasted_iota(jnp.int32, (MIX_HALO, POOL_WIDTH), 1)
        gdim = POOL_WIDTH // len(POOL_WINDOWS)
        winh = jnp.where(lane < gdim, float(POOL_WINDOWS[0]),
                         jnp.where(lane < 2 * gdim, float(POOL_WINDOWS[1]),
                                   jnp.where(lane < 3 * gdim, float(POOL_WINDOWS[2]), float(POOL_WINDOWS[3]))))
        cnth = jnp.minimum((nh + 1).astype(F32), winh)
        ext_e[pl.ds(ts, MIX_HALO), :] = jnp.where(last, 0.0, ddn_ref[...] / cnth)
        dz_ref[:, c1:inw] = (_window_sums(ext_e, 0, ts, 1) - ddv).astype(BF16)
        dz_ref[:, 0:c0] = dza_ref[...]

        dh = _dot_nt(dz_ref[...], w_ref[...])
        dx, dsh, dsc, dg = _norm_mod_bwd(dh, x_ref[...], g_ref[...], s_ref[...], dres_ref[...])
        dx_ref[...] = dx
        dsh_ref[...] += dsh
        dsc_ref[...] += dsc
        dg_ref[...] += dg

    tile, prev, _ = _mixer_specs(s, ts, inw)
    _, _, nxt_b = _mixer_specs(s, ts, CONV_WIDTH)
    _, _, nxt_c = _mixer_specs(s, ts, POOL_WIDTH)
    row = pl.BlockSpec((ts, d), lambda i: (i, 0))
    vec = pl.BlockSpec((1, d), lambda i: (0, 0))
    return _hbm_call(
        body, name=name,
        grid=(s // ts,),
        in_specs=[tile, prev, pl.BlockSpec((ts, c0), lambda i: (i, 0)),
                  pl.BlockSpec((ts, CONV_WIDTH), lambda i: (i, 0)), nxt_b,
                  pl.BlockSpec((ts, POOL_WIDTH), lambda i: (i, 0)), nxt_c,
                  row, row, vec, vec, _const_spec(cw.shape),
                  pl.BlockSpec(w.shape, lambda i: (0, 0), pipeline_mode=pl.Buffered(1))],
        out_specs=[row, pl.BlockSpec((ts, inw), lambda i: (i, 0)), vec, vec, vec,
                   _const_spec((CONV_K, CONV_WIDTH)), _const_spec((1, CONV_WIDTH))],
        out_shape=[jax.ShapeDtypeStruct((s, d), F32), jax.ShapeDtypeStruct((s, inw), BF16)]
        + [jax.ShapeDtypeStruct((1, d), F32)] * 3
        + [jax.ShapeDtypeStruct((CONV_K, CONV_WIDTH), F32), jax.ShapeDtypeStruct((1, CONV_WIDTH), F32)],
        scratch_shapes=[pltpu.VMEM((ts + MIX_HALO, CONV_WIDTH), F32), pltpu.VMEM((ts + MIX_HALO, CONV_WIDTH), F32),
                        pltpu.VMEM((ts + MIX_HALO, POOL_WIDTH), F32)],
        compiler_params=pltpu.CompilerParams(dimension_semantics=("arbitrary",),
                                             vmem_limit_bytes=_vmem_limit(16 * ts * inw * 4 + inw * d * 2)),
    )(z, z, dza, dcb, dcb, dd, dd, x, dres, gain, s1p, cw, w)


def _ffn_specs(s, ts, tc, half_blocks):
    nbh = ts // FFN_HALO

    def tile(off):
        return pl.BlockSpec((ts, tc), lambda j, i: (i, j + off))

    def prev(off):
        return pl.BlockSpec((FFN_HALO, tc), lambda j, i: (jnp.maximum(i * nbh - 1, 0), j + off))

    def vec(rows, off):
        return pl.BlockSpec((rows, tc), lambda j, i: (0, j + off))

    return tile, prev, vec


def _rows_before(cur, prev, k):
    row = lax.broadcasted_iota(jnp.int32, cur.shape, 0)
    return jnp.where(row >= k, pltpu.roll(cur, k, 0), pltpu.roll(prev, k, 0))


def _conv3_rows(cur, prev, w_ref, b_ref, cols):
    x1 = _rows_before(cur, prev, 1)
    x2 = _rows_before(cur, prev, 2)
    u = b_ref[:, cols] + w_ref[2:3, cols] * cur + w_ref[1:2, cols] * x1 + w_ref[0:1, cols] * x2
    return u, x2, x1


def _halo_chunk(h_ref, cols, first):
    h = jnp.where(first, 0.0, h_ref[:, cols])
    return jnp.concatenate([h] * (FFN_ROWS // FFN_HALO), axis=0)


def _ffn_act_fwd(p, cw, cb, name):
    s, f2 = p.shape
    f = f2 // 2
    tc = f // 2
    hb = f // tc
    ts = _row_tile(s, 256)

    def body(pg_ref, pgh_ref, pv_ref, pvh_ref, wg_ref, wv_ref, bg_ref, bv_ref, act_ref):
        first = pl.program_id(1) == 0
        for c in range(tc // LANES):
            cols = slice(c * LANES, (c + 1) * LANES)

            def chunk(r, carry, cols=cols):
                pg_prev, pv_prev = carry
                rows = pl.ds(pl.multiple_of(r * FFN_ROWS, FFN_ROWS), FFN_ROWS)
                pg = pg_ref[rows, cols]
                pv = pv_ref[rows, cols]
                ug, _, _ = _conv3_rows(pg, pg_prev, wg_ref, bg_ref, cols)
                uv, _, _ = _conv3_rows(pv, pv_prev, wv_ref, bv_ref, cols)
                act_ref[rows, cols] = (_gelu(ug) * uv).astype(BF16)
                return pg, pv

            def step(r, carry, chunk=chunk):
                for u in range(FFN_UNROLL):
                    carry = chunk(r * FFN_UNROLL + u, carry)
                return carry

            lax.fori_loop(0, ts // (FFN_ROWS * FFN_UNROLL), step,
                          (_halo_chunk(pgh_ref, cols, first), _halo_chunk(pvh_ref, cols, first)))

    tile, prev, vec = _ffn_specs(s, ts, tc, hb)
    return _hbm_call(
        body, name=name,
        grid=(hb, s // ts),
        in_specs=[tile(0), prev(0), tile(hb), prev(hb), vec(FFN_CONV_K, 0), vec(FFN_CONV_K, hb), vec(1, 0), vec(1, hb)],
        out_specs=pl.BlockSpec((ts, tc), lambda j, i: (i, j)),
        out_shape=jax.ShapeDtypeStruct((s, f), BF16),
        compiler_params=pltpu.CompilerParams(dimension_semantics=("arbitrary", "arbitrary"),
                                             vmem_limit_bytes=_vmem_limit(8 * ts * tc * 4)),
    )(p, p, p, p, cw, cw, cb, cb)


def _ffn_act_bwd(p, dact, cw, cb, name):
    s, f2 = p.shape
    f = f2 // 2
    tc = f // 2
    hb = f // tc
    ts = _row_tile(s, 256)

    def body(pg_ref, pgh_ref, pv_ref, pvh_ref, da_ref, wg_ref, wv_ref, bg_ref, bv_ref,
             dug_ref, duv_ref, dwg_ref, dwv_ref, dbg_ref, dbv_ref):
        first = pl.program_id(1) == 0

        @pl.when(first)
        def _():
            for ref in (dwg_ref, dwv_ref, dbg_ref, dbv_ref):
                ref[...] = jnp.zeros_like(ref)

        zero = jnp.zeros((FFN_ROWS, LANES), F32)
        for c in range(tc // LANES):
            cols = slice(c * LANES, (c + 1) * LANES)

            def chunk(r, carry, cols=cols):
                pg_prev, pv_prev, ag0, ag1, ag2, av0, av1, av2, sg, sv = carry
                rows = pl.ds(pl.multiple_of(r * FFN_ROWS, FFN_ROWS), FFN_ROWS)
                pg = pg_ref[rows, cols]
                pv = pv_ref[rows, cols]
                ug, pg2, pg1 = _conv3_rows(pg, pg_prev, wg_ref, bg_ref, cols)
                uv, pv2, pv1 = _conv3_rows(pv, pv_prev, wv_ref, bv_ref, cols)
                da = da_ref[rows, cols]
                dug = da * uv * _gelu_grad(ug)
                duv = da * _gelu(ug)
                dug_ref[rows, cols] = dug
                duv_ref[rows, cols] = duv
                return (pg, pv, ag0 + dug * pg2, ag1 + dug * pg1, ag2 + dug * pg,
                        av0 + duv * pv2, av1 + duv * pv1, av2 + duv * pv, sg + dug, sv + duv)

            def step(r, carry, chunk=chunk):
                for u in range(FFN_UNROLL):
                    carry = chunk(r * FFN_UNROLL + u, carry)
                return carry

            out = lax.fori_loop(0, ts // (FFN_ROWS * FFN_UNROLL), step,
                                (_halo_chunk(pgh_ref, cols, first), _halo_chunk(pvh_ref, cols, first)) + (zero,) * 8)
            for k in range(FFN_CONV_K):
                dwg_ref[k:k + 1, cols] += _rsum(out[2 + k])
                dwv_ref[k:k + 1, cols] += _rsum(out[5 + k])
            dbg_ref[:, cols] += _rsum(out[8])
            dbv_ref[:, cols] += _rsum(out[9])

    tile, prev, vec = _ffn_specs(s, ts, tc, hb)
    half = pl.BlockSpec((ts, tc), lambda j, i: (i, j))
    wacc = pl.BlockSpec((FFN_CONV_K, tc), lambda j, i: (0, j))
    bacc = pl.BlockSpec((1, tc), lambda j, i: (0, j))
    return _hbm_call(
        body, name=name,
        grid=(hb, s // ts),
        in_specs=[tile(0), prev(0), tile(hb), prev(hb), half, vec(FFN_CONV_K, 0), vec(FFN_CONV_K, hb), vec(1, 0), vec(1, hb)],
        out_specs=[half, half, wacc, wacc, bacc, bacc],
        out_shape=[jax.ShapeDtypeStruct((s, f), F32)] * 2 + [jax.ShapeDtypeStruct((FFN_CONV_K, f), F32)] * 2
        + [jax.ShapeDtypeStruct((1, f), F32)] * 2,
        compiler_params=pltpu.CompilerParams(dimension_semantics=("arbitrary", "arbitrary"),
                                             vmem_limit_bytes=_vmem_limit(12 * ts * tc * 4)),
    )(p, p, p, p, dact, cw, cw, cb, cb)


def _ffn_in_bwd(dug, duv, cw, w, x, dres, gain, s1p, name):
    s, f = dug.shape
    d = x.shape[1]
    ts = _row_tile(s, 256)
    tc = w.shape[2]
    assert f % tc == 0 and w.shape[0] * tc == 2 * f
    nbh = ts // FFN_HALO

    def body(dug_ref, dugn_ref, duv_ref, duvn_ref, cw_ref, w_ref, x_ref, dres_ref, g_ref, s_ref,
             dx_ref, dp_ref, dsh_ref, dsc_ref, dg_ref, ext):
        i = pl.program_id(0)
        last = i == pl.num_programs(0) - 1

        @pl.when(i == 0)
        def _():
            for ref in (dsh_ref, dsc_ref, dg_ref):
                ref[...] = jnp.zeros_like(ref)

        dh = jnp.zeros((ts, d), F32)
        for half, (t_ref, n_ref) in enumerate(((dug_ref, dugn_ref), (duv_ref, duvn_ref))):
            for cb in range(f // tc):
                cols = slice(cb * tc, (cb + 1) * tc)
                wcols = slice(half * f + cb * tc, half * f + (cb + 1) * tc)
                ext[pl.ds(0, ts), :] = t_ref[:, cols]
                ext[pl.ds(ts, FFN_HALO), :] = jnp.where(last, 0.0, n_ref[:, cols])
                acc = cw_ref[FFN_CONV_K - 1:FFN_CONV_K, wcols] * t_ref[:, cols]
                for k in range(FFN_CONV_K - 1):
                    acc = acc + cw_ref[k:k + 1, wcols] * ext[pl.ds(FFN_CONV_K - 1 - k, ts), :]
                dpb = acc.astype(BF16)
                dp_ref[:, wcols] = dpb
                dh = dh + _dot_nt(dpb, w_ref[half * (f // tc) + cb])
        dx, dsh, dsc, dg = _norm_mod_bwd(dh, x_ref[...], g_ref[...], s_ref[...], dres_ref[...])
        dx_ref[...] = dx
        dsh_ref[...] += dsh
        dsc_ref[...] += dsc
        dg_ref[...] += dg

    tile = pl.BlockSpec((ts, f), lambda i: (i, 0))
    nxt = pl.BlockSpec((FFN_HALO, f), lambda i: (jnp.minimum((i + 1) * nbh, s // FFN_HALO - 1), 0))
    row = pl.BlockSpec((ts, d), lambda i: (i, 0))
    vec = pl.BlockSpec((1, d), lambda i: (0, 0))
    return _hbm_call(
        body, name=name,
        grid=(s // ts,),
        in_specs=[tile, nxt, tile, nxt, _const_spec(cw.shape),
                  pl.BlockSpec(w.shape, lambda i: (0, 0, 0), pipeline_mode=pl.Buffered(1)), row, row, vec, vec],
        out_specs=[row, pl.BlockSpec((ts, 2 * f), lambda i: (i, 0)), vec, vec, vec],
        out_shape=[jax.ShapeDtypeStruct((s, d), F32), jax.ShapeDtypeStruct((s, 2 * f), BF16)] + [jax.ShapeDtypeStruct((1, d), F32)] * 3,
        scratch_shapes=[pltpu.VMEM((ts + FFN_HALO, tc), F32)],
        compiler_params=pltpu.CompilerParams(
            dimension_semantics=("arbitrary",),
            vmem_limit_bytes=_vmem_limit(4 * ts * f * 4 + 2 * f * d * 2 + 2 * ts * 2 * f * 2 + 12 * ts * d * 4 + 6 * ts * tc * 4)),
    )(dug, dug, duv, duv, cw, w, x, dres, gain, s1p)


def _adamw_math(w, g, m, v):
    m = ADAM_B1 * m + (1.0 - ADAM_B1) * g
    v = ADAM_B2 * v + (1.0 - ADAM_B2) * (g * g)
    m_hat = m / (1.0 - ADAM_B1 ** ADAM_STEP)
    v_hat = v / (1.0 - ADAM_B2 ** ADAM_STEP)
    delta = -ADAM_LR * (m_hat / (jnp.sqrt(v_hat) + ADAM_EPS) + ADAM_WD * w)
    return delta, m, v


def _adam_rows(rows, cols):
    want = max(8, (2 * 1024 * 1024 // (cols * 4)) // 8 * 8)
    tr = min(rows, want)
    while rows % tr:
        tr -= 8
    return tr


def _adamw(w, m, v, g_parts, name):
    shape = w.shape
    nl = shape[0] if w.ndim == 3 else 1
    r, c = shape[-2], shape[-1]
    tr = _adam_rows(r, c)
    ng = len(g_parts)

    def body(*refs):
        w_ref, m_ref, v_ref = refs[0:3]
        g_refs = refs[3:3 + ng]
        g_out, d_out, m_out, v_out = refs[3 + ng:]
        g = g_refs[0][...]
        for gr in g_refs[1:]:
            g = g + gr[...]
        delta, mn, vn = _adamw_math(w_ref[...], g, m_ref[...], v_ref[...])
        g_out[...] = g
        d_out[...] = delta
        m_out[...] = mn
        v_out[...] = vn

    blk = pl.BlockSpec((None, tr, c), lambda l, i: (l, i, 0))
    outs = _hbm_call(
        body, name=name,
        grid=(nl, r // tr),
        in_specs=[blk] * (3 + ng),
        out_specs=[blk] * 4,
        out_shape=[jax.ShapeDtypeStruct((nl, r, c), F32)] * 4,
        compiler_params=pltpu.CompilerParams(dimension_semantics=("arbitrary", "arbitrary"),
                                             vmem_limit_bytes=_vmem_limit(2 * (7 + ng) * tr * max(c, 128) * 4 + (8 << 20))),
    )(*[a.reshape(nl, r, c) for a in (w, m, v, *g_parts)])
    return [o.reshape(shape) for o in outs]


def _modw_adamw(sct, dmod, w, m, v, name):
    nl, d, n = w.shape
    tr = _row_tile(d, 128)

    def body(sct_ref, dm_ref, w_ref, m_ref, v_ref, g_out, d_out, m_out, v_out):
        sc = sct_ref[...].astype(BF16).astype(F32)
        dm = dm_ref[...].astype(BF16).astype(F32)
        g = sc[:, 0:1] * dm[0:1, :]
        for b in range(1, N_DEV):
            g = g + sc[:, b:b + 1] * dm[b:b + 1, :]
        delta, mn, vn = _adamw_math(w_ref[...], g, m_ref[...], v_ref[...])
        g_out[...] = g
        d_out[...] = delta
        m_out[...] = mn
        v_out[...] = vn

    blk = pl.BlockSpec((None, tr, n), lambda l, i: (l, i, 0))
    return _hbm_call(
        body, name=name,
        grid=(nl, d // tr),
        in_specs=[pl.BlockSpec((tr, N_DEV), lambda l, i: (i, 0)), pl.BlockSpec((None, N_DEV, n), lambda l, i: (l, 0, 0)),
                  blk, blk, blk],
        out_specs=[blk] * 4,
        out_shape=[jax.ShapeDtypeStruct((nl, d, n), F32)] * 4,
        compiler_params=pltpu.CompilerParams(dimension_semantics=("arbitrary", "arbitrary"),
                                             vmem_limit_bytes=_vmem_limit(2 * 8 * tr * n * 4 + (8 << 20))),
    )(sct, dmod, w, m, v)


def _reduce4(recvs, name):
    nl = len(recvs)
    shape = recvs[0].shape[1:]
    c = shape[-1]
    r = math.prod(shape[:-1])
    tr = _adam_rows(r, c)
    nt = r // tr

    def body(*refs):
        o_ref = refs[nl]
        for l in range(nl):
            @pl.when(pl.program_id(0) == l)
            def _():
                acc = refs[l][0].astype(F32)
                for k in range(1, N_CHIPS):
                    acc = acc + refs[l][k].astype(F32)
                o_ref[...] = acc

    def in_map(l):
        return lambda ll, i: (0, jnp.where(ll < l, 0, jnp.where(ll > l, nt - 1, i)), 0)

    return _hbm_call(
        body, name=name,
        grid=(nl, nt),
        in_specs=[pl.BlockSpec((N_CHIPS, tr, c), in_map(l)) for l in range(nl)],
        out_specs=pl.BlockSpec((None, tr, c), lambda ll, i: (ll, i, 0)),
        out_shape=jax.ShapeDtypeStruct((nl, r, c), F32),
        compiler_params=pltpu.CompilerParams(dimension_semantics=("arbitrary", "arbitrary"),
                                             vmem_limit_bytes=_vmem_limit(2 * 8 * nl * tr * max(c, 128) * 4 + (8 << 20))),
    )(*[rv.reshape(N_CHIPS, r, c) for rv in recvs]).reshape((nl,) + shape)


def _my_place():
    return lax.axis_index("x"), lax.axis_index("y"), lax.axis_index("c")


def _chip_coords(j):
    return j // 2, j % 2


def _mod_forward(c, mod_w, mod_b4):
    nl, d, n = mod_w.shape
    kc = 256

    def body(c_ref, w_ref, b_ref, mod_ref, sc_ref, cbuf, stage, s1, r1, s2, r2):
        mx, my, mc = _my_place()
        me = 4 * mx + 2 * my + mc
        q = 2 * mx + my
        cv = c_ref[...]
        cbuf[me] = jnp.broadcast_to(cv * jax.nn.sigmoid(cv), (8, d))
        sends = []
        for t in range(N_DEV):
            tx, ty = _chip_coords(t // 2)
            cp = pltpu.make_async_remote_copy(src_ref=cbuf.at[me], dst_ref=cbuf.at[me], send_sem=s1.at[t], recv_sem=r1.at[me],
                                              device_id=(tx, ty, t % 2), device_id_type=MESH)

            @pl.when(t != me)
            def _():
                cp.start()

            sends.append((t, cp))
        for t in range(N_DEV):
            @pl.when(t != me)
            def _():
                pltpu.make_async_remote_copy(src_ref=cbuf.at[t], dst_ref=cbuf.at[t], send_sem=s1.at[t], recv_sem=r1.at[t],
                                             device_id=(mx, my, mc), device_id_type=MESH).wait_recv()
        for t, cp in sends:
            @pl.when(t != me)
            def _():
                cp.wait_send()

        row = lax.broadcasted_iota(jnp.int32, (8, d), 0)
        sc_all = jnp.zeros((8, d), F32)
        for t in range(N_DEV):
            sc_all = sc_all + jnp.where(row == t, cbuf[t], 0.0)
        sc_ref[...] = sc_all
        rown = lax.broadcasted_iota(jnp.int32, (8, n), 0)
        for l in range(nl):
            acc = jnp.zeros((8, n), F32)
            for k0 in range(0, d, kc):
                acc = acc + _dot(sc_all[:, k0:k0 + kc].astype(BF16), w_ref[l, k0:k0 + kc, :].astype(BF16))
            acc = acc + b_ref[l, q]
            for j in range(N_CHIPS):
                jx, jy = _chip_coords(j)
                bdest = 4 * jx + 2 * jy + mc
                rowv = jnp.sum(jnp.where(rown == bdest, acc, 0.0), axis=0, keepdims=True)
                stage[j, l] = jnp.broadcast_to(rowv, (8, n))
        sends2 = []
        for j in range(N_CHIPS):
            jx, jy = _chip_coords(j)
            cp = pltpu.make_async_remote_copy(src_ref=stage.at[j], dst_ref=mod_ref.at[:, q], send_sem=s2.at[j], recv_sem=r2.at[q],
                                              device_id=(jx, jy, mc), device_id_type=MESH)

            @pl.when(j != q)
            def _():
                cp.start()

            @pl.when(j == q)
            def _():
                for l in range(nl):
                    mod_ref[l, j] = stage[j, l]

            sends2.append((j, cp))
        for j in range(N_CHIPS):
            @pl.when(j != q)
            def _():
                pltpu.make_async_remote_copy(src_ref=stage.at[j], dst_ref=mod_ref.at[:, j], send_sem=s2.at[j], recv_sem=r2.at[j],
                                             device_id=(mx, my, mc), device_id_type=MESH).wait_recv()
        for j, cp in sends2:
            @pl.when(j != q)
            def _():
                cp.wait_send()

    vm = pl.BlockSpec(memory_space=pltpu.VMEM)
    return pl.pallas_call(
        body, name="mod_forward",
        in_specs=[vm, vm, vm],
        out_specs=[vm, vm],
        out_shape=[jax.ShapeDtypeStruct((nl, N_CHIPS, 8, n), F32), jax.ShapeDtypeStruct((8, d), F32)],
        scratch_shapes=[pltpu.VMEM((N_DEV, 8, d), F32), pltpu.VMEM((N_CHIPS, nl, 8, n), F32),
                        pltpu.SemaphoreType.DMA((N_DEV,)), pltpu.SemaphoreType.DMA((N_DEV,)),
                        pltpu.SemaphoreType.DMA((N_CHIPS,)), pltpu.SemaphoreType.DMA((N_CHIPS,))],
        compiler_params=pltpu.CompilerParams(vmem_limit_bytes=_vmem_limit(2 * nl * d * n * 4 + (8 << 20))),
    )(c, mod_w, mod_b4)


_HBM_SPEC = pl.BlockSpec(memory_space=pltpu.HBM)
_SEM_SPEC = pl.BlockSpec(memory_space=pltpu.SEMAPHORE)
_DATAFLOW = pltpu.SideEffectType.DATAFLOW_SIDE_EFFECTING


def _slot(ref, scatter, j):
    return ref.at[j] if scatter else ref


def _exchange_start(groups, scatter, after, name):
    flat = [a for g in groups for a in g]
    na = len(flat)
    ng = len(groups)
    sizes = [len(g) for g in groups]
    first = [sum(sizes[:g]) for g in range(ng)]
    where = [(g, k) for g in range(ng) for k in range(sizes[g])]
    mx, my, _ = _my_place()
    qo = 2 * mx + my
    lands = []
    for a in flat:
        own = lax.dynamic_index_in_dim(a, qo, 0, keepdims=False) if scatter else a
        lands.append(lax.dynamic_update_index_in_dim(lax.empty((N_CHIPS,) + own.shape, a.dtype), own, qo, 0))

    def body(*refs):
        ins, lnd = refs[:na], refs[na:2 * na]
        ssems, rsems = refs[2 * na + 1:2 * na + 1 + ng], refs[2 * na + 1 + ng:2 * na + 1 + 2 * ng]
        token = refs[-1]
        mx, my, mc = _my_place()
        q = 2 * mx + my
        for j in range(N_CHIPS):
            jx, jy = _chip_coords(j)
            for a in range(na):
                g, k = where[a]

                @pl.when(j != q)
                def _():
                    pltpu.make_async_remote_copy(src_ref=_slot(ins[a], scatter, j), dst_ref=lnd[a].at[q],
                                                 send_sem=ssems[g].at[k * N_CHIPS + j], recv_sem=rsems[g].at[k * N_CHIPS + q],
                                                 device_id=(jx, jy, mc), device_id_type=MESH).start()
        token[...] = jnp.zeros_like(token)

    sem_shapes = [pltpu.SemaphoreType.DMA((n * N_CHIPS,)) for n in sizes]
    outs = pl.pallas_call(
        body, name=name,
        in_specs=[_HBM_SPEC] * (2 * na) + [pl.BlockSpec(memory_space=pl.ANY)],
        out_specs=[_SEM_SPEC] * (2 * ng) + [_HBM_SPEC] * (2 * na) + [pl.BlockSpec(memory_space=pltpu.VMEM)],
        out_shape=sem_shapes + sem_shapes + [pltpu.HBM(a.shape, a.dtype) for a in flat + lands]
        + [jax.ShapeDtypeStruct((8, 128), F32)],
        input_output_aliases={i: 2 * ng + i for i in range(2 * na)},
        compiler_params=pltpu.CompilerParams(has_side_effects=_DATAFLOW),
    )(*[pltpu.with_memory_space_constraint(a, pltpu.HBM) for a in flat + lands], after)
    ssems, rsems = outs[:ng], outs[ng:2 * ng]
    src_thru, land_thru = outs[2 * ng:2 * ng + na], outs[2 * ng + na:2 * ng + 2 * na]
    states = [(src_thru[first[g]:first[g] + sizes[g]], land_thru[first[g]:first[g] + sizes[g]], ssems[g], rsems[g])
              for g in range(ng)]
    return states, outs[-1]


def _exchange_wait(state, scatter, after, name):
    srcs, lands, ssem, rsem = state
    na = len(srcs)

    def body(*refs):
        ins, lnd = refs[:na], refs[na:2 * na]
        ssem_ref, rsem_ref = refs[2 * na], refs[2 * na + 1]
        mx, my, mc = _my_place()
        q = 2 * mx + my
        for j in range(N_CHIPS):
            for a in range(na):
                @pl.when(j != q)
                def _():
                    cp = pltpu.make_async_remote_copy(src_ref=_slot(ins[a], scatter, j), dst_ref=lnd[a].at[j],
                                                      send_sem=ssem_ref.at[a * N_CHIPS + j], recv_sem=rsem_ref.at[a * N_CHIPS + j],
                                                      device_id=(mx, my, mc), device_id_type=MESH)
                    cp.wait_send()
                    cp.wait_recv()

    outs = pl.pallas_call(
        body, name=name,
        in_specs=[_HBM_SPEC] * (2 * na) + [_SEM_SPEC, _SEM_SPEC, pl.BlockSpec(memory_space=pl.ANY)],
        out_specs=[_HBM_SPEC] * (2 * na),
        out_shape=[pltpu.HBM(a.shape, a.dtype) for a in list(srcs) + list(lands)],
        input_output_aliases={i: i for i in range(2 * na)},
        compiler_params=pltpu.CompilerParams(has_side_effects=_DATAFLOW),
    )(*srcs, *lands, ssem, rsem, after)
    return outs[na:]


def _swap_with_sibling(arrs):
    na = len(arrs)

    def body(*refs):
        ins, outs = refs[:na], refs[na:2 * na]
        ssem, rsem = refs[2 * na:]
        mx, my, mc = _my_place()
        cps = [pltpu.make_async_remote_copy(src_ref=ins[a], dst_ref=outs[a], send_sem=ssem.at[a], recv_sem=rsem.at[a],
                                            device_id=(mx, my, 1 - mc), device_id_type=MESH) for a in range(na)]
        for cp in cps:
            cp.start()
        for cp in cps:
            cp.wait()

    hbm = pl.BlockSpec(memory_space=pl.ANY)
    return _hbm_call(
        body, name="swap_sibling",
        in_specs=[hbm] * na,
        out_specs=[hbm] * na,
        out_shape=[jax.ShapeDtypeStruct(a.shape, a.dtype) for a in arrs],
        scratch_shapes=[pltpu.SemaphoreType.DMA((na,)), pltpu.SemaphoreType.DMA((na,))],
    )(*arrs)


def _allreduce_small(rows_all, rows_sum):
    ra, c = rows_all.shape
    r = rows_sum.shape[0]
    ch = r // N_DEV
    assert ch % 8 == 0 and ch * N_DEV == r

    def body(a_ref, s_ref, all_ref, sum_ref, rbuf, red, sa, rva, sb, rvb, sc, rvc):
        mx, my, mc = _my_place()
        me = 4 * mx + 2 * my + mc
        mine = pl.ds(pl.multiple_of(me * ch, 8), ch)
        all_ref[me] = a_ref[...]
        rbuf[me] = s_ref[mine, :]

        def dev(t):
            tx, ty = _chip_coords(t // 2)
            return (tx, ty, t % 2)

        def everyone_else(fn):
            for t in range(N_DEV):
                @pl.when(t != me)
                def _():
                    fn(t)

        def copy_a(t, slot):
            return pltpu.make_async_remote_copy(src_ref=a_ref, dst_ref=all_ref.at[slot], send_sem=sa.at[t], recv_sem=rva.at[slot],
                                                device_id=dev(t), device_id_type=MESH)

        def copy_b(t, slot):
            return pltpu.make_async_remote_copy(src_ref=s_ref.at[pl.ds(t * ch, ch), :], dst_ref=rbuf.at[slot], send_sem=sb.at[t],
                                                recv_sem=rvb.at[slot], device_id=dev(t), device_id_type=MESH)

        def copy_c(t, chunk_start, slot):
            return pltpu.make_async_remote_copy(src_ref=red, dst_ref=sum_ref.at[pl.ds(chunk_start, ch), :], send_sem=sc.at[t],
                                                recv_sem=rvc.at[slot], device_id=dev(t), device_id_type=MESH)

        everyone_else(lambda t: (copy_a(t, me).start(), copy_b(t, me).start()))
        everyone_else(lambda t: (copy_a(t, t).wait_recv(), copy_b(t, t).wait_recv()))
        everyone_else(lambda t: (copy_a(t, me).wait_send(), copy_b(t, me).wait_send()))
        acc = rbuf[0]
        for t in range(1, N_DEV):
            acc = acc + rbuf[t]
        red[...] = acc
        sum_ref[mine, :] = acc
        everyone_else(lambda t: copy_c(t, pl.multiple_of(me * ch, 8), me).start())
        everyone_else(lambda t: copy_c(t, t * ch, t).wait_recv())
        everyone_else(lambda t: copy_c(t, pl.multiple_of(me * ch, 8), me).wait_send())

    vm = pl.BlockSpec(memory_space=pltpu.VMEM)
    return pl.pallas_call(
        body, name="allreduce_small",
        in_specs=[vm, vm],
        out_specs=[vm, vm],
        out_shape=[jax.ShapeDtypeStruct((N_DEV, ra, c), F32), jax.ShapeDtypeStruct((r, c), F32)],
        scratch_shapes=[pltpu.VMEM((N_DEV, ch, c), F32), pltpu.VMEM((ch, c), F32)] + [pltpu.SemaphoreType.DMA((N_DEV,))] * 6,
        compiler_params=pltpu.CompilerParams(vmem_limit_bytes=_vmem_limit((3 * r + 2 * N_DEV * ra) * c * 4 + (4 << 20))),
    )(rows_all, rows_sum)


def _pack(arrs, row_multiple=8):
    rows, layout, at = [], [], 0
    for a in arrs:
        n = a.size
        nr = -(-n // (8 * SMALL_COLS)) * 8
        flat = a.reshape(-1)
        if nr * SMALL_COLS != n:
            flat = jnp.pad(flat, (0, nr * SMALL_COLS - n))
        rows.append(flat.reshape(nr, SMALL_COLS))
        layout.append((at, nr, a.shape))
        at += nr
    pad = -at % row_multiple
    if pad:
        rows.append(jnp.zeros((pad, SMALL_COLS), F32))
    return jnp.concatenate(rows, axis=0), layout


def _unpack(buf, layout):
    out = []
    for at, nr, shape in layout:
        n = math.prod(shape)
        out.append(buf[at:at + nr].reshape(-1)[:n].reshape(shape))
    return out


SMALL_NAMES = ("mod_b", "mix_pre_g", "mix_post_g", "sgu_norm_g", "sgu_norm_b", "sgu_w", "sgu_b", "conv_b", "conv_norm_g",
               "conv_norm_b", "pool_w", "pool_scale", "branch_g", "ffn_pre_g", "ffn_post_g", "ffn_conv_b")
SHARDED_SMALL = ("conv_w", "ffn_conv_w")
WEIGHT_ORDER = ("mod_w", "mod_b", "mix_pre_g", "mix_post_g", "w_in", "sgu_norm_g", "sgu_norm_b", "sgu_w", "sgu_b", "conv_w",
                "conv_b", "conv_norm_g", "conv_norm_b", "pool_w", "pool_scale", "branch_g", "w_out", "ffn_pre_g", "ffn_post_g",
                "ffn_up", "ffn_conv_w", "ffn_conv_b", "ffn_down")


def _block_diag(blocks):
    n, a, b = blocks.shape
    eye = jnp.eye(n, dtype=blocks.dtype)
    return (eye[:, None, :, None] * blocks[:, :, None, :]).reshape(n * a, n * b)


def _diag_blocks(mat, n):
    a = mat.shape[0] // n
    return jnp.stack([mat[g * a:(g + 1) * a, g * a:(g + 1) * a] for g in range(n)])


def _step(x, c, loss_target, w, m, v):
    nl = w["mod_w"].shape[0]
    s, d = x.shape[1], x.shape[2]
    heads = SGU_WIDTH // HEAD_DIM
    groups = len(POOL_WINDOWS)
    mx, my, _ = _my_place()
    q = 2 * mx + my
    x0 = x.reshape(s, d)
    tgt = loss_target.reshape(s, d)

    nmod = w["mod_w"].shape[2]
    kin = w["w_in"].shape[2]
    inw = kin * N_CHIPS
    f2 = w["ffn_up"].shape[2] * N_CHIPS
    f = f2 // 2

    def wgroups(l):
        return [[w["w_in"][l].astype(BF16), w["conv_w"][l], w["ffn_conv_w"][l]], [w["w_out"][l].astype(BF16)],
                [w["ffn_up"][l].astype(BF16)], [w["ffn_down"][l].astype(BF16)]]

    gstates = {}
    (gstates[0, 0],), gtoken = _exchange_start(wgroups(0)[:1], False, c, "gather_start_in_0")
    mod4, sc_all = _mod_forward(c + gtoken[0:1, 0:1], w["mod_w"], w["mod_b"].reshape(nl, N_CHIPS, 1, nmod))
    mod = mod4[:, :, 0, :].reshape(nl, N_MOD, 1, d)

    tril = jnp.tril(jnp.ones((CHUNK, CHUNK), bool))
    bd = _block_diag(jnp.ones((heads, HEAD_DIM, HEAD_DIM), BF16))

    def mixer_params(l, conv_w):
        wm = jnp.where(tril[None], w["sgu_w"][l], 0.0)
        pw = _block_diag(w["pool_w"][l])
        return dict(
            bd=bd, ng=w["sgu_norm_g"][l][None], nb=w["sgu_norm_b"][l][None],
            wm=wm.astype(BF16), wmt=jnp.swapaxes(wm, 1, 2).astype(BF16),
            bias=jnp.repeat(w["sgu_b"][l].T, HEAD_DIM, axis=1),
            cw=conv_w, cb=w["conv_b"][l][None], cng=w["conv_norm_g"][l][None], cnb=w["conv_norm_b"][l][None],
            pw=pw.astype(BF16), pwt=pw.T.astype(BF16), ps=w["pool_scale"][l][None], bg=w["branch_g"][l][None])

    saved = []
    xl = x0
    arrived = {0: _exchange_wait(gstates[0, 0], False, mod4, "gather_wait_in_0")}
    for l in range(nl):
        sh1, sc1, g1, sh2, sc2, g2 = [mod[l, k] for k in range(N_MOD)]
        gpre1, gpost1 = w["mix_pre_g"][l][None], w["mix_post_g"][l][None]
        gpre2, gpost2 = w["ffn_pre_g"][l][None], w["ffn_post_g"][l][None]
        fcb = w["ffn_conv_b"][l][None]
        sh1_after, bg_after, g1_after, sh2_after, fcb_after = sh1, w["branch_g"][l][None], g1, sh2, fcb
        g_win, g_cw, g_fcw = arrived[l][:3]
        if l == 0:
            (gstates[0, 1], gstates[0, 2]), tok = _exchange_start(wgroups(0)[1:3], False, g_win, "gather_start_out_0")
            sh1_after = sh1 + tok[0:1, 0:1]
        w_in = jnp.transpose(g_win, (1, 0, 2)).reshape(d, inw)
        conv_w = jnp.transpose(g_cw, (1, 0, 2)).reshape(CONV_K, CONV_WIDTH)
        ffn_cw = jnp.transpose(g_fcw, (1, 0, 2)).reshape(FFN_CONV_K, f2)
        mp = mixer_params(l, conv_w)
        z, h1 = _norm_mod_matmul(xl, gpre1, 1.0 + sc1, sh1_after, w_in[None], f"mix_in_{l}")
        if l == 0:
            (g_wout,) = _exchange_wait(gstates[0, 1], False, z, "gather_wait_out_0")
        else:
            g_wout = arrived[l][3]
        w_out = g_wout.reshape(d, d)
        ycat, cbo, dpool = _mixer_fwd(z, dict(mp, bg=bg_after), f"mixer_fwd_{l}")
        (up,) = _exchange_wait(gstates[l, 2], False, ycat, f"gather_wait_up_{l}")
        if l == 0:
            (gstates[0, 3],), tok = _exchange_start(wgroups(0)[3:4], False, up, "gather_start_down_0")
            g1_after = g1 + tok[0:1, 0:1]
        o, x1 = _matmul_norm_resid(ycat, w_out, xl, g1_after, gpost1, f"mix_out_{l}")
        (g_down,) = _exchange_wait(gstates[l, 3], False, x1, f"gather_wait_down_{l}")
        down = g_down.reshape(f, d)
        if l + 1 < nl:
            (gstates[l + 1, 0], gstates[l + 1, 1]), tok = _exchange_start(wgroups(l + 1)[0:2], False, g_down,
                                                                        f"gather_start_in_{l + 1}")
            sh2_after = sh2 + tok[0:1, 0:1]
        p, h2 = _norm_mod_matmul(x1, gpre2, 1.0 + sc2, sh2_after, up, f"ffn_in_{l}")
        if l + 1 < nl:
            nxt = _exchange_wait(gstates[l + 1, 0], False, p, f"gather_wait_in_{l + 1}")
            nxt_out = _exchange_wait(gstates[l + 1, 1], False, nxt[0], f"gather_wait_out_{l + 1}")
            arrived[l + 1] = list(nxt) + list(nxt_out)
            (gstates[l + 1, 2], gstates[l + 1, 3]), tok = _exchange_start(wgroups(l + 1)[2:4], False, nxt_out[0],
                                                                        f"gather_start_up_{l + 1}")
            fcb_after = fcb + tok[0:1, 0:1]
        act = _ffn_act_fwd(p, ffn_cw, fcb_after, f"ffn_act_{l}")
        qo, x2 = _matmul_norm_resid(act, down, x1, g2, gpost2, f"ffn_out_{l}")
        saved.append(dict(x=xl, z=z, h1=h1, ycat=ycat, cbo=cbo, dpool=dpool, o=o, x1=x1, p=p, h2=h2, act=act, qo=qo, mp=mp, fcb=fcb,
                          w_in=w_in, w_out=w_out, up=up, down=down, ffn_cw=ffn_cw,
                          mods=(sh1, sc1, g1, sh2, sc2, g2), gains=(gpre1, gpost1, gpre2, gpost2)))
        xl = x2

    dx, loss_row = _loss_head(xl, tgt)

    small = {n: [None] * nl for n in SMALL_NAMES + SHARDED_SMALL}
    dmods = [None] * nl
    tn = f2 // N_CHIPS
    sstates = {}
    token = None
    for l in reversed(range(nl)):
        sv = saved[l]
        sh1, sc1, g1, sh2, sc2, g2 = sv["mods"]
        gpre1, gpost1, gpre2, gpost2 = sv["gains"]
        if token is not None:
            g2 = g2 + token[0:1, 0:1]
        dq, dact, dg2, dgpost2 = _resid_bwd_matmul(dx, sv["qo"], g2, gpost2, sv["down"], f"ffn_out_bwd_{l}")
        g_down = _wgrad(sv["act"], dq, (f, lambda j: 0), (d, lambda j: 0), jax.ShapeDtypeStruct((f, d), BF16),
                        (1, lambda j: (0, 0)), (f, d), f"wgrad_ffn_down_{l}")
        dug, duv, dfwg, dfwv, dfbg, dfbv = _ffn_act_bwd(sv["p"], dact, sv["ffn_cw"], sv["fcb"], f"ffn_act_bwd_{l}")
        dx1, dp, dsh2, dsc2, dgpre2 = _ffn_in_bwd(dug, duv, sv["ffn_cw"], sv["up"], sv["x1"], dx, gpre2, 1.0 + sc2,
                                                  f"ffn_in_bwd_{l}")
        g_up = _wgrad(sv["h2"], dp, (d, lambda j: 0), (tn, lambda j: j), jax.ShapeDtypeStruct((N_CHIPS, d, tn), BF16),
                      (N_CHIPS, lambda j: (j, 0, 0)), (None, d, tn), f"wgrad_ffn_up_{l}")
        (sstates[l, 0],), token = _exchange_start([[g_down.reshape(N_CHIPS, f // N_CHIPS, d), g_up]], True, g_up,
                                                  f"scatter_start_ffn_{l}")
        do, dycat, dg1, dgpost1 = _resid_bwd_matmul(dx1, sv["o"], g1 + token[0:1, 0:1], gpost1, sv["w_out"],
                                                    f"mix_out_bwd_{l}")
        g_out = _wgrad(sv["ycat"], do, (d, lambda j: 0), (d, lambda j: 0), jax.ShapeDtypeStruct((d, d), BF16),
                       (1, lambda j: (0, 0)), (d, d), f"wgrad_w_out_{l}")
        (sstates[l, 1],), token = _exchange_start([[g_out.reshape(N_CHIPS, d // N_CHIPS, d)]], True, g_out,
                                                  f"scatter_start_out_{l}")
        mp_after = dict(sv["mp"], bg=sv["mp"]["bg"] + token[0:1, 0:1])
        (dza, dcb, dd, dbg, dwm, dbias, dng, dnb, dcng, dcnb, dps, dpw) = _mixer_bwd_a(sv["z"], sv["cbo"], sv["dpool"], dycat, mp_after, f"mixer_bwd_a_{l}")
        dx, dz, dsh1, dsc1, dgpre1, dcw, dcbias = _mixer_bwd_b(
            sv["z"], dza, dcb, dd, sv["x"], dx1, gpre1, 1.0 + sc1, sv["mp"]["cw"], sv["w_in"], f"mixer_bwd_b_{l}")
        g_in = _wgrad(sv["h1"], dz, (d, lambda j: 0), (inw, lambda j: 0), jax.ShapeDtypeStruct((d, inw), BF16),
                      (1, lambda j: (0, 0)), (d, inw), f"wgrad_w_in_{l}")
        g_in_parts = jnp.transpose(g_in.reshape(d, N_CHIPS, kin), (1, 0, 2))
        if l > 0:
            (sstates[l, 2],), token = _exchange_start([[g_in_parts]], True, g_in_parts, f"scatter_start_in_{l}")

        dmods[l] = jnp.concatenate([dsh1, dsc1, dg1, dsh2, dsc2, dg2], axis=0)
        small["mix_pre_g"][l], small["mix_post_g"][l] = dgpre1[0], dgpost1[0]
        small["ffn_pre_g"][l], small["ffn_post_g"][l] = dgpre2[0], dgpost2[0]
        small["sgu_norm_g"][l], small["sgu_norm_b"][l] = dng[0], dnb[0]
        small["sgu_w"][l] = jnp.where(tril[None], dwm, 0.0)
        small["sgu_b"][l] = dbias.reshape(CHUNK, heads, HEAD_DIM).sum(-1).T
        small["conv_b"][l], small["conv_norm_g"][l], small["conv_norm_b"][l] = dcbias[0], dcng[0], dcnb[0]
        small["pool_w"][l], small["pool_scale"][l], small["branch_g"][l] = _diag_blocks(dpw, groups), dps[0], dbg[0]
        small["ffn_conv_b"][l] = jnp.concatenate([dfbg[0], dfbv[0]])
        small["conv_w"][l] = dcw
        small["ffn_conv_w"][l] = jnp.concatenate([dfwg, dfwv], axis=1)

    names = [n for n in SMALL_NAMES if n != "mod_b"] + list(SHARDED_SMALL)
    dmod_rows, _ = _pack([jnp.stack(dmods)])
    packed, layout = _pack([jnp.stack(dmods), loss_row] + [jnp.stack(small[n]) for n in names], 8 * N_DEV)
    gathered, summed = _allreduce_small(dmod_rows, packed)
    (sstates[0, 2],), token = _exchange_start([[g_in_parts]], True, summed, "scatter_start_in_0")
    parts = _unpack(summed, layout)
    loss = parts[1][0, 0]
    gsmall = dict(zip(names, parts[2:]))
    gsmall["mod_b"] = parts[0].reshape(nl, N_MOD * d)
    dmod_all = gathered[:, :nl * N_MOD].reshape(N_DEV, nl, N_MOD * d)
    dmod_mine = jnp.transpose(lax.dynamic_slice_in_dim(dmod_all, q * nmod, nmod, axis=2), (1, 0, 2))

    grads, deltas, new_m, new_v = {}, {}, {}, {}

    def put(name, res):
        grads[name], deltas[name], new_m[name], new_v[name] = res

    put("mod_w", _modw_adamw(sc_all.T + token[0:1, 0:1], dmod_mine, w["mod_w"], m["mod_w"], v["mod_w"], "adamw_mod_w"))

    pw_, lay = _pack([w[n] for n in SMALL_NAMES])
    pm_, _ = _pack([m[n] for n in SMALL_NAMES])
    pv_, _ = _pack([v[n] for n in SMALL_NAMES])
    pg_, _ = _pack([gsmall[n] for n in SMALL_NAMES])
    res = _adamw(pw_, pm_, pv_, [pg_], "adamw_small")
    for n, g_, d_, m_, v_ in zip(SMALL_NAMES, *[_unpack(r_, lay) for r_ in res]):
        put(n, (g_, d_, m_, v_))

    gsh = {"conv_w": lax.dynamic_slice_in_dim(gsmall["conv_w"], q * (CONV_WIDTH // N_CHIPS), CONV_WIDTH // N_CHIPS, axis=2),
           "ffn_conv_w": lax.dynamic_slice_in_dim(gsmall["ffn_conv_w"], q * (f2 // N_CHIPS), f2 // N_CHIPS, axis=2)}
    pw_, lay = _pack([w[n] for n in SHARDED_SMALL])
    pm_, _ = _pack([m[n] for n in SHARDED_SMALL])
    pv_, _ = _pack([v[n] for n in SHARDED_SMALL])
    pg_, _ = _pack([gsh[n] for n in SHARDED_SMALL])
    res = _adamw(pw_, pm_, pv_, [pg_], "adamw_sharded_small")
    for n, g_, d_, m_, v_ in zip(SHARDED_SMALL, *[_unpack(r_, lay) for r_ in res]):
        put(n, (g_, d_, m_, v_))

    recv = dict(w_in=[None] * nl, w_out=[None] * nl, ffn_up=[None] * nl, ffn_down=[None] * nl)
    done = grads["mod_w"]
    for l in reversed(range(nl)):
        recv["ffn_down"][l], recv["ffn_up"][l] = _exchange_wait(sstates[l, 0], True, done, f"scatter_wait_ffn_{l}")
        (recv["w_out"][l],) = _exchange_wait(sstates[l, 1], True, recv["ffn_up"][l], f"scatter_wait_out_{l}")
        (recv["w_in"][l],) = _exchange_wait(sstates[l, 2], True, recv["w_out"][l], f"scatter_wait_in_{l}")
        done = recv["w_in"][l]
    big = ("w_in", "w_out", "ffn_up", "ffn_down")
    mine = [_reduce4(recv[n], f"reduce4_{n}") for n in big]
    theirs = _swap_with_sibling(mine)
    for n, a, b in zip(big, mine, theirs):
        put(n, _adamw(w[n], m[n], v[n], [a, b], f"adamw_{n}"))

    return (loss, dx.reshape(1, s, d), *[grads[n] for n in WEIGHT_ORDER], *[deltas[n] for n in WEIGHT_ORDER],
            *[new_m[n] for n in WEIGHT_ORDER], *[new_v[n] for n in WEIGHT_ORDER])


def kernel(x, c, mod_w, mod_b, mix_pre_g, mix_post_g, w_in, sgu_norm_g, sgu_norm_b, sgu_w, sgu_b, conv_w, conv_b, conv_norm_g, conv_norm_b, pool_w, pool_scale, branch_g, w_out, ffn_pre_g, ffn_post_g, ffn_up, ffn_conv_w, ffn_conv_b, ffn_down, loss_target, m_mod_w, m_mod_b, m_mix_pre_g, m_mix_post_g, m_w_in, m_sgu_norm_g, m_sgu_norm_b, m_sgu_w, m_sgu_b, m_conv_w, m_conv_b, m_conv_norm_g, m_conv_norm_b, m_pool_w, m_pool_scale, m_branch_g, m_w_out, m_ffn_pre_g, m_ffn_post_g, m_ffn_up, m_ffn_conv_w, m_ffn_conv_b, m_ffn_down, v_mod_w, v_mod_b, v_mix_pre_g, v_mix_post_g, v_w_in, v_sgu_norm_g, v_sgu_norm_b, v_sgu_w, v_sgu_b, v_conv_w, v_conv_b, v_conv_norm_g, v_conv_norm_b, v_pool_w, v_pool_scale, v_branch_g, v_w_out, v_ffn_pre_g, v_ffn_post_g, v_ffn_up, v_ffn_conv_w, v_ffn_conv_b, v_ffn_down):
    w = dict(mod_w=mod_w, mod_b=mod_b, mix_pre_g=mix_pre_g, mix_post_g=mix_post_g, w_in=w_in, sgu_norm_g=sgu_norm_g,
             sgu_norm_b=sgu_norm_b, sgu_w=sgu_w, sgu_b=sgu_b, conv_w=conv_w, conv_b=conv_b, conv_norm_g=conv_norm_g,
             conv_norm_b=conv_norm_b, pool_w=pool_w, pool_scale=pool_scale, branch_g=branch_g, w_out=w_out,
             ffn_pre_g=ffn_pre_g, ffn_post_g=ffn_post_g, ffn_up=ffn_up, ffn_conv_w=ffn_conv_w, ffn_conv_b=ffn_conv_b,
             ffn_down=ffn_down)
    m = dict(mod_w=m_mod_w, mod_b=m_mod_b, mix_pre_g=m_mix_pre_g, mix_post_g=m_mix_post_g, w_in=m_w_in,
             sgu_norm_g=m_sgu_norm_g, sgu_norm_b=m_sgu_norm_b, sgu_w=m_sgu_w, sgu_b=m_sgu_b, conv_w=m_conv_w,
             conv_b=m_conv_b, conv_norm_g=m_conv_norm_g, conv_norm_b=m_conv_norm_b, pool_w=m_pool_w,
             pool_scale=m_pool_scale, branch_g=m_branch_g, w_out=m_w_out, ffn_pre_g=m_ffn_pre_g, ffn_post_g=m_ffn_post_g,
             ffn_up=m_ffn_up, ffn_conv_w=m_ffn_conv_w, ffn_conv_b=m_ffn_conv_b, ffn_down=m_ffn_down)
    v = dict(mod_w=v_mod_w, mod_b=v_mod_b, mix_pre_g=v_mix_pre_g, mix_post_g=v_mix_post_g, w_in=v_w_in,
             sgu_norm_g=v_sgu_norm_g, sgu_norm_b=v_sgu_norm_b, sgu_w=v_sgu_w, sgu_b=v_sgu_b, conv_w=v_conv_w,
             conv_b=v_conv_b, conv_norm_g=v_conv_norm_g, conv_norm_b=v_conv_norm_b, pool_w=v_pool_w,
             pool_scale=v_pool_scale, branch_g=v_branch_g, w_out=v_w_out, ffn_pre_g=v_ffn_pre_g, ffn_post_g=v_ffn_post_g,
             ffn_up=v_ffn_up, ffn_conv_w=v_ffn_conv_w, ffn_conv_b=v_ffn_conv_b, ffn_down=v_ffn_down)
    return _step(x, c, loss_target, w, m, v)
```

```python
import functools
import math

import jax
import jax.numpy as jnp
from jax import lax
from jax.experimental import pallas as pl
from jax.experimental.pallas import tpu as pltpu

F32 = jnp.float32
BF16 = jnp.bfloat16
MESH = pl.DeviceIdType.MESH

EPS = 1e-6
HEAD_DIM = 64
CHUNK = 128
SGU_WIDTH = 384
CONV_WIDTH = 384
POOL_WIDTH = 256
POOL_WINDOWS = (2, 4, 8, 16)
CONV_K = 31
FFN_CONV_K = 3
N_MOD = 6
N_CHIPS = 4
N_DEV = 8

ADAM_LR = 0.001
ADAM_B1 = 0.9
ADAM_B2 = 0.999
ADAM_EPS = 1e-08
ADAM_WD = 0.01
ADAM_STEP = 10

MIX_HALO = 32
FFN_HALO = 8
FFN_ROWS = 16
FFN_UNROLL = 8
LANES = 128
CONV_ROWS = 32
SMALL_COLS = 1024
VMEM_BYTES_V7X = 64 * 1024 * 1024


def _vmem_limit(estimate_bytes):
    return int(min(max(estimate_bytes, 16 * 1024 * 1024), VMEM_BYTES_V7X - 8 * 1024 * 1024))


def _row_tile(s, want):
    return want if s % want == 0 else math.gcd(s, want)


def _rsum(v):
    return jnp.sum(v, axis=0, keepdims=True)


def _rmean(v):
    return jnp.mean(v, axis=-1, keepdims=True)


def _gelu(v):
    k = math.sqrt(2.0 / math.pi)
    return 0.5 * v * (1.0 + jnp.tanh(k * (v + 0.044715 * v * v * v)))


def _gelu_grad(v):
    k = math.sqrt(2.0 / math.pi)
    t = jnp.tanh(k * (v + 0.044715 * v * v * v))
    return 0.5 * (1.0 + t) + 0.5 * v * (1.0 - t * t) * (k * (1.0 + 3.0 * 0.044715 * v * v))


def _dot(a, b):
    return jnp.dot(a, b, preferred_element_type=F32)


def _dot_nt(a, b):
    return lax.dot_general(a, b, (((1,), (1,)), ((), ())), preferred_element_type=F32)


def _dot_tn(a, b):
    return lax.dot_general(a, b, (((0,), (0,)), ((), ())), preferred_element_type=F32)


def _group_mean(v, bd):
    hi = v.astype(BF16)
    lo = (v - hi.astype(F32)).astype(BF16)
    return (_dot(hi, bd) + _dot(lo, bd)) * (1.0 / HEAD_DIM)


def _const_spec(shape):
    nd = len(shape)
    return pl.BlockSpec(shape, lambda *_: (0,) * nd)


def _hbm_call(body, **kw):
    call = pl.pallas_call(body, **kw)
    return lambda *args: call(*[pltpu.with_memory_space_constraint(a, pltpu.HBM) for a in args])


def _norm_mod_matmul(x, gain, s1p, shift, w, name):
    s, d = x.shape
    nb, _, tn = w.shape
    ts = _row_tile(s, 512 if nb * tn <= 2048 else 256)

    def body(x_ref, g_ref, s_ref, b_ref, w_ref, z_ref, h_ref):
        xv = x_ref[...]
        r = lax.rsqrt(_rmean(xv * xv) + EPS)
        h = ((xv * r) * g_ref[...] * s_ref[...] + b_ref[...]).astype(BF16)
        h_ref[...] = h
        for j in range(nb):
            z_ref[:, j * tn:(j + 1) * tn] = _dot(h, w_ref[j])

    vec = pl.BlockSpec((1, d), lambda i: (0, 0))
    return _hbm_call(
        body, name=name,
        grid=(s // ts,),
        in_specs=[pl.BlockSpec((ts, d), lambda i: (i, 0)), vec, vec, vec,
                  pl.BlockSpec((nb, d, tn), lambda i: (0, 0, 0), pipeline_mode=pl.Buffered(1))],
        out_specs=[pl.BlockSpec((ts, nb * tn), lambda i: (i, 0)), pl.BlockSpec((ts, d), lambda i: (i, 0))],
        out_shape=[jax.ShapeDtypeStruct((s, nb * tn), F32), jax.ShapeDtypeStruct((s, d), BF16)],
        compiler_params=pltpu.CompilerParams(
            dimension_semantics=("arbitrary",),
            vmem_limit_bytes=_vmem_limit(2 * (ts * d * 4 + ts * nb * tn * 4 + ts * d * 2) + nb * d * tn * 2 + 4 * ts * d * 4)),
    )(x, gain, s1p, shift, w)


def _matmul_norm_resid(a, w, xres, gate, gpost, name):
    s, k = a.shape
    d = w.shape[1]
    ts = _row_tile(s, 512)

    def body(a_ref, w_ref, x_ref, gate_ref, gp_ref, o_ref, xn_ref):
        o = _dot(a_ref[...], w_ref[...])
        o_ref[...] = o
        r = lax.rsqrt(_rmean(o * o) + EPS)
        xn_ref[...] = x_ref[...] + gate_ref[...] * ((o * r) * gp_ref[...])

    vec = pl.BlockSpec((1, d), lambda i: (0, 0))
    row = pl.BlockSpec((ts, d), lambda i: (i, 0))
    return _hbm_call(
        body, name=name,
        grid=(s // ts,),
        in_specs=[pl.BlockSpec((ts, k), lambda i: (i, 0)),
                  pl.BlockSpec((k, d), lambda i: (0, 0), pipeline_mode=pl.Buffered(1)), row, vec, vec],
        out_specs=[row, row],
        out_shape=[jax.ShapeDtypeStruct((s, d), F32)] * 2,
        compiler_params=pltpu.CompilerParams(
            dimension_semantics=("arbitrary",),
            vmem_limit_bytes=_vmem_limit(2 * (ts * k * 2 + 3 * ts * d * 4) + k * d * 2 + 4 * ts * d * 4)),
    )(a, w, xres, gate, gpost)


def _wgrad(a, b, acols, bcols, out_struct, out_index, out_block, name):
    s = a.shape[0]
    ts = _row_tile(s, 1024)
    aw, afn = acols
    bw, bfn = bcols
    nj = out_index[0]
    oidx = out_index[1]

    def body(a_ref, b_ref, o_ref, acc):
        i = pl.program_id(1)

        @pl.when(i == 0)
        def _():
            acc[...] = jnp.zeros_like(acc)

        acc[...] += _dot_tn(a_ref[...], b_ref[...])

        @pl.when(i == pl.num_programs(1) - 1)
        def _():
            o_ref[...] = acc[...].astype(o_ref.dtype)

    return _hbm_call(
        body, name=name,
        grid=(nj, s // ts),
        in_specs=[pl.BlockSpec((ts, aw), lambda j, i: (i, afn(j))), pl.BlockSpec((ts, bw), lambda j, i: (i, bfn(j)))],
        out_specs=pl.BlockSpec(out_block, lambda j, i: oidx(j)),
        out_shape=out_struct,
        scratch_shapes=[pltpu.VMEM((aw, bw), F32)],
        compiler_params=pltpu.CompilerParams(
            dimension_semantics=("arbitrary", "arbitrary"),
            vmem_limit_bytes=_vmem_limit(2 * (ts * aw * 2 + ts * bw * 2) + 3 * aw * bw * 4 + ts * aw * 4)),
    )(a, b)


def _loss_head(xo, tgt):
    s, d = xo.shape
    ts = _row_tile(s, 512)

    def body(x_ref, t_ref, dx_ref, l_ref, acc):
        i = pl.program_id(0)

        @pl.when(i == 0)
        def _():
            acc[...] = jnp.zeros_like(acc)

        e = x_ref[...] - t_ref[...]
        dx_ref[...] = e * (1.0 / d)
        acc[...] += _rsum(e * e)

        @pl.when(i == pl.num_programs(0) - 1)
        def _():
            tot = jnp.sum(acc[...], axis=-1, keepdims=True) * (0.5 / d)
            l_ref[...] = jnp.broadcast_to(tot, l_ref.shape)

    row = pl.BlockSpec((ts, d), lambda i: (i, 0))
    return _hbm_call(
        body, name="loss_head",
        grid=(s // ts,),
        in_specs=[row, row],
        out_specs=[row, pl.BlockSpec((1, SMALL_COLS), lambda i: (0, 0))],
        out_shape=[jax.ShapeDtypeStruct((s, d), F32), jax.ShapeDtypeStruct((1, SMALL_COLS), F32)],
        scratch_shapes=[pltpu.VMEM((1, d), F32)],
        compiler_params=pltpu.CompilerParams(dimension_semantics=("arbitrary",)),
    )(xo, tgt)


def _resid_bwd_matmul(dxn, o, gate, gpost, w, name):
    s, d = dxn.shape
    k = w.shape[0]
    ts = _row_tile(s, 512)

    def body(dx_ref, o_ref, gate_ref, gp_ref, w_ref, do_ref, da_ref, dgate_ref, dgp_ref):
        i = pl.program_id(0)

        @pl.when(i == 0)
        def _():
            dgate_ref[...] = jnp.zeros_like(dgate_ref)
            dgp_ref[...] = jnp.zeros_like(dgp_ref)

        dx = dx_ref[...]
        o = o_ref[...]
        r = lax.rsqrt(_rmean(o * o) + EPS)
        on = o * r
        dgate_ref[...] += _rsum(dx * (on * gp_ref[...]))
        don = dx * gate_ref[...]
        dgp_ref[...] += _rsum(don * on)
        t = don * gp_ref[...]
        do = (r * (t - on * _rmean(t * on))).astype(BF16)
        do_ref[...] = do
        da_ref[...] = _dot_nt(do, w_ref[...])

    vec = pl.BlockSpec((1, d), lambda i: (0, 0))
    row = pl.BlockSpec((ts, d), lambda i: (i, 0))
    return _hbm_call(
        body, name=name,
        grid=(s // ts,),
        in_specs=[row, row, vec, vec, pl.BlockSpec((k, d), lambda i: (0, 0), pipeline_mode=pl.Buffered(1))],
        out_specs=[row, pl.BlockSpec((ts, k), lambda i: (i, 0)), vec, vec],
        out_shape=[jax.ShapeDtypeStruct((s, d), BF16), jax.ShapeDtypeStruct((s, k), F32),
                   jax.ShapeDtypeStruct((1, d), F32), jax.ShapeDtypeStruct((1, d), F32)],
        compiler_params=pltpu.CompilerParams(
            dimension_semantics=("arbitrary",),
            vmem_limit_bytes=_vmem_limit(2 * (2 * ts * d * 4 + ts * d * 2 + ts * k * 4) + d * k * 2 + 6 * ts * d * 4)),
    )(dxn, o, gate, gpost, w)


def _norm_mod_bwd(dh, xv, gain, s1p, dres):
    r = lax.rsqrt(_rmean(xv * xv) + EPS)
    xn = xv * r
    dshift = _rsum(dh)
    t = dh * xn
    dscale = _rsum(t * gain)
    dgain = _rsum(t * s1p)
    dxn = dh * (gain * s1p)
    dx = r * (dxn - xn * _rmean(dxn * xn)) + dres
    return dx, dshift, dscale, dgain


def _lane_lt(shape, bound):
    return lax.broadcasted_iota(jnp.int32, shape, 1) < bound


def _sgu_forward(z_ref, bd_ref, ng_ref, nb_ref, wm_ref, bias_ref, ts, ya_s, f_s):
    u = _gelu(z_ref[:, 0:SGU_WIDTH])
    v = _gelu(z_ref[:, SGU_WIDTH:2 * SGU_WIDTH])
    bd = bd_ref[...]
    vc = v - _group_mean(v, bd)
    rstd = lax.rsqrt(_group_mean(vc * vc, bd) + EPS)
    vhat = vc * rstd
    vn = (vhat * ng_ref[...] + nb_ref[...]).astype(BF16)
    left = _lane_lt((CHUNK, CHUNK), HEAD_DIM)
    for n in range(ts // CHUNK):
        rows = slice(n * CHUNK, (n + 1) * CHUNK)
        for p in range(SGU_WIDTH // CHUNK):
            cols = slice(p * CHUNK, (p + 1) * CHUNK)
            blk = vn[rows, cols]
            f = jnp.where(left, _dot(wm_ref[2 * p], blk), _dot(wm_ref[2 * p + 1], blk)) + bias_ref[:, cols]
            if f_s is not None:
                f_s[rows, cols] = f
            ya_s[rows, cols] = u[rows, cols] * f
    return u, vhat, rstd, vn


def _conv31_forward(z_ref, zh_ref, first, cw_ref, cb_ref, ts, ext_b, cbs):
    a = z_ref[:, 2 * SGU_WIDTH:2 * SGU_WIDTH + CONV_WIDTH]
    g = z_ref[:, 2 * SGU_WIDTH + CONV_WIDTH:2 * SGU_WIDTH + 2 * CONV_WIDTH]
    ah = zh_ref[:, 2 * SGU_WIDTH:2 * SGU_WIDTH + CONV_WIDTH]
    gh = zh_ref[:, 2 * SGU_WIDTH + CONV_WIDTH:2 * SGU_WIDTH + 2 * CONV_WIDTH]
    ext_b[pl.ds(0, MIX_HALO), :] = jnp.where(first, 0.0, ah * jax.nn.sigmoid(gh))
    ext_b[pl.ds(MIX_HALO, ts), :] = a * jax.nn.sigmoid(g)
    for r in range(ts // CONV_ROWS):
        acc = jnp.broadcast_to(cb_ref[...], (CONV_ROWS, CONV_WIDTH))
        for k in range(CONV_K):
            acc = acc + cw_ref[k:k + 1, :] * ext_b[pl.ds(MIX_HALO - (CONV_K - 1) + k + r * CONV_ROWS, CONV_ROWS), :]
        cbs[pl.ds(r * CONV_ROWS, CONV_ROWS), :] = acc


def _pool_counts(i, ts):
    pos1 = (i * ts + 1 + lax.broadcasted_iota(jnp.int32, (ts, POOL_WIDTH), 0)).astype(F32)
    lane = lax.broadcasted_iota(jnp.int32, (ts, POOL_WIDTH), 1)
    gdim = POOL_WIDTH // len(POOL_WINDOWS)
    win = jnp.where(lane < gdim, float(POOL_WINDOWS[0]),
                    jnp.where(lane < 2 * gdim, float(POOL_WINDOWS[1]),
                              jnp.where(lane < 3 * gdim, float(POOL_WINDOWS[2]), float(POOL_WINDOWS[3]))))
    return jnp.minimum(pos1, win)


def _window_sums(ext, base, ts, sign):
    lane = lax.broadcasted_iota(jnp.int32, (ts, POOL_WIDTH), 1)
    gdim = POOL_WIDTH // len(POOL_WINDOWS)
    run = jnp.zeros((ts, POOL_WIDTH), F32)
    out = jnp.zeros((ts, POOL_WIDTH), F32)
    for m in range(POOL_WINDOWS[-1]):
        run = run + ext[pl.ds(base + sign * m, ts), :]
        for gi, win in enumerate(POOL_WINDOWS):
            if m == win - 1:
                out = jnp.where((lane >= gi * gdim) & (lane < (gi + 1) * gdim), run, out)
    return out


def _pool_forward(z_ref, zh_ref, first, i, ts, ext_c):
    c0 = 2 * SGU_WIDTH + 2 * CONV_WIDTH
    zc = z_ref[:, c0:c0 + POOL_WIDTH]
    ext_c[pl.ds(0, MIX_HALO), :] = jnp.where(first, 0.0, zh_ref[:, c0:c0 + POOL_WIDTH])
    ext_c[pl.ds(MIX_HALO, ts), :] = zc
    sums = _window_sums(ext_c, MIX_HALO, ts, -1)
    return sums / _pool_counts(i, ts) - zc


def _layer_norm_rows(v):
    mu = _rmean(v)
    vc = v - mu
    rstd = lax.rsqrt(_rmean(vc * vc) + EPS)
    return vc * rstd, rstd


def _mixer_specs(s, ts, width):
    nbh = ts // MIX_HALO
    tile = pl.BlockSpec((ts, width), lambda i: (i, 0))
    prev = pl.BlockSpec((MIX_HALO, width), lambda i: (jnp.maximum(i * nbh - 1, 0), 0))
    nxt = pl.BlockSpec((MIX_HALO, width), lambda i: (jnp.minimum((i + 1) * nbh, s // MIX_HALO - 1), 0))
    return tile, prev, nxt


def _mixer_fwd(z, mp, name):
    s, inw = z.shape
    d = SGU_WIDTH + CONV_WIDTH + POOL_WIDTH
    ts = _row_tile(s, 256)

    def body(z_ref, zh_ref, bd_ref, ng_ref, nb_ref, wm_ref, bias_ref, cw_ref, cb_ref, cng_ref, cnb_ref,
             pw_ref, ps_ref, bg_ref, y_ref, cbs, dpool_ref, ya_s, ext_b, ext_c):
        i = pl.program_id(0)
        first = i == 0
        _sgu_forward(z_ref, bd_ref, ng_ref, nb_ref, wm_ref, bias_ref, ts, ya_s, None)
        ya = ya_s[...]
        ra = lax.rsqrt(_rmean(ya * ya) + EPS)
        y_ref[:, 0:SGU_WIDTH] = ((ya * ra) * bg_ref[:, 0:SGU_WIDTH]).astype(BF16)

        _conv31_forward(z_ref, zh_ref, first, cw_ref, cb_ref, ts, ext_b, cbs)
        chat, _ = _layer_norm_rows(cbs[...])
        lin = chat * cng_ref[...] + cnb_ref[...]
        yb = lin * jax.nn.sigmoid(lin)
        rb = lax.rsqrt(_rmean(yb * yb) + EPS)
        y_ref[:, SGU_WIDTH:SGU_WIDTH + CONV_WIDTH] = ((yb * rb) * bg_ref[:, SGU_WIDTH:SGU_WIDTH + CONV_WIDTH]).astype(BF16)

        dpool = _pool_forward(z_ref, zh_ref, first, i, ts, ext_c)
        dpool_ref[...] = dpool
        yc = _dot(dpool.astype(BF16), pw_ref[...]) * ps_ref[...]
        rc = lax.rsqrt(_rmean(yc * yc) + EPS)
        y_ref[:, SGU_WIDTH + CONV_WIDTH:d] = ((yc * rc) * bg_ref[:, SGU_WIDTH + CONV_WIDTH:d]).astype(BF16)

    tile, prev, _ = _mixer_specs(s, ts, inw)
    consts = [mp["bd"], mp["ng"], mp["nb"], mp["wm"], mp["bias"], mp["cw"], mp["cb"], mp["cng"], mp["cnb"],
              mp["pw"], mp["ps"], mp["bg"]]
    return _hbm_call(
        body, name=name,
        grid=(s // ts,),
        in_specs=[tile, prev] + [_const_spec(c.shape) for c in consts],
        out_specs=[pl.BlockSpec((ts, d), lambda i: (i, 0)), pl.BlockSpec((ts, CONV_WIDTH), lambda i: (i, 0)),
                   pl.BlockSpec((ts, POOL_WIDTH), lambda i: (i, 0))],
        out_shape=[jax.ShapeDtypeStruct((s, d), BF16), jax.ShapeDtypeStruct((s, CONV_WIDTH), F32),
                   jax.ShapeDtypeStruct((s, POOL_WIDTH), F32)],
        scratch_shapes=[pltpu.VMEM((ts, SGU_WIDTH), F32), pltpu.VMEM((ts + MIX_HALO, CONV_WIDTH), F32),
                        pltpu.VMEM((ts + MIX_HALO, POOL_WIDTH), F32)],
        compiler_params=pltpu.CompilerParams(dimension_semantics=("arbitrary",),
                                             vmem_limit_bytes=_vmem_limit(16 * ts * inw * 4)),
    )(z, z, *consts)


def _mixer_bwd_a(z, cb, dpool, dy, mp, name):
    s, inw = z.shape
    d = SGU_WIDTH + CONV_WIDTH + POOL_WIDTH
    ts = _row_tile(s, 256)
    nchunk = ts // CHUNK

    def rms_bwd(dyn, y, g):
        r = lax.rsqrt(_rmean(y * y) + EPS)
        yn = y * r
        dg = _rsum(dyn * yn)
        t = dyn * g
        return r * (t - yn * _rmean(t * yn)), dg

    def body(z_ref, cbs, dpool_ref, dy_ref, bd_ref, ng_ref, nb_ref, wm_ref, wmt_ref, bias_ref, cng_ref, cnb_ref,
             pw_ref, pwt_ref, ps_ref, bg_ref,
             dza_ref, dcb_ref, dd_ref, dbg_ref, dwm_ref, dbias_ref, dng_ref, dnb_ref, dcng_ref, dcnb_ref, dps_ref, dpw_ref,
             ya_s, f_s, dvn_s):
        i = pl.program_id(0)
        first = i == 0

        @pl.when(first)
        def _():
            for ref in (dwm_ref, dbias_ref, dng_ref, dnb_ref, dcng_ref, dcnb_ref, dps_ref, dpw_ref):
                ref[...] = jnp.zeros_like(ref)

        u, vhat, rstd, vn = _sgu_forward(z_ref, bd_ref, ng_ref, nb_ref, wm_ref, bias_ref, ts, ya_s, f_s)
        dya, dbg_a = rms_bwd(dy_ref[:, 0:SGU_WIDTH], ya_s[...], bg_ref[:, 0:SGU_WIDTH])
        du = dya * f_s[...]
        df = dya * u
        dfb = df.astype(BF16)
        left = _lane_lt((CHUNK, CHUNK), HEAD_DIM)
        zero = jnp.zeros((CHUNK, CHUNK), BF16)
        dbias = jnp.zeros((CHUNK, SGU_WIDTH), F32)
        for n in range(nchunk):
            rows = slice(n * CHUNK, (n + 1) * CHUNK)
            dbias = dbias + df[rows, :]
            for p in range(SGU_WIDTH // CHUNK):
                cols = slice(p * CHUNK, (p + 1) * CHUNK)
                dblk = dfb[rows, cols]
                vblk = vn[rows, cols]
                dwm_ref[2 * p] += _dot_nt(jnp.where(left, dblk, zero), vblk)
                dwm_ref[2 * p + 1] += _dot_nt(jnp.where(left, zero, dblk), vblk)
                dvn_s[rows, cols] = jnp.where(left, _dot(wmt_ref[2 * p], dblk), _dot(wmt_ref[2 * p + 1], dblk))
        dbias_ref[...] += dbias
        dvn = dvn_s[...]
        dng_ref[...] += _rsum(dvn * vhat)
        dnb_ref[...] += _rsum(dvn)
        dvh = dvn * ng_ref[...]
        bd = bd_ref[...]
        dv = rstd * (dvh - _group_mean(dvh, bd) - vhat * _group_mean(dvh * vhat, bd))
        dza_ref[:, 0:SGU_WIDTH] = (du * _gelu_grad(z_ref[:, 0:SGU_WIDTH])).astype(BF16)
        dza_ref[:, SGU_WIDTH:2 * SGU_WIDTH] = (dv * _gelu_grad(z_ref[:, SGU_WIDTH:2 * SGU_WIDTH])).astype(BF16)

        chat, crstd = _layer_norm_rows(cbs[...])
        lin = chat * cng_ref[...] + cnb_ref[...]
        sl = jax.nn.sigmoid(lin)
        dyb, dbg_b = rms_bwd(dy_ref[:, SGU_WIDTH:SGU_WIDTH + CONV_WIDTH], lin * sl, bg_ref[:, SGU_WIDTH:SGU_WIDTH + CONV_WIDTH])
        dlin = dyb * (sl * (1.0 + lin * (1.0 - sl)))
        dcng_ref[...] += _rsum(dlin * chat)
        dcnb_ref[...] += _rsum(dlin)
        dch = dlin * cng_ref[...]
        dcb_ref[...] = crstd * (dch - _rmean(dch) - chat * _rmean(dch * chat))

        dpb = dpool_ref[...].astype(BF16)
        ycp = _dot(dpb, pw_ref[...])
        dyc, dbg_c = rms_bwd(dy_ref[:, SGU_WIDTH + CONV_WIDTH:d], ycp * ps_ref[...], bg_ref[:, SGU_WIDTH + CONV_WIDTH:d])
        dps_ref[...] += _rsum(dyc * ycp)
        dycp = (dyc * ps_ref[...]).astype(BF16)
        dpw_ref[...] += _dot_tn(dpb, dycp)
        dd_ref[...] = _dot(dycp, pwt_ref[...])

        @pl.when(first)
        def _():
            dbg_ref[...] = jnp.zeros_like(dbg_ref)

        dbg_ref[:, 0:SGU_WIDTH] += dbg_a
        dbg_ref[:, SGU_WIDTH:SGU_WIDTH + CONV_WIDTH] += dbg_b
        dbg_ref[:, SGU_WIDTH + CONV_WIDTH:d] += dbg_c

    tile, _, _ = _mixer_specs(s, ts, inw)
    consts = [mp["bd"], mp["ng"], mp["nb"], mp["wm"], mp["wmt"], mp["bias"], mp["cng"], mp["cnb"],
              mp["pw"], mp["pwt"], mp["ps"], mp["bg"]]
    acc_shapes = [(1, d), (2 * (SGU_WIDTH // CHUNK), CHUNK, CHUNK), (CHUNK, SGU_WIDTH), (1, SGU_WIDTH), (1, SGU_WIDTH),
                  (1, CONV_WIDTH), (1, CONV_WIDTH), (1, POOL_WIDTH), (POOL_WIDTH, POOL_WIDTH)]
    return _hbm_call(
        body, name=name,
        grid=(s // ts,),
        in_specs=[tile, pl.BlockSpec((ts, CONV_WIDTH), lambda i: (i, 0)), pl.BlockSpec((ts, POOL_WIDTH), lambda i: (i, 0)),
                  pl.BlockSpec((ts, d), lambda i: (i, 0))] + [_const_spec(c.shape) for c in consts],
        out_specs=[pl.BlockSpec((ts, 2 * SGU_WIDTH), lambda i: (i, 0)), pl.BlockSpec((ts, CONV_WIDTH), lambda i: (i, 0)),
                   pl.BlockSpec((ts, POOL_WIDTH), lambda i: (i, 0))] + [_const_spec(a) for a in acc_shapes],
        out_shape=[jax.ShapeDtypeStruct((s, 2 * SGU_WIDTH), BF16), jax.ShapeDtypeStruct((s, CONV_WIDTH), F32),
                   jax.ShapeDtypeStruct((s, POOL_WIDTH), F32)] + [jax.ShapeDtypeStruct(a, F32) for a in acc_shapes],
        scratch_shapes=[pltpu.VMEM((ts, SGU_WIDTH), F32), pltpu.VMEM((ts, SGU_WIDTH), F32), pltpu.VMEM((ts, SGU_WIDTH), F32)],
        compiler_params=pltpu.CompilerParams(dimension_semantics=("arbitrary",),
                                             vmem_limit_bytes=_vmem_limit(24 * ts * inw * 4)),
    )(z, cb, dpool, dy, *consts)


def _mixer_bwd_b(z, dza, dcb, dd, x, dres, gain, s1p, cw, w, name):
    s, inw = z.shape
    d = x.shape[1]
    ts = _row_tile(s, 256)
    c0 = 2 * SGU_WIDTH
    c1 = c0 + 2 * CONV_WIDTH

    def body(z_ref, zh_ref, dza_ref, dcb_ref, dcbn_ref, dd_ref, ddn_ref, x_ref, dres_ref, g_ref, s_ref, cw_ref, w_ref,
             dx_ref, dz_ref, dsh_ref, dsc_ref, dg_ref, dcw_ref, dcbias_ref, ext_b, ext_n, ext_e):
        i = pl.program_id(0)
        first = i == 0
        last = i == pl.num_programs(0) - 1

        @pl.when(first)
        def _():
            for ref in (dsh_ref, dsc_ref, dg_ref, dcw_ref, dcbias_ref):
                ref[...] = jnp.zeros_like(ref)

        a = z_ref[:, c0:c0 + CONV_WIDTH]
        sg = jax.nn.sigmoid(z_ref[:, c0 + CONV_WIDTH:c1])
        ah = zh_ref[:, c0:c0 + CONV_WIDTH]
        gh = zh_ref[:, c0 + CONV_WIDTH:c1]
        ext_b[pl.ds(0, MIX_HALO), :] = jnp.where(first, 0.0, ah * jax.nn.sigmoid(gh))
        ext_b[pl.ds(MIX_HALO, ts), :] = a * sg
        dcbv = dcb_ref[...]
        dcbias_ref[...] += _rsum(dcbv)
        for k in range(CONV_K):
            dcw_ref[k:k + 1, :] += _rsum(dcbv * ext_b[pl.ds(MIX_HALO - (CONV_K - 1) + k, ts), :])

        ext_n[pl.ds(0, ts), :] = dcbv
        ext_n[pl.ds(ts, MIX_HALO), :] = jnp.where(last, 0.0, dcbn_ref[...])
        for r in range(ts // CONV_ROWS):
            acc = jnp.zeros((CONV_ROWS, CONV_WIDTH), F32)
            for k in range(CONV_K):
                acc = acc + cw_ref[k:k + 1, :] * ext_n[pl.ds(CONV_K - 1 - k + r * CONV_ROWS, CONV_ROWS), :]
            rows = pl.ds(r * CONV_ROWS, CONV_ROWS)
            ar = z_ref[rows, c0:c0 + CONV_WIDTH]
            sr = jax.nn.sigmoid(z_ref[rows, c0 + CONV_WIDTH:c1])
            dz_ref[rows, c0:c0 + CONV_WIDTH] = (acc * sr).astype(BF16)
            dz_ref[rows, c0 + CONV_WIDTH:c1] = (acc * ar * sr * (1.0 - sr)).astype(BF16)

        ddv = dd_ref[...]
        ext_e[pl.ds(0, ts), :] = ddv / _pool_counts(i, ts)
        nh = (i + 1) * ts + lax.broadcasted_iota(jnp.int32, (MIX_HALO, POOL_WIDTH), 0)
        lane = lax.broadcasted_iota(jnp.int32, (MIX_HALO, POOL_WIDTH), 1)
        gdim = POOL_WIDTH // len(POOL_WINDOWS)
        winh = jnp.where(lane < gdim, float(POOL_WINDOWS[0]),
                         jnp.where(lane < 2 * gdim, float(POOL_WINDOWS[1]),
                                   jnp.where(lane < 3 * gdim, float(POOL_WINDOWS[2]), float(POOL_WINDOWS[3]))))
        cnth = jnp.minimum((nh + 1).astype(F32), winh)
        ext_e[pl.ds(ts, MIX_HALO), :] = jnp.where(last, 0.0, ddn_ref[...] / cnth)
        dz_ref[:, c1:inw] = (_window_sums(ext_e, 0, ts, 1) - ddv).astype(BF16)
        dz_ref[:, 0:c0] = dza_ref[...]

        dh = _dot_nt(dz_ref[...], w_ref[...])
        dx, dsh, dsc, dg = _norm_mod_bwd(dh, x_ref[...], g_ref[...], s_ref[...], dres_ref[...])
        dx_ref[...] = dx
        dsh_ref[...] += dsh
        dsc_ref[...] += dsc
        dg_ref[...] += dg

    tile, prev, _ = _mixer_specs(s, ts, inw)
    _, _, nxt_b = _mixer_specs(s, ts, CONV_WIDTH)
    _, _, nxt_c = _mixer_specs(s, ts, POOL_WIDTH)
    row = pl.BlockSpec((ts, d), lambda i: (i, 0))
    vec = pl.BlockSpec((1, d), lambda i: (0, 0))
    return _hbm_call(
        body, name=name,
        grid=(s // ts,),
        in_specs=[tile, prev, pl.BlockSpec((ts, c0), lambda i: (i, 0)),
                  pl.BlockSpec((ts, CONV_WIDTH), lambda i: (i, 0)), nxt_b,
                  pl.BlockSpec((ts, POOL_WIDTH), lambda i: (i, 0)), nxt_c,
                  row, row, vec, vec, _const_spec(cw.shape),
                  pl.BlockSpec(w.shape, lambda i: (0, 0), pipeline_mode=pl.Buffered(1))],
        out_specs=[row, pl.BlockSpec((ts, inw), lambda i: (i, 0)), vec, vec, vec,
                   _const_spec((CONV_K, CONV_WIDTH)), _const_spec((1, CONV_WIDTH))],
        out_shape=[jax.ShapeDtypeStruct((s, d), F32), jax.ShapeDtypeStruct((s, inw), BF16)]
        + [jax.ShapeDtypeStruct((1, d), F32)] * 3
        + [jax.ShapeDtypeStruct((CONV_K, CONV_WIDTH), F32), jax.ShapeDtypeStruct((1, CONV_WIDTH), F32)],
        scratch_shapes=[pltpu.VMEM((ts + MIX_HALO, CONV_WIDTH), F32), pltpu.VMEM((ts + MIX_HALO, CONV_WIDTH), F32),
                        pltpu.VMEM((ts + MIX_HALO, POOL_WIDTH), F32)],
        compiler_params=pltpu.CompilerParams(dimension_semantics=("arbitrary",),
                                             vmem_limit_bytes=_vmem_limit(16 * ts * inw * 4 + inw * d * 2)),
    )(z, z, dza, dcb, dcb, dd, dd, x, dres, gain, s1p, cw, w)


def _ffn_specs(s, ts, tc, half_blocks):
    nbh = ts // FFN_HALO

    def tile(off):
        return pl.BlockSpec((ts, tc), lambda j, i: (i, j + off))

    def prev(off):
        return pl.BlockSpec((FFN_HALO, tc), lambda j, i: (jnp.maximum(i * nbh - 1, 0), j + off))

    def vec(rows, off):
        return pl.BlockSpec((rows, tc), lambda j, i: (0, j + off))

    return tile, prev, vec


def _rows_before(cur, prev, k):
    row = lax.broadcasted_iota(jnp.int32, cur.shape, 0)
    return jnp.where(row >= k, pltpu.roll(cur, k, 0), pltpu.roll(prev, k, 0))


def _conv3_rows(cur, prev, w_ref, b_ref, cols):
    x1 = _rows_before(cur, prev, 1)
    x2 = _rows_before(cur, prev, 2)
    u = b_ref[:, cols] + w_ref[2:3, cols] * cur + w_ref[1:2, cols] * x1 + w_ref[0:1, cols] * x2
    return u, x2, x1


def _halo_chunk(h_ref, cols, first):
    h = jnp.where(first, 0.0, h_ref[:, cols])
    return jnp.concatenate([h] * (FFN_ROWS // FFN_HALO), axis=0)


def _ffn_act_fwd(p, cw, cb, name):
    s, f2 = p.shape
    f = f2 // 2
    tc = f // 2
    hb = f // tc
    ts = _row_tile(s, 256)

    def body(pg_ref, pgh_ref, pv_ref, pvh_ref, wg_ref, wv_ref, bg_ref, bv_ref, act_ref):
        first = pl.program_id(1) == 0
        for c in range(tc // LANES):
            cols = slice(c * LANES, (c + 1) * LANES)

            def chunk(r, carry, cols=cols):
                pg_prev, pv_prev = carry
                rows = pl.ds(pl.multiple_of(r * FFN_ROWS, FFN_ROWS), FFN_ROWS)
                pg = pg_ref[rows, cols]
                pv = pv_ref[rows, cols]
                ug, _, _ = _conv3_rows(pg, pg_prev, wg_ref, bg_ref, cols)
                uv, _, _ = _conv3_rows(pv, pv_prev, wv_ref, bv_ref, cols)
                act_ref[rows, cols] = (_gelu(ug) * uv).astype(BF16)
                return pg, pv

            def step(r, carry, chunk=chunk):
                for u in range(FFN_UNROLL):
                    carry = chunk(r * FFN_UNROLL + u, carry)
                return carry

            lax.fori_loop(0, ts // (FFN_ROWS * FFN_UNROLL), step,
                          (_halo_chunk(pgh_ref, cols, first), _halo_chunk(pvh_ref, cols, first)))

    tile, prev, vec = _ffn_specs(s, ts, tc, hb)
    return _hbm_call(
        body, name=name,
        grid=(hb, s // ts),
        in_specs=[tile(0), prev(0), tile(hb), prev(hb), vec(FFN_CONV_K, 0), vec(FFN_CONV_K, hb), vec(1, 0), vec(1, hb)],
        out_specs=pl.BlockSpec((ts, tc), lambda j, i: (i, j)),
        out_shape=jax.ShapeDtypeStruct((s, f), BF16),
        compiler_params=pltpu.CompilerParams(dimension_semantics=("arbitrary", "arbitrary"),
                                             vmem_limit_bytes=_vmem_limit(8 * ts * tc * 4)),
    )(p, p, p, p, cw, cw, cb, cb)


def _ffn_act_bwd(p, dact, cw, cb, name):
    s, f2 = p.shape
    f = f2 // 2
    tc = f // 2
    hb = f // tc
    ts = _row_tile(s, 256)

    def body(pg_ref, pgh_ref, pv_ref, pvh_ref, da_ref, wg_ref, wv_ref, bg_ref, bv_ref,
             dug_ref, duv_ref, dwg_ref, dwv_ref, dbg_ref, dbv_ref):
        first = pl.program_id(1) == 0

        @pl.when(first)
        def _():
            for ref in (dwg_ref, dwv_ref, dbg_ref, dbv_ref):
                ref[...] = jnp.zeros_like(ref)

        zero = jnp.zeros((FFN_ROWS, LANES), F32)
        for c in range(tc // LANES):
            cols = slice(c * LANES, (c + 1) * LANES)

            def chunk(r, carry, cols=cols):
                pg_prev, pv_prev, ag0, ag1, ag2, av0, av1, av2, sg, sv = carry
                rows = pl.ds(pl.multiple_of(r * FFN_ROWS, FFN_ROWS), FFN_ROWS)
                pg = pg_ref[rows, cols]
                pv = pv_ref[rows, cols]
                ug, pg2, pg1 = _conv3_rows(pg, pg_prev, wg_ref, bg_ref, cols)
                uv, pv2, pv1 = _conv3_rows(pv, pv_prev, wv_ref, bv_ref, cols)
                da = da_ref[rows, cols]
                dug = da * uv * _gelu_grad(ug)
                duv = da * _gelu(ug)
                dug_ref[rows, cols] = dug
                duv_ref[rows, cols] = duv
                return (pg, pv, ag0 + dug * pg2, ag1 + dug * pg1, ag2 + dug * pg,
                        av0 + duv * pv2, av1 + duv * pv1, av2 + duv * pv, sg + dug, sv + duv)

            def step(r, carry, chunk=chunk):
                for u in range(FFN_UNROLL):
                    carry = chunk(r * FFN_UNROLL + u, carry)
                return carry

            out = lax.fori_loop(0, ts // (FFN_ROWS * FFN_UNROLL), step,
                                (_halo_chunk(pgh_ref, cols, first), _halo_chunk(pvh_ref, cols, first)) + (zero,) * 8)
            for k in range(FFN_CONV_K):
                dwg_ref[k:k + 1, cols] += _rsum(out[2 + k])
                dwv_ref[k:k + 1, cols] += _rsum(out[5 + k])
            dbg_ref[:, cols] += _rsum(out[8])
            dbv_ref[:, cols] += _rsum(out[9])

    tile, prev, vec = _ffn_specs(s, ts, tc, hb)
    half = pl.BlockSpec((ts, tc), lambda j, i: (i, j))
    wacc = pl.BlockSpec((FFN_CONV_K, tc), lambda j, i: (0, j))
    bacc = pl.BlockSpec((1, tc), lambda j, i: (0, j))
    return _hbm_call(
        body, name=name,
        grid=(hb, s // ts),
        in_specs=[tile(0), prev(0), tile(hb), prev(hb), half, vec(FFN_CONV_K, 0), vec(FFN_CONV_K, hb), vec(1, 0), vec(1, hb)],
        out_specs=[half, half, wacc, wacc, bacc, bacc],
        out_shape=[jax.ShapeDtypeStruct((s, f), F32)] * 2 + [jax.ShapeDtypeStruct((FFN_CONV_K, f), F32)] * 2
        + [jax.ShapeDtypeStruct((1, f), F32)] * 2,
        compiler_params=pltpu.CompilerParams(dimension_semantics=("arbitrary", "arbitrary"),
                                             vmem_limit_bytes=_vmem_limit(12 * ts * tc * 4)),
    )(p, p, p, p, dact, cw, cw, cb, cb)


def _ffn_in_bwd(dug, duv, cw, w, x, dres, gain, s1p, name):
    s, f = dug.shape
    d = x.shape[1]
    ts = _row_tile(s, 256)
    tc = w.shape[2]
    assert f % tc == 0 and w.shape[0] * tc == 2 * f
    nbh = ts // FFN_HALO

    def body(dug_ref, dugn_ref, duv_ref, duvn_ref, cw_ref, w_ref, x_ref, dres_ref, g_ref, s_ref,
             dx_ref, dp_ref, dsh_ref, dsc_ref, dg_ref, ext):
        i = pl.program_id(0)
        last = i == pl.num_programs(0) - 1

        @pl.when(i == 0)
        def _():
            for ref in (dsh_ref, dsc_ref, dg_ref):
                ref[...] = jnp.zeros_like(ref)

        dh = jnp.zeros((ts, d), F32)
        for half, (t_ref, n_ref) in enumerate(((dug_ref, dugn_ref), (duv_ref, duvn_ref))):
            for cb in range(f // tc):
                cols = slice(cb * tc, (cb + 1) * tc)
                wcols = slice(half * f + cb * tc, half * f + (cb + 1) * tc)
                ext[pl.ds(0, ts), :] = t_ref[:, cols]
                ext[pl.ds(ts, FFN_HALO), :] = jnp.where(last, 0.0, n_ref[:, cols])
                acc = cw_ref[FFN_CONV_K - 1:FFN_CONV_K, wcols] * t_ref[:, cols]
                for k in range(FFN_CONV_K - 1):
                    acc = acc + cw_ref[k:k + 1, wcols] * ext[pl.ds(FFN_CONV_K - 1 - k, ts), :]
                dpb = acc.astype(BF16)
                dp_ref[:, wcols] = dpb
                dh = dh + _dot_nt(dpb, w_ref[half * (f // tc) + cb])
        dx, dsh, dsc, dg = _norm_mod_bwd(dh, x_ref[...], g_ref[...], s_ref[...], dres_ref[...])
        dx_ref[...] = dx
        dsh_ref[...] += dsh
        dsc_ref[...] += dsc
        dg_ref[...] += dg

    tile = pl.BlockSpec((ts, f), lambda i: (i, 0))
    nxt = pl.BlockSpec((FFN_HALO, f), lambda i: (jnp.minimum((i + 1) * nbh, s // FFN_HALO - 1), 0))
    row = pl.BlockSpec((ts, d), lambda i: (i, 0))
    vec = pl.BlockSpec((1, d), lambda i: (0, 0))
    return _hbm_call(
        body, name=name,
        grid=(s // ts,),
        in_specs=[tile, nxt, tile, nxt, _const_spec(cw.shape),
                  pl.BlockSpec(w.shape, lambda i: (0, 0, 0), pipeline_mode=pl.Buffered(1)), row, row, vec, vec],
        out_specs=[row, pl.BlockSpec((ts, 2 * f), lambda i: (i, 0)), vec, vec, vec],
        out_shape=[jax.ShapeDtypeStruct((s, d), F32), jax.ShapeDtypeStruct((s, 2 * f), BF16)] + [jax.ShapeDtypeStruct((1, d), F32)] * 3,
        scratch_shapes=[pltpu.VMEM((ts + FFN_HALO, tc), F32)],
        compiler_params=pltpu.CompilerParams(
            dimension_semantics=("arbitrary",),
            vmem_limit_bytes=_vmem_limit(4 * ts * f * 4 + 2 * f * d * 2 + 2 * ts * 2 * f * 2 + 12 * ts * d * 4 + 6 * ts * tc * 4)),
    )(dug, dug, duv, duv, cw, w, x, dres, gain, s1p)


def _adamw_math(w, g, m, v):
    m = ADAM_B1 * m + (1.0 - ADAM_B1) * g
    v = ADAM_B2 * v + (1.0 - ADAM_B2) * (g * g)
    m_hat = m / (1.0 - ADAM_B1 ** ADAM_STEP)
    v_hat = v / (1.0 - ADAM_B2 ** ADAM_STEP)
    delta = -ADAM_LR * (m_hat / (jnp.sqrt(v_hat) + ADAM_EPS) + ADAM_WD * w)
    return delta, m, v


def _adam_rows(rows, cols):
    want = max(8, (2 * 1024 * 1024 // (cols * 4)) // 8 * 8)
    tr = min(rows, want)
    while rows % tr:
        tr -= 8
    return tr


def _adamw(w, m, v, g_parts, name):
    shape = w.shape
    nl = shape[0] if w.ndim == 3 else 1
    r, c = shape[-2], shape[-1]
    tr = _adam_rows(r, c)
    ng = len(g_parts)

    def body(*refs):
        w_ref, m_ref, v_ref = refs[0:3]
        g_refs = refs[3:3 + ng]
        g_out, d_out, m_out, v_out = refs[3 + ng:]
        g = g_refs[0][...]
        for gr in g_refs[1:]:
            g = g + gr[...]
        delta, mn, vn = _adamw_math(w_ref[...], g, m_ref[...], v_ref[...])
        g_out[...] = g
        d_out[...] = delta
        m_out[...] = mn
        v_out[...] = vn

    blk = pl.BlockSpec((None, tr, c), lambda l, i: (l, i, 0))
    outs = _hbm_call(
        body, name=name,
        grid=(nl, r // tr),
        in_specs=[blk] * (3 + ng),
        out_specs=[blk] * 4,
        out_shape=[jax.ShapeDtypeStruct((nl, r, c), F32)] * 4,
        compiler_params=pltpu.CompilerParams(dimension_semantics=("arbitrary", "arbitrary"),
                                             vmem_limit_bytes=_vmem_limit(2 * (7 + ng) * tr * max(c, 128) * 4 + (8 << 20))),
    )(*[a.reshape(nl, r, c) for a in (w, m, v, *g_parts)])
    return [o.reshape(shape) for o in outs]


def _adamw_many(ws, ms, vs, gs, name):
    n = len(ws)

    def body(*refs):
        for k in range(n):
            w_ref, m_ref, v_ref, g_ref = refs[k], refs[n + k], refs[2 * n + k], refs[3 * n + k]
            delta, mn, vn = _adamw_math(w_ref[...], g_ref[...], m_ref[...], v_ref[...])
            refs[4 * n + 3 * k][...] = delta
            refs[4 * n + 3 * k + 1][...] = mn
            refs[4 * n + 3 * k + 2][...] = vn

    vm = pl.BlockSpec(memory_space=pltpu.VMEM)
    outs = pl.pallas_call(
        body, name=name,
        in_specs=[vm] * (4 * n),
        out_specs=[vm] * (3 * n),
        out_shape=[jax.ShapeDtypeStruct(a.shape, F32) for a in ws for _ in range(3)],
        compiler_params=pltpu.CompilerParams(vmem_limit_bytes=_vmem_limit(16 * sum(a.size for a in ws) * 4 + (8 << 20))),
    )(*ws, *ms, *vs, *gs)
    return [tuple(outs[3 * k:3 * k + 3]) for k in range(n)]


def _modw_adamw(sct, dmod, w, m, v, name):
    nl, d, n = w.shape
    tr = _row_tile(d, 128)

    def body(sct_ref, dm_ref, w_ref, m_ref, v_ref, g_out, d_out, m_out, v_out):
        sc = sct_ref[...].astype(BF16).astype(F32)
        dm = dm_ref[...].astype(BF16).astype(F32)
        g = sc[:, 0:1] * dm[0:1, :]
        for b in range(1, N_DEV):
            g = g + sc[:, b:b + 1] * dm[b:b + 1, :]
        delta, mn, vn = _adamw_math(w_ref[...], g, m_ref[...], v_ref[...])
        g_out[...] = g
        d_out[...] = delta
        m_out[...] = mn
        v_out[...] = vn

    blk = pl.BlockSpec((None, tr, n), lambda l, i: (l, i, 0))
    return _hbm_call(
        body, name=name,
        grid=(nl, d // tr),
        in_specs=[pl.BlockSpec((tr, N_DEV), lambda l, i: (i, 0)), pl.BlockSpec((None, N_DEV, n), lambda l, i: (l, 0, 0)),
                  blk, blk, blk],
        out_specs=[blk] * 4,
        out_shape=[jax.ShapeDtypeStruct((nl, d, n), F32)] * 4,
        compiler_params=pltpu.CompilerParams(dimension_semantics=("arbitrary", "arbitrary"),
                                             vmem_limit_bytes=_vmem_limit(2 * 8 * tr * n * 4 + (8 << 20))),
    )(sct, dmod, w, m, v)


def _reduce4(recvs, name):
    nl = len(recvs)
    shape = recvs[0].shape[1:]
    c = shape[-1]
    r = math.prod(shape[:-1])
    tr = _adam_rows(r, c)
    nt = r // tr

    def body(*refs):
        o_ref = refs[nl]
        for l in range(nl):
            @pl.when(pl.program_id(0) == l)
            def _():
                acc = refs[l][0].astype(F32)
                for k in range(1, N_CHIPS):
                    acc = acc + refs[l][k].astype(F32)
                o_ref[...] = acc

    def in_map(l):
        return lambda ll, i: (0, jnp.where(ll < l, 0, jnp.where(ll > l, nt - 1, i)), 0)

    return _hbm_call(
        body, name=name,
        grid=(nl, nt),
        in_specs=[pl.BlockSpec((N_CHIPS, tr, c), in_map(l)) for l in range(nl)],
        out_specs=pl.BlockSpec((None, tr, c), lambda ll, i: (ll, i, 0)),
        out_shape=jax.ShapeDtypeStruct((nl, r, c), F32),
        compiler_params=pltpu.CompilerParams(dimension_semantics=("arbitrary", "arbitrary"),
                                             vmem_limit_bytes=_vmem_limit(2 * 8 * nl * tr * max(c, 128) * 4 + (8 << 20))),
    )(*[rv.reshape(N_CHIPS, r, c) for rv in recvs]).reshape((nl,) + shape)


def _my_place():
    return lax.axis_index("x"), lax.axis_index("y"), lax.axis_index("c")


def _chip_coords(j):
    return j // 2, j % 2


def _mod_forward(c, mod_w, mod_b4):
    nl, d, n = mod_w.shape
    kc = 256

    def body(c_ref, w_ref, b_ref, mod_ref, sc_ref, cbuf, stage, s1, r1, s2, r2):
        mx, my, mc = _my_place()
        me = 4 * mx + 2 * my + mc
        q = 2 * mx + my
        cv = c_ref[...]
        cbuf[me] = jnp.broadcast_to(cv * jax.nn.sigmoid(cv), (8, d))
        sends = []
        for t in range(N_DEV):
            tx, ty = _chip_coords(t // 2)
            cp = pltpu.make_async_remote_copy(src_ref=cbuf.at[me], dst_ref=cbuf.at[me], send_sem=s1.at[t], recv_sem=r1.at[me],
                                              device_id=(tx, ty, t % 2), device_id_type=MESH)

            @pl.when(t != me)
            def _():
                cp.start()

            sends.append((t, cp))
        for t in range(N_DEV):
            @pl.when(t != me)
            def _():
                pltpu.make_async_remote_copy(src_ref=cbuf.at[t], dst_ref=cbuf.at[t], send_sem=s1.at[t], recv_sem=r1.at[t],
                                             device_id=(mx, my, mc), device_id_type=MESH).wait_recv()
        for t, cp in sends:
            @pl.when(t != me)
            def _():
                cp.wait_send()

        row = lax.broadcasted_iota(jnp.int32, (8, d), 0)
        sc_all = jnp.zeros((8, d), F32)
        for t in range(N_DEV):
            sc_all = sc_all + jnp.where(row == t, cbuf[t], 0.0)
        sc_ref[...] = sc_all
        rown = lax.broadcasted_iota(jnp.int32, (8, n), 0)
        for l in range(nl):
            acc = jnp.zeros((8, n), F32)
            for k0 in range(0, d, kc):
                acc = acc + _dot(sc_all[:, k0:k0 + kc].astype(BF16), w_ref[l, k0:k0 + kc, :].astype(BF16))
            acc = acc + b_ref[l, q]
            for j in range(N_CHIPS):
                jx, jy = _chip_coords(j)
                bdest = 4 * jx + 2 * jy + mc
                rowv = jnp.sum(jnp.where(rown == bdest, acc, 0.0), axis=0, keepdims=True)
                stage[j, l] = jnp.broadcast_to(rowv, (8, n))
        sends2 = []
        for j in range(N_CHIPS):
            jx, jy = _chip_coords(j)
            cp = pltpu.make_async_remote_copy(src_ref=stage.at[j], dst_ref=mod_ref.at[:, q], send_sem=s2.at[j], recv_sem=r2.at[q],
                                              device_id=(jx, jy, mc), device_id_type=MESH)

            @pl.when(j != q)
            def _():
                cp.start()

            @pl.when(j == q)
            def _():
                for l in range(nl):
                    mod_ref[l, j] = stage[j, l]

            sends2.append((j, cp))
        for j in range(N_CHIPS):
            @pl.when(j != q)
            def _():
                pltpu.make_async_remote_copy(src_ref=stage.at[j], dst_ref=mod_ref.at[:, j], send_sem=s2.at[j], recv_sem=r2.at[j],
                                             device_id=(mx, my, mc), device_id_type=MESH).wait_recv()
        for j, cp in sends2:
            @pl.when(j != q)
            def _():
                cp.wait_send()

    vm = pl.BlockSpec(memory_space=pltpu.VMEM)
    return pl.pallas_call(
        body, name="mod_forward",
        in_specs=[vm, vm, vm],
        out_specs=[vm, vm],
        out_shape=[jax.ShapeDtypeStruct((nl, N_CHIPS, 8, n), F32), jax.ShapeDtypeStruct((8, d), F32)],
        scratch_shapes=[pltpu.VMEM((N_DEV, 8, d), F32), pltpu.VMEM((N_CHIPS, nl, 8, n), F32),
                        pltpu.SemaphoreType.DMA((N_DEV,)), pltpu.SemaphoreType.DMA((N_DEV,)),
                        pltpu.SemaphoreType.DMA((N_CHIPS,)), pltpu.SemaphoreType.DMA((N_CHIPS,))],
        compiler_params=pltpu.CompilerParams(vmem_limit_bytes=_vmem_limit(2 * nl * d * n * 4 + (8 << 20))),
    )(c, mod_w, mod_b4)


_HBM_SPEC = pl.BlockSpec(memory_space=pltpu.HBM)
_SEM_SPEC = pl.BlockSpec(memory_space=pltpu.SEMAPHORE)
_DATAFLOW = pltpu.SideEffectType.DATAFLOW_SIDE_EFFECTING


def _slot(ref, scatter, j):
    return ref.at[j] if scatter else ref


def _exchange_start(groups, scatter, after, name):
    flat = [a for g in groups for a in g]
    na = len(flat)
    ng = len(groups)
    sizes = [len(g) for g in groups]
    first = [sum(sizes[:g]) for g in range(ng)]
    where = [(g, k) for g in range(ng) for k in range(sizes[g])]
    mx, my, _ = _my_place()
    qo = 2 * mx + my
    lands = []
    for a in flat:
        own = lax.dynamic_index_in_dim(a, qo, 0, keepdims=False) if scatter else a
        lands.append(lax.dynamic_update_index_in_dim(lax.empty((N_CHIPS,) + own.shape, a.dtype), own, qo, 0))

    def body(*refs):
        ins, lnd = refs[:na], refs[na:2 * na]
        ssems, rsems = refs[2 * na + 1:2 * na + 1 + ng], refs[2 * na + 1 + ng:2 * na + 1 + 2 * ng]
        token = refs[-1]
        mx, my, mc = _my_place()
        q = 2 * mx + my
        for j in range(N_CHIPS):
            jx, jy = _chip_coords(j)
            for a in range(na):
                g, k = where[a]

                @pl.when(j != q)
                def _():
                    pltpu.make_async_remote_copy(src_ref=_slot(ins[a], scatter, j), dst_ref=lnd[a].at[q],
                                                 send_sem=ssems[g].at[k * N_CHIPS + j], recv_sem=rsems[g].at[k * N_CHIPS + q],
                                                 device_id=(jx, jy, mc), device_id_type=MESH).start()
        token[...] = jnp.zeros_like(token)

    sem_shapes = [pltpu.SemaphoreType.DMA((n * N_CHIPS,)) for n in sizes]
    outs = pl.pallas_call(
        body, name=name,
        in_specs=[_HBM_SPEC] * (2 * na) + [pl.BlockSpec(memory_space=pl.ANY)],
        out_specs=[_SEM_SPEC] * (2 * ng) + [_HBM_SPEC] * (2 * na) + [pl.BlockSpec(memory_space=pltpu.VMEM)],
        out_shape=sem_shapes + sem_shapes + [pltpu.HBM(a.shape, a.dtype) for a in flat + lands]
        + [jax.ShapeDtypeStruct((8, 128), F32)],
        input_output_aliases={i: 2 * ng + i for i in range(2 * na)},
        compiler_params=pltpu.CompilerParams(has_side_effects=_DATAFLOW),
    )(*[pltpu.with_memory_space_constraint(a, pltpu.HBM) for a in flat + lands], after)
    ssems, rsems = outs[:ng], outs[ng:2 * ng]
    src_thru, land_thru = outs[2 * ng:2 * ng + na], outs[2 * ng + na:2 * ng + 2 * na]
    states = [(src_thru[first[g]:first[g] + sizes[g]], land_thru[first[g]:first[g] + sizes[g]], ssems[g], rsems[g])
              for g in range(ng)]
    return states, outs[-1]


def _exchange_wait(state, scatter, after, name):
    srcs, lands, ssem, rsem = state
    na = len(srcs)

    def body(*refs):
        ins, lnd = refs[:na], refs[na:2 * na]
        ssem_ref, rsem_ref = refs[2 * na], refs[2 * na + 1]
        mx, my, mc = _my_place()
        q = 2 * mx + my
        for j in range(N_CHIPS):
            for a in range(na):
                @pl.when(j != q)
                def _():
                    cp = pltpu.make_async_remote_copy(src_ref=_slot(ins[a], scatter, j), dst_ref=lnd[a].at[j],
                                                      send_sem=ssem_ref.at[a * N_CHIPS + j], recv_sem=rsem_ref.at[a * N_CHIPS + j],
                                                      device_id=(mx, my, mc), device_id_type=MESH)
                    cp.wait_send()
                    cp.wait_recv()

    outs = pl.pallas_call(
        body, name=name,
        in_specs=[_HBM_SPEC] * (2 * na) + [_SEM_SPEC, _SEM_SPEC, pl.BlockSpec(memory_space=pl.ANY)],
        out_specs=[_HBM_SPEC] * (2 * na),
        out_shape=[pltpu.HBM(a.shape, a.dtype) for a in list(srcs) + list(lands)],
        input_output_aliases={i: i for i in range(2 * na)},
        compiler_params=pltpu.CompilerParams(has_side_effects=_DATAFLOW),
    )(*srcs, *lands, ssem, rsem, after)
    return outs[na:]


def _swap_with_sibling(arrs):
    na = len(arrs)

    def body(*refs):
        ins, outs = refs[:na], refs[na:2 * na]
        ssem, rsem = refs[2 * na:]
        mx, my, mc = _my_place()
        cps = [pltpu.make_async_remote_copy(src_ref=ins[a], dst_ref=outs[a], send_sem=ssem.at[a], recv_sem=rsem.at[a],
                                            device_id=(mx, my, 1 - mc), device_id_type=MESH) for a in range(na)]
        for cp in cps:
            cp.start()
        for cp in cps:
            cp.wait()

    hbm = pl.BlockSpec(memory_space=pl.ANY)
    return _hbm_call(
        body, name="swap_sibling",
        in_specs=[hbm] * na,
        out_specs=[hbm] * na,
        out_shape=[jax.ShapeDtypeStruct(a.shape, a.dtype) for a in arrs],
        scratch_shapes=[pltpu.SemaphoreType.DMA((na,)), pltpu.SemaphoreType.DMA((na,))],
    )(*arrs)


def _allreduce_small(rows_all, rows_sum):
    ra, c = rows_all.shape
    r = rows_sum.shape[0]
    ch = r // N_DEV
    assert ch % 8 == 0 and ch * N_DEV == r

    def body(a_ref, s_ref, all_ref, sum_ref, rbuf, red, sa, rva, sb, rvb, sc, rvc):
        mx, my, mc = _my_place()
        me = 4 * mx + 2 * my + mc
        mine = pl.ds(pl.multiple_of(me * ch, 8), ch)
        all_ref[me] = a_ref[...]
        rbuf[me] = s_ref[mine, :]

        def dev(t):
            tx, ty = _chip_coords(t // 2)
            return (tx, ty, t % 2)

        def everyone_else(fn):
            for t in range(N_DEV):
                @pl.when(t != me)
                def _():
                    fn(t)

        def copy_a(t, slot):
            return pltpu.make_async_remote_copy(src_ref=a_ref, dst_ref=all_ref.at[slot], send_sem=sa.at[t], recv_sem=rva.at[slot],
                                                device_id=dev(t), device_id_type=MESH)

        def copy_b(t, slot):
            return pltpu.make_async_remote_copy(src_ref=s_ref.at[pl.ds(t * ch, ch), :], dst_ref=rbuf.at[slot], send_sem=sb.at[t],
                                                recv_sem=rvb.at[slot], device_id=dev(t), device_id_type=MESH)

        def copy_c(t, chunk_start, slot):
            return pltpu.make_async_remote_copy(src_ref=red, dst_ref=sum_ref.at[pl.ds(chunk_start, ch), :], send_sem=sc.at[t],
                                                recv_sem=rvc.at[slot], device_id=dev(t), device_id_type=MESH)

        everyone_else(lambda t: (copy_a(t, me).start(), copy_b(t, me).start()))
        everyone_else(lambda t: (copy_a(t, t).wait_recv(), copy_b(t, t).wait_recv()))
        everyone_else(lambda t: (copy_a(t, me).wait_send(), copy_b(t, me).wait_send()))
        acc = rbuf[0]
        for t in range(1, N_DEV):
            acc = acc + rbuf[t]
        red[...] = acc
        sum_ref[mine, :] = acc
        everyone_else(lambda t: copy_c(t, pl.multiple_of(me * ch, 8), me).start())
        everyone_else(lambda t: copy_c(t, t * ch, t).wait_recv())
        everyone_else(lambda t: copy_c(t, pl.multiple_of(me * ch, 8), me).wait_send())

    vm = pl.BlockSpec(memory_space=pltpu.VMEM)
    return pl.pallas_call(
        body, name="allreduce_small",
        in_specs=[vm, vm],
        out_specs=[vm, vm],
        out_shape=[jax.ShapeDtypeStruct((N_DEV, ra, c), F32), jax.ShapeDtypeStruct((r, c), F32)],
        scratch_shapes=[pltpu.VMEM((N_DEV, ch, c), F32), pltpu.VMEM((ch, c), F32)] + [pltpu.SemaphoreType.DMA((N_DEV,))] * 6,
        compiler_params=pltpu.CompilerParams(vmem_limit_bytes=_vmem_limit((3 * r + 2 * N_DEV * ra) * c * 4 + (4 << 20))),
    )(rows_all, rows_sum)


def _pack(arrs, row_multiple=8):
    rows, layout, at = [], [], 0
    for a in arrs:
        n = a.size
        nr = -(-n // (8 * SMALL_COLS)) * 8
        flat = a.reshape(-1)
        if nr * SMALL_COLS != n:
            flat = jnp.pad(flat, (0, nr * SMALL_COLS - n))
        rows.append(flat.reshape(nr, SMALL_COLS))
        layout.append((at, nr, a.shape))
        at += nr
    pad = -at % row_multiple
    if pad:
        rows.append(jnp.zeros((pad, SMALL_COLS), F32))
    return jnp.concatenate(rows, axis=0), layout


def _unpack(buf, layout):
    out = []
    for at, nr, shape in layout:
        n = math.prod(shape)
        out.append(buf[at:at + nr].reshape(-1)[:n].reshape(shape))
    return out


SMALL_NAMES = ("mod_b", "mix_pre_g", "mix_post_g", "sgu_norm_g", "sgu_norm_b", "sgu_w", "sgu_b", "conv_b", "conv_norm_g",
               "conv_norm_b", "pool_w", "pool_scale", "branch_g", "ffn_pre_g", "ffn_post_g", "ffn_conv_b")
SHARDED_SMALL = ("conv_w", "ffn_conv_w")
WEIGHT_ORDER = ("mod_w", "mod_b", "mix_pre_g", "mix_post_g", "w_in", "sgu_norm_g", "sgu_norm_b", "sgu_w", "sgu_b", "conv_w",
                "conv_b", "conv_norm_g", "conv_norm_b", "pool_w", "pool_scale", "branch_g", "w_out", "ffn_pre_g", "ffn_post_g",
                "ffn_up", "ffn_conv_w", "ffn_conv_b", "ffn_down")


def _block_diag(blocks):
    n, a, b = blocks.shape
    eye = jnp.eye(n, dtype=blocks.dtype)
    return (eye[:, None, :, None] * blocks[:, :, None, :]).reshape(n * a, n * b)


def _diag_blocks(mat, n):
    a = mat.shape[0] // n
    return jnp.stack([mat[g * a:(g + 1) * a, g * a:(g + 1) * a] for g in range(n)])


def _step(x, c, loss_target, w, m, v):
    nl = w["mod_w"].shape[0]
    s, d = x.shape[1], x.shape[2]
    heads = SGU_WIDTH // HEAD_DIM
    groups = len(POOL_WINDOWS)
    mx, my, _ = _my_place()
    q = 2 * mx + my
    x0 = x.reshape(s, d)
    tgt = loss_target.reshape(s, d)

    nmod = w["mod_w"].shape[2]
    kin = w["w_in"].shape[2]
    inw = kin * N_CHIPS
    f2 = w["ffn_up"].shape[2] * N_CHIPS
    f = f2 // 2

    def wgroups(l):
        return [[w["w_in"][l].astype(BF16), w["conv_w"][l], w["ffn_conv_w"][l]], [w["w_out"][l].astype(BF16)],
                [w["ffn_up"][l].astype(BF16)], [w["ffn_down"][l].astype(BF16)]]

    gstates = {}
    (gstates[0, 0], gstates[0, 1]), gtoken = _exchange_start(wgroups(0)[:2], False, c, "gather_start_in_0")
    mod4, sc_all = _mod_forward(c + gtoken[0:1, 0:1], w["mod_w"], w["mod_b"].reshape(nl, N_CHIPS, 1, nmod))
    mod = mod4[:, :, 0, :].reshape(nl, N_MOD, 1, d)

    tril = jnp.tril(jnp.ones((CHUNK, CHUNK), bool))
    bd = _block_diag(jnp.ones((heads, HEAD_DIM, HEAD_DIM), BF16))

    def mixer_params(l, conv_w):
        wm = jnp.where(tril[None], w["sgu_w"][l], 0.0)
        pw = _block_diag(w["pool_w"][l])
        return dict(
            bd=bd, ng=w["sgu_norm_g"][l][None], nb=w["sgu_norm_b"][l][None],
            wm=wm.astype(BF16), wmt=jnp.swapaxes(wm, 1, 2).astype(BF16),
            bias=jnp.repeat(w["sgu_b"][l].T, HEAD_DIM, axis=1),
            cw=conv_w, cb=w["conv_b"][l][None], cng=w["conv_norm_g"][l][None], cnb=w["conv_norm_b"][l][None],
            pw=pw.astype(BF16), pwt=pw.T.astype(BF16), ps=w["pool_scale"][l][None], bg=w["branch_g"][l][None])

    saved = []
    xl = x0
    arrived = {0: list(_exchange_wait(gstates[0, 0], False, mod4, "gather_wait_in_0"))}
    arrived[0] += list(_exchange_wait(gstates[0, 1], False, arrived[0][0], "gather_wait_out_0"))
    for l in range(nl):
        sh1, sc1, g1, sh2, sc2, g2 = [mod[l, k] for k in range(N_MOD)]
        gpre1, gpost1 = w["mix_pre_g"][l][None], w["mix_post_g"][l][None]
        gpre2, gpost2 = w["ffn_pre_g"][l][None], w["ffn_post_g"][l][None]
        fcb = w["ffn_conv_b"][l][None]
        sh1_after, bg_after, g1_after, sh2_after, fcb_after = sh1, w["branch_g"][l][None], g1, sh2, fcb
        g_win, g_cw, g_fcw = arrived[l][:3]
        if l == 0:
            (gstates[0, 2], gstates[0, 3]), tok = _exchange_start(wgroups(0)[2:4], False, arrived[0][3], "gather_start_up_0")
            sh1_after = sh1 + tok[0:1, 0:1]
        w_in = jnp.transpose(g_win, (1, 0, 2)).reshape(d, inw)
        conv_w = jnp.transpose(g_cw, (1, 0, 2)).reshape(CONV_K, CONV_WIDTH)
        ffn_cw = jnp.transpose(g_fcw, (1, 0, 2)).reshape(FFN_CONV_K, f2)
        mp = mixer_params(l, conv_w)
        z, h1 = _norm_mod_matmul(xl, gpre1, 1.0 + sc1, sh1_after, w_in[None], f"mix_in_{l}")
        w_out = arrived[l][3].reshape(d, d)
        ycat, cbo, dpool = _mixer_fwd(z, dict(mp, bg=bg_after), f"mixer_fwd_{l}")
        (up,) = _exchange_wait(gstates[l, 2], False, ycat, f"gather_wait_up_{l}")
        o, x1 = _matmul_norm_resid(ycat, w_out, xl, g1_after, gpost1, f"mix_out_{l}")
        (g_down,) = _exchange_wait(gstates[l, 3], False, x1, f"gather_wait_down_{l}")
        down = g_down.reshape(f, d)
        if l + 1 < nl:
            (gstates[l + 1, 0], gstates[l + 1, 1]), tok = _exchange_start(wgroups(l + 1)[0:2], False, g_down,
                                                                        f"gather_start_in_{l + 1}")
            sh2_after = sh2 + tok[0:1, 0:1]
        p, h2 = _norm_mod_matmul(x1, gpre2, 1.0 + sc2, sh2_after, up, f"ffn_in_{l}")
        if l + 1 < nl:
            nxt = _exchange_wait(gstates[l + 1, 0], False, p, f"gather_wait_in_{l + 1}")
            nxt_out = _exchange_wait(gstates[l + 1, 1], False, nxt[0], f"gather_wait_out_{l + 1}")
            arrived[l + 1] = list(nxt) + list(nxt_out)
            (gstates[l + 1, 2], gstates[l + 1, 3]), tok = _exchange_start(wgroups(l + 1)[2:4], False, nxt_out[0],
                                                                        f"gather_start_up_{l + 1}")
            fcb_after = fcb + tok[0:1, 0:1]
        act = _ffn_act_fwd(p, ffn_cw, fcb_after, f"ffn_act_{l}")
        qo, x2 = _matmul_norm_resid(act, down, x1, g2, gpost2, f"ffn_out_{l}")
        saved.append(dict(x=xl, z=z, h1=h1, ycat=ycat, cbo=cbo, dpool=dpool, o=o, x1=x1, p=p, h2=h2, act=act, qo=qo, mp=mp, fcb=fcb,
                          w_in=w_in, w_out=w_out, up=up, down=down, ffn_cw=ffn_cw,
                          mods=(sh1, sc1, g1, sh2, sc2, g2), gains=(gpre1, gpost1, gpre2, gpost2)))
        xl = x2

    dx, loss_row = _loss_head(xl, tgt)

    small = {n: [None] * nl for n in SMALL_NAMES + SHARDED_SMALL}
    dmods = [None] * nl
    tn = f2 // N_CHIPS
    sstates = {}
    token = None
    for l in reversed(range(nl)):
        sv = saved[l]
        sh1, sc1, g1, sh2, sc2, g2 = sv["mods"]
        gpre1, gpost1, gpre2, gpost2 = sv["gains"]
        if token is not None:
            g2 = g2 + token[0:1, 0:1]
        dq, dact, dg2, dgpost2 = _resid_bwd_matmul(dx, sv["qo"], g2, gpost2, sv["down"], f"ffn_out_bwd_{l}")
        g_down = _wgrad(sv["act"], dq, (f, lambda j: 0), (d, lambda j: 0), jax.ShapeDtypeStruct((f, d), BF16),
                        (1, lambda j: (0, 0)), (f, d), f"wgrad_ffn_down_{l}")
        dug, duv, dfwg, dfwv, dfbg, dfbv = _ffn_act_bwd(sv["p"], dact, sv["ffn_cw"], sv["fcb"], f"ffn_act_bwd_{l}")
        dx1, dp, dsh2, dsc2, dgpre2 = _ffn_in_bwd(dug, duv, sv["ffn_cw"], sv["up"], sv["x1"], dx, gpre2, 1.0 + sc2,
                                                  f"ffn_in_bwd_{l}")
        g_up = _wgrad(sv["h2"], dp, (d, lambda j: 0), (tn, lambda j: j), jax.ShapeDtypeStruct((N_CHIPS, d, tn), BF16),
                      (N_CHIPS, lambda j: (j, 0, 0)), (None, d, tn), f"wgrad_ffn_up_{l}")
        (sstates[l, 0],), token = _exchange_start([[g_down.reshape(N_CHIPS, f // N_CHIPS, d), g_up]], True, g_up,
                                                  f"scatter_start_ffn_{l}")
        do, dycat, dg1, dgpost1 = _resid_bwd_matmul(dx1, sv["o"], g1 + token[0:1, 0:1], gpost1, sv["w_out"],
                                                    f"mix_out_bwd_{l}")
        g_out = _wgrad(sv["ycat"], do, (d, lambda j: 0), (d, lambda j: 0), jax.ShapeDtypeStruct((d, d), BF16),
                       (1, lambda j: (0, 0)), (d, d), f"wgrad_w_out_{l}")
        (sstates[l, 1],), token = _exchange_start([[g_out.reshape(N_CHIPS, d // N_CHIPS, d)]], True, g_out,
                                                  f"scatter_start_out_{l}")
        mp_after = dict(sv["mp"], bg=sv["mp"]["bg"] + token[0:1, 0:1])
        (dza, dcb, dd, dbg, dwm, dbias, dng, dnb, dcng, dcnb, dps, dpw) = _mixer_bwd_a(sv["z"], sv["cbo"], sv["dpool"], dycat, mp_after, f"mixer_bwd_a_{l}")
        dx, dz, dsh1, dsc1, dgpre1, dcw, dcbias = _mixer_bwd_b(
            sv["z"], dza, dcb, dd, sv["x"], dx1, gpre1, 1.0 + sc1, sv["mp"]["cw"], sv["w_in"], f"mixer_bwd_b_{l}")
        g_in = _wgrad(sv["h1"], dz, (d, lambda j: 0), (inw, lambda j: 0), jax.ShapeDtypeStruct((d, inw), BF16),
                      (1, lambda j: (0, 0)), (d, inw), f"wgrad_w_in_{l}")
        g_in_parts = jnp.transpose(g_in.reshape(d, N_CHIPS, kin), (1, 0, 2))
        if l > 0:
            (sstates[l, 2],), token = _exchange_start([[g_in_parts]], True, g_in_parts, f"scatter_start_in_{l}")

        dmods[l] = jnp.concatenate([dsh1, dsc1, dg1, dsh2, dsc2, dg2], axis=0)
        small["mix_pre_g"][l], small["mix_post_g"][l] = dgpre1[0], dgpost1[0]
        small["ffn_pre_g"][l], small["ffn_post_g"][l] = dgpre2[0], dgpost2[0]
        small["sgu_norm_g"][l], small["sgu_norm_b"][l] = dng[0], dnb[0]
        small["sgu_w"][l] = jnp.where(tril[None], dwm, 0.0)
        small["sgu_b"][l] = dbias.reshape(CHUNK, heads, HEAD_DIM).sum(-1).T
        small["conv_b"][l], small["conv_norm_g"][l], small["conv_norm_b"][l] = dcbias[0], dcng[0], dcnb[0]
        small["pool_w"][l], small["pool_scale"][l], small["branch_g"][l] = _diag_blocks(dpw, groups), dps[0], dbg[0]
        small["ffn_conv_b"][l] = jnp.concatenate([dfbg[0], dfbv[0]])
        small["conv_w"][l] = dcw
        small["ffn_conv_w"][l] = jnp.concatenate([dfwg, dfwv], axis=1)

    names = [n for n in SMALL_NAMES if n != "mod_b"] + list(SHARDED_SMALL)
    dmod_rows, _ = _pack([jnp.stack(dmods)])
    packed, layout = _pack([jnp.stack(dmods), loss_row] + [jnp.stack(small[n]) for n in names], 8 * N_DEV)
    gathered, summed = _allreduce_small(dmod_rows, packed)
    (sstates[0, 2],), token = _exchange_start([[g_in_parts]], True, summed, "scatter_start_in_0")
    parts = _unpack(summed, layout)
    loss = parts[1][0, 0]
    gsmall = dict(zip(names, parts[2:]))
    gsmall["mod_b"] = parts[0].reshape(nl, N_MOD * d)
    dmod_all = gathered[:, :nl * N_MOD].reshape(N_DEV, nl, N_MOD * d)
    dmod_mine = jnp.transpose(lax.dynamic_slice_in_dim(dmod_all, q * nmod, nmod, axis=2), (1, 0, 2))

    grads, deltas, new_m, new_v = {}, {}, {}, {}

    def put(name, res):
        grads[name], deltas[name], new_m[name], new_v[name] = res

    put("mod_w", _modw_adamw(sc_all.T + token[0:1, 0:1], dmod_mine, w["mod_w"], m["mod_w"], v["mod_w"], "adamw_mod_w"))

    gsmall["conv_w"] = lax.dynamic_slice_in_dim(gsmall["conv_w"], q * (CONV_WIDTH // N_CHIPS), CONV_WIDTH // N_CHIPS, axis=2)
    gsmall["ffn_conv_w"] = lax.dynamic_slice_in_dim(gsmall["ffn_conv_w"], q * (f2 // N_CHIPS), f2 // N_CHIPS, axis=2)
    snames = SMALL_NAMES + SHARDED_SMALL
    res = _adamw_many([w[n] for n in snames], [m[n] for n in snames], [v[n] for n in snames], [gsmall[n] for n in snames],
                      "adamw_small")
    for n, (d_, m_, v_) in zip(snames, res):
        put(n, (gsmall[n], d_, m_, v_))

    recv = dict(w_in=[None] * nl, w_out=[None] * nl, ffn_up=[None] * nl, ffn_down=[None] * nl)
    done = grads["mod_w"]
    for l in reversed(range(nl)):
        recv["ffn_down"][l], recv["ffn_up"][l] = _exchange_wait(sstates[l, 0], True, done, f"scatter_wait_ffn_{l}")
        (recv["w_out"][l],) = _exchange_wait(sstates[l, 1], True, recv["ffn_up"][l], f"scatter_wait_out_{l}")
        (recv["w_in"][l],) = _exchange_wait(sstates[l, 2], True, recv["w_out"][l], f"scatter_wait_in_{l}")
        done = recv["w_in"][l]
    big = ("w_in", "w_out", "ffn_up", "ffn_down")
    mine = [_reduce4(recv[n], f"reduce4_{n}") for n in big]
    theirs = _swap_with_sibling(mine)
    for n, a, b in zip(big, mine, theirs):
        put(n, _adamw(w[n], m[n], v[n], [a, b], f"adamw_{n}"))

    return (loss, dx.reshape(1, s, d), *[grads[n] for n in WEIGHT_ORDER], *[deltas[n] for n in WEIGHT_ORDER],
            *[new_m[n] for n in WEIGHT_ORDER], *[new_v[n] for n in WEIGHT_ORDER])


def kernel(x, c, mod_w, mod_b, mix_pre_g, mix_post_g, w_in, sgu_norm_g, sgu_norm_b, sgu_w, sgu_b, conv_w, conv_b, conv_norm_g, conv_norm_b, pool_w, pool_scale, branch_g, w_out, ffn_pre_g, ffn_post_g, ffn_up, ffn_conv_w, ffn_conv_b, ffn_down, loss_target, m_mod_w, m_mod_b, m_mix_pre_g, m_mix_post_g, m_w_in, m_sgu_norm_g, m_sgu_norm_b, m_sgu_w, m_sgu_b, m_conv_w, m_conv_b, m_conv_norm_g, m_conv_norm_b, m_pool_w, m_pool_scale, m_branch_g, m_w_out, m_ffn_pre_g, m_ffn_post_g, m_ffn_up, m_ffn_conv_w, m_ffn_conv_b, m_ffn_down, v_mod_w, v_mod_b, v_mix_pre_g, v_mix_post_g, v_w_in, v_sgu_norm_g, v_sgu_norm_b, v_sgu_w, v_sgu_b, v_conv_w, v_conv_b, v_conv_norm_g, v_conv_norm_b, v_pool_w, v_pool_scale, v_branch_g, v_w_out, v_ffn_pre_g, v_ffn_post_g, v_ffn_up, v_ffn_conv_w, v_ffn_conv_b, v_ffn_down):
    w = dict(mod_w=mod_w, mod_b=mod_b, mix_pre_g=mix_pre_g, mix_post_g=mix_post_g, w_in=w_in, sgu_norm_g=sgu_norm_g,
             sgu_norm_b=sgu_norm_b, sgu_w=sgu_w, sgu_b=sgu_b, conv_w=conv_w, conv_b=conv_b, conv_norm_g=conv_norm_g,
             conv_norm_b=conv_norm_b, pool_w=pool_w, pool_scale=pool_scale, branch_g=branch_g, w_out=w_out,
             ffn_pre_g=ffn_pre_g, ffn_post_g=ffn_post_g, ffn_up=ffn_up, ffn_conv_w=ffn_conv_w, ffn_conv_b=ffn_conv_b,
             ffn_down=ffn_down)
    m = dict(mod_w=m_mod_w, mod_b=m_mod_b, mix_pre_g=m_mix_pre_g, mix_post_g=m_mix_post_g, w_in=m_w_in,
             sgu_norm_g=m_sgu_norm_g, sgu_norm_b=m_sgu_norm_b, sgu_w=m_sgu_w, sgu_b=m_sgu_b, conv_w=m_conv_w,
             conv_b=m_conv_b, conv_norm_g=m_conv_norm_g, conv_norm_b=m_conv_norm_b, pool_w=m_pool_w,
             pool_scale=m_pool_scale, branch_g=m_branch_g, w_out=m_w_out, ffn_pre_g=m_ffn_pre_g, ffn_post_g=m_ffn_post_g,
             ffn_up=m_ffn_up, ffn_conv_w=m_ffn_conv_w, ffn_conv_b=m_ffn_conv_b, ffn_down=m_ffn_down)
    v = dict(mod_w=v_mod_w, mod_b=v_mod_b, mix_pre_g=v_mix_pre_g, mix_post_g=v_mix_post_g, w_in=v_w_in,
             sgu_norm_g=v_sgu_norm_g, sgu_norm_b=v_sgu_norm_b, sgu_w=v_sgu_w, sgu_b=v_sgu_b, conv_w=v_conv_w,
             conv_b=v_conv_b, conv_norm_g=v_conv_norm_g, conv_norm_b=v_conv_norm_b, pool_w=v_pool_w,
             pool_scale=v_pool_scale, branch_g=v_branch_g, w_out=v_w_out, ffn_pre_g=v_ffn_pre_g, ffn_post_g=v_ffn_post_g,
             ffn_up=v_ffn_up, ffn_conv_w=v_ffn_conv_w, ffn_conv_b=v_ffn_conv_b, ffn_down=v_ffn_down)
    return _step(x, c, loss_target, w, m, v)
```

```python
import functools
import math

import jax
import jax.numpy as jnp
from jax import lax
from jax.experimental import pallas as pl
from jax.experimental.pallas import tpu as pltpu

F32 = jnp.float32
BF16 = jnp.bfloat16
MESH = pl.DeviceIdType.MESH

EPS = 1e-6
HEAD_DIM = 64
CHUNK = 128
SGU_WIDTH = 384
CONV_WIDTH = 384
POOL_WIDTH = 256
POOL_WINDOWS = (2, 4, 8, 16)
CONV_K = 31
FFN_CONV_K = 3
N_MOD = 6
N_CHIPS = 4
N_DEV = 8

ADAM_LR = 0.001
ADAM_B1 = 0.9
ADAM_B2 = 0.999
ADAM_EPS = 1e-08
ADAM_WD = 0.01
ADAM_STEP = 10

MIX_HALO = 32
FFN_HALO = 8
FFN_ROWS = 16
FFN_UNROLL = 8
LANES = 128
CONV_ROWS = 32
SMALL_COLS = 1024
VMEM_BYTES_V7X = 64 * 1024 * 1024


def _vmem_limit(estimate_bytes):
    return int(min(max(estimate_bytes, 16 * 1024 * 1024), VMEM_BYTES_V7X - 8 * 1024 * 1024))


def _row_tile(s, want):
    return want if s % want == 0 else math.gcd(s, want)


def _rsum(v):
    return jnp.sum(v, axis=0, keepdims=True)


def _rmean(v):
    return jnp.mean(v, axis=-1, keepdims=True)


def _gelu(v):
    k = math.sqrt(2.0 / math.pi)
    return 0.5 * v * (1.0 + jnp.tanh(k * (v + 0.044715 * v * v * v)))


def _gelu_grad(v):
    k = math.sqrt(2.0 / math.pi)
    t = jnp.tanh(k * (v + 0.044715 * v * v * v))
    return 0.5 * (1.0 + t) + 0.5 * v * (1.0 - t * t) * (k * (1.0 + 3.0 * 0.044715 * v * v))


def _dot(a, b):
    return jnp.dot(a, b, preferred_element_type=F32)


def _dot_nt(a, b):
    return lax.dot_general(a, b, (((1,), (1,)), ((), ())), preferred_element_type=F32)


def _dot_tn(a, b):
    return lax.dot_general(a, b, (((0,), (0,)), ((), ())), preferred_element_type=F32)


def _group_mean(v, bd):
    hi = v.astype(BF16)
    lo = (v - hi.astype(F32)).astype(BF16)
    return (_dot(hi, bd) + _dot(lo, bd)) * (1.0 / HEAD_DIM)


def _const_spec(shape):
    nd = len(shape)
    return pl.BlockSpec(shape, lambda *_: (0,) * nd)


def _hbm_call(body, **kw):
    call = pl.pallas_call(body, **kw)
    return lambda *args: call(*[pltpu.with_memory_space_constraint(a, pltpu.HBM) for a in args])


def _norm_mod_matmul(x, gain, s1p, shift, w, name):
    s, d = x.shape
    nb, _, tn = w.shape
    ts = _row_tile(s, 512 if nb * tn <= 2048 else 256)

    def body(x_ref, g_ref, s_ref, b_ref, w_ref, z_ref, h_ref):
        xv = x_ref[...]
        r = lax.rsqrt(_rmean(xv * xv) + EPS)
        h = ((xv * r) * g_ref[...] * s_ref[...] + b_ref[...]).astype(BF16)
        h_ref[...] = h
        for j in range(nb):
            z_ref[:, j * tn:(j + 1) * tn] = _dot(h, w_ref[j])

    vec = pl.BlockSpec((1, d), lambda i: (0, 0))
    return _hbm_call(
        body, name=name,
        grid=(s // ts,),
        in_specs=[pl.BlockSpec((ts, d), lambda i: (i, 0)), vec, vec, vec,
                  pl.BlockSpec((nb, d, tn), lambda i: (0, 0, 0), pipeline_mode=pl.Buffered(1))],
        out_specs=[pl.BlockSpec((ts, nb * tn), lambda i: (i, 0)), pl.BlockSpec((ts, d), lambda i: (i, 0))],
        out_shape=[jax.ShapeDtypeStruct((s, nb * tn), F32), jax.ShapeDtypeStruct((s, d), BF16)],
        compiler_params=pltpu.CompilerParams(
            dimension_semantics=("arbitrary",),
            vmem_limit_bytes=_vmem_limit(2 * (ts * d * 4 + ts * nb * tn * 4 + ts * d * 2) + nb * d * tn * 2 + 4 * ts * d * 4)),
    )(x, gain, s1p, shift, w)


def _matmul_norm_resid(a, w, xres, gate, gpost, name):
    s, k = a.shape
    d = w.shape[1]
    ts = _row_tile(s, 512)

    def body(a_ref, w_ref, x_ref, gate_ref, gp_ref, o_ref, xn_ref):
        o = _dot(a_ref[...], w_ref[...])
        o_ref[...] = o
        r = lax.rsqrt(_rmean(o * o) + EPS)
        xn_ref[...] = x_ref[...] + gate_ref[...] * ((o * r) * gp_ref[...])

    vec = pl.BlockSpec((1, d), lambda i: (0, 0))
    row = pl.BlockSpec((ts, d), lambda i: (i, 0))
    return _hbm_call(
        body, name=name,
        grid=(s // ts,),
        in_specs=[pl.BlockSpec((ts, k), lambda i: (i, 0)),
                  pl.BlockSpec((k, d), lambda i: (0, 0), pipeline_mode=pl.Buffered(1)), row, vec, vec],
        out_specs=[row, row],
        out_shape=[jax.ShapeDtypeStruct((s, d), F32)] * 2,
        compiler_params=pltpu.CompilerParams(
            dimension_semantics=("arbitrary",),
            vmem_limit_bytes=_vmem_limit(2 * (ts * k * 2 + 3 * ts * d * 4) + k * d * 2 + 4 * ts * d * 4)),
    )(a, w, xres, gate, gpost)


def _wgrad(a, b, acols, bcols, out_struct, out_index, out_block, name):
    s = a.shape[0]
    ts = _row_tile(s, 1024)
    aw, afn = acols
    bw, bfn = bcols
    nj = out_index[0]
    oidx = out_index[1]

    def body(a_ref, b_ref, o_ref, acc):
        i = pl.program_id(1)

        @pl.when(i == 0)
        def _():
            acc[...] = jnp.zeros_like(acc)

        acc[...] += _dot_tn(a_ref[...], b_ref[...])

        @pl.when(i == pl.num_programs(1) - 1)
        def _():
            o_ref[...] = acc[...].astype(o_ref.dtype)

    return _hbm_call(
        body, name=name,
        grid=(nj, s // ts),
        in_specs=[pl.BlockSpec((ts, aw), lambda j, i: (i, afn(j))), pl.BlockSpec((ts, bw), lambda j, i: (i, bfn(j)))],
        out_specs=pl.BlockSpec(out_block, lambda j, i: oidx(j)),
        out_shape=out_struct,
        scratch_shapes=[pltpu.VMEM((aw, bw), F32)],
        compiler_params=pltpu.CompilerParams(
            dimension_semantics=("arbitrary", "arbitrary"),
            vmem_limit_bytes=_vmem_limit(2 * (ts * aw * 2 + ts * bw * 2) + 3 * aw * bw * 4 + ts * aw * 4)),
    )(a, b)


def _loss_head(xo, tgt):
    s, d = xo.shape
    ts = _row_tile(s, 512)

    def body(x_ref, t_ref, dx_ref, l_ref, acc):
        i = pl.program_id(0)

        @pl.when(i == 0)
        def _():
            acc[...] = jnp.zeros_like(acc)

        e = x_ref[...] - t_ref[...]
        dx_ref[...] = e * (1.0 / d)
        acc[...] += _rsum(e * e)

        @pl.when(i == pl.num_programs(0) - 1)
        def _():
            tot = jnp.sum(acc[...], axis=-1, keepdims=True) * (0.5 / d)
            l_ref[...] = jnp.broadcast_to(tot, l_ref.shape)

    row = pl.BlockSpec((ts, d), lambda i: (i, 0))
    return _hbm_call(
        body, name="loss_head",
        grid=(s // ts,),
        in_specs=[row, row],
        out_specs=[row, pl.BlockSpec((1, SMALL_COLS), lambda i: (0, 0))],
        out_shape=[jax.ShapeDtypeStruct((s, d), F32), jax.ShapeDtypeStruct((1, SMALL_COLS), F32)],
        scratch_shapes=[pltpu.VMEM((1, d), F32)],
        compiler_params=pltpu.CompilerParams(dimension_semantics=("arbitrary",)),
    )(xo, tgt)


def _resid_bwd_matmul(dxn, o, gate, gpost, w, name):
    s, d = dxn.shape
    k = w.shape[0]
    ts = _row_tile(s, 512)

    def body(dx_ref, o_ref, gate_ref, gp_ref, w_ref, do_ref, da_ref, dgate_ref, dgp_ref):
        i = pl.program_id(0)

        @pl.when(i == 0)
        def _():
            dgate_ref[...] = jnp.zeros_like(dgate_ref)
            dgp_ref[...] = jnp.zeros_like(dgp_ref)

        dx = dx_ref[...]
        o = o_ref[...]
        r = lax.rsqrt(_rmean(o * o) + EPS)
        on = o * r
        dgate_ref[...] += _rsum(dx * (on * gp_ref[...]))
        don = dx * gate_ref[...]
        dgp_ref[...] += _rsum(don * on)
        t = don * gp_ref[...]
        do = (r * (t - on * _rmean(t * on))).astype(BF16)
        do_ref[...] = do
        da_ref[...] = _dot_nt(do, w_ref[...])

    vec = pl.BlockSpec((1, d), lambda i: (0, 0))
    row = pl.BlockSpec((ts, d), lambda i: (i, 0))
    return _hbm_call(
        body, name=name,
        grid=(s // ts,),
        in_specs=[row, row, vec, vec, pl.BlockSpec((k, d), lambda i: (0, 0), pipeline_mode=pl.Buffered(1))],
        out_specs=[row, pl.BlockSpec((ts, k), lambda i: (i, 0)), vec, vec],
        out_shape=[jax.ShapeDtypeStruct((s, d), BF16), jax.ShapeDtypeStruct((s, k), F32),
                   jax.ShapeDtypeStruct((1, d), F32), jax.ShapeDtypeStruct((1, d), F32)],
        compiler_params=pltpu.CompilerParams(
            dimension_semantics=("arbitrary",),
            vmem_limit_bytes=_vmem_limit(2 * (2 * ts * d * 4 + ts * d * 2 + ts * k * 4) + d * k * 2 + 6 * ts * d * 4)),
    )(dxn, o, gate, gpost, w)


def _norm_mod_bwd(dh, xv, gain, s1p, dres):
    r = lax.rsqrt(_rmean(xv * xv) + EPS)
    xn = xv * r
    dshift = _rsum(dh)
    t = dh * xn
    dscale = _rsum(t * gain)
    dgain = _rsum(t * s1p)
    dxn = dh * (gain * s1p)
    dx = r * (dxn - xn * _rmean(dxn * xn)) + dres
    return dx, dshift, dscale, dgain


def _lane_lt(shape, bound):
    return lax.broadcasted_iota(jnp.int32, shape, 1) < bound


def _sgu_forward(z_ref, bd_ref, ng_ref, nb_ref, wm_ref, bias_ref, ts, ya_s, f_s):
    u = _gelu(z_ref[:, 0:SGU_WIDTH])
    v = _gelu(z_ref[:, SGU_WIDTH:2 * SGU_WIDTH])
    bd = bd_ref[...]
    vc = v - _group_mean(v, bd)
    rstd = lax.rsqrt(_group_mean(vc * vc, bd) + EPS)
    vhat = vc * rstd
    vn = (vhat * ng_ref[...] + nb_ref[...]).astype(BF16)
    left = _lane_lt((CHUNK, CHUNK), HEAD_DIM)
    for n in range(ts // CHUNK):
        rows = slice(n * CHUNK, (n + 1) * CHUNK)
        for p in range(SGU_WIDTH // CHUNK):
            cols = slice(p * CHUNK, (p + 1) * CHUNK)
            blk = vn[rows, cols]
            f = jnp.where(left, _dot(wm_ref[2 * p], blk), _dot(wm_ref[2 * p + 1], blk)) + bias_ref[:, cols]
            if f_s is not None:
                f_s[rows, cols] = f
            ya_s[rows, cols] = u[rows, cols] * f
    return u, vhat, rstd, vn


def _conv31_forward(z_ref, zh_ref, first, cw_ref, cb_ref, ts, ext_b, cbs):
    a = z_ref[:, 2 * SGU_WIDTH:2 * SGU_WIDTH + CONV_WIDTH]
    g = z_ref[:, 2 * SGU_WIDTH + CONV_WIDTH:2 * SGU_WIDTH + 2 * CONV_WIDTH]
    ah = zh_ref[:, 2 * SGU_WIDTH:2 * SGU_WIDTH + CONV_WIDTH]
    gh = zh_ref[:, 2 * SGU_WIDTH + CONV_WIDTH:2 * SGU_WIDTH + 2 * CONV_WIDTH]
    ext_b[pl.ds(0, MIX_HALO), :] = jnp.where(first, 0.0, ah * jax.nn.sigmoid(gh))
    ext_b[pl.ds(MIX_HALO, ts), :] = a * jax.nn.sigmoid(g)
    for r in range(ts // CONV_ROWS):
        acc = jnp.broadcast_to(cb_ref[...], (CONV_ROWS, CONV_WIDTH))
        for k in range(CONV_K):
            acc = acc + cw_ref[k:k + 1, :] * ext_b[pl.ds(MIX_HALO - (CONV_K - 1) + k + r * CONV_ROWS, CONV_ROWS), :]
        cbs[pl.ds(r * CONV_ROWS, CONV_ROWS), :] = acc


def _pool_counts(i, ts):
    pos1 = (i * ts + 1 + lax.broadcasted_iota(jnp.int32, (ts, POOL_WIDTH), 0)).astype(F32)
    lane = lax.broadcasted_iota(jnp.int32, (ts, POOL_WIDTH), 1)
    gdim = POOL_WIDTH // len(POOL_WINDOWS)
    win = jnp.where(lane < gdim, float(POOL_WINDOWS[0]),
                    jnp.where(lane < 2 * gdim, float(POOL_WINDOWS[1]),
                              jnp.where(lane < 3 * gdim, float(POOL_WINDOWS[2]), float(POOL_WINDOWS[3]))))
    return jnp.minimum(pos1, win)


def _window_sums(ext, base, ts, sign):
    lane = lax.broadcasted_iota(jnp.int32, (ts, POOL_WIDTH), 1)
    gdim = POOL_WIDTH // len(POOL_WINDOWS)
    run = jnp.zeros((ts, POOL_WIDTH), F32)
    out = jnp.zeros((ts, POOL_WIDTH), F32)
    for m in range(POOL_WINDOWS[-1]):
        run = run + ext[pl.ds(base + sign * m, ts), :]
        for gi, win in enumerate(POOL_WINDOWS):
            if m == win - 1:
                out = jnp.where((lane >= gi * gdim) & (lane < (gi + 1) * gdim), run, out)
    return out


def _pool_forward(z_ref, zh_ref, first, i, ts, ext_c):
    c0 = 2 * SGU_WIDTH + 2 * CONV_WIDTH
    zc = z_ref[:, c0:c0 + POOL_WIDTH]
    ext_c[pl.ds(0, MIX_HALO), :] = jnp.where(first, 0.0, zh_ref[:, c0:c0 + POOL_WIDTH])
    ext_c[pl.ds(MIX_HALO, ts), :] = zc
    sums = _window_sums(ext_c, MIX_HALO, ts, -1)
    return sums / _pool_counts(i, ts) - zc


def _layer_norm_rows(v):
    mu = _rmean(v)
    vc = v - mu
    rstd = lax.rsqrt(_rmean(vc * vc) + EPS)
    return vc * rstd, rstd


def _mixer_specs(s, ts, width):
    nbh = ts // MIX_HALO
    tile = pl.BlockSpec((ts, width), lambda i: (i, 0))
    prev = pl.BlockSpec((MIX_HALO, width), lambda i: (jnp.maximum(i * nbh - 1, 0), 0))
    nxt = pl.BlockSpec((MIX_HALO, width), lambda i: (jnp.minimum((i + 1) * nbh, s // MIX_HALO - 1), 0))
    return tile, prev, nxt


def _mixer_fwd(z, mp, name):
    s, inw = z.shape
    d = SGU_WIDTH + CONV_WIDTH + POOL_WIDTH
    ts = _row_tile(s, 256)

    def body(z_ref, zh_ref, bd_ref, ng_ref, nb_ref, wm_ref, bias_ref, cw_ref, cb_ref, cng_ref, cnb_ref,
             pw_ref, ps_ref, bg_ref, y_ref, cbs, dpool_ref, ya_s, ext_b, ext_c):
        i = pl.program_id(0)
        first = i == 0
        _sgu_forward(z_ref, bd_ref, ng_ref, nb_ref, wm_ref, bias_ref, ts, ya_s, None)
        ya = ya_s[...]
        ra = lax.rsqrt(_rmean(ya * ya) + EPS)
        y_ref[:, 0:SGU_WIDTH] = ((ya * ra) * bg_ref[:, 0:SGU_WIDTH]).astype(BF16)

        _conv31_forward(z_ref, zh_ref, first, cw_ref, cb_ref, ts, ext_b, cbs)
        chat, _ = _layer_norm_rows(cbs[...])
        lin = chat * cng_ref[...] + cnb_ref[...]
        yb = lin * jax.nn.sigmoid(lin)
        rb = lax.rsqrt(_rmean(yb * yb) + EPS)
        y_ref[:, SGU_WIDTH:SGU_WIDTH + CONV_WIDTH] = ((yb * rb) * bg_ref[:, SGU_WIDTH:SGU_WIDTH + CONV_WIDTH]).astype(BF16)

        dpool = _pool_forward(z_ref, zh_ref, first, i, ts, ext_c)
        dpool_ref[...] = dpool
        yc = _dot(dpool.astype(BF16), pw_ref[...]) * ps_ref[...]
        rc = lax.rsqrt(_rmean(yc * yc) + EPS)
        y_ref[:, SGU_WIDTH + CONV_WIDTH:d] = ((yc * rc) * bg_ref[:, SGU_WIDTH + CONV_WIDTH:d]).astype(BF16)

    tile, prev, _ = _mixer_specs(s, ts, inw)
    consts = [mp["bd"], mp["ng"], mp["nb"], mp["wm"], mp["bias"], mp["cw"], mp["cb"], mp["cng"], mp["cnb"],
              mp["pw"], mp["ps"], mp["bg"]]
    return _hbm_call(
        body, name=name,
        grid=(s // ts,),
        in_specs=[tile, prev] + [_const_spec(c.shape) for c in consts],
        out_specs=[pl.BlockSpec((ts, d), lambda i: (i, 0)), pl.BlockSpec((ts, CONV_WIDTH), lambda i: (i, 0)),
                   pl.BlockSpec((ts, POOL_WIDTH), lambda i: (i, 0))],
        out_shape=[jax.ShapeDtypeStruct((s, d), BF16), jax.ShapeDtypeStruct((s, CONV_WIDTH), F32),
                   jax.ShapeDtypeStruct((s, POOL_WIDTH), F32)],
        scratch_shapes=[pltpu.VMEM((ts, SGU_WIDTH), F32), pltpu.VMEM((ts + MIX_HALO, CONV_WIDTH), F32),
                        pltpu.VMEM((ts + MIX_HALO, POOL_WIDTH), F32)],
        compiler_params=pltpu.CompilerParams(dimension_semantics=("arbitrary",),
                                             vmem_limit_bytes=_vmem_limit(16 * ts * inw * 4)),
    )(z, z, *consts)


def _mixer_bwd_a(z, cb, dpool, dy, mp, name):
    s, inw = z.shape
    d = SGU_WIDTH + CONV_WIDTH + POOL_WIDTH
    ts = _row_tile(s, 256)
    nchunk = ts // CHUNK

    def rms_bwd(dyn, y, g):
        r = lax.rsqrt(_rmean(y * y) + EPS)
        yn = y * r
        dg = _rsum(dyn * yn)
        t = dyn * g
        return r * (t - yn * _rmean(t * yn)), dg

    def body(z_ref, cbs, dpool_ref, dy_ref, bd_ref, ng_ref, nb_ref, wm_ref, wmt_ref, bias_ref, cng_ref, cnb_ref,
             pw_ref, pwt_ref, ps_ref, bg_ref,
             dza_ref, dcb_ref, dd_ref, dbg_ref, dwm_ref, dbias_ref, dng_ref, dnb_ref, dcng_ref, dcnb_ref, dps_ref, dpw_ref,
             ya_s, f_s, dvn_s):
        i = pl.program_id(0)
        first = i == 0

        @pl.when(first)
        def _():
            for ref in (dwm_ref, dbias_ref, dng_ref, dnb_ref, dcng_ref, dcnb_ref, dps_ref, dpw_ref):
                ref[...] = jnp.zeros_like(ref)

        u, vhat, rstd, vn = _sgu_forward(z_ref, bd_ref, ng_ref, nb_ref, wm_ref, bias_ref, ts, ya_s, f_s)
        dya, dbg_a = rms_bwd(dy_ref[:, 0:SGU_WIDTH], ya_s[...], bg_ref[:, 0:SGU_WIDTH])
        du = dya * f_s[...]
        df = dya * u
        dfb = df.astype(BF16)
        left = _lane_lt((CHUNK, CHUNK), HEAD_DIM)
        zero = jnp.zeros((CHUNK, CHUNK), BF16)
        dbias = jnp.zeros((CHUNK, SGU_WIDTH), F32)
        for n in range(nchunk):
            rows = slice(n * CHUNK, (n + 1) * CHUNK)
            dbias = dbias + df[rows, :]
            for p in range(SGU_WIDTH // CHUNK):
                cols = slice(p * CHUNK, (p + 1) * CHUNK)
                dblk = dfb[rows, cols]
                vblk = vn[rows, cols]
                dwm_ref[2 * p] += _dot_nt(jnp.where(left, dblk, zero), vblk)
                dwm_ref[2 * p + 1] += _dot_nt(jnp.where(left, zero, dblk), vblk)
                dvn_s[rows, cols] = jnp.where(left, _dot(wmt_ref[2 * p], dblk), _dot(wmt_ref[2 * p + 1], dblk))
        dbias_ref[...] += dbias
        dvn = dvn_s[...]
        dng_ref[...] += _rsum(dvn * vhat)
        dnb_ref[...] += _rsum(dvn)
        dvh = dvn * ng_ref[...]
        bd = bd_ref[...]
        dv = rstd * (dvh - _group_mean(dvh, bd) - vhat * _group_mean(dvh * vhat, bd))
        dza_ref[:, 0:SGU_WIDTH] = (du * _gelu_grad(z_ref[:, 0:SGU_WIDTH])).astype(BF16)
        dza_ref[:, SGU_WIDTH:2 * SGU_WIDTH] = (dv * _gelu_grad(z_ref[:, SGU_WIDTH:2 * SGU_WIDTH])).astype(BF16)

        chat, crstd = _layer_norm_rows(cbs[...])
        lin = chat * cng_ref[...] + cnb_ref[...]
        sl = jax.nn.sigmoid(lin)
        dyb, dbg_b = rms_bwd(dy_ref[:, SGU_WIDTH:SGU_WIDTH + CONV_WIDTH], lin * sl, bg_ref[:, SGU_WIDTH:SGU_WIDTH + CONV_WIDTH])
        dlin = dyb * (sl * (1.0 + lin * (1.0 - sl)))
        dcng_ref[...] += _rsum(dlin * chat)
        dcnb_ref[...] += _rsum(dlin)
        dch = dlin * cng_ref[...]
        dcb_ref[...] = crstd * (dch - _rmean(dch) - chat * _rmean(dch * chat))

        dpb = dpool_ref[...].astype(BF16)
        ycp = _dot(dpb, pw_ref[...])
        dyc, dbg_c = rms_bwd(dy_ref[:, SGU_WIDTH + CONV_WIDTH:d], ycp * ps_ref[...], bg_ref[:, SGU_WIDTH + CONV_WIDTH:d])
        dps_ref[...] += _rsum(dyc * ycp)
        dycp = (dyc * ps_ref[...]).astype(BF16)
        dpw_ref[...] += _dot_tn(dpb, dycp)
        dd_ref[...] = _dot(dycp, pwt_ref[...])

        @pl.when(first)
        def _():
            dbg_ref[...] = jnp.zeros_like(dbg_ref)

        dbg_ref[:, 0:SGU_WIDTH] += dbg_a
        dbg_ref[:, SGU_WIDTH:SGU_WIDTH + CONV_WIDTH] += dbg_b
        dbg_ref[:, SGU_WIDTH + CONV_WIDTH:d] += dbg_c

    tile, _, _ = _mixer_specs(s, ts, inw)
    consts = [mp["bd"], mp["ng"], mp["nb"], mp["wm"], mp["wmt"], mp["bias"], mp["cng"], mp["cnb"],
              mp["pw"], mp["pwt"], mp["ps"], mp["bg"]]
    acc_shapes = [(1, d), (2 * (SGU_WIDTH // CHUNK), CHUNK, CHUNK), (CHUNK, SGU_WIDTH), (1, SGU_WIDTH), (1, SGU_WIDTH),
                  (1, CONV_WIDTH), (1, CONV_WIDTH), (1, POOL_WIDTH), (POOL_WIDTH, POOL_WIDTH)]
    return _hbm_call(
        body, name=name,
        grid=(s // ts,),
        in_specs=[tile, pl.BlockSpec((ts, CONV_WIDTH), lambda i: (i, 0)), pl.BlockSpec((ts, POOL_WIDTH), lambda i: (i, 0)),
                  pl.BlockSpec((ts, d), lambda i: (i, 0))] + [_const_spec(c.shape) for c in consts],
        out_specs=[pl.BlockSpec((ts, 2 * SGU_WIDTH), lambda i: (i, 0)), pl.BlockSpec((ts, CONV_WIDTH), lambda i: (i, 0)),
                   pl.BlockSpec((ts, POOL_WIDTH), lambda i: (i, 0))] + [_const_spec(a) for a in acc_shapes],
        out_shape=[jax.ShapeDtypeStruct((s, 2 * SGU_WIDTH), BF16), jax.ShapeDtypeStruct((s, CONV_WIDTH), F32),
                   jax.ShapeDtypeStruct((s, POOL_WIDTH), F32)] + [jax.ShapeDtypeStruct(a, F32) for a in acc_shapes],
        scratch_shapes=[pltpu.VMEM((ts, SGU_WIDTH), F32), pltpu.VMEM((ts, SGU_WIDTH), F32), pltpu.VMEM((ts, SGU_WIDTH), F32)],
        compiler_params=pltpu.CompilerParams(dimension_semantics=("arbitrary",),
                                             vmem_limit_bytes=_vmem_limit(24 * ts * inw * 4)),
    )(z, cb, dpool, dy, *consts)


def _mixer_bwd_b(z, dza, dcb, dd, x, dres, gain, s1p, cw, w, name):
    s, inw = z.shape
    d = x.shape[1]
    ts = _row_tile(s, 256)
    c0 = 2 * SGU_WIDTH
    c1 = c0 + 2 * CONV_WIDTH

    def body(z_ref, zh_ref, dza_ref, dcb_ref, dcbn_ref, dd_ref, ddn_ref, x_ref, dres_ref, g_ref, s_ref, cw_ref, w_ref,
             dx_ref, dz_ref, dsh_ref, dsc_ref, dg_ref, dcw_ref, dcbias_ref, ext_b, ext_n, ext_e):
        i = pl.program_id(0)
        first = i == 0
        last = i == pl.num_programs(0) - 1

        @pl.when(first)
        def _():
            for ref in (dsh_ref, dsc_ref, dg_ref, dcw_ref, dcbias_ref):
                ref[...] = jnp.zeros_like(ref)

        a = z_ref[:, c0:c0 + CONV_WIDTH]
        sg = jax.nn.sigmoid(z_ref[:, c0 + CONV_WIDTH:c1])
        ah = zh_ref[:, c0:c0 + CONV_WIDTH]
        gh = zh_ref[:, c0 + CONV_WIDTH:c1]
        ext_b[pl.ds(0, MIX_HALO), :] = jnp.where(first, 0.0, ah * jax.nn.sigmoid(gh))
        ext_b[pl.ds(MIX_HALO, ts), :] = a * sg
        dcbv = dcb_ref[...]
        dcbias_ref[...] += _rsum(dcbv)
        for k in range(CONV_K):
            dcw_ref[k:k + 1, :] += _rsum(dcbv * ext_b[pl.ds(MIX_HALO - (CONV_K - 1) + k, ts), :])

        ext_n[pl.ds(0, ts), :] = dcbv
        ext_n[pl.ds(ts, MIX_HALO), :] = jnp.where(last, 0.0, dcbn_ref[...])
        for r in range(ts // CONV_ROWS):
            acc = jnp.zeros((CONV_ROWS, CONV_WIDTH), F32)
            for k in range(CONV_K):
                acc = acc + cw_ref[k:k + 1, :] * ext_n[pl.ds(CONV_K - 1 - k + r * CONV_ROWS, CONV_ROWS), :]
            rows = pl.ds(r * CONV_ROWS, CONV_ROWS)
            ar = z_ref[rows, c0:c0 + CONV_WIDTH]
            sr = jax.nn.sigmoid(z_ref[rows, c0 + CONV_WIDTH:c1])
            dz_ref[rows, c0:c0 + CONV_WIDTH] = (acc * sr).astype(BF16)
            dz_ref[rows, c0 + CONV_WIDTH:c1] = (acc * ar * sr * (1.0 - sr)).astype(BF16)

        ddv = dd_ref[...]
        ext_e[pl.ds(0, ts), :] = ddv / _pool_counts(i, ts)
        nh = (i + 1) * ts + lax.broadcasted_iota(jnp.int32, (MIX_HALO, POOL_WIDTH), 0)
        lane = lax.broadcasted_iota(jnp.int32, (MIX_HALO, POOL_WIDTH), 1)
        gdim = POOL_WIDTH // len(POOL_WINDOWS)
        winh = jnp.where(lane < gdim, float(POOL_WINDOWS[0]),
                         jnp.where(lane < 2 * gdim, float(POOL_WINDOWS[1]),
                                   jnp.where(lane < 3 * gdim, float(POOL_WINDOWS[2]), float(POOL_WINDOWS[3]))))
        cnth = jnp.minimum((nh + 1).astype(F32), winh)
        ext_e[pl.ds(ts, MIX_HALO), :] = jnp.where(last, 0.0, ddn_ref[...] / cnth)
        dz_ref[:, c1:inw] = (_window_sums(ext_e, 0, ts, 1) - ddv).astype(BF16)
        dz_ref[:, 0:c0] = dza_ref[...]

        dh = _dot_nt(dz_ref[...], w_ref[...])
        dx, dsh, dsc, dg = _norm_mod_bwd(dh, x_ref[...], g_ref[...], s_ref[...], dres_ref[...])
        dx_ref[...] = dx
        dsh_ref[...] += dsh
        dsc_ref[...] += dsc
        dg_ref[...] += dg

    tile, prev, _ = _mixer_specs(s, ts, inw)
    _, _, nxt_b = _mixer_specs(s, ts, CONV_WIDTH)
    _, _, nxt_c = _mixer_specs(s, ts, POOL_WIDTH)
    row = pl.BlockSpec((ts, d), lambda i: (i, 0))
    vec = pl.BlockSpec((1, d), lambda i: (0, 0))
    return _hbm_call(
        body, name=name,
        grid=(s // ts,),
        in_specs=[tile, prev, pl.BlockSpec((ts, c0), lambda i: (i, 0)),
                  pl.BlockSpec((ts, CONV_WIDTH), lambda i: (i, 0)), nxt_b,
                  pl.BlockSpec((ts, POOL_WIDTH), lambda i: (i, 0)), nxt_c,
                  row, row, vec, vec, _const_spec(cw.shape),
                  pl.BlockSpec(w.shape, lambda i: (0, 0), pipeline_mode=pl.Buffered(1))],
        out_specs=[row, pl.BlockSpec((ts, inw), lambda i: (i, 0)), vec, vec, vec,
                   _const_spec((CONV_K, CONV_WIDTH)), _const_spec((1, CONV_WIDTH))],
        out_shape=[jax.ShapeDtypeStruct((s, d), F32), jax.ShapeDtypeStruct((s, inw), BF16)]
        + [jax.ShapeDtypeStruct((1, d), F32)] * 3
        + [jax.ShapeDtypeStruct((CONV_K, CONV_WIDTH), F32), jax.ShapeDtypeStruct((1, CONV_WIDTH), F32)],
        scratch_shapes=[pltpu.VMEM((ts + MIX_HALO, CONV_WIDTH), F32), pltpu.VMEM((ts + MIX_HALO, CONV_WIDTH), F32),
                        pltpu.VMEM((ts + MIX_HALO, POOL_WIDTH), F32)],
        compiler_params=pltpu.CompilerParams(dimension_semantics=("arbitrary",),
                                             vmem_limit_bytes=_vmem_limit(16 * ts * inw * 4 + inw * d * 2)),
    )(z, z, dza, dcb, dcb, dd, dd, x, dres, gain, s1p, cw, w)


def _ffn_specs(s, ts, tc, half_blocks):
    nbh = ts // FFN_HALO

    def tile(off):
        return pl.BlockSpec((ts, tc), lambda j, i: (i, j + off))

    def prev(off):
        return pl.BlockSpec((FFN_HALO, tc), lambda j, i: (jnp.maximum(i * nbh - 1, 0), j + off))

    def vec(rows, off):
        return pl.BlockSpec((rows, tc), lambda j, i: (0, j + off))

    return tile, prev, vec


def _rows_before(cur, prev, k):
    row = lax.broadcasted_iota(jnp.int32, cur.shape, 0)
    return jnp.where(row >= k, pltpu.roll(cur, k, 0), pltpu.roll(prev, k, 0))


def _conv3_rows(cur, prev, w_ref, b_ref, cols):
    x1 = _rows_before(cur, prev, 1)
    x2 = _rows_before(cur, prev, 2)
    u = b_ref[:, cols] + w_ref[2:3, cols] * cur + w_ref[1:2, cols] * x1 + w_ref[0:1, cols] * x2
    return u, x2, x1


def _halo_chunk(h_ref, cols, first):
    h = jnp.where(first, 0.0, h_ref[:, cols])
    return jnp.concatenate([h] * (FFN_ROWS // FFN_HALO), axis=0)


def _ffn_act_fwd(p, cw, cb, name):
    s, f2 = p.shape
    f = f2 // 2
    tc = f // 2
    hb = f // tc
    ts = _row_tile(s, 256)

    def body(pg_ref, pgh_ref, pv_ref, pvh_ref, wg_ref, wv_ref, bg_ref, bv_ref, act_ref):
        first = pl.program_id(1) == 0
        for c in range(tc // LANES):
            cols = slice(c * LANES, (c + 1) * LANES)

            def chunk(r, carry, cols=cols):
                pg_prev, pv_prev = carry
                rows = pl.ds(pl.multiple_of(r * FFN_ROWS, FFN_ROWS), FFN_ROWS)
                pg = pg_ref[rows, cols]
                pv = pv_ref[rows, cols]
                ug, _, _ = _conv3_rows(pg, pg_prev, wg_ref, bg_ref, cols)
                uv, _, _ = _conv3_rows(pv, pv_prev, wv_ref, bv_ref, cols)
                act_ref[rows, cols] = (_gelu(ug) * uv).astype(BF16)
                return pg, pv

            def step(r, carry, chunk=chunk):
                for u in range(FFN_UNROLL):
                    carry = chunk(r * FFN_UNROLL + u, carry)
                return carry

            lax.fori_loop(0, ts // (FFN_ROWS * FFN_UNROLL), step,
                          (_halo_chunk(pgh_ref, cols, first), _halo_chunk(pvh_ref, cols, first)))

    tile, prev, vec = _ffn_specs(s, ts, tc, hb)
    return _hbm_call(
        body, name=name,
        grid=(hb, s // ts),
        in_specs=[tile(0), prev(0), tile(hb), prev(hb), vec(FFN_CONV_K, 0), vec(FFN_CONV_K, hb), vec(1, 0), vec(1, hb)],
        out_specs=pl.BlockSpec((ts, tc), lambda j, i: (i, j)),
        out_shape=jax.ShapeDtypeStruct((s, f), BF16),
        compiler_params=pltpu.CompilerParams(dimension_semantics=("arbitrary", "arbitrary"),
                                             vmem_limit_bytes=_vmem_limit(8 * ts * tc * 4)),
    )(p, p, p, p, cw, cw, cb, cb)


def _ffn_act_bwd(p, dact, cw, cb, name):
    s, f2 = p.shape
    f = f2 // 2
    tc = f // 2
    hb = f // tc
    ts = _row_tile(s, 256)

    def body(pg_ref, pgh_ref, pv_ref, pvh_ref, da_ref, wg_ref, wv_ref, bg_ref, bv_ref,
             dug_ref, duv_ref, dwg_ref, dwv_ref, dbg_ref, dbv_ref):
        first = pl.program_id(1) == 0

        @pl.when(first)
        def _():
            for ref in (dwg_ref, dwv_ref, dbg_ref, dbv_ref):
                ref[...] = jnp.zeros_like(ref)

        zero = jnp.zeros((FFN_ROWS, LANES), F32)
        for c in range(tc // LANES):
            cols = slice(c * LANES, (c + 1) * LANES)

            def chunk(r, carry, cols=cols):
                pg_prev, pv_prev, ag0, ag1, ag2, av0, av1, av2, sg, sv = carry
                rows = pl.ds(pl.multiple_of(r * FFN_ROWS, FFN_ROWS), FFN_ROWS)
                pg = pg_ref[rows, cols]
                pv = pv_ref[rows, cols]
                ug, pg2, pg1 = _conv3_rows(pg, pg_prev, wg_ref, bg_ref, cols)
                uv, pv2, pv1 = _conv3_rows(pv, pv_prev, wv_ref, bv_ref, cols)
                da = da_ref[rows, cols]
                dug = da * uv * _gelu_grad(ug)
                duv = da * _gelu(ug)
                dug_ref[rows, cols] = dug
                duv_ref[rows, cols] = duv
                return (pg, pv, ag0 + dug * pg2, ag1 + dug * pg1, ag2 + dug * pg,
                        av0 + duv * pv2, av1 + duv * pv1, av2 + duv * pv, sg + dug, sv + duv)

            def step(r, carry, chunk=chunk):
                for u in range(FFN_UNROLL):
                    carry = chunk(r * FFN_UNROLL + u, carry)
                return carry

            out = lax.fori_loop(0, ts // (FFN_ROWS * FFN_UNROLL), step,
                                (_halo_chunk(pgh_ref, cols, first), _halo_chunk(pvh_ref, cols, first)) + (zero,) * 8)
            for k in range(FFN_CONV_K):
                dwg_ref[k:k + 1, cols] += _rsum(out[2 + k])
                dwv_ref[k:k + 1, cols] += _rsum(out[5 + k])
            dbg_ref[:, cols] += _rsum(out[8])
            dbv_ref[:, cols] += _rsum(out[9])

    tile, prev, vec = _ffn_specs(s, ts, tc, hb)
    half = pl.BlockSpec((ts, tc), lambda j, i: (i, j))
    wacc = pl.BlockSpec((FFN_CONV_K, tc), lambda j, i: (0, j))
    bacc = pl.BlockSpec((1, tc), lambda j, i: (0, j))
    return _hbm_call(
        body, name=name,
        grid=(hb, s // ts),
        in_specs=[tile(0), prev(0), tile(hb), prev(hb), half, vec(FFN_CONV_K, 0), vec(FFN_CONV_K, hb), vec(1, 0), vec(1, hb)],
        out_specs=[half, half, wacc, wacc, bacc, bacc],
        out_shape=[jax.ShapeDtypeStruct((s, f), F32)] * 2 + [jax.ShapeDtypeStruct((FFN_CONV_K, f), F32)] * 2
        + [jax.ShapeDtypeStruct((1, f), F32)] * 2,
        compiler_params=pltpu.CompilerParams(dimension_semantics=("arbitrary", "arbitrary"),
                                             vmem_limit_bytes=_vmem_limit(12 * ts * tc * 4)),
    )(p, p, p, p, dact, cw, cw, cb, cb)


def _ffn_in_bwd(dug, duv, cw, w, x, dres, gain, s1p, name):
    s, f = dug.shape
    d = x.shape[1]
    ts = _row_tile(s, 256)
    tc = w.shape[2]
    assert f % tc == 0 and w.shape[0] * tc == 2 * f
    nbh = ts // FFN_HALO

    def body(dug_ref, dugn_ref, duv_ref, duvn_ref, cw_ref, w_ref, x_ref, dres_ref, g_ref, s_ref,
             dx_ref, dp_ref, dsh_ref, dsc_ref, dg_ref, ext):
        i = pl.program_id(0)
        last = i == pl.num_programs(0) - 1

        @pl.when(i == 0)
        def _():
            for ref in (dsh_ref, dsc_ref, dg_ref):
                ref[...] = jnp.zeros_like(ref)

        dh = jnp.zeros((ts, d), F32)
        for half, (t_ref, n_ref) in enumerate(((dug_ref, dugn_ref), (duv_ref, duvn_ref))):
            for cb in range(f // tc):
                cols = slice(cb * tc, (cb + 1) * tc)
                wcols = slice(half * f + cb * tc, half * f + (cb + 1) * tc)
                ext[pl.ds(0, ts), :] = t_ref[:, cols]
                ext[pl.ds(ts, FFN_HALO), :] = jnp.where(last, 0.0, n_ref[:, cols])
                acc = cw_ref[FFN_CONV_K - 1:FFN_CONV_K, wcols] * t_ref[:, cols]
                for k in range(FFN_CONV_K - 1):
                    acc = acc + cw_ref[k:k + 1, wcols] * ext[pl.ds(FFN_CONV_K - 1 - k, ts), :]
                dpb = acc.astype(BF16)
                dp_ref[:, wcols] = dpb
                dh = dh + _dot_nt(dpb, w_ref[half * (f // tc) + cb])
        dx, dsh, dsc, dg = _norm_mod_bwd(dh, x_ref[...], g_ref[...], s_ref[...], dres_ref[...])
        dx_ref[...] = dx
        dsh_ref[...] += dsh
        dsc_ref[...] += dsc
        dg_ref[...] += dg

    tile = pl.BlockSpec((ts, f), lambda i: (i, 0))
    nxt = pl.BlockSpec((FFN_HALO, f), lambda i: (jnp.minimum((i + 1) * nbh, s // FFN_HALO - 1), 0))
    row = pl.BlockSpec((ts, d), lambda i: (i, 0))
    vec = pl.BlockSpec((1, d), lambda i: (0, 0))
    return _hbm_call(
        body, name=name,
        grid=(s // ts,),
        in_specs=[tile, nxt, tile, nxt, _const_spec(cw.shape),
                  pl.BlockSpec(w.shape, lambda i: (0, 0, 0), pipeline_mode=pl.Buffered(1)), row, row, vec, vec],
        out_specs=[row, pl.BlockSpec((ts, 2 * f), lambda i: (i, 0)), vec, vec, vec],
        out_shape=[jax.ShapeDtypeStruct((s, d), F32), jax.ShapeDtypeStruct((s, 2 * f), BF16)] + [jax.ShapeDtypeStruct((1, d), F32)] * 3,
        scratch_shapes=[pltpu.VMEM((ts + FFN_HALO, tc), F32)],
        compiler_params=pltpu.CompilerParams(
            dimension_semantics=("arbitrary",),
            vmem_limit_bytes=_vmem_limit(4 * ts * f * 4 + 2 * f * d * 2 + 2 * ts * 2 * f * 2 + 12 * ts * d * 4 + 6 * ts * tc * 4)),
    )(dug, dug, duv, duv, cw, w, x, dres, gain, s1p)


def _adamw_math(w, g, m, v):
    m = ADAM_B1 * m + (1.0 - ADAM_B1) * g
    v = ADAM_B2 * v + (1.0 - ADAM_B2) * (g * g)
    m_hat = m / (1.0 - ADAM_B1 ** ADAM_STEP)
    v_hat = v / (1.0 - ADAM_B2 ** ADAM_STEP)
    delta = -ADAM_LR * (m_hat / (jnp.sqrt(v_hat) + ADAM_EPS) + ADAM_WD * w)
    return delta, m, v


def _adam_rows(rows, cols):
    want = max(8, (2 * 1024 * 1024 // (cols * 4)) // 8 * 8)
    tr = min(rows, want)
    while rows % tr:
        tr -= 8
    return tr


def _adamw(w, m, v, g_parts, name):
    shape = w.shape
    nl = shape[0] if w.ndim == 3 else 1
    r, c = shape[-2], shape[-1]
    tr = _adam_rows(r, c)
    ng = len(g_parts)

    def body(*refs):
        w_ref, m_ref, v_ref = refs[0:3]
        g_refs = refs[3:3 + ng]
        g_out, d_out, m_out, v_out = refs[3 + ng:]
        g = g_refs[0][...]
        for gr in g_refs[1:]:
            g = g + gr[...]
        delta, mn, vn = _adamw_math(w_ref[...], g, m_ref[...], v_ref[...])
        g_out[...] = g
        d_out[...] = delta
        m_out[...] = mn
        v_out[...] = vn

    blk = pl.BlockSpec((None, tr, c), lambda l, i: (l, i, 0))
    outs = _hbm_call(
        body, name=name,
        grid=(nl, r // tr),
        in_specs=[blk] * (3 + ng),
        out_specs=[blk] * 4,
        out_shape=[jax.ShapeDtypeStruct((nl, r, c), F32)] * 4,
        compiler_params=pltpu.CompilerParams(dimension_semantics=("arbitrary", "arbitrary"),
                                             vmem_limit_bytes=_vmem_limit(2 * (7 + ng) * tr * max(c, 128) * 4 + (8 << 20))),
    )(*[a.reshape(nl, r, c) for a in (w, m, v, *g_parts)])
    return [o.reshape(shape) for o in outs]


def _adamw_many(ws, ms, vs, gs, name):
    n = len(ws)

    def body(*refs):
        for k in range(n):
            w_ref, m_ref, v_ref, g_ref = refs[k], refs[n + k], refs[2 * n + k], refs[3 * n + k]
            delta, mn, vn = _adamw_math(w_ref[...], g_ref[...], m_ref[...], v_ref[...])
            refs[4 * n + 3 * k][...] = delta
            refs[4 * n + 3 * k + 1][...] = mn
            refs[4 * n + 3 * k + 2][...] = vn

    specs = [_const_spec(a.shape) for a in ws]
    outs = _hbm_call(
        body, name=name,
        grid=(1,),
        in_specs=specs * 4,
        out_specs=[sp for sp in specs for _ in range(3)],
        out_shape=[jax.ShapeDtypeStruct(a.shape, F32) for a in ws for _ in range(3)],
        compiler_params=pltpu.CompilerParams(dimension_semantics=("arbitrary",),
                                             vmem_limit_bytes=_vmem_limit(20 * sum(a.size for a in ws) * 4 + (8 << 20))),
    )(*ws, *ms, *vs, *gs)
    return [tuple(outs[3 * k:3 * k + 3]) for k in range(n)]


def _modw_adamw(sct, dmod, w, m, v, name):
    nl, d, n = w.shape
    tr = _row_tile(d, 128)

    def body(sct_ref, dm_ref, w_ref, m_ref, v_ref, g_out, d_out, m_out, v_out):
        sc = sct_ref[...].astype(BF16).astype(F32)
        dm = dm_ref[...].astype(BF16).astype(F32)
        g = sc[:, 0:1] * dm[0:1, :]
        for b in range(1, N_DEV):
            g = g + sc[:, b:b + 1] * dm[b:b + 1, :]
        delta, mn, vn = _adamw_math(w_ref[...], g, m_ref[...], v_ref[...])
        g_out[...] = g
        d_out[...] = delta
        m_out[...] = mn
        v_out[...] = vn

    blk = pl.BlockSpec((None, tr, n), lambda l, i: (l, i, 0))
    return _hbm_call(
        body, name=name,
        grid=(nl, d // tr),
        in_specs=[pl.BlockSpec((tr, N_DEV), lambda l, i: (i, 0)), pl.BlockSpec((None, N_DEV, n), lambda l, i: (l, 0, 0)),
                  blk, blk, blk],
        out_specs=[blk] * 4,
        out_shape=[jax.ShapeDtypeStruct((nl, d, n), F32)] * 4,
        compiler_params=pltpu.CompilerParams(dimension_semantics=("arbitrary", "arbitrary"),
                                             vmem_limit_bytes=_vmem_limit(2 * 8 * tr * n * 4 + (8 << 20))),
    )(sct, dmod, w, m, v)


def _reduce4(recvs, name):
    nl = len(recvs)
    shape = recvs[0].shape[1:]
    c = shape[-1]
    r = math.prod(shape[:-1])
    tr = _adam_rows(r, c)
    nt = r // tr

    def body(*refs):
        o_ref = refs[nl]
        for l in range(nl):
            @pl.when(pl.program_id(0) == l)
            def _():
                acc = refs[l][0].astype(F32)
                for k in range(1, N_CHIPS):
                    acc = acc + refs[l][k].astype(F32)
                o_ref[...] = acc

    def in_map(l):
        return lambda ll, i: (0, jnp.where(ll < l, 0, jnp.where(ll > l, nt - 1, i)), 0)

    return _hbm_call(
        body, name=name,
        grid=(nl, nt),
        in_specs=[pl.BlockSpec((N_CHIPS, tr, c), in_map(l)) for l in range(nl)],
        out_specs=pl.BlockSpec((None, tr, c), lambda ll, i: (ll, i, 0)),
        out_shape=jax.ShapeDtypeStruct((nl, r, c), F32),
        compiler_params=pltpu.CompilerParams(dimension_semantics=("arbitrary", "arbitrary"),
                                             vmem_limit_bytes=_vmem_limit(2 * 8 * nl * tr * max(c, 128) * 4 + (8 << 20))),
    )(*[rv.reshape(N_CHIPS, r, c) for rv in recvs]).reshape((nl,) + shape)


def _my_place():
    return lax.axis_index("x"), lax.axis_index("y"), lax.axis_index("c")


def _chip_coords(j):
    return j // 2, j % 2


def _mod_forward(c, mod_w, mod_b4):
    nl, d, n = mod_w.shape
    kc = 256

    def body(c_ref, w_ref, b_ref, mod_ref, sc_ref, cbuf, stage, s1, r1, s2, r2):
        mx, my, mc = _my_place()
        me = 4 * mx + 2 * my + mc
        q = 2 * mx + my
        cv = c_ref[...]
        cbuf[me] = jnp.broadcast_to(cv * jax.nn.sigmoid(cv), (8, d))
        sends = []
        for t in range(N_DEV):
            tx, ty = _chip_coords(t // 2)
            cp = pltpu.make_async_remote_copy(src_ref=cbuf.at[me], dst_ref=cbuf.at[me], send_sem=s1.at[t], recv_sem=r1.at[me],
                                              device_id=(tx, ty, t % 2), device_id_type=MESH)

            @pl.when(t != me)
            def _():
                cp.start()

            sends.append((t, cp))
        for t in range(N_DEV):
            @pl.when(t != me)
            def _():
                pltpu.make_async_remote_copy(src_ref=cbuf.at[t], dst_ref=cbuf.at[t], send_sem=s1.at[t], recv_sem=r1.at[t],
                                             device_id=(mx, my, mc), device_id_type=MESH).wait_recv()
        for t, cp in sends:
            @pl.when(t != me)
            def _():
                cp.wait_send()

        row = lax.broadcasted_iota(jnp.int32, (8, d), 0)
        sc_all = jnp.zeros((8, d), F32)
        for t in range(N_DEV):
            sc_all = sc_all + jnp.where(row == t, cbuf[t], 0.0)
        sc_ref[...] = sc_all
        rown = lax.broadcasted_iota(jnp.int32, (8, n), 0)
        for l in range(nl):
            acc = jnp.zeros((8, n), F32)
            for k0 in range(0, d, kc):
                acc = acc + _dot(sc_all[:, k0:k0 + kc].astype(BF16), w_ref[l, k0:k0 + kc, :].astype(BF16))
            acc = acc + b_ref[l, q]
            for j in range(N_CHIPS):
                jx, jy = _chip_coords(j)
                bdest = 4 * jx + 2 * jy + mc
                rowv = jnp.sum(jnp.where(rown == bdest, acc, 0.0), axis=0, keepdims=True)
                stage[j, l] = jnp.broadcast_to(rowv, (8, n))
        sends2 = []
        for j in range(N_CHIPS):
            jx, jy = _chip_coords(j)
            cp = pltpu.make_async_remote_copy(src_ref=stage.at[j], dst_ref=mod_ref.at[:, q], send_sem=s2.at[j], recv_sem=r2.at[q],
                                              device_id=(jx, jy, mc), device_id_type=MESH)

            @pl.when(j != q)
            def _():
                cp.start()

            @pl.when(j == q)
            def _():
                for l in range(nl):
                    mod_ref[l, j] = stage[j, l]

            sends2.append((j, cp))
        for j in range(N_CHIPS):
            @pl.when(j != q)
            def _():
                pltpu.make_async_remote_copy(src_ref=stage.at[j], dst_ref=mod_ref.at[:, j], send_sem=s2.at[j], recv_sem=r2.at[j],
                                             device_id=(mx, my, mc), device_id_type=MESH).wait_recv()
        for j, cp in sends2:
            @pl.when(j != q)
            def _():
                cp.wait_send()

    vm = pl.BlockSpec(memory_space=pltpu.VMEM)
    return pl.pallas_call(
        body, name="mod_forward",
        in_specs=[vm, vm, vm],
        out_specs=[vm, vm],
        out_shape=[jax.ShapeDtypeStruct((nl, N_CHIPS, 8, n), F32), jax.ShapeDtypeStruct((8, d), F32)],
        scratch_shapes=[pltpu.VMEM((N_DEV, 8, d), F32), pltpu.VMEM((N_CHIPS, nl, 8, n), F32),
                        pltpu.SemaphoreType.DMA((N_DEV,)), pltpu.SemaphoreType.DMA((N_DEV,)),
                        pltpu.SemaphoreType.DMA((N_CHIPS,)), pltpu.SemaphoreType.DMA((N_CHIPS,))],
        compiler_params=pltpu.CompilerParams(vmem_limit_bytes=_vmem_limit(2 * nl * d * n * 4 + (8 << 20))),
    )(c, mod_w, mod_b4)


_HBM_SPEC = pl.BlockSpec(memory_space=pltpu.HBM)
_SEM_SPEC = pl.BlockSpec(memory_space=pltpu.SEMAPHORE)
_DATAFLOW = pltpu.SideEffectType.DATAFLOW_SIDE_EFFECTING


def _slot(ref, scatter, j):
    return ref.at[j] if scatter else ref


def _exchange_start(groups, scatter, after, name):
    flat = [a for g in groups for a in g]
    na = len(flat)
    ng = len(groups)
    sizes = [len(g) for g in groups]
    first = [sum(sizes[:g]) for g in range(ng)]
    where = [(g, k) for g in range(ng) for k in range(sizes[g])]
    mx, my, _ = _my_place()
    qo = 2 * mx + my
    lands = []
    for a in flat:
        own = lax.dynamic_index_in_dim(a, qo, 0, keepdims=False) if scatter else a
        lands.append(lax.dynamic_update_index_in_dim(lax.empty((N_CHIPS,) + own.shape, a.dtype), own, qo, 0))

    def body(*refs):
        ins, lnd = refs[:na], refs[na:2 * na]
        ssems, rsems = refs[2 * na + 1:2 * na + 1 + ng], refs[2 * na + 1 + ng:2 * na + 1 + 2 * ng]
        token = refs[-1]
        mx, my, mc = _my_place()
        q = 2 * mx + my
        for j in range(N_CHIPS):
            jx, jy = _chip_coords(j)
            for a in range(na):
                g, k = where[a]

                @pl.when(j != q)
                def _():
                    pltpu.make_async_remote_copy(src_ref=_slot(ins[a], scatter, j), dst_ref=lnd[a].at[q],
                                                 send_sem=ssems[g].at[k * N_CHIPS + j], recv_sem=rsems[g].at[k * N_CHIPS + q],
                                                 device_id=(jx, jy, mc), device_id_type=MESH).start()
        token[...] = jnp.zeros_like(token)

    sem_shapes = [pltpu.SemaphoreType.DMA((n * N_CHIPS,)) for n in sizes]
    outs = pl.pallas_call(
        body, name=name,
        in_specs=[_HBM_SPEC] * (2 * na) + [pl.BlockSpec(memory_space=pl.ANY)],
        out_specs=[_SEM_SPEC] * (2 * ng) + [_HBM_SPEC] * (2 * na) + [pl.BlockSpec(memory_space=pltpu.VMEM)],
        out_shape=sem_shapes + sem_shapes + [pltpu.HBM(a.shape, a.dtype) for a in flat + lands]
        + [jax.ShapeDtypeStruct((8, 128), F32)],
        input_output_aliases={i: 2 * ng + i for i in range(2 * na)},
        compiler_params=pltpu.CompilerParams(has_side_effects=_DATAFLOW),
    )(*[pltpu.with_memory_space_constraint(a, pltpu.HBM) for a in flat + lands], after)
    ssems, rsems = outs[:ng], outs[ng:2 * ng]
    src_thru, land_thru = outs[2 * ng:2 * ng + na], outs[2 * ng + na:2 * ng + 2 * na]
    states = [(src_thru[first[g]:first[g] + sizes[g]], land_thru[first[g]:first[g] + sizes[g]], ssems[g], rsems[g])
              for g in range(ng)]
    return states, outs[-1]


def _exchange_wait(state, scatter, after, name):
    srcs, lands, ssem, rsem = state
    na = len(srcs)

    def body(*refs):
        ins, lnd = refs[:na], refs[na:2 * na]
        ssem_ref, rsem_ref = refs[2 * na], refs[2 * na + 1]
        mx, my, mc = _my_place()
        q = 2 * mx + my
        for j in range(N_CHIPS):
            for a in range(na):
                @pl.when(j != q)
                def _():
                    cp = pltpu.make_async_remote_copy(src_ref=_slot(ins[a], scatter, j), dst_ref=lnd[a].at[j],
                                                      send_sem=ssem_ref.at[a * N_CHIPS + j], recv_sem=rsem_ref.at[a * N_CHIPS + j],
                                                      device_id=(mx, my, mc), device_id_type=MESH)
                    cp.wait_send()
                    cp.wait_recv()

    outs = pl.pallas_call(
        body, name=name,
        in_specs=[_HBM_SPEC] * (2 * na) + [_SEM_SPEC, _SEM_SPEC, pl.BlockSpec(memory_space=pl.ANY)],
        out_specs=[_HBM_SPEC] * (2 * na),
        out_shape=[pltpu.HBM(a.shape, a.dtype) for a in list(srcs) + list(lands)],
        input_output_aliases={i: i for i in range(2 * na)},
        compiler_params=pltpu.CompilerParams(has_side_effects=_DATAFLOW),
    )(*srcs, *lands, ssem, rsem, after)
    return outs[na:]


def _sibling_copy(src, dst, ssem, rsem, a):
    mx, my, mc = _my_place()
    return pltpu.make_async_remote_copy(src_ref=src, dst_ref=dst, send_sem=ssem.at[a], recv_sem=rsem.at[a],
                                        device_id=(mx, my, 1 - mc), device_id_type=MESH)


def _swap_start(arrs, after, name):
    na = len(arrs)
    lands = [lax.empty(a.shape, a.dtype) for a in arrs]

    def body(*refs):
        ins, lnd = refs[:na], refs[na:2 * na]
        ssem, rsem, token = refs[2 * na + 1], refs[2 * na + 2], refs[-1]
        for a in range(na):
            _sibling_copy(ins[a], lnd[a], ssem, rsem, a).start()
        token[...] = jnp.zeros_like(token)

    outs = pl.pallas_call(
        body, name=name,
        in_specs=[_HBM_SPEC] * (2 * na) + [pl.BlockSpec(memory_space=pl.ANY)],
        out_specs=[_SEM_SPEC] * 2 + [_HBM_SPEC] * (2 * na) + [pl.BlockSpec(memory_space=pltpu.VMEM)],
        out_shape=[pltpu.SemaphoreType.DMA((na,))] * 2 + [pltpu.HBM(a.shape, a.dtype) for a in list(arrs) + lands]
        + [jax.ShapeDtypeStruct((8, 128), F32)],
        input_output_aliases={i: 2 + i for i in range(2 * na)},
        compiler_params=pltpu.CompilerParams(has_side_effects=_DATAFLOW),
    )(*[pltpu.with_memory_space_constraint(a, pltpu.HBM) for a in list(arrs) + lands], after)
    return (outs[2:2 + na], outs[2 + na:2 + 2 * na], outs[0], outs[1]), outs[-1]


def _swap_wait(state, after, name):
    srcs, lands, ssem, rsem = state
    na = len(srcs)

    def body(*refs):
        ins, lnd = refs[:na], refs[na:2 * na]
        ssem_ref, rsem_ref = refs[2 * na], refs[2 * na + 1]
        for a in range(na):
            cp = _sibling_copy(ins[a], lnd[a], ssem_ref, rsem_ref, a)
            cp.wait_send()
            cp.wait_recv()

    outs = pl.pallas_call(
        body, name=name,
        in_specs=[_HBM_SPEC] * (2 * na) + [_SEM_SPEC, _SEM_SPEC, pl.BlockSpec(memory_space=pl.ANY)],
        out_specs=[_HBM_SPEC] * (2 * na),
        out_shape=[pltpu.HBM(a.shape, a.dtype) for a in list(srcs) + list(lands)],
        input_output_aliases={i: i for i in range(2 * na)},
        compiler_params=pltpu.CompilerParams(has_side_effects=_DATAFLOW),
    )(*srcs, *lands, ssem, rsem, after)
    return outs[:na], outs[na:]


def _allreduce_small(rows_all, rows_sum):
    ra, c = rows_all.shape
    r = rows_sum.shape[0]
    ch = r // N_DEV
    assert ch % 8 == 0 and ch * N_DEV == r

    def body(a_ref, s_ref, all_ref, sum_ref, rbuf, red, sa, rva, sb, rvb, sc, rvc):
        mx, my, mc = _my_place()
        me = 4 * mx + 2 * my + mc
        mine = pl.ds(pl.multiple_of(me * ch, 8), ch)
        all_ref[me] = a_ref[...]
        rbuf[me] = s_ref[mine, :]

        def dev(t):
            tx, ty = _chip_coords(t // 2)
            return (tx, ty, t % 2)

        def everyone_else(fn):
            for t in range(N_DEV):
                @pl.when(t != me)
                def _():
                    fn(t)

        def copy_a(t, slot):
            return pltpu.make_async_remote_copy(src_ref=a_ref, dst_ref=all_ref.at[slot], send_sem=sa.at[t], recv_sem=rva.at[slot],
                                                device_id=dev(t), device_id_type=MESH)

        def copy_b(t, slot):
            return pltpu.make_async_remote_copy(src_ref=s_ref.at[pl.ds(t * ch, ch), :], dst_ref=rbuf.at[slot], send_sem=sb.at[t],
                                                recv_sem=rvb.at[slot], device_id=dev(t), device_id_type=MESH)

        def copy_c(t, chunk_start, slot):
            return pltpu.make_async_remote_copy(src_ref=red, dst_ref=sum_ref.at[pl.ds(chunk_start, ch), :], send_sem=sc.at[t],
                                                recv_sem=rvc.at[slot], device_id=dev(t), device_id_type=MESH)

        everyone_else(lambda t: (copy_a(t, me).start(), copy_b(t, me).start()))
        everyone_else(lambda t: (copy_a(t, t).wait_recv(), copy_b(t, t).wait_recv()))
        everyone_else(lambda t: (copy_a(t, me).wait_send(), copy_b(t, me).wait_send()))
        acc = rbuf[0]
        for t in range(1, N_DEV):
            acc = acc + rbuf[t]
        red[...] = acc
        sum_ref[mine, :] = acc
        everyone_else(lambda t: copy_c(t, pl.multiple_of(me * ch, 8), me).start())
        everyone_else(lambda t: copy_c(t, t * ch, t).wait_recv())
        everyone_else(lambda t: copy_c(t, pl.multiple_of(me * ch, 8), me).wait_send())

    vm = pl.BlockSpec(memory_space=pltpu.VMEM)
    return pl.pallas_call(
        body, name="allreduce_small",
        in_specs=[vm, vm],
        out_specs=[vm, vm],
        out_shape=[jax.ShapeDtypeStruct((N_DEV, ra, c), F32), jax.ShapeDtypeStruct((r, c), F32)],
        scratch_shapes=[pltpu.VMEM((N_DEV, ch, c), F32), pltpu.VMEM((ch, c), F32)] + [pltpu.SemaphoreType.DMA((N_DEV,))] * 6,
        compiler_params=pltpu.CompilerParams(vmem_limit_bytes=_vmem_limit((3 * r + 2 * N_DEV * ra) * c * 4 + (4 << 20))),
    )(rows_all, rows_sum)


def _pack(arrs, row_multiple=8):
    rows, layout, at = [], [], 0
    for a in arrs:
        n = a.size
        nr = -(-n // (8 * SMALL_COLS)) * 8
        flat = a.reshape(-1)
        if nr * SMALL_COLS != n:
            flat = jnp.pad(flat, (0, nr * SMALL_COLS - n))
        rows.append(flat.reshape(nr, SMALL_COLS))
        layout.append((at, nr, a.shape))
        at += nr
    pad = -at % row_multiple
    if pad:
        rows.append(jnp.zeros((pad, SMALL_COLS), F32))
    return jnp.concatenate(rows, axis=0), layout


def _unpack(buf, layout):
    out = []
    for at, nr, shape in layout:
        n = math.prod(shape)
        out.append(buf[at:at + nr].reshape(-1)[:n].reshape(shape))
    return out


SMALL_NAMES = ("mod_b", "mix_pre_g", "mix_post_g", "sgu_norm_g", "sgu_norm_b", "sgu_w", "sgu_b", "conv_b", "conv_norm_g",
               "conv_norm_b", "pool_w", "pool_scale", "branch_g", "ffn_pre_g", "ffn_post_g", "ffn_conv_b")
SHARDED_SMALL = ("conv_w", "ffn_conv_w")
WEIGHT_ORDER = ("mod_w", "mod_b", "mix_pre_g", "mix_post_g", "w_in", "sgu_norm_g", "sgu_norm_b", "sgu_w", "sgu_b", "conv_w",
                "conv_b", "conv_norm_g", "conv_norm_b", "pool_w", "pool_scale", "branch_g", "w_out", "ffn_pre_g", "ffn_post_g",
                "ffn_up", "ffn_conv_w", "ffn_conv_b", "ffn_down")


def _block_diag(blocks):
    n, a, b = blocks.shape
    eye = jnp.eye(n, dtype=blocks.dtype)
    return (eye[:, None, :, None] * blocks[:, :, None, :]).reshape(n * a, n * b)


def _diag_blocks(mat, n):
    a = mat.shape[0] // n
    return jnp.stack([mat[g * a:(g + 1) * a, g * a:(g + 1) * a] for g in range(n)])


def _step(x, c, loss_target, w, m, v):
    nl = w["mod_w"].shape[0]
    s, d = x.shape[1], x.shape[2]
    heads = SGU_WIDTH // HEAD_DIM
    groups = len(POOL_WINDOWS)
    mx, my, _ = _my_place()
    q = 2 * mx + my
    x0 = x.reshape(s, d)
    tgt = loss_target.reshape(s, d)

    nmod = w["mod_w"].shape[2]
    kin = w["w_in"].shape[2]
    inw = kin * N_CHIPS
    f2 = w["ffn_up"].shape[2] * N_CHIPS
    f = f2 // 2

    def wgroups(l):
        return [[w["w_in"][l].astype(BF16), w["conv_w"][l], w["ffn_conv_w"][l]], [w["w_out"][l].astype(BF16)],
                [w["ffn_up"][l].astype(BF16)], [w["ffn_down"][l].astype(BF16)]]

    gstates = {}
    (gstates[0, 0], gstates[0, 1]), gtoken = _exchange_start(wgroups(0)[:2], False, c, "gather_start_in_0")
    mod4, sc_all = _mod_forward(c + gtoken[0:1, 0:1], w["mod_w"], w["mod_b"].reshape(nl, N_CHIPS, 1, nmod))
    mod = mod4[:, :, 0, :].reshape(nl, N_MOD, 1, d)

    tril = jnp.tril(jnp.ones((CHUNK, CHUNK), bool))
    bd = _block_diag(jnp.ones((heads, HEAD_DIM, HEAD_DIM), BF16))

    def mixer_params(l, conv_w):
        wm = jnp.where(tril[None], w["sgu_w"][l], 0.0)
        pw = _block_diag(w["pool_w"][l])
        return dict(
            bd=bd, ng=w["sgu_norm_g"][l][None], nb=w["sgu_norm_b"][l][None],
            wm=wm.astype(BF16), wmt=jnp.swapaxes(wm, 1, 2).astype(BF16),
            bias=jnp.repeat(w["sgu_b"][l].T, HEAD_DIM, axis=1),
            cw=conv_w, cb=w["conv_b"][l][None], cng=w["conv_norm_g"][l][None], cnb=w["conv_norm_b"][l][None],
            pw=pw.astype(BF16), pwt=pw.T.astype(BF16), ps=w["pool_scale"][l][None], bg=w["branch_g"][l][None])

    saved = []
    xl = x0
    arrived = {0: list(_exchange_wait(gstates[0, 0], False, mod4, "gather_wait_in_0"))}
    arrived[0] += list(_exchange_wait(gstates[0, 1], False, arrived[0][0], "gather_wait_out_0"))
    for l in range(nl):
        sh1, sc1, g1, sh2, sc2, g2 = [mod[l, k] for k in range(N_MOD)]
        gpre1, gpost1 = w["mix_pre_g"][l][None], w["mix_post_g"][l][None]
        gpre2, gpost2 = w["ffn_pre_g"][l][None], w["ffn_post_g"][l][None]
        fcb = w["ffn_conv_b"][l][None]
        sh1_after, bg_after, g1_after, sh2_after, fcb_after = sh1, w["branch_g"][l][None], g1, sh2, fcb
        g_win, g_cw, g_fcw = arrived[l][:3]
        if l == 0:
            (gstates[0, 2],), tok = _exchange_start(wgroups(0)[2:3], False, arrived[0][3], "gather_start_up_0")
            sh1_after = sh1 + tok[0:1, 0:1]
        w_in = jnp.transpose(g_win, (1, 0, 2)).reshape(d, inw)
        conv_w = jnp.transpose(g_cw, (1, 0, 2)).reshape(CONV_K, CONV_WIDTH)
        ffn_cw = jnp.transpose(g_fcw, (1, 0, 2)).reshape(FFN_CONV_K, f2)
        mp = mixer_params(l, conv_w)
        z, h1 = _norm_mod_matmul(xl, gpre1, 1.0 + sc1, sh1_after, w_in[None], f"mix_in_{l}")
        w_out = arrived[l][3].reshape(d, d)
        ycat, cbo, dpool = _mixer_fwd(z, dict(mp, bg=bg_after), f"mixer_fwd_{l}")
        (up,) = _exchange_wait(gstates[l, 2], False, ycat, f"gather_wait_up_{l}")
        if l == 0:
            (gstates[0, 3],), tok = _exchange_start(wgroups(0)[3:4], False, up, "gather_start_down_0")
            g1_after = g1 + tok[0:1, 0:1]
        o, x1 = _matmul_norm_resid(ycat, w_out, xl, g1_after, gpost1, f"mix_out_{l}")
        (g_down,) = _exchange_wait(gstates[l, 3], False, x1, f"gather_wait_down_{l}")
        down = g_down.reshape(f, d)
        if l + 1 < nl:
            (gstates[l + 1, 0], gstates[l + 1, 1]), tok = _exchange_start(wgroups(l + 1)[0:2], False, g_down,
                                                                        f"gather_start_in_{l + 1}")
            sh2_after = sh2 + tok[0:1, 0:1]
        p, h2 = _norm_mod_matmul(x1, gpre2, 1.0 + sc2, sh2_after, up, f"ffn_in_{l}")
        if l + 1 < nl:
            nxt = _exchange_wait(gstates[l + 1, 0], False, p, f"gather_wait_in_{l + 1}")
            nxt_out = _exchange_wait(gstates[l + 1, 1], False, nxt[0], f"gather_wait_out_{l + 1}")
            arrived[l + 1] = list(nxt) + list(nxt_out)
            (gstates[l + 1, 2], gstates[l + 1, 3]), tok = _exchange_start(wgroups(l + 1)[2:4], False, nxt_out[0],
                                                                        f"gather_start_up_{l + 1}")
            fcb_after = fcb + tok[0:1, 0:1]
        act = _ffn_act_fwd(p, ffn_cw, fcb_after, f"ffn_act_{l}")
        qo, x2 = _matmul_norm_resid(act, down, x1, g2, gpost2, f"ffn_out_{l}")
        saved.append(dict(x=xl, z=z, h1=h1, ycat=ycat, cbo=cbo, dpool=dpool, o=o, x1=x1, p=p, h2=h2, act=act, qo=qo, mp=mp, fcb=fcb,
                          w_in=w_in, w_out=w_out, up=up, down=down, ffn_cw=ffn_cw,
                          mods=(sh1, sc1, g1, sh2, sc2, g2), gains=(gpre1, gpost1, gpre2, gpost2)))
        xl = x2

    dx, loss_row = _loss_head(xl, tgt)

    small = {n: [None] * nl for n in SMALL_NAMES + SHARDED_SMALL}
    dmods = [None] * nl
    tn = f2 // N_CHIPS
    sstates = {}
    token = None
    for l in reversed(range(nl)):
        sv = saved[l]
        sh1, sc1, g1, sh2, sc2, g2 = sv["mods"]
        gpre1, gpost1, gpre2, gpost2 = sv["gains"]
        if token is not None:
            g2 = g2 + token[0:1, 0:1]
        dq, dact, dg2, dgpost2 = _resid_bwd_matmul(dx, sv["qo"], g2, gpost2, sv["down"], f"ffn_out_bwd_{l}")
        g_down = _wgrad(sv["act"], dq, (f, lambda j: 0), (d, lambda j: 0), jax.ShapeDtypeStruct((f, d), BF16),
                        (1, lambda j: (0, 0)), (f, d), f"wgrad_ffn_down_{l}")
        dug, duv, dfwg, dfwv, dfbg, dfbv = _ffn_act_bwd(sv["p"], dact, sv["ffn_cw"], sv["fcb"], f"ffn_act_bwd_{l}")
        dx1, dp, dsh2, dsc2, dgpre2 = _ffn_in_bwd(dug, duv, sv["ffn_cw"], sv["up"], sv["x1"], dx, gpre2, 1.0 + sc2,
                                                  f"ffn_in_bwd_{l}")
        g_up = _wgrad(sv["h2"], dp, (d, lambda j: 0), (tn, lambda j: j), jax.ShapeDtypeStruct((N_CHIPS, d, tn), BF16),
                      (N_CHIPS, lambda j: (j, 0, 0)), (None, d, tn), f"wgrad_ffn_up_{l}")
        (sstates[l, 0],), token = _exchange_start([[g_down.reshape(N_CHIPS, f // N_CHIPS, d), g_up]], True, g_up,
                                                  f"scatter_start_ffn_{l}")
        do, dycat, dg1, dgpost1 = _resid_bwd_matmul(dx1, sv["o"], g1 + token[0:1, 0:1], gpost1, sv["w_out"],
                                                    f"mix_out_bwd_{l}")
        g_out = _wgrad(sv["ycat"], do, (d, lambda j: 0), (d, lambda j: 0), jax.ShapeDtypeStruct((d, d), BF16),
                       (1, lambda j: (0, 0)), (d, d), f"wgrad_w_out_{l}")
        (sstates[l, 1],), token = _exchange_start([[g_out.reshape(N_CHIPS, d // N_CHIPS, d)]], True, g_out,
                                                  f"scatter_start_out_{l}")
        mp_after = dict(sv["mp"], bg=sv["mp"]["bg"] + token[0:1, 0:1])
        (dza, dcb, dd, dbg, dwm, dbias, dng, dnb, dcng, dcnb, dps, dpw) = _mixer_bwd_a(sv["z"], sv["cbo"], sv["dpool"], dycat, mp_after, f"mixer_bwd_a_{l}")
        dx, dz, dsh1, dsc1, dgpre1, dcw, dcbias = _mixer_bwd_b(
            sv["z"], dza, dcb, dd, sv["x"], dx1, gpre1, 1.0 + sc1, sv["mp"]["cw"], sv["w_in"], f"mixer_bwd_b_{l}")
        g_in = _wgrad(sv["h1"], dz, (d, lambda j: 0), (inw, lambda j: 0), jax.ShapeDtypeStruct((d, inw), BF16),
                      (1, lambda j: (0, 0)), (d, inw), f"wgrad_w_in_{l}")
        g_in_parts = jnp.transpose(g_in.reshape(d, N_CHIPS, kin), (1, 0, 2))
        if l > 0:
            (sstates[l, 2],), token = _exchange_start([[g_in_parts]], True, g_in_parts, f"scatter_start_in_{l}")

        dmods[l] = jnp.concatenate([dsh1, dsc1, dg1, dsh2, dsc2, dg2], axis=0)
        small["mix_pre_g"][l], small["mix_post_g"][l] = dgpre1[0], dgpost1[0]
        small["ffn_pre_g"][l], small["ffn_post_g"][l] = dgpre2[0], dgpost2[0]
        small["sgu_norm_g"][l], small["sgu_norm_b"][l] = dng[0], dnb[0]
        small["sgu_w"][l] = jnp.where(tril[None], dwm, 0.0)
        small["sgu_b"][l] = dbias.reshape(CHUNK, heads, HEAD_DIM).sum(-1).T
        small["conv_b"][l], small["conv_norm_g"][l], small["conv_norm_b"][l] = dcbias[0], dcng[0], dcnb[0]
        small["pool_w"][l], small["pool_scale"][l], small["branch_g"][l] = _diag_blocks(dpw, groups), dps[0], dbg[0]
        small["ffn_conv_b"][l] = jnp.concatenate([dfbg[0], dfbv[0]])
        small["conv_w"][l] = dcw
        small["ffn_conv_w"][l] = jnp.concatenate([dfwg, dfwv], axis=1)

    names = [n for n in SMALL_NAMES if n != "mod_b"] + list(SHARDED_SMALL)
    dmod_rows, _ = _pack([jnp.stack(dmods)])
    packed, layout = _pack([jnp.stack(dmods), loss_row] + [jnp.stack(small[n]) for n in names], 8 * N_DEV)
    gathered, summed = _allreduce_small(dmod_rows, packed)
    (sstates[0, 2],), token = _exchange_start([[g_in_parts]], True, summed, "scatter_start_in_0")
    parts = _unpack(summed, layout)
    loss = parts[1][0, 0]
    gsmall = dict(zip(names, parts[2:]))
    gsmall["mod_b"] = parts[0].reshape(nl, N_MOD * d)
    dmod_all = gathered[:, :nl * N_MOD].reshape(N_DEV, nl, N_MOD * d)
    dmod_mine = jnp.transpose(lax.dynamic_slice_in_dim(dmod_all, q * nmod, nmod, axis=2), (1, 0, 2))

    grads, deltas, new_m, new_v = {}, {}, {}, {}

    def put(name, res):
        grads[name], deltas[name], new_m[name], new_v[name] = res

    recv = dict(w_in=[None] * nl, w_out=[None] * nl, ffn_up=[None] * nl, ffn_down=[None] * nl)
    done = token
    for l in reversed(range(nl)):
        recv["ffn_down"][l], recv["ffn_up"][l] = _exchange_wait(sstates[l, 0], True, done, f"scatter_wait_ffn_{l}")
        (recv["w_out"][l],) = _exchange_wait(sstates[l, 1], True, recv["ffn_up"][l], f"scatter_wait_out_{l}")
        done = recv["w_out"][l]
    big = ("w_in", "w_out", "ffn_up", "ffn_down")
    mine = {n: _reduce4(recv[n], f"reduce4_{n}") for n in big[1:]}
    done = mine["ffn_down"]
    for l in reversed(range(nl)):
        (recv["w_in"][l],) = _exchange_wait(sstates[l, 2], True, done, f"scatter_wait_in_{l}")
        done = recv["w_in"][l]
    mine["w_in"] = _reduce4(recv["w_in"], "reduce4_w_in")
    swap_state, token = _swap_start([mine[n] for n in big], mine["w_in"], "swap_start")

    put("mod_w", _modw_adamw(sc_all.T + token[0:1, 0:1], dmod_mine, w["mod_w"], m["mod_w"], v["mod_w"], "adamw_mod_w"))

    gsmall["conv_w"] = lax.dynamic_slice_in_dim(gsmall["conv_w"], q * (CONV_WIDTH // N_CHIPS), CONV_WIDTH // N_CHIPS, axis=2)
    gsmall["ffn_conv_w"] = lax.dynamic_slice_in_dim(gsmall["ffn_conv_w"], q * (f2 // N_CHIPS), f2 // N_CHIPS, axis=2)
    snames = SMALL_NAMES + SHARDED_SMALL
    res = _adamw_many([w[n] for n in snames], [m[n] for n in snames], [v[n] for n in snames], [gsmall[n] for n in snames],
                      "adamw_small")
    for n, (d_, m_, v_) in zip(snames, res):
        put(n, (gsmall[n], d_, m_, v_))

    sent, theirs = _swap_wait(swap_state, grads["mod_w"], "swap_wait")
    for n, a, b in zip(big, sent, theirs):
        put(n, _adamw(w[n], m[n], v[n], [a, b], f"adamw_{n}"))

    return (loss, dx.reshape(1, s, d), *[grads[n] for n in WEIGHT_ORDER], *[deltas[n] for n in WEIGHT_ORDER],
            *[new_m[n] for n in WEIGHT_ORDER], *[new_v[n] for n in WEIGHT_ORDER])


def kernel(x, c, mod_w, mod_b, mix_pre_g, mix_post_g, w_in, sgu_norm_g, sgu_norm_b, sgu_w, sgu_b, conv_w, conv_b, conv_norm_g, conv_norm_b, pool_w, pool_scale, branch_g, w_out, ffn_pre_g, ffn_post_g, ffn_up, ffn_conv_w, ffn_conv_b, ffn_down, loss_target, m_mod_w, m_mod_b, m_mix_pre_g, m_mix_post_g, m_w_in, m_sgu_norm_g, m_sgu_norm_b, m_sgu_w, m_sgu_b, m_conv_w, m_conv_b, m_conv_norm_g, m_conv_norm_b, m_pool_w, m_pool_scale, m_branch_g, m_w_out, m_ffn_pre_g, m_ffn_post_g, m_ffn_up, m_ffn_conv_w, m_ffn_conv_b, m_ffn_down, v_mod_w, v_mod_b, v_mix_pre_g, v_mix_post_g, v_w_in, v_sgu_norm_g, v_sgu_norm_b, v_sgu_w, v_sgu_b, v_conv_w, v_conv_b, v_conv_norm_g, v_conv_norm_b, v_pool_w, v_pool_scale, v_branch_g, v_w_out, v_ffn_pre_g, v_ffn_post_g, v_ffn_up, v_ffn_conv_w, v_ffn_conv_b, v_ffn_down):
    w = dict(mod_w=mod_w, mod_b=mod_b, mix_pre_g=mix_pre_g, mix_post_g=mix_post_g, w_in=w_in, sgu_norm_g=sgu_norm_g,
             sgu_norm_b=sgu_norm_b, sgu_w=sgu_w, sgu_b=sgu_b, conv_w=conv_w, conv_b=conv_b, conv_norm_g=conv_norm_g,
             conv_norm_b=conv_norm_b, pool_w=pool_w, pool_scale=pool_scale, branch_g=branch_g, w_out=w_out,
             ffn_pre_g=ffn_pre_g, ffn_post_g=ffn_post_g, ffn_up=ffn_up, ffn_conv_w=ffn_conv_w, ffn_conv_b=ffn_conv_b,
             ffn_down=ffn_down)
    m = dict(mod_w=m_mod_w, mod_b=m_mod_b, mix_pre_g=m_mix_pre_g, mix_post_g=m_mix_post_g, w_in=m_w_in,
             sgu_norm_g=m_sgu_norm_g, sgu_norm_b=m_sgu_norm_b, sgu_w=m_sgu_w, sgu_b=m_sgu_b, conv_w=m_conv_w,
             conv_b=m_conv_b, conv_norm_g=m_conv_norm_g, conv_norm_b=m_conv_norm_b, pool_w=m_pool_w,
             pool_scale=m_pool_scale, branch_g=m_branch_g, w_out=m_w_out, ffn_pre_g=m_ffn_pre_g, ffn_post_g=m_ffn_post_g,
             ffn_up=m_ffn_up, ffn_conv_w=m_ffn_conv_w, ffn_conv_b=m_ffn_conv_b, ffn_down=m_ffn_down)
    v = dict(mod_w=v_mod_w, mod_b=v_mod_b, mix_pre_g=v_mix_pre_g, mix_post_g=v_mix_post_g, w_in=v_w_in,
             sgu_norm_g=v_sgu_norm_g, sgu_norm_b=v_sgu_norm_b, sgu_w=v_sgu_w, sgu_b=v_sgu_b, conv_w=v_conv_w,
             conv_b=v_conv_b, conv_norm_g=v_conv_norm_g, conv_norm_b=v_conv_norm_b, pool_w=v_pool_w,
             pool_scale=v_pool_scale, branch_g=v_branch_g, w_out=v_w_out, ffn_pre_g=v_ffn_pre_g, ffn_post_g=v_ffn_post_g,
             ffn_up=v_ffn_up, ffn_conv_w=v_ffn_conv_w, ffn_conv_b=v_ffn_conv_b, ffn_down=v_ffn_down)
    return _step(x, c, loss_target, w, m, v)
```

```python
import functools
import math

import jax
import jax.numpy as jnp
from jax import lax
from jax.experimental import pallas as pl
from jax.experimental.pallas import tpu as pltpu

F32 = jnp.float32
BF16 = jnp.bfloat16
MESH = pl.DeviceIdType.MESH

EPS = 1e-6
HEAD_DIM = 64
CHUNK = 128
SGU_WIDTH = 384
CONV_WIDTH = 384
POOL_WIDTH = 256
POOL_WINDOWS = (2, 4, 8, 16)
CONV_K = 31
FFN_CONV_K = 3
N_MOD = 6
N_CHIPS = 4
N_DEV = 8

ADAM_LR = 0.001
ADAM_B1 = 0.9
ADAM_B2 = 0.999
ADAM_EPS = 1e-08
ADAM_WD = 0.01
ADAM_STEP = 10

MIX_HALO = 32
FFN_HALO = 8
FFN_ROWS = 16
FFN_UNROLL = 8
LANES = 128
CONV_ROWS = 32
SMALL_COLS = 1024
VMEM_BYTES_V7X = 64 * 1024 * 1024


def _vmem_limit(estimate_bytes):
    return int(min(max(estimate_bytes, 16 * 1024 * 1024), VMEM_BYTES_V7X - 8 * 1024 * 1024))


def _row_tile(s, want):
    return want if s % want == 0 else math.gcd(s, want)


def _rsum(v):
    return jnp.sum(v, axis=0, keepdims=True)


def _rmean(v):
    return jnp.mean(v, axis=-1, keepdims=True)


def _gelu(v):
    k = math.sqrt(2.0 / math.pi)
    return 0.5 * v * (1.0 + jnp.tanh(k * (v + 0.044715 * v * v * v)))


def _gelu_grad(v):
    k = math.sqrt(2.0 / math.pi)
    t = jnp.tanh(k * (v + 0.044715 * v * v * v))
    return 0.5 * (1.0 + t) + 0.5 * v * (1.0 - t * t) * (k * (1.0 + 3.0 * 0.044715 * v * v))


def _dot(a, b):
    return jnp.dot(a, b, preferred_element_type=F32)


def _dot_nt(a, b):
    return lax.dot_general(a, b, (((1,), (1,)), ((), ())), preferred_element_type=F32)


def _dot_tn(a, b):
    return lax.dot_general(a, b, (((0,), (0,)), ((), ())), preferred_element_type=F32)


def _group_mean(v, bd):
    hi = v.astype(BF16)
    lo = (v - hi.astype(F32)).astype(BF16)
    return (_dot(hi, bd) + _dot(lo, bd)) * (1.0 / HEAD_DIM)


def _const_spec(shape):
    nd = len(shape)
    return pl.BlockSpec(shape, lambda *_: (0,) * nd)


def _hbm_call(body, **kw):
    call = pl.pallas_call(body, **kw)
    return lambda *args: call(*[pltpu.with_memory_space_constraint(a, pltpu.HBM) for a in args])


def _norm_mod_matmul(x, gain, s1p, shift, w, name):
    s, d = x.shape
    nb, _, tn = w.shape
    ts = _row_tile(s, 512 if nb * tn <= 2048 else 256)

    def body(x_ref, g_ref, s_ref, b_ref, w_ref, z_ref, h_ref):
        xv = x_ref[...]
        r = lax.rsqrt(_rmean(xv * xv) + EPS)
        h = ((xv * r) * g_ref[...] * s_ref[...] + b_ref[...]).astype(BF16)
        h_ref[...] = h
        for j in range(nb):
            z_ref[:, j * tn:(j + 1) * tn] = _dot(h, w_ref[j])

    vec = pl.BlockSpec((1, d), lambda i: (0, 0))
    return _hbm_call(
        body, name=name,
        grid=(s // ts,),
        in_specs=[pl.BlockSpec((ts, d), lambda i: (i, 0)), vec, vec, vec,
                  pl.BlockSpec((nb, d, tn), lambda i: (0, 0, 0), pipeline_mode=pl.Buffered(1))],
        out_specs=[pl.BlockSpec((ts, nb * tn), lambda i: (i, 0)), pl.BlockSpec((ts, d), lambda i: (i, 0))],
        out_shape=[jax.ShapeDtypeStruct((s, nb * tn), F32), jax.ShapeDtypeStruct((s, d), BF16)],
        compiler_params=pltpu.CompilerParams(
            dimension_semantics=("arbitrary",),
            vmem_limit_bytes=_vmem_limit(2 * (ts * d * 4 + ts * nb * tn * 4 + ts * d * 2) + nb * d * tn * 2 + 4 * ts * d * 4)),
    )(x, gain, s1p, shift, w)


def _matmul_norm_resid(a, w, xres, gate, gpost, name):
    s, k = a.shape
    d = w.shape[1]
    ts = _row_tile(s, 512)

    def body(a_ref, w_ref, x_ref, gate_ref, gp_ref, o_ref, xn_ref):
        o = _dot(a_ref[...], w_ref[...])
        o_ref[...] = o
        r = lax.rsqrt(_rmean(o * o) + EPS)
        xn_ref[...] = x_ref[...] + gate_ref[...] * ((o * r) * gp_ref[...])

    vec = pl.BlockSpec((1, d), lambda i: (0, 0))
    row = pl.BlockSpec((ts, d), lambda i: (i, 0))
    return _hbm_call(
        body, name=name,
        grid=(s // ts,),
        in_specs=[pl.BlockSpec((ts, k), lambda i: (i, 0)),
                  pl.BlockSpec((k, d), lambda i: (0, 0), pipeline_mode=pl.Buffered(1)), row, vec, vec],
        out_specs=[row, row],
        out_shape=[jax.ShapeDtypeStruct((s, d), F32)] * 2,
        compiler_params=pltpu.CompilerParams(
            dimension_semantics=("arbitrary",),
            vmem_limit_bytes=_vmem_limit(2 * (ts * k * 2 + 3 * ts * d * 4) + k * d * 2 + 4 * ts * d * 4)),
    )(a, w, xres, gate, gpost)


def _wgrad(a, b, acols, bcols, out_struct, out_index, out_block, name):
    s = a.shape[0]
    ts = _row_tile(s, 1024)
    aw, afn = acols
    bw, bfn = bcols
    nj = out_index[0]
    oidx = out_index[1]

    def body(a_ref, b_ref, o_ref, acc):
        i = pl.program_id(1)

        @pl.when(i == 0)
        def _():
            acc[...] = jnp.zeros_like(acc)

        acc[...] += _dot_tn(a_ref[...], b_ref[...])

        @pl.when(i == pl.num_programs(1) - 1)
        def _():
            o_ref[...] = acc[...].astype(o_ref.dtype)

    return _hbm_call(
        body, name=name,
        grid=(nj, s // ts),
        in_specs=[pl.BlockSpec((ts, aw), lambda j, i: (i, afn(j))), pl.BlockSpec((ts, bw), lambda j, i: (i, bfn(j)))],
        out_specs=pl.BlockSpec(out_block, lambda j, i: oidx(j)),
        out_shape=out_struct,
        scratch_shapes=[pltpu.VMEM((aw, bw), F32)],
        compiler_params=pltpu.CompilerParams(
            dimension_semantics=("arbitrary", "arbitrary"),
            vmem_limit_bytes=_vmem_limit(2 * (ts * aw * 2 + ts * bw * 2) + 3 * aw * bw * 4 + ts * aw * 4)),
    )(a, b)


def _loss_head(xo, tgt):
    s, d = xo.shape
    ts = _row_tile(s, 512)

    def body(x_ref, t_ref, dx_ref, l_ref, acc):
        i = pl.program_id(0)

        @pl.when(i == 0)
        def _():
            acc[...] = jnp.zeros_like(acc)

        e = x_ref[...] - t_ref[...]
        dx_ref[...] = e * (1.0 / d)
        acc[...] += _rsum(e * e)

        @pl.when(i == pl.num_programs(0) - 1)
        def _():
            tot = jnp.sum(acc[...], axis=-1, keepdims=True) * (0.5 / d)
            l_ref[...] = jnp.broadcast_to(tot, l_ref.shape)

    row = pl.BlockSpec((ts, d), lambda i: (i, 0))
    return _hbm_call(
        body, name="loss_head",
        grid=(s // ts,),
        in_specs=[row, row],
        out_specs=[row, pl.BlockSpec((1, SMALL_COLS), lambda i: (0, 0))],
        out_shape=[jax.ShapeDtypeStruct((s, d), F32), jax.ShapeDtypeStruct((1, SMALL_COLS), F32)],
        scratch_shapes=[pltpu.VMEM((1, d), F32)],
        compiler_params=pltpu.CompilerParams(dimension_semantics=("arbitrary",)),
    )(xo, tgt)


def _resid_bwd_matmul(dxn, o, gate, gpost, w, name):
    s, d = dxn.shape
    k = w.shape[0]
    ts = _row_tile(s, 512)

    def body(dx_ref, o_ref, gate_ref, gp_ref, w_ref, do_ref, da_ref, dgate_ref, dgp_ref):
        i = pl.program_id(0)

        @pl.when(i == 0)
        def _():
            dgate_ref[...] = jnp.zeros_like(dgate_ref)
            dgp_ref[...] = jnp.zeros_like(dgp_ref)

        dx = dx_ref[...]
        o = o_ref[...]
        r = lax.rsqrt(_rmean(o * o) + EPS)
        on = o * r
        dgate_ref[...] += _rsum(dx * (on * gp_ref[...]))
        don = dx * gate_ref[...]
        dgp_ref[...] += _rsum(don * on)
        t = don * gp_ref[...]
        do = (r * (t - on * _rmean(t * on))).astype(BF16)
        do_ref[...] = do
        da_ref[...] = _dot_nt(do, w_ref[...])

    vec = pl.BlockSpec((1, d), lambda i: (0, 0))
    row = pl.BlockSpec((ts, d), lambda i: (i, 0))
    return _hbm_call(
        body, name=name,
        grid=(s // ts,),
        in_specs=[row, row, vec, vec, pl.BlockSpec((k, d), lambda i: (0, 0), pipeline_mode=pl.Buffered(1))],
        out_specs=[row, pl.BlockSpec((ts, k), lambda i: (i, 0)), vec, vec],
        out_shape=[jax.ShapeDtypeStruct((s, d), BF16), jax.ShapeDtypeStruct((s, k), F32),
                   jax.ShapeDtypeStruct((1, d), F32), jax.ShapeDtypeStruct((1, d), F32)],
        compiler_params=pltpu.CompilerParams(
            dimension_semantics=("arbitrary",),
            vmem_limit_bytes=_vmem_limit(2 * (2 * ts * d * 4 + ts * d * 2 + ts * k * 4) + d * k * 2 + 6 * ts * d * 4)),
    )(dxn, o, gate, gpost, w)


def _norm_mod_bwd(dh, xv, gain, s1p, dres):
    r = lax.rsqrt(_rmean(xv * xv) + EPS)
    xn = xv * r
    dshift = _rsum(dh)
    t = dh * xn
    dscale = _rsum(t * gain)
    dgain = _rsum(t * s1p)
    dxn = dh * (gain * s1p)
    dx = r * (dxn - xn * _rmean(dxn * xn)) + dres
    return dx, dshift, dscale, dgain


def _lane_lt(shape, bound):
    return lax.broadcasted_iota(jnp.int32, shape, 1) < bound


def _sgu_forward(z_ref, bd_ref, ng_ref, nb_ref, wm_ref, bias_ref, ts, ya_s, f_s):
    u = _gelu(z_ref[:, 0:SGU_WIDTH])
    v = _gelu(z_ref[:, SGU_WIDTH:2 * SGU_WIDTH])
    bd = bd_ref[...]
    vc = v - _group_mean(v, bd)
    rstd = lax.rsqrt(_group_mean(vc * vc, bd) + EPS)
    vhat = vc * rstd
    vn = (vhat * ng_ref[...] + nb_ref[...]).astype(BF16)
    left = _lane_lt((CHUNK, CHUNK), HEAD_DIM)
    for n in range(ts // CHUNK):
        rows = slice(n * CHUNK, (n + 1) * CHUNK)
        for p in range(SGU_WIDTH // CHUNK):
            cols = slice(p * CHUNK, (p + 1) * CHUNK)
            blk = vn[rows, cols]
            f = jnp.where(left, _dot(wm_ref[2 * p], blk), _dot(wm_ref[2 * p + 1], blk)) + bias_ref[:, cols]
            if f_s is not None:
                f_s[rows, cols] = f
            ya_s[rows, cols] = u[rows, cols] * f
    return u, vhat, rstd, vn


def _conv31_forward(z_ref, zh_ref, first, cw_ref, cb_ref, ts, ext_b, cbs):
    a = z_ref[:, 2 * SGU_WIDTH:2 * SGU_WIDTH + CONV_WIDTH]
    g = z_ref[:, 2 * SGU_WIDTH + CONV_WIDTH:2 * SGU_WIDTH + 2 * CONV_WIDTH]
    ah = zh_ref[:, 2 * SGU_WIDTH:2 * SGU_WIDTH + CONV_WIDTH]
    gh = zh_ref[:, 2 * SGU_WIDTH + CONV_WIDTH:2 * SGU_WIDTH + 2 * CONV_WIDTH]
    ext_b[pl.ds(0, MIX_HALO), :] = jnp.where(first, 0.0, ah * jax.nn.sigmoid(gh))
    ext_b[pl.ds(MIX_HALO, ts), :] = a * jax.nn.sigmoid(g)
    for r in range(ts // CONV_ROWS):
        acc = jnp.broadcast_to(cb_ref[...], (CONV_ROWS, CONV_WIDTH))
        for k in range(CONV_K):
            acc = acc + cw_ref[k:k + 1, :] * ext_b[pl.ds(MIX_HALO - (CONV_K - 1) + k + r * CONV_ROWS, CONV_ROWS), :]
        cbs[pl.ds(r * CONV_ROWS, CONV_ROWS), :] = acc


def _pool_counts(i, ts):
    pos1 = (i * ts + 1 + lax.broadcasted_iota(jnp.int32, (ts, POOL_WIDTH), 0)).astype(F32)
    lane = lax.broadcasted_iota(jnp.int32, (ts, POOL_WIDTH), 1)
    gdim = POOL_WIDTH // len(POOL_WINDOWS)
    win = jnp.where(lane < gdim, float(POOL_WINDOWS[0]),
                    jnp.where(lane < 2 * gdim, float(POOL_WINDOWS[1]),
                              jnp.where(lane < 3 * gdim, float(POOL_WINDOWS[2]), float(POOL_WINDOWS[3]))))
    return jnp.minimum(pos1, win)


def _window_sums(ext, base, ts, sign):
    lane = lax.broadcasted_iota(jnp.int32, (ts, POOL_WIDTH), 1)
    gdim = POOL_WIDTH // len(POOL_WINDOWS)
    run = jnp.zeros((ts, POOL_WIDTH), F32)
    out = jnp.zeros((ts, POOL_WIDTH), F32)
    for m in range(POOL_WINDOWS[-1]):
        run = run + ext[pl.ds(base + sign * m, ts), :]
        for gi, win in enumerate(POOL_WINDOWS):
            if m == win - 1:
                out = jnp.where((lane >= gi * gdim) & (lane < (gi + 1) * gdim), run, out)
    return out


def _pool_forward(z_ref, zh_ref, first, i, ts, ext_c):
    c0 = 2 * SGU_WIDTH + 2 * CONV_WIDTH
    zc = z_ref[:, c0:c0 + POOL_WIDTH]
    ext_c[pl.ds(0, MIX_HALO), :] = jnp.where(first, 0.0, zh_ref[:, c0:c0 + POOL_WIDTH])
    ext_c[pl.ds(MIX_HALO, ts), :] = zc
    sums = _window_sums(ext_c, MIX_HALO, ts, -1)
    return sums / _pool_counts(i, ts) - zc


def _layer_norm_rows(v):
    mu = _rmean(v)
    vc = v - mu
    rstd = lax.rsqrt(_rmean(vc * vc) + EPS)
    return vc * rstd, rstd


def _mixer_specs(s, ts, width):
    nbh = ts // MIX_HALO
    tile = pl.BlockSpec((ts, width), lambda i: (i, 0))
    prev = pl.BlockSpec((MIX_HALO, width), lambda i: (jnp.maximum(i * nbh - 1, 0), 0))
    nxt = pl.BlockSpec((MIX_HALO, width), lambda i: (jnp.minimum((i + 1) * nbh, s // MIX_HALO - 1), 0))
    return tile, prev, nxt


def _mixer_fwd(z, mp, name):
    s, inw = z.shape
    d = SGU_WIDTH + CONV_WIDTH + POOL_WIDTH
    ts = _row_tile(s, 256)

    def body(z_ref, zh_ref, bd_ref, ng_ref, nb_ref, wm_ref, bias_ref, cw_ref, cb_ref, cng_ref, cnb_ref,
             pw_ref, ps_ref, bg_ref, y_ref, cbs, dpool_ref, ya_s, ext_b, ext_c):
        i = pl.program_id(0)
        first = i == 0
        _sgu_forward(z_ref, bd_ref, ng_ref, nb_ref, wm_ref, bias_ref, ts, ya_s, None)
        ya = ya_s[...]
        ra = lax.rsqrt(_rmean(ya * ya) + EPS)
        y_ref[:, 0:SGU_WIDTH] = ((ya * ra) * bg_ref[:, 0:SGU_WIDTH]).astype(BF16)

        _conv31_forward(z_ref, zh_ref, first, cw_ref, cb_ref, ts, ext_b, cbs)
        chat, _ = _layer_norm_rows(cbs[...])
        lin = chat * cng_ref[...] + cnb_ref[...]
        yb = lin * jax.nn.sigmoid(lin)
        rb = lax.rsqrt(_rmean(yb * yb) + EPS)
        y_ref[:, SGU_WIDTH:SGU_WIDTH + CONV_WIDTH] = ((yb * rb) * bg_ref[:, SGU_WIDTH:SGU_WIDTH + CONV_WIDTH]).astype(BF16)

        dpool = _pool_forward(z_ref, zh_ref, first, i, ts, ext_c)
        dpool_ref[...] = dpool
        yc = _dot(dpool.astype(BF16), pw_ref[...]) * ps_ref[...]
        rc = lax.rsqrt(_rmean(yc * yc) + EPS)
        y_ref[:, SGU_WIDTH + CONV_WIDTH:d] = ((yc * rc) * bg_ref[:, SGU_WIDTH + CONV_WIDTH:d]).astype(BF16)

    tile, prev, _ = _mixer_specs(s, ts, inw)
    consts = [mp["bd"], mp["ng"], mp["nb"], mp["wm"], mp["bias"], mp["cw"], mp["cb"], mp["cng"], mp["cnb"],
              mp["pw"], mp["ps"], mp["bg"]]
    return _hbm_call(
        body, name=name,
        grid=(s // ts,),
        in_specs=[tile, prev] + [_const_spec(c.shape) for c in consts],
        out_specs=[pl.BlockSpec((ts, d), lambda i: (i, 0)), pl.BlockSpec((ts, CONV_WIDTH), lambda i: (i, 0)),
                   pl.BlockSpec((ts, POOL_WIDTH), lambda i: (i, 0))],
        out_shape=[jax.ShapeDtypeStruct((s, d), BF16), jax.ShapeDtypeStruct((s, CONV_WIDTH), F32),
                   jax.ShapeDtypeStruct((s, POOL_WIDTH), F32)],
        scratch_shapes=[pltpu.VMEM((ts, SGU_WIDTH), F32), pltpu.VMEM((ts + MIX_HALO, CONV_WIDTH), F32),
                        pltpu.VMEM((ts + MIX_HALO, POOL_WIDTH), F32)],
        compiler_params=pltpu.CompilerParams(dimension_semantics=("arbitrary",),
                                             vmem_limit_bytes=_vmem_limit(16 * ts * inw * 4)),
    )(z, z, *consts)


def _mixer_bwd_a(z, cb, dpool, dy, mp, name):
    s, inw = z.shape
    d = SGU_WIDTH + CONV_WIDTH + POOL_WIDTH
    ts = _row_tile(s, 256)
    nchunk = ts // CHUNK

    def rms_bwd(dyn, y, g):
        r = lax.rsqrt(_rmean(y * y) + EPS)
        yn = y * r
        dg = _rsum(dyn * yn)
        t = dyn * g
        return r * (t - yn * _rmean(t * yn)), dg

    def body(z_ref, cbs, dpool_ref, dy_ref, bd_ref, ng_ref, nb_ref, wm_ref, wmt_ref, bias_ref, cng_ref, cnb_ref,
             pw_ref, pwt_ref, ps_ref, bg_ref,
             dza_ref, dcb_ref, dd_ref, dbg_ref, dwm_ref, dbias_ref, dng_ref, dnb_ref, dcng_ref, dcnb_ref, dps_ref, dpw_ref,
             ya_s, f_s, dvn_s):
        i = pl.program_id(0)
        first = i == 0

        @pl.when(first)
        def _():
            for ref in (dwm_ref, dbias_ref, dng_ref, dnb_ref, dcng_ref, dcnb_ref, dps_ref, dpw_ref):
                ref[...] = jnp.zeros_like(ref)

        u, vhat, rstd, vn = _sgu_forward(z_ref, bd_ref, ng_ref, nb_ref, wm_ref, bias_ref, ts, ya_s, f_s)
        dya, dbg_a = rms_bwd(dy_ref[:, 0:SGU_WIDTH], ya_s[...], bg_ref[:, 0:SGU_WIDTH])
        du = dya * f_s[...]
        df = dya * u
        dfb = df.astype(BF16)
        left = _lane_lt((CHUNK, CHUNK), HEAD_DIM)
        zero = jnp.zeros((CHUNK, CHUNK), BF16)
        dbias = jnp.zeros((CHUNK, SGU_WIDTH), F32)
        for n in range(nchunk):
            rows = slice(n * CHUNK, (n + 1) * CHUNK)
            dbias = dbias + df[rows, :]
            for p in range(SGU_WIDTH // CHUNK):
                cols = slice(p * CHUNK, (p + 1) * CHUNK)
                dblk = dfb[rows, cols]
                vblk = vn[rows, cols]
                dwm_ref[2 * p] += _dot_nt(jnp.where(left, dblk, zero), vblk)
                dwm_ref[2 * p + 1] += _dot_nt(jnp.where(left, zero, dblk), vblk)
                dvn_s[rows, cols] = jnp.where(left, _dot(wmt_ref[2 * p], dblk), _dot(wmt_ref[2 * p + 1], dblk))
        dbias_ref[...] += dbias
        dvn = dvn_s[...]
        dng_ref[...] += _rsum(dvn * vhat)
        dnb_ref[...] += _rsum(dvn)
        dvh = dvn * ng_ref[...]
        bd = bd_ref[...]
        dv = rstd * (dvh - _group_mean(dvh, bd) - vhat * _group_mean(dvh * vhat, bd))
        dza_ref[:, 0:SGU_WIDTH] = (du * _gelu_grad(z_ref[:, 0:SGU_WIDTH])).astype(BF16)
        dza_ref[:, SGU_WIDTH:2 * SGU_WIDTH] = (dv * _gelu_grad(z_ref[:, SGU_WIDTH:2 * SGU_WIDTH])).astype(BF16)

        chat, crstd = _layer_norm_rows(cbs[...])
        lin = chat * cng_ref[...] + cnb_ref[...]
        sl = jax.nn.sigmoid(lin)
        dyb, dbg_b = rms_bwd(dy_ref[:, SGU_WIDTH:SGU_WIDTH + CONV_WIDTH], lin * sl, bg_ref[:, SGU_WIDTH:SGU_WIDTH + CONV_WIDTH])
        dlin = dyb * (sl * (1.0 + lin * (1.0 - sl)))
        dcng_ref[...] += _rsum(dlin * chat)
        dcnb_ref[...] += _rsum(dlin)
        dch = dlin * cng_ref[...]
        dcb_ref[...] = crstd * (dch - _rmean(dch) - chat * _rmean(dch * chat))

        dpb = dpool_ref[...].astype(BF16)
        ycp = _dot(dpb, pw_ref[...])
        dyc, dbg_c = rms_bwd(dy_ref[:, SGU_WIDTH + CONV_WIDTH:d], ycp * ps_ref[...], bg_ref[:, SGU_WIDTH + CONV_WIDTH:d])
        dps_ref[...] += _rsum(dyc * ycp)
        dycp = (dyc * ps_ref[...]).astype(BF16)
        dpw_ref[...] += _dot_tn(dpb, dycp)
        dd_ref[...] = _dot(dycp, pwt_ref[...])

        @pl.when(first)
        def _():
            dbg_ref[...] = jnp.zeros_like(dbg_ref)

        dbg_ref[:, 0:SGU_WIDTH] += dbg_a
        dbg_ref[:, SGU_WIDTH:SGU_WIDTH + CONV_WIDTH] += dbg_b
        dbg_ref[:, SGU_WIDTH + CONV_WIDTH:d] += dbg_c

    tile, _, _ = _mixer_specs(s, ts, inw)
    consts = [mp["bd"], mp["ng"], mp["nb"], mp["wm"], mp["wmt"], mp["bias"], mp["cng"], mp["cnb"],
              mp["pw"], mp["pwt"], mp["ps"], mp["bg"]]
    acc_shapes = [(1, d), (2 * (SGU_WIDTH // CHUNK), CHUNK, CHUNK), (CHUNK, SGU_WIDTH), (1, SGU_WIDTH), (1, SGU_WIDTH),
                  (1, CONV_WIDTH), (1, CONV_WIDTH), (1, POOL_WIDTH), (POOL_WIDTH, POOL_WIDTH)]
    return _hbm_call(
        body, name=name,
        grid=(s // ts,),
        in_specs=[tile, pl.BlockSpec((ts, CONV_WIDTH), lambda i: (i, 0)), pl.BlockSpec((ts, POOL_WIDTH), lambda i: (i, 0)),
                  pl.BlockSpec((ts, d), lambda i: (i, 0))] + [_const_spec(c.shape) for c in consts],
        out_specs=[pl.BlockSpec((ts, 2 * SGU_WIDTH), lambda i: (i, 0)), pl.BlockSpec((ts, CONV_WIDTH), lambda i: (i, 0)),
                   pl.BlockSpec((ts, POOL_WIDTH), lambda i: (i, 0))] + [_const_spec(a) for a in acc_shapes],
        out_shape=[jax.ShapeDtypeStruct((s, 2 * SGU_WIDTH), BF16), jax.ShapeDtypeStruct((s, CONV_WIDTH), F32),
                   jax.ShapeDtypeStruct((s, POOL_WIDTH), F32)] + [jax.ShapeDtypeStruct(a, F32) for a in acc_shapes],
        scratch_shapes=[pltpu.VMEM((ts, SGU_WIDTH), F32), pltpu.VMEM((ts, SGU_WIDTH), F32), pltpu.VMEM((ts, SGU_WIDTH), F32)],
        compiler_params=pltpu.CompilerParams(dimension_semantics=("arbitrary",),
                                             vmem_limit_bytes=_vmem_limit(24 * ts * inw * 4)),
    )(z, cb, dpool, dy, *consts)


def _mixer_bwd_b(z, dza, dcb, dd, x, dres, gain, s1p, cw, w, name):
    s, inw = z.shape
    d = x.shape[1]
    ts = _row_tile(s, 256)
    c0 = 2 * SGU_WIDTH
    c1 = c0 + 2 * CONV_WIDTH

    def body(z_ref, zh_ref, dza_ref, dcb_ref, dcbn_ref, dd_ref, ddn_ref, x_ref, dres_ref, g_ref, s_ref, cw_ref, w_ref,
             dx_ref, dz_ref, dsh_ref, dsc_ref, dg_ref, dcw_ref, dcbias_ref, ext_b, ext_n, ext_e):
        i = pl.program_id(0)
        first = i == 0
        last = i == pl.num_programs(0) - 1

        @pl.when(first)
        def _():
            for ref in (dsh_ref, dsc_ref, dg_ref, dcw_ref, dcbias_ref):
                ref[...] = jnp.zeros_like(ref)

        a = z_ref[:, c0:c0 + CONV_WIDTH]
        sg = jax.nn.sigmoid(z_ref[:, c0 + CONV_WIDTH:c1])
        ah = zh_ref[:, c0:c0 + CONV_WIDTH]
        gh = zh_ref[:, c0 + CONV_WIDTH:c1]
        ext_b[pl.ds(0, MIX_HALO), :] = jnp.where(first, 0.0, ah * jax.nn.sigmoid(gh))
        ext_b[pl.ds(MIX_HALO, ts), :] = a * sg
        dcbv = dcb_ref[...]
        dcbias_ref[...] += _rsum(dcbv)
        for k in range(CONV_K):
            dcw_ref[k:k + 1, :] += _rsum(dcbv * ext_b[pl.ds(MIX_HALO - (CONV_K - 1) + k, ts), :])

        ext_n[pl.ds(0, ts), :] = dcbv
        ext_n[pl.ds(ts, MIX_HALO), :] = jnp.where(last, 0.0, dcbn_ref[...])
        for r in range(ts // CONV_ROWS):
            acc = jnp.zeros((CONV_ROWS, CONV_WIDTH), F32)
            for k in range(CONV_K):
                acc = acc + cw_ref[k:k + 1, :] * ext_n[pl.ds(CONV_K - 1 - k + r * CONV_ROWS, CONV_ROWS), :]
            rows = pl.ds(r * CONV_ROWS, CONV_ROWS)
            ar = z_ref[rows, c0:c0 + CONV_WIDTH]
            sr = jax.nn.sigmoid(z_ref[rows, c0 + CONV_WIDTH:c1])
            dz_ref[rows, c0:c0 + CONV_WIDTH] = (acc * sr).astype(BF16)
            dz_ref[rows, c0 + CONV_WIDTH:c1] = (acc * ar * sr * (1.0 - sr)).astype(BF16)

        ddv = dd_ref[...]
        ext_e[pl.ds(0, ts), :] = ddv / _pool_counts(i, ts)
        nh = (i + 1) * ts + lax.broadcasted_iota(jnp.int32, (MIX_HALO, POOL_WIDTH), 0)
        lane = lax.broadcasted_iota(jnp.int32, (MIX_HALO, POOL_WIDTH), 1)
        gdim = POOL_WIDTH // len(POOL_WINDOWS)
        winh = jnp.where(lane < gdim, float(POOL_WINDOWS[0]),
                         jnp.where(lane < 2 * gdim, float(POOL_WINDOWS[1]),
                                   jnp.where(lane < 3 * gdim, float(POOL_WINDOWS[2]), float(POOL_WINDOWS[3]))))
        cnth = jnp.minimum((nh + 1).astype(F32), winh)
        ext_e[pl.ds(ts, MIX_HALO), :] = jnp.where(last, 0.0, ddn_ref[...] / cnth)
        dz_ref[:, c1:inw] = (_window_sums(ext_e, 0, ts, 1) - ddv).astype(BF16)
        dz_ref[:, 0:c0] = dza_ref[...]

        dh = _dot_nt(dz_ref[...], w_ref[...])
        dx, dsh, dsc, dg = _norm_mod_bwd(dh, x_ref[...], g_ref[...], s_ref[...], dres_ref[...])
        dx_ref[...] = dx
        dsh_ref[...] += dsh
        dsc_ref[...] += dsc
        dg_ref[...] += dg

    tile, prev, _ = _mixer_specs(s, ts, inw)
    _, _, nxt_b = _mixer_specs(s, ts, CONV_WIDTH)
    _, _, nxt_c = _mixer_specs(s, ts, POOL_WIDTH)
    row = pl.BlockSpec((ts, d), lambda i: (i, 0))
    vec = pl.BlockSpec((1, d), lambda i: (0, 0))
    return _hbm_call(
        body, name=name,
        grid=(s // ts,),
        in_specs=[tile, prev, pl.BlockSpec((ts, c0), lambda i: (i, 0)),
                  pl.BlockSpec((ts, CONV_WIDTH), lambda i: (i, 0)), nxt_b,
                  pl.BlockSpec((ts, POOL_WIDTH), lambda i: (i, 0)), nxt_c,
                  row, row, vec, vec, _const_spec(cw.shape),
                  pl.BlockSpec(w.shape, lambda i: (0, 0), pipeline_mode=pl.Buffered(1))],
        out_specs=[row, pl.BlockSpec((ts, inw), lambda i: (i, 0)), vec, vec, vec,
                   _const_spec((CONV_K, CONV_WIDTH)), _const_spec((1, CONV_WIDTH))],
        out_shape=[jax.ShapeDtypeStruct((s, d), F32), jax.ShapeDtypeStruct((s, inw), BF16)]
        + [jax.ShapeDtypeStruct((1, d), F32)] * 3
        + [jax.ShapeDtypeStruct((CONV_K, CONV_WIDTH), F32), jax.ShapeDtypeStruct((1, CONV_WIDTH), F32)],
        scratch_shapes=[pltpu.VMEM((ts + MIX_HALO, CONV_WIDTH), F32), pltpu.VMEM((ts + MIX_HALO, CONV_WIDTH), F32),
                        pltpu.VMEM((ts + MIX_HALO, POOL_WIDTH), F32)],
        compiler_params=pltpu.CompilerParams(dimension_semantics=("arbitrary",),
                                             vmem_limit_bytes=_vmem_limit(16 * ts * inw * 4 + inw * d * 2)),
    )(z, z, dza, dcb, dcb, dd, dd, x, dres, gain, s1p, cw, w)


def _ffn_specs(s, ts, tc, half_blocks):
    nbh = ts // FFN_HALO

    def tile(off):
        return pl.BlockSpec((ts, tc), lambda j, i: (i, j + off))

    def prev(off):
        return pl.BlockSpec((FFN_HALO, tc), lambda j, i: (jnp.maximum(i * nbh - 1, 0), j + off))

    def vec(rows, off):
        return pl.BlockSpec((rows, tc), lambda j, i: (0, j + off))

    return tile, prev, vec


def _rows_before(cur, prev, k):
    row = lax.broadcasted_iota(jnp.int32, cur.shape, 0)
    return jnp.where(row >= k, pltpu.roll(cur, k, 0), pltpu.roll(prev, k, 0))


def _conv3_rows(cur, prev, w_ref, b_ref, cols):
    x1 = _rows_before(cur, prev, 1)
    x2 = _rows_before(cur, prev, 2)
    u = b_ref[:, cols] + w_ref[2:3, cols] * cur + w_ref[1:2, cols] * x1 + w_ref[0:1, cols] * x2
    return u, x2, x1


def _halo_chunk(h_ref, cols, first):
    h = jnp.where(first, 0.0, h_ref[:, cols])
    return jnp.concatenate([h] * (FFN_ROWS // FFN_HALO), axis=0)


def _ffn_act_fwd(p, cw, cb, name):
    s, f2 = p.shape
    f = f2 // 2
    tc = f // 2
    hb = f // tc
    ts = _row_tile(s, 256)

    def body(pg_ref, pgh_ref, pv_ref, pvh_ref, wg_ref, wv_ref, bg_ref, bv_ref, act_ref):
        first = pl.program_id(1) == 0
        for c in range(tc // LANES):
            cols = slice(c * LANES, (c + 1) * LANES)

            def chunk(r, carry, cols=cols):
                pg_prev, pv_prev = carry
                rows = pl.ds(pl.multiple_of(r * FFN_ROWS, FFN_ROWS), FFN_ROWS)
                pg = pg_ref[rows, cols]
                pv = pv_ref[rows, cols]
                ug, _, _ = _conv3_rows(pg, pg_prev, wg_ref, bg_ref, cols)
                uv, _, _ = _conv3_rows(pv, pv_prev, wv_ref, bv_ref, cols)
                act_ref[rows, cols] = (_gelu(ug) * uv).astype(BF16)
                return pg, pv

            def step(r, carry, chunk=chunk):
                for u in range(FFN_UNROLL):
                    carry = chunk(r * FFN_UNROLL + u, carry)
                return carry

            lax.fori_loop(0, ts // (FFN_ROWS * FFN_UNROLL), step,
                          (_halo_chunk(pgh_ref, cols, first), _halo_chunk(pvh_ref, cols, first)))

    tile, prev, vec = _ffn_specs(s, ts, tc, hb)
    return _hbm_call(
        body, name=name,
        grid=(hb, s // ts),
        in_specs=[tile(0), prev(0), tile(hb), prev(hb), vec(FFN_CONV_K, 0), vec(FFN_CONV_K, hb), vec(1, 0), vec(1, hb)],
        out_specs=pl.BlockSpec((ts, tc), lambda j, i: (i, j)),
        out_shape=jax.ShapeDtypeStruct((s, f), BF16),
        compiler_params=pltpu.CompilerParams(dimension_semantics=("arbitrary", "arbitrary"),
                                             vmem_limit_bytes=_vmem_limit(8 * ts * tc * 4)),
    )(p, p, p, p, cw, cw, cb, cb)


def _ffn_act_bwd(p, dact, cw, cb, name):
    s, f2 = p.shape
    f = f2 // 2
    tc = f // 2
    hb = f // tc
    ts = _row_tile(s, 256)

    def body(pg_ref, pgh_ref, pv_ref, pvh_ref, da_ref, wg_ref, wv_ref, bg_ref, bv_ref,
             dug_ref, duv_ref, dwg_ref, dwv_ref, dbg_ref, dbv_ref):
        first = pl.program_id(1) == 0

        @pl.when(first)
        def _():
            for ref in (dwg_ref, dwv_ref, dbg_ref, dbv_ref):
                ref[...] = jnp.zeros_like(ref)

        zero = jnp.zeros((FFN_ROWS, LANES), F32)
        for c in range(tc // LANES):
            cols = slice(c * LANES, (c + 1) * LANES)

            def chunk(r, carry, cols=cols):
                pg_prev, pv_prev, ag0, ag1, ag2, av0, av1, av2, sg, sv = carry
                rows = pl.ds(pl.multiple_of(r * FFN_ROWS, FFN_ROWS), FFN_ROWS)
                pg = pg_ref[rows, cols]
                pv = pv_ref[rows, cols]
                ug, pg2, pg1 = _conv3_rows(pg, pg_prev, wg_ref, bg_ref, cols)
                uv, pv2, pv1 = _conv3_rows(pv, pv_prev, wv_ref, bv_ref, cols)
                da = da_ref[rows, cols]
                dug = da * uv * _gelu_grad(ug)
                duv = da * _gelu(ug)
                dug_ref[rows, cols] = dug
                duv_ref[rows, cols] = duv
                return (pg, pv, ag0 + dug * pg2, ag1 + dug * pg1, ag2 + dug * pg,
                        av0 + duv * pv2, av1 + duv * pv1, av2 + duv * pv, sg + dug, sv + duv)

            def step(r, carry, chunk=chunk):
                for u in range(FFN_UNROLL):
                    carry = chunk(r * FFN_UNROLL + u, carry)
                return carry

            out = lax.fori_loop(0, ts // (FFN_ROWS * FFN_UNROLL), step,
                                (_halo_chunk(pgh_ref, cols, first), _halo_chunk(pvh_ref, cols, first)) + (zero,) * 8)
            for k in range(FFN_CONV_K):
                dwg_ref[k:k + 1, cols] += _rsum(out[2 + k])
                dwv_ref[k:k + 1, cols] += _rsum(out[5 + k])
            dbg_ref[:, cols] += _rsum(out[8])
            dbv_ref[:, cols] += _rsum(out[9])

    tile, prev, vec = _ffn_specs(s, ts, tc, hb)
    half = pl.BlockSpec((ts, tc), lambda j, i: (i, j))
    wacc = pl.BlockSpec((FFN_CONV_K, tc), lambda j, i: (0, j))
    bacc = pl.BlockSpec((1, tc), lambda j, i: (0, j))
    return _hbm_call(
        body, name=name,
        grid=(hb, s // ts),
        in_specs=[tile(0), prev(0), tile(hb), prev(hb), half, vec(FFN_CONV_K, 0), vec(FFN_CONV_K, hb), vec(1, 0), vec(1, hb)],
        out_specs=[half, half, wacc, wacc, bacc, bacc],
        out_shape=[jax.ShapeDtypeStruct((s, f), F32)] * 2 + [jax.ShapeDtypeStruct((FFN_CONV_K, f), F32)] * 2
        + [jax.ShapeDtypeStruct((1, f), F32)] * 2,
        compiler_params=pltpu.CompilerParams(dimension_semantics=("arbitrary", "arbitrary"),
                                             vmem_limit_bytes=_vmem_limit(12 * ts * tc * 4)),
    )(p, p, p, p, dact, cw, cw, cb, cb)


def _ffn_in_bwd(dug, duv, cw, w, x, dres, gain, s1p, name):
    s, f = dug.shape
    d = x.shape[1]
    ts = _row_tile(s, 256)
    tc = w.shape[2]
    assert f % tc == 0 and w.shape[0] * tc == 2 * f
    nbh = ts // FFN_HALO

    def body(dug_ref, dugn_ref, duv_ref, duvn_ref, cw_ref, w_ref, x_ref, dres_ref, g_ref, s_ref,
             dx_ref, dp_ref, dsh_ref, dsc_ref, dg_ref, ext):
        i = pl.program_id(0)
        last = i == pl.num_programs(0) - 1

        @pl.when(i == 0)
        def _():
            for ref in (dsh_ref, dsc_ref, dg_ref):
                ref[...] = jnp.zeros_like(ref)

        dh = jnp.zeros((ts, d), F32)
        for half, (t_ref, n_ref) in enumerate(((dug_ref, dugn_ref), (duv_ref, duvn_ref))):
            for cb in range(f // tc):
                cols = slice(cb * tc, (cb + 1) * tc)
                wcols = slice(half * f + cb * tc, half * f + (cb + 1) * tc)
                ext[pl.ds(0, ts), :] = t_ref[:, cols]
                ext[pl.ds(ts, FFN_HALO), :] = jnp.where(last, 0.0, n_ref[:, cols])
                acc = cw_ref[FFN_CONV_K - 1:FFN_CONV_K, wcols] * t_ref[:, cols]
                for k in range(FFN_CONV_K - 1):
                    acc = acc + cw_ref[k:k + 1, wcols] * ext[pl.ds(FFN_CONV_K - 1 - k, ts), :]
                dpb = acc.astype(BF16)
                dp_ref[:, wcols] = dpb
                dh = dh + _dot_nt(dpb, w_ref[half * (f // tc) + cb])
        dx, dsh, dsc, dg = _norm_mod_bwd(dh, x_ref[...], g_ref[...], s_ref[...], dres_ref[...])
        dx_ref[...] = dx
        dsh_ref[...] += dsh
        dsc_ref[...] += dsc
        dg_ref[...] += dg

    tile = pl.BlockSpec((ts, f), lambda i: (i, 0))
    nxt = pl.BlockSpec((FFN_HALO, f), lambda i: (jnp.minimum((i + 1) * nbh, s // FFN_HALO - 1), 0))
    row = pl.BlockSpec((ts, d), lambda i: (i, 0))
    vec = pl.BlockSpec((1, d), lambda i: (0, 0))
    return _hbm_call(
        body, name=name,
        grid=(s // ts,),
        in_specs=[tile, nxt, tile, nxt, _const_spec(cw.shape),
                  pl.BlockSpec(w.shape, lambda i: (0, 0, 0), pipeline_mode=pl.Buffered(1)), row, row, vec, vec],
        out_specs=[row, pl.BlockSpec((ts, 2 * f), lambda i: (i, 0)), vec, vec, vec],
        out_shape=[jax.ShapeDtypeStruct((s, d), F32), jax.ShapeDtypeStruct((s, 2 * f), BF16)] + [jax.ShapeDtypeStruct((1, d), F32)] * 3,
        scratch_shapes=[pltpu.VMEM((ts + FFN_HALO, tc), F32)],
        compiler_params=pltpu.CompilerParams(
            dimension_semantics=("arbitrary",),
            vmem_limit_bytes=_vmem_limit(4 * ts * f * 4 + 2 * f * d * 2 + 2 * ts * 2 * f * 2 + 12 * ts * d * 4 + 6 * ts * tc * 4)),
    )(dug, dug, duv, duv, cw, w, x, dres, gain, s1p)


def _adamw_math(w, g, m, v):
    m = ADAM_B1 * m + (1.0 - ADAM_B1) * g
    v = ADAM_B2 * v + (1.0 - ADAM_B2) * (g * g)
    m_hat = m / (1.0 - ADAM_B1 ** ADAM_STEP)
    v_hat = v / (1.0 - ADAM_B2 ** ADAM_STEP)
    delta = -ADAM_LR * (m_hat / (jnp.sqrt(v_hat) + ADAM_EPS) + ADAM_WD * w)
    return delta, m, v


def _adam_rows(rows, cols):
    want = max(8, (2 * 1024 * 1024 // (cols * 4)) // 8 * 8)
    tr = min(rows, want)
    while rows % tr:
        tr -= 8
    return tr


def _adamw(w, m, v, g_parts, name):
    shape = w.shape
    nl = shape[0] if w.ndim == 3 else 1
    r, c = shape[-2], shape[-1]
    tr = _adam_rows(r, c)
    ng = len(g_parts)

    def body(*refs):
        w_ref, m_ref, v_ref = refs[0:3]
        g_refs = refs[3:3 + ng]
        g_out, d_out, m_out, v_out = refs[3 + ng:]
        g = g_refs[0][...]
        for gr in g_refs[1:]:
            g = g + gr[...]
        delta, mn, vn = _adamw_math(w_ref[...], g, m_ref[...], v_ref[...])
        g_out[...] = g
        d_out[...] = delta
        m_out[...] = mn
        v_out[...] = vn

    blk = pl.BlockSpec((None, tr, c), lambda l, i: (l, i, 0))
    outs = _hbm_call(
        body, name=name,
        grid=(nl, r // tr),
        in_specs=[blk] * (3 + ng),
        out_specs=[blk] * 4,
        out_shape=[jax.ShapeDtypeStruct((nl, r, c), F32)] * 4,
        compiler_params=pltpu.CompilerParams(dimension_semantics=("arbitrary", "arbitrary"),
                                             vmem_limit_bytes=_vmem_limit(2 * (7 + ng) * tr * max(c, 128) * 4 + (8 << 20))),
    )(*[a.reshape(nl, r, c) for a in (w, m, v, *g_parts)])
    return [o.reshape(shape) for o in outs]


def _adamw_many(ws, ms, vs, gs, name):
    n = len(ws)

    def body(*refs):
        for k in range(n):
            w_ref, m_ref, v_ref, g_ref = refs[k], refs[n + k], refs[2 * n + k], refs[3 * n + k]
            delta, mn, vn = _adamw_math(w_ref[...], g_ref[...], m_ref[...], v_ref[...])
            refs[4 * n + 3 * k][...] = delta
            refs[4 * n + 3 * k + 1][...] = mn
            refs[4 * n + 3 * k + 2][...] = vn

    specs = [_const_spec(a.shape) for a in ws]
    outs = _hbm_call(
        body, name=name,
        grid=(1,),
        in_specs=specs * 4,
        out_specs=[sp for sp in specs for _ in range(3)],
        out_shape=[jax.ShapeDtypeStruct(a.shape, F32) for a in ws for _ in range(3)],
        compiler_params=pltpu.CompilerParams(dimension_semantics=("arbitrary",),
                                             vmem_limit_bytes=_vmem_limit(20 * sum(a.size for a in ws) * 4 + (8 << 20))),
    )(*ws, *ms, *vs, *gs)
    return [tuple(outs[3 * k:3 * k + 3]) for k in range(n)]


def _modw_adamw(sct, dmod, w, m, v, name):
    nl, d, n = w.shape
    tr = _row_tile(d, 128)

    def body(sct_ref, dm_ref, w_ref, m_ref, v_ref, g_out, d_out, m_out, v_out):
        sc = sct_ref[...].astype(BF16).astype(F32)
        dm = dm_ref[...].astype(BF16).astype(F32)
        g = sc[:, 0:1] * dm[0:1, :]
        for b in range(1, N_DEV):
            g = g + sc[:, b:b + 1] * dm[b:b + 1, :]
        delta, mn, vn = _adamw_math(w_ref[...], g, m_ref[...], v_ref[...])
        g_out[...] = g
        d_out[...] = delta
        m_out[...] = mn
        v_out[...] = vn

    blk = pl.BlockSpec((None, tr, n), lambda l, i: (l, i, 0))
    return _hbm_call(
        body, name=name,
        grid=(nl, d // tr),
        in_specs=[pl.BlockSpec((tr, N_DEV), lambda l, i: (i, 0)), pl.BlockSpec((None, N_DEV, n), lambda l, i: (l, 0, 0)),
                  blk, blk, blk],
        out_specs=[blk] * 4,
        out_shape=[jax.ShapeDtypeStruct((nl, d, n), F32)] * 4,
        compiler_params=pltpu.CompilerParams(dimension_semantics=("arbitrary", "arbitrary"),
                                             vmem_limit_bytes=_vmem_limit(2 * 8 * tr * n * 4 + (8 << 20))),
    )(sct, dmod, w, m, v)


def _reduce4(recvs, name):
    nl = len(recvs)
    shape = recvs[0].shape[1:]
    c = shape[-1]
    r = math.prod(shape[:-1])
    tr = _adam_rows(r, c)
    nt = r // tr

    def body(*refs):
        o_ref = refs[nl]
        for l in range(nl):
            @pl.when(pl.program_id(0) == l)
            def _():
                acc = refs[l][0].astype(F32)
                for k in range(1, N_CHIPS):
                    acc = acc + refs[l][k].astype(F32)
                o_ref[...] = acc

    def in_map(l):
        return lambda ll, i: (0, jnp.where(ll < l, 0, jnp.where(ll > l, nt - 1, i)), 0)

    return _hbm_call(
        body, name=name,
        grid=(nl, nt),
        in_specs=[pl.BlockSpec((N_CHIPS, tr, c), in_map(l)) for l in range(nl)],
        out_specs=pl.BlockSpec((None, tr, c), lambda ll, i: (ll, i, 0)),
        out_shape=jax.ShapeDtypeStruct((nl, r, c), F32),
        compiler_params=pltpu.CompilerParams(dimension_semantics=("arbitrary", "arbitrary"),
                                             vmem_limit_bytes=_vmem_limit(2 * 8 * nl * tr * max(c, 128) * 4 + (8 << 20))),
    )(*[rv.reshape(N_CHIPS, r, c) for rv in recvs]).reshape((nl,) + shape)


def _my_place():
    return lax.axis_index("x"), lax.axis_index("y"), lax.axis_index("c")


def _chip_coords(j):
    return j // 2, j % 2


def _mod_forward(c, mod_w, mod_b4):
    nl, d, n = mod_w.shape
    kc = 256

    def body(c_ref, w_ref, b_ref, mod_ref, sc_ref, cbuf, stage, s1, r1, s2, r2):
        mx, my, mc = _my_place()
        me = 4 * mx + 2 * my + mc
        q = 2 * mx + my
        cv = c_ref[...]
        cbuf[me] = jnp.broadcast_to(cv * jax.nn.sigmoid(cv), (8, d))
        sends = []
        for t in range(N_DEV):
            tx, ty = _chip_coords(t // 2)
            cp = pltpu.make_async_remote_copy(src_ref=cbuf.at[me], dst_ref=cbuf.at[me], send_sem=s1.at[t], recv_sem=r1.at[me],
                                              device_id=(tx, ty, t % 2), device_id_type=MESH)

            @pl.when(t != me)
            def _():
                cp.start()

            sends.append((t, cp))
        for t in range(N_DEV):
            @pl.when(t != me)
            def _():
                pltpu.make_async_remote_copy(src_ref=cbuf.at[t], dst_ref=cbuf.at[t], send_sem=s1.at[t], recv_sem=r1.at[t],
                                             device_id=(mx, my, mc), device_id_type=MESH).wait_recv()
        for t, cp in sends:
            @pl.when(t != me)
            def _():
                cp.wait_send()

        row = lax.broadcasted_iota(jnp.int32, (8, d), 0)
        sc_all = jnp.zeros((8, d), F32)
        for t in range(N_DEV):
            sc_all = sc_all + jnp.where(row == t, cbuf[t], 0.0)
        sc_ref[...] = sc_all
        rown = lax.broadcasted_iota(jnp.int32, (8, n), 0)
        for l in range(nl):
            acc = jnp.zeros((8, n), F32)
            for k0 in range(0, d, kc):
                acc = acc + _dot(sc_all[:, k0:k0 + kc].astype(BF16), w_ref[l, k0:k0 + kc, :].astype(BF16))
            acc = acc + b_ref[l, q]
            for j in range(N_CHIPS):
                jx, jy = _chip_coords(j)
                bdest = 4 * jx + 2 * jy + mc
                rowv = jnp.sum(jnp.where(rown == bdest, acc, 0.0), axis=0, keepdims=True)
                stage[j, l] = jnp.broadcast_to(rowv, (8, n))
        sends2 = []
        for j in range(N_CHIPS):
            jx, jy = _chip_coords(j)
            cp = pltpu.make_async_remote_copy(src_ref=stage.at[j], dst_ref=mod_ref.at[:, q], send_sem=s2.at[j], recv_sem=r2.at[q],
                                              device_id=(jx, jy, mc), device_id_type=MESH)

            @pl.when(j != q)
            def _():
                cp.start()

            @pl.when(j == q)
            def _():
                for l in range(nl):
                    mod_ref[l, j] = stage[j, l]

            sends2.append((j, cp))
        for j in range(N_CHIPS):
            @pl.when(j != q)
            def _():
                pltpu.make_async_remote_copy(src_ref=stage.at[j], dst_ref=mod_ref.at[:, j], send_sem=s2.at[j], recv_sem=r2.at[j],
                                             device_id=(mx, my, mc), device_id_type=MESH).wait_recv()
        for j, cp in sends2:
            @pl.when(j != q)
            def _():
                cp.wait_send()

    vm = pl.BlockSpec(memory_space=pltpu.VMEM)
    return pl.pallas_call(
        body, name="mod_forward",
        in_specs=[vm, vm, vm],
        out_specs=[vm, vm],
        out_shape=[jax.ShapeDtypeStruct((nl, N_CHIPS, 8, n), F32), jax.ShapeDtypeStruct((8, d), F32)],
        scratch_shapes=[pltpu.VMEM((N_DEV, 8, d), F32), pltpu.VMEM((N_CHIPS, nl, 8, n), F32),
                        pltpu.SemaphoreType.DMA((N_DEV,)), pltpu.SemaphoreType.DMA((N_DEV,)),
                        pltpu.SemaphoreType.DMA((N_CHIPS,)), pltpu.SemaphoreType.DMA((N_CHIPS,))],
        compiler_params=pltpu.CompilerParams(vmem_limit_bytes=_vmem_limit(2 * nl * d * n * 4 + (8 << 20))),
    )(c, mod_w, mod_b4)


_HBM_SPEC = pl.BlockSpec(memory_space=pltpu.HBM)
_SEM_SPEC = pl.BlockSpec(memory_space=pltpu.SEMAPHORE)
_DATAFLOW = pltpu.SideEffectType.DATAFLOW_SIDE_EFFECTING


def _slot(ref, scatter, j):
    return ref.at[j] if scatter else ref


def _exchange_start(groups, scatter, after, name):
    flat = [a for g in groups for a in g]
    na = len(flat)
    ng = len(groups)
    sizes = [len(g) for g in groups]
    first = [sum(sizes[:g]) for g in range(ng)]
    where = [(g, k) for g in range(ng) for k in range(sizes[g])]
    mx, my, _ = _my_place()
    qo = 2 * mx + my
    lands = []
    for a in flat:
        own = lax.dynamic_index_in_dim(a, qo, 0, keepdims=False) if scatter else a
        lands.append(lax.dynamic_update_index_in_dim(lax.empty((N_CHIPS,) + own.shape, a.dtype), own, qo, 0))

    def body(*refs):
        ins, lnd = refs[:na], refs[na:2 * na]
        ssems, rsems = refs[2 * na + 1:2 * na + 1 + ng], refs[2 * na + 1 + ng:2 * na + 1 + 2 * ng]
        token = refs[-1]
        mx, my, mc = _my_place()
        q = 2 * mx + my
        for j in range(N_CHIPS):
            jx, jy = _chip_coords(j)
            for a in range(na):
                g, k = where[a]

                @pl.when(j != q)
                def _():
                    pltpu.make_async_remote_copy(src_ref=_slot(ins[a], scatter, j), dst_ref=lnd[a].at[q],
                                                 send_sem=ssems[g].at[k * N_CHIPS + j], recv_sem=rsems[g].at[k * N_CHIPS + q],
                                                 device_id=(jx, jy, mc), device_id_type=MESH).start()
        token[...] = jnp.zeros_like(token)

    sem_shapes = [pltpu.SemaphoreType.DMA((n * N_CHIPS,)) for n in sizes]
    outs = pl.pallas_call(
        body, name=name,
        in_specs=[_HBM_SPEC] * (2 * na) + [pl.BlockSpec(memory_space=pl.ANY)],
        out_specs=[_SEM_SPEC] * (2 * ng) + [_HBM_SPEC] * (2 * na) + [pl.BlockSpec(memory_space=pltpu.VMEM)],
        out_shape=sem_shapes + sem_shapes + [pltpu.HBM(a.shape, a.dtype) for a in flat + lands]
        + [jax.ShapeDtypeStruct((8, 128), F32)],
        input_output_aliases={i: 2 * ng + i for i in range(2 * na)},
        compiler_params=pltpu.CompilerParams(has_side_effects=_DATAFLOW),
    )(*[pltpu.with_memory_space_constraint(a, pltpu.HBM) for a in flat + lands], after)
    ssems, rsems = outs[:ng], outs[ng:2 * ng]
    src_thru, land_thru = outs[2 * ng:2 * ng + na], outs[2 * ng + na:2 * ng + 2 * na]
    states = [(src_thru[first[g]:first[g] + sizes[g]], land_thru[first[g]:first[g] + sizes[g]], ssems[g], rsems[g])
              for g in range(ng)]
    return states, outs[-1]


def _exchange_wait(state, scatter, after, name):
    srcs, lands, ssem, rsem = state
    na = len(srcs)

    def body(*refs):
        ins, lnd = refs[:na], refs[na:2 * na]
        ssem_ref, rsem_ref = refs[2 * na], refs[2 * na + 1]
        mx, my, mc = _my_place()
        q = 2 * mx + my
        for j in range(N_CHIPS):
            for a in range(na):
                @pl.when(j != q)
                def _():
                    cp = pltpu.make_async_remote_copy(src_ref=_slot(ins[a], scatter, j), dst_ref=lnd[a].at[j],
                                                      send_sem=ssem_ref.at[a * N_CHIPS + j], recv_sem=rsem_ref.at[a * N_CHIPS + j],
                                                      device_id=(mx, my, mc), device_id_type=MESH)
                    cp.wait_send()
                    cp.wait_recv()

    outs = pl.pallas_call(
        body, name=name,
        in_specs=[_HBM_SPEC] * (2 * na) + [_SEM_SPEC, _SEM_SPEC, pl.BlockSpec(memory_space=pl.ANY)],
        out_specs=[_HBM_SPEC] * (2 * na),
        out_shape=[pltpu.HBM(a.shape, a.dtype) for a in list(srcs) + list(lands)],
        input_output_aliases={i: i for i in range(2 * na)},
        compiler_params=pltpu.CompilerParams(has_side_effects=_DATAFLOW),
    )(*srcs, *lands, ssem, rsem, after)
    return outs[na:]


def _sibling_copy(src, dst, ssem, rsem, a):
    mx, my, mc = _my_place()
    return pltpu.make_async_remote_copy(src_ref=src, dst_ref=dst, send_sem=ssem.at[a], recv_sem=rsem.at[a],
                                        device_id=(mx, my, 1 - mc), device_id_type=MESH)


def _swap_start(arrs, after, name):
    na = len(arrs)
    lands = [lax.empty(a.shape, a.dtype) for a in arrs]

    def body(*refs):
        ins, lnd = refs[:na], refs[na:2 * na]
        ssem, rsem, token = refs[2 * na + 1], refs[2 * na + 2], refs[-1]
        for a in range(na):
            _sibling_copy(ins[a], lnd[a], ssem, rsem, a).start()
        token[...] = jnp.zeros_like(token)

    outs = pl.pallas_call(
        body, name=name,
        in_specs=[_HBM_SPEC] * (2 * na) + [pl.BlockSpec(memory_space=pl.ANY)],
        out_specs=[_SEM_SPEC] * 2 + [_HBM_SPEC] * (2 * na) + [pl.BlockSpec(memory_space=pltpu.VMEM)],
        out_shape=[pltpu.SemaphoreType.DMA((na,))] * 2 + [pltpu.HBM(a.shape, a.dtype) for a in list(arrs) + lands]
        + [jax.ShapeDtypeStruct((8, 128), F32)],
        input_output_aliases={i: 2 + i for i in range(2 * na)},
        compiler_params=pltpu.CompilerParams(has_side_effects=_DATAFLOW),
    )(*[pltpu.with_memory_space_constraint(a, pltpu.HBM) for a in list(arrs) + lands], after)
    return (outs[2:2 + na], outs[2 + na:2 + 2 * na], outs[0], outs[1]), outs[-1]


def _swap_wait(state, after, name):
    srcs, lands, ssem, rsem = state
    na = len(srcs)

    def body(*refs):
        ins, lnd = refs[:na], refs[na:2 * na]
        ssem_ref, rsem_ref = refs[2 * na], refs[2 * na + 1]
        for a in range(na):
            cp = _sibling_copy(ins[a], lnd[a], ssem_ref, rsem_ref, a)
            cp.wait_send()
            cp.wait_recv()

    outs = pl.pallas_call(
        body, name=name,
        in_specs=[_HBM_SPEC] * (2 * na) + [_SEM_SPEC, _SEM_SPEC, pl.BlockSpec(memory_space=pl.ANY)],
        out_specs=[_HBM_SPEC] * (2 * na),
        out_shape=[pltpu.HBM(a.shape, a.dtype) for a in list(srcs) + list(lands)],
        input_output_aliases={i: i for i in range(2 * na)},
        compiler_params=pltpu.CompilerParams(has_side_effects=_DATAFLOW),
    )(*srcs, *lands, ssem, rsem, after)
    return outs[:na], outs[na:]


def _allreduce_small(rows_all, rows_sum):
    ra, c = rows_all.shape
    r = rows_sum.shape[0]
    ch = r // N_DEV
    assert ch % 8 == 0 and ch * N_DEV == r

    def body(a_ref, s_ref, all_ref, sum_ref, rbuf, red, sa, rva, sb, rvb, sc, rvc):
        mx, my, mc = _my_place()
        me = 4 * mx + 2 * my + mc
        mine = pl.ds(pl.multiple_of(me * ch, 8), ch)
        all_ref[me] = a_ref[...]
        rbuf[me] = s_ref[mine, :]

        def dev(t):
            tx, ty = _chip_coords(t // 2)
            return (tx, ty, t % 2)

        def everyone_else(fn):
            for t in range(N_DEV):
                @pl.when(t != me)
                def _():
                    fn(t)

        def copy_a(t, slot):
            return pltpu.make_async_remote_copy(src_ref=a_ref, dst_ref=all_ref.at[slot], send_sem=sa.at[t], recv_sem=rva.at[slot],
                                                device_id=dev(t), device_id_type=MESH)

        def copy_b(t, slot):
            return pltpu.make_async_remote_copy(src_ref=s_ref.at[pl.ds(t * ch, ch), :], dst_ref=rbuf.at[slot], send_sem=sb.at[t],
                                                recv_sem=rvb.at[slot], device_id=dev(t), device_id_type=MESH)

        def copy_c(t, chunk_start, slot):
            return pltpu.make_async_remote_copy(src_ref=red, dst_ref=sum_ref.at[pl.ds(chunk_start, ch), :], send_sem=sc.at[t],
                                                recv_sem=rvc.at[slot], device_id=dev(t), device_id_type=MESH)

        everyone_else(lambda t: (copy_a(t, me).start(), copy_b(t, me).start()))
        everyone_else(lambda t: (copy_a(t, t).wait_recv(), copy_b(t, t).wait_recv()))
        everyone_else(lambda t: (copy_a(t, me).wait_send(), copy_b(t, me).wait_send()))
        acc = rbuf[0]
        for t in range(1, N_DEV):
            acc = acc + rbuf[t]
        red[...] = acc
        sum_ref[mine, :] = acc
        everyone_else(lambda t: copy_c(t, pl.multiple_of(me * ch, 8), me).start())
        everyone_else(lambda t: copy_c(t, t * ch, t).wait_recv())
        everyone_else(lambda t: copy_c(t, pl.multiple_of(me * ch, 8), me).wait_send())

    vm = pl.BlockSpec(memory_space=pltpu.VMEM)
    return pl.pallas_call(
        body, name="allreduce_small",
        in_specs=[vm, vm],
        out_specs=[vm, vm],
        out_shape=[jax.ShapeDtypeStruct((N_DEV, ra, c), F32), jax.ShapeDtypeStruct((r, c), F32)],
        scratch_shapes=[pltpu.VMEM((N_DEV, ch, c), F32), pltpu.VMEM((ch, c), F32)] + [pltpu.SemaphoreType.DMA((N_DEV,))] * 6,
        compiler_params=pltpu.CompilerParams(vmem_limit_bytes=_vmem_limit((3 * r + 2 * N_DEV * ra) * c * 4 + (4 << 20))),
    )(rows_all, rows_sum)


def _pack(arrs, row_multiple=8):
    rows, layout, at = [], [], 0
    for a in arrs:
        n = a.size
        nr = -(-n // (8 * SMALL_COLS)) * 8
        flat = a.reshape(-1)
        if nr * SMALL_COLS != n:
            flat = jnp.pad(flat, (0, nr * SMALL_COLS - n))
        rows.append(flat.reshape(nr, SMALL_COLS))
        layout.append((at, nr, a.shape))
        at += nr
    pad = -at % row_multiple
    if pad:
        rows.append(jnp.zeros((pad, SMALL_COLS), F32))
    return jnp.concatenate(rows, axis=0), layout


def _unpack(buf, layout):
    out = []
    for at, nr, shape in layout:
        n = math.prod(shape)
        out.append(buf[at:at + nr].reshape(-1)[:n].reshape(shape))
    return out


SMALL_NAMES = ("mod_b", "mix_pre_g", "mix_post_g", "sgu_norm_g", "sgu_norm_b", "sgu_w", "sgu_b", "conv_b", "conv_norm_g",
               "conv_norm_b", "pool_w", "pool_scale", "branch_g", "ffn_pre_g", "ffn_post_g", "ffn_conv_b")
SHARDED_SMALL = ("conv_w", "ffn_conv_w")
WEIGHT_ORDER = ("mod_w", "mod_b", "mix_pre_g", "mix_post_g", "w_in", "sgu_norm_g", "sgu_norm_b", "sgu_w", "sgu_b", "conv_w",
                "conv_b", "conv_norm_g", "conv_norm_b", "pool_w", "pool_scale", "branch_g", "w_out", "ffn_pre_g", "ffn_post_g",
                "ffn_up", "ffn_conv_w", "ffn_conv_b", "ffn_down")


def _block_diag(blocks):
    n, a, b = blocks.shape
    eye = jnp.eye(n, dtype=blocks.dtype)
    return (eye[:, None, :, None] * blocks[:, :, None, :]).reshape(n * a, n * b)


def _diag_blocks(mat, n):
    a = mat.shape[0] // n
    return jnp.stack([mat[g * a:(g + 1) * a, g * a:(g + 1) * a] for g in range(n)])


def _step(x, c, loss_target, w, m, v):
    nl = w["mod_w"].shape[0]
    s, d = x.shape[1], x.shape[2]
    heads = SGU_WIDTH // HEAD_DIM
    groups = len(POOL_WINDOWS)
    mx, my, _ = _my_place()
    q = 2 * mx + my
    x0 = x.reshape(s, d)
    tgt = loss_target.reshape(s, d)

    nmod = w["mod_w"].shape[2]
    kin = w["w_in"].shape[2]
    inw = kin * N_CHIPS
    f2 = w["ffn_up"].shape[2] * N_CHIPS
    f = f2 // 2

    def wgroups(l):
        return [[w["w_in"][l].astype(BF16), w["conv_w"][l], w["ffn_conv_w"][l]], [w["w_out"][l].astype(BF16)],
                [w["ffn_up"][l].astype(BF16)], [w["ffn_down"][l].astype(BF16)]]

    gstates = {}
    (gstates[0, 0], gstates[0, 1]), gtoken = _exchange_start(wgroups(0)[:2], False, c, "gather_start_in_0")
    mod4, sc_all = _mod_forward(c + gtoken[0:1, 0:1], w["mod_w"], w["mod_b"].reshape(nl, N_CHIPS, 1, nmod))
    mod = mod4[:, :, 0, :].reshape(nl, N_MOD, 1, d)

    tril = jnp.tril(jnp.ones((CHUNK, CHUNK), bool))
    bd = _block_diag(jnp.ones((heads, HEAD_DIM, HEAD_DIM), BF16))

    def mixer_params(l, conv_w):
        wm = jnp.where(tril[None], w["sgu_w"][l], 0.0)
        pw = _block_diag(w["pool_w"][l])
        return dict(
            bd=bd, ng=w["sgu_norm_g"][l][None], nb=w["sgu_norm_b"][l][None],
            wm=wm.astype(BF16), wmt=jnp.swapaxes(wm, 1, 2).astype(BF16),
            bias=jnp.repeat(w["sgu_b"][l].T, HEAD_DIM, axis=1),
            cw=conv_w, cb=w["conv_b"][l][None], cng=w["conv_norm_g"][l][None], cnb=w["conv_norm_b"][l][None],
            pw=pw.astype(BF16), pwt=pw.T.astype(BF16), ps=w["pool_scale"][l][None], bg=w["branch_g"][l][None])

    saved = []
    xl = x0
    arrived = {0: list(_exchange_wait(gstates[0, 0], False, mod4, "gather_wait_in_0"))}
    arrived[0] += list(_exchange_wait(gstates[0, 1], False, arrived[0][0], "gather_wait_out_0"))
    for l in range(nl):
        sh1, sc1, g1, sh2, sc2, g2 = [mod[l, k] for k in range(N_MOD)]
        gpre1, gpost1 = w["mix_pre_g"][l][None], w["mix_post_g"][l][None]
        gpre2, gpost2 = w["ffn_pre_g"][l][None], w["ffn_post_g"][l][None]
        fcb = w["ffn_conv_b"][l][None]
        sh1_after, bg_after, g1_after, sh2_after, fcb_after = sh1, w["branch_g"][l][None], g1, sh2, fcb
        g_win, g_cw, g_fcw = arrived[l][:3]
        if l == 0:
            (gstates[0, 2],), tok = _exchange_start(wgroups(0)[2:3], False, arrived[0][3], "gather_start_up_0")
            sh1_after = sh1 + tok[0:1, 0:1]
        w_in = jnp.transpose(g_win, (1, 0, 2)).reshape(d, inw)
        conv_w = jnp.transpose(g_cw, (1, 0, 2)).reshape(CONV_K, CONV_WIDTH)
        ffn_cw = jnp.transpose(g_fcw, (1, 0, 2)).reshape(FFN_CONV_K, f2)
        mp = mixer_params(l, conv_w)
        z, h1 = _norm_mod_matmul(xl, gpre1, 1.0 + sc1, sh1_after, w_in[None], f"mix_in_{l}")
        w_out = arrived[l][3].reshape(d, d)
        ycat, cbo, dpool = _mixer_fwd(z, dict(mp, bg=bg_after), f"mixer_fwd_{l}")
        (up,) = _exchange_wait(gstates[l, 2], False, ycat, f"gather_wait_up_{l}")
        if l == 0:
            (gstates[0, 3],), tok = _exchange_start(wgroups(0)[3:4], False, up, "gather_start_down_0")
            g1_after = g1 + tok[0:1, 0:1]
        o, x1 = _matmul_norm_resid(ycat, w_out, xl, g1_after, gpost1, f"mix_out_{l}")
        if l + 1 < nl:
            (gstates[l + 1, 0], gstates[l + 1, 1]), tok = _exchange_start(wgroups(l + 1)[0:2], False, x1,
                                                                        f"gather_start_in_{l + 1}")
            sh2_after = sh2 + tok[0:1, 0:1]
        p, h2 = _norm_mod_matmul(x1, gpre2, 1.0 + sc2, sh2_after, up, f"ffn_in_{l}")
        if l + 1 < nl:
            nxt = _exchange_wait(gstates[l + 1, 0], False, p, f"gather_wait_in_{l + 1}")
            nxt_out = _exchange_wait(gstates[l + 1, 1], False, nxt[0], f"gather_wait_out_{l + 1}")
            arrived[l + 1] = list(nxt) + list(nxt_out)
            (gstates[l + 1, 2], gstates[l + 1, 3]), tok = _exchange_start(wgroups(l + 1)[2:4], False, nxt_out[0],
                                                                        f"gather_start_up_{l + 1}")
            fcb_after = fcb + tok[0:1, 0:1]
        act = _ffn_act_fwd(p, ffn_cw, fcb_after, f"ffn_act_{l}")
        (g_down,) = _exchange_wait(gstates[l, 3], False, act, f"gather_wait_down_{l}")
        down = g_down.reshape(f, d)
        qo, x2 = _matmul_norm_resid(act, down, x1, g2, gpost2, f"ffn_out_{l}")
        saved.append(dict(x=xl, z=z, h1=h1, ycat=ycat, cbo=cbo, dpool=dpool, o=o, x1=x1, p=p, h2=h2, act=act, qo=qo, mp=mp, fcb=fcb,
                          w_in=w_in, w_out=w_out, up=up, down=down, ffn_cw=ffn_cw,
                          mods=(sh1, sc1, g1, sh2, sc2, g2), gains=(gpre1, gpost1, gpre2, gpost2)))
        xl = x2

    dx, loss_row = _loss_head(xl, tgt)

    small = {n: [None] * nl for n in SMALL_NAMES + SHARDED_SMALL}
    dmods = [None] * nl
    tn = f2 // N_CHIPS
    sstates = {}
    token = None
    for l in reversed(range(nl)):
        sv = saved[l]
        sh1, sc1, g1, sh2, sc2, g2 = sv["mods"]
        gpre1, gpost1, gpre2, gpost2 = sv["gains"]
        if token is not None:
            g2 = g2 + token[0:1, 0:1]
        dq, dact, dg2, dgpost2 = _resid_bwd_matmul(dx, sv["qo"], g2, gpost2, sv["down"], f"ffn_out_bwd_{l}")
        g_down = _wgrad(sv["act"], dq, (f, lambda j: 0), (d, lambda j: 0), jax.ShapeDtypeStruct((f, d), BF16),
                        (1, lambda j: (0, 0)), (f, d), f"wgrad_ffn_down_{l}")
        dug, duv, dfwg, dfwv, dfbg, dfbv = _ffn_act_bwd(sv["p"], dact, sv["ffn_cw"], sv["fcb"], f"ffn_act_bwd_{l}")
        dx1, dp, dsh2, dsc2, dgpre2 = _ffn_in_bwd(dug, duv, sv["ffn_cw"], sv["up"], sv["x1"], dx, gpre2, 1.0 + sc2,
                                                  f"ffn_in_bwd_{l}")
        g_up = _wgrad(sv["h2"], dp, (d, lambda j: 0), (tn, lambda j: j), jax.ShapeDtypeStruct((N_CHIPS, d, tn), BF16),
                      (N_CHIPS, lambda j: (j, 0, 0)), (None, d, tn), f"wgrad_ffn_up_{l}")
        (sstates[l, 0],), token = _exchange_start([[g_down.reshape(N_CHIPS, f // N_CHIPS, d), g_up]], True, g_up,
                                                  f"scatter_start_ffn_{l}")
        do, dycat, dg1, dgpost1 = _resid_bwd_matmul(dx1, sv["o"], g1 + token[0:1, 0:1], gpost1, sv["w_out"],
                                                    f"mix_out_bwd_{l}")
        g_out = _wgrad(sv["ycat"], do, (d, lambda j: 0), (d, lambda j: 0), jax.ShapeDtypeStruct((d, d), BF16),
                       (1, lambda j: (0, 0)), (d, d), f"wgrad_w_out_{l}")
        (sstates[l, 1],), token = _exchange_start([[g_out.reshape(N_CHIPS, d // N_CHIPS, d)]], True, g_out,
                                                  f"scatter_start_out_{l}")
        mp_after = dict(sv["mp"], bg=sv["mp"]["bg"] + token[0:1, 0:1])
        (dza, dcb, dd, dbg, dwm, dbias, dng, dnb, dcng, dcnb, dps, dpw) = _mixer_bwd_a(sv["z"], sv["cbo"], sv["dpool"], dycat, mp_after, f"mixer_bwd_a_{l}")
        dx, dz, dsh1, dsc1, dgpre1, dcw, dcbias = _mixer_bwd_b(
            sv["z"], dza, dcb, dd, sv["x"], dx1, gpre1, 1.0 + sc1, sv["mp"]["cw"], sv["w_in"], f"mixer_bwd_b_{l}")
        g_in = _wgrad(sv["h1"], dz, (d, lambda j: 0), (inw, lambda j: 0), jax.ShapeDtypeStruct((d, inw), BF16),
                      (1, lambda j: (0, 0)), (d, inw), f"wgrad_w_in_{l}")
        g_in_parts = jnp.transpose(g_in.reshape(d, N_CHIPS, kin), (1, 0, 2))
        if l > 0:
            (sstates[l, 2],), token = _exchange_start([[g_in_parts]], True, g_in_parts, f"scatter_start_in_{l}")

        dmods[l] = jnp.concatenate([dsh1, dsc1, dg1, dsh2, dsc2, dg2], axis=0)
        small["mix_pre_g"][l], small["mix_post_g"][l] = dgpre1[0], dgpost1[0]
        small["ffn_pre_g"][l], small["ffn_post_g"][l] = dgpre2[0], dgpost2[0]
        small["sgu_norm_g"][l], small["sgu_norm_b"][l] = dng[0], dnb[0]
        small["sgu_w"][l] = jnp.where(tril[None], dwm, 0.0)
        small["sgu_b"][l] = dbias.reshape(CHUNK, heads, HEAD_DIM).sum(-1).T
        small["conv_b"][l], small["conv_norm_g"][l], small["conv_norm_b"][l] = dcbias[0], dcng[0], dcnb[0]
        small["pool_w"][l], small["pool_scale"][l], small["branch_g"][l] = _diag_blocks(dpw, groups), dps[0], dbg[0]
        small["ffn_conv_b"][l] = jnp.concatenate([dfbg[0], dfbv[0]])
        small["conv_w"][l] = dcw
        small["ffn_conv_w"][l] = jnp.concatenate([dfwg, dfwv], axis=1)

    names = [n for n in SMALL_NAMES if n != "mod_b"] + list(SHARDED_SMALL)
    dmod_rows, _ = _pack([jnp.stack(dmods)])
    packed, layout = _pack([jnp.stack(dmods), loss_row] + [jnp.stack(small[n]) for n in names], 8 * N_DEV)
    gathered, summed = _allreduce_small(dmod_rows, packed)
    (sstates[0, 2],), token = _exchange_start([[g_in_parts]], True, summed, "scatter_start_in_0")
    parts = _unpack(summed, layout)
    loss = parts[1][0, 0]
    gsmall = dict(zip(names, parts[2:]))
    gsmall["mod_b"] = parts[0].reshape(nl, N_MOD * d)
    dmod_all = gathered[:, :nl * N_MOD].reshape(N_DEV, nl, N_MOD * d)
    dmod_mine = jnp.transpose(lax.dynamic_slice_in_dim(dmod_all, q * nmod, nmod, axis=2), (1, 0, 2))

    grads, deltas, new_m, new_v = {}, {}, {}, {}

    def put(name, res):
        grads[name], deltas[name], new_m[name], new_v[name] = res

    recv = dict(w_in=[None] * nl, w_out=[None] * nl, ffn_up=[None] * nl, ffn_down=[None] * nl)
    done = token
    for l in reversed(range(nl)):
        recv["ffn_down"][l], recv["ffn_up"][l] = _exchange_wait(sstates[l, 0], True, done, f"scatter_wait_ffn_{l}")
        (recv["w_out"][l],) = _exchange_wait(sstates[l, 1], True, recv["ffn_up"][l], f"scatter_wait_out_{l}")
        done = recv["w_out"][l]
    big = ("w_out", "ffn_up", "ffn_down", "w_in")
    mine = {n: _reduce4(recv[n], f"reduce4_{n}") for n in big[:3]}
    swap_a, token = _swap_start([mine[n] for n in big[:3]], mine["ffn_down"], "swap_start_a")

    put("mod_w", _modw_adamw(sc_all.T + token[0:1, 0:1], dmod_mine, w["mod_w"], m["mod_w"], v["mod_w"], "adamw_mod_w"))
    done = grads["mod_w"]
    for l in reversed(range(nl)):
        (recv["w_in"][l],) = _exchange_wait(sstates[l, 2], True, done, f"scatter_wait_in_{l}")
        done = recv["w_in"][l]
    mine["w_in"] = _reduce4(recv["w_in"], "reduce4_w_in")
    swap_b, token = _swap_start([mine["w_in"]], mine["w_in"], "swap_start_b")

    gsmall["conv_w"] = lax.dynamic_slice_in_dim(gsmall["conv_w"], q * (CONV_WIDTH // N_CHIPS), CONV_WIDTH // N_CHIPS, axis=2)
    gsmall["ffn_conv_w"] = lax.dynamic_slice_in_dim(gsmall["ffn_conv_w"], q * (f2 // N_CHIPS), f2 // N_CHIPS, axis=2)
    snames = SMALL_NAMES + SHARDED_SMALL
    res = _adamw_many([w[n] for n in snames], [m[n] for n in snames], [v[n] for n in snames], [gsmall[n] for n in snames],
                      "adamw_small")
    for n, (d_, m_, v_) in zip(snames, res):
        put(n, (gsmall[n], d_, m_, v_))

    sent, theirs = _swap_wait(swap_a, deltas["mod_b"], "swap_wait_a")
    sent_b, theirs_b = _swap_wait(swap_b, sent[0], "swap_wait_b")
    for n, a, b in zip(big, list(sent) + list(sent_b), list(theirs) + list(theirs_b)):
        put(n, _adamw(w[n], m[n], v[n], [a, b], f"adamw_{n}"))

    return (loss, dx.reshape(1, s, d), *[grads[n] for n in WEIGHT_ORDER], *[deltas[n] for n in WEIGHT_ORDER],
            *[new_m[n] for n in WEIGHT_ORDER], *[new_v[n] for n in WEIGHT_ORDER])


def kernel(x, c, mod_w, mod_b, mix_pre_g, mix_post_g, w_in, sgu_norm_g, sgu_norm_b, sgu_w, sgu_b, conv_w, conv_b, conv_norm_g, conv_norm_b, pool_w, pool_scale, branch_g, w_out, ffn_pre_g, ffn_post_g, ffn_up, ffn_conv_w, ffn_conv_b, ffn_down, loss_target, m_mod_w, m_mod_b, m_mix_pre_g, m_mix_post_g, m_w_in, m_sgu_norm_g, m_sgu_norm_b, m_sgu_w, m_sgu_b, m_conv_w, m_conv_b, m_conv_norm_g, m_conv_norm_b, m_pool_w, m_pool_scale, m_branch_g, m_w_out, m_ffn_pre_g, m_ffn_post_g, m_ffn_up, m_ffn_conv_w, m_ffn_conv_b, m_ffn_down, v_mod_w, v_mod_b, v_mix_pre_g, v_mix_post_g, v_w_in, v_sgu_norm_g, v_sgu_norm_b, v_sgu_w, v_sgu_b, v_conv_w, v_conv_b, v_conv_norm_g, v_conv_norm_b, v_pool_w, v_pool_scale, v_branch_g, v_w_out, v_ffn_pre_g, v_ffn_post_g, v_ffn_up, v_ffn_conv_w, v_ffn_conv_b, v_ffn_down):
    w = dict(mod_w=mod_w, mod_b=mod_b, mix_pre_g=mix_pre_g, mix_post_g=mix_post_g, w_in=w_in, sgu_norm_g=sgu_norm_g,
             sgu_norm_b=sgu_norm_b, sgu_w=sgu_w, sgu_b=sgu_b, conv_w=conv_w, conv_b=conv_b, conv_norm_g=conv_norm_g,
             conv_norm_b=conv_norm_b, pool_w=pool_w, pool_scale=pool_scale, branch_g=branch_g, w_out=w_out,
             ffn_pre_g=ffn_pre_g, ffn_post_g=ffn_post_g, ffn_up=ffn_up, ffn_conv_w=ffn_conv_w, ffn_conv_b=ffn_conv_b,
             ffn_down=ffn_down)
    m = dict(mod_w=m_mod_w, mod_b=m_mod_b, mix_pre_g=m_mix_pre_g, mix_post_g=m_mix_post_g, w_in=m_w_in,
             sgu_norm_g=m_sgu_norm_g, sgu_norm_b=m_sgu_norm_b, sgu_w=m_sgu_w, sgu_b=m_sgu_b, conv_w=m_conv_w,
             conv_b=m_conv_b, conv_norm_g=m_conv_norm_g, conv_norm_b=m_conv_norm_b, pool_w=m_pool_w,
             pool_scale=m_pool_scale, branch_g=m_branch_g, w_out=m_w_out, ffn_pre_g=m_ffn_pre_g, ffn_post_g=m_ffn_post_g,
             ffn_up=m_ffn_up, ffn_conv_w=m_ffn_conv_w, ffn_conv_b=m_ffn_conv_b, ffn_down=m_ffn_down)
    v = dict(mod_w=v_mod_w, mod_b=v_mod_b, mix_pre_g=v_mix_pre_g, mix_post_g=v_mix_post_g, w_in=v_w_in,
             sgu_norm_g=v_sgu_norm_g, sgu_norm_b=v_sgu_norm_b, sgu_w=v_sgu_w, sgu_b=v_sgu_b, conv_w=v_conv_w,
             conv_b=v_conv_b, conv_norm_g=v_conv_norm_g, conv_norm_b=v_conv_norm_b, pool_w=v_pool_w,
             pool_scale=v_pool_scale, branch_g=v_branch_g, w_out=v_w_out, ffn_pre_g=v_ffn_pre_g, ffn_post_g=v_ffn_post_g,
             ffn_up=v_ffn_up, ffn_conv_w=v_ffn_conv_w, ffn_conv_b=v_ffn_conv_b, ffn_down=v_ffn_down)
    return _step(x, c, loss_target, w, m, v)
```

```python
import functools
import math

import jax
import jax.numpy as jnp
from jax import lax
from jax.experimental import pallas as pl
from jax.experimental.pallas import tpu as pltpu

F32 = jnp.float32
BF16 = jnp.bfloat16
MESH = pl.DeviceIdType.MESH

EPS = 1e-6
HEAD_DIM = 64
CHUNK = 128
SGU_WIDTH = 384
CONV_WIDTH = 384
POOL_WIDTH = 256
POOL_WINDOWS = (2, 4, 8, 16)
CONV_K = 31
FFN_CONV_K = 3
N_MOD = 6
N_CHIPS = 4
N_DEV = 8

ADAM_LR = 0.001
ADAM_B1 = 0.9
ADAM_B2 = 0.999
ADAM_EPS = 1e-08
ADAM_WD = 0.01
ADAM_STEP = 10

MIX_HALO = 32
FFN_HALO = 8
FFN_ROWS = 16
FFN_UNROLL = 8
LANES = 128
CONV_ROWS = 32
SMALL_COLS = 1024
VMEM_BYTES_V7X = 64 * 1024 * 1024


def _vmem_limit(estimate_bytes):
    return int(min(max(estimate_bytes, 16 * 1024 * 1024), VMEM_BYTES_V7X - 8 * 1024 * 1024))


def _row_tile(s, want):
    return want if s % want == 0 else math.gcd(s, want)


def _rsum(v):
    return jnp.sum(v, axis=0, keepdims=True)


def _rmean(v):
    return jnp.mean(v, axis=-1, keepdims=True)


def _gelu(v):
    k = math.sqrt(2.0 / math.pi)
    return 0.5 * v * (1.0 + jnp.tanh(k * (v + 0.044715 * v * v * v)))


def _gelu_grad(v):
    k = math.sqrt(2.0 / math.pi)
    t = jnp.tanh(k * (v + 0.044715 * v * v * v))
    return 0.5 * (1.0 + t) + 0.5 * v * (1.0 - t * t) * (k * (1.0 + 3.0 * 0.044715 * v * v))


def _dot(a, b):
    return jnp.dot(a, b, preferred_element_type=F32)


def _dot_nt(a, b):
    return lax.dot_general(a, b, (((1,), (1,)), ((), ())), preferred_element_type=F32)


def _dot_tn(a, b):
    return lax.dot_general(a, b, (((0,), (0,)), ((), ())), preferred_element_type=F32)


def _group_mean(v, bd):
    hi = v.astype(BF16)
    lo = (v - hi.astype(F32)).astype(BF16)
    return (_dot(hi, bd) + _dot(lo, bd)) * (1.0 / HEAD_DIM)


def _const_spec(shape):
    nd = len(shape)
    return pl.BlockSpec(shape, lambda *_: (0,) * nd)


def _hbm_call(body, **kw):
    call = pl.pallas_call(body, **kw)
    return lambda *args: call(*[pltpu.with_memory_space_constraint(a, pltpu.HBM) for a in args])


def _norm_mod_matmul(x, gain, s1p, shift, w, name, transposed=False):
    s, d = x.shape
    nb = w.shape[0]
    tn = w.shape[1] if transposed else w.shape[2]
    ts = _row_tile(s, 512 if nb * tn <= 2048 else 256)

    def body(x_ref, g_ref, s_ref, b_ref, w_ref, z_ref, h_ref):
        xv = x_ref[...]
        r = lax.rsqrt(_rmean(xv * xv) + EPS)
        h = ((xv * r) * g_ref[...] * s_ref[...] + b_ref[...]).astype(BF16)
        h_ref[...] = h
        for j in range(nb):
            z_ref[:, j * tn:(j + 1) * tn] = _dot_nt(h, w_ref[j]) if transposed else _dot(h, w_ref[j])

    vec = pl.BlockSpec((1, d), lambda i: (0, 0))
    return _hbm_call(
        body, name=name,
        grid=(s // ts,),
        in_specs=[pl.BlockSpec((ts, d), lambda i: (i, 0)), vec, vec, vec,
                  pl.BlockSpec(w.shape, lambda i: (0, 0, 0), pipeline_mode=pl.Buffered(1))],
        out_specs=[pl.BlockSpec((ts, nb * tn), lambda i: (i, 0)), pl.BlockSpec((ts, d), lambda i: (i, 0))],
        out_shape=[jax.ShapeDtypeStruct((s, nb * tn), F32), jax.ShapeDtypeStruct((s, d), BF16)],
        compiler_params=pltpu.CompilerParams(
            dimension_semantics=("arbitrary",),
            vmem_limit_bytes=_vmem_limit(2 * (ts * d * 4 + ts * nb * tn * 4 + ts * d * 2) + nb * d * tn * 2 + 4 * ts * d * 4)),
    )(x, gain, s1p, shift, w)


def _matmul_norm_resid(a, w, xres, gate, gpost, name):
    s, k = a.shape
    d = w.shape[1]
    ts = _row_tile(s, 512)

    def body(a_ref, w_ref, x_ref, gate_ref, gp_ref, o_ref, xn_ref):
        o = _dot(a_ref[...], w_ref[...])
        o_ref[...] = o
        r = lax.rsqrt(_rmean(o * o) + EPS)
        xn_ref[...] = x_ref[...] + gate_ref[...] * ((o * r) * gp_ref[...])

    vec = pl.BlockSpec((1, d), lambda i: (0, 0))
    row = pl.BlockSpec((ts, d), lambda i: (i, 0))
    return _hbm_call(
        body, name=name,
        grid=(s // ts,),
        in_specs=[pl.BlockSpec((ts, k), lambda i: (i, 0)),
                  pl.BlockSpec((k, d), lambda i: (0, 0), pipeline_mode=pl.Buffered(1)), row, vec, vec],
        out_specs=[row, row],
        out_shape=[jax.ShapeDtypeStruct((s, d), F32)] * 2,
        compiler_params=pltpu.CompilerParams(
            dimension_semantics=("arbitrary",),
            vmem_limit_bytes=_vmem_limit(2 * (ts * k * 2 + 3 * ts * d * 4) + k * d * 2 + 4 * ts * d * 4)),
    )(a, w, xres, gate, gpost)


def _wgrad(a, b, acols, bcols, out_struct, out_index, out_block, name):
    s = a.shape[0]
    ts = _row_tile(s, 1024)
    aw, afn = acols
    bw, bfn = bcols
    nj = out_index[0]
    oidx = out_index[1]

    def body(a_ref, b_ref, o_ref, acc):
        i = pl.program_id(1)

        @pl.when(i == 0)
        def _():
            acc[...] = jnp.zeros_like(acc)

        acc[...] += _dot_tn(a_ref[...], b_ref[...])

        @pl.when(i == pl.num_programs(1) - 1)
        def _():
            o_ref[...] = acc[...].astype(o_ref.dtype)

    return _hbm_call(
        body, name=name,
        grid=(nj, s // ts),
        in_specs=[pl.BlockSpec((ts, aw), lambda j, i: (i, afn(j))), pl.BlockSpec((ts, bw), lambda j, i: (i, bfn(j)))],
        out_specs=pl.BlockSpec(out_block, lambda j, i: oidx(j)),
        out_shape=out_struct,
        scratch_shapes=[pltpu.VMEM((aw, bw), F32)],
        compiler_params=pltpu.CompilerParams(
            dimension_semantics=("arbitrary", "arbitrary"),
            vmem_limit_bytes=_vmem_limit(2 * (ts * aw * 2 + ts * bw * 2) + 3 * aw * bw * 4 + ts * aw * 4)),
    )(a, b)


def _loss_head(xo, tgt):
    s, d = xo.shape
    ts = _row_tile(s, 512)

    def body(x_ref, t_ref, dx_ref, l_ref, acc):
        i = pl.program_id(0)

        @pl.when(i == 0)
        def _():
            acc[...] = jnp.zeros_like(acc)

        e = x_ref[...] - t_ref[...]
        dx_ref[...] = e * (1.0 / d)
        acc[...] += _rsum(e * e)

        @pl.when(i == pl.num_programs(0) - 1)
        def _():
            tot = jnp.sum(acc[...], axis=-1, keepdims=True) * (0.5 / d)
            l_ref[...] = jnp.broadcast_to(tot, l_ref.shape)

    row = pl.BlockSpec((ts, d), lambda i: (i, 0))
    return _hbm_call(
        body, name="loss_head",
        grid=(s // ts,),
        in_specs=[row, row],
        out_specs=[row, pl.BlockSpec((1, SMALL_COLS), lambda i: (0, 0))],
        out_shape=[jax.ShapeDtypeStruct((s, d), F32), jax.ShapeDtypeStruct((1, SMALL_COLS), F32)],
        scratch_shapes=[pltpu.VMEM((1, d), F32)],
        compiler_params=pltpu.CompilerParams(dimension_semantics=("arbitrary",)),
    )(xo, tgt)


def _resid_bwd_matmul(dxn, o, gate, gpost, w, name):
    s, d = dxn.shape
    k = w.shape[0]
    ts = _row_tile(s, 512)

    def body(dx_ref, o_ref, gate_ref, gp_ref, w_ref, do_ref, da_ref, dgate_ref, dgp_ref):
        i = pl.program_id(0)

        @pl.when(i == 0)
        def _():
            dgate_ref[...] = jnp.zeros_like(dgate_ref)
            dgp_ref[...] = jnp.zeros_like(dgp_ref)

        dx = dx_ref[...]
        o = o_ref[...]
        r = lax.rsqrt(_rmean(o * o) + EPS)
        on = o * r
        dgate_ref[...] += _rsum(dx * (on * gp_ref[...]))
        don = dx * gate_ref[...]
        dgp_ref[...] += _rsum(don * on)
        t = don * gp_ref[...]
        do = (r * (t - on * _rmean(t * on))).astype(BF16)
        do_ref[...] = do
        da_ref[...] = _dot_nt(do, w_ref[...])

    vec = pl.BlockSpec((1, d), lambda i: (0, 0))
    row = pl.BlockSpec((ts, d), lambda i: (i, 0))
    return _hbm_call(
        body, name=name,
        grid=(s // ts,),
        in_specs=[row, row, vec, vec, pl.BlockSpec((k, d), lambda i: (0, 0), pipeline_mode=pl.Buffered(1))],
        out_specs=[row, pl.BlockSpec((ts, k), lambda i: (i, 0)), vec, vec],
        out_shape=[jax.ShapeDtypeStruct((s, d), BF16), jax.ShapeDtypeStruct((s, k), F32),
                   jax.ShapeDtypeStruct((1, d), F32), jax.ShapeDtypeStruct((1, d), F32)],
        compiler_params=pltpu.CompilerParams(
            dimension_semantics=("arbitrary",),
            vmem_limit_bytes=_vmem_limit(2 * (2 * ts * d * 4 + ts * d * 2 + ts * k * 4) + d * k * 2 + 6 * ts * d * 4)),
    )(dxn, o, gate, gpost, w)


def _norm_mod_bwd(dh, xv, gain, s1p, dres):
    r = lax.rsqrt(_rmean(xv * xv) + EPS)
    xn = xv * r
    dshift = _rsum(dh)
    t = dh * xn
    dscale = _rsum(t * gain)
    dgain = _rsum(t * s1p)
    dxn = dh * (gain * s1p)
    dx = r * (dxn - xn * _rmean(dxn * xn)) + dres
    return dx, dshift, dscale, dgain


def _lane_lt(shape, bound):
    return lax.broadcasted_iota(jnp.int32, shape, 1) < bound


def _sgu_forward(z_ref, bd_ref, ng_ref, nb_ref, wm_ref, bias_ref, ts, ya_s, f_s):
    u = _gelu(z_ref[:, 0:SGU_WIDTH])
    v = _gelu(z_ref[:, SGU_WIDTH:2 * SGU_WIDTH])
    bd = bd_ref[...]
    vc = v - _group_mean(v, bd)
    rstd = lax.rsqrt(_group_mean(vc * vc, bd) + EPS)
    vhat = vc * rstd
    vn = (vhat * ng_ref[...] + nb_ref[...]).astype(BF16)
    left = _lane_lt((CHUNK, CHUNK), HEAD_DIM)
    for n in range(ts // CHUNK):
        rows = slice(n * CHUNK, (n + 1) * CHUNK)
        for p in range(SGU_WIDTH // CHUNK):
            cols = slice(p * CHUNK, (p + 1) * CHUNK)
            blk = vn[rows, cols]
            f = jnp.where(left, _dot(wm_ref[2 * p], blk), _dot(wm_ref[2 * p + 1], blk)) + bias_ref[:, cols]
            if f_s is not None:
                f_s[rows, cols] = f
            ya_s[rows, cols] = u[rows, cols] * f
    return u, vhat, rstd, vn


def _conv31_forward(z_ref, zh_ref, first, cw_ref, cb_ref, ts, ext_b, cbs):
    a = z_ref[:, 2 * SGU_WIDTH:2 * SGU_WIDTH + CONV_WIDTH]
    g = z_ref[:, 2 * SGU_WIDTH + CONV_WIDTH:2 * SGU_WIDTH + 2 * CONV_WIDTH]
    ah = zh_ref[:, 2 * SGU_WIDTH:2 * SGU_WIDTH + CONV_WIDTH]
    gh = zh_ref[:, 2 * SGU_WIDTH + CONV_WIDTH:2 * SGU_WIDTH + 2 * CONV_WIDTH]
    ext_b[pl.ds(0, MIX_HALO), :] = jnp.where(first, 0.0, ah * jax.nn.sigmoid(gh))
    ext_b[pl.ds(MIX_HALO, ts), :] = a * jax.nn.sigmoid(g)
    for r in range(ts // CONV_ROWS):
        acc = jnp.broadcast_to(cb_ref[...], (CONV_ROWS, CONV_WIDTH))
        for k in range(CONV_K):
            acc = acc + cw_ref[k:k + 1, :] * ext_b[pl.ds(MIX_HALO - (CONV_K - 1) + k + r * CONV_ROWS, CONV_ROWS), :]
        cbs[pl.ds(r * CONV_ROWS, CONV_ROWS), :] = acc


def _pool_counts(i, ts):
    pos1 = (i * ts + 1 + lax.broadcasted_iota(jnp.int32, (ts, POOL_WIDTH), 0)).astype(F32)
    lane = lax.broadcasted_iota(jnp.int32, (ts, POOL_WIDTH), 1)
    gdim = POOL_WIDTH // len(POOL_WINDOWS)
    win = jnp.where(lane < gdim, float(POOL_WINDOWS[0]),
                    jnp.where(lane < 2 * gdim, float(POOL_WINDOWS[1]),
                              jnp.where(lane < 3 * gdim, float(POOL_WINDOWS[2]), float(POOL_WINDOWS[3]))))
    return jnp.minimum(pos1, win)


def _window_sums(ext, base, ts, sign):
    lane = lax.broadcasted_iota(jnp.int32, (ts, POOL_WIDTH), 1)
    gdim = POOL_WIDTH // len(POOL_WINDOWS)
    run = jnp.zeros((ts, POOL_WIDTH), F32)
    out = jnp.zeros((ts, POOL_WIDTH), F32)
    for m in range(POOL_WINDOWS[-1]):
        run = run + ext[pl.ds(base + sign * m, ts), :]
        for gi, win in enumerate(POOL_WINDOWS):
            if m == win - 1:
                out = jnp.where((lane >= gi * gdim) & (lane < (gi + 1) * gdim), run, out)
    return out


def _pool_forward(z_ref, zh_ref, first, i, ts, ext_c):
    c0 = 2 * SGU_WIDTH + 2 * CONV_WIDTH
    zc = z_ref[:, c0:c0 + POOL_WIDTH]
    ext_c[pl.ds(0, MIX_HALO), :] = jnp.where(first, 0.0, zh_ref[:, c0:c0 + POOL_WIDTH])
    ext_c[pl.ds(MIX_HALO, ts), :] = zc
    sums = _window_sums(ext_c, MIX_HALO, ts, -1)
    return sums / _pool_counts(i, ts) - zc


def _layer_norm_rows(v):
    mu = _rmean(v)
    vc = v - mu
    rstd = lax.rsqrt(_rmean(vc * vc) + EPS)
    return vc * rstd, rstd


def _mixer_specs(s, ts, width):
    nbh = ts // MIX_HALO
    tile = pl.BlockSpec((ts, width), lambda i: (i, 0))
    prev = pl.BlockSpec((MIX_HALO, width), lambda i: (jnp.maximum(i * nbh - 1, 0), 0))
    nxt = pl.BlockSpec((MIX_HALO, width), lambda i: (jnp.minimum((i + 1) * nbh, s // MIX_HALO - 1), 0))
    return tile, prev, nxt


def _mixer_fwd(z, mp, name):
    s, inw = z.shape
    d = SGU_WIDTH + CONV_WIDTH + POOL_WIDTH
    ts = _row_tile(s, 256)

    def body(z_ref, zh_ref, bd_ref, ng_ref, nb_ref, wm_ref, bias_ref, cw_ref, cb_ref, cng_ref, cnb_ref,
             pw_ref, ps_ref, bg_ref, y_ref, cbs, dpool_ref, ya_s, ext_b, ext_c):
        i = pl.program_id(0)
        first = i == 0
        _sgu_forward(z_ref, bd_ref, ng_ref, nb_ref, wm_ref, bias_ref, ts, ya_s, None)
        ya = ya_s[...]
        ra = lax.rsqrt(_rmean(ya * ya) + EPS)
        y_ref[:, 0:SGU_WIDTH] = ((ya * ra) * bg_ref[:, 0:SGU_WIDTH]).astype(BF16)

        _conv31_forward(z_ref, zh_ref, first, cw_ref, cb_ref, ts, ext_b, cbs)
        chat, _ = _layer_norm_rows(cbs[...])
        lin = chat * cng_ref[...] + cnb_ref[...]
        yb = lin * jax.nn.sigmoid(lin)
        rb = lax.rsqrt(_rmean(yb * yb) + EPS)
        y_ref[:, SGU_WIDTH:SGU_WIDTH + CONV_WIDTH] = ((yb * rb) * bg_ref[:, SGU_WIDTH:SGU_WIDTH + CONV_WIDTH]).astype(BF16)

        dpool = _pool_forward(z_ref, zh_ref, first, i, ts, ext_c)
        dpool_ref[...] = dpool
        yc = _dot(dpool.astype(BF16), pw_ref[...]) * ps_ref[...]
        rc = lax.rsqrt(_rmean(yc * yc) + EPS)
        y_ref[:, SGU_WIDTH + CONV_WIDTH:d] = ((yc * rc) * bg_ref[:, SGU_WIDTH + CONV_WIDTH:d]).astype(BF16)

    tile, prev, _ = _mixer_specs(s, ts, inw)
    consts = [mp["bd"], mp["ng"], mp["nb"], mp["wm"], mp["bias"], mp["cw"], mp["cb"], mp["cng"], mp["cnb"],
              mp["pw"], mp["ps"], mp["bg"]]
    return _hbm_call(
        body, name=name,
        grid=(s // ts,),
        in_specs=[tile, prev] + [_const_spec(c.shape) for c in consts],
        out_specs=[pl.BlockSpec((ts, d), lambda i: (i, 0)), pl.BlockSpec((ts, CONV_WIDTH), lambda i: (i, 0)),
                   pl.BlockSpec((ts, POOL_WIDTH), lambda i: (i, 0))],
        out_shape=[jax.ShapeDtypeStruct((s, d), BF16), jax.ShapeDtypeStruct((s, CONV_WIDTH), F32),
                   jax.ShapeDtypeStruct((s, POOL_WIDTH), F32)],
        scratch_shapes=[pltpu.VMEM((ts, SGU_WIDTH), F32), pltpu.VMEM((ts + MIX_HALO, CONV_WIDTH), F32),
                        pltpu.VMEM((ts + MIX_HALO, POOL_WIDTH), F32)],
        compiler_params=pltpu.CompilerParams(dimension_semantics=("arbitrary",),
                                             vmem_limit_bytes=_vmem_limit(16 * ts * inw * 4)),
    )(z, z, *consts)


def _mixer_bwd_a(z, cb, dpool, dy, mp, name):
    s, inw = z.shape
    d = SGU_WIDTH + CONV_WIDTH + POOL_WIDTH
    ts = _row_tile(s, 256)
    nchunk = ts // CHUNK

    def rms_bwd(dyn, y, g):
        r = lax.rsqrt(_rmean(y * y) + EPS)
        yn = y * r
        dg = _rsum(dyn * yn)
        t = dyn * g
        return r * (t - yn * _rmean(t * yn)), dg

    def body(z_ref, cbs, dpool_ref, dy_ref, bd_ref, ng_ref, nb_ref, wm_ref, wmt_ref, bias_ref, cng_ref, cnb_ref,
             pw_ref, pwt_ref, ps_ref, bg_ref,
             dza_ref, dcb_ref, dd_ref, dbg_ref, dwm_ref, dbias_ref, dng_ref, dnb_ref, dcng_ref, dcnb_ref, dps_ref, dpw_ref,
             ya_s, f_s, dvn_s):
        i = pl.program_id(0)
        first = i == 0

        @pl.when(first)
        def _():
            for ref in (dwm_ref, dbias_ref, dng_ref, dnb_ref, dcng_ref, dcnb_ref, dps_ref, dpw_ref):
                ref[...] = jnp.zeros_like(ref)

        u, vhat, rstd, vn = _sgu_forward(z_ref, bd_ref, ng_ref, nb_ref, wm_ref, bias_ref, ts, ya_s, f_s)
        dya, dbg_a = rms_bwd(dy_ref[:, 0:SGU_WIDTH], ya_s[...], bg_ref[:, 0:SGU_WIDTH])
        du = dya * f_s[...]
        df = dya * u
        dfb = df.astype(BF16)
        left = _lane_lt((CHUNK, CHUNK), HEAD_DIM)
        zero = jnp.zeros((CHUNK, CHUNK), BF16)
        dbias = jnp.zeros((CHUNK, SGU_WIDTH), F32)
        for n in range(nchunk):
            rows = slice(n * CHUNK, (n + 1) * CHUNK)
            dbias = dbias + df[rows, :]
            for p in range(SGU_WIDTH // CHUNK):
                cols = slice(p * CHUNK, (p + 1) * CHUNK)
                dblk = dfb[rows, cols]
                vblk = vn[rows, cols]
                dwm_ref[2 * p] += _dot_nt(jnp.where(left, dblk, zero), vblk)
                dwm_ref[2 * p + 1] += _dot_nt(jnp.where(left, zero, dblk), vblk)
                dvn_s[rows, cols] = jnp.where(left, _dot(wmt_ref[2 * p], dblk), _dot(wmt_ref[2 * p + 1], dblk))
        dbias_ref[...] += dbias
        dvn = dvn_s[...]
        dng_ref[...] += _rsum(dvn * vhat)
        dnb_ref[...] += _rsum(dvn)
        dvh = dvn * ng_ref[...]
        bd = bd_ref[...]
        dv = rstd * (dvh - _group_mean(dvh, bd) - vhat * _group_mean(dvh * vhat, bd))
        dza_ref[:, 0:SGU_WIDTH] = (du * _gelu_grad(z_ref[:, 0:SGU_WIDTH])).astype(BF16)
        dza_ref[:, SGU_WIDTH:2 * SGU_WIDTH] = (dv * _gelu_grad(z_ref[:, SGU_WIDTH:2 * SGU_WIDTH])).astype(BF16)

        chat, crstd = _layer_norm_rows(cbs[...])
        lin = chat * cng_ref[...] + cnb_ref[...]
        sl = jax.nn.sigmoid(lin)
        dyb, dbg_b = rms_bwd(dy_ref[:, SGU_WIDTH:SGU_WIDTH + CONV_WIDTH], lin * sl, bg_ref[:, SGU_WIDTH:SGU_WIDTH + CONV_WIDTH])
        dlin = dyb * (sl * (1.0 + lin * (1.0 - sl)))
        dcng_ref[...] += _rsum(dlin * chat)
        dcnb_ref[...] += _rsum(dlin)
        dch = dlin * cng_ref[...]
        dcb_ref[...] = crstd * (dch - _rmean(dch) - chat * _rmean(dch * chat))

        dpb = dpool_ref[...].astype(BF16)
        ycp = _dot(dpb, pw_ref[...])
        dyc, dbg_c = rms_bwd(dy_ref[:, SGU_WIDTH + CONV_WIDTH:d], ycp * ps_ref[...], bg_ref[:, SGU_WIDTH + CONV_WIDTH:d])
        dps_ref[...] += _rsum(dyc * ycp)
        dycp = (dyc * ps_ref[...]).astype(BF16)
        dpw_ref[...] += _dot_tn(dpb, dycp)
        dd_ref[...] = _dot(dycp, pwt_ref[...])

        @pl.when(first)
        def _():
            dbg_ref[...] = jnp.zeros_like(dbg_ref)

        dbg_ref[:, 0:SGU_WIDTH] += dbg_a
        dbg_ref[:, SGU_WIDTH:SGU_WIDTH + CONV_WIDTH] += dbg_b
        dbg_ref[:, SGU_WIDTH + CONV_WIDTH:d] += dbg_c

    tile, _, _ = _mixer_specs(s, ts, inw)
    consts = [mp["bd"], mp["ng"], mp["nb"], mp["wm"], mp["wmt"], mp["bias"], mp["cng"], mp["cnb"],
              mp["pw"], mp["pwt"], mp["ps"], mp["bg"]]
    acc_shapes = [(1, d), (2 * (SGU_WIDTH // CHUNK), CHUNK, CHUNK), (CHUNK, SGU_WIDTH), (1, SGU_WIDTH), (1, SGU_WIDTH),
                  (1, CONV_WIDTH), (1, CONV_WIDTH), (1, POOL_WIDTH), (POOL_WIDTH, POOL_WIDTH)]
    return _hbm_call(
        body, name=name,
        grid=(s // ts,),
        in_specs=[tile, pl.BlockSpec((ts, CONV_WIDTH), lambda i: (i, 0)), pl.BlockSpec((ts, POOL_WIDTH), lambda i: (i, 0)),
                  pl.BlockSpec((ts, d), lambda i: (i, 0))] + [_const_spec(c.shape) for c in consts],
        out_specs=[pl.BlockSpec((ts, 2 * SGU_WIDTH), lambda i: (i, 0)), pl.BlockSpec((ts, CONV_WIDTH), lambda i: (i, 0)),
                   pl.BlockSpec((ts, POOL_WIDTH), lambda i: (i, 0))] + [_const_spec(a) for a in acc_shapes],
        out_shape=[jax.ShapeDtypeStruct((s, 2 * SGU_WIDTH), BF16), jax.ShapeDtypeStruct((s, CONV_WIDTH), F32),
                   jax.ShapeDtypeStruct((s, POOL_WIDTH), F32)] + [jax.ShapeDtypeStruct(a, F32) for a in acc_shapes],
        scratch_shapes=[pltpu.VMEM((ts, SGU_WIDTH), F32), pltpu.VMEM((ts, SGU_WIDTH), F32), pltpu.VMEM((ts, SGU_WIDTH), F32)],
        compiler_params=pltpu.CompilerParams(dimension_semantics=("arbitrary",),
                                             vmem_limit_bytes=_vmem_limit(24 * ts * inw * 4)),
    )(z, cb, dpool, dy, *consts)


def _mixer_bwd_b(z, dza, dcb, dd, x, dres, gain, s1p, cw, w, name):
    s, inw = z.shape
    d = x.shape[1]
    ts = _row_tile(s, 256)
    c0 = 2 * SGU_WIDTH
    c1 = c0 + 2 * CONV_WIDTH

    def body(z_ref, zh_ref, dza_ref, dcb_ref, dcbn_ref, dd_ref, ddn_ref, x_ref, dres_ref, g_ref, s_ref, cw_ref, w_ref,
             dx_ref, dz_ref, dsh_ref, dsc_ref, dg_ref, dcw_ref, dcbias_ref, ext_b, ext_n, ext_e):
        i = pl.program_id(0)
        first = i == 0
        last = i == pl.num_programs(0) - 1

        @pl.when(first)
        def _():
            for ref in (dsh_ref, dsc_ref, dg_ref, dcw_ref, dcbias_ref):
                ref[...] = jnp.zeros_like(ref)

        a = z_ref[:, c0:c0 + CONV_WIDTH]
        sg = jax.nn.sigmoid(z_ref[:, c0 + CONV_WIDTH:c1])
        ah = zh_ref[:, c0:c0 + CONV_WIDTH]
        gh = zh_ref[:, c0 + CONV_WIDTH:c1]
        ext_b[pl.ds(0, MIX_HALO), :] = jnp.where(first, 0.0, ah * jax.nn.sigmoid(gh))
        ext_b[pl.ds(MIX_HALO, ts), :] = a * sg
        dcbv = dcb_ref[...]
        dcbias_ref[...] += _rsum(dcbv)
        for k in range(CONV_K):
            dcw_ref[k:k + 1, :] += _rsum(dcbv * ext_b[pl.ds(MIX_HALO - (CONV_K - 1) + k, ts), :])

        ext_n[pl.ds(0, ts), :] = dcbv
        ext_n[pl.ds(ts, MIX_HALO), :] = jnp.where(last, 0.0, dcbn_ref[...])
        for r in range(ts // CONV_ROWS):
            acc = jnp.zeros((CONV_ROWS, CONV_WIDTH), F32)
            for k in range(CONV_K):
                acc = acc + cw_ref[k:k + 1, :] * ext_n[pl.ds(CONV_K - 1 - k + r * CONV_ROWS, CONV_ROWS), :]
            rows = pl.ds(r * CONV_ROWS, CONV_ROWS)
            ar = z_ref[rows, c0:c0 + CONV_WIDTH]
            sr = jax.nn.sigmoid(z_ref[rows, c0 + CONV_WIDTH:c1])
            dz_ref[rows, c0:c0 + CONV_WIDTH] = (acc * sr).astype(BF16)
            dz_ref[rows, c0 + CONV_WIDTH:c1] = (acc * ar * sr * (1.0 - sr)).astype(BF16)

        ddv = dd_ref[...]
        ext_e[pl.ds(0, ts), :] = ddv / _pool_counts(i, ts)
        nh = (i + 1) * ts + lax.broadcasted_iota(jnp.int32, (MIX_HALO, POOL_WIDTH), 0)
        lane = lax.broadcasted_iota(jnp.int32, (MIX_HALO, POOL_WIDTH), 1)
        gdim = POOL_WIDTH // len(POOL_WINDOWS)
        winh = jnp.where(lane < gdim, float(POOL_WINDOWS[0]),
                         jnp.where(lane < 2 * gdim, float(POOL_WINDOWS[1]),
                                   jnp.where(lane < 3 * gdim, float(POOL_WINDOWS[2]), float(POOL_WINDOWS[3]))))
        cnth = jnp.minimum((nh + 1).astype(F32), winh)
        ext_e[pl.ds(ts, MIX_HALO), :] = jnp.where(last, 0.0, ddn_ref[...] / cnth)
        dz_ref[:, c1:inw] = (_window_sums(ext_e, 0, ts, 1) - ddv).astype(BF16)
        dz_ref[:, 0:c0] = dza_ref[...]

        dh = _dot(dz_ref[...], w_ref[...])
        dx, dsh, dsc, dg = _norm_mod_bwd(dh, x_ref[...], g_ref[...], s_ref[...], dres_ref[...])
        dx_ref[...] = dx
        dsh_ref[...] += dsh
        dsc_ref[...] += dsc
        dg_ref[...] += dg

    tile, prev, _ = _mixer_specs(s, ts, inw)
    _, _, nxt_b = _mixer_specs(s, ts, CONV_WIDTH)
    _, _, nxt_c = _mixer_specs(s, ts, POOL_WIDTH)
    row = pl.BlockSpec((ts, d), lambda i: (i, 0))
    vec = pl.BlockSpec((1, d), lambda i: (0, 0))
    return _hbm_call(
        body, name=name,
        grid=(s // ts,),
        in_specs=[tile, prev, pl.BlockSpec((ts, c0), lambda i: (i, 0)),
                  pl.BlockSpec((ts, CONV_WIDTH), lambda i: (i, 0)), nxt_b,
                  pl.BlockSpec((ts, POOL_WIDTH), lambda i: (i, 0)), nxt_c,
                  row, row, vec, vec, _const_spec(cw.shape),
                  pl.BlockSpec(w.shape, lambda i: (0, 0), pipeline_mode=pl.Buffered(1))],
        out_specs=[row, pl.BlockSpec((ts, inw), lambda i: (i, 0)), vec, vec, vec,
                   _const_spec((CONV_K, CONV_WIDTH)), _const_spec((1, CONV_WIDTH))],
        out_shape=[jax.ShapeDtypeStruct((s, d), F32), jax.ShapeDtypeStruct((s, inw), BF16)]
        + [jax.ShapeDtypeStruct((1, d), F32)] * 3
        + [jax.ShapeDtypeStruct((CONV_K, CONV_WIDTH), F32), jax.ShapeDtypeStruct((1, CONV_WIDTH), F32)],
        scratch_shapes=[pltpu.VMEM((ts + MIX_HALO, CONV_WIDTH), F32), pltpu.VMEM((ts + MIX_HALO, CONV_WIDTH), F32),
                        pltpu.VMEM((ts + MIX_HALO, POOL_WIDTH), F32)],
        compiler_params=pltpu.CompilerParams(dimension_semantics=("arbitrary",),
                                             vmem_limit_bytes=_vmem_limit(16 * ts * inw * 4 + inw * d * 2)),
    )(z, z, dza, dcb, dcb, dd, dd, x, dres, gain, s1p, cw, w)


def _ffn_specs(s, ts, tc, half_blocks):
    nbh = ts // FFN_HALO

    def tile(off):
        return pl.BlockSpec((ts, tc), lambda j, i: (i, j + off))

    def prev(off):
        return pl.BlockSpec((FFN_HALO, tc), lambda j, i: (jnp.maximum(i * nbh - 1, 0), j + off))

    def vec(rows, off):
        return pl.BlockSpec((rows, tc), lambda j, i: (0, j + off))

    return tile, prev, vec


def _rows_before(cur, prev, k):
    row = lax.broadcasted_iota(jnp.int32, cur.shape, 0)
    return jnp.where(row >= k, pltpu.roll(cur, k, 0), pltpu.roll(prev, k, 0))


def _conv3_rows(cur, prev, w_ref, b_ref, cols):
    x1 = _rows_before(cur, prev, 1)
    x2 = _rows_before(cur, prev, 2)
    u = b_ref[:, cols] + w_ref[2:3, cols] * cur + w_ref[1:2, cols] * x1 + w_ref[0:1, cols] * x2
    return u, x2, x1


def _halo_chunk(h_ref, cols, first):
    h = jnp.where(first, 0.0, h_ref[:, cols])
    return jnp.concatenate([h] * (FFN_ROWS // FFN_HALO), axis=0)


def _ffn_act_fwd(p, cw, cb, name):
    s, f2 = p.shape
    f = f2 // 2
    tc = f // 2
    hb = f // tc
    ts = _row_tile(s, 256)

    def body(pg_ref, pgh_ref, pv_ref, pvh_ref, wg_ref, wv_ref, bg_ref, bv_ref, act_ref):
        first = pl.program_id(1) == 0
        for c in range(tc // LANES):
            cols = slice(c * LANES, (c + 1) * LANES)

            def chunk(r, carry, cols=cols):
                pg_prev, pv_prev = carry
                rows = pl.ds(pl.multiple_of(r * FFN_ROWS, FFN_ROWS), FFN_ROWS)
                pg = pg_ref[rows, cols]
                pv = pv_ref[rows, cols]
                ug, _, _ = _conv3_rows(pg, pg_prev, wg_ref, bg_ref, cols)
                uv, _, _ = _conv3_rows(pv, pv_prev, wv_ref, bv_ref, cols)
                act_ref[rows, cols] = (_gelu(ug) * uv).astype(BF16)
                return pg, pv

            def step(r, carry, chunk=chunk):
                for u in range(FFN_UNROLL):
                    carry = chunk(r * FFN_UNROLL + u, carry)
                return carry

            lax.fori_loop(0, ts // (FFN_ROWS * FFN_UNROLL), step,
                          (_halo_chunk(pgh_ref, cols, first), _halo_chunk(pvh_ref, cols, first)))

    tile, prev, vec = _ffn_specs(s, ts, tc, hb)
    return _hbm_call(
        body, name=name,
        grid=(hb, s // ts),
        in_specs=[tile(0), prev(0), tile(hb), prev(hb), vec(FFN_CONV_K, 0), vec(FFN_CONV_K, hb), vec(1, 0), vec(1, hb)],
        out_specs=pl.BlockSpec((ts, tc), lambda j, i: (i, j)),
        out_shape=jax.ShapeDtypeStruct((s, f), BF16),
        compiler_params=pltpu.CompilerParams(dimension_semantics=("arbitrary", "arbitrary"),
                                             vmem_limit_bytes=_vmem_limit(8 * ts * tc * 4)),
    )(p, p, p, p, cw, cw, cb, cb)


def _ffn_act_bwd(p, dact, cw, cb, name):
    s, f2 = p.shape
    f = f2 // 2
    tc = f // 2
    hb = f // tc
    ts = _row_tile(s, 256)

    def body(pg_ref, pgh_ref, pv_ref, pvh_ref, da_ref, wg_ref, wv_ref, bg_ref, bv_ref,
             dug_ref, duv_ref, dwg_ref, dwv_ref, dbg_ref, dbv_ref):
        first = pl.program_id(1) == 0

        @pl.when(first)
        def _():
            for ref in (dwg_ref, dwv_ref, dbg_ref, dbv_ref):
                ref[...] = jnp.zeros_like(ref)

        zero = jnp.zeros((FFN_ROWS, LANES), F32)
        for c in range(tc // LANES):
            cols = slice(c * LANES, (c + 1) * LANES)

            def chunk(r, carry, cols=cols):
                pg_prev, pv_prev, ag0, ag1, ag2, av0, av1, av2, sg, sv = carry
                rows = pl.ds(pl.multiple_of(r * FFN_ROWS, FFN_ROWS), FFN_ROWS)
                pg = pg_ref[rows, cols]
                pv = pv_ref[rows, cols]
                ug, pg2, pg1 = _conv3_rows(pg, pg_prev, wg_ref, bg_ref, cols)
                uv, pv2, pv1 = _conv3_rows(pv, pv_prev, wv_ref, bv_ref, cols)
                da = da_ref[rows, cols]
                dug = da * uv * _gelu_grad(ug)
                duv = da * _gelu(ug)
                dug_ref[rows, cols] = dug
                duv_ref[rows, cols] = duv
                return (pg, pv, ag0 + dug * pg2, ag1 + dug * pg1, ag2 + dug * pg,
                        av0 + duv * pv2, av1 + duv * pv1, av2 + duv * pv, sg + dug, sv + duv)

            def step(r, carry, chunk=chunk):
                for u in range(FFN_UNROLL):
                    carry = chunk(r * FFN_UNROLL + u, carry)
                return carry

            out = lax.fori_loop(0, ts // (FFN_ROWS * FFN_UNROLL), step,
                                (_halo_chunk(pgh_ref, cols, first), _halo_chunk(pvh_ref, cols, first)) + (zero,) * 8)
            for k in range(FFN_CONV_K):
                dwg_ref[k:k + 1, cols] += _rsum(out[2 + k])
                dwv_ref[k:k + 1, cols] += _rsum(out[5 + k])
            dbg_ref[:, cols] += _rsum(out[8])
            dbv_ref[:, cols] += _rsum(out[9])

    tile, prev, vec = _ffn_specs(s, ts, tc, hb)
    half = pl.BlockSpec((ts, tc), lambda j, i: (i, j))
    wacc = pl.BlockSpec((FFN_CONV_K, tc), lambda j, i: (0, j))
    bacc = pl.BlockSpec((1, tc), lambda j, i: (0, j))
    return _hbm_call(
        body, name=name,
        grid=(hb, s // ts),
        in_specs=[tile(0), prev(0), tile(hb), prev(hb), half, vec(FFN_CONV_K, 0), vec(FFN_CONV_K, hb), vec(1, 0), vec(1, hb)],
        out_specs=[half, half, wacc, wacc, bacc, bacc],
        out_shape=[jax.ShapeDtypeStruct((s, f), F32)] * 2 + [jax.ShapeDtypeStruct((FFN_CONV_K, f), F32)] * 2
        + [jax.ShapeDtypeStruct((1, f), F32)] * 2,
        compiler_params=pltpu.CompilerParams(dimension_semantics=("arbitrary", "arbitrary"),
                                             vmem_limit_bytes=_vmem_limit(12 * ts * tc * 4)),
    )(p, p, p, p, dact, cw, cw, cb, cb)


def _ffn_in_bwd(dug, duv, cw, w, x, dres, gain, s1p, name):
    s, f = dug.shape
    d = x.shape[1]
    ts = _row_tile(s, 256)
    tc = w.shape[2]
    assert f % tc == 0 and w.shape[0] * tc == 2 * f
    nbh = ts // FFN_HALO

    def body(dug_ref, dugn_ref, duv_ref, duvn_ref, cw_ref, w_ref, x_ref, dres_ref, g_ref, s_ref,
             dx_ref, dp_ref, dsh_ref, dsc_ref, dg_ref, ext):
        i = pl.program_id(0)
        last = i == pl.num_programs(0) - 1

        @pl.when(i == 0)
        def _():
            for ref in (dsh_ref, dsc_ref, dg_ref):
                ref[...] = jnp.zeros_like(ref)

        dh = jnp.zeros((ts, d), F32)
        for half, (t_ref, n_ref) in enumerate(((dug_ref, dugn_ref), (duv_ref, duvn_ref))):
            for cb in range(f // tc):
                cols = slice(cb * tc, (cb + 1) * tc)
                wcols = slice(half * f + cb * tc, half * f + (cb + 1) * tc)
                ext[pl.ds(0, ts), :] = t_ref[:, cols]
                ext[pl.ds(ts, FFN_HALO), :] = jnp.where(last, 0.0, n_ref[:, cols])
                acc = cw_ref[FFN_CONV_K - 1:FFN_CONV_K, wcols] * t_ref[:, cols]
                for k in range(FFN_CONV_K - 1):
                    acc = acc + cw_ref[k:k + 1, wcols] * ext[pl.ds(FFN_CONV_K - 1 - k, ts), :]
                dpb = acc.astype(BF16)
                dp_ref[:, wcols] = dpb
                dh = dh + _dot_nt(dpb, w_ref[half * (f // tc) + cb])
        dx, dsh, dsc, dg = _norm_mod_bwd(dh, x_ref[...], g_ref[...], s_ref[...], dres_ref[...])
        dx_ref[...] = dx
        dsh_ref[...] += dsh
        dsc_ref[...] += dsc
        dg_ref[...] += dg

    tile = pl.BlockSpec((ts, f), lambda i: (i, 0))
    nxt = pl.BlockSpec((FFN_HALO, f), lambda i: (jnp.minimum((i + 1) * nbh, s // FFN_HALO - 1), 0))
    row = pl.BlockSpec((ts, d), lambda i: (i, 0))
    vec = pl.BlockSpec((1, d), lambda i: (0, 0))
    return _hbm_call(
        body, name=name,
        grid=(s // ts,),
        in_specs=[tile, nxt, tile, nxt, _const_spec(cw.shape),
                  pl.BlockSpec(w.shape, lambda i: (0, 0, 0), pipeline_mode=pl.Buffered(1)), row, row, vec, vec],
        out_specs=[row, pl.BlockSpec((ts, 2 * f), lambda i: (i, 0)), vec, vec, vec],
        out_shape=[jax.ShapeDtypeStruct((s, d), F32), jax.ShapeDtypeStruct((s, 2 * f), BF16)] + [jax.ShapeDtypeStruct((1, d), F32)] * 3,
        scratch_shapes=[pltpu.VMEM((ts + FFN_HALO, tc), F32)],
        compiler_params=pltpu.CompilerParams(
            dimension_semantics=("arbitrary",),
            vmem_limit_bytes=_vmem_limit(4 * ts * f * 4 + 2 * f * d * 2 + 2 * ts * 2 * f * 2 + 12 * ts * d * 4 + 6 * ts * tc * 4)),
    )(dug, dug, duv, duv, cw, w, x, dres, gain, s1p)


def _adamw_math(w, g, m, v):
    m = ADAM_B1 * m + (1.0 - ADAM_B1) * g
    v = ADAM_B2 * v + (1.0 - ADAM_B2) * (g * g)
    m_hat = m / (1.0 - ADAM_B1 ** ADAM_STEP)
    v_hat = v / (1.0 - ADAM_B2 ** ADAM_STEP)
    delta = -ADAM_LR * (m_hat / (jnp.sqrt(v_hat) + ADAM_EPS) + ADAM_WD * w)
    return delta, m, v


def _adam_rows(rows, cols):
    want = max(8, (2 * 1024 * 1024 // (cols * 4)) // 8 * 8)
    tr = min(rows, want)
    while rows % tr:
        tr -= 8
    return tr


def _adamw(w, m, v, g_parts, name):
    shape = w.shape
    nl = shape[0] if w.ndim == 3 else 1
    r, c = shape[-2], shape[-1]
    tr = _adam_rows(r, c)
    ng = len(g_parts)

    def body(*refs):
        w_ref, m_ref, v_ref = refs[0:3]
        g_refs = refs[3:3 + ng]
        g_out, d_out, m_out, v_out = refs[3 + ng:]
        g = g_refs[0][...]
        for gr in g_refs[1:]:
            g = g + gr[...]
        delta, mn, vn = _adamw_math(w_ref[...], g, m_ref[...], v_ref[...])
        g_out[...] = g
        d_out[...] = delta
        m_out[...] = mn
        v_out[...] = vn

    blk = pl.BlockSpec((None, tr, c), lambda l, i: (l, i, 0))
    outs = _hbm_call(
        body, name=name,
        grid=(nl, r // tr),
        in_specs=[blk] * (3 + ng),
        out_specs=[blk] * 4,
        out_shape=[jax.ShapeDtypeStruct((nl, r, c), F32)] * 4,
        compiler_params=pltpu.CompilerParams(dimension_semantics=("arbitrary", "arbitrary"),
                                             vmem_limit_bytes=_vmem_limit(2 * (7 + ng) * tr * max(c, 128) * 4 + (8 << 20))),
    )(*[a.reshape(nl, r, c) for a in (w, m, v, *g_parts)])
    return [o.reshape(shape) for o in outs]


def _adamw_many(ws, ms, vs, gs, name):
    n = len(ws)

    def body(*refs):
        for k in range(n):
            w_ref, m_ref, v_ref, g_ref = refs[k], refs[n + k], refs[2 * n + k], refs[3 * n + k]
            delta, mn, vn = _adamw_math(w_ref[...], g_ref[...], m_ref[...], v_ref[...])
            refs[4 * n + 3 * k][...] = delta
            refs[4 * n + 3 * k + 1][...] = mn
            refs[4 * n + 3 * k + 2][...] = vn

    specs = [_const_spec(a.shape) for a in ws]
    outs = _hbm_call(
        body, name=name,
        grid=(1,),
        in_specs=specs * 4,
        out_specs=[sp for sp in specs for _ in range(3)],
        out_shape=[jax.ShapeDtypeStruct(a.shape, F32) for a in ws for _ in range(3)],
        compiler_params=pltpu.CompilerParams(dimension_semantics=("arbitrary",),
                                             vmem_limit_bytes=_vmem_limit(20 * sum(a.size for a in ws) * 4 + (8 << 20))),
    )(*ws, *ms, *vs, *gs)
    return [tuple(outs[3 * k:3 * k + 3]) for k in range(n)]


def _modw_adamw(sct, dmod, w, m, v, name):
    nl, d, n = w.shape
    tr = _row_tile(d, 128)

    def body(sct_ref, dm_ref, w_ref, m_ref, v_ref, g_out, d_out, m_out, v_out):
        sc = sct_ref[...].astype(BF16).astype(F32)
        dm = dm_ref[...].astype(BF16).astype(F32)
        g = sc[:, 0:1] * dm[0:1, :]
        for b in range(1, N_DEV):
            g = g + sc[:, b:b + 1] * dm[b:b + 1, :]
        delta, mn, vn = _adamw_math(w_ref[...], g, m_ref[...], v_ref[...])
        g_out[...] = g
        d_out[...] = delta
        m_out[...] = mn
        v_out[...] = vn

    blk = pl.BlockSpec((None, tr, n), lambda l, i: (l, i, 0))
    return _hbm_call(
        body, name=name,
        grid=(nl, d // tr),
        in_specs=[pl.BlockSpec((tr, N_DEV), lambda l, i: (i, 0)), pl.BlockSpec((None, N_DEV, n), lambda l, i: (l, 0, 0)),
                  blk, blk, blk],
        out_specs=[blk] * 4,
        out_shape=[jax.ShapeDtypeStruct((nl, d, n), F32)] * 4,
        compiler_params=pltpu.CompilerParams(dimension_semantics=("arbitrary", "arbitrary"),
                                             vmem_limit_bytes=_vmem_limit(2 * 8 * tr * n * 4 + (8 << 20))),
    )(sct, dmod, w, m, v)


def _reduce4(recvs, name):
    nl = len(recvs)
    shape = recvs[0].shape[1:]
    c = shape[-1]
    r = math.prod(shape[:-1])
    tr = _adam_rows(r, c)
    nt = r // tr

    def body(*refs):
        o_ref = refs[nl]
        for l in range(nl):
            @pl.when(pl.program_id(0) == l)
            def _():
                acc = refs[l][0].astype(F32)
                for k in range(1, N_CHIPS):
                    acc = acc + refs[l][k].astype(F32)
                o_ref[...] = acc

    def in_map(l):
        return lambda ll, i: (0, jnp.where(ll < l, 0, jnp.where(ll > l, nt - 1, i)), 0)

    return _hbm_call(
        body, name=name,
        grid=(nl, nt),
        in_specs=[pl.BlockSpec((N_CHIPS, tr, c), in_map(l)) for l in range(nl)],
        out_specs=pl.BlockSpec((None, tr, c), lambda ll, i: (ll, i, 0)),
        out_shape=jax.ShapeDtypeStruct((nl, r, c), F32),
        compiler_params=pltpu.CompilerParams(dimension_semantics=("arbitrary", "arbitrary"),
                                             vmem_limit_bytes=_vmem_limit(2 * 8 * nl * tr * max(c, 128) * 4 + (8 << 20))),
    )(*[rv.reshape(N_CHIPS, r, c) for rv in recvs]).reshape((nl,) + shape)


def _my_place():
    return lax.axis_index("x"), lax.axis_index("y"), lax.axis_index("c")


def _chip_coords(j):
    return j // 2, j % 2


def _mod_forward(c, mod_w, mod_b4):
    nl, d, n = mod_w.shape
    kc = 256

    def body(c_ref, w_ref, b_ref, mod_ref, sc_ref, cbuf, stage, s1, r1, s2, r2):
        mx, my, mc = _my_place()
        me = 4 * mx + 2 * my + mc
        q = 2 * mx + my
        cv = c_ref[...]
        cbuf[me] = jnp.broadcast_to(cv * jax.nn.sigmoid(cv), (8, d))
        sends = []
        for t in range(N_DEV):
            tx, ty = _chip_coords(t // 2)
            cp = pltpu.make_async_remote_copy(src_ref=cbuf.at[me], dst_ref=cbuf.at[me], send_sem=s1.at[t], recv_sem=r1.at[me],
                                              device_id=(tx, ty, t % 2), device_id_type=MESH)

            @pl.when(t != me)
            def _():
                cp.start()

            sends.append((t, cp))
        for t in range(N_DEV):
            @pl.when(t != me)
            def _():
                pltpu.make_async_remote_copy(src_ref=cbuf.at[t], dst_ref=cbuf.at[t], send_sem=s1.at[t], recv_sem=r1.at[t],
                                             device_id=(mx, my, mc), device_id_type=MESH).wait_recv()
        for t, cp in sends:
            @pl.when(t != me)
            def _():
                cp.wait_send()

        row = lax.broadcasted_iota(jnp.int32, (8, d), 0)
        sc_all = jnp.zeros((8, d), F32)
        for t in range(N_DEV):
            sc_all = sc_all + jnp.where(row == t, cbuf[t], 0.0)
        sc_ref[...] = sc_all
        rown = lax.broadcasted_iota(jnp.int32, (8, n), 0)
        for l in range(nl):
            acc = jnp.zeros((8, n), F32)
            for k0 in range(0, d, kc):
                acc = acc + _dot(sc_all[:, k0:k0 + kc].astype(BF16), w_ref[l, k0:k0 + kc, :].astype(BF16))
            acc = acc + b_ref[l, q]
            for j in range(N_CHIPS):
                jx, jy = _chip_coords(j)
                bdest = 4 * jx + 2 * jy + mc
                rowv = jnp.sum(jnp.where(rown == bdest, acc, 0.0), axis=0, keepdims=True)
                stage[j, l] = jnp.broadcast_to(rowv, (8, n))
        sends2 = []
        for j in range(N_CHIPS):
            jx, jy = _chip_coords(j)
            cp = pltpu.make_async_remote_copy(src_ref=stage.at[j], dst_ref=mod_ref.at[:, q], send_sem=s2.at[j], recv_sem=r2.at[q],
                                              device_id=(jx, jy, mc), device_id_type=MESH)

            @pl.when(j != q)
            def _():
                cp.start()

            @pl.when(j == q)
            def _():
                for l in range(nl):
                    mod_ref[l, j] = stage[j, l]

            sends2.append((j, cp))
        for j in range(N_CHIPS):
            @pl.when(j != q)
            def _():
                pltpu.make_async_remote_copy(src_ref=stage.at[j], dst_ref=mod_ref.at[:, j], send_sem=s2.at[j], recv_sem=r2.at[j],
                                             device_id=(mx, my, mc), device_id_type=MESH).wait_recv()
        for j, cp in sends2:
            @pl.when(j != q)
            def _():
                cp.wait_send()

    vm = pl.BlockSpec(memory_space=pltpu.VMEM)
    return pl.pallas_call(
        body, name="mod_forward",
        in_specs=[vm, vm, vm],
        out_specs=[vm, vm],
        out_shape=[jax.ShapeDtypeStruct((nl, N_CHIPS, 8, n), F32), jax.ShapeDtypeStruct((8, d), F32)],
        scratch_shapes=[pltpu.VMEM((N_DEV, 8, d), F32), pltpu.VMEM((N_CHIPS, nl, 8, n), F32),
                        pltpu.SemaphoreType.DMA((N_DEV,)), pltpu.SemaphoreType.DMA((N_DEV,)),
                        pltpu.SemaphoreType.DMA((N_CHIPS,)), pltpu.SemaphoreType.DMA((N_CHIPS,))],
        compiler_params=pltpu.CompilerParams(vmem_limit_bytes=_vmem_limit(2 * nl * d * n * 4 + (8 << 20))),
    )(c, mod_w, mod_b4)


_HBM_SPEC = pl.BlockSpec(memory_space=pltpu.HBM)
_SEM_SPEC = pl.BlockSpec(memory_space=pltpu.SEMAPHORE)
_DATAFLOW = pltpu.SideEffectType.DATAFLOW_SIDE_EFFECTING


def _slot(ref, scatter, j):
    return ref.at[j] if scatter else ref


def _exchange_start(groups, scatter, after, name):
    flat = [a for g in groups for a in g]
    na = len(flat)
    ng = len(groups)
    sizes = [len(g) for g in groups]
    first = [sum(sizes[:g]) for g in range(ng)]
    where = [(g, k) for g in range(ng) for k in range(sizes[g])]
    mx, my, _ = _my_place()
    qo = 2 * mx + my
    lands = []
    for a in flat:
        own = lax.dynamic_index_in_dim(a, qo, 0, keepdims=False) if scatter else a
        lands.append(lax.dynamic_update_index_in_dim(lax.empty((N_CHIPS,) + own.shape, a.dtype), own, qo, 0))

    def body(*refs):
        ins, lnd = refs[:na], refs[na:2 * na]
        ssems, rsems = refs[2 * na + 1:2 * na + 1 + ng], refs[2 * na + 1 + ng:2 * na + 1 + 2 * ng]
        token = refs[-1]
        mx, my, mc = _my_place()
        q = 2 * mx + my
        for j in range(N_CHIPS):
            jx, jy = _chip_coords(j)
            for a in range(na):
                g, k = where[a]

                @pl.when(j != q)
                def _():
                    pltpu.make_async_remote_copy(src_ref=_slot(ins[a], scatter, j), dst_ref=lnd[a].at[q],
                                                 send_sem=ssems[g].at[k * N_CHIPS + j], recv_sem=rsems[g].at[k * N_CHIPS + q],
                                                 device_id=(jx, jy, mc), device_id_type=MESH).start()
        token[...] = jnp.zeros_like(token)

    sem_shapes = [pltpu.SemaphoreType.DMA((n * N_CHIPS,)) for n in sizes]
    outs = pl.pallas_call(
        body, name=name,
        in_specs=[_HBM_SPEC] * (2 * na) + [pl.BlockSpec(memory_space=pl.ANY)],
        out_specs=[_SEM_SPEC] * (2 * ng) + [_HBM_SPEC] * (2 * na) + [pl.BlockSpec(memory_space=pltpu.VMEM)],
        out_shape=sem_shapes + sem_shapes + [pltpu.HBM(a.shape, a.dtype) for a in flat + lands]
        + [jax.ShapeDtypeStruct((8, 128), F32)],
        input_output_aliases={i: 2 * ng + i for i in range(2 * na)},
        compiler_params=pltpu.CompilerParams(has_side_effects=_DATAFLOW),
    )(*[pltpu.with_memory_space_constraint(a, pltpu.HBM) for a in flat + lands], after)
    ssems, rsems = outs[:ng], outs[ng:2 * ng]
    src_thru, land_thru = outs[2 * ng:2 * ng + na], outs[2 * ng + na:2 * ng + 2 * na]
    states = [(src_thru[first[g]:first[g] + sizes[g]], land_thru[first[g]:first[g] + sizes[g]], ssems[g], rsems[g])
              for g in range(ng)]
    return states, outs[-1]


def _exchange_wait(state, scatter, after, name):
    srcs, lands, ssem, rsem = state
    na = len(srcs)

    def body(*refs):
        ins, lnd = refs[:na], refs[na:2 * na]
        ssem_ref, rsem_ref = refs[2 * na], refs[2 * na + 1]
        mx, my, mc = _my_place()
        q = 2 * mx + my
        for j in range(N_CHIPS):
            for a in range(na):
                @pl.when(j != q)
                def _():
                    cp = pltpu.make_async_remote_copy(src_ref=_slot(ins[a], scatter, j), dst_ref=lnd[a].at[j],
                                                      send_sem=ssem_ref.at[a * N_CHIPS + j], recv_sem=rsem_ref.at[a * N_CHIPS + j],
                                                      device_id=(mx, my, mc), device_id_type=MESH)
                    cp.wait_send()
                    cp.wait_recv()

    outs = pl.pallas_call(
        body, name=name,
        in_specs=[_HBM_SPEC] * (2 * na) + [_SEM_SPEC, _SEM_SPEC, pl.BlockSpec(memory_space=pl.ANY)],
        out_specs=[_HBM_SPEC] * (2 * na),
        out_shape=[pltpu.HBM(a.shape, a.dtype) for a in list(srcs) + list(lands)],
        input_output_aliases={i: i for i in range(2 * na)},
        compiler_params=pltpu.CompilerParams(has_side_effects=_DATAFLOW),
    )(*srcs, *lands, ssem, rsem, after)
    return outs[na:]


def _sibling_copy(src, dst, ssem, rsem, a):
    mx, my, mc = _my_place()
    return pltpu.make_async_remote_copy(src_ref=src, dst_ref=dst, send_sem=ssem.at[a], recv_sem=rsem.at[a],
                                        device_id=(mx, my, 1 - mc), device_id_type=MESH)


def _swap_start(arrs, after, name):
    na = len(arrs)
    lands = [lax.empty(a.shape, a.dtype) for a in arrs]

    def body(*refs):
        ins, lnd = refs[:na], refs[na:2 * na]
        ssem, rsem, token = refs[2 * na + 1], refs[2 * na + 2], refs[-1]
        for a in range(na):
            _sibling_copy(ins[a], lnd[a], ssem, rsem, a).start()
        token[...] = jnp.zeros_like(token)

    outs = pl.pallas_call(
        body, name=name,
        in_specs=[_HBM_SPEC] * (2 * na) + [pl.BlockSpec(memory_space=pl.ANY)],
        out_specs=[_SEM_SPEC] * 2 + [_HBM_SPEC] * (2 * na) + [pl.BlockSpec(memory_space=pltpu.VMEM)],
        out_shape=[pltpu.SemaphoreType.DMA((na,))] * 2 + [pltpu.HBM(a.shape, a.dtype) for a in list(arrs) + lands]
        + [jax.ShapeDtypeStruct((8, 128), F32)],
        input_output_aliases={i: 2 + i for i in range(2 * na)},
        compiler_params=pltpu.CompilerParams(has_side_effects=_DATAFLOW),
    )(*[pltpu.with_memory_space_constraint(a, pltpu.HBM) for a in list(arrs) + lands], after)
    return (outs[2:2 + na], outs[2 + na:2 + 2 * na], outs[0], outs[1]), outs[-1]


def _swap_wait(state, after, name):
    srcs, lands, ssem, rsem = state
    na = len(srcs)

    def body(*refs):
        ins, lnd = refs[:na], refs[na:2 * na]
        ssem_ref, rsem_ref = refs[2 * na], refs[2 * na + 1]
        for a in range(na):
            cp = _sibling_copy(ins[a], lnd[a], ssem_ref, rsem_ref, a)
            cp.wait_send()
            cp.wait_recv()

    outs = pl.pallas_call(
        body, name=name,
        in_specs=[_HBM_SPEC] * (2 * na) + [_SEM_SPEC, _SEM_SPEC, pl.BlockSpec(memory_space=pl.ANY)],
        out_specs=[_HBM_SPEC] * (2 * na),
        out_shape=[pltpu.HBM(a.shape, a.dtype) for a in list(srcs) + list(lands)],
        input_output_aliases={i: i for i in range(2 * na)},
        compiler_params=pltpu.CompilerParams(has_side_effects=_DATAFLOW),
    )(*srcs, *lands, ssem, rsem, after)
    return outs[:na], outs[na:]


def _allreduce_small(rows_all, rows_sum):
    ra, c = rows_all.shape
    r = rows_sum.shape[0]
    ch = r // N_DEV
    assert ch % 8 == 0 and ch * N_DEV == r

    def body(a_ref, s_ref, all_ref, sum_ref, rbuf, red, sa, rva, sb, rvb, sc, rvc):
        mx, my, mc = _my_place()
        me = 4 * mx + 2 * my + mc
        mine = pl.ds(pl.multiple_of(me * ch, 8), ch)
        all_ref[me] = a_ref[...]
        rbuf[me] = s_ref[mine, :]

        def dev(t):
            tx, ty = _chip_coords(t // 2)
            return (tx, ty, t % 2)

        def everyone_else(fn):
            for t in range(N_DEV):
                @pl.when(t != me)
                def _():
                    fn(t)

        def copy_a(t, slot):
            return pltpu.make_async_remote_copy(src_ref=a_ref, dst_ref=all_ref.at[slot], send_sem=sa.at[t], recv_sem=rva.at[slot],
                                                device_id=dev(t), device_id_type=MESH)

        def copy_b(t, slot):
            return pltpu.make_async_remote_copy(src_ref=s_ref.at[pl.ds(t * ch, ch), :], dst_ref=rbuf.at[slot], send_sem=sb.at[t],
                                                recv_sem=rvb.at[slot], device_id=dev(t), device_id_type=MESH)

        def copy_c(t, chunk_start, slot):
            return pltpu.make_async_remote_copy(src_ref=red, dst_ref=sum_ref.at[pl.ds(chunk_start, ch), :], send_sem=sc.at[t],
                                                recv_sem=rvc.at[slot], device_id=dev(t), device_id_type=MESH)

        everyone_else(lambda t: (copy_a(t, me).start(), copy_b(t, me).start()))
        everyone_else(lambda t: (copy_a(t, t).wait_recv(), copy_b(t, t).wait_recv()))
        everyone_else(lambda t: (copy_a(t, me).wait_send(), copy_b(t, me).wait_send()))
        acc = rbuf[0]
        for t in range(1, N_DEV):
            acc = acc + rbuf[t]
        red[...] = acc
        sum_ref[mine, :] = acc
        everyone_else(lambda t: copy_c(t, pl.multiple_of(me * ch, 8), me).start())
        everyone_else(lambda t: copy_c(t, t * ch, t).wait_recv())
        everyone_else(lambda t: copy_c(t, pl.multiple_of(me * ch, 8), me).wait_send())

    vm = pl.BlockSpec(memory_space=pltpu.VMEM)
    return pl.pallas_call(
        body, name="allreduce_small",
        in_specs=[vm, vm],
        out_specs=[vm, vm],
        out_shape=[jax.ShapeDtypeStruct((N_DEV, ra, c), F32), jax.ShapeDtypeStruct((r, c), F32)],
        scratch_shapes=[pltpu.VMEM((N_DEV, ch, c), F32), pltpu.VMEM((ch, c), F32)] + [pltpu.SemaphoreType.DMA((N_DEV,))] * 6,
        compiler_params=pltpu.CompilerParams(vmem_limit_bytes=_vmem_limit((3 * r + 2 * N_DEV * ra) * c * 4 + (4 << 20))),
    )(rows_all, rows_sum)


def _pack(arrs, row_multiple=8):
    rows, layout, at = [], [], 0
    for a in arrs:
        n = a.size
        nr = -(-n // (8 * SMALL_COLS)) * 8
        flat = a.reshape(-1)
        if nr * SMALL_COLS != n:
            flat = jnp.pad(flat, (0, nr * SMALL_COLS - n))
        rows.append(flat.reshape(nr, SMALL_COLS))
        layout.append((at, nr, a.shape))
        at += nr
    pad = -at % row_multiple
    if pad:
        rows.append(jnp.zeros((pad, SMALL_COLS), F32))
    return jnp.concatenate(rows, axis=0), layout


def _unpack(buf, layout):
    out = []
    for at, nr, shape in layout:
        n = math.prod(shape)
        out.append(buf[at:at + nr].reshape(-1)[:n].reshape(shape))
    return out


SMALL_NAMES = ("mod_b", "mix_pre_g", "mix_post_g", "sgu_norm_g", "sgu_norm_b", "sgu_w", "sgu_b", "conv_b", "conv_norm_g",
               "conv_norm_b", "pool_w", "pool_scale", "branch_g", "ffn_pre_g", "ffn_post_g", "ffn_conv_b")
SHARDED_SMALL = ("conv_w", "ffn_conv_w")
WEIGHT_ORDER = ("mod_w", "mod_b", "mix_pre_g", "mix_post_g", "w_in", "sgu_norm_g", "sgu_norm_b", "sgu_w", "sgu_b", "conv_w",
                "conv_b", "conv_norm_g", "conv_norm_b", "pool_w", "pool_scale", "branch_g", "w_out", "ffn_pre_g", "ffn_post_g",
                "ffn_up", "ffn_conv_w", "ffn_conv_b", "ffn_down")


def _block_diag(blocks):
    n, a, b = blocks.shape
    eye = jnp.eye(n, dtype=blocks.dtype)
    return (eye[:, None, :, None] * blocks[:, :, None, :]).reshape(n * a, n * b)


def _diag_blocks(mat, n):
    a = mat.shape[0] // n
    return jnp.stack([mat[g * a:(g + 1) * a, g * a:(g + 1) * a] for g in range(n)])


def _step(x, c, loss_target, w, m, v):
    nl = w["mod_w"].shape[0]
    s, d = x.shape[1], x.shape[2]
    heads = SGU_WIDTH // HEAD_DIM
    groups = len(POOL_WINDOWS)
    mx, my, _ = _my_place()
    q = 2 * mx + my
    x0 = x.reshape(s, d)
    tgt = loss_target.reshape(s, d)

    nmod = w["mod_w"].shape[2]
    kin = w["w_in"].shape[2]
    inw = kin * N_CHIPS
    f2 = w["ffn_up"].shape[2] * N_CHIPS
    f = f2 // 2

    def wgroups(l):
        return [[jnp.swapaxes(w["w_in"][l], 0, 1).astype(BF16), w["conv_w"][l], w["ffn_conv_w"][l]], [w["w_out"][l].astype(BF16)],
                [w["ffn_up"][l].astype(BF16)], [w["ffn_down"][l].astype(BF16)]]

    gstates = {}
    (gstates[0, 0], gstates[0, 1]), gtoken = _exchange_start(wgroups(0)[:2], False, c, "gather_start_in_0")
    mod4, sc_all = _mod_forward(c + gtoken[0:1, 0:1], w["mod_w"], w["mod_b"].reshape(nl, N_CHIPS, 1, nmod))
    mod = mod4[:, :, 0, :].reshape(nl, N_MOD, 1, d)

    tril = jnp.tril(jnp.ones((CHUNK, CHUNK), bool))
    bd = _block_diag(jnp.ones((heads, HEAD_DIM, HEAD_DIM), BF16))

    def mixer_params(l, conv_w):
        wm = jnp.where(tril[None], w["sgu_w"][l], 0.0)
        pw = _block_diag(w["pool_w"][l])
        return dict(
            bd=bd, ng=w["sgu_norm_g"][l][None], nb=w["sgu_norm_b"][l][None],
            wm=wm.astype(BF16), wmt=jnp.swapaxes(wm, 1, 2).astype(BF16),
            bias=jnp.repeat(w["sgu_b"][l].T, HEAD_DIM, axis=1),
            cw=conv_w, cb=w["conv_b"][l][None], cng=w["conv_norm_g"][l][None], cnb=w["conv_norm_b"][l][None],
            pw=pw.astype(BF16), pwt=pw.T.astype(BF16), ps=w["pool_scale"][l][None], bg=w["branch_g"][l][None])

    saved = []
    xl = x0
    arrived = {0: list(_exchange_wait(gstates[0, 0], False, mod4, "gather_wait_in_0"))}
    arrived[0] += list(_exchange_wait(gstates[0, 1], False, arrived[0][0], "gather_wait_out_0"))
    for l in range(nl):
        sh1, sc1, g1, sh2, sc2, g2 = [mod[l, k] for k in range(N_MOD)]
        gpre1, gpost1 = w["mix_pre_g"][l][None], w["mix_post_g"][l][None]
        gpre2, gpost2 = w["ffn_pre_g"][l][None], w["ffn_post_g"][l][None]
        fcb = w["ffn_conv_b"][l][None]
        sh1_after, bg_after, g1_after, sh2_after, fcb_after = sh1, w["branch_g"][l][None], g1, sh2, fcb
        g_win, g_cw, g_fcw = arrived[l][:3]
        if l == 0:
            (gstates[0, 2],), tok = _exchange_start(wgroups(0)[2:3], False, arrived[0][3], "gather_start_up_0")
            sh1_after = sh1 + tok[0:1, 0:1]
        w_in = g_win.reshape(inw, d)
        conv_w = jnp.transpose(g_cw, (1, 0, 2)).reshape(CONV_K, CONV_WIDTH)
        ffn_cw = jnp.transpose(g_fcw, (1, 0, 2)).reshape(FFN_CONV_K, f2)
        mp = mixer_params(l, conv_w)
        z, h1 = _norm_mod_matmul(xl, gpre1, 1.0 + sc1, sh1_after, w_in[None], f"mix_in_{l}", transposed=True)
        w_out = arrived[l][3].reshape(d, d)
        ycat, cbo, dpool = _mixer_fwd(z, dict(mp, bg=bg_after), f"mixer_fwd_{l}")
        (up,) = _exchange_wait(gstates[l, 2], False, ycat, f"gather_wait_up_{l}")
        if l == 0:
            (gstates[0, 3],), tok = _exchange_start(wgroups(0)[3:4], False, up, "gather_start_down_0")
            g1_after = g1 + tok[0:1, 0:1]
        o, x1 = _matmul_norm_resid(ycat, w_out, xl, g1_after, gpost1, f"mix_out_{l}")
        if l + 1 < nl:
            (gstates[l + 1, 0], gstates[l + 1, 1]), tok = _exchange_start(wgroups(l + 1)[0:2], False, x1,
                                                                        f"gather_start_in_{l + 1}")
            sh2_after = sh2 + tok[0:1, 0:1]
        p, h2 = _norm_mod_matmul(x1, gpre2, 1.0 + sc2, sh2_after, up, f"ffn_in_{l}")
        if l + 1 < nl:
            nxt = _exchange_wait(gstates[l + 1, 0], False, p, f"gather_wait_in_{l + 1}")
            nxt_out = _exchange_wait(gstates[l + 1, 1], False, nxt[0], f"gather_wait_out_{l + 1}")
            arrived[l + 1] = list(nxt) + list(nxt_out)
            (gstates[l + 1, 2], gstates[l + 1, 3]), tok = _exchange_start(wgroups(l + 1)[2:4], False, nxt_out[0],
                                                                        f"gather_start_up_{l + 1}")
            fcb_after = fcb + tok[0:1, 0:1]
        act = _ffn_act_fwd(p, ffn_cw, fcb_after, f"ffn_act_{l}")
        (g_down,) = _exchange_wait(gstates[l, 3], False, act, f"gather_wait_down_{l}")
        down = g_down.reshape(f, d)
        qo, x2 = _matmul_norm_resid(act, down, x1, g2, gpost2, f"ffn_out_{l}")
        saved.append(dict(x=xl, z=z, h1=h1, ycat=ycat, cbo=cbo, dpool=dpool, o=o, x1=x1, p=p, h2=h2, act=act, qo=qo, mp=mp, fcb=fcb,
                          w_in=w_in, w_out=w_out, up=up, down=down, ffn_cw=ffn_cw,
                          mods=(sh1, sc1, g1, sh2, sc2, g2), gains=(gpre1, gpost1, gpre2, gpost2)))
        xl = x2

    dx, loss_row = _loss_head(xl, tgt)

    small = {n: [None] * nl for n in SMALL_NAMES + SHARDED_SMALL}
    dmods = [None] * nl
    tn = f2 // N_CHIPS
    sstates = {}
    token = None
    for l in reversed(range(nl)):
        sv = saved[l]
        sh1, sc1, g1, sh2, sc2, g2 = sv["mods"]
        gpre1, gpost1, gpre2, gpost2 = sv["gains"]
        if token is not None:
            g2 = g2 + token[0:1, 0:1]
        dq, dact, dg2, dgpost2 = _resid_bwd_matmul(dx, sv["qo"], g2, gpost2, sv["down"], f"ffn_out_bwd_{l}")
        g_down = _wgrad(sv["act"], dq, (f, lambda j: 0), (d, lambda j: 0), jax.ShapeDtypeStruct((f, d), BF16),
                        (1, lambda j: (0, 0)), (f, d), f"wgrad_ffn_down_{l}")
        dug, duv, dfwg, dfwv, dfbg, dfbv = _ffn_act_bwd(sv["p"], dact, sv["ffn_cw"], sv["fcb"], f"ffn_act_bwd_{l}")
        dx1, dp, dsh2, dsc2, dgpre2 = _ffn_in_bwd(dug, duv, sv["ffn_cw"], sv["up"], sv["x1"], dx, gpre2, 1.0 + sc2,
                                                  f"ffn_in_bwd_{l}")
        g_up = _wgrad(sv["h2"], dp, (d, lambda j: 0), (tn, lambda j: j), jax.ShapeDtypeStruct((N_CHIPS, d, tn), BF16),
                      (N_CHIPS, lambda j: (j, 0, 0)), (None, d, tn), f"wgrad_ffn_up_{l}")
        (sstates[l, 0],), token = _exchange_start([[g_down.reshape(N_CHIPS, f // N_CHIPS, d), g_up]], True, g_up,
                                                  f"scatter_start_ffn_{l}")
        do, dycat, dg1, dgpost1 = _resid_bwd_matmul(dx1, sv["o"], g1 + token[0:1, 0:1], gpost1, sv["w_out"],
                                                    f"mix_out_bwd_{l}")
        g_out = _wgrad(sv["ycat"], do, (d, lambda j: 0), (d, lambda j: 0), jax.ShapeDtypeStruct((d, d), BF16),
                       (1, lambda j: (0, 0)), (d, d), f"wgrad_w_out_{l}")
        (sstates[l, 1],), token = _exchange_start([[g_out.reshape(N_CHIPS, d // N_CHIPS, d)]], True, g_out,
                                                  f"scatter_start_out_{l}")
        mp_after = dict(sv["mp"], bg=sv["mp"]["bg"] + token[0:1, 0:1])
        (dza, dcb, dd, dbg, dwm, dbias, dng, dnb, dcng, dcnb, dps, dpw) = _mixer_bwd_a(sv["z"], sv["cbo"], sv["dpool"], dycat, mp_after, f"mixer_bwd_a_{l}")
        dx, dz, dsh1, dsc1, dgpre1, dcw, dcbias = _mixer_bwd_b(
            sv["z"], dza, dcb, dd, sv["x"], dx1, gpre1, 1.0 + sc1, sv["mp"]["cw"], sv["w_in"], f"mixer_bwd_b_{l}")
        g_in = _wgrad(dz, sv["h1"], (inw, lambda j: 0), (d, lambda j: 0), jax.ShapeDtypeStruct((inw, d), BF16),
                      (1, lambda j: (0, 0)), (inw, d), f"wgrad_w_in_{l}")
        g_in_parts = g_in.reshape(N_CHIPS, kin, d)
        if l > 0:
            (sstates[l, 2],), token = _exchange_start([[g_in_parts]], True, g_in_parts, f"scatter_start_in_{l}")

        dmods[l] = jnp.concatenate([dsh1, dsc1, dg1, dsh2, dsc2, dg2], axis=0)
        small["mix_pre_g"][l], small["mix_post_g"][l] = dgpre1[0], dgpost1[0]
        small["ffn_pre_g"][l], small["ffn_post_g"][l] = dgpre2[0], dgpost2[0]
        small["sgu_norm_g"][l], small["sgu_norm_b"][l] = dng[0], dnb[0]
        small["sgu_w"][l] = jnp.where(tril[None], dwm, 0.0)
        small["sgu_b"][l] = dbias.reshape(CHUNK, heads, HEAD_DIM).sum(-1).T
        small["conv_b"][l], small["conv_norm_g"][l], small["conv_norm_b"][l] = dcbias[0], dcng[0], dcnb[0]
        small["pool_w"][l], small["pool_scale"][l], small["branch_g"][l] = _diag_blocks(dpw, groups), dps[0], dbg[0]
        small["ffn_conv_b"][l] = jnp.concatenate([dfbg[0], dfbv[0]])
        small["conv_w"][l] = dcw
        small["ffn_conv_w"][l] = jnp.concatenate([dfwg, dfwv], axis=1)

    names = [n for n in SMALL_NAMES if n != "mod_b"] + list(SHARDED_SMALL)
    dmod_rows, _ = _pack([jnp.stack(dmods)])
    packed, layout = _pack([jnp.stack(dmods), loss_row] + [jnp.stack(small[n]) for n in names], 8 * N_DEV)
    gathered, summed = _allreduce_small(dmod_rows, packed)
    (sstates[0, 2],), token = _exchange_start([[g_in_parts]], True, summed, "scatter_start_in_0")
    parts = _unpack(summed, layout)
    loss = parts[1][0, 0]
    gsmall = dict(zip(names, parts[2:]))
    gsmall["mod_b"] = parts[0].reshape(nl, N_MOD * d)
    dmod_all = gathered[:, :nl * N_MOD].reshape(N_DEV, nl, N_MOD * d)
    dmod_mine = jnp.transpose(lax.dynamic_slice_in_dim(dmod_all, q * nmod, nmod, axis=2), (1, 0, 2))

    grads, deltas, new_m, new_v = {}, {}, {}, {}

    def put(name, res):
        grads[name], deltas[name], new_m[name], new_v[name] = res

    recv = dict(w_in=[None] * nl, w_out=[None] * nl, ffn_up=[None] * nl, ffn_down=[None] * nl)
    done = token
    for l in reversed(range(nl)):
        recv["ffn_down"][l], recv["ffn_up"][l] = _exchange_wait(sstates[l, 0], True, done, f"scatter_wait_ffn_{l}")
        (recv["w_out"][l],) = _exchange_wait(sstates[l, 1], True, recv["ffn_up"][l], f"scatter_wait_out_{l}")
        done = recv["w_out"][l]
    big = ("w_out", "ffn_up", "ffn_down", "w_in")
    mine = {n: _reduce4(recv[n], f"reduce4_{n}") for n in big[:3]}
    swap_a, token = _swap_start([mine[n] for n in big[:3]], mine["ffn_down"], "swap_start_a")

    put("mod_w", _modw_adamw(sc_all.T + token[0:1, 0:1], dmod_mine, w["mod_w"], m["mod_w"], v["mod_w"], "adamw_mod_w"))
    done = grads["mod_w"]
    for l in reversed(range(nl)):
        (recv["w_in"][l],) = _exchange_wait(sstates[l, 2], True, done, f"scatter_wait_in_{l}")
        done = recv["w_in"][l]
    mine["w_in"] = _reduce4(recv["w_in"], "reduce4_w_in")
    swap_b, token = _swap_start([mine["w_in"]], mine["w_in"], "swap_start_b")

    gsmall["conv_w"] = lax.dynamic_slice_in_dim(gsmall["conv_w"], q * (CONV_WIDTH // N_CHIPS), CONV_WIDTH // N_CHIPS, axis=2)
    gsmall["ffn_conv_w"] = lax.dynamic_slice_in_dim(gsmall["ffn_conv_w"], q * (f2 // N_CHIPS), f2 // N_CHIPS, axis=2)
    snames = SMALL_NAMES + SHARDED_SMALL
    res = _adamw_many([w[n] for n in snames], [m[n] for n in snames], [v[n] for n in snames], [gsmall[n] for n in snames],
                      "adamw_small")
    for n, (d_, m_, v_) in zip(snames, res):
        put(n, (gsmall[n], d_, m_, v_))

    sent, theirs = _swap_wait(swap_a, deltas["mod_b"], "swap_wait_a")
    sent_b, theirs_b = _swap_wait(swap_b, sent[0], "swap_wait_b")
    for n, a, b in zip(big, list(sent) + list(sent_b), list(theirs) + list(theirs_b)):
        if n == "w_in":
            res = _adamw(*[jnp.swapaxes(t[n], 1, 2) for t in (w, m, v)], [a, b], f"adamw_{n}")
            put(n, [jnp.swapaxes(r_, 1, 2) for r_ in res])
        else:
            put(n, _adamw(w[n], m[n], v[n], [a, b], f"adamw_{n}"))

    return (loss, dx.reshape(1, s, d), *[grads[n] for n in WEIGHT_ORDER], *[deltas[n] for n in WEIGHT_ORDER],
            *[new_m[n] for n in WEIGHT_ORDER], *[new_v[n] for n in WEIGHT_ORDER])


def kernel(x, c, mod_w, mod_b, mix_pre_g, mix_post_g, w_in, sgu_norm_g, sgu_norm_b, sgu_w, sgu_b, conv_w, conv_b, conv_norm_g, conv_norm_b, pool_w, pool_scale, branch_g, w_out, ffn_pre_g, ffn_post_g, ffn_up, ffn_conv_w, ffn_conv_b, ffn_down, loss_target, m_mod_w, m_mod_b, m_mix_pre_g, m_mix_post_g, m_w_in, m_sgu_norm_g, m_sgu_norm_b, m_sgu_w, m_sgu_b, m_conv_w, m_conv_b, m_conv_norm_g, m_conv_norm_b, m_pool_w, m_pool_scale, m_branch_g, m_w_out, m_ffn_pre_g, m_ffn_post_g, m_ffn_up, m_ffn_conv_w, m_ffn_conv_b, m_ffn_down, v_mod_w, v_mod_b, v_mix_pre_g, v_mix_post_g, v_w_in, v_sgu_norm_g, v_sgu_norm_b, v_sgu_w, v_sgu_b, v_conv_w, v_conv_b, v_conv_norm_g, v_conv_norm_b, v_pool_w, v_pool_scale, v_branch_g, v_w_out, v_ffn_pre_g, v_ffn_post_g, v_ffn_up, v_ffn_conv_w, v_ffn_conv_b, v_ffn_down):
    w = dict(mod_w=mod_w, mod_b=mod_b, mix_pre_g=mix_pre_g, mix_post_g=mix_post_g, w_in=w_in, sgu_norm_g=sgu_norm_g,
             sgu_norm_b=sgu_norm_b, sgu_w=sgu_w, sgu_b=sgu_b, conv_w=conv_w, conv_b=conv_b, conv_norm_g=conv_norm_g,
             conv_norm_b=conv_norm_b, pool_w=pool_w, pool_scale=pool_scale, branch_g=branch_g, w_out=w_out,
             ffn_pre_g=ffn_pre_g, ffn_post_g=ffn_post_g, ffn_up=ffn_up, ffn_conv_w=ffn_conv_w, ffn_conv_b=ffn_conv_b,
             ffn_down=ffn_down)
    m = dict(mod_w=m_mod_w, mod_b=m_mod_b, mix_pre_g=m_mix_pre_g, mix_post_g=m_mix_post_g, w_in=m_w_in,
             sgu_norm_g=m_sgu_norm_g, sgu_norm_b=m_sgu_norm_b, sgu_w=m_sgu_w, sgu_b=m_sgu_b, conv_w=m_conv_w,
             conv_b=m_conv_b, conv_norm_g=m_conv_norm_g, conv_norm_b=m_conv_norm_b, pool_w=m_pool_w,
             pool_scale=m_pool_scale, branch_g=m_branch_g, w_out=m_w_out, ffn_pre_g=m_ffn_pre_g, ffn_post_g=m_ffn_post_g,
             ffn_up=m_ffn_up, ffn_conv_w=m_ffn_conv_w, ffn_conv_b=m_ffn_conv_b, ffn_down=m_ffn_down)
    v = dict(mod_w=v_mod_w, mod_b=v_mod_b, mix_pre_g=v_mix_pre_g, mix_post_g=v_mix_post_g, w_in=v_w_in,
             sgu_norm_g=v_sgu_norm_g, sgu_norm_b=v_sgu_norm_b, sgu_w=v_sgu_w, sgu_b=v_sgu_b, conv_w=v_conv_w,
             conv_b=v_conv_b, conv_norm_g=v_conv_norm_g, conv_norm_b=v_conv_norm_b, pool_w=v_pool_w,
             pool_scale=v_pool_scale, branch_g=v_branch_g, w_out=v_w_out, ffn_pre_g=v_ffn_pre_g, ffn_post_g=v_ffn_post_g,
             ffn_up=v_ffn_up, ffn_conv_w=v_ffn_conv_w, ffn_conv_b=v_ffn_conv_b, ffn_down=v_ffn_down)
    return _step(x, c, loss_target, w, m, v)
```

```python
import functools
import math

import jax
import jax.numpy as jnp
from jax import lax
from jax.experimental import pallas as pl
from jax.experimental.pallas import tpu as pltpu

F32 = jnp.float32
BF16 = jnp.bfloat16
MESH = pl.DeviceIdType.MESH

EPS = 1e-6
HEAD_DIM = 64
CHUNK = 128
SGU_WIDTH = 384
CONV_WIDTH = 384
POOL_WIDTH = 256
POOL_WINDOWS = (2, 4, 8, 16)
CONV_K = 31
FFN_CONV_K = 3
N_MOD = 6
N_CHIPS = 4
N_DEV = 8

ADAM_LR = 0.001
ADAM_B1 = 0.9
ADAM_B2 = 0.999
ADAM_EPS = 1e-08
ADAM_WD = 0.01
ADAM_STEP = 10

MIX_HALO = 32
FFN_HALO = 8
FFN_ROWS = 16
FFN_UNROLL = 8
LANES = 128
CONV_ROWS = 32
SMALL_COLS = 1024
VMEM_BYTES_V7X = 64 * 1024 * 1024


def _vmem_limit(estimate_bytes):
    return int(min(max(estimate_bytes, 16 * 1024 * 1024), VMEM_BYTES_V7X - 8 * 1024 * 1024))


def _row_tile(s, want):
    return want if s % want == 0 else math.gcd(s, want)


def _rsum(v):
    return jnp.sum(v, axis=0, keepdims=True)


def _rmean(v):
    return jnp.mean(v, axis=-1, keepdims=True)


def _gelu(v):
    k = math.sqrt(2.0 / math.pi)
    return 0.5 * v * (1.0 + jnp.tanh(k * (v + 0.044715 * v * v * v)))


def _gelu_grad(v):
    k = math.sqrt(2.0 / math.pi)
    t = jnp.tanh(k * (v + 0.044715 * v * v * v))
    return 0.5 * (1.0 + t) + 0.5 * v * (1.0 - t * t) * (k * (1.0 + 3.0 * 0.044715 * v * v))


def _dot(a, b):
    return jnp.dot(a, b, preferred_element_type=F32)


def _dot_nt(a, b):
    return lax.dot_general(a, b, (((1,), (1,)), ((), ())), preferred_element_type=F32)


def _dot_tn(a, b):
    return lax.dot_general(a, b, (((0,), (0,)), ((), ())), preferred_element_type=F32)


def _group_mean(v, bd):
    hi = v.astype(BF16)
    lo = (v - hi.astype(F32)).astype(BF16)
    return (_dot(hi, bd) + _dot(lo, bd)) * (1.0 / HEAD_DIM)


def _const_spec(shape):
    nd = len(shape)
    return pl.BlockSpec(shape, lambda *_: (0,) * nd)


def _hbm_call(body, **kw):
    call = pl.pallas_call(body, **kw)
    return lambda *args: call(*[pltpu.with_memory_space_constraint(a, pltpu.HBM) for a in args])


def _norm_mod_matmul(x, gain, s1p, shift, w, name, transposed=False):
    s, d = x.shape
    nb = w.shape[0]
    tn = w.shape[1] if transposed else w.shape[2]
    ts = _row_tile(s, 512 if nb * tn <= 2048 else 256)

    def body(x_ref, g_ref, s_ref, b_ref, w_ref, z_ref, h_ref):
        xv = x_ref[...]
        r = lax.rsqrt(_rmean(xv * xv) + EPS)
        h = ((xv * r) * g_ref[...] * s_ref[...] + b_ref[...]).astype(BF16)
        h_ref[...] = h
        for j in range(nb):
            z_ref[:, j * tn:(j + 1) * tn] = _dot_nt(h, w_ref[j]) if transposed else _dot(h, w_ref[j])

    vec = pl.BlockSpec((1, d), lambda i: (0, 0))
    return _hbm_call(
        body, name=name,
        grid=(s // ts,),
        in_specs=[pl.BlockSpec((ts, d), lambda i: (i, 0)), vec, vec, vec,
                  pl.BlockSpec(w.shape, lambda i: (0, 0, 0), pipeline_mode=pl.Buffered(1))],
        out_specs=[pl.BlockSpec((ts, nb * tn), lambda i: (i, 0)), pl.BlockSpec((ts, d), lambda i: (i, 0))],
        out_shape=[jax.ShapeDtypeStruct((s, nb * tn), F32), jax.ShapeDtypeStruct((s, d), BF16)],
        compiler_params=pltpu.CompilerParams(
            dimension_semantics=("arbitrary",),
            vmem_limit_bytes=_vmem_limit(2 * (ts * d * 4 + ts * nb * tn * 4 + ts * d * 2) + nb * d * tn * 2 + 4 * ts * d * 4)),
    )(x, gain, s1p, shift, w)


def _matmul_norm_resid(a, w, xres, gate, gpost, name):
    s, k = a.shape
    d = w.shape[1]
    ts = _row_tile(s, 512)

    def body(a_ref, w_ref, x_ref, gate_ref, gp_ref, o_ref, xn_ref):
        o = _dot(a_ref[...], w_ref[...])
        o_ref[...] = o
        r = lax.rsqrt(_rmean(o * o) + EPS)
        xn_ref[...] = x_ref[...] + gate_ref[...] * ((o * r) * gp_ref[...])

    vec = pl.BlockSpec((1, d), lambda i: (0, 0))
    row = pl.BlockSpec((ts, d), lambda i: (i, 0))
    return _hbm_call(
        body, name=name,
        grid=(s // ts,),
        in_specs=[pl.BlockSpec((ts, k), lambda i: (i, 0)),
                  pl.BlockSpec((k, d), lambda i: (0, 0), pipeline_mode=pl.Buffered(1)), row, vec, vec],
        out_specs=[row, row],
        out_shape=[jax.ShapeDtypeStruct((s, d), F32)] * 2,
        compiler_params=pltpu.CompilerParams(
            dimension_semantics=("arbitrary",),
            vmem_limit_bytes=_vmem_limit(2 * (ts * k * 2 + 3 * ts * d * 4) + k * d * 2 + 4 * ts * d * 4)),
    )(a, w, xres, gate, gpost)


def _wgrad(a, b, acols, bcols, out_struct, out_index, out_block, name):
    s = a.shape[0]
    ts = _row_tile(s, 1024)
    aw, afn = acols
    bw, bfn = bcols
    nj = out_index[0]
    oidx = out_index[1]

    def body(a_ref, b_ref, o_ref, acc):
        i = pl.program_id(1)

        @pl.when(i == 0)
        def _():
            acc[...] = jnp.zeros_like(acc)

        acc[...] += _dot_tn(a_ref[...], b_ref[...])

        @pl.when(i == pl.num_programs(1) - 1)
        def _():
            o_ref[...] = acc[...].astype(o_ref.dtype)

    return _hbm_call(
        body, name=name,
        grid=(nj, s // ts),
        in_specs=[pl.BlockSpec((ts, aw), lambda j, i: (i, afn(j))), pl.BlockSpec((ts, bw), lambda j, i: (i, bfn(j)))],
        out_specs=pl.BlockSpec(out_block, lambda j, i: oidx(j)),
        out_shape=out_struct,
        scratch_shapes=[pltpu.VMEM((aw, bw), F32)],
        compiler_params=pltpu.CompilerParams(
            dimension_semantics=("arbitrary", "arbitrary"),
            vmem_limit_bytes=_vmem_limit(2 * (ts * aw * 2 + ts * bw * 2) + 3 * aw * bw * 4 + ts * aw * 4)),
    )(a, b)


def _loss_head(xo, tgt):
    s, d = xo.shape
    ts = _row_tile(s, 512)

    def body(x_ref, t_ref, dx_ref, l_ref, acc):
        i = pl.program_id(0)

        @pl.when(i == 0)
        def _():
            acc[...] = jnp.zeros_like(acc)

        e = x_ref[...] - t_ref[...]
        dx_ref[...] = e * (1.0 / d)
        acc[...] += _rsum(e * e)

        @pl.when(i == pl.num_programs(0) - 1)
        def _():
            tot = jnp.sum(acc[...], axis=-1, keepdims=True) * (0.5 / d)
            l_ref[...] = jnp.broadcast_to(tot, l_ref.shape)

    row = pl.BlockSpec((ts, d), lambda i: (i, 0))
    return _hbm_call(
        body, name="loss_head",
        grid=(s // ts,),
        in_specs=[row, row],
        out_specs=[row, pl.BlockSpec((1, SMALL_COLS), lambda i: (0, 0))],
        out_shape=[jax.ShapeDtypeStruct((s, d), F32), jax.ShapeDtypeStruct((1, SMALL_COLS), F32)],
        scratch_shapes=[pltpu.VMEM((1, d), F32)],
        compiler_params=pltpu.CompilerParams(dimension_semantics=("arbitrary",)),
    )(xo, tgt)


def _resid_bwd_matmul(dxn, o, gate, gpost, w, name):
    s, d = dxn.shape
    k = w.shape[0]
    ts = _row_tile(s, 512)

    def body(dx_ref, o_ref, gate_ref, gp_ref, w_ref, do_ref, da_ref, dgate_ref, dgp_ref):
        i = pl.program_id(0)

        @pl.when(i == 0)
        def _():
            dgate_ref[...] = jnp.zeros_like(dgate_ref)
            dgp_ref[...] = jnp.zeros_like(dgp_ref)

        dx = dx_ref[...]
        o = o_ref[...]
        r = lax.rsqrt(_rmean(o * o) + EPS)
        on = o * r
        dgate_ref[...] += _rsum(dx * (on * gp_ref[...]))
        don = dx * gate_ref[...]
        dgp_ref[...] += _rsum(don * on)
        t = don * gp_ref[...]
        do = (r * (t - on * _rmean(t * on))).astype(BF16)
        do_ref[...] = do
        da_ref[...] = _dot_nt(do, w_ref[...])

    vec = pl.BlockSpec((1, d), lambda i: (0, 0))
    row = pl.BlockSpec((ts, d), lambda i: (i, 0))
    return _hbm_call(
        body, name=name,
        grid=(s // ts,),
        in_specs=[row, row, vec, vec, pl.BlockSpec((k, d), lambda i: (0, 0), pipeline_mode=pl.Buffered(1))],
        out_specs=[row, pl.BlockSpec((ts, k), lambda i: (i, 0)), vec, vec],
        out_shape=[jax.ShapeDtypeStruct((s, d), BF16), jax.ShapeDtypeStruct((s, k), F32),
                   jax.ShapeDtypeStruct((1, d), F32), jax.ShapeDtypeStruct((1, d), F32)],
        compiler_params=pltpu.CompilerParams(
            dimension_semantics=("arbitrary",),
            vmem_limit_bytes=_vmem_limit(2 * (2 * ts * d * 4 + ts * d * 2 + ts * k * 4) + d * k * 2 + 6 * ts * d * 4)),
    )(dxn, o, gate, gpost, w)


def _norm_mod_bwd(dh, xv, gain, s1p, dres):
    r = lax.rsqrt(_rmean(xv * xv) + EPS)
    xn = xv * r
    dshift = _rsum(dh)
    t = dh * xn
    dscale = _rsum(t * gain)
    dgain = _rsum(t * s1p)
    dxn = dh * (gain * s1p)
    dx = r * (dxn - xn * _rmean(dxn * xn)) + dres
    return dx, dshift, dscale, dgain


def _lane_lt(shape, bound):
    return lax.broadcasted_iota(jnp.int32, shape, 1) < bound


def _sgu_forward(z_ref, bd_ref, ng_ref, nb_ref, wm_ref, bias_ref, ts, ya_s, f_s):
    u = _gelu(z_ref[:, 0:SGU_WIDTH])
    v = _gelu(z_ref[:, SGU_WIDTH:2 * SGU_WIDTH])
    bd = bd_ref[...]
    vc = v - _group_mean(v, bd)
    rstd = lax.rsqrt(_group_mean(vc * vc, bd) + EPS)
    vhat = vc * rstd
    vn = (vhat * ng_ref[...] + nb_ref[...]).astype(BF16)
    left = _lane_lt((CHUNK, CHUNK), HEAD_DIM)
    for n in range(ts // CHUNK):
        rows = slice(n * CHUNK, (n + 1) * CHUNK)
        for p in range(SGU_WIDTH // CHUNK):
            cols = slice(p * CHUNK, (p + 1) * CHUNK)
            blk = vn[rows, cols]
            f = jnp.where(left, _dot(wm_ref[2 * p], blk), _dot(wm_ref[2 * p + 1], blk)) + bias_ref[:, cols]
            if f_s is not None:
                f_s[rows, cols] = f
            ya_s[rows, cols] = u[rows, cols] * f
    return u, vhat, rstd, vn


def _shifted_copies(ext, ext8, ts):
    for b in range(1, 8):
        ext8[b - 1] = ext[pl.ds(b, ts + MIX_HALO - 8), :]


def _rows_at(ext, ext8, start, nrows):
    b = start % 8
    return ext[pl.ds(start, nrows), :] if b == 0 else ext8[b - 1, pl.ds(start - b, nrows), :]


def _conv31_forward(z_ref, zh_ref, first, cw_ref, cb_ref, ts, ext_b, ext8, cbs):
    a = z_ref[:, 2 * SGU_WIDTH:2 * SGU_WIDTH + CONV_WIDTH]
    g = z_ref[:, 2 * SGU_WIDTH + CONV_WIDTH:2 * SGU_WIDTH + 2 * CONV_WIDTH]
    ah = zh_ref[:, 2 * SGU_WIDTH:2 * SGU_WIDTH + CONV_WIDTH]
    gh = zh_ref[:, 2 * SGU_WIDTH + CONV_WIDTH:2 * SGU_WIDTH + 2 * CONV_WIDTH]
    ext_b[pl.ds(0, MIX_HALO), :] = jnp.where(first, 0.0, ah * jax.nn.sigmoid(gh))
    ext_b[pl.ds(MIX_HALO, ts), :] = a * jax.nn.sigmoid(g)
    _shifted_copies(ext_b, ext8, ts)
    for r in range(ts // CONV_ROWS):
        acc = jnp.broadcast_to(cb_ref[...], (CONV_ROWS, CONV_WIDTH))
        for k in range(CONV_K):
            acc = acc + cw_ref[k:k + 1, :] * _rows_at(ext_b, ext8, MIX_HALO - (CONV_K - 1) + k + r * CONV_ROWS, CONV_ROWS)
        cbs[pl.ds(r * CONV_ROWS, CONV_ROWS), :] = acc


def _pool_counts(i, ts):
    pos1 = (i * ts + 1 + lax.broadcasted_iota(jnp.int32, (ts, POOL_WIDTH), 0)).astype(F32)
    lane = lax.broadcasted_iota(jnp.int32, (ts, POOL_WIDTH), 1)
    gdim = POOL_WIDTH // len(POOL_WINDOWS)
    win = jnp.where(lane < gdim, float(POOL_WINDOWS[0]),
                    jnp.where(lane < 2 * gdim, float(POOL_WINDOWS[1]),
                              jnp.where(lane < 3 * gdim, float(POOL_WINDOWS[2]), float(POOL_WINDOWS[3]))))
    return jnp.minimum(pos1, win)


def _window_sums(ext, base, ts, sign):
    lane = lax.broadcasted_iota(jnp.int32, (ts, POOL_WIDTH), 1)
    gdim = POOL_WIDTH // len(POOL_WINDOWS)
    run = jnp.zeros((ts, POOL_WIDTH), F32)
    out = jnp.zeros((ts, POOL_WIDTH), F32)
    for m in range(POOL_WINDOWS[-1]):
        run = run + ext[pl.ds(base + sign * m, ts), :]
        for gi, win in enumerate(POOL_WINDOWS):
            if m == win - 1:
                out = jnp.where((lane >= gi * gdim) & (lane < (gi + 1) * gdim), run, out)
    return out


def _pool_forward(z_ref, zh_ref, first, i, ts, ext_c):
    c0 = 2 * SGU_WIDTH + 2 * CONV_WIDTH
    zc = z_ref[:, c0:c0 + POOL_WIDTH]
    ext_c[pl.ds(0, MIX_HALO), :] = jnp.where(first, 0.0, zh_ref[:, c0:c0 + POOL_WIDTH])
    ext_c[pl.ds(MIX_HALO, ts), :] = zc
    sums = _window_sums(ext_c, MIX_HALO, ts, -1)
    return sums / _pool_counts(i, ts) - zc


def _layer_norm_rows(v):
    mu = _rmean(v)
    vc = v - mu
    rstd = lax.rsqrt(_rmean(vc * vc) + EPS)
    return vc * rstd, rstd


def _mixer_specs(s, ts, width):
    nbh = ts // MIX_HALO
    tile = pl.BlockSpec((ts, width), lambda i: (i, 0))
    prev = pl.BlockSpec((MIX_HALO, width), lambda i: (jnp.maximum(i * nbh - 1, 0), 0))
    nxt = pl.BlockSpec((MIX_HALO, width), lambda i: (jnp.minimum((i + 1) * nbh, s // MIX_HALO - 1), 0))
    return tile, prev, nxt


def _mixer_fwd(z, mp, name):
    s, inw = z.shape
    d = SGU_WIDTH + CONV_WIDTH + POOL_WIDTH
    ts = _row_tile(s, 256)

    def body(z_ref, zh_ref, bd_ref, ng_ref, nb_ref, wm_ref, bias_ref, cw_ref, cb_ref, cng_ref, cnb_ref,
             pw_ref, ps_ref, bg_ref, y_ref, cbs, dpool_ref, ya_s, ext_b, ext_c, ext8):
        i = pl.program_id(0)
        first = i == 0
        _sgu_forward(z_ref, bd_ref, ng_ref, nb_ref, wm_ref, bias_ref, ts, ya_s, None)
        ya = ya_s[...]
        ra = lax.rsqrt(_rmean(ya * ya) + EPS)
        y_ref[:, 0:SGU_WIDTH] = ((ya * ra) * bg_ref[:, 0:SGU_WIDTH]).astype(BF16)

        _conv31_forward(z_ref, zh_ref, first, cw_ref, cb_ref, ts, ext_b, ext8, cbs)
        chat, _ = _layer_norm_rows(cbs[...])
        lin = chat * cng_ref[...] + cnb_ref[...]
        yb = lin * jax.nn.sigmoid(lin)
        rb = lax.rsqrt(_rmean(yb * yb) + EPS)
        y_ref[:, SGU_WIDTH:SGU_WIDTH + CONV_WIDTH] = ((yb * rb) * bg_ref[:, SGU_WIDTH:SGU_WIDTH + CONV_WIDTH]).astype(BF16)

        dpool = _pool_forward(z_ref, zh_ref, first, i, ts, ext_c)
        dpool_ref[...] = dpool
        yc = _dot(dpool.astype(BF16), pw_ref[...]) * ps_ref[...]
        rc = lax.rsqrt(_rmean(yc * yc) + EPS)
        y_ref[:, SGU_WIDTH + CONV_WIDTH:d] = ((yc * rc) * bg_ref[:, SGU_WIDTH + CONV_WIDTH:d]).astype(BF16)

    tile, prev, _ = _mixer_specs(s, ts, inw)
    consts = [mp["bd"], mp["ng"], mp["nb"], mp["wm"], mp["bias"], mp["cw"], mp["cb"], mp["cng"], mp["cnb"],
              mp["pw"], mp["ps"], mp["bg"]]
    return _hbm_call(
        body, name=name,
        grid=(s // ts,),
        in_specs=[tile, prev] + [_const_spec(c.shape) for c in consts],
        out_specs=[pl.BlockSpec((ts, d), lambda i: (i, 0)), pl.BlockSpec((ts, CONV_WIDTH), lambda i: (i, 0)),
                   pl.BlockSpec((ts, POOL_WIDTH), lambda i: (i, 0))],
        out_shape=[jax.ShapeDtypeStruct((s, d), BF16), jax.ShapeDtypeStruct((s, CONV_WIDTH), F32),
                   jax.ShapeDtypeStruct((s, POOL_WIDTH), F32)],
        scratch_shapes=[pltpu.VMEM((ts, SGU_WIDTH), F32), pltpu.VMEM((ts + MIX_HALO, CONV_WIDTH), F32),
                        pltpu.VMEM((ts + MIX_HALO, POOL_WIDTH), F32), pltpu.VMEM((7, ts + MIX_HALO - 8, CONV_WIDTH), F32)],
        compiler_params=pltpu.CompilerParams(dimension_semantics=("arbitrary",),
                                             vmem_limit_bytes=_vmem_limit(16 * ts * inw * 4)),
    )(z, z, *consts)


def _mixer_bwd_a(z, cb, dpool, dy, mp, name):
    s, inw = z.shape
    d = SGU_WIDTH + CONV_WIDTH + POOL_WIDTH
    ts = _row_tile(s, 256)
    nchunk = ts // CHUNK

    def rms_bwd(dyn, y, g):
        r = lax.rsqrt(_rmean(y * y) + EPS)
        yn = y * r
        dg = _rsum(dyn * yn)
        t = dyn * g
        return r * (t - yn * _rmean(t * yn)), dg

    def body(z_ref, cbs, dpool_ref, dy_ref, bd_ref, ng_ref, nb_ref, wm_ref, wmt_ref, bias_ref, cng_ref, cnb_ref,
             pw_ref, pwt_ref, ps_ref, bg_ref,
             dza_ref, dcb_ref, dd_ref, dbg_ref, dwm_ref, dbias_ref, dng_ref, dnb_ref, dcng_ref, dcnb_ref, dps_ref, dpw_ref,
             ya_s, f_s, dvn_s):
        i = pl.program_id(0)
        first = i == 0

        @pl.when(first)
        def _():
            for ref in (dwm_ref, dbias_ref, dng_ref, dnb_ref, dcng_ref, dcnb_ref, dps_ref, dpw_ref):
                ref[...] = jnp.zeros_like(ref)

        u, vhat, rstd, vn = _sgu_forward(z_ref, bd_ref, ng_ref, nb_ref, wm_ref, bias_ref, ts, ya_s, f_s)
        dya, dbg_a = rms_bwd(dy_ref[:, 0:SGU_WIDTH], ya_s[...], bg_ref[:, 0:SGU_WIDTH])
        du = dya * f_s[...]
        df = dya * u
        dfb = df.astype(BF16)
        left = _lane_lt((CHUNK, CHUNK), HEAD_DIM)
        zero = jnp.zeros((CHUNK, CHUNK), BF16)
        dbias = jnp.zeros((CHUNK, SGU_WIDTH), F32)
        for n in range(nchunk):
            rows = slice(n * CHUNK, (n + 1) * CHUNK)
            dbias = dbias + df[rows, :]
            for p in range(SGU_WIDTH // CHUNK):
                cols = slice(p * CHUNK, (p + 1) * CHUNK)
                dblk = dfb[rows, cols]
                vblk = vn[rows, cols]
                dwm_ref[2 * p] += _dot_nt(jnp.where(left, dblk, zero), vblk)
                dwm_ref[2 * p + 1] += _dot_nt(jnp.where(left, zero, dblk), vblk)
                dvn_s[rows, cols] = jnp.where(left, _dot(wmt_ref[2 * p], dblk), _dot(wmt_ref[2 * p + 1], dblk))
        dbias_ref[...] += dbias
        dvn = dvn_s[...]
        dng_ref[...] += _rsum(dvn * vhat)
        dnb_ref[...] += _rsum(dvn)
        dvh = dvn * ng_ref[...]
        bd = bd_ref[...]
        dv = rstd * (dvh - _group_mean(dvh, bd) - vhat * _group_mean(dvh * vhat, bd))
        dza_ref[:, 0:SGU_WIDTH] = (du * _gelu_grad(z_ref[:, 0:SGU_WIDTH])).astype(BF16)
        dza_ref[:, SGU_WIDTH:2 * SGU_WIDTH] = (dv * _gelu_grad(z_ref[:, SGU_WIDTH:2 * SGU_WIDTH])).astype(BF16)

        chat, crstd = _layer_norm_rows(cbs[...])
        lin = chat * cng_ref[...] + cnb_ref[...]
        sl = jax.nn.sigmoid(lin)
        dyb, dbg_b = rms_bwd(dy_ref[:, SGU_WIDTH:SGU_WIDTH + CONV_WIDTH], lin * sl, bg_ref[:, SGU_WIDTH:SGU_WIDTH + CONV_WIDTH])
        dlin = dyb * (sl * (1.0 + lin * (1.0 - sl)))
        dcng_ref[...] += _rsum(dlin * chat)
        dcnb_ref[...] += _rsum(dlin)
        dch = dlin * cng_ref[...]
        dcb_ref[...] = crstd * (dch - _rmean(dch) - chat * _rmean(dch * chat))

        dpb = dpool_ref[...].astype(BF16)
        ycp = _dot(dpb, pw_ref[...])
        dyc, dbg_c = rms_bwd(dy_ref[:, SGU_WIDTH + CONV_WIDTH:d], ycp * ps_ref[...], bg_ref[:, SGU_WIDTH + CONV_WIDTH:d])
        dps_ref[...] += _rsum(dyc * ycp)
        dycp = (dyc * ps_ref[...]).astype(BF16)
        dpw_ref[...] += _dot_tn(dpb, dycp)
        dd_ref[...] = _dot(dycp, pwt_ref[...])

        @pl.when(first)
        def _():
            dbg_ref[...] = jnp.zeros_like(dbg_ref)

        dbg_ref[:, 0:SGU_WIDTH] += dbg_a
        dbg_ref[:, SGU_WIDTH:SGU_WIDTH + CONV_WIDTH] += dbg_b
        dbg_ref[:, SGU_WIDTH + CONV_WIDTH:d] += dbg_c

    tile, _, _ = _mixer_specs(s, ts, inw)
    consts = [mp["bd"], mp["ng"], mp["nb"], mp["wm"], mp["wmt"], mp["bias"], mp["cng"], mp["cnb"],
              mp["pw"], mp["pwt"], mp["ps"], mp["bg"]]
    acc_shapes = [(1, d), (2 * (SGU_WIDTH // CHUNK), CHUNK, CHUNK), (CHUNK, SGU_WIDTH), (1, SGU_WIDTH), (1, SGU_WIDTH),
                  (1, CONV_WIDTH), (1, CONV_WIDTH), (1, POOL_WIDTH), (POOL_WIDTH, POOL_WIDTH)]
    return _hbm_call(
        body, name=name,
        grid=(s // ts,),
        in_specs=[tile, pl.BlockSpec((ts, CONV_WIDTH), lambda i: (i, 0)), pl.BlockSpec((ts, POOL_WIDTH), lambda i: (i, 0)),
                  pl.BlockSpec((ts, d), lambda i: (i, 0))] + [_const_spec(c.shape) for c in consts],
        out_specs=[pl.BlockSpec((ts, 2 * SGU_WIDTH), lambda i: (i, 0)), pl.BlockSpec((ts, CONV_WIDTH), lambda i: (i, 0)),
                   pl.BlockSpec((ts, POOL_WIDTH), lambda i: (i, 0))] + [_const_spec(a) for a in acc_shapes],
        out_shape=[jax.ShapeDtypeStruct((s, 2 * SGU_WIDTH), BF16), jax.ShapeDtypeStruct((s, CONV_WIDTH), F32),
                   jax.ShapeDtypeStruct((s, POOL_WIDTH), F32)] + [jax.ShapeDtypeStruct(a, F32) for a in acc_shapes],
        scratch_shapes=[pltpu.VMEM((ts, SGU_WIDTH), F32), pltpu.VMEM((ts, SGU_WIDTH), F32), pltpu.VMEM((ts, SGU_WIDTH), F32)],
        compiler_params=pltpu.CompilerParams(dimension_semantics=("arbitrary",),
                                             vmem_limit_bytes=_vmem_limit(24 * ts * inw * 4)),
    )(z, cb, dpool, dy, *consts)


def _mixer_bwd_b(z, dza, dcb, dd, x, dres, gain, s1p, cw, w, name):
    s, inw = z.shape
    d = x.shape[1]
    ts = _row_tile(s, 256)
    c0 = 2 * SGU_WIDTH
    c1 = c0 + 2 * CONV_WIDTH

    def body(z_ref, zh_ref, dza_ref, dcb_ref, dcbn_ref, dd_ref, ddn_ref, x_ref, dres_ref, g_ref, s_ref, cw_ref, w_ref,
             dx_ref, dz_ref, dsh_ref, dsc_ref, dg_ref, dcw_ref, dcbias_ref, ext_b, ext_n, ext_e, ext8):
        i = pl.program_id(0)
        first = i == 0
        last = i == pl.num_programs(0) - 1

        @pl.when(first)
        def _():
            for ref in (dsh_ref, dsc_ref, dg_ref, dcw_ref, dcbias_ref):
                ref[...] = jnp.zeros_like(ref)

        a = z_ref[:, c0:c0 + CONV_WIDTH]
        sg = jax.nn.sigmoid(z_ref[:, c0 + CONV_WIDTH:c1])
        ah = zh_ref[:, c0:c0 + CONV_WIDTH]
        gh = zh_ref[:, c0 + CONV_WIDTH:c1]
        ext_b[pl.ds(0, MIX_HALO), :] = jnp.where(first, 0.0, ah * jax.nn.sigmoid(gh))
        ext_b[pl.ds(MIX_HALO, ts), :] = a * sg
        _shifted_copies(ext_b, ext8, ts)
        dcbv = dcb_ref[...]
        dcbias_ref[...] += _rsum(dcbv)
        for k in range(CONV_K):
            dcw_ref[k:k + 1, :] += _rsum(dcbv * _rows_at(ext_b, ext8, MIX_HALO - (CONV_K - 1) + k, ts))

        ext_n[pl.ds(0, ts), :] = dcbv
        ext_n[pl.ds(ts, MIX_HALO), :] = jnp.where(last, 0.0, dcbn_ref[...])
        _shifted_copies(ext_n, ext8, ts)
        for r in range(ts // CONV_ROWS):
            acc = jnp.zeros((CONV_ROWS, CONV_WIDTH), F32)
            for k in range(CONV_K):
                acc = acc + cw_ref[k:k + 1, :] * _rows_at(ext_n, ext8, CONV_K - 1 - k + r * CONV_ROWS, CONV_ROWS)
            rows = pl.ds(r * CONV_ROWS, CONV_ROWS)
            ar = z_ref[rows, c0:c0 + CONV_WIDTH]
            sr = jax.nn.sigmoid(z_ref[rows, c0 + CONV_WIDTH:c1])
            dz_ref[rows, c0:c0 + CONV_WIDTH] = (acc * sr).astype(BF16)
            dz_ref[rows, c0 + CONV_WIDTH:c1] = (acc * ar * sr * (1.0 - sr)).astype(BF16)

        ddv = dd_ref[...]
        ext_e[pl.ds(0, ts), :] = ddv / _pool_counts(i, ts)
        nh = (i + 1) * ts + lax.broadcasted_iota(jnp.int32, (MIX_HALO, POOL_WIDTH), 0)
        lane = lax.broadcasted_iota(jnp.int32, (MIX_HALO, POOL_WIDTH), 1)
        gdim = POOL_WIDTH // len(POOL_WINDOWS)
        winh = jnp.where(lane < gdim, float(POOL_WINDOWS[0]),
                         jnp.where(lane < 2 * gdim, float(POOL_WINDOWS[1]),
                                   jnp.where(lane < 3 * gdim, float(POOL_WINDOWS[2]), float(POOL_WINDOWS[3]))))
        cnth = jnp.minimum((nh + 1).astype(F32), winh)
        ext_e[pl.ds(ts, MIX_HALO), :] = jnp.where(last, 0.0, ddn_ref[...] / cnth)
        dz_ref[:, c1:inw] = (_window_sums(ext_e, 0, ts, 1) - ddv).astype(BF16)
        dz_ref[:, 0:c0] = dza_ref[...]

        dh = _dot(dz_ref[...], w_ref[...])
        dx, dsh, dsc, dg = _norm_mod_bwd(dh, x_ref[...], g_ref[...], s_ref[...], dres_ref[...])
        dx_ref[...] = dx
        dsh_ref[...] += dsh
        dsc_ref[...] += dsc
        dg_ref[...] += dg

    tile, prev, _ = _mixer_specs(s, ts, inw)
    _, _, nxt_b = _mixer_specs(s, ts, CONV_WIDTH)
    _, _, nxt_c = _mixer_specs(s, ts, POOL_WIDTH)
    row = pl.BlockSpec((ts, d), lambda i: (i, 0))
    vec = pl.BlockSpec((1, d), lambda i: (0, 0))
    return _hbm_call(
        body, name=name,
        grid=(s // ts,),
        in_specs=[tile, prev, pl.BlockSpec((ts, c0), lambda i: (i, 0)),
                  pl.BlockSpec((ts, CONV_WIDTH), lambda i: (i, 0)), nxt_b,
                  pl.BlockSpec((ts, POOL_WIDTH), lambda i: (i, 0)), nxt_c,
                  row, row, vec, vec, _const_spec(cw.shape),
                  pl.BlockSpec(w.shape, lambda i: (0, 0), pipeline_mode=pl.Buffered(1))],
        out_specs=[row, pl.BlockSpec((ts, inw), lambda i: (i, 0)), vec, vec, vec,
                   _const_spec((CONV_K, CONV_WIDTH)), _const_spec((1, CONV_WIDTH))],
        out_shape=[jax.ShapeDtypeStruct((s, d), F32), jax.ShapeDtypeStruct((s, inw), BF16)]
        + [jax.ShapeDtypeStruct((1, d), F32)] * 3
        + [jax.ShapeDtypeStruct((CONV_K, CONV_WIDTH), F32), jax.ShapeDtypeStruct((1, CONV_WIDTH), F32)],
        scratch_shapes=[pltpu.VMEM((ts + MIX_HALO, CONV_WIDTH), F32), pltpu.VMEM((ts + MIX_HALO, CONV_WIDTH), F32),
                        pltpu.VMEM((ts + MIX_HALO, POOL_WIDTH), F32), pltpu.VMEM((7, ts + MIX_HALO - 8, CONV_WIDTH), F32)],
        compiler_params=pltpu.CompilerParams(dimension_semantics=("arbitrary",),
                                             vmem_limit_bytes=_vmem_limit(18 * ts * inw * 4 + inw * d * 2)),
    )(z, z, dza, dcb, dcb, dd, dd, x, dres, gain, s1p, cw, w)


def _ffn_specs(s, ts, tc, half_blocks):
    nbh = ts // FFN_HALO

    def tile(off):
        return pl.BlockSpec((ts, tc), lambda j, i: (i, j + off))

    def prev(off):
        return pl.BlockSpec((FFN_HALO, tc), lambda j, i: (jnp.maximum(i * nbh - 1, 0), j + off))

    def vec(rows, off):
        return pl.BlockSpec((rows, tc), lambda j, i: (0, j + off))

    return tile, prev, vec


def _rows_before(cur, prev, k):
    row = lax.broadcasted_iota(jnp.int32, cur.shape, 0)
    return jnp.where(row >= k, pltpu.roll(cur, k, 0), pltpu.roll(prev, k, 0))


def _conv3_rows(cur, prev, w_ref, b_ref, cols):
    x1 = _rows_before(cur, prev, 1)
    x2 = _rows_before(cur, prev, 2)
    u = b_ref[:, cols] + w_ref[2:3, cols] * cur + w_ref[1:2, cols] * x1 + w_ref[0:1, cols] * x2
    return u, x2, x1


def _halo_chunk(h_ref, cols, first):
    h = jnp.where(first, 0.0, h_ref[:, cols])
    return jnp.concatenate([h] * (FFN_ROWS // FFN_HALO), axis=0)


def _ffn_act_fwd(p, cw, cb, name):
    s, f2 = p.shape
    f = f2 // 2
    tc = f // 2
    hb = f // tc
    ts = _row_tile(s, 256)

    def body(pg_ref, pgh_ref, pv_ref, pvh_ref, wg_ref, wv_ref, bg_ref, bv_ref, act_ref):
        first = pl.program_id(1) == 0
        for c in range(tc // LANES):
            cols = slice(c * LANES, (c + 1) * LANES)

            def chunk(r, carry, cols=cols):
                pg_prev, pv_prev = carry
                rows = pl.ds(pl.multiple_of(r * FFN_ROWS, FFN_ROWS), FFN_ROWS)
                pg = pg_ref[rows, cols]
                pv = pv_ref[rows, cols]
                ug, _, _ = _conv3_rows(pg, pg_prev, wg_ref, bg_ref, cols)
                uv, _, _ = _conv3_rows(pv, pv_prev, wv_ref, bv_ref, cols)
                act_ref[rows, cols] = (_gelu(ug) * uv).astype(BF16)
                return pg, pv

            def step(r, carry, chunk=chunk):
                for u in range(FFN_UNROLL):
                    carry = chunk(r * FFN_UNROLL + u, carry)
                return carry

            lax.fori_loop(0, ts // (FFN_ROWS * FFN_UNROLL), step,
                          (_halo_chunk(pgh_ref, cols, first), _halo_chunk(pvh_ref, cols, first)))

    tile, prev, vec = _ffn_specs(s, ts, tc, hb)
    return _hbm_call(
        body, name=name,
        grid=(hb, s // ts),
        in_specs=[tile(0), prev(0), tile(hb), prev(hb), vec(FFN_CONV_K, 0), vec(FFN_CONV_K, hb), vec(1, 0), vec(1, hb)],
        out_specs=pl.BlockSpec((ts, tc), lambda j, i: (i, j)),
        out_shape=jax.ShapeDtypeStruct((s, f), BF16),
        compiler_params=pltpu.CompilerParams(dimension_semantics=("arbitrary", "arbitrary"),
                                             vmem_limit_bytes=_vmem_limit(8 * ts * tc * 4)),
    )(p, p, p, p, cw, cw, cb, cb)


def _ffn_act_bwd(p, dact, cw, cb, name):
    s, f2 = p.shape
    f = f2 // 2
    tc = f // 2
    hb = f // tc
    ts = _row_tile(s, 256)

    def body(pg_ref, pgh_ref, pv_ref, pvh_ref, da_ref, wg_ref, wv_ref, bg_ref, bv_ref,
             dug_ref, duv_ref, dwg_ref, dwv_ref, dbg_ref, dbv_ref):
        first = pl.program_id(1) == 0

        @pl.when(first)
        def _():
            for ref in (dwg_ref, dwv_ref, dbg_ref, dbv_ref):
                ref[...] = jnp.zeros_like(ref)

        zero = jnp.zeros((FFN_ROWS, LANES), F32)
        for c in range(tc // LANES):
            cols = slice(c * LANES, (c + 1) * LANES)

            def chunk(r, carry, cols=cols):
                pg_prev, pv_prev, ag0, ag1, ag2, av0, av1, av2, sg, sv = carry
                rows = pl.ds(pl.multiple_of(r * FFN_ROWS, FFN_ROWS), FFN_ROWS)
                pg = pg_ref[rows, cols]
                pv = pv_ref[rows, cols]
                ug, pg2, pg1 = _conv3_rows(pg, pg_prev, wg_ref, bg_ref, cols)
                uv, pv2, pv1 = _conv3_rows(pv, pv_prev, wv_ref, bv_ref, cols)
                da = da_ref[rows, cols]
                dug = da * uv * _gelu_grad(ug)
                duv = da * _gelu(ug)
                dug_ref[rows, cols] = dug
                duv_ref[rows, cols] = duv
                return (pg, pv, ag0 + dug * pg2, ag1 + dug * pg1, ag2 + dug * pg,
                        av0 + duv * pv2, av1 + duv * pv1, av2 + duv * pv, sg + dug, sv + duv)

            def step(r, carry, chunk=chunk):
                for u in range(FFN_UNROLL):
                    carry = chunk(r * FFN_UNROLL + u, carry)
                return carry

            out = lax.fori_loop(0, ts // (FFN_ROWS * FFN_UNROLL), step,
                                (_halo_chunk(pgh_ref, cols, first), _halo_chunk(pvh_ref, cols, first)) + (zero,) * 8)
            for k in range(FFN_CONV_K):
                dwg_ref[k:k + 1, cols] += _rsum(out[2 + k])
                dwv_ref[k:k + 1, cols] += _rsum(out[5 + k])
            dbg_ref[:, cols] += _rsum(out[8])
            dbv_ref[:, cols] += _rsum(out[9])

    tile, prev, vec = _ffn_specs(s, ts, tc, hb)
    half = pl.BlockSpec((ts, tc), lambda j, i: (i, j))
    wacc = pl.BlockSpec((FFN_CONV_K, tc), lambda j, i: (0, j))
    bacc = pl.BlockSpec((1, tc), lambda j, i: (0, j))
    return _hbm_call(
        body, name=name,
        grid=(hb, s // ts),
        in_specs=[tile(0), prev(0), tile(hb), prev(hb), half, vec(FFN_CONV_K, 0), vec(FFN_CONV_K, hb), vec(1, 0), vec(1, hb)],
        out_specs=[half, half, wacc, wacc, bacc, bacc],
        out_shape=[jax.ShapeDtypeStruct((s, f), F32)] * 2 + [jax.ShapeDtypeStruct((FFN_CONV_K, f), F32)] * 2
        + [jax.ShapeDtypeStruct((1, f), F32)] * 2,
        compiler_params=pltpu.CompilerParams(dimension_semantics=("arbitrary", "arbitrary"),
                                             vmem_limit_bytes=_vmem_limit(12 * ts * tc * 4)),
    )(p, p, p, p, dact, cw, cw, cb, cb)


def _ffn_in_bwd(dug, duv, cw, w, x, dres, gain, s1p, name):
    s, f = dug.shape
    d = x.shape[1]
    ts = _row_tile(s, 256)
    tc = w.shape[2]
    assert f % tc == 0 and w.shape[0] * tc == 2 * f
    nbh = ts // FFN_HALO

    def body(dug_ref, dugn_ref, duv_ref, duvn_ref, cw_ref, w_ref, x_ref, dres_ref, g_ref, s_ref,
             dx_ref, dp_ref, dsh_ref, dsc_ref, dg_ref, ext):
        i = pl.program_id(0)
        last = i == pl.num_programs(0) - 1

        @pl.when(i == 0)
        def _():
            for ref in (dsh_ref, dsc_ref, dg_ref):
                ref[...] = jnp.zeros_like(ref)

        dh = jnp.zeros((ts, d), F32)
        for half, (t_ref, n_ref) in enumerate(((dug_ref, dugn_ref), (duv_ref, duvn_ref))):
            for cb in range(f // tc):
                cols = slice(cb * tc, (cb + 1) * tc)
                wcols = slice(half * f + cb * tc, half * f + (cb + 1) * tc)
                ext[pl.ds(0, ts), :] = t_ref[:, cols]
                ext[pl.ds(ts, FFN_HALO), :] = jnp.where(last, 0.0, n_ref[:, cols])
                acc = cw_ref[FFN_CONV_K - 1:FFN_CONV_K, wcols] * t_ref[:, cols]
                for k in range(FFN_CONV_K - 1):
                    acc = acc + cw_ref[k:k + 1, wcols] * ext[pl.ds(FFN_CONV_K - 1 - k, ts), :]
                dpb = acc.astype(BF16)
                dp_ref[:, wcols] = dpb
                dh = dh + _dot_nt(dpb, w_ref[half * (f // tc) + cb])
        dx, dsh, dsc, dg = _norm_mod_bwd(dh, x_ref[...], g_ref[...], s_ref[...], dres_ref[...])
        dx_ref[...] = dx
        dsh_ref[...] += dsh
        dsc_ref[...] += dsc
        dg_ref[...] += dg

    tile = pl.BlockSpec((ts, f), lambda i: (i, 0))
    nxt = pl.BlockSpec((FFN_HALO, f), lambda i: (jnp.minimum((i + 1) * nbh, s // FFN_HALO - 1), 0))
    row = pl.BlockSpec((ts, d), lambda i: (i, 0))
    vec = pl.BlockSpec((1, d), lambda i: (0, 0))
    return _hbm_call(
        body, name=name,
        grid=(s // ts,),
        in_specs=[tile, nxt, tile, nxt, _const_spec(cw.shape),
                  pl.BlockSpec(w.shape, lambda i: (0, 0, 0), pipeline_mode=pl.Buffered(1)), row, row, vec, vec],
        out_specs=[row, pl.BlockSpec((ts, 2 * f), lambda i: (i, 0)), vec, vec, vec],
        out_shape=[jax.ShapeDtypeStruct((s, d), F32), jax.ShapeDtypeStruct((s, 2 * f), BF16)] + [jax.ShapeDtypeStruct((1, d), F32)] * 3,
        scratch_shapes=[pltpu.VMEM((ts + FFN_HALO, tc), F32)],
        compiler_params=pltpu.CompilerParams(
            dimension_semantics=("arbitrary",),
            vmem_limit_bytes=_vmem_limit(4 * ts * f * 4 + 2 * f * d * 2 + 2 * ts * 2 * f * 2 + 12 * ts * d * 4 + 6 * ts * tc * 4)),
    )(dug, dug, duv, duv, cw, w, x, dres, gain, s1p)


def _adamw_math(w, g, m, v):
    m = ADAM_B1 * m + (1.0 - ADAM_B1) * g
    v = ADAM_B2 * v + (1.0 - ADAM_B2) * (g * g)
    m_hat = m / (1.0 - ADAM_B1 ** ADAM_STEP)
    v_hat = v / (1.0 - ADAM_B2 ** ADAM_STEP)
    delta = -ADAM_LR * (m_hat / (jnp.sqrt(v_hat) + ADAM_EPS) + ADAM_WD * w)
    return delta, m, v


def _adam_rows(rows, cols):
    want = max(8, (2 * 1024 * 1024 // (cols * 4)) // 8 * 8)
    tr = min(rows, want)
    while rows % tr:
        tr -= 8
    return tr


def _adamw(w, m, v, g_parts, name):
    shape = w.shape
    nl = shape[0] if w.ndim == 3 else 1
    r, c = shape[-2], shape[-1]
    tr = _adam_rows(r, c)
    ng = len(g_parts)

    def body(*refs):
        w_ref, m_ref, v_ref = refs[0:3]
        g_refs = refs[3:3 + ng]
        g_out, d_out, m_out, v_out = refs[3 + ng:]
        g = g_refs[0][...]
        for gr in g_refs[1:]:
            g = g + gr[...]
        delta, mn, vn = _adamw_math(w_ref[...], g, m_ref[...], v_ref[...])
        g_out[...] = g
        d_out[...] = delta
        m_out[...] = mn
        v_out[...] = vn

    blk = pl.BlockSpec((None, tr, c), lambda l, i: (l, i, 0))
    outs = _hbm_call(
        body, name=name,
        grid=(nl, r // tr),
        in_specs=[blk] * (3 + ng),
        out_specs=[blk] * 4,
        out_shape=[jax.ShapeDtypeStruct((nl, r, c), F32)] * 4,
        compiler_params=pltpu.CompilerParams(dimension_semantics=("arbitrary", "arbitrary"),
                                             vmem_limit_bytes=_vmem_limit(2 * (7 + ng) * tr * max(c, 128) * 4 + (8 << 20))),
    )(*[a.reshape(nl, r, c) for a in (w, m, v, *g_parts)])
    return [o.reshape(shape) for o in outs]


def _adamw_many(ws, ms, vs, gs, name):
    n = len(ws)

    def body(*refs):
        for k in range(n):
            w_ref, m_ref, v_ref, g_ref = refs[k], refs[n + k], refs[2 * n + k], refs[3 * n + k]
            delta, mn, vn = _adamw_math(w_ref[...], g_ref[...], m_ref[...], v_ref[...])
            refs[4 * n + 3 * k][...] = delta
            refs[4 * n + 3 * k + 1][...] = mn
            refs[4 * n + 3 * k + 2][...] = vn

    specs = [_const_spec(a.shape) for a in ws]
    outs = _hbm_call(
        body, name=name,
        grid=(1,),
        in_specs=specs * 4,
        out_specs=[sp for sp in specs for _ in range(3)],
        out_shape=[jax.ShapeDtypeStruct(a.shape, F32) for a in ws for _ in range(3)],
        compiler_params=pltpu.CompilerParams(dimension_semantics=("arbitrary",),
                                             vmem_limit_bytes=_vmem_limit(20 * sum(a.size for a in ws) * 4 + (8 << 20))),
    )(*ws, *ms, *vs, *gs)
    return [tuple(outs[3 * k:3 * k + 3]) for k in range(n)]


def _modw_adamw(sct, dmod, w, m, v, name):
    nl, d, n = w.shape
    tr = _row_tile(d, 128)

    def body(sct_ref, dm_ref, w_ref, m_ref, v_ref, g_out, d_out, m_out, v_out):
        sc = sct_ref[...].astype(BF16).astype(F32)
        dm = dm_ref[...].astype(BF16).astype(F32)
        g = sc[:, 0:1] * dm[0:1, :]
        for b in range(1, N_DEV):
            g = g + sc[:, b:b + 1] * dm[b:b + 1, :]
        delta, mn, vn = _adamw_math(w_ref[...], g, m_ref[...], v_ref[...])
        g_out[...] = g
        d_out[...] = delta
        m_out[...] = mn
        v_out[...] = vn

    blk = pl.BlockSpec((None, tr, n), lambda l, i: (l, i, 0))
    return _hbm_call(
        body, name=name,
        grid=(nl, d // tr),
        in_specs=[pl.BlockSpec((tr, N_DEV), lambda l, i: (i, 0)), pl.BlockSpec((None, N_DEV, n), lambda l, i: (l, 0, 0)),
                  blk, blk, blk],
        out_specs=[blk] * 4,
        out_shape=[jax.ShapeDtypeStruct((nl, d, n), F32)] * 4,
        compiler_params=pltpu.CompilerParams(dimension_semantics=("arbitrary", "arbitrary"),
                                             vmem_limit_bytes=_vmem_limit(2 * 8 * tr * n * 4 + (8 << 20))),
    )(sct, dmod, w, m, v)


def _reduce4(recvs, name):
    nl = len(recvs)
    shape = recvs[0].shape[1:]
    c = shape[-1]
    r = math.prod(shape[:-1])
    tr = _adam_rows(r, c)
    nt = r // tr

    def body(*refs):
        o_ref = refs[nl]
        for l in range(nl):
            @pl.when(pl.program_id(0) == l)
            def _():
                acc = refs[l][0].astype(F32)
                for k in range(1, N_CHIPS):
                    acc = acc + refs[l][k].astype(F32)
                o_ref[...] = acc

    def in_map(l):
        return lambda ll, i: (0, jnp.where(ll < l, 0, jnp.where(ll > l, nt - 1, i)), 0)

    return _hbm_call(
        body, name=name,
        grid=(nl, nt),
        in_specs=[pl.BlockSpec((N_CHIPS, tr, c), in_map(l)) for l in range(nl)],
        out_specs=pl.BlockSpec((None, tr, c), lambda ll, i: (ll, i, 0)),
        out_shape=jax.ShapeDtypeStruct((nl, r, c), F32),
        compiler_params=pltpu.CompilerParams(dimension_semantics=("arbitrary", "arbitrary"),
                                             vmem_limit_bytes=_vmem_limit(2 * 8 * nl * tr * max(c, 128) * 4 + (8 << 20))),
    )(*[rv.reshape(N_CHIPS, r, c) for rv in recvs]).reshape((nl,) + shape)


def _my_place():
    return lax.axis_index("x"), lax.axis_index("y"), lax.axis_index("c")


def _chip_coords(j):
    return j // 2, j % 2


def _mod_forward(c, mod_w, mod_b4):
    nl, d, n = mod_w.shape
    kc = 256

    def body(c_ref, w_ref, b_ref, mod_ref, sc_ref, cbuf, stage, s1, r1, s2, r2):
        mx, my, mc = _my_place()
        me = 4 * mx + 2 * my + mc
        q = 2 * mx + my
        cv = c_ref[...]
        cbuf[me] = jnp.broadcast_to(cv * jax.nn.sigmoid(cv), (8, d))
        sends = []
        for t in range(N_DEV):
            tx, ty = _chip_coords(t // 2)
            cp = pltpu.make_async_remote_copy(src_ref=cbuf.at[me], dst_ref=cbuf.at[me], send_sem=s1.at[t], recv_sem=r1.at[me],
                                              device_id=(tx, ty, t % 2), device_id_type=MESH)

            @pl.when(t != me)
            def _():
                cp.start()

            sends.append((t, cp))
        for t in range(N_DEV):
            @pl.when(t != me)
            def _():
                pltpu.make_async_remote_copy(src_ref=cbuf.at[t], dst_ref=cbuf.at[t], send_sem=s1.at[t], recv_sem=r1.at[t],
                                             device_id=(mx, my, mc), device_id_type=MESH).wait_recv()
        for t, cp in sends:
            @pl.when(t != me)
            def _():
                cp.wait_send()

        row = lax.broadcasted_iota(jnp.int32, (8, d), 0)
        sc_all = jnp.zeros((8, d), F32)
        for t in range(N_DEV):
            sc_all = sc_all + jnp.where(row == t, cbuf[t], 0.0)
        sc_ref[...] = sc_all
        rown = lax.broadcasted_iota(jnp.int32, (8, n), 0)
        for l in range(nl):
            acc = jnp.zeros((8, n), F32)
            for k0 in range(0, d, kc):
                acc = acc + _dot(sc_all[:, k0:k0 + kc].astype(BF16), w_ref[l, k0:k0 + kc, :].astype(BF16))
            acc = acc + b_ref[l, q]
            for j in range(N_CHIPS):
                jx, jy = _chip_coords(j)
                bdest = 4 * jx + 2 * jy + mc
                rowv = jnp.sum(jnp.where(rown == bdest, acc, 0.0), axis=0, keepdims=True)
                stage[j, l] = jnp.broadcast_to(rowv, (8, n))
        sends2 = []
        for j in range(N_CHIPS):
            jx, jy = _chip_coords(j)
            cp = pltpu.make_async_remote_copy(src_ref=stage.at[j], dst_ref=mod_ref.at[:, q], send_sem=s2.at[j], recv_sem=r2.at[q],
                                              device_id=(jx, jy, mc), device_id_type=MESH)

            @pl.when(j != q)
            def _():
                cp.start()

            @pl.when(j == q)
            def _():
                for l in range(nl):
                    mod_ref[l, j] = stage[j, l]

            sends2.append((j, cp))
        for j in range(N_CHIPS):
            @pl.when(j != q)
            def _():
                pltpu.make_async_remote_copy(src_ref=stage.at[j], dst_ref=mod_ref.at[:, j], send_sem=s2.at[j], recv_sem=r2.at[j],
                                             device_id=(mx, my, mc), device_id_type=MESH).wait_recv()
        for j, cp in sends2:
            @pl.when(j != q)
            def _():
                cp.wait_send()

    vm = pl.BlockSpec(memory_space=pltpu.VMEM)
    return pl.pallas_call(
        body, name="mod_forward",
        in_specs=[vm, vm, vm],
        out_specs=[vm, vm],
        out_shape=[jax.ShapeDtypeStruct((nl, N_CHIPS, 8, n), F32), jax.ShapeDtypeStruct((8, d), F32)],
        scratch_shapes=[pltpu.VMEM((N_DEV, 8, d), F32), pltpu.VMEM((N_CHIPS, nl, 8, n), F32),
                        pltpu.SemaphoreType.DMA((N_DEV,)), pltpu.SemaphoreType.DMA((N_DEV,)),
                        pltpu.SemaphoreType.DMA((N_CHIPS,)), pltpu.SemaphoreType.DMA((N_CHIPS,))],
        compiler_params=pltpu.CompilerParams(vmem_limit_bytes=_vmem_limit(2 * nl * d * n * 4 + (8 << 20))),
    )(c, mod_w, mod_b4)


_HBM_SPEC = pl.BlockSpec(memory_space=pltpu.HBM)
_SEM_SPEC = pl.BlockSpec(memory_space=pltpu.SEMAPHORE)
_DATAFLOW = pltpu.SideEffectType.DATAFLOW_SIDE_EFFECTING


def _slot(ref, scatter, j):
    return ref.at[j] if scatter else ref


def _exchange_start(groups, scatter, after, name):
    flat = [a for g in groups for a in g]
    na = len(flat)
    ng = len(groups)
    sizes = [len(g) for g in groups]
    first = [sum(sizes[:g]) for g in range(ng)]
    where = [(g, k) for g in range(ng) for k in range(sizes[g])]
    mx, my, _ = _my_place()
    qo = 2 * mx + my
    lands = []
    for a in flat:
        own = lax.dynamic_index_in_dim(a, qo, 0, keepdims=False) if scatter else a
        lands.append(lax.dynamic_update_index_in_dim(lax.empty((N_CHIPS,) + own.shape, a.dtype), own, qo, 0))

    def body(*refs):
        ins, lnd = refs[:na], refs[na:2 * na]
        ssems, rsems = refs[2 * na + 1:2 * na + 1 + ng], refs[2 * na + 1 + ng:2 * na + 1 + 2 * ng]
        token = refs[-1]
        mx, my, mc = _my_place()
        q = 2 * mx + my
        for j in range(N_CHIPS):
            jx, jy = _chip_coords(j)
            for a in range(na):
                g, k = where[a]

                @pl.when(j != q)
                def _():
                    pltpu.make_async_remote_copy(src_ref=_slot(ins[a], scatter, j), dst_ref=lnd[a].at[q],
                                                 send_sem=ssems[g].at[k * N_CHIPS + j], recv_sem=rsems[g].at[k * N_CHIPS + q],
                                                 device_id=(jx, jy, mc), device_id_type=MESH).start()
        token[...] = jnp.zeros_like(token)

    sem_shapes = [pltpu.SemaphoreType.DMA((n * N_CHIPS,)) for n in sizes]
    outs = pl.pallas_call(
        body, name=name,
        in_specs=[_HBM_SPEC] * (2 * na) + [pl.BlockSpec(memory_space=pl.ANY)],
        out_specs=[_SEM_SPEC] * (2 * ng) + [_HBM_SPEC] * (2 * na) + [pl.BlockSpec(memory_space=pltpu.VMEM)],
        out_shape=sem_shapes + sem_shapes + [pltpu.HBM(a.shape, a.dtype) for a in flat + lands]
        + [jax.ShapeDtypeStruct((8, 128), F32)],
        input_output_aliases={i: 2 * ng + i for i in range(2 * na)},
        compiler_params=pltpu.CompilerParams(has_side_effects=_DATAFLOW),
    )(*[pltpu.with_memory_space_constraint(a, pltpu.HBM) for a in flat + lands], after)
    ssems, rsems = outs[:ng], outs[ng:2 * ng]
    src_thru, land_thru = outs[2 * ng:2 * ng + na], outs[2 * ng + na:2 * ng + 2 * na]
    states = [(src_thru[first[g]:first[g] + sizes[g]], land_thru[first[g]:first[g] + sizes[g]], ssems[g], rsems[g])
              for g in range(ng)]
    return states, outs[-1]


def _exchange_wait(state, scatter, after, name):
    srcs, lands, ssem, rsem = state
    na = len(srcs)

    def body(*refs):
        ins, lnd = refs[:na], refs[na:2 * na]
        ssem_ref, rsem_ref = refs[2 * na], refs[2 * na + 1]
        mx, my, mc = _my_place()
        q = 2 * mx + my
        for j in range(N_CHIPS):
            for a in range(na):
                @pl.when(j != q)
                def _():
                    cp = pltpu.make_async_remote_copy(src_ref=_slot(ins[a], scatter, j), dst_ref=lnd[a].at[j],
                                                      send_sem=ssem_ref.at[a * N_CHIPS + j], recv_sem=rsem_ref.at[a * N_CHIPS + j],
                                                      device_id=(mx, my, mc), device_id_type=MESH)
                    cp.wait_send()
                    cp.wait_recv()

    outs = pl.pallas_call(
        body, name=name,
        in_specs=[_HBM_SPEC] * (2 * na) + [_SEM_SPEC, _SEM_SPEC, pl.BlockSpec(memory_space=pl.ANY)],
        out_specs=[_HBM_SPEC] * (2 * na),
        out_shape=[pltpu.HBM(a.shape, a.dtype) for a in list(srcs) + list(lands)],
        input_output_aliases={i: i for i in range(2 * na)},
        compiler_params=pltpu.CompilerParams(has_side_effects=_DATAFLOW),
    )(*srcs, *lands, ssem, rsem, after)
    return outs[na:]


def _sibling_copy(src, dst, ssem, rsem, a):
    mx, my, mc = _my_place()
    return pltpu.make_async_remote_copy(src_ref=src, dst_ref=dst, send_sem=ssem.at[a], recv_sem=rsem.at[a],
                                        device_id=(mx, my, 1 - mc), device_id_type=MESH)


def _swap_start(arrs, after, name):
    na = len(arrs)
    lands = [lax.empty(a.shape, a.dtype) for a in arrs]

    def body(*refs):
        ins, lnd = refs[:na], refs[na:2 * na]
        ssem, rsem, token = refs[2 * na + 1], refs[2 * na + 2], refs[-1]
        for a in range(na):
            _sibling_copy(ins[a], lnd[a], ssem, rsem, a).start()
        token[...] = jnp.zeros_like(token)

    outs = pl.pallas_call(
        body, name=name,
        in_specs=[_HBM_SPEC] * (2 * na) + [pl.BlockSpec(memory_space=pl.ANY)],
        out_specs=[_SEM_SPEC] * 2 + [_HBM_SPEC] * (2 * na) + [pl.BlockSpec(memory_space=pltpu.VMEM)],
        out_shape=[pltpu.SemaphoreType.DMA((na,))] * 2 + [pltpu.HBM(a.shape, a.dtype) for a in list(arrs) + lands]
        + [jax.ShapeDtypeStruct((8, 128), F32)],
        input_output_aliases={i: 2 + i for i in range(2 * na)},
        compiler_params=pltpu.CompilerParams(has_side_effects=_DATAFLOW),
    )(*[pltpu.with_memory_space_constraint(a, pltpu.HBM) for a in list(arrs) + lands], after)
    return (outs[2:2 + na], outs[2 + na:2 + 2 * na], outs[0], outs[1]), outs[-1]


def _swap_wait(state, after, name):
    srcs, lands, ssem, rsem = state
    na = len(srcs)

    def body(*refs):
        ins, lnd = refs[:na], refs[na:2 * na]
        ssem_ref, rsem_ref = refs[2 * na], refs[2 * na + 1]
        for a in range(na):
            cp = _sibling_copy(ins[a], lnd[a], ssem_ref, rsem_ref, a)
            cp.wait_send()
            cp.wait_recv()

    outs = pl.pallas_call(
        body, name=name,
        in_specs=[_HBM_SPEC] * (2 * na) + [_SEM_SPEC, _SEM_SPEC, pl.BlockSpec(memory_space=pl.ANY)],
        out_specs=[_HBM_SPEC] * (2 * na),
        out_shape=[pltpu.HBM(a.shape, a.dtype) for a in list(srcs) + list(lands)],
        input_output_aliases={i: i for i in range(2 * na)},
        compiler_params=pltpu.CompilerParams(has_side_effects=_DATAFLOW),
    )(*srcs, *lands, ssem, rsem, after)
    return outs[:na], outs[na:]


def _allreduce_small(rows_all, rows_sum):
    ra, c = rows_all.shape
    r = rows_sum.shape[0]
    ch = r // N_DEV
    assert ch % 8 == 0 and ch * N_DEV == r

    def body(a_ref, s_ref, all_ref, sum_ref, rbuf, red, sa, rva, sb, rvb, sc, rvc):
        mx, my, mc = _my_place()
        me = 4 * mx + 2 * my + mc
        mine = pl.ds(pl.multiple_of(me * ch, 8), ch)
        all_ref[me] = a_ref[...]
        rbuf[me] = s_ref[mine, :]

        def dev(t):
            tx, ty = _chip_coords(t // 2)
            return (tx, ty, t % 2)

        def everyone_else(fn):
            for t in range(N_DEV):
                @pl.when(t != me)
                def _():
                    fn(t)

        def copy_a(t, slot):
            return pltpu.make_async_remote_copy(src_ref=a_ref, dst_ref=all_ref.at[slot], send_sem=sa.at[t], recv_sem=rva.at[slot],
                                                device_id=dev(t), device_id_type=MESH)

        def copy_b(t, slot):
            return pltpu.make_async_remote_copy(src_ref=s_ref.at[pl.ds(t * ch, ch), :], dst_ref=rbuf.at[slot], send_sem=sb.at[t],
                                                recv_sem=rvb.at[slot], device_id=dev(t), device_id_type=MESH)

        def copy_c(t, chunk_start, slot):
            return pltpu.make_async_remote_copy(src_ref=red, dst_ref=sum_ref.at[pl.ds(chunk_start, ch), :], send_sem=sc.at[t],
                                                recv_sem=rvc.at[slot], device_id=dev(t), device_id_type=MESH)

        everyone_else(lambda t: (copy_a(t, me).start(), copy_b(t, me).start()))
        everyone_else(lambda t: (copy_a(t, t).wait_recv(), copy_b(t, t).wait_recv()))
        everyone_else(lambda t: (copy_a(t, me).wait_send(), copy_b(t, me).wait_send()))
        acc = rbuf[0]
        for t in range(1, N_DEV):
            acc = acc + rbuf[t]
        red[...] = acc
        sum_ref[mine, :] = acc
        everyone_else(lambda t: copy_c(t, pl.multiple_of(me * ch, 8), me).start())
        everyone_else(lambda t: copy_c(t, t * ch, t).wait_recv())
        everyone_else(lambda t: copy_c(t, pl.multiple_of(me * ch, 8), me).wait_send())

    vm = pl.BlockSpec(memory_space=pltpu.VMEM)
    return pl.pallas_call(
        body, name="allreduce_small",
        in_specs=[vm, vm],
        out_specs=[vm, vm],
        out_shape=[jax.ShapeDtypeStruct((N_DEV, ra, c), F32), jax.ShapeDtypeStruct((r, c), F32)],
        scratch_shapes=[pltpu.VMEM((N_DEV, ch, c), F32), pltpu.VMEM((ch, c), F32)] + [pltpu.SemaphoreType.DMA((N_DEV,))] * 6,
        compiler_params=pltpu.CompilerParams(vmem_limit_bytes=_vmem_limit((3 * r + 2 * N_DEV * ra) * c * 4 + (4 << 20))),
    )(rows_all, rows_sum)


def _pack(arrs, row_multiple=8):
    rows, layout, at = [], [], 0
    for a in arrs:
        n = a.size
        nr = -(-n // (8 * SMALL_COLS)) * 8
        flat = a.reshape(-1)
        if nr * SMALL_COLS != n:
            flat = jnp.pad(flat, (0, nr * SMALL_COLS - n))
        rows.append(flat.reshape(nr, SMALL_COLS))
        layout.append((at, nr, a.shape))
        at += nr
    pad = -at % row_multiple
    if pad:
        rows.append(jnp.zeros((pad, SMALL_COLS), F32))
    return jnp.concatenate(rows, axis=0), layout


def _unpack(buf, layout):
    out = []
    for at, nr, shape in layout:
        n = math.prod(shape)
        out.append(buf[at:at + nr].reshape(-1)[:n].reshape(shape))
    return out


SMALL_NAMES = ("mod_b", "mix_pre_g", "mix_post_g", "sgu_norm_g", "sgu_norm_b", "sgu_w", "sgu_b", "conv_b", "conv_norm_g",
               "conv_norm_b", "pool_w", "pool_scale", "branch_g", "ffn_pre_g", "ffn_post_g", "ffn_conv_b")
SHARDED_SMALL = ("conv_w", "ffn_conv_w")
WEIGHT_ORDER = ("mod_w", "mod_b", "mix_pre_g", "mix_post_g", "w_in", "sgu_norm_g", "sgu_norm_b", "sgu_w", "sgu_b", "conv_w",
                "conv_b", "conv_norm_g", "conv_norm_b", "pool_w", "pool_scale", "branch_g", "w_out", "ffn_pre_g", "ffn_post_g",
                "ffn_up", "ffn_conv_w", "ffn_conv_b", "ffn_down")


def _block_diag(blocks):
    n, a, b = blocks.shape
    eye = jnp.eye(n, dtype=blocks.dtype)
    return (eye[:, None, :, None] * blocks[:, :, None, :]).reshape(n * a, n * b)


def _diag_blocks(mat, n):
    a = mat.shape[0] // n
    return jnp.stack([mat[g * a:(g + 1) * a, g * a:(g + 1) * a] for g in range(n)])


def _step(x, c, loss_target, w, m, v):
    nl = w["mod_w"].shape[0]
    s, d = x.shape[1], x.shape[2]
    heads = SGU_WIDTH // HEAD_DIM
    groups = len(POOL_WINDOWS)
    mx, my, _ = _my_place()
    q = 2 * mx + my
    x0 = x.reshape(s, d)
    tgt = loss_target.reshape(s, d)

    nmod = w["mod_w"].shape[2]
    kin = w["w_in"].shape[2]
    inw = kin * N_CHIPS
    f2 = w["ffn_up"].shape[2] * N_CHIPS
    f = f2 // 2

    def wgroups(l):
        return [[jnp.swapaxes(w["w_in"][l], 0, 1).astype(BF16), w["conv_w"][l], w["ffn_conv_w"][l]], [w["w_out"][l].astype(BF16)],
                [w["ffn_up"][l].astype(BF16)], [w["ffn_down"][l].astype(BF16)]]

    gstates = {}
    (gstates[0, 0], gstates[0, 1]), gtoken = _exchange_start(wgroups(0)[:2], False, c, "gather_start_in_0")
    mod4, sc_all = _mod_forward(c + gtoken[0:1, 0:1], w["mod_w"], w["mod_b"].reshape(nl, N_CHIPS, 1, nmod))
    mod = mod4[:, :, 0, :].reshape(nl, N_MOD, 1, d)

    tril = jnp.tril(jnp.ones((CHUNK, CHUNK), bool))
    bd = _block_diag(jnp.ones((heads, HEAD_DIM, HEAD_DIM), BF16))

    def mixer_params(l, conv_w):
        wm = jnp.where(tril[None], w["sgu_w"][l], 0.0)
        pw = _block_diag(w["pool_w"][l])
        return dict(
            bd=bd, ng=w["sgu_norm_g"][l][None], nb=w["sgu_norm_b"][l][None],
            wm=wm.astype(BF16), wmt=jnp.swapaxes(wm, 1, 2).astype(BF16),
            bias=jnp.repeat(w["sgu_b"][l].T, HEAD_DIM, axis=1),
            cw=conv_w, cb=w["conv_b"][l][None], cng=w["conv_norm_g"][l][None], cnb=w["conv_norm_b"][l][None],
            pw=pw.astype(BF16), pwt=pw.T.astype(BF16), ps=w["pool_scale"][l][None], bg=w["branch_g"][l][None])

    saved = []
    xl = x0
    arrived = {0: list(_exchange_wait(gstates[0, 0], False, mod4, "gather_wait_in_0"))}
    arrived[0] += list(_exchange_wait(gstates[0, 1], False, arrived[0][0], "gather_wait_out_0"))
    for l in range(nl):
        sh1, sc1, g1, sh2, sc2, g2 = [mod[l, k] for k in range(N_MOD)]
        gpre1, gpost1 = w["mix_pre_g"][l][None], w["mix_post_g"][l][None]
        gpre2, gpost2 = w["ffn_pre_g"][l][None], w["ffn_post_g"][l][None]
        fcb = w["ffn_conv_b"][l][None]
        sh1_after, bg_after, g1_after, sh2_after, fcb_after = sh1, w["branch_g"][l][None], g1, sh2, fcb
        g_win, g_cw, g_fcw = arrived[l][:3]
        if l == 0:
            (gstates[0, 2],), tok = _exchange_start(wgroups(0)[2:3], False, arrived[0][3], "gather_start_up_0")
            sh1_after = sh1 + tok[0:1, 0:1]
        w_in = g_win.reshape(inw, d)
        conv_w = jnp.transpose(g_cw, (1, 0, 2)).reshape(CONV_K, CONV_WIDTH)
        ffn_cw = jnp.transpose(g_fcw, (1, 0, 2)).reshape(FFN_CONV_K, f2)
        mp = mixer_params(l, conv_w)
        z, h1 = _norm_mod_matmul(xl, gpre1, 1.0 + sc1, sh1_after, w_in[None], f"mix_in_{l}", transposed=True)
        w_out = arrived[l][3].reshape(d, d)
        ycat, cbo, dpool = _mixer_fwd(z, dict(mp, bg=bg_after), f"mixer_fwd_{l}")
        (up,) = _exchange_wait(gstates[l, 2], False, ycat, f"gather_wait_up_{l}")
        if l == 0:
            (gstates[0, 3],), tok = _exchange_start(wgroups(0)[3:4], False, up, "gather_start_down_0")
            g1_after = g1 + tok[0:1, 0:1]
        o, x1 = _matmul_norm_resid(ycat, w_out, xl, g1_after, gpost1, f"mix_out_{l}")
        if l + 1 < nl:
            (gstates[l + 1, 0], gstates[l + 1, 1]), tok = _exchange_start(wgroups(l + 1)[0:2], False, x1,
                                                                        f"gather_start_in_{l + 1}")
            sh2_after = sh2 + tok[0:1, 0:1]
        p, h2 = _norm_mod_matmul(x1, gpre2, 1.0 + sc2, sh2_after, up, f"ffn_in_{l}")
        if l + 1 < nl:
            nxt = _exchange_wait(gstates[l + 1, 0], False, p, f"gather_wait_in_{l + 1}")
            nxt_out = _exchange_wait(gstates[l + 1, 1], False, nxt[0], f"gather_wait_out_{l + 1}")
            arrived[l + 1] = list(nxt) + list(nxt_out)
            (gstates[l + 1, 2], gstates[l + 1, 3]), tok = _exchange_start(wgroups(l + 1)[2:4], False, nxt_out[0],
                                                                        f"gather_start_up_{l + 1}")
            fcb_after = fcb + tok[0:1, 0:1]
        act = _ffn_act_fwd(p, ffn_cw, fcb_after, f"ffn_act_{l}")
        (g_down,) = _exchange_wait(gstates[l, 3], False, act, f"gather_wait_down_{l}")
        down = g_down.reshape(f, d)
        qo, x2 = _matmul_norm_resid(act, down, x1, g2, gpost2, f"ffn_out_{l}")
        saved.append(dict(x=xl, z=z, h1=h1, ycat=ycat, cbo=cbo, dpool=dpool, o=o, x1=x1, p=p, h2=h2, act=act, qo=qo, mp=mp, fcb=fcb,
                          w_in=w_in, w_out=w_out, up=up, down=down, ffn_cw=ffn_cw,
                          mods=(sh1, sc1, g1, sh2, sc2, g2), gains=(gpre1, gpost1, gpre2, gpost2)))
        xl = x2

    dx, loss_row = _loss_head(xl, tgt)

    small = {n: [None] * nl for n in SMALL_NAMES + SHARDED_SMALL}
    dmods = [None] * nl
    tn = f2 // N_CHIPS
    sstates = {}
    token = None
    for l in reversed(range(nl)):
        sv = saved[l]
        sh1, sc1, g1, sh2, sc2, g2 = sv["mods"]
        gpre1, gpost1, gpre2, gpost2 = sv["gains"]
        if token is not None:
            g2 = g2 + token[0:1, 0:1]
        dq, dact, dg2, dgpost2 = _resid_bwd_matmul(dx, sv["qo"], g2, gpost2, sv["down"], f"ffn_out_bwd_{l}")
        g_down = _wgrad(sv["act"], dq, (f, lambda j: 0), (d, lambda j: 0), jax.ShapeDtypeStruct((f, d), BF16),
                        (1, lambda j: (0, 0)), (f, d), f"wgrad_ffn_down_{l}")
        dug, duv, dfwg, dfwv, dfbg, dfbv = _ffn_act_bwd(sv["p"], dact, sv["ffn_cw"], sv["fcb"], f"ffn_act_bwd_{l}")
        dx1, dp, dsh2, dsc2, dgpre2 = _ffn_in_bwd(dug, duv, sv["ffn_cw"], sv["up"], sv["x1"], dx, gpre2, 1.0 + sc2,
                                                  f"ffn_in_bwd_{l}")
        g_up = _wgrad(sv["h2"], dp, (d, lambda j: 0), (tn, lambda j: j), jax.ShapeDtypeStruct((N_CHIPS, d, tn), BF16),
                      (N_CHIPS, lambda j: (j, 0, 0)), (None, d, tn), f"wgrad_ffn_up_{l}")
        (sstates[l, 0],), token = _exchange_start([[g_down.reshape(N_CHIPS, f // N_CHIPS, d), g_up]], True, g_up,
                                                  f"scatter_start_ffn_{l}")
        do, dycat, dg1, dgpost1 = _resid_bwd_matmul(dx1, sv["o"], g1 + token[0:1, 0:1], gpost1, sv["w_out"],
                                                    f"mix_out_bwd_{l}")
        g_out = _wgrad(sv["ycat"], do, (d, lambda j: 0), (d, lambda j: 0), jax.ShapeDtypeStruct((d, d), BF16),
                       (1, lambda j: (0, 0)), (d, d), f"wgrad_w_out_{l}")
        (sstates[l, 1],), token = _exchange_start([[g_out.reshape(N_CHIPS, d // N_CHIPS, d)]], True, g_out,
                                                  f"scatter_start_out_{l}")
        mp_after = dict(sv["mp"], bg=sv["mp"]["bg"] + token[0:1, 0:1])
        (dza, dcb, dd, dbg, dwm, dbias, dng, dnb, dcng, dcnb, dps, dpw) = _mixer_bwd_a(sv["z"], sv["cbo"], sv["dpool"], dycat, mp_after, f"mixer_bwd_a_{l}")
        dx, dz, dsh1, dsc1, dgpre1, dcw, dcbias = _mixer_bwd_b(
            sv["z"], dza, dcb, dd, sv["x"], dx1, gpre1, 1.0 + sc1, sv["mp"]["cw"], sv["w_in"], f"mixer_bwd_b_{l}")
        g_in = _wgrad(dz, sv["h1"], (inw, lambda j: 0), (d, lambda j: 0), jax.ShapeDtypeStruct((inw, d), BF16),
                      (1, lambda j: (0, 0)), (inw, d), f"wgrad_w_in_{l}")
        g_in_parts = g_in.reshape(N_CHIPS, kin, d)
        if l > 0:
            (sstates[l, 2],), token = _exchange_start([[g_in_parts]], True, g_in_parts, f"scatter_start_in_{l}")

        dmods[l] = jnp.concatenate([dsh1, dsc1, dg1, dsh2, dsc2, dg2], axis=0)
        small["mix_pre_g"][l], small["mix_post_g"][l] = dgpre1[0], dgpost1[0]
        small["ffn_pre_g"][l], small["ffn_post_g"][l] = dgpre2[0], dgpost2[0]
        small["sgu_norm_g"][l], small["sgu_norm_b"][l] = dng[0], dnb[0]
        small["sgu_w"][l] = jnp.where(tril[None], dwm, 0.0)
        small["sgu_b"][l] = dbias.reshape(CHUNK, heads, HEAD_DIM).sum(-1).T
        small["conv_b"][l], small["conv_norm_g"][l], small["conv_norm_b"][l] = dcbias[0], dcng[0], dcnb[0]
        small["pool_w"][l], small["pool_scale"][l], small["branch_g"][l] = _diag_blocks(dpw, groups), dps[0], dbg[0]
        small["ffn_conv_b"][l] = jnp.concatenate([dfbg[0], dfbv[0]])
        small["conv_w"][l] = dcw
        small["ffn_conv_w"][l] = jnp.concatenate([dfwg, dfwv], axis=1)

    names = [n for n in SMALL_NAMES if n != "mod_b"] + list(SHARDED_SMALL)
    dmod_rows, _ = _pack([jnp.stack(dmods)])
    packed, layout = _pack([jnp.stack(dmods), loss_row] + [jnp.stack(small[n]) for n in names], 8 * N_DEV)
    gathered, summed = _allreduce_small(dmod_rows, packed)
    (sstates[0, 2],), token = _exchange_start([[g_in_parts]], True, summed, "scatter_start_in_0")
    parts = _unpack(summed, layout)
    loss = parts[1][0, 0]
    gsmall = dict(zip(names, parts[2:]))
    gsmall["mod_b"] = parts[0].reshape(nl, N_MOD * d)
    dmod_all = gathered[:, :nl * N_MOD].reshape(N_DEV, nl, N_MOD * d)
    dmod_mine = jnp.transpose(lax.dynamic_slice_in_dim(dmod_all, q * nmod, nmod, axis=2), (1, 0, 2))

    grads, deltas, new_m, new_v = {}, {}, {}, {}

    def put(name, res):
        grads[name], deltas[name], new_m[name], new_v[name] = res

    recv = dict(w_in=[None] * nl, w_out=[None] * nl, ffn_up=[None] * nl, ffn_down=[None] * nl)
    done = token
    for l in reversed(range(nl)):
        recv["ffn_down"][l], recv["ffn_up"][l] = _exchange_wait(sstates[l, 0], True, done, f"scatter_wait_ffn_{l}")
        (recv["w_out"][l],) = _exchange_wait(sstates[l, 1], True, recv["ffn_up"][l], f"scatter_wait_out_{l}")
        done = recv["w_out"][l]
    big = ("w_out", "ffn_up", "ffn_down", "w_in")
    mine = {n: _reduce4(recv[n], f"reduce4_{n}") for n in big[:3]}
    swap_a, token = _swap_start([mine[n] for n in big[:3]], mine["ffn_down"], "swap_start_a")

    put("mod_w", _modw_adamw(sc_all.T + token[0:1, 0:1], dmod_mine, w["mod_w"], m["mod_w"], v["mod_w"], "adamw_mod_w"))
    done = grads["mod_w"]
    for l in reversed(range(nl)):
        (recv["w_in"][l],) = _exchange_wait(sstates[l, 2], True, done, f"scatter_wait_in_{l}")
        done = recv["w_in"][l]
    mine["w_in"] = _reduce4(recv["w_in"], "reduce4_w_in")
    swap_b, token = _swap_start([mine["w_in"]], mine["w_in"], "swap_start_b")

    gsmall["conv_w"] = lax.dynamic_slice_in_dim(gsmall["conv_w"], q * (CONV_WIDTH // N_CHIPS), CONV_WIDTH // N_CHIPS, axis=2)
    gsmall["ffn_conv_w"] = lax.dynamic_slice_in_dim(gsmall["ffn_conv_w"], q * (f2 // N_CHIPS), f2 // N_CHIPS, axis=2)
    snames = SMALL_NAMES + SHARDED_SMALL
    res = _adamw_many([w[n] for n in snames], [m[n] for n in snames], [v[n] for n in snames], [gsmall[n] for n in snames],
                      "adamw_small")
    for n, (d_, m_, v_) in zip(snames, res):
        put(n, (gsmall[n], d_, m_, v_))

    sent, theirs = _swap_wait(swap_a, deltas["mod_b"], "swap_wait_a")
    sent_b, theirs_b = _swap_wait(swap_b, sent[0], "swap_wait_b")
    for n, a, b in zip(big, list(sent) + list(sent_b), list(theirs) + list(theirs_b)):
        if n == "w_in":
            res = _adamw(*[jnp.swapaxes(t[n], 1, 2) for t in (w, m, v)], [a, b], f"adamw_{n}")
            put(n, [jnp.swapaxes(r_, 1, 2) for r_ in res])
        else:
            put(n, _adamw(w[n], m[n], v[n], [a, b], f"adamw_{n}"))

    return (loss, dx.reshape(1, s, d), *[grads[n] for n in WEIGHT_ORDER], *[deltas[n] for n in WEIGHT_ORDER],
            *[new_m[n] for n in WEIGHT_ORDER], *[new_v[n] for n in WEIGHT_ORDER])


def kernel(x, c, mod_w, mod_b, mix_pre_g, mix_post_g, w_in, sgu_norm_g, sgu_norm_b, sgu_w, sgu_b, conv_w, conv_b, conv_norm_g, conv_norm_b, pool_w, pool_scale, branch_g, w_out, ffn_pre_g, ffn_post_g, ffn_up, ffn_conv_w, ffn_conv_b, ffn_down, loss_target, m_mod_w, m_mod_b, m_mix_pre_g, m_mix_post_g, m_w_in, m_sgu_norm_g, m_sgu_norm_b, m_sgu_w, m_sgu_b, m_conv_w, m_conv_b, m_conv_norm_g, m_conv_norm_b, m_pool_w, m_pool_scale, m_branch_g, m_w_out, m_ffn_pre_g, m_ffn_post_g, m_ffn_up, m_ffn_conv_w, m_ffn_conv_b, m_ffn_down, v_mod_w, v_mod_b, v_mix_pre_g, v_mix_post_g, v_w_in, v_sgu_norm_g, v_sgu_norm_b, v_sgu_w, v_sgu_b, v_conv_w, v_conv_b, v_conv_norm_g, v_conv_norm_b, v_pool_w, v_pool_scale, v_branch_g, v_w_out, v_ffn_pre_g, v_ffn_post_g, v_ffn_up, v_ffn_conv_w, v_ffn_conv_b, v_ffn_down):
    w = dict(mod_w=mod_w, mod_b=mod_b, mix_pre_g=mix_pre_g, mix_post_g=mix_post_g, w_in=w_in, sgu_norm_g=sgu_norm_g,
             sgu_norm_b=sgu_norm_b, sgu_w=sgu_w, sgu_b=sgu_b, conv_w=conv_w, conv_b=conv_b, conv_norm_g=conv_norm_g,
             conv_norm_b=conv_norm_b, pool_w=pool_w, pool_scale=pool_scale, branch_g=branch_g, w_out=w_out,
             ffn_pre_g=ffn_pre_g, ffn_post_g=ffn_post_g, ffn_up=ffn_up, ffn_conv_w=ffn_conv_w, ffn_conv_b=ffn_conv_b,
             ffn_down=ffn_down)
    m = dict(mod_w=m_mod_w, mod_b=m_mod_b, mix_pre_g=m_mix_pre_g, mix_post_g=m_mix_post_g, w_in=m_w_in,
             sgu_norm_g=m_sgu_norm_g, sgu_norm_b=m_sgu_norm_b, sgu_w=m_sgu_w, sgu_b=m_sgu_b, conv_w=m_conv_w,
             conv_b=m_conv_b, conv_norm_g=m_conv_norm_g, conv_norm_b=m_conv_norm_b, pool_w=m_pool_w,
             pool_scale=m_pool_scale, branch_g=m_branch_g, w_out=m_w_out, ffn_pre_g=m_ffn_pre_g, ffn_post_g=m_ffn_post_g,
             ffn_up=m_ffn_up, ffn_conv_w=m_ffn_conv_w, ffn_conv_b=m_ffn_conv_b, ffn_down=m_ffn_down)
    v = dict(mod_w=v_mod_w, mod_b=v_mod_b, mix_pre_g=v_mix_pre_g, mix_post_g=v_mix_post_g, w_in=v_w_in,
             sgu_norm_g=v_sgu_norm_g, sgu_norm_b=v_sgu_norm_b, sgu_w=v_sgu_w, sgu_b=v_sgu_b, conv_w=v_conv_w,
             conv_b=v_conv_b, conv_norm_g=v_conv_norm_g, conv_norm_b=v_conv_norm_b, pool_w=v_pool_w,
             pool_scale=v_pool_scale, branch_g=v_branch_g, w_out=v_w_out, ffn_pre_g=v_ffn_pre_g, ffn_post_g=v_ffn_post_g,
             ffn_up=v_ffn_up, ffn_conv_w=v_ffn_conv_w, ffn_conv_b=v_ffn_conv_b, ffn_down=v_ffn_down)
    return _step(x, c, loss_target, w, m, v)
```

```python
import functools
import math

import jax
import jax.numpy as jnp
from jax import lax
from jax.experimental import pallas as pl
from jax.experimental.pallas import tpu as pltpu

F32 = jnp.float32
BF16 = jnp.bfloat16
MESH = pl.DeviceIdType.MESH

EPS = 1e-6
HEAD_DIM = 64
CHUNK = 128
SGU_WIDTH = 384
CONV_WIDTH = 384
POOL_WIDTH = 256
POOL_WINDOWS = (2, 4, 8, 16)
CONV_K = 31
FFN_CONV_K = 3
N_MOD = 6
N_CHIPS = 4
N_DEV = 8

ADAM_LR = 0.001
ADAM_B1 = 0.9
ADAM_B2 = 0.999
ADAM_EPS = 1e-08
ADAM_WD = 0.01
ADAM_STEP = 10

MIX_HALO = 32
FFN_HALO = 8
FFN_ROWS = 16
FFN_UNROLL = 8
LANES = 128
CONV_ROWS = 32
SMALL_COLS = 1024
VMEM_BYTES_V7X = 64 * 1024 * 1024


def _vmem_limit(estimate_bytes):
    return int(min(max(estimate_bytes, 16 * 1024 * 1024), VMEM_BYTES_V7X - 8 * 1024 * 1024))


def _row_tile(s, want):
    return want if s % want == 0 else math.gcd(s, want)


def _rsum(v):
    return jnp.sum(v, axis=0, keepdims=True)


def _rmean(v):
    return jnp.mean(v, axis=-1, keepdims=True)


def _gelu(v):
    k = math.sqrt(2.0 / math.pi)
    return 0.5 * v * (1.0 + jnp.tanh(k * (v + 0.044715 * v * v * v)))


def _gelu_grad(v):
    k = math.sqrt(2.0 / math.pi)
    t = jnp.tanh(k * (v + 0.044715 * v * v * v))
    return 0.5 * (1.0 + t) + 0.5 * v * (1.0 - t * t) * (k * (1.0 + 3.0 * 0.044715 * v * v))


def _dot(a, b):
    return jnp.dot(a, b, preferred_element_type=F32)


def _dot_nt(a, b):
    return lax.dot_general(a, b, (((1,), (1,)), ((), ())), preferred_element_type=F32)


def _dot_tn(a, b):
    return lax.dot_general(a, b, (((0,), (0,)), ((), ())), preferred_element_type=F32)


def _group_mean(v, bd):
    hi = v.astype(BF16)
    lo = (v - hi.astype(F32)).astype(BF16)
    return (_dot(hi, bd) + _dot(lo, bd)) * (1.0 / HEAD_DIM)


def _const_spec(shape):
    nd = len(shape)
    return pl.BlockSpec(shape, lambda *_: (0,) * nd)


def _hbm_call(body, **kw):
    call = pl.pallas_call(body, **kw)
    return lambda *args: call(*[pltpu.with_memory_space_constraint(a, pltpu.HBM) for a in args])


def _norm_mod_matmul(x, gain, s1p, shift, w, name, transposed=False):
    s, d = x.shape
    nb = w.shape[0]
    tn = w.shape[1] if transposed else w.shape[2]
    ts = _row_tile(s, 512 if nb * tn <= 2048 else 256)

    def body(x_ref, g_ref, s_ref, b_ref, w_ref, z_ref, h_ref):
        xv = x_ref[...]
        r = lax.rsqrt(_rmean(xv * xv) + EPS)
        h = ((xv * r) * g_ref[...] * s_ref[...] + b_ref[...]).astype(BF16)
        h_ref[...] = h
        for j in range(nb):
            z_ref[:, j * tn:(j + 1) * tn] = _dot_nt(h, w_ref[j]) if transposed else _dot(h, w_ref[j])

    vec = pl.BlockSpec((1, d), lambda i: (0, 0))
    return _hbm_call(
        body, name=name,
        grid=(s // ts,),
        in_specs=[pl.BlockSpec((ts, d), lambda i: (i, 0)), vec, vec, vec,
                  pl.BlockSpec(w.shape, lambda i: (0, 0, 0), pipeline_mode=pl.Buffered(1))],
        out_specs=[pl.BlockSpec((ts, nb * tn), lambda i: (i, 0)), pl.BlockSpec((ts, d), lambda i: (i, 0))],
        out_shape=[jax.ShapeDtypeStruct((s, nb * tn), F32), jax.ShapeDtypeStruct((s, d), BF16)],
        compiler_params=pltpu.CompilerParams(
            dimension_semantics=("arbitrary",),
            vmem_limit_bytes=_vmem_limit(2 * (ts * d * 4 + ts * nb * tn * 4 + ts * d * 2) + nb * d * tn * 2 + 4 * ts * d * 4)),
    )(x, gain, s1p, shift, w)


def _matmul_norm_resid(a, w, xres, gate, gpost, name):
    s, k = a.shape
    d = w.shape[1]
    ts = _row_tile(s, 512)

    def body(a_ref, w_ref, x_ref, gate_ref, gp_ref, o_ref, xn_ref):
        o = _dot(a_ref[...], w_ref[...])
        o_ref[...] = o
        r = lax.rsqrt(_rmean(o * o) + EPS)
        xn_ref[...] = x_ref[...] + gate_ref[...] * ((o * r) * gp_ref[...])

    vec = pl.BlockSpec((1, d), lambda i: (0, 0))
    row = pl.BlockSpec((ts, d), lambda i: (i, 0))
    return _hbm_call(
        body, name=name,
        grid=(s // ts,),
        in_specs=[pl.BlockSpec((ts, k), lambda i: (i, 0)),
                  pl.BlockSpec((k, d), lambda i: (0, 0), pipeline_mode=pl.Buffered(1)), row, vec, vec],
        out_specs=[row, row],
        out_shape=[jax.ShapeDtypeStruct((s, d), F32)] * 2,
        compiler_params=pltpu.CompilerParams(
            dimension_semantics=("arbitrary",),
            vmem_limit_bytes=_vmem_limit(2 * (ts * k * 2 + 3 * ts * d * 4) + k * d * 2 + 4 * ts * d * 4)),
    )(a, w, xres, gate, gpost)


def _wgrad(a, b, acols, bcols, out_struct, out_index, out_block, name):
    s = a.shape[0]
    ts = _row_tile(s, 1024)
    aw, afn = acols
    bw, bfn = bcols
    nj = out_index[0]
    oidx = out_index[1]

    def body(a_ref, b_ref, o_ref, acc):
        i = pl.program_id(1)

        @pl.when(i == 0)
        def _():
            acc[...] = jnp.zeros_like(acc)

        acc[...] += _dot_tn(a_ref[...], b_ref[...])

        @pl.when(i == pl.num_programs(1) - 1)
        def _():
            o_ref[...] = acc[...].astype(o_ref.dtype)

    return _hbm_call(
        body, name=name,
        grid=(nj, s // ts),
        in_specs=[pl.BlockSpec((ts, aw), lambda j, i: (i, afn(j))), pl.BlockSpec((ts, bw), lambda j, i: (i, bfn(j)))],
        out_specs=pl.BlockSpec(out_block, lambda j, i: oidx(j)),
        out_shape=out_struct,
        scratch_shapes=[pltpu.VMEM((aw, bw), F32)],
        compiler_params=pltpu.CompilerParams(
            dimension_semantics=("arbitrary", "arbitrary"),
            vmem_limit_bytes=_vmem_limit(2 * (ts * aw * 2 + ts * bw * 2) + 3 * aw * bw * 4 + ts * aw * 4)),
    )(a, b)


def _loss_head(xo, tgt):
    s, d = xo.shape
    ts = _row_tile(s, 512)

    def body(x_ref, t_ref, dx_ref, l_ref, acc):
        i = pl.program_id(0)

        @pl.when(i == 0)
        def _():
            acc[...] = jnp.zeros_like(acc)

        e = x_ref[...] - t_ref[...]
        dx_ref[...] = e * (1.0 / d)
        acc[...] += _rsum(e * e)

        @pl.when(i == pl.num_programs(0) - 1)
        def _():
            tot = jnp.sum(acc[...], axis=-1, keepdims=True) * (0.5 / d)
            l_ref[...] = jnp.broadcast_to(tot, l_ref.shape)

    row = pl.BlockSpec((ts, d), lambda i: (i, 0))
    return _hbm_call(
        body, name="loss_head",
        grid=(s // ts,),
        in_specs=[row, row],
        out_specs=[row, pl.BlockSpec((1, SMALL_COLS), lambda i: (0, 0))],
        out_shape=[jax.ShapeDtypeStruct((s, d), F32), jax.ShapeDtypeStruct((1, SMALL_COLS), F32)],
        scratch_shapes=[pltpu.VMEM((1, d), F32)],
        compiler_params=pltpu.CompilerParams(dimension_semantics=("arbitrary",)),
    )(xo, tgt)


def _resid_bwd_matmul(dxn, o, gate, gpost, w, name):
    s, d = dxn.shape
    k = w.shape[0]
    ts = _row_tile(s, 512)

    def body(dx_ref, o_ref, gate_ref, gp_ref, w_ref, do_ref, da_ref, dgate_ref, dgp_ref):
        i = pl.program_id(0)

        @pl.when(i == 0)
        def _():
            dgate_ref[...] = jnp.zeros_like(dgate_ref)
            dgp_ref[...] = jnp.zeros_like(dgp_ref)

        dx = dx_ref[...]
        o = o_ref[...]
        r = lax.rsqrt(_rmean(o * o) + EPS)
        on = o * r
        dgate_ref[...] += _rsum(dx * (on * gp_ref[...]))
        don = dx * gate_ref[...]
        dgp_ref[...] += _rsum(don * on)
        t = don * gp_ref[...]
        do = (r * (t - on * _rmean(t * on))).astype(BF16)
        do_ref[...] = do
        da_ref[...] = _dot_nt(do, w_ref[...])

    vec = pl.BlockSpec((1, d), lambda i: (0, 0))
    row = pl.BlockSpec((ts, d), lambda i: (i, 0))
    return _hbm_call(
        body, name=name,
        grid=(s // ts,),
        in_specs=[row, row, vec, vec, pl.BlockSpec((k, d), lambda i: (0, 0), pipeline_mode=pl.Buffered(1))],
        out_specs=[row, pl.BlockSpec((ts, k), lambda i: (i, 0)), vec, vec],
        out_shape=[jax.ShapeDtypeStruct((s, d), BF16), jax.ShapeDtypeStruct((s, k), F32),
                   jax.ShapeDtypeStruct((1, d), F32), jax.ShapeDtypeStruct((1, d), F32)],
        compiler_params=pltpu.CompilerParams(
            dimension_semantics=("arbitrary",),
            vmem_limit_bytes=_vmem_limit(2 * (2 * ts * d * 4 + ts * d * 2 + ts * k * 4) + d * k * 2 + 6 * ts * d * 4)),
    )(dxn, o, gate, gpost, w)


def _norm_mod_bwd(dh, xv, gain, s1p, dres):
    r = lax.rsqrt(_rmean(xv * xv) + EPS)
    xn = xv * r
    dshift = _rsum(dh)
    t = dh * xn
    dscale = _rsum(t * gain)
    dgain = _rsum(t * s1p)
    dxn = dh * (gain * s1p)
    dx = r * (dxn - xn * _rmean(dxn * xn)) + dres
    return dx, dshift, dscale, dgain


def _lane_lt(shape, bound):
    return lax.broadcasted_iota(jnp.int32, shape, 1) < bound


def _sgu_forward(z_ref, bd_ref, ng_ref, nb_ref, wm_ref, bias_ref, ts, ya_s, f_s):
    u = _gelu(z_ref[:, 0:SGU_WIDTH])
    v = _gelu(z_ref[:, SGU_WIDTH:2 * SGU_WIDTH])
    bd = bd_ref[...]
    vc = v - _group_mean(v, bd)
    rstd = lax.rsqrt(_group_mean(vc * vc, bd) + EPS)
    vhat = vc * rstd
    vn = (vhat * ng_ref[...] + nb_ref[...]).astype(BF16)
    left = _lane_lt((CHUNK, CHUNK), HEAD_DIM)
    for n in range(ts // CHUNK):
        rows = slice(n * CHUNK, (n + 1) * CHUNK)
        for p in range(SGU_WIDTH // CHUNK):
            cols = slice(p * CHUNK, (p + 1) * CHUNK)
            blk = vn[rows, cols]
            f = jnp.where(left, _dot(wm_ref[2 * p], blk), _dot(wm_ref[2 * p + 1], blk)) + bias_ref[:, cols]
            if f_s is not None:
                f_s[rows, cols] = f
            ya_s[rows, cols] = u[rows, cols] * f
    return u, vhat, rstd, vn


def _shifted_copies(ext, ext8, ts):
    for b in range(1, 8):
        ext8[b - 1] = ext[pl.ds(b, ts + MIX_HALO - 8), :]


def _rows_at(ext, ext8, start, nrows):
    b = start % 8
    return ext[pl.ds(start, nrows), :] if b == 0 else ext8[b - 1, pl.ds(start - b, nrows), :]


def _conv31_forward(z_ref, zh_ref, first, cw_ref, cb_ref, ts, ext_b, ext8, cbs):
    a = z_ref[:, 2 * SGU_WIDTH:2 * SGU_WIDTH + CONV_WIDTH]
    g = z_ref[:, 2 * SGU_WIDTH + CONV_WIDTH:2 * SGU_WIDTH + 2 * CONV_WIDTH]
    ah = zh_ref[:, 2 * SGU_WIDTH:2 * SGU_WIDTH + CONV_WIDTH]
    gh = zh_ref[:, 2 * SGU_WIDTH + CONV_WIDTH:2 * SGU_WIDTH + 2 * CONV_WIDTH]
    ext_b[pl.ds(0, MIX_HALO), :] = jnp.where(first, 0.0, ah * jax.nn.sigmoid(gh))
    ext_b[pl.ds(MIX_HALO, ts), :] = a * jax.nn.sigmoid(g)
    _shifted_copies(ext_b, ext8, ts)
    for r in range(ts // CONV_ROWS):
        acc = jnp.broadcast_to(cb_ref[...], (CONV_ROWS, CONV_WIDTH))
        for k in range(CONV_K):
            acc = acc + cw_ref[k:k + 1, :] * _rows_at(ext_b, ext8, MIX_HALO - (CONV_K - 1) + k + r * CONV_ROWS, CONV_ROWS)
        cbs[pl.ds(r * CONV_ROWS, CONV_ROWS), :] = acc


def _pool_counts(i, ts):
    pos1 = (i * ts + 1 + lax.broadcasted_iota(jnp.int32, (ts, POOL_WIDTH), 0)).astype(F32)
    lane = lax.broadcasted_iota(jnp.int32, (ts, POOL_WIDTH), 1)
    gdim = POOL_WIDTH // len(POOL_WINDOWS)
    win = jnp.where(lane < gdim, float(POOL_WINDOWS[0]),
                    jnp.where(lane < 2 * gdim, float(POOL_WINDOWS[1]),
                              jnp.where(lane < 3 * gdim, float(POOL_WINDOWS[2]), float(POOL_WINDOWS[3]))))
    return jnp.minimum(pos1, win)


def _window_sums(ext, base, ts, sign):
    lane = lax.broadcasted_iota(jnp.int32, (ts, POOL_WIDTH), 1)
    gdim = POOL_WIDTH // len(POOL_WINDOWS)
    run = jnp.zeros((ts, POOL_WIDTH), F32)
    out = jnp.zeros((ts, POOL_WIDTH), F32)
    for m in range(POOL_WINDOWS[-1]):
        run = run + ext[pl.ds(base + sign * m, ts), :]
        for gi, win in enumerate(POOL_WINDOWS):
            if m == win - 1:
                out = jnp.where((lane >= gi * gdim) & (lane < (gi + 1) * gdim), run, out)
    return out


def _pool_forward(z_ref, zh_ref, first, i, ts, ext_c):
    c0 = 2 * SGU_WIDTH + 2 * CONV_WIDTH
    zc = z_ref[:, c0:c0 + POOL_WIDTH]
    ext_c[pl.ds(0, MIX_HALO), :] = jnp.where(first, 0.0, zh_ref[:, c0:c0 + POOL_WIDTH])
    ext_c[pl.ds(MIX_HALO, ts), :] = zc
    sums = _window_sums(ext_c, MIX_HALO, ts, -1)
    return sums / _pool_counts(i, ts) - zc


def _layer_norm_rows(v):
    mu = _rmean(v)
    vc = v - mu
    rstd = lax.rsqrt(_rmean(vc * vc) + EPS)
    return vc * rstd, rstd


def _mixer_specs(s, ts, width):
    nbh = ts // MIX_HALO
    tile = pl.BlockSpec((ts, width), lambda i: (i, 0))
    prev = pl.BlockSpec((MIX_HALO, width), lambda i: (jnp.maximum(i * nbh - 1, 0), 0))
    nxt = pl.BlockSpec((MIX_HALO, width), lambda i: (jnp.minimum((i + 1) * nbh, s // MIX_HALO - 1), 0))
    return tile, prev, nxt


def _mixer_fwd(z, mp, name):
    s, inw = z.shape
    d = SGU_WIDTH + CONV_WIDTH + POOL_WIDTH
    ts = _row_tile(s, 256)

    def body(z_ref, zh_ref, bd_ref, ng_ref, nb_ref, wm_ref, bias_ref, cw_ref, cb_ref, cng_ref, cnb_ref,
             pw_ref, ps_ref, bg_ref, y_ref, cbs, dpool_ref, ya_s, ext_b, ext_c, ext8):
        i = pl.program_id(0)
        first = i == 0
        _sgu_forward(z_ref, bd_ref, ng_ref, nb_ref, wm_ref, bias_ref, ts, ya_s, None)
        ya = ya_s[...]
        ra = lax.rsqrt(_rmean(ya * ya) + EPS)
        y_ref[:, 0:SGU_WIDTH] = ((ya * ra) * bg_ref[:, 0:SGU_WIDTH]).astype(BF16)

        _conv31_forward(z_ref, zh_ref, first, cw_ref, cb_ref, ts, ext_b, ext8, cbs)
        chat, _ = _layer_norm_rows(cbs[...])
        lin = chat * cng_ref[...] + cnb_ref[...]
        yb = lin * jax.nn.sigmoid(lin)
        rb = lax.rsqrt(_rmean(yb * yb) + EPS)
        y_ref[:, SGU_WIDTH:SGU_WIDTH + CONV_WIDTH] = ((yb * rb) * bg_ref[:, SGU_WIDTH:SGU_WIDTH + CONV_WIDTH]).astype(BF16)

        dpool = _pool_forward(z_ref, zh_ref, first, i, ts, ext_c)
        dpool_ref[...] = dpool
        yc = _dot(dpool.astype(BF16), pw_ref[...]) * ps_ref[...]
        rc = lax.rsqrt(_rmean(yc * yc) + EPS)
        y_ref[:, SGU_WIDTH + CONV_WIDTH:d] = ((yc * rc) * bg_ref[:, SGU_WIDTH + CONV_WIDTH:d]).astype(BF16)

    tile, prev, _ = _mixer_specs(s, ts, inw)
    consts = [mp["bd"], mp["ng"], mp["nb"], mp["wm"], mp["bias"], mp["cw"], mp["cb"], mp["cng"], mp["cnb"],
              mp["pw"], mp["ps"], mp["bg"]]
    return _hbm_call(
        body, name=name,
        grid=(s // ts,),
        in_specs=[tile, prev] + [_const_spec(c.shape) for c in consts],
        out_specs=[pl.BlockSpec((ts, d), lambda i: (i, 0)), pl.BlockSpec((ts, CONV_WIDTH), lambda i: (i, 0)),
                   pl.BlockSpec((ts, POOL_WIDTH), lambda i: (i, 0))],
        out_shape=[jax.ShapeDtypeStruct((s, d), BF16), jax.ShapeDtypeStruct((s, CONV_WIDTH), F32),
                   jax.ShapeDtypeStruct((s, POOL_WIDTH), F32)],
        scratch_shapes=[pltpu.VMEM((ts, SGU_WIDTH), F32), pltpu.VMEM((ts + MIX_HALO, CONV_WIDTH), F32),
                        pltpu.VMEM((ts + MIX_HALO, POOL_WIDTH), F32), pltpu.VMEM((7, ts + MIX_HALO - 8, CONV_WIDTH), F32)],
        compiler_params=pltpu.CompilerParams(dimension_semantics=("arbitrary",),
                                             vmem_limit_bytes=_vmem_limit(16 * ts * inw * 4)),
    )(z, z, *consts)


def _mixer_bwd_a(z, cb, dpool, dy, mp, name):
    s, inw = z.shape
    d = SGU_WIDTH + CONV_WIDTH + POOL_WIDTH
    ts = _row_tile(s, 256)
    nchunk = ts // CHUNK

    def rms_bwd(dyn, y, g):
        r = lax.rsqrt(_rmean(y * y) + EPS)
        yn = y * r
        dg = _rsum(dyn * yn)
        t = dyn * g
        return r * (t - yn * _rmean(t * yn)), dg

    def body(z_ref, cbs, dpool_ref, dy_ref, bd_ref, ng_ref, nb_ref, wm_ref, wmt_ref, bias_ref, cng_ref, cnb_ref,
             pw_ref, pwt_ref, ps_ref, bg_ref,
             dza_ref, dcb_ref, dd_ref, dbg_ref, dwm_ref, dbias_ref, dng_ref, dnb_ref, dcng_ref, dcnb_ref, dps_ref, dpw_ref,
             ya_s, f_s, dvn_s):
        i = pl.program_id(0)
        first = i == 0

        @pl.when(first)
        def _():
            for ref in (dwm_ref, dbias_ref, dng_ref, dnb_ref, dcng_ref, dcnb_ref, dps_ref, dpw_ref):
                ref[...] = jnp.zeros_like(ref)

        u, vhat, rstd, vn = _sgu_forward(z_ref, bd_ref, ng_ref, nb_ref, wm_ref, bias_ref, ts, ya_s, f_s)
        dya, dbg_a = rms_bwd(dy_ref[:, 0:SGU_WIDTH], ya_s[...], bg_ref[:, 0:SGU_WIDTH])
        du = dya * f_s[...]
        df = dya * u
        dfb = df.astype(BF16)
        left = _lane_lt((CHUNK, CHUNK), HEAD_DIM)
        zero = jnp.zeros((CHUNK, CHUNK), BF16)
        dbias = jnp.zeros((CHUNK, SGU_WIDTH), F32)
        for n in range(nchunk):
            rows = slice(n * CHUNK, (n + 1) * CHUNK)
            dbias = dbias + df[rows, :]
            for p in range(SGU_WIDTH // CHUNK):
                cols = slice(p * CHUNK, (p + 1) * CHUNK)
                dblk = dfb[rows, cols]
                vblk = vn[rows, cols]
                dwm_ref[2 * p] += _dot_nt(jnp.where(left, dblk, zero), vblk)
                dwm_ref[2 * p + 1] += _dot_nt(jnp.where(left, zero, dblk), vblk)
                dvn_s[rows, cols] = jnp.where(left, _dot(wmt_ref[2 * p], dblk), _dot(wmt_ref[2 * p + 1], dblk))
        dbias_ref[...] += dbias
        dvn = dvn_s[...]
        dng_ref[...] += _rsum(dvn * vhat)
        dnb_ref[...] += _rsum(dvn)
        dvh = dvn * ng_ref[...]
        bd = bd_ref[...]
        dv = rstd * (dvh - _group_mean(dvh, bd) - vhat * _group_mean(dvh * vhat, bd))
        dza_ref[:, 0:SGU_WIDTH] = (du * _gelu_grad(z_ref[:, 0:SGU_WIDTH])).astype(BF16)
        dza_ref[:, SGU_WIDTH:2 * SGU_WIDTH] = (dv * _gelu_grad(z_ref[:, SGU_WIDTH:2 * SGU_WIDTH])).astype(BF16)

        chat, crstd = _layer_norm_rows(cbs[...])
        lin = chat * cng_ref[...] + cnb_ref[...]
        sl = jax.nn.sigmoid(lin)
        dyb, dbg_b = rms_bwd(dy_ref[:, SGU_WIDTH:SGU_WIDTH + CONV_WIDTH], lin * sl, bg_ref[:, SGU_WIDTH:SGU_WIDTH + CONV_WIDTH])
        dlin = dyb * (sl * (1.0 + lin * (1.0 - sl)))
        dcng_ref[...] += _rsum(dlin * chat)
        dcnb_ref[...] += _rsum(dlin)
        dch = dlin * cng_ref[...]
        dcb_ref[...] = crstd * (dch - _rmean(dch) - chat * _rmean(dch * chat))

        dpb = dpool_ref[...].astype(BF16)
        ycp = _dot(dpb, pw_ref[...])
        dyc, dbg_c = rms_bwd(dy_ref[:, SGU_WIDTH + CONV_WIDTH:d], ycp * ps_ref[...], bg_ref[:, SGU_WIDTH + CONV_WIDTH:d])
        dps_ref[...] += _rsum(dyc * ycp)
        dycp = (dyc * ps_ref[...]).astype(BF16)
        dpw_ref[...] += _dot_tn(dpb, dycp)
        dd_ref[...] = _dot(dycp, pwt_ref[...])

        @pl.when(first)
        def _():
            dbg_ref[...] = jnp.zeros_like(dbg_ref)

        dbg_ref[:, 0:SGU_WIDTH] += dbg_a
        dbg_ref[:, SGU_WIDTH:SGU_WIDTH + CONV_WIDTH] += dbg_b
        dbg_ref[:, SGU_WIDTH + CONV_WIDTH:d] += dbg_c

    tile, _, _ = _mixer_specs(s, ts, inw)
    consts = [mp["bd"], mp["ng"], mp["nb"], mp["wm"], mp["wmt"], mp["bias"], mp["cng"], mp["cnb"],
              mp["pw"], mp["pwt"], mp["ps"], mp["bg"]]
    acc_shapes = [(1, d), (2 * (SGU_WIDTH // CHUNK), CHUNK, CHUNK), (CHUNK, SGU_WIDTH), (1, SGU_WIDTH), (1, SGU_WIDTH),
                  (1, CONV_WIDTH), (1, CONV_WIDTH), (1, POOL_WIDTH), (POOL_WIDTH, POOL_WIDTH)]
    return _hbm_call(
        body, name=name,
        grid=(s // ts,),
        in_specs=[tile, pl.BlockSpec((ts, CONV_WIDTH), lambda i: (i, 0)), pl.BlockSpec((ts, POOL_WIDTH), lambda i: (i, 0)),
                  pl.BlockSpec((ts, d), lambda i: (i, 0))] + [_const_spec(c.shape) for c in consts],
        out_specs=[pl.BlockSpec((ts, 2 * SGU_WIDTH), lambda i: (i, 0)), pl.BlockSpec((ts, CONV_WIDTH), lambda i: (i, 0)),
                   pl.BlockSpec((ts, POOL_WIDTH), lambda i: (i, 0))] + [_const_spec(a) for a in acc_shapes],
        out_shape=[jax.ShapeDtypeStruct((s, 2 * SGU_WIDTH), BF16), jax.ShapeDtypeStruct((s, CONV_WIDTH), F32),
                   jax.ShapeDtypeStruct((s, POOL_WIDTH), F32)] + [jax.ShapeDtypeStruct(a, F32) for a in acc_shapes],
        scratch_shapes=[pltpu.VMEM((ts, SGU_WIDTH), F32), pltpu.VMEM((ts, SGU_WIDTH), F32), pltpu.VMEM((ts, SGU_WIDTH), F32)],
        compiler_params=pltpu.CompilerParams(dimension_semantics=("arbitrary",),
                                             vmem_limit_bytes=_vmem_limit(24 * ts * inw * 4)),
    )(z, cb, dpool, dy, *consts)


def _mixer_bwd_b(z, dza, dcb, dd, x, dres, gain, s1p, cw, w, name):
    s, inw = z.shape
    d = x.shape[1]
    ts = _row_tile(s, 256)
    c0 = 2 * SGU_WIDTH
    c1 = c0 + 2 * CONV_WIDTH

    def body(z_ref, zh_ref, dza_ref, dcb_ref, dcbn_ref, dd_ref, ddn_ref, x_ref, dres_ref, g_ref, s_ref, cw_ref, w_ref,
             dx_ref, dz_ref, dsh_ref, dsc_ref, dg_ref, dcw_ref, dcbias_ref, ext_b, ext_n, ext_e, ext8):
        i = pl.program_id(0)
        first = i == 0
        last = i == pl.num_programs(0) - 1

        @pl.when(first)
        def _():
            for ref in (dsh_ref, dsc_ref, dg_ref, dcw_ref, dcbias_ref):
                ref[...] = jnp.zeros_like(ref)

        a = z_ref[:, c0:c0 + CONV_WIDTH]
        sg = jax.nn.sigmoid(z_ref[:, c0 + CONV_WIDTH:c1])
        ah = zh_ref[:, c0:c0 + CONV_WIDTH]
        gh = zh_ref[:, c0 + CONV_WIDTH:c1]
        ext_b[pl.ds(0, MIX_HALO), :] = jnp.where(first, 0.0, ah * jax.nn.sigmoid(gh))
        ext_b[pl.ds(MIX_HALO, ts), :] = a * sg
        _shifted_copies(ext_b, ext8, ts)
        dcbv = dcb_ref[...]
        dcbias_ref[...] += _rsum(dcbv)
        for k in range(CONV_K):
            dcw_ref[k:k + 1, :] += _rsum(dcbv * _rows_at(ext_b, ext8, MIX_HALO - (CONV_K - 1) + k, ts))

        ext_n[pl.ds(0, ts), :] = dcbv
        ext_n[pl.ds(ts, MIX_HALO), :] = jnp.where(last, 0.0, dcbn_ref[...])
        _shifted_copies(ext_n, ext8, ts)
        for r in range(ts // CONV_ROWS):
            acc = jnp.zeros((CONV_ROWS, CONV_WIDTH), F32)
            for k in range(CONV_K):
                acc = acc + cw_ref[k:k + 1, :] * _rows_at(ext_n, ext8, CONV_K - 1 - k + r * CONV_ROWS, CONV_ROWS)
            rows = pl.ds(r * CONV_ROWS, CONV_ROWS)
            ar = z_ref[rows, c0:c0 + CONV_WIDTH]
            sr = jax.nn.sigmoid(z_ref[rows, c0 + CONV_WIDTH:c1])
            dz_ref[rows, c0:c0 + CONV_WIDTH] = (acc * sr).astype(BF16)
            dz_ref[rows, c0 + CONV_WIDTH:c1] = (acc * ar * sr * (1.0 - sr)).astype(BF16)

        ddv = dd_ref[...]
        ext_e[pl.ds(0, ts), :] = ddv / _pool_counts(i, ts)
        nh = (i + 1) * ts + lax.broadcasted_iota(jnp.int32, (MIX_HALO, POOL_WIDTH), 0)
        lane = lax.broadcasted_iota(jnp.int32, (MIX_HALO, POOL_WIDTH), 1)
        gdim = POOL_WIDTH // len(POOL_WINDOWS)
        winh = jnp.where(lane < gdim, float(POOL_WINDOWS[0]),
                         jnp.where(lane < 2 * gdim, float(POOL_WINDOWS[1]),
                                   jnp.where(lane < 3 * gdim, float(POOL_WINDOWS[2]), float(POOL_WINDOWS[3]))))
        cnth = jnp.minimum((nh + 1).astype(F32), winh)
        ext_e[pl.ds(ts, MIX_HALO), :] = jnp.where(last, 0.0, ddn_ref[...] / cnth)
        dz_ref[:, c1:inw] = (_window_sums(ext_e, 0, ts, 1) - ddv).astype(BF16)
        dz_ref[:, 0:c0] = dza_ref[...]

        dh = _dot(dz_ref[...], w_ref[...])
        dx, dsh, dsc, dg = _norm_mod_bwd(dh, x_ref[...], g_ref[...], s_ref[...], dres_ref[...])
        dx_ref[...] = dx
        dsh_ref[...] += dsh
        dsc_ref[...] += dsc
        dg_ref[...] += dg

    tile, prev, _ = _mixer_specs(s, ts, inw)
    _, _, nxt_b = _mixer_specs(s, ts, CONV_WIDTH)
    _, _, nxt_c = _mixer_specs(s, ts, POOL_WIDTH)
    row = pl.BlockSpec((ts, d), lambda i: (i, 0))
    vec = pl.BlockSpec((1, d), lambda i: (0, 0))
    return _hbm_call(
        body, name=name,
        grid=(s // ts,),
        in_specs=[tile, prev, pl.BlockSpec((ts, c0), lambda i: (i, 0)),
                  pl.BlockSpec((ts, CONV_WIDTH), lambda i: (i, 0)), nxt_b,
                  pl.BlockSpec((ts, POOL_WIDTH), lambda i: (i, 0)), nxt_c,
                  row, row, vec, vec, _const_spec(cw.shape),
                  pl.BlockSpec(w.shape, lambda i: (0, 0), pipeline_mode=pl.Buffered(1))],
        out_specs=[row, pl.BlockSpec((ts, inw), lambda i: (i, 0)), vec, vec, vec,
                   _const_spec((CONV_K, CONV_WIDTH)), _const_spec((1, CONV_WIDTH))],
        out_shape=[jax.ShapeDtypeStruct((s, d), F32), jax.ShapeDtypeStruct((s, inw), BF16)]
        + [jax.ShapeDtypeStruct((1, d), F32)] * 3
        + [jax.ShapeDtypeStruct((CONV_K, CONV_WIDTH), F32), jax.ShapeDtypeStruct((1, CONV_WIDTH), F32)],
        scratch_shapes=[pltpu.VMEM((ts + MIX_HALO, CONV_WIDTH), F32), pltpu.VMEM((ts + MIX_HALO, CONV_WIDTH), F32),
                        pltpu.VMEM((ts + MIX_HALO, POOL_WIDTH), F32), pltpu.VMEM((7, ts + MIX_HALO - 8, CONV_WIDTH), F32)],
        compiler_params=pltpu.CompilerParams(dimension_semantics=("arbitrary",),
                                             vmem_limit_bytes=_vmem_limit(18 * ts * inw * 4 + inw * d * 2)),
    )(z, z, dza, dcb, dcb, dd, dd, x, dres, gain, s1p, cw, w)


def _ffn_specs(s, ts, tc, half_blocks):
    nbh = ts // FFN_HALO

    def tile(off):
        return pl.BlockSpec((ts, tc), lambda j, i: (i, j + off))

    def prev(off):
        return pl.BlockSpec((FFN_HALO, tc), lambda j, i: (jnp.maximum(i * nbh - 1, 0), j + off))

    def vec(rows, off):
        return pl.BlockSpec((rows, tc), lambda j, i: (0, j + off))

    return tile, prev, vec


def _rows_before(cur, prev, k):
    row = lax.broadcasted_iota(jnp.int32, cur.shape, 0)
    return jnp.where(row >= k, pltpu.roll(cur, k, 0), pltpu.roll(prev, k, 0))


def _conv3_rows(cur, prev, w_ref, b_ref, cols):
    x1 = _rows_before(cur, prev, 1)
    x2 = _rows_before(cur, prev, 2)
    u = b_ref[:, cols] + w_ref[2:3, cols] * cur + w_ref[1:2, cols] * x1 + w_ref[0:1, cols] * x2
    return u, x2, x1


def _halo_chunk(h_ref, cols, first):
    h = jnp.where(first, 0.0, h_ref[:, cols])
    return jnp.concatenate([h] * (FFN_ROWS // FFN_HALO), axis=0)


def _ffn_act_fwd(p, cw, cb, name):
    s, f2 = p.shape
    f = f2 // 2
    tc = f // 2
    hb = f // tc
    ts = _row_tile(s, 256)

    def body(pg_ref, pgh_ref, pv_ref, pvh_ref, wg_ref, wv_ref, bg_ref, bv_ref, act_ref):
        first = pl.program_id(1) == 0
        for c in range(tc // LANES):
            cols = slice(c * LANES, (c + 1) * LANES)

            def chunk(r, carry, cols=cols):
                pg_prev, pv_prev = carry
                rows = pl.ds(pl.multiple_of(r * FFN_ROWS, FFN_ROWS), FFN_ROWS)
                pg = pg_ref[rows, cols]
                pv = pv_ref[rows, cols]
                ug, _, _ = _conv3_rows(pg, pg_prev, wg_ref, bg_ref, cols)
                uv, _, _ = _conv3_rows(pv, pv_prev, wv_ref, bv_ref, cols)
                act_ref[rows, cols] = (_gelu(ug) * uv).astype(BF16)
                return pg, pv

            def step(r, carry, chunk=chunk):
                for u in range(FFN_UNROLL):
                    carry = chunk(r * FFN_UNROLL + u, carry)
                return carry

            lax.fori_loop(0, ts // (FFN_ROWS * FFN_UNROLL), step,
                          (_halo_chunk(pgh_ref, cols, first), _halo_chunk(pvh_ref, cols, first)))

    tile, prev, vec = _ffn_specs(s, ts, tc, hb)
    return _hbm_call(
        body, name=name,
        grid=(hb, s // ts),
        in_specs=[tile(0), prev(0), tile(hb), prev(hb), vec(FFN_CONV_K, 0), vec(FFN_CONV_K, hb), vec(1, 0), vec(1, hb)],
        out_specs=pl.BlockSpec((ts, tc), lambda j, i: (i, j)),
        out_shape=jax.ShapeDtypeStruct((s, f), BF16),
        compiler_params=pltpu.CompilerParams(dimension_semantics=("arbitrary", "arbitrary"),
                                             vmem_limit_bytes=_vmem_limit(8 * ts * tc * 4)),
    )(p, p, p, p, cw, cw, cb, cb)


def _ffn_act_bwd(p, dact, cw, cb, name):
    s, f2 = p.shape
    f = f2 // 2
    tc = f // 2
    hb = f // tc
    ts = _row_tile(s, 256)

    def body(pg_ref, pgh_ref, pv_ref, pvh_ref, da_ref, wg_ref, wv_ref, bg_ref, bv_ref,
             dug_ref, duv_ref, dwg_ref, dwv_ref, dbg_ref, dbv_ref):
        first = pl.program_id(1) == 0

        @pl.when(first)
        def _():
            for ref in (dwg_ref, dwv_ref, dbg_ref, dbv_ref):
                ref[...] = jnp.zeros_like(ref)

        zero = jnp.zeros((FFN_ROWS, LANES), F32)
        for c in range(tc // LANES):
            cols = slice(c * LANES, (c + 1) * LANES)

            def chunk(r, carry, cols=cols):
                pg_prev, pv_prev, ag0, ag1, ag2, av0, av1, av2, sg, sv = carry
                rows = pl.ds(pl.multiple_of(r * FFN_ROWS, FFN_ROWS), FFN_ROWS)
                pg = pg_ref[rows, cols]
                pv = pv_ref[rows, cols]
                ug, pg2, pg1 = _conv3_rows(pg, pg_prev, wg_ref, bg_ref, cols)
                uv, pv2, pv1 = _conv3_rows(pv, pv_prev, wv_ref, bv_ref, cols)
                da = da_ref[rows, cols]
                dug = da * uv * _gelu_grad(ug)
                duv = da * _gelu(ug)
                dug_ref[rows, cols] = dug
                duv_ref[rows, cols] = duv
                return (pg, pv, ag0 + dug * pg2, ag1 + dug * pg1, ag2 + dug * pg,
                        av0 + duv * pv2, av1 + duv * pv1, av2 + duv * pv, sg + dug, sv + duv)

            def step(r, carry, chunk=chunk):
                for u in range(FFN_UNROLL):
                    carry = chunk(r * FFN_UNROLL + u, carry)
                return carry

            out = lax.fori_loop(0, ts // (FFN_ROWS * FFN_UNROLL), step,
                                (_halo_chunk(pgh_ref, cols, first), _halo_chunk(pvh_ref, cols, first)) + (zero,) * 8)
            for k in range(FFN_CONV_K):
                dwg_ref[k:k + 1, cols] += _rsum(out[2 + k])
                dwv_ref[k:k + 1, cols] += _rsum(out[5 + k])
            dbg_ref[:, cols] += _rsum(out[8])
            dbv_ref[:, cols] += _rsum(out[9])

    tile, prev, vec = _ffn_specs(s, ts, tc, hb)
    half = pl.BlockSpec((ts, tc), lambda j, i: (i, j))
    wacc = pl.BlockSpec((FFN_CONV_K, tc), lambda j, i: (0, j))
    bacc = pl.BlockSpec((1, tc), lambda j, i: (0, j))
    return _hbm_call(
        body, name=name,
        grid=(hb, s // ts),
        in_specs=[tile(0), prev(0), tile(hb), prev(hb), half, vec(FFN_CONV_K, 0), vec(FFN_CONV_K, hb), vec(1, 0), vec(1, hb)],
        out_specs=[half, half, wacc, wacc, bacc, bacc],
        out_shape=[jax.ShapeDtypeStruct((s, f), F32)] * 2 + [jax.ShapeDtypeStruct((FFN_CONV_K, f), F32)] * 2
        + [jax.ShapeDtypeStruct((1, f), F32)] * 2,
        compiler_params=pltpu.CompilerParams(dimension_semantics=("arbitrary", "arbitrary"),
                                             vmem_limit_bytes=_vmem_limit(12 * ts * tc * 4)),
    )(p, p, p, p, dact, cw, cw, cb, cb)


def _ffn_in_bwd(dug, duv, cw, w, x, dres, gain, s1p, name):
    s, f = dug.shape
    d = x.shape[1]
    ts = _row_tile(s, 256)
    tc = w.shape[2]
    assert f % tc == 0 and w.shape[0] * tc == 2 * f
    nbh = ts // FFN_HALO

    def body(dug_ref, dugn_ref, duv_ref, duvn_ref, cw_ref, w_ref, x_ref, dres_ref, g_ref, s_ref,
             dx_ref, dp_ref, dsh_ref, dsc_ref, dg_ref, ext):
        i = pl.program_id(0)
        last = i == pl.num_programs(0) - 1

        @pl.when(i == 0)
        def _():
            for ref in (dsh_ref, dsc_ref, dg_ref):
                ref[...] = jnp.zeros_like(ref)

        dh = jnp.zeros((ts, d), F32)
        for half, (t_ref, n_ref) in enumerate(((dug_ref, dugn_ref), (duv_ref, duvn_ref))):
            for cb in range(f // tc):
                cols = slice(cb * tc, (cb + 1) * tc)
                wcols = slice(half * f + cb * tc, half * f + (cb + 1) * tc)
                ext[pl.ds(0, ts), :] = t_ref[:, cols]
                ext[pl.ds(ts, FFN_HALO), :] = jnp.where(last, 0.0, n_ref[:, cols])
                acc = cw_ref[FFN_CONV_K - 1:FFN_CONV_K, wcols] * t_ref[:, cols]
                for k in range(FFN_CONV_K - 1):
                    acc = acc + cw_ref[k:k + 1, wcols] * ext[pl.ds(FFN_CONV_K - 1 - k, ts), :]
                dpb = acc.astype(BF16)
                dp_ref[:, wcols] = dpb
                dh = dh + _dot_nt(dpb, w_ref[half * (f // tc) + cb])
        dx, dsh, dsc, dg = _norm_mod_bwd(dh, x_ref[...], g_ref[...], s_ref[...], dres_ref[...])
        dx_ref[...] = dx
        dsh_ref[...] += dsh
        dsc_ref[...] += dsc
        dg_ref[...] += dg

    tile = pl.BlockSpec((ts, f), lambda i: (i, 0))
    nxt = pl.BlockSpec((FFN_HALO, f), lambda i: (jnp.minimum((i + 1) * nbh, s // FFN_HALO - 1), 0))
    row = pl.BlockSpec((ts, d), lambda i: (i, 0))
    vec = pl.BlockSpec((1, d), lambda i: (0, 0))
    return _hbm_call(
        body, name=name,
        grid=(s // ts,),
        in_specs=[tile, nxt, tile, nxt, _const_spec(cw.shape),
                  pl.BlockSpec(w.shape, lambda i: (0, 0, 0), pipeline_mode=pl.Buffered(1)), row, row, vec, vec],
        out_specs=[row, pl.BlockSpec((ts, 2 * f), lambda i: (i, 0)), vec, vec, vec],
        out_shape=[jax.ShapeDtypeStruct((s, d), F32), jax.ShapeDtypeStruct((s, 2 * f), BF16)] + [jax.ShapeDtypeStruct((1, d), F32)] * 3,
        scratch_shapes=[pltpu.VMEM((ts + FFN_HALO, tc), F32)],
        compiler_params=pltpu.CompilerParams(
            dimension_semantics=("arbitrary",),
            vmem_limit_bytes=_vmem_limit(4 * ts * f * 4 + 2 * f * d * 2 + 2 * ts * 2 * f * 2 + 12 * ts * d * 4 + 6 * ts * tc * 4)),
    )(dug, dug, duv, duv, cw, w, x, dres, gain, s1p)


def _adamw_math(w, g, m, v):
    m = ADAM_B1 * m + (1.0 - ADAM_B1) * g
    v = ADAM_B2 * v + (1.0 - ADAM_B2) * (g * g)
    m_hat = m / (1.0 - ADAM_B1 ** ADAM_STEP)
    v_hat = v / (1.0 - ADAM_B2 ** ADAM_STEP)
    delta = -ADAM_LR * (m_hat / (jnp.sqrt(v_hat) + ADAM_EPS) + ADAM_WD * w)
    return delta, m, v


def _adam_rows(rows, cols):
    want = max(8, (2 * 1024 * 1024 // (cols * 4)) // 8 * 8)
    tr = min(rows, want)
    while rows % tr:
        tr -= 8
    return tr


def _adamw(w, m, v, g_parts, name):
    shape = w.shape
    nl = shape[0] if w.ndim == 3 else 1
    r, c = shape[-2], shape[-1]
    tr = _adam_rows(r, c)
    ng = len(g_parts)

    def body(*refs):
        w_ref, m_ref, v_ref = refs[0:3]
        g_refs = refs[3:3 + ng]
        g_out, d_out, m_out, v_out = refs[3 + ng:]
        g = g_refs[0][...]
        for gr in g_refs[1:]:
            g = g + gr[...]
        delta, mn, vn = _adamw_math(w_ref[...], g, m_ref[...], v_ref[...])
        g_out[...] = g
        d_out[...] = delta
        m_out[...] = mn
        v_out[...] = vn

    blk = pl.BlockSpec((None, tr, c), lambda l, i: (l, i, 0))
    outs = _hbm_call(
        body, name=name,
        grid=(nl, r // tr),
        in_specs=[blk] * (3 + ng),
        out_specs=[blk] * 4,
        out_shape=[jax.ShapeDtypeStruct((nl, r, c), F32)] * 4,
        compiler_params=pltpu.CompilerParams(dimension_semantics=("arbitrary", "arbitrary"),
                                             vmem_limit_bytes=_vmem_limit(2 * (7 + ng) * tr * max(c, 128) * 4 + (8 << 20))),
    )(*[a.reshape(nl, r, c) for a in (w, m, v, *g_parts)])
    return [o.reshape(shape) for o in outs]


def _adamw_many(ws, ms, vs, gs, name):
    n = len(ws)

    def body(*refs):
        for k in range(n):
            w_ref, m_ref, v_ref, g_ref = refs[k], refs[n + k], refs[2 * n + k], refs[3 * n + k]
            delta, mn, vn = _adamw_math(w_ref[...], g_ref[...], m_ref[...], v_ref[...])
            refs[4 * n + 3 * k][...] = delta
            refs[4 * n + 3 * k + 1][...] = mn
            refs[4 * n + 3 * k + 2][...] = vn

    specs = [_const_spec(a.shape) for a in ws]
    outs = _hbm_call(
        body, name=name,
        grid=(1,),
        in_specs=specs * 4,
        out_specs=[sp for sp in specs for _ in range(3)],
        out_shape=[jax.ShapeDtypeStruct(a.shape, F32) for a in ws for _ in range(3)],
        compiler_params=pltpu.CompilerParams(dimension_semantics=("arbitrary",),
                                             vmem_limit_bytes=_vmem_limit(20 * sum(a.size for a in ws) * 4 + (8 << 20))),
    )(*ws, *ms, *vs, *gs)
    return [tuple(outs[3 * k:3 * k + 3]) for k in range(n)]


def _modw_adamw(sct, dmod, w, m, v, name):
    nl, d, n = w.shape
    tr = _row_tile(d, 128)

    def body(sct_ref, dm_ref, w_ref, m_ref, v_ref, g_out, d_out, m_out, v_out):
        sc = sct_ref[...].astype(BF16).astype(F32)
        dm = dm_ref[...].astype(BF16).astype(F32)
        g = sc[:, 0:1] * dm[0:1, :]
        for b in range(1, N_DEV):
            g = g + sc[:, b:b + 1] * dm[b:b + 1, :]
        delta, mn, vn = _adamw_math(w_ref[...], g, m_ref[...], v_ref[...])
        g_out[...] = g
        d_out[...] = delta
        m_out[...] = mn
        v_out[...] = vn

    blk = pl.BlockSpec((None, tr, n), lambda l, i: (l, i, 0))
    return _hbm_call(
        body, name=name,
        grid=(nl, d // tr),
        in_specs=[pl.BlockSpec((tr, N_DEV), lambda l, i: (i, 0)), pl.BlockSpec((None, N_DEV, n), lambda l, i: (l, 0, 0)),
                  blk, blk, blk],
        out_specs=[blk] * 4,
        out_shape=[jax.ShapeDtypeStruct((nl, d, n), F32)] * 4,
        compiler_params=pltpu.CompilerParams(dimension_semantics=("arbitrary", "arbitrary"),
                                             vmem_limit_bytes=_vmem_limit(2 * 8 * tr * n * 4 + (8 << 20))),
    )(sct, dmod, w, m, v)


def _reduce4(recvs, name):
    nl = len(recvs)
    shape = recvs[0].shape[1:]
    c = shape[-1]
    r = math.prod(shape[:-1])
    tr = _adam_rows(r, c)
    nt = r // tr

    def body(*refs):
        o_ref = refs[nl]
        for l in range(nl):
            @pl.when(pl.program_id(0) == l)
            def _():
                acc = refs[l][0].astype(F32)
                for k in range(1, N_CHIPS):
                    acc = acc + refs[l][k].astype(F32)
                o_ref[...] = acc

    def in_map(l):
        return lambda ll, i: (0, jnp.where(ll < l, 0, jnp.where(ll > l, nt - 1, i)), 0)

    return _hbm_call(
        body, name=name,
        grid=(nl, nt),
        in_specs=[pl.BlockSpec((N_CHIPS, tr, c), in_map(l)) for l in range(nl)],
        out_specs=pl.BlockSpec((None, tr, c), lambda ll, i: (ll, i, 0)),
        out_shape=jax.ShapeDtypeStruct((nl, r, c), F32),
        compiler_params=pltpu.CompilerParams(dimension_semantics=("arbitrary", "arbitrary"),
                                             vmem_limit_bytes=_vmem_limit(2 * 8 * nl * tr * max(c, 128) * 4 + (8 << 20))),
    )(*[rv.reshape(N_CHIPS, r, c) for rv in recvs]).reshape((nl,) + shape)


def _my_place():
    return lax.axis_index("x"), lax.axis_index("y"), lax.axis_index("c")


def _chip_coords(j):
    return j // 2, j % 2


def _mod_forward(c, mod_w, mod_b4):
    nl, d, n = mod_w.shape
    kc = 256

    def body(c_ref, w_ref, b_ref, mod_ref, sc_ref, cbuf, stage, s1, r1, s2, r2):
        mx, my, mc = _my_place()
        me = 4 * mx + 2 * my + mc
        q = 2 * mx + my
        cv = c_ref[...]
        cbuf[me] = jnp.broadcast_to(cv * jax.nn.sigmoid(cv), (8, d))
        sends = []
        for t in range(N_DEV):
            tx, ty = _chip_coords(t // 2)
            cp = pltpu.make_async_remote_copy(src_ref=cbuf.at[me], dst_ref=cbuf.at[me], send_sem=s1.at[t], recv_sem=r1.at[me],
                                              device_id=(tx, ty, t % 2), device_id_type=MESH)

            @pl.when(t != me)
            def _():
                cp.start()

            sends.append((t, cp))
        for t in range(N_DEV):
            @pl.when(t != me)
            def _():
                pltpu.make_async_remote_copy(src_ref=cbuf.at[t], dst_ref=cbuf.at[t], send_sem=s1.at[t], recv_sem=r1.at[t],
                                             device_id=(mx, my, mc), device_id_type=MESH).wait_recv()
        for t, cp in sends:
            @pl.when(t != me)
            def _():
                cp.wait_send()

        row = lax.broadcasted_iota(jnp.int32, (8, d), 0)
        sc_all = jnp.zeros((8, d), F32)
        for t in range(N_DEV):
            sc_all = sc_all + jnp.where(row == t, cbuf[t], 0.0)
        sc_ref[...] = sc_all
        rown = lax.broadcasted_iota(jnp.int32, (8, n), 0)
        for l in range(nl):
            acc = jnp.zeros((8, n), F32)
            for k0 in range(0, d, kc):
                acc = acc + _dot(sc_all[:, k0:k0 + kc].astype(BF16), w_ref[l, k0:k0 + kc, :].astype(BF16))
            acc = acc + b_ref[l, q]
            for j in range(N_CHIPS):
                jx, jy = _chip_coords(j)
                bdest = 4 * jx + 2 * jy + mc
                rowv = jnp.sum(jnp.where(rown == bdest, acc, 0.0), axis=0, keepdims=True)
                stage[j, l] = jnp.broadcast_to(rowv, (8, n))
        sends2 = []
        for j in range(N_CHIPS):
            jx, jy = _chip_coords(j)
            cp = pltpu.make_async_remote_copy(src_ref=stage.at[j], dst_ref=mod_ref.at[:, q], send_sem=s2.at[j], recv_sem=r2.at[q],
                                              device_id=(jx, jy, mc), device_id_type=MESH)

            @pl.when(j != q)
            def _():
                cp.start()

            @pl.when(j == q)
            def _():
                for l in range(nl):
                    mod_ref[l, j] = stage[j, l]

            sends2.append((j, cp))
        for j in range(N_CHIPS):
            @pl.when(j != q)
            def _():
                pltpu.make_async_remote_copy(src_ref=stage.at[j], dst_ref=mod_ref.at[:, j], send_sem=s2.at[j], recv_sem=r2.at[j],
                                             device_id=(mx, my, mc), device_id_type=MESH).wait_recv()
        for j, cp in sends2:
            @pl.when(j != q)
            def _():
                cp.wait_send()

    vm = pl.BlockSpec(memory_space=pltpu.VMEM)
    return pl.pallas_call(
        body, name="mod_forward",
        in_specs=[vm, vm, vm],
        out_specs=[vm, vm],
        out_shape=[jax.ShapeDtypeStruct((nl, N_CHIPS, 8, n), F32), jax.ShapeDtypeStruct((8, d), F32)],
        scratch_shapes=[pltpu.VMEM((N_DEV, 8, d), F32), pltpu.VMEM((N_CHIPS, nl, 8, n), F32),
                        pltpu.SemaphoreType.DMA((N_DEV,)), pltpu.SemaphoreType.DMA((N_DEV,)),
                        pltpu.SemaphoreType.DMA((N_CHIPS,)), pltpu.SemaphoreType.DMA((N_CHIPS,))],
        compiler_params=pltpu.CompilerParams(vmem_limit_bytes=_vmem_limit(2 * nl * d * n * 4 + (8 << 20))),
    )(c, mod_w, mod_b4)


_HBM_SPEC = pl.BlockSpec(memory_space=pltpu.HBM)
_SEM_SPEC = pl.BlockSpec(memory_space=pltpu.SEMAPHORE)
_DATAFLOW = pltpu.SideEffectType.DATAFLOW_SIDE_EFFECTING


def _slot(ref, scatter, j):
    return ref.at[j] if scatter else ref


def _exchange_start(groups, scatter, after, name):
    flat = [a for g in groups for a in g]
    na = len(flat)
    ng = len(groups)
    sizes = [len(g) for g in groups]
    first = [sum(sizes[:g]) for g in range(ng)]
    where = [(g, k) for g in range(ng) for k in range(sizes[g])]
    mx, my, _ = _my_place()
    qo = 2 * mx + my
    lands = []
    for a in flat:
        own = lax.dynamic_index_in_dim(a, qo, 0, keepdims=False) if scatter else a
        lands.append(lax.dynamic_update_index_in_dim(lax.empty((N_CHIPS,) + own.shape, a.dtype), own, qo, 0))

    def body(*refs):
        ins, lnd = refs[:na], refs[na:2 * na]
        ssems, rsems = refs[2 * na + 1:2 * na + 1 + ng], refs[2 * na + 1 + ng:2 * na + 1 + 2 * ng]
        token = refs[-1]
        mx, my, mc = _my_place()
        q = 2 * mx + my
        for j in range(N_CHIPS):
            jx, jy = _chip_coords(j)
            for a in range(na):
                g, k = where[a]

                @pl.when(j != q)
                def _():
                    pltpu.make_async_remote_copy(src_ref=_slot(ins[a], scatter, j), dst_ref=lnd[a].at[q],
                                                 send_sem=ssems[g].at[k * N_CHIPS + j], recv_sem=rsems[g].at[k * N_CHIPS + q],
                                                 device_id=(jx, jy, mc), device_id_type=MESH).start()
        token[...] = jnp.zeros_like(token)

    sem_shapes = [pltpu.SemaphoreType.DMA((n * N_CHIPS,)) for n in sizes]
    outs = pl.pallas_call(
        body, name=name,
        in_specs=[_HBM_SPEC] * (2 * na) + [pl.BlockSpec(memory_space=pl.ANY)],
        out_specs=[_SEM_SPEC] * (2 * ng) + [_HBM_SPEC] * (2 * na) + [pl.BlockSpec(memory_space=pltpu.VMEM)],
        out_shape=sem_shapes + sem_shapes + [pltpu.HBM(a.shape, a.dtype) for a in flat + lands]
        + [jax.ShapeDtypeStruct((8, 128), F32)],
        input_output_aliases={i: 2 * ng + i for i in range(2 * na)},
        compiler_params=pltpu.CompilerParams(has_side_effects=_DATAFLOW),
    )(*[pltpu.with_memory_space_constraint(a, pltpu.HBM) for a in flat + lands], after)
    ssems, rsems = outs[:ng], outs[ng:2 * ng]
    src_thru, land_thru = outs[2 * ng:2 * ng + na], outs[2 * ng + na:2 * ng + 2 * na]
    states = [(src_thru[first[g]:first[g] + sizes[g]], land_thru[first[g]:first[g] + sizes[g]], ssems[g], rsems[g])
              for g in range(ng)]
    return states, outs[-1]


def _exchange_wait(state, scatter, after, name):
    srcs, lands, ssem, rsem = state
    na = len(srcs)

    def body(*refs):
        ins, lnd = refs[:na], refs[na:2 * na]
        ssem_ref, rsem_ref = refs[2 * na], refs[2 * na + 1]
        mx, my, mc = _my_place()
        q = 2 * mx + my
        for j in range(N_CHIPS):
            for a in range(na):
                @pl.when(j != q)
                def _():
                    cp = pltpu.make_async_remote_copy(src_ref=_slot(ins[a], scatter, j), dst_ref=lnd[a].at[j],
                                                      send_sem=ssem_ref.at[a * N_CHIPS + j], recv_sem=rsem_ref.at[a * N_CHIPS + j],
                                                      device_id=(mx, my, mc), device_id_type=MESH)
                    cp.wait_send()
                    cp.wait_recv()

    outs = pl.pallas_call(
        body, name=name,
        in_specs=[_HBM_SPEC] * (2 * na) + [_SEM_SPEC, _SEM_SPEC, pl.BlockSpec(memory_space=pl.ANY)],
        out_specs=[_HBM_SPEC] * (2 * na),
        out_shape=[pltpu.HBM(a.shape, a.dtype) for a in list(srcs) + list(lands)],
        input_output_aliases={i: i for i in range(2 * na)},
        compiler_params=pltpu.CompilerParams(has_side_effects=_DATAFLOW),
    )(*srcs, *lands, ssem, rsem, after)
    return outs[na:]


def _sibling_copy(src, dst, ssem, rsem, a):
    mx, my, mc = _my_place()
    return pltpu.make_async_remote_copy(src_ref=src, dst_ref=dst, send_sem=ssem.at[a], recv_sem=rsem.at[a],
                                        device_id=(mx, my, 1 - mc), device_id_type=MESH)


def _swap_start(arrs, after, name):
    na = len(arrs)
    lands = [lax.empty(a.shape, a.dtype) for a in arrs]

    def body(*refs):
        ins, lnd = refs[:na], refs[na:2 * na]
        ssem, rsem, token = refs[2 * na + 1], refs[2 * na + 2], refs[-1]
        for a in range(na):
            _sibling_copy(ins[a], lnd[a], ssem, rsem, a).start()
        token[...] = jnp.zeros_like(token)

    outs = pl.pallas_call(
        body, name=name,
        in_specs=[_HBM_SPEC] * (2 * na) + [pl.BlockSpec(memory_space=pl.ANY)],
        out_specs=[_SEM_SPEC] * 2 + [_HBM_SPEC] * (2 * na) + [pl.BlockSpec(memory_space=pltpu.VMEM)],
        out_shape=[pltpu.SemaphoreType.DMA((na,))] * 2 + [pltpu.HBM(a.shape, a.dtype) for a in list(arrs) + lands]
        + [jax.ShapeDtypeStruct((8, 128), F32)],
        input_output_aliases={i: 2 + i for i in range(2 * na)},
        compiler_params=pltpu.CompilerParams(has_side_effects=_DATAFLOW),
    )(*[pltpu.with_memory_space_constraint(a, pltpu.HBM) for a in list(arrs) + lands], after)
    return (outs[2:2 + na], outs[2 + na:2 + 2 * na], outs[0], outs[1]), outs[-1]


def _swap_wait(state, after, name):
    srcs, lands, ssem, rsem = state
    na = len(srcs)

    def body(*refs):
        ins, lnd = refs[:na], refs[na:2 * na]
        ssem_ref, rsem_ref = refs[2 * na], refs[2 * na + 1]
        for a in range(na):
            cp = _sibling_copy(ins[a], lnd[a], ssem_ref, rsem_ref, a)
            cp.wait_send()
            cp.wait_recv()

    outs = pl.pallas_call(
        body, name=name,
        in_specs=[_HBM_SPEC] * (2 * na) + [_SEM_SPEC, _SEM_SPEC, pl.BlockSpec(memory_space=pl.ANY)],
        out_specs=[_HBM_SPEC] * (2 * na),
        out_shape=[pltpu.HBM(a.shape, a.dtype) for a in list(srcs) + list(lands)],
        input_output_aliases={i: i for i in range(2 * na)},
        compiler_params=pltpu.CompilerParams(has_side_effects=_DATAFLOW),
    )(*srcs, *lands, ssem, rsem, after)
    return outs[:na], outs[na:]


def _allreduce_small(rows_all, rows_sum):
    ra, c = rows_all.shape
    r = rows_sum.shape[0]
    ch = r // N_DEV
    assert ch % 8 == 0 and ch * N_DEV == r

    def body(a_ref, s_ref, all_ref, sum_ref, rbuf, red, sa, rva, sb, rvb, sc, rvc):
        mx, my, mc = _my_place()
        me = 4 * mx + 2 * my + mc
        mine = pl.ds(pl.multiple_of(me * ch, 8), ch)
        all_ref[me] = a_ref[...]
        rbuf[me] = s_ref[mine, :]

        def dev(t):
            tx, ty = _chip_coords(t // 2)
            return (tx, ty, t % 2)

        def everyone_else(fn):
            for t in range(N_DEV):
                @pl.when(t != me)
                def _():
                    fn(t)

        def copy_a(t, slot):
            return pltpu.make_async_remote_copy(src_ref=a_ref, dst_ref=all_ref.at[slot], send_sem=sa.at[t], recv_sem=rva.at[slot],
                                                device_id=dev(t), device_id_type=MESH)

        def copy_b(t, slot):
            return pltpu.make_async_remote_copy(src_ref=s_ref.at[pl.ds(t * ch, ch), :], dst_ref=rbuf.at[slot], send_sem=sb.at[t],
                                                recv_sem=rvb.at[slot], device_id=dev(t), device_id_type=MESH)

        def copy_c(t, chunk_start, slot):
            return pltpu.make_async_remote_copy(src_ref=red, dst_ref=sum_ref.at[pl.ds(chunk_start, ch), :], send_sem=sc.at[t],
                                                recv_sem=rvc.at[slot], device_id=dev(t), device_id_type=MESH)

        everyone_else(lambda t: (copy_a(t, me).start(), copy_b(t, me).start()))
        everyone_else(lambda t: (copy_a(t, t).wait_recv(), copy_b(t, t).wait_recv()))
        everyone_else(lambda t: (copy_a(t, me).wait_send(), copy_b(t, me).wait_send()))
        acc = rbuf[0]
        for t in range(1, N_DEV):
            acc = acc + rbuf[t]
        red[...] = acc
        sum_ref[mine, :] = acc
        everyone_else(lambda t: copy_c(t, pl.multiple_of(me * ch, 8), me).start())
        everyone_else(lambda t: copy_c(t, t * ch, t).wait_recv())
        everyone_else(lambda t: copy_c(t, pl.multiple_of(me * ch, 8), me).wait_send())

    vm = pl.BlockSpec(memory_space=pltpu.VMEM)
    return pl.pallas_call(
        body, name="allreduce_small",
        in_specs=[vm, vm],
        out_specs=[vm, vm],
        out_shape=[jax.ShapeDtypeStruct((N_DEV, ra, c), F32), jax.ShapeDtypeStruct((r, c), F32)],
        scratch_shapes=[pltpu.VMEM((N_DEV, ch, c), F32), pltpu.VMEM((ch, c), F32)] + [pltpu.SemaphoreType.DMA((N_DEV,))] * 6,
        compiler_params=pltpu.CompilerParams(vmem_limit_bytes=_vmem_limit((3 * r + 2 * N_DEV * ra) * c * 4 + (4 << 20))),
    )(rows_all, rows_sum)


def _pack(arrs, row_multiple=8):
    rows, layout, at = [], [], 0
    for a in arrs:
        n = a.size
        nr = -(-n // (8 * SMALL_COLS)) * 8
        flat = a.reshape(-1)
        if nr * SMALL_COLS != n:
            flat = jnp.pad(flat, (0, nr * SMALL_COLS - n))
        rows.append(flat.reshape(nr, SMALL_COLS))
        layout.append((at, nr, a.shape))
        at += nr
    pad = -at % row_multiple
    if pad:
        rows.append(jnp.zeros((pad, SMALL_COLS), F32))
    return jnp.concatenate(rows, axis=0), layout


def _unpack(buf, layout):
    out = []
    for at, nr, shape in layout:
        n = math.prod(shape)
        out.append(buf[at:at + nr].reshape(-1)[:n].reshape(shape))
    return out


SMALL_NAMES = ("mod_b", "mix_pre_g", "mix_post_g", "sgu_norm_g", "sgu_norm_b", "sgu_w", "sgu_b", "conv_b", "conv_norm_g",
               "conv_norm_b", "pool_w", "pool_scale", "branch_g", "ffn_pre_g", "ffn_post_g", "ffn_conv_b")
SHARDED_SMALL = ("conv_w", "ffn_conv_w")
WEIGHT_ORDER = ("mod_w", "mod_b", "mix_pre_g", "mix_post_g", "w_in", "sgu_norm_g", "sgu_norm_b", "sgu_w", "sgu_b", "conv_w",
                "conv_b", "conv_norm_g", "conv_norm_b", "pool_w", "pool_scale", "branch_g", "w_out", "ffn_pre_g", "ffn_post_g",
                "ffn_up", "ffn_conv_w", "ffn_conv_b", "ffn_down")


def _block_diag(blocks):
    n, a, b = blocks.shape
    eye = jnp.eye(n, dtype=blocks.dtype)
    return (eye[:, None, :, None] * blocks[:, :, None, :]).reshape(n * a, n * b)


def _diag_blocks(mat, n):
    a = mat.shape[0] // n
    return jnp.stack([mat[g * a:(g + 1) * a, g * a:(g + 1) * a] for g in range(n)])


def _step(x, c, loss_target, w, m, v):
    nl = w["mod_w"].shape[0]
    s, d = x.shape[1], x.shape[2]
    heads = SGU_WIDTH // HEAD_DIM
    groups = len(POOL_WINDOWS)
    mx, my, _ = _my_place()
    q = 2 * mx + my
    x0 = x.reshape(s, d)
    tgt = loss_target.reshape(s, d)

    nmod = w["mod_w"].shape[2]
    kin = w["w_in"].shape[2]
    inw = kin * N_CHIPS
    f2 = w["ffn_up"].shape[2] * N_CHIPS
    f = f2 // 2

    def wgroups(l):
        return [[jnp.swapaxes(w["w_in"][l], 0, 1).astype(BF16), w["conv_w"][l], w["ffn_conv_w"][l]], [w["w_out"][l].astype(BF16)],
                [w["ffn_up"][l].astype(BF16)], [w["ffn_down"][l].astype(BF16)]]

    gstates = {}
    (gstates[0, 0], gstates[0, 1], gstates[0, 2]), gtoken = _exchange_start(wgroups(0)[:3], False, c, "gather_start_in_0")
    mod4, sc_all = _mod_forward(c + gtoken[0:1, 0:1], w["mod_w"], w["mod_b"].reshape(nl, N_CHIPS, 1, nmod))
    mod = mod4[:, :, 0, :].reshape(nl, N_MOD, 1, d)

    tril = jnp.tril(jnp.ones((CHUNK, CHUNK), bool))
    bd = _block_diag(jnp.ones((heads, HEAD_DIM, HEAD_DIM), BF16))

    def mixer_params(l, conv_w):
        wm = jnp.where(tril[None], w["sgu_w"][l], 0.0)
        pw = _block_diag(w["pool_w"][l])
        return dict(
            bd=bd, ng=w["sgu_norm_g"][l][None], nb=w["sgu_norm_b"][l][None],
            wm=wm.astype(BF16), wmt=jnp.swapaxes(wm, 1, 2).astype(BF16),
            bias=jnp.repeat(w["sgu_b"][l].T, HEAD_DIM, axis=1),
            cw=conv_w, cb=w["conv_b"][l][None], cng=w["conv_norm_g"][l][None], cnb=w["conv_norm_b"][l][None],
            pw=pw.astype(BF16), pwt=pw.T.astype(BF16), ps=w["pool_scale"][l][None], bg=w["branch_g"][l][None])

    saved = []
    xl = x0
    arrived = {0: list(_exchange_wait(gstates[0, 0], False, mod4, "gather_wait_in_0"))}
    arrived[0] += list(_exchange_wait(gstates[0, 1], False, arrived[0][0], "gather_wait_out_0"))
    for l in range(nl):
        sh1, sc1, g1, sh2, sc2, g2 = [mod[l, k] for k in range(N_MOD)]
        gpre1, gpost1 = w["mix_pre_g"][l][None], w["mix_post_g"][l][None]
        gpre2, gpost2 = w["ffn_pre_g"][l][None], w["ffn_post_g"][l][None]
        fcb = w["ffn_conv_b"][l][None]
        sh1_after, bg_after, g1_after, sh2_after, fcb_after = sh1, w["branch_g"][l][None], g1, sh2, fcb
        g_win, g_cw, g_fcw = arrived[l][:3]
        w_in = g_win.reshape(inw, d)
        conv_w = jnp.transpose(g_cw, (1, 0, 2)).reshape(CONV_K, CONV_WIDTH)
        ffn_cw = jnp.transpose(g_fcw, (1, 0, 2)).reshape(FFN_CONV_K, f2)
        mp = mixer_params(l, conv_w)
        z, h1 = _norm_mod_matmul(xl, gpre1, 1.0 + sc1, sh1_after, w_in[None], f"mix_in_{l}", transposed=True)
        w_out = arrived[l][3].reshape(d, d)
        ycat, cbo, dpool = _mixer_fwd(z, dict(mp, bg=bg_after), f"mixer_fwd_{l}")
        (up,) = _exchange_wait(gstates[l, 2], False, ycat, f"gather_wait_up_{l}")
        if l == 0:
            (gstates[0, 3],), tok = _exchange_start(wgroups(0)[3:4], False, up, "gather_start_down_0")
            g1_after = g1 + tok[0:1, 0:1]
        o, x1 = _matmul_norm_resid(ycat, w_out, xl, g1_after, gpost1, f"mix_out_{l}")
        if l + 1 < nl:
            (gstates[l + 1, 0], gstates[l + 1, 1]), tok = _exchange_start(wgroups(l + 1)[0:2], False, x1,
                                                                        f"gather_start_in_{l + 1}")
            sh2_after = sh2 + tok[0:1, 0:1]
        p, h2 = _norm_mod_matmul(x1, gpre2, 1.0 + sc2, sh2_after, up, f"ffn_in_{l}")
        if l + 1 < nl:
            nxt = _exchange_wait(gstates[l + 1, 0], False, p, f"gather_wait_in_{l + 1}")
            nxt_out = _exchange_wait(gstates[l + 1, 1], False, nxt[0], f"gather_wait_out_{l + 1}")
            arrived[l + 1] = list(nxt) + list(nxt_out)
            (gstates[l + 1, 2], gstates[l + 1, 3]), tok = _exchange_start(wgroups(l + 1)[2:4], False, nxt_out[0],
                                                                        f"gather_start_up_{l + 1}")
            fcb_after = fcb + tok[0:1, 0:1]
        act = _ffn_act_fwd(p, ffn_cw, fcb_after, f"ffn_act_{l}")
        (g_down,) = _exchange_wait(gstates[l, 3], False, act, f"gather_wait_down_{l}")
        down = g_down.reshape(f, d)
        qo, x2 = _matmul_norm_resid(act, down, x1, g2, gpost2, f"ffn_out_{l}")
        saved.append(dict(x=xl, z=z, h1=h1, ycat=ycat, cbo=cbo, dpool=dpool, o=o, x1=x1, p=p, h2=h2, act=act, qo=qo, mp=mp, fcb=fcb,
                          w_in=w_in, w_out=w_out, up=up, down=down, ffn_cw=ffn_cw,
                          mods=(sh1, sc1, g1, sh2, sc2, g2), gains=(gpre1, gpost1, gpre2, gpost2)))
        xl = x2

    dx, loss_row = _loss_head(xl, tgt)

    small = {n: [None] * nl for n in SMALL_NAMES + SHARDED_SMALL}
    dmods = [None] * nl
    tn = f2 // N_CHIPS
    sstates = {}
    token = None
    for l in reversed(range(nl)):
        sv = saved[l]
        sh1, sc1, g1, sh2, sc2, g2 = sv["mods"]
        gpre1, gpost1, gpre2, gpost2 = sv["gains"]
        if token is not None:
            g2 = g2 + token[0:1, 0:1]
        dq, dact, dg2, dgpost2 = _resid_bwd_matmul(dx, sv["qo"], g2, gpost2, sv["down"], f"ffn_out_bwd_{l}")
        g_down = _wgrad(sv["act"], dq, (f, lambda j: 0), (d, lambda j: 0), jax.ShapeDtypeStruct((f, d), BF16),
                        (1, lambda j: (0, 0)), (f, d), f"wgrad_ffn_down_{l}")
        dug, duv, dfwg, dfwv, dfbg, dfbv = _ffn_act_bwd(sv["p"], dact, sv["ffn_cw"], sv["fcb"], f"ffn_act_bwd_{l}")
        dx1, dp, dsh2, dsc2, dgpre2 = _ffn_in_bwd(dug, duv, sv["ffn_cw"], sv["up"], sv["x1"], dx, gpre2, 1.0 + sc2,
                                                  f"ffn_in_bwd_{l}")
        g_up = _wgrad(sv["h2"], dp, (d, lambda j: 0), (tn, lambda j: j), jax.ShapeDtypeStruct((N_CHIPS, d, tn), BF16),
                      (N_CHIPS, lambda j: (j, 0, 0)), (None, d, tn), f"wgrad_ffn_up_{l}")
        (sstates[l, 0],), token = _exchange_start([[g_down.reshape(N_CHIPS, f // N_CHIPS, d), g_up]], True, g_up,
                                                  f"scatter_start_ffn_{l}")
        do, dycat, dg1, dgpost1 = _resid_bwd_matmul(dx1, sv["o"], g1 + token[0:1, 0:1], gpost1, sv["w_out"],
                                                    f"mix_out_bwd_{l}")
        g_out = _wgrad(sv["ycat"], do, (d, lambda j: 0), (d, lambda j: 0), jax.ShapeDtypeStruct((d, d), BF16),
                       (1, lambda j: (0, 0)), (d, d), f"wgrad_w_out_{l}")
        (sstates[l, 1],), token = _exchange_start([[g_out.reshape(N_CHIPS, d // N_CHIPS, d)]], True, g_out,
                                                  f"scatter_start_out_{l}")
        mp_after = dict(sv["mp"], bg=sv["mp"]["bg"] + token[0:1, 0:1])
        (dza, dcb, dd, dbg, dwm, dbias, dng, dnb, dcng, dcnb, dps, dpw) = _mixer_bwd_a(sv["z"], sv["cbo"], sv["dpool"], dycat, mp_after, f"mixer_bwd_a_{l}")
        dx, dz, dsh1, dsc1, dgpre1, dcw, dcbias = _mixer_bwd_b(
            sv["z"], dza, dcb, dd, sv["x"], dx1, gpre1, 1.0 + sc1, sv["mp"]["cw"], sv["w_in"], f"mixer_bwd_b_{l}")
        g_in = _wgrad(dz, sv["h1"], (inw, lambda j: 0), (d, lambda j: 0), jax.ShapeDtypeStruct((inw, d), BF16),
                      (1, lambda j: (0, 0)), (inw, d), f"wgrad_w_in_{l}")
        g_in_parts = g_in.reshape(N_CHIPS, kin, d)
        if l > 0:
            (sstates[l, 2],), token = _exchange_start([[g_in_parts]], True, g_in_parts, f"scatter_start_in_{l}")

        dmods[l] = jnp.concatenate([dsh1, dsc1, dg1, dsh2, dsc2, dg2], axis=0)
        small["mix_pre_g"][l], small["mix_post_g"][l] = dgpre1[0], dgpost1[0]
        small["ffn_pre_g"][l], small["ffn_post_g"][l] = dgpre2[0], dgpost2[0]
        small["sgu_norm_g"][l], small["sgu_norm_b"][l] = dng[0], dnb[0]
        small["sgu_w"][l] = jnp.where(tril[None], dwm, 0.0)
        small["sgu_b"][l] = dbias.reshape(CHUNK, heads, HEAD_DIM).sum(-1).T
        small["conv_b"][l], small["conv_norm_g"][l], small["conv_norm_b"][l] = dcbias[0], dcng[0], dcnb[0]
        small["pool_w"][l], small["pool_scale"][l], small["branch_g"][l] = _diag_blocks(dpw, groups), dps[0], dbg[0]
        small["ffn_conv_b"][l] = jnp.concatenate([dfbg[0], dfbv[0]])
        small["conv_w"][l] = dcw
        small["ffn_conv_w"][l] = jnp.concatenate([dfwg, dfwv], axis=1)

    names = [n for n in SMALL_NAMES if n != "mod_b"] + list(SHARDED_SMALL)
    dmod_rows, _ = _pack([jnp.stack(dmods)])
    packed, layout = _pack([jnp.stack(dmods), loss_row] + [jnp.stack(small[n]) for n in names], 8 * N_DEV)
    gathered, summed = _allreduce_small(dmod_rows, packed)
    (sstates[0, 2],), token = _exchange_start([[g_in_parts]], True, summed, "scatter_start_in_0")
    parts = _unpack(summed, layout)
    loss = parts[1][0, 0]
    gsmall = dict(zip(names, parts[2:]))
    gsmall["mod_b"] = parts[0].reshape(nl, N_MOD * d)
    dmod_all = gathered[:, :nl * N_MOD].reshape(N_DEV, nl, N_MOD * d)
    dmod_mine = jnp.transpose(lax.dynamic_slice_in_dim(dmod_all, q * nmod, nmod, axis=2), (1, 0, 2))

    grads, deltas, new_m, new_v = {}, {}, {}, {}

    def put(name, res):
        grads[name], deltas[name], new_m[name], new_v[name] = res

    recv = dict(w_in=[None] * nl, w_out=[None] * nl, ffn_up=[None] * nl, ffn_down=[None] * nl)
    done = token
    for l in reversed(range(nl)):
        recv["ffn_down"][l], recv["ffn_up"][l] = _exchange_wait(sstates[l, 0], True, done, f"scatter_wait_ffn_{l}")
        (recv["w_out"][l],) = _exchange_wait(sstates[l, 1], True, recv["ffn_up"][l], f"scatter_wait_out_{l}")
        done = recv["w_out"][l]
    big = ("w_out", "ffn_up", "ffn_down", "w_in")
    mine = {n: _reduce4(recv[n], f"reduce4_{n}") for n in big[:3]}
    swap_a, token = _swap_start([mine[n] for n in big[:3]], mine["ffn_down"], "swap_start_a")

    put("mod_w", _modw_adamw(sc_all.T + token[0:1, 0:1], dmod_mine, w["mod_w"], m["mod_w"], v["mod_w"], "adamw_mod_w"))
    done = grads["mod_w"]
    for l in reversed(range(nl)):
        (recv["w_in"][l],) = _exchange_wait(sstates[l, 2], True, done, f"scatter_wait_in_{l}")
        done = recv["w_in"][l]
    mine["w_in"] = _reduce4(recv["w_in"], "reduce4_w_in")
    swap_b, token = _swap_start([mine["w_in"]], mine["w_in"], "swap_start_b")

    gsmall["conv_w"] = lax.dynamic_slice_in_dim(gsmall["conv_w"], q * (CONV_WIDTH // N_CHIPS), CONV_WIDTH // N_CHIPS, axis=2)
    gsmall["ffn_conv_w"] = lax.dynamic_slice_in_dim(gsmall["ffn_conv_w"], q * (f2 // N_CHIPS), f2 // N_CHIPS, axis=2)
    snames = SMALL_NAMES + SHARDED_SMALL
    res = _adamw_many([w[n] for n in snames], [m[n] for n in snames], [v[n] for n in snames], [gsmall[n] for n in snames],
                      "adamw_small")
    for n, (d_, m_, v_) in zip(snames, res):
        put(n, (gsmall[n], d_, m_, v_))

    sent, theirs = _swap_wait(swap_a, deltas["mod_b"], "swap_wait_a")
    sent_b, theirs_b = _swap_wait(swap_b, sent[0], "swap_wait_b")
    for n, a, b in zip(big, list(sent) + list(sent_b), list(theirs) + list(theirs_b)):
        if n == "w_in":
            res = _adamw(*[jnp.swapaxes(t[n], 1, 2) for t in (w, m, v)], [a, b], f"adamw_{n}")
            put(n, [jnp.swapaxes(r_, 1, 2) for r_ in res])
        else:
            put(n, _adamw(w[n], m[n], v[n], [a, b], f"adamw_{n}"))

    return (loss, dx.reshape(1, s, d), *[grads[n] for n in WEIGHT_ORDER], *[deltas[n] for n in WEIGHT_ORDER],
            *[new_m[n] for n in WEIGHT_ORDER], *[new_v[n] for n in WEIGHT_ORDER])


def kernel(x, c, mod_w, mod_b, mix_pre_g, mix_post_g, w_in, sgu_norm_g, sgu_norm_b, sgu_w, sgu_b, conv_w, conv_b, conv_norm_g, conv_norm_b, pool_w, pool_scale, branch_g, w_out, ffn_pre_g, ffn_post_g, ffn_up, ffn_conv_w, ffn_conv_b, ffn_down, loss_target, m_mod_w, m_mod_b, m_mix_pre_g, m_mix_post_g, m_w_in, m_sgu_norm_g, m_sgu_norm_b, m_sgu_w, m_sgu_b, m_conv_w, m_conv_b, m_conv_norm_g, m_conv_norm_b, m_pool_w, m_pool_scale, m_branch_g, m_w_out, m_ffn_pre_g, m_ffn_post_g, m_ffn_up, m_ffn_conv_w, m_ffn_conv_b, m_ffn_down, v_mod_w, v_mod_b, v_mix_pre_g, v_mix_post_g, v_w_in, v_sgu_norm_g, v_sgu_norm_b, v_sgu_w, v_sgu_b, v_conv_w, v_conv_b, v_conv_norm_g, v_conv_norm_b, v_pool_w, v_pool_scale, v_branch_g, v_w_out, v_ffn_pre_g, v_ffn_post_g, v_ffn_up, v_ffn_conv_w, v_ffn_conv_b, v_ffn_down):
    w = dict(mod_w=mod_w, mod_b=mod_b, mix_pre_g=mix_pre_g, mix_post_g=mix_post_g, w_in=w_in, sgu_norm_g=sgu_norm_g,
             sgu_norm_b=sgu_norm_b, sgu_w=sgu_w, sgu_b=sgu_b, conv_w=conv_w, conv_b=conv_b, conv_norm_g=conv_norm_g,
             conv_norm_b=conv_norm_b, pool_w=pool_w, pool_scale=pool_scale, branch_g=branch_g, w_out=w_out,
             ffn_pre_g=ffn_pre_g, ffn_post_g=ffn_post_g, ffn_up=ffn_up, ffn_conv_w=ffn_conv_w, ffn_conv_b=ffn_conv_b,
             ffn_down=ffn_down)
    m = dict(mod_w=m_mod_w, mod_b=m_mod_b, mix_pre_g=m_mix_pre_g, mix_post_g=m_mix_post_g, w_in=m_w_in,
             sgu_norm_g=m_sgu_norm_g, sgu_norm_b=m_sgu_norm_b, sgu_w=m_sgu_w, sgu_b=m_sgu_b, conv_w=m_conv_w,
             conv_b=m_conv_b, conv_norm_g=m_conv_norm_g, conv_norm_b=m_conv_norm_b, pool_w=m_pool_w,
             pool_scale=m_pool_scale, branch_g=m_branch_g, w_out=m_w_out, ffn_pre_g=m_ffn_pre_g, ffn_post_g=m_ffn_post_g,
             ffn_up=m_ffn_up, ffn_conv_w=m_ffn_conv_w, ffn_conv_b=m_ffn_conv_b, ffn_down=m_ffn_down)
    v = dict(mod_w=v_mod_w, mod_b=v_mod_b, mix_pre_g=v_mix_pre_g, mix_post_g=v_mix_post_g, w_in=v_w_in,
             sgu_norm_g=v_sgu_norm_g, sgu_norm_b=v_sgu_norm_b, sgu_w=v_sgu_w, sgu_b=v_sgu_b, conv_w=v_conv_w,
             conv_b=v_conv_b, conv_norm_g=v_conv_norm_g, conv_norm_b=v_conv_norm_b, pool_w=v_pool_w,
             pool_scale=v_pool_scale, branch_g=v_branch_g, w_out=v_w_out, ffn_pre_g=v_ffn_pre_g, ffn_post_g=v_ffn_post_g,
             ffn_up=v_ffn_up, ffn_conv_w=v_ffn_conv_w, ffn_conv_b=v_ffn_conv_b, ffn_down=v_ffn_down)
    return _step(x, c, loss_target, w, m, v)
```

```python
import functools
import math

import jax
import jax.numpy as jnp
from jax import lax
from jax.experimental import pallas as pl
from jax.experimental.pallas import tpu as pltpu

F32 = jnp.float32
BF16 = jnp.bfloat16
MESH = pl.DeviceIdType.MESH

EPS = 1e-6
HEAD_DIM = 64
CHUNK = 128
SGU_WIDTH = 384
CONV_WIDTH = 384
POOL_WIDTH = 256
POOL_WINDOWS = (2, 4, 8, 16)
CONV_K = 31
FFN_CONV_K = 3
N_MOD = 6
N_CHIPS = 4
N_DEV = 8

ADAM_LR = 0.001
ADAM_B1 = 0.9
ADAM_B2 = 0.999
ADAM_EPS = 1e-08
ADAM_WD = 0.01
ADAM_STEP = 10

MIX_HALO = 32
FFN_HALO = 8
FFN_ROWS = 16
FFN_UNROLL = 8
LANES = 128
CONV_ROWS = 32
SMALL_COLS = 1024
VMEM_BYTES_V7X = 64 * 1024 * 1024


def _vmem_limit(estimate_bytes):
    return int(min(max(estimate_bytes, 16 * 1024 * 1024), VMEM_BYTES_V7X - 8 * 1024 * 1024))


def _row_tile(s, want):
    return want if s % want == 0 else math.gcd(s, want)


def _rsum(v):
    return jnp.sum(v, axis=0, keepdims=True)


def _rmean(v):
    return jnp.mean(v, axis=-1, keepdims=True)


def _gelu(v):
    k = math.sqrt(2.0 / math.pi)
    return 0.5 * v * (1.0 + jnp.tanh(k * (v + 0.044715 * v * v * v)))


def _gelu_grad(v):
    k = math.sqrt(2.0 / math.pi)
    t = jnp.tanh(k * (v + 0.044715 * v * v * v))
    return 0.5 * (1.0 + t) + 0.5 * v * (1.0 - t * t) * (k * (1.0 + 3.0 * 0.044715 * v * v))


def _dot(a, b):
    return jnp.dot(a, b, preferred_element_type=F32)


def _dot_nt(a, b):
    return lax.dot_general(a, b, (((1,), (1,)), ((), ())), preferred_element_type=F32)


def _dot_tn(a, b):
    return lax.dot_general(a, b, (((0,), (0,)), ((), ())), preferred_element_type=F32)


def _group_mean(v, bd):
    hi = v.astype(BF16)
    lo = (v - hi.astype(F32)).astype(BF16)
    return (_dot(hi, bd) + _dot(lo, bd)) * (1.0 / HEAD_DIM)


def _const_spec(shape):
    nd = len(shape)
    return pl.BlockSpec(shape, lambda *_: (0,) * nd)


def _hbm_call(body, **kw):
    call = pl.pallas_call(body, **kw)
    return lambda *args: call(*[pltpu.with_memory_space_constraint(a, pltpu.HBM) for a in args])


def _norm_mod_matmul(x, gain, s1p, shift, w, name, transposed=False):
    s, d = x.shape
    nb = w.shape[0]
    tn = w.shape[1] if transposed else w.shape[2]
    ts = _row_tile(s, 512 if nb * tn <= 2048 else 256)

    def body(x_ref, g_ref, s_ref, b_ref, w_ref, z_ref, h_ref):
        xv = x_ref[...]
        r = lax.rsqrt(_rmean(xv * xv) + EPS)
        h = ((xv * r) * g_ref[...] * s_ref[...] + b_ref[...]).astype(BF16)
        h_ref[...] = h
        for j in range(nb):
            z_ref[:, j * tn:(j + 1) * tn] = _dot_nt(h, w_ref[j]) if transposed else _dot(h, w_ref[j])

    vec = pl.BlockSpec((1, d), lambda i: (0, 0))
    return _hbm_call(
        body, name=name,
        grid=(s // ts,),
        in_specs=[pl.BlockSpec((ts, d), lambda i: (i, 0)), vec, vec, vec,
                  pl.BlockSpec(w.shape, lambda i: (0, 0, 0), pipeline_mode=pl.Buffered(1))],
        out_specs=[pl.BlockSpec((ts, nb * tn), lambda i: (i, 0)), pl.BlockSpec((ts, d), lambda i: (i, 0))],
        out_shape=[jax.ShapeDtypeStruct((s, nb * tn), F32), jax.ShapeDtypeStruct((s, d), BF16)],
        compiler_params=pltpu.CompilerParams(
            dimension_semantics=("arbitrary",),
            vmem_limit_bytes=_vmem_limit(2 * (ts * d * 4 + ts * nb * tn * 4 + ts * d * 2) + nb * d * tn * 2 + 4 * ts * d * 4)),
    )(x, gain, s1p, shift, w)


def _matmul_norm_resid(a, w, xres, gate, gpost, name):
    s, k = a.shape
    d = w.shape[1]
    ts = _row_tile(s, 512)

    def body(a_ref, w_ref, x_ref, gate_ref, gp_ref, o_ref, xn_ref):
        o = _dot(a_ref[...], w_ref[...])
        o_ref[...] = o
        r = lax.rsqrt(_rmean(o * o) + EPS)
        xn_ref[...] = x_ref[...] + gate_ref[...] * ((o * r) * gp_ref[...])

    vec = pl.BlockSpec((1, d), lambda i: (0, 0))
    row = pl.BlockSpec((ts, d), lambda i: (i, 0))
    return _hbm_call(
        body, name=name,
        grid=(s // ts,),
        in_specs=[pl.BlockSpec((ts, k), lambda i: (i, 0)),
                  pl.BlockSpec((k, d), lambda i: (0, 0), pipeline_mode=pl.Buffered(1)), row, vec, vec],
        out_specs=[row, row],
        out_shape=[jax.ShapeDtypeStruct((s, d), F32)] * 2,
        compiler_params=pltpu.CompilerParams(
            dimension_semantics=("arbitrary",),
            vmem_limit_bytes=_vmem_limit(2 * (ts * k * 2 + 3 * ts * d * 4) + k * d * 2 + 4 * ts * d * 4)),
    )(a, w, xres, gate, gpost)


def _wgrad(a, b, acols, bcols, out_struct, out_index, out_block, name):
    s = a.shape[0]
    aw, afn = acols
    bw, bfn = bcols
    ts = _row_tile(s, 2048 if aw * bw <= 2 * 1024 * 1024 else 1024)
    nj = out_index[0]
    oidx = out_index[1]

    def body(a_ref, b_ref, o_ref, acc):
        i = pl.program_id(1)

        @pl.when(i == 0)
        def _():
            acc[...] = jnp.zeros_like(acc)

        acc[...] += _dot_tn(a_ref[...], b_ref[...])

        @pl.when(i == pl.num_programs(1) - 1)
        def _():
            o_ref[...] = acc[...].astype(o_ref.dtype)

    return _hbm_call(
        body, name=name,
        grid=(nj, s // ts),
        in_specs=[pl.BlockSpec((ts, aw), lambda j, i: (i, afn(j))), pl.BlockSpec((ts, bw), lambda j, i: (i, bfn(j)))],
        out_specs=pl.BlockSpec(out_block, lambda j, i: oidx(j)),
        out_shape=out_struct,
        scratch_shapes=[pltpu.VMEM((aw, bw), F32)],
        compiler_params=pltpu.CompilerParams(
            dimension_semantics=("arbitrary", "arbitrary"),
            vmem_limit_bytes=_vmem_limit(2 * (ts * aw * 2 + ts * bw * 2) + 3 * aw * bw * 4 + ts * aw * 4)),
    )(a, b)


def _loss_head(xo, tgt):
    s, d = xo.shape
    ts = _row_tile(s, 512)

    def body(x_ref, t_ref, dx_ref, l_ref, acc):
        i = pl.program_id(0)

        @pl.when(i == 0)
        def _():
            acc[...] = jnp.zeros_like(acc)

        e = x_ref[...] - t_ref[...]
        dx_ref[...] = e * (1.0 / d)
        acc[...] += _rsum(e * e)

        @pl.when(i == pl.num_programs(0) - 1)
        def _():
            tot = jnp.sum(acc[...], axis=-1, keepdims=True) * (0.5 / d)
            l_ref[...] = jnp.broadcast_to(tot, l_ref.shape)

    row = pl.BlockSpec((ts, d), lambda i: (i, 0))
    return _hbm_call(
        body, name="loss_head",
        grid=(s // ts,),
        in_specs=[row, row],
        out_specs=[row, pl.BlockSpec((1, SMALL_COLS), lambda i: (0, 0))],
        out_shape=[jax.ShapeDtypeStruct((s, d), F32), jax.ShapeDtypeStruct((1, SMALL_COLS), F32)],
        scratch_shapes=[pltpu.VMEM((1, d), F32)],
        compiler_params=pltpu.CompilerParams(dimension_semantics=("arbitrary",)),
    )(xo, tgt)


def _resid_bwd_matmul(dxn, o, gate, gpost, w, name):
    s, d = dxn.shape
    k = w.shape[0]
    ts = _row_tile(s, 512)

    def body(dx_ref, o_ref, gate_ref, gp_ref, w_ref, do_ref, da_ref, dgate_ref, dgp_ref):
        i = pl.program_id(0)

        @pl.when(i == 0)
        def _():
            dgate_ref[...] = jnp.zeros_like(dgate_ref)
            dgp_ref[...] = jnp.zeros_like(dgp_ref)

        dx = dx_ref[...]
        o = o_ref[...]
        r = lax.rsqrt(_rmean(o * o) + EPS)
        on = o * r
        dgate_ref[...] += _rsum(dx * (on * gp_ref[...]))
        don = dx * gate_ref[...]
        dgp_ref[...] += _rsum(don * on)
        t = don * gp_ref[...]
        do = (r * (t - on * _rmean(t * on))).astype(BF16)
        do_ref[...] = do
        da_ref[...] = _dot_nt(do, w_ref[...])

    vec = pl.BlockSpec((1, d), lambda i: (0, 0))
    row = pl.BlockSpec((ts, d), lambda i: (i, 0))
    return _hbm_call(
        body, name=name,
        grid=(s // ts,),
        in_specs=[row, row, vec, vec, pl.BlockSpec((k, d), lambda i: (0, 0), pipeline_mode=pl.Buffered(1))],
        out_specs=[row, pl.BlockSpec((ts, k), lambda i: (i, 0)), vec, vec],
        out_shape=[jax.ShapeDtypeStruct((s, d), BF16), jax.ShapeDtypeStruct((s, k), F32),
                   jax.ShapeDtypeStruct((1, d), F32), jax.ShapeDtypeStruct((1, d), F32)],
        compiler_params=pltpu.CompilerParams(
            dimension_semantics=("arbitrary",),
            vmem_limit_bytes=_vmem_limit(2 * (2 * ts * d * 4 + ts * d * 2 + ts * k * 4) + d * k * 2 + 6 * ts * d * 4)),
    )(dxn, o, gate, gpost, w)


def _norm_mod_bwd(dh, xv, gain, s1p, dres):
    r = lax.rsqrt(_rmean(xv * xv) + EPS)
    xn = xv * r
    dshift = _rsum(dh)
    t = dh * xn
    dscale = _rsum(t * gain)
    dgain = _rsum(t * s1p)
    dxn = dh * (gain * s1p)
    dx = r * (dxn - xn * _rmean(dxn * xn)) + dres
    return dx, dshift, dscale, dgain


def _lane_lt(shape, bound):
    return lax.broadcasted_iota(jnp.int32, shape, 1) < bound


def _sgu_forward(z_ref, bd_ref, ng_ref, nb_ref, wm_ref, bias_ref, ts, ya_s, f_s):
    u = _gelu(z_ref[:, 0:SGU_WIDTH])
    v = _gelu(z_ref[:, SGU_WIDTH:2 * SGU_WIDTH])
    bd = bd_ref[...]
    vc = v - _group_mean(v, bd)
    rstd = lax.rsqrt(_group_mean(vc * vc, bd) + EPS)
    vhat = vc * rstd
    vn = (vhat * ng_ref[...] + nb_ref[...]).astype(BF16)
    left = _lane_lt((CHUNK, CHUNK), HEAD_DIM)
    for n in range(ts // CHUNK):
        rows = slice(n * CHUNK, (n + 1) * CHUNK)
        for p in range(SGU_WIDTH // CHUNK):
            cols = slice(p * CHUNK, (p + 1) * CHUNK)
            blk = vn[rows, cols]
            f = jnp.where(left, _dot(wm_ref[2 * p], blk), _dot(wm_ref[2 * p + 1], blk)) + bias_ref[:, cols]
            if f_s is not None:
                f_s[rows, cols] = f
            ya_s[rows, cols] = u[rows, cols] * f
    return u, vhat, rstd, vn


def _shifted_copies(ext, ext8, ts):
    for b in range(1, 8):
        ext8[b - 1] = ext[pl.ds(b, ts + MIX_HALO - 8), :]


def _rows_at(ext, ext8, start, nrows):
    b = start % 8
    return ext[pl.ds(start, nrows), :] if b == 0 else ext8[b - 1, pl.ds(start - b, nrows), :]


def _conv31_forward(z_ref, zh_ref, first, cw_ref, cb_ref, ts, ext_b, ext8, cbs):
    a = z_ref[:, 2 * SGU_WIDTH:2 * SGU_WIDTH + CONV_WIDTH]
    g = z_ref[:, 2 * SGU_WIDTH + CONV_WIDTH:2 * SGU_WIDTH + 2 * CONV_WIDTH]
    ah = zh_ref[:, 2 * SGU_WIDTH:2 * SGU_WIDTH + CONV_WIDTH]
    gh = zh_ref[:, 2 * SGU_WIDTH + CONV_WIDTH:2 * SGU_WIDTH + 2 * CONV_WIDTH]
    ext_b[pl.ds(0, MIX_HALO), :] = jnp.where(first, 0.0, ah * jax.nn.sigmoid(gh))
    ext_b[pl.ds(MIX_HALO, ts), :] = a * jax.nn.sigmoid(g)
    _shifted_copies(ext_b, ext8, ts)
    for r in range(ts // CONV_ROWS):
        acc = jnp.broadcast_to(cb_ref[...], (CONV_ROWS, CONV_WIDTH))
        for k in range(CONV_K):
            acc = acc + cw_ref[k:k + 1, :] * _rows_at(ext_b, ext8, MIX_HALO - (CONV_K - 1) + k + r * CONV_ROWS, CONV_ROWS)
        cbs[pl.ds(r * CONV_ROWS, CONV_ROWS), :] = acc


def _pool_counts(i, ts):
    pos1 = (i * ts + 1 + lax.broadcasted_iota(jnp.int32, (ts, POOL_WIDTH), 0)).astype(F32)
    lane = lax.broadcasted_iota(jnp.int32, (ts, POOL_WIDTH), 1)
    gdim = POOL_WIDTH // len(POOL_WINDOWS)
    win = jnp.where(lane < gdim, float(POOL_WINDOWS[0]),
                    jnp.where(lane < 2 * gdim, float(POOL_WINDOWS[1]),
                              jnp.where(lane < 3 * gdim, float(POOL_WINDOWS[2]), float(POOL_WINDOWS[3]))))
    return jnp.minimum(pos1, win)


def _window_sums(ext, base, ts, sign):
    lane = lax.broadcasted_iota(jnp.int32, (ts, POOL_WIDTH), 1)
    gdim = POOL_WIDTH // len(POOL_WINDOWS)
    run = jnp.zeros((ts, POOL_WIDTH), F32)
    out = jnp.zeros((ts, POOL_WIDTH), F32)
    for m in range(POOL_WINDOWS[-1]):
        run = run + ext[pl.ds(base + sign * m, ts), :]
        for gi, win in enumerate(POOL_WINDOWS):
            if m == win - 1:
                out = jnp.where((lane >= gi * gdim) & (lane < (gi + 1) * gdim), run, out)
    return out


def _pool_forward(z_ref, zh_ref, first, i, ts, ext_c):
    c0 = 2 * SGU_WIDTH + 2 * CONV_WIDTH
    zc = z_ref[:, c0:c0 + POOL_WIDTH]
    ext_c[pl.ds(0, MIX_HALO), :] = jnp.where(first, 0.0, zh_ref[:, c0:c0 + POOL_WIDTH])
    ext_c[pl.ds(MIX_HALO, ts), :] = zc
    sums = _window_sums(ext_c, MIX_HALO, ts, -1)
    return sums / _pool_counts(i, ts) - zc


def _layer_norm_rows(v):
    mu = _rmean(v)
    vc = v - mu
    rstd = lax.rsqrt(_rmean(vc * vc) + EPS)
    return vc * rstd, rstd


def _mixer_specs(s, ts, width):
    nbh = ts // MIX_HALO
    tile = pl.BlockSpec((ts, width), lambda i: (i, 0))
    prev = pl.BlockSpec((MIX_HALO, width), lambda i: (jnp.maximum(i * nbh - 1, 0), 0))
    nxt = pl.BlockSpec((MIX_HALO, width), lambda i: (jnp.minimum((i + 1) * nbh, s // MIX_HALO - 1), 0))
    return tile, prev, nxt


def _mixer_fwd(z, mp, name):
    s, inw = z.shape
    d = SGU_WIDTH + CONV_WIDTH + POOL_WIDTH
    ts = _row_tile(s, 256)

    def body(z_ref, zh_ref, bd_ref, ng_ref, nb_ref, wm_ref, bias_ref, cw_ref, cb_ref, cng_ref, cnb_ref,
             pw_ref, ps_ref, bg_ref, y_ref, cbs, dpool_ref, ya_s, ext_b, ext_c, ext8):
        i = pl.program_id(0)
        first = i == 0
        _sgu_forward(z_ref, bd_ref, ng_ref, nb_ref, wm_ref, bias_ref, ts, ya_s, None)
        ya = ya_s[...]
        ra = lax.rsqrt(_rmean(ya * ya) + EPS)
        y_ref[:, 0:SGU_WIDTH] = ((ya * ra) * bg_ref[:, 0:SGU_WIDTH]).astype(BF16)

        _conv31_forward(z_ref, zh_ref, first, cw_ref, cb_ref, ts, ext_b, ext8, cbs)
        chat, _ = _layer_norm_rows(cbs[...])
        lin = chat * cng_ref[...] + cnb_ref[...]
        yb = lin * jax.nn.sigmoid(lin)
        rb = lax.rsqrt(_rmean(yb * yb) + EPS)
        y_ref[:, SGU_WIDTH:SGU_WIDTH + CONV_WIDTH] = ((yb * rb) * bg_ref[:, SGU_WIDTH:SGU_WIDTH + CONV_WIDTH]).astype(BF16)

        dpool = _pool_forward(z_ref, zh_ref, first, i, ts, ext_c)
        dpool_ref[...] = dpool
        yc = _dot(dpool.astype(BF16), pw_ref[...]) * ps_ref[...]
        rc = lax.rsqrt(_rmean(yc * yc) + EPS)
        y_ref[:, SGU_WIDTH + CONV_WIDTH:d] = ((yc * rc) * bg_ref[:, SGU_WIDTH + CONV_WIDTH:d]).astype(BF16)

    tile, prev, _ = _mixer_specs(s, ts, inw)
    consts = [mp["bd"], mp["ng"], mp["nb"], mp["wm"], mp["bias"], mp["cw"], mp["cb"], mp["cng"], mp["cnb"],
              mp["pw"], mp["ps"], mp["bg"]]
    return _hbm_call(
        body, name=name,
        grid=(s // ts,),
        in_specs=[tile, prev] + [_const_spec(c.shape) for c in consts],
        out_specs=[pl.BlockSpec((ts, d), lambda i: (i, 0)), pl.BlockSpec((ts, CONV_WIDTH), lambda i: (i, 0)),
                   pl.BlockSpec((ts, POOL_WIDTH), lambda i: (i, 0))],
        out_shape=[jax.ShapeDtypeStruct((s, d), BF16), jax.ShapeDtypeStruct((s, CONV_WIDTH), F32),
                   jax.ShapeDtypeStruct((s, POOL_WIDTH), F32)],
        scratch_shapes=[pltpu.VMEM((ts, SGU_WIDTH), F32), pltpu.VMEM((ts + MIX_HALO, CONV_WIDTH), F32),
                        pltpu.VMEM((ts + MIX_HALO, POOL_WIDTH), F32), pltpu.VMEM((7, ts + MIX_HALO - 8, CONV_WIDTH), F32)],
        compiler_params=pltpu.CompilerParams(dimension_semantics=("arbitrary",),
                                             vmem_limit_bytes=_vmem_limit(16 * ts * inw * 4)),
    )(z, z, *consts)


def _mixer_bwd_a(z, cb, dpool, dy, mp, name):
    s, inw = z.shape
    d = SGU_WIDTH + CONV_WIDTH + POOL_WIDTH
    ts = _row_tile(s, 256)
    nchunk = ts // CHUNK

    def rms_bwd(dyn, y, g):
        r = lax.rsqrt(_rmean(y * y) + EPS)
        yn = y * r
        dg = _rsum(dyn * yn)
        t = dyn * g
        return r * (t - yn * _rmean(t * yn)), dg

    def body(z_ref, cbs, dpool_ref, dy_ref, bd_ref, ng_ref, nb_ref, wm_ref, wmt_ref, bias_ref, cng_ref, cnb_ref,
             pw_ref, pwt_ref, ps_ref, bg_ref,
             dza_ref, dcb_ref, dd_ref, dbg_ref, dwm_ref, dbias_ref, dng_ref, dnb_ref, dcng_ref, dcnb_ref, dps_ref, dpw_ref,
             ya_s, f_s, dvn_s):
        i = pl.program_id(0)
        first = i == 0

        @pl.when(first)
        def _():
            for ref in (dwm_ref, dbias_ref, dng_ref, dnb_ref, dcng_ref, dcnb_ref, dps_ref, dpw_ref):
                ref[...] = jnp.zeros_like(ref)

        u, vhat, rstd, vn = _sgu_forward(z_ref, bd_ref, ng_ref, nb_ref, wm_ref, bias_ref, ts, ya_s, f_s)
        dya, dbg_a = rms_bwd(dy_ref[:, 0:SGU_WIDTH], ya_s[...], bg_ref[:, 0:SGU_WIDTH])
        du = dya * f_s[...]
        df = dya * u
        dfb = df.astype(BF16)
        left = _lane_lt((CHUNK, CHUNK), HEAD_DIM)
        zero = jnp.zeros((CHUNK, CHUNK), BF16)
        dbias = jnp.zeros((CHUNK, SGU_WIDTH), F32)
        for n in range(nchunk):
            rows = slice(n * CHUNK, (n + 1) * CHUNK)
            dbias = dbias + df[rows, :]
            for p in range(SGU_WIDTH // CHUNK):
                cols = slice(p * CHUNK, (p + 1) * CHUNK)
                dblk = dfb[rows, cols]
                vblk = vn[rows, cols]
                dwm_ref[2 * p] += _dot_nt(jnp.where(left, dblk, zero), vblk)
                dwm_ref[2 * p + 1] += _dot_nt(jnp.where(left, zero, dblk), vblk)
                dvn_s[rows, cols] = jnp.where(left, _dot(wmt_ref[2 * p], dblk), _dot(wmt_ref[2 * p + 1], dblk))
        dbias_ref[...] += dbias
        dvn = dvn_s[...]
        dng_ref[...] += _rsum(dvn * vhat)
        dnb_ref[...] += _rsum(dvn)
        dvh = dvn * ng_ref[...]
        bd = bd_ref[...]
        dv = rstd * (dvh - _group_mean(dvh, bd) - vhat * _group_mean(dvh * vhat, bd))
        dza_ref[:, 0:SGU_WIDTH] = (du * _gelu_grad(z_ref[:, 0:SGU_WIDTH])).astype(BF16)
        dza_ref[:, SGU_WIDTH:2 * SGU_WIDTH] = (dv * _gelu_grad(z_ref[:, SGU_WIDTH:2 * SGU_WIDTH])).astype(BF16)

        chat, crstd = _layer_norm_rows(cbs[...])
        lin = chat * cng_ref[...] + cnb_ref[...]
        sl = jax.nn.sigmoid(lin)
        dyb, dbg_b = rms_bwd(dy_ref[:, SGU_WIDTH:SGU_WIDTH + CONV_WIDTH], lin * sl, bg_ref[:, SGU_WIDTH:SGU_WIDTH + CONV_WIDTH])
        dlin = dyb * (sl * (1.0 + lin * (1.0 - sl)))
        dcng_ref[...] += _rsum(dlin * chat)
        dcnb_ref[...] += _rsum(dlin)
        dch = dlin * cng_ref[...]
        dcb_ref[...] = crstd * (dch - _rmean(dch) - chat * _rmean(dch * chat))

        dpb = dpool_ref[...].astype(BF16)
        ycp = _dot(dpb, pw_ref[...])
        dyc, dbg_c = rms_bwd(dy_ref[:, SGU_WIDTH + CONV_WIDTH:d], ycp * ps_ref[...], bg_ref[:, SGU_WIDTH + CONV_WIDTH:d])
        dps_ref[...] += _rsum(dyc * ycp)
        dycp = (dyc * ps_ref[...]).astype(BF16)
        dpw_ref[...] += _dot_tn(dpb, dycp)
        dd_ref[...] = _dot(dycp, pwt_ref[...])

        @pl.when(first)
        def _():
            dbg_ref[...] = jnp.zeros_like(dbg_ref)

        dbg_ref[:, 0:SGU_WIDTH] += dbg_a
        dbg_ref[:, SGU_WIDTH:SGU_WIDTH + CONV_WIDTH] += dbg_b
        dbg_ref[:, SGU_WIDTH + CONV_WIDTH:d] += dbg_c

    tile, _, _ = _mixer_specs(s, ts, inw)
    consts = [mp["bd"], mp["ng"], mp["nb"], mp["wm"], mp["wmt"], mp["bias"], mp["cng"], mp["cnb"],
              mp["pw"], mp["pwt"], mp["ps"], mp["bg"]]
    acc_shapes = [(1, d), (2 * (SGU_WIDTH // CHUNK), CHUNK, CHUNK), (CHUNK, SGU_WIDTH), (1, SGU_WIDTH), (1, SGU_WIDTH),
                  (1, CONV_WIDTH), (1, CONV_WIDTH), (1, POOL_WIDTH), (POOL_WIDTH, POOL_WIDTH)]
    return _hbm_call(
        body, name=name,
        grid=(s // ts,),
        in_specs=[tile, pl.BlockSpec((ts, CONV_WIDTH), lambda i: (i, 0)), pl.BlockSpec((ts, POOL_WIDTH), lambda i: (i, 0)),
                  pl.BlockSpec((ts, d), lambda i: (i, 0))] + [_const_spec(c.shape) for c in consts],
        out_specs=[pl.BlockSpec((ts, 2 * SGU_WIDTH), lambda i: (i, 0)), pl.BlockSpec((ts, CONV_WIDTH), lambda i: (i, 0)),
                   pl.BlockSpec((ts, POOL_WIDTH), lambda i: (i, 0))] + [_const_spec(a) for a in acc_shapes],
        out_shape=[jax.ShapeDtypeStruct((s, 2 * SGU_WIDTH), BF16), jax.ShapeDtypeStruct((s, CONV_WIDTH), F32),
                   jax.ShapeDtypeStruct((s, POOL_WIDTH), F32)] + [jax.ShapeDtypeStruct(a, F32) for a in acc_shapes],
        scratch_shapes=[pltpu.VMEM((ts, SGU_WIDTH), F32), pltpu.VMEM((ts, SGU_WIDTH), F32), pltpu.VMEM((ts, SGU_WIDTH), F32)],
        compiler_params=pltpu.CompilerParams(dimension_semantics=("arbitrary",),
                                             vmem_limit_bytes=_vmem_limit(24 * ts * inw * 4)),
    )(z, cb, dpool, dy, *consts)


def _mixer_bwd_b(z, dza, dcb, dd, x, dres, gain, s1p, cw, w, name):
    s, inw = z.shape
    d = x.shape[1]
    ts = _row_tile(s, 256)
    c0 = 2 * SGU_WIDTH
    c1 = c0 + 2 * CONV_WIDTH

    def body(z_ref, zh_ref, dza_ref, dcb_ref, dcbn_ref, dd_ref, ddn_ref, x_ref, dres_ref, g_ref, s_ref, cw_ref, w_ref,
             dx_ref, dz_ref, dsh_ref, dsc_ref, dg_ref, dcw_ref, dcbias_ref, ext_b, ext_n, ext_e, ext8):
        i = pl.program_id(0)
        first = i == 0
        last = i == pl.num_programs(0) - 1

        @pl.when(first)
        def _():
            for ref in (dsh_ref, dsc_ref, dg_ref, dcw_ref, dcbias_ref):
                ref[...] = jnp.zeros_like(ref)

        a = z_ref[:, c0:c0 + CONV_WIDTH]
        sg = jax.nn.sigmoid(z_ref[:, c0 + CONV_WIDTH:c1])
        ah = zh_ref[:, c0:c0 + CONV_WIDTH]
        gh = zh_ref[:, c0 + CONV_WIDTH:c1]
        ext_b[pl.ds(0, MIX_HALO), :] = jnp.where(first, 0.0, ah * jax.nn.sigmoid(gh))
        ext_b[pl.ds(MIX_HALO, ts), :] = a * sg
        _shifted_copies(ext_b, ext8, ts)
        dcbv = dcb_ref[...]
        dcbias_ref[...] += _rsum(dcbv)
        for k in range(CONV_K):
            dcw_ref[k:k + 1, :] += _rsum(dcbv * _rows_at(ext_b, ext8, MIX_HALO - (CONV_K - 1) + k, ts))

        ext_n[pl.ds(0, ts), :] = dcbv
        ext_n[pl.ds(ts, MIX_HALO), :] = jnp.where(last, 0.0, dcbn_ref[...])
        _shifted_copies(ext_n, ext8, ts)
        for r in range(ts // CONV_ROWS):
            acc = jnp.zeros((CONV_ROWS, CONV_WIDTH), F32)
            for k in range(CONV_K):
                acc = acc + cw_ref[k:k + 1, :] * _rows_at(ext_n, ext8, CONV_K - 1 - k + r * CONV_ROWS, CONV_ROWS)
            rows = pl.ds(r * CONV_ROWS, CONV_ROWS)
            ar = z_ref[rows, c0:c0 + CONV_WIDTH]
            sr = jax.nn.sigmoid(z_ref[rows, c0 + CONV_WIDTH:c1])
            dz_ref[rows, c0:c0 + CONV_WIDTH] = (acc * sr).astype(BF16)
            dz_ref[rows, c0 + CONV_WIDTH:c1] = (acc * ar * sr * (1.0 - sr)).astype(BF16)

        ddv = dd_ref[...]
        ext_e[pl.ds(0, ts), :] = ddv / _pool_counts(i, ts)
        nh = (i + 1) * ts + lax.broadcasted_iota(jnp.int32, (MIX_HALO, POOL_WIDTH), 0)
        lane = lax.broadcasted_iota(jnp.int32, (MIX_HALO, POOL_WIDTH), 1)
        gdim = POOL_WIDTH // len(POOL_WINDOWS)
        winh = jnp.where(lane < gdim, float(POOL_WINDOWS[0]),
                         jnp.where(lane < 2 * gdim, float(POOL_WINDOWS[1]),
                                   jnp.where(lane < 3 * gdim, float(POOL_WINDOWS[2]), float(POOL_WINDOWS[3]))))
        cnth = jnp.minimum((nh + 1).astype(F32), winh)
        ext_e[pl.ds(ts, MIX_HALO), :] = jnp.where(last, 0.0, ddn_ref[...] / cnth)
        dz_ref[:, c1:inw] = (_window_sums(ext_e, 0, ts, 1) - ddv).astype(BF16)
        dz_ref[:, 0:c0] = dza_ref[...]

        dh = _dot(dz_ref[...], w_ref[...])
        dx, dsh, dsc, dg = _norm_mod_bwd(dh, x_ref[...], g_ref[...], s_ref[...], dres_ref[...])
        dx_ref[...] = dx
        dsh_ref[...] += dsh
        dsc_ref[...] += dsc
        dg_ref[...] += dg

    tile, prev, _ = _mixer_specs(s, ts, inw)
    _, _, nxt_b = _mixer_specs(s, ts, CONV_WIDTH)
    _, _, nxt_c = _mixer_specs(s, ts, POOL_WIDTH)
    row = pl.BlockSpec((ts, d), lambda i: (i, 0))
    vec = pl.BlockSpec((1, d), lambda i: (0, 0))
    return _hbm_call(
        body, name=name,
        grid=(s // ts,),
        in_specs=[tile, prev, pl.BlockSpec((ts, c0), lambda i: (i, 0)),
                  pl.BlockSpec((ts, CONV_WIDTH), lambda i: (i, 0)), nxt_b,
                  pl.BlockSpec((ts, POOL_WIDTH), lambda i: (i, 0)), nxt_c,
                  row, row, vec, vec, _const_spec(cw.shape),
                  pl.BlockSpec(w.shape, lambda i: (0, 0), pipeline_mode=pl.Buffered(1))],
        out_specs=[row, pl.BlockSpec((ts, inw), lambda i: (i, 0)), vec, vec, vec,
                   _const_spec((CONV_K, CONV_WIDTH)), _const_spec((1, CONV_WIDTH))],
        out_shape=[jax.ShapeDtypeStruct((s, d), F32), jax.ShapeDtypeStruct((s, inw), BF16)]
        + [jax.ShapeDtypeStruct((1, d), F32)] * 3
        + [jax.ShapeDtypeStruct((CONV_K, CONV_WIDTH), F32), jax.ShapeDtypeStruct((1, CONV_WIDTH), F32)],
        scratch_shapes=[pltpu.VMEM((ts + MIX_HALO, CONV_WIDTH), F32), pltpu.VMEM((ts + MIX_HALO, CONV_WIDTH), F32),
                        pltpu.VMEM((ts + MIX_HALO, POOL_WIDTH), F32), pltpu.VMEM((7, ts + MIX_HALO - 8, CONV_WIDTH), F32)],
        compiler_params=pltpu.CompilerParams(dimension_semantics=("arbitrary",),
                                             vmem_limit_bytes=_vmem_limit(18 * ts * inw * 4 + inw * d * 2)),
    )(z, z, dza, dcb, dcb, dd, dd, x, dres, gain, s1p, cw, w)


def _ffn_specs(s, ts, tc, half_blocks):
    nbh = ts // FFN_HALO

    def tile(off):
        return pl.BlockSpec((ts, tc), lambda j, i: (i, j + off))

    def prev(off):
        return pl.BlockSpec((FFN_HALO, tc), lambda j, i: (jnp.maximum(i * nbh - 1, 0), j + off))

    def vec(rows, off):
        return pl.BlockSpec((rows, tc), lambda j, i: (0, j + off))

    return tile, prev, vec


def _rows_before(cur, prev, k):
    row = lax.broadcasted_iota(jnp.int32, cur.shape, 0)
    return jnp.where(row >= k, pltpu.roll(cur, k, 0), pltpu.roll(prev, k, 0))


def _conv3_rows(cur, prev, w_ref, b_ref, cols):
    x1 = _rows_before(cur, prev, 1)
    x2 = _rows_before(cur, prev, 2)
    u = b_ref[:, cols] + w_ref[2:3, cols] * cur + w_ref[1:2, cols] * x1 + w_ref[0:1, cols] * x2
    return u, x2, x1


def _halo_chunk(h_ref, cols, first):
    h = jnp.where(first, 0.0, h_ref[:, cols])
    return jnp.concatenate([h] * (FFN_ROWS // FFN_HALO), axis=0)


def _ffn_act_fwd(p, cw, cb, name):
    s, f2 = p.shape
    f = f2 // 2
    tc = f // 2
    hb = f // tc
    ts = _row_tile(s, 256)

    def body(pg_ref, pgh_ref, pv_ref, pvh_ref, wg_ref, wv_ref, bg_ref, bv_ref, act_ref):
        first = pl.program_id(1) == 0
        for c in range(tc // LANES):
            cols = slice(c * LANES, (c + 1) * LANES)

            def chunk(r, carry, cols=cols):
                pg_prev, pv_prev = carry
                rows = pl.ds(pl.multiple_of(r * FFN_ROWS, FFN_ROWS), FFN_ROWS)
                pg = pg_ref[rows, cols]
                pv = pv_ref[rows, cols]
                ug, _, _ = _conv3_rows(pg, pg_prev, wg_ref, bg_ref, cols)
                uv, _, _ = _conv3_rows(pv, pv_prev, wv_ref, bv_ref, cols)
                act_ref[rows, cols] = (_gelu(ug) * uv).astype(BF16)
                return pg, pv

            def step(r, carry, chunk=chunk):
                for u in range(FFN_UNROLL):
                    carry = chunk(r * FFN_UNROLL + u, carry)
                return carry

            lax.fori_loop(0, ts // (FFN_ROWS * FFN_UNROLL), step,
                          (_halo_chunk(pgh_ref, cols, first), _halo_chunk(pvh_ref, cols, first)))

    tile, prev, vec = _ffn_specs(s, ts, tc, hb)
    return _hbm_call(
        body, name=name,
        grid=(hb, s // ts),
        in_specs=[tile(0), prev(0), tile(hb), prev(hb), vec(FFN_CONV_K, 0), vec(FFN_CONV_K, hb), vec(1, 0), vec(1, hb)],
        out_specs=pl.BlockSpec((ts, tc), lambda j, i: (i, j)),
        out_shape=jax.ShapeDtypeStruct((s, f), BF16),
        compiler_params=pltpu.CompilerParams(dimension_semantics=("arbitrary", "arbitrary"),
                                             vmem_limit_bytes=_vmem_limit(8 * ts * tc * 4)),
    )(p, p, p, p, cw, cw, cb, cb)


def _ffn_act_bwd(p, dact, cw, cb, name):
    s, f2 = p.shape
    f = f2 // 2
    tc = f // 2
    hb = f // tc
    ts = _row_tile(s, 256)

    def body(pg_ref, pgh_ref, pv_ref, pvh_ref, da_ref, wg_ref, wv_ref, bg_ref, bv_ref,
             dug_ref, duv_ref, dwg_ref, dwv_ref, dbg_ref, dbv_ref):
        first = pl.program_id(1) == 0

        @pl.when(first)
        def _():
            for ref in (dwg_ref, dwv_ref, dbg_ref, dbv_ref):
                ref[...] = jnp.zeros_like(ref)

        zero = jnp.zeros((FFN_ROWS, LANES), F32)
        for c in range(tc // LANES):
            cols = slice(c * LANES, (c + 1) * LANES)

            def chunk(r, carry, cols=cols):
                pg_prev, pv_prev, ag0, ag1, ag2, av0, av1, av2, sg, sv = carry
                rows = pl.ds(pl.multiple_of(r * FFN_ROWS, FFN_ROWS), FFN_ROWS)
                pg = pg_ref[rows, cols]
                pv = pv_ref[rows, cols]
                ug, pg2, pg1 = _conv3_rows(pg, pg_prev, wg_ref, bg_ref, cols)
                uv, pv2, pv1 = _conv3_rows(pv, pv_prev, wv_ref, bv_ref, cols)
                da = da_ref[rows, cols]
                dug = da * uv * _gelu_grad(ug)
                duv = da * _gelu(ug)
                dug_ref[rows, cols] = dug
                duv_ref[rows, cols] = duv
                return (pg, pv, ag0 + dug * pg2, ag1 + dug * pg1, ag2 + dug * pg,
                        av0 + duv * pv2, av1 + duv * pv1, av2 + duv * pv, sg + dug, sv + duv)

            def step(r, carry, chunk=chunk):
                for u in range(FFN_UNROLL):
                    carry = chunk(r * FFN_UNROLL + u, carry)
                return carry

            out = lax.fori_loop(0, ts // (FFN_ROWS * FFN_UNROLL), step,
                                (_halo_chunk(pgh_ref, cols, first), _halo_chunk(pvh_ref, cols, first)) + (zero,) * 8)
            for k in range(FFN_CONV_K):
                dwg_ref[k:k + 1, cols] += _rsum(out[2 + k])
                dwv_ref[k:k + 1, cols] += _rsum(out[5 + k])
            dbg_ref[:, cols] += _rsum(out[8])
            dbv_ref[:, cols] += _rsum(out[9])

    tile, prev, vec = _ffn_specs(s, ts, tc, hb)
    half = pl.BlockSpec((ts, tc), lambda j, i: (i, j))
    wacc = pl.BlockSpec((FFN_CONV_K, tc), lambda j, i: (0, j))
    bacc = pl.BlockSpec((1, tc), lambda j, i: (0, j))
    return _hbm_call(
        body, name=name,
        grid=(hb, s // ts),
        in_specs=[tile(0), prev(0), tile(hb), prev(hb), half, vec(FFN_CONV_K, 0), vec(FFN_CONV_K, hb), vec(1, 0), vec(1, hb)],
        out_specs=[half, half, wacc, wacc, bacc, bacc],
        out_shape=[jax.ShapeDtypeStruct((s, f), F32)] * 2 + [jax.ShapeDtypeStruct((FFN_CONV_K, f), F32)] * 2
        + [jax.ShapeDtypeStruct((1, f), F32)] * 2,
        compiler_params=pltpu.CompilerParams(dimension_semantics=("arbitrary", "arbitrary"),
                                             vmem_limit_bytes=_vmem_limit(12 * ts * tc * 4)),
    )(p, p, p, p, dact, cw, cw, cb, cb)


def _ffn_in_bwd(dug, duv, cw, w, x, dres, gain, s1p, name):
    s, f = dug.shape
    d = x.shape[1]
    ts = _row_tile(s, 256)
    tc = w.shape[2]
    assert f % tc == 0 and w.shape[0] * tc == 2 * f
    nbh = ts // FFN_HALO

    def body(dug_ref, dugn_ref, duv_ref, duvn_ref, cw_ref, w_ref, x_ref, dres_ref, g_ref, s_ref,
             dx_ref, dp_ref, dsh_ref, dsc_ref, dg_ref, ext):
        i = pl.program_id(0)
        last = i == pl.num_programs(0) - 1

        @pl.when(i == 0)
        def _():
            for ref in (dsh_ref, dsc_ref, dg_ref):
                ref[...] = jnp.zeros_like(ref)

        dh = jnp.zeros((ts, d), F32)
        for half, (t_ref, n_ref) in enumerate(((dug_ref, dugn_ref), (duv_ref, duvn_ref))):
            for cb in range(f // tc):
                cols = slice(cb * tc, (cb + 1) * tc)
                wcols = slice(half * f + cb * tc, half * f + (cb + 1) * tc)
                ext[pl.ds(0, ts), :] = t_ref[:, cols]
                ext[pl.ds(ts, FFN_HALO), :] = jnp.where(last, 0.0, n_ref[:, cols])
                acc = cw_ref[FFN_CONV_K - 1:FFN_CONV_K, wcols] * t_ref[:, cols]
                for k in range(FFN_CONV_K - 1):
                    acc = acc + cw_ref[k:k + 1, wcols] * ext[pl.ds(FFN_CONV_K - 1 - k, ts), :]
                dpb = acc.astype(BF16)
                dp_ref[:, wcols] = dpb
                dh = dh + _dot_nt(dpb, w_ref[half * (f // tc) + cb])
        dx, dsh, dsc, dg = _norm_mod_bwd(dh, x_ref[...], g_ref[...], s_ref[...], dres_ref[...])
        dx_ref[...] = dx
        dsh_ref[...] += dsh
        dsc_ref[...] += dsc
        dg_ref[...] += dg

    tile = pl.BlockSpec((ts, f), lambda i: (i, 0))
    nxt = pl.BlockSpec((FFN_HALO, f), lambda i: (jnp.minimum((i + 1) * nbh, s // FFN_HALO - 1), 0))
    row = pl.BlockSpec((ts, d), lambda i: (i, 0))
    vec = pl.BlockSpec((1, d), lambda i: (0, 0))
    return _hbm_call(
        body, name=name,
        grid=(s // ts,),
        in_specs=[tile, nxt, tile, nxt, _const_spec(cw.shape),
                  pl.BlockSpec(w.shape, lambda i: (0, 0, 0), pipeline_mode=pl.Buffered(1)), row, row, vec, vec],
        out_specs=[row, pl.BlockSpec((ts, 2 * f), lambda i: (i, 0)), vec, vec, vec],
        out_shape=[jax.ShapeDtypeStruct((s, d), F32), jax.ShapeDtypeStruct((s, 2 * f), BF16)] + [jax.ShapeDtypeStruct((1, d), F32)] * 3,
        scratch_shapes=[pltpu.VMEM((ts + FFN_HALO, tc), F32)],
        compiler_params=pltpu.CompilerParams(
            dimension_semantics=("arbitrary",),
            vmem_limit_bytes=_vmem_limit(4 * ts * f * 4 + 2 * f * d * 2 + 2 * ts * 2 * f * 2 + 12 * ts * d * 4 + 6 * ts * tc * 4)),
    )(dug, dug, duv, duv, cw, w, x, dres, gain, s1p)


def _adamw_math(w, g, m, v):
    m = ADAM_B1 * m + (1.0 - ADAM_B1) * g
    v = ADAM_B2 * v + (1.0 - ADAM_B2) * (g * g)
    m_hat = m / (1.0 - ADAM_B1 ** ADAM_STEP)
    v_hat = v / (1.0 - ADAM_B2 ** ADAM_STEP)
    delta = -ADAM_LR * (m_hat / (jnp.sqrt(v_hat) + ADAM_EPS) + ADAM_WD * w)
    return delta, m, v


def _adam_rows(rows, cols):
    want = max(8, (2 * 1024 * 1024 // (cols * 4)) // 8 * 8)
    tr = min(rows, want)
    while rows % tr:
        tr -= 8
    return tr


def _adamw(w, m, v, g_parts, name):
    shape = w.shape
    nl = shape[0] if w.ndim == 3 else 1
    r, c = shape[-2], shape[-1]
    tr = _adam_rows(r, c)
    ng = len(g_parts)

    def body(*refs):
        w_ref, m_ref, v_ref = refs[0:3]
        g_refs = refs[3:3 + ng]
        g_out, d_out, m_out, v_out = refs[3 + ng:]
        g = g_refs[0][...]
        for gr in g_refs[1:]:
            g = g + gr[...]
        delta, mn, vn = _adamw_math(w_ref[...], g, m_ref[...], v_ref[...])
        g_out[...] = g
        d_out[...] = delta
        m_out[...] = mn
        v_out[...] = vn

    blk = pl.BlockSpec((None, tr, c), lambda l, i: (l, i, 0))
    outs = _hbm_call(
        body, name=name,
        grid=(nl, r // tr),
        in_specs=[blk] * (3 + ng),
        out_specs=[blk] * 4,
        out_shape=[jax.ShapeDtypeStruct((nl, r, c), F32)] * 4,
        compiler_params=pltpu.CompilerParams(dimension_semantics=("arbitrary", "arbitrary"),
                                             vmem_limit_bytes=_vmem_limit(2 * (7 + ng) * tr * max(c, 128) * 4 + (8 << 20))),
    )(*[a.reshape(nl, r, c) for a in (w, m, v, *g_parts)])
    return [o.reshape(shape) for o in outs]


def _adamw_many(ws, ms, vs, gs, name):
    n = len(ws)

    def body(*refs):
        for k in range(n):
            w_ref, m_ref, v_ref, g_ref = refs[k], refs[n + k], refs[2 * n + k], refs[3 * n + k]
            delta, mn, vn = _adamw_math(w_ref[...], g_ref[...], m_ref[...], v_ref[...])
            refs[4 * n + 3 * k][...] = delta
            refs[4 * n + 3 * k + 1][...] = mn
            refs[4 * n + 3 * k + 2][...] = vn

    specs = [_const_spec(a.shape) for a in ws]
    outs = _hbm_call(
        body, name=name,
        grid=(1,),
        in_specs=specs * 4,
        out_specs=[sp for sp in specs for _ in range(3)],
        out_shape=[jax.ShapeDtypeStruct(a.shape, F32) for a in ws for _ in range(3)],
        compiler_params=pltpu.CompilerParams(dimension_semantics=("arbitrary",),
                                             vmem_limit_bytes=_vmem_limit(20 * sum(a.size for a in ws) * 4 + (8 << 20))),
    )(*ws, *ms, *vs, *gs)
    return [tuple(outs[3 * k:3 * k + 3]) for k in range(n)]


def _modw_adamw(sct, dmod, w, m, v, name):
    nl, d, n = w.shape
    tr = _row_tile(d, 128)

    def body(sct_ref, dm_ref, w_ref, m_ref, v_ref, g_out, d_out, m_out, v_out):
        sc = sct_ref[...].astype(BF16).astype(F32)
        dm = dm_ref[...].astype(BF16).astype(F32)
        g = sc[:, 0:1] * dm[0:1, :]
        for b in range(1, N_DEV):
            g = g + sc[:, b:b + 1] * dm[b:b + 1, :]
        delta, mn, vn = _adamw_math(w_ref[...], g, m_ref[...], v_ref[...])
        g_out[...] = g
        d_out[...] = delta
        m_out[...] = mn
        v_out[...] = vn

    blk = pl.BlockSpec((None, tr, n), lambda l, i: (l, i, 0))
    return _hbm_call(
        body, name=name,
        grid=(nl, d // tr),
        in_specs=[pl.BlockSpec((tr, N_DEV), lambda l, i: (i, 0)), pl.BlockSpec((None, N_DEV, n), lambda l, i: (l, 0, 0)),
                  blk, blk, blk],
        out_specs=[blk] * 4,
        out_shape=[jax.ShapeDtypeStruct((nl, d, n), F32)] * 4,
        compiler_params=pltpu.CompilerParams(dimension_semantics=("arbitrary", "arbitrary"),
                                             vmem_limit_bytes=_vmem_limit(2 * 8 * tr * n * 4 + (8 << 20))),
    )(sct, dmod, w, m, v)


def _reduce4(recvs, name):
    nl = len(recvs)
    shape = recvs[0].shape[1:]
    c = shape[-1]
    r = math.prod(shape[:-1])
    tr = _adam_rows(r, c)
    nt = r // tr

    def body(*refs):
        o_ref = refs[nl]
        for l in range(nl):
            @pl.when(pl.program_id(0) == l)
            def _():
                acc = refs[l][0].astype(F32)
                for k in range(1, N_CHIPS):
                    acc = acc + refs[l][k].astype(F32)
                o_ref[...] = acc

    def in_map(l):
        return lambda ll, i: (0, jnp.where(ll < l, 0, jnp.where(ll > l, nt - 1, i)), 0)

    return _hbm_call(
        body, name=name,
        grid=(nl, nt),
        in_specs=[pl.BlockSpec((N_CHIPS, tr, c), in_map(l)) for l in range(nl)],
        out_specs=pl.BlockSpec((None, tr, c), lambda ll, i: (ll, i, 0)),
        out_shape=jax.ShapeDtypeStruct((nl, r, c), F32),
        compiler_params=pltpu.CompilerParams(dimension_semantics=("arbitrary", "arbitrary"),
                                             vmem_limit_bytes=_vmem_limit(2 * 8 * nl * tr * max(c, 128) * 4 + (8 << 20))),
    )(*[rv.reshape(N_CHIPS, r, c) for rv in recvs]).reshape((nl,) + shape)


def _my_place():
    return lax.axis_index("x"), lax.axis_index("y"), lax.axis_index("c")


def _chip_coords(j):
    return j // 2, j % 2


def _mod_forward(c, mod_w, mod_b4):
    nl, d, n = mod_w.shape
    kc = 256

    def body(c_ref, w_ref, b_ref, mod_ref, sc_ref, cbuf, stage, s1, r1, s2, r2):
        mx, my, mc = _my_place()
        me = 4 * mx + 2 * my + mc
        q = 2 * mx + my
        cv = c_ref[...]
        cbuf[me] = jnp.broadcast_to(cv * jax.nn.sigmoid(cv), (8, d))
        sends = []
        for t in range(N_DEV):
            tx, ty = _chip_coords(t // 2)
            cp = pltpu.make_async_remote_copy(src_ref=cbuf.at[me], dst_ref=cbuf.at[me], send_sem=s1.at[t], recv_sem=r1.at[me],
                                              device_id=(tx, ty, t % 2), device_id_type=MESH)

            @pl.when(t != me)
            def _():
                cp.start()

            sends.append((t, cp))
        for t in range(N_DEV):
            @pl.when(t != me)
            def _():
                pltpu.make_async_remote_copy(src_ref=cbuf.at[t], dst_ref=cbuf.at[t], send_sem=s1.at[t], recv_sem=r1.at[t],
                                             device_id=(mx, my, mc), device_id_type=MESH).wait_recv()
        for t, cp in sends:
            @pl.when(t != me)
            def _():
                cp.wait_send()

        row = lax.broadcasted_iota(jnp.int32, (8, d), 0)
        sc_all = jnp.zeros((8, d), F32)
        for t in range(N_DEV):
            sc_all = sc_all + jnp.where(row == t, cbuf[t], 0.0)
        sc_ref[...] = sc_all
        rown = lax.broadcasted_iota(jnp.int32, (8, n), 0)
        for l in range(nl):
            acc = jnp.zeros((8, n), F32)
            for k0 in range(0, d, kc):
                acc = acc + _dot(sc_all[:, k0:k0 + kc].astype(BF16), w_ref[l, k0:k0 + kc, :].astype(BF16))
            acc = acc + b_ref[l, q]
            for j in range(N_CHIPS):
                jx, jy = _chip_coords(j)
                bdest = 4 * jx + 2 * jy + mc
                rowv = jnp.sum(jnp.where(rown == bdest, acc, 0.0), axis=0, keepdims=True)
                stage[j, l] = jnp.broadcast_to(rowv, (8, n))
        sends2 = []
        for j in range(N_CHIPS):
            jx, jy = _chip_coords(j)
            cp = pltpu.make_async_remote_copy(src_ref=stage.at[j], dst_ref=mod_ref.at[:, q], send_sem=s2.at[j], recv_sem=r2.at[q],
                                              device_id=(jx, jy, mc), device_id_type=MESH)

            @pl.when(j != q)
            def _():
                cp.start()

            @pl.when(j == q)
            def _():
                for l in range(nl):
                    mod_ref[l, j] = stage[j, l]

            sends2.append((j, cp))
        for j in range(N_CHIPS):
            @pl.when(j != q)
            def _():
                pltpu.make_async_remote_copy(src_ref=stage.at[j], dst_ref=mod_ref.at[:, j], send_sem=s2.at[j], recv_sem=r2.at[j],
                                             device_id=(mx, my, mc), device_id_type=MESH).wait_recv()
        for j, cp in sends2:
            @pl.when(j != q)
            def _():
                cp.wait_send()

    vm = pl.BlockSpec(memory_space=pltpu.VMEM)
    return pl.pallas_call(
        body, name="mod_forward",
        in_specs=[vm, vm, vm],
        out_specs=[vm, vm],
        out_shape=[jax.ShapeDtypeStruct((nl, N_CHIPS, 8, n), F32), jax.ShapeDtypeStruct((8, d), F32)],
        scratch_shapes=[pltpu.VMEM((N_DEV, 8, d), F32), pltpu.VMEM((N_CHIPS, nl, 8, n), F32),
                        pltpu.SemaphoreType.DMA((N_DEV,)), pltpu.SemaphoreType.DMA((N_DEV,)),
                        pltpu.SemaphoreType.DMA((N_CHIPS,)), pltpu.SemaphoreType.DMA((N_CHIPS,))],
        compiler_params=pltpu.CompilerParams(vmem_limit_bytes=_vmem_limit(2 * nl * d * n * 4 + (8 << 20))),
    )(c, mod_w, mod_b4)


_HBM_SPEC = pl.BlockSpec(memory_space=pltpu.HBM)
_SEM_SPEC = pl.BlockSpec(memory_space=pltpu.SEMAPHORE)
_DATAFLOW = pltpu.SideEffectType.DATAFLOW_SIDE_EFFECTING


def _slot(ref, scatter, j):
    return ref.at[j] if scatter else ref


def _exchange_start(groups, scatter, after, name):
    flat = [a for g in groups for a in g]
    na = len(flat)
    ng = len(groups)
    sizes = [len(g) for g in groups]
    first = [sum(sizes[:g]) for g in range(ng)]
    where = [(g, k) for g in range(ng) for k in range(sizes[g])]
    mx, my, _ = _my_place()
    qo = 2 * mx + my
    lands = []
    for a in flat:
        own = lax.dynamic_index_in_dim(a, qo, 0, keepdims=False) if scatter else a
        lands.append(lax.dynamic_update_index_in_dim(lax.empty((N_CHIPS,) + own.shape, a.dtype), own, qo, 0))

    def body(*refs):
        ins, lnd = refs[:na], refs[na:2 * na]
        ssems, rsems = refs[2 * na + 1:2 * na + 1 + ng], refs[2 * na + 1 + ng:2 * na + 1 + 2 * ng]
        token = refs[-1]
        mx, my, mc = _my_place()
        q = 2 * mx + my
        for j in range(N_CHIPS):
            jx, jy = _chip_coords(j)
            for a in range(na):
                g, k = where[a]

                @pl.when(j != q)
                def _():
                    pltpu.make_async_remote_copy(src_ref=_slot(ins[a], scatter, j), dst_ref=lnd[a].at[q],
                                                 send_sem=ssems[g].at[k * N_CHIPS + j], recv_sem=rsems[g].at[k * N_CHIPS + q],
                                                 device_id=(jx, jy, mc), device_id_type=MESH).start()
        token[...] = jnp.zeros_like(token)

    sem_shapes = [pltpu.SemaphoreType.DMA((n * N_CHIPS,)) for n in sizes]
    outs = pl.pallas_call(
        body, name=name,
        in_specs=[_HBM_SPEC] * (2 * na) + [pl.BlockSpec(memory_space=pl.ANY)],
        out_specs=[_SEM_SPEC] * (2 * ng) + [_HBM_SPEC] * (2 * na) + [pl.BlockSpec(memory_space=pltpu.VMEM)],
        out_shape=sem_shapes + sem_shapes + [pltpu.HBM(a.shape, a.dtype) for a in flat + lands]
        + [jax.ShapeDtypeStruct((8, 128), F32)],
        input_output_aliases={i: 2 * ng + i for i in range(2 * na)},
        compiler_params=pltpu.CompilerParams(has_side_effects=_DATAFLOW),
    )(*[pltpu.with_memory_space_constraint(a, pltpu.HBM) for a in flat + lands], after)
    ssems, rsems = outs[:ng], outs[ng:2 * ng]
    src_thru, land_thru = outs[2 * ng:2 * ng + na], outs[2 * ng + na:2 * ng + 2 * na]
    states = [(src_thru[first[g]:first[g] + sizes[g]], land_thru[first[g]:first[g] + sizes[g]], ssems[g], rsems[g])
              for g in range(ng)]
    return states, outs[-1]


def _exchange_wait(state, scatter, after, name):
    srcs, lands, ssem, rsem = state
    na = len(srcs)

    def body(*refs):
        ins, lnd = refs[:na], refs[na:2 * na]
        ssem_ref, rsem_ref = refs[2 * na], refs[2 * na + 1]
        mx, my, mc = _my_place()
        q = 2 * mx + my
        for j in range(N_CHIPS):
            for a in range(na):
                @pl.when(j != q)
                def _():
                    cp = pltpu.make_async_remote_copy(src_ref=_slot(ins[a], scatter, j), dst_ref=lnd[a].at[j],
                                                      send_sem=ssem_ref.at[a * N_CHIPS + j], recv_sem=rsem_ref.at[a * N_CHIPS + j],
                                                      device_id=(mx, my, mc), device_id_type=MESH)
                    cp.wait_send()
                    cp.wait_recv()

    outs = pl.pallas_call(
        body, name=name,
        in_specs=[_HBM_SPEC] * (2 * na) + [_SEM_SPEC, _SEM_SPEC, pl.BlockSpec(memory_space=pl.ANY)],
        out_specs=[_HBM_SPEC] * (2 * na),
        out_shape=[pltpu.HBM(a.shape, a.dtype) for a in list(srcs) + list(lands)],
        input_output_aliases={i: i for i in range(2 * na)},
        compiler_params=pltpu.CompilerParams(has_side_effects=_DATAFLOW),
    )(*srcs, *lands, ssem, rsem, after)
    return outs[na:]


def _sibling_copy(src, dst, ssem, rsem, a):
    mx, my, mc = _my_place()
    return pltpu.make_async_remote_copy(src_ref=src, dst_ref=dst, send_sem=ssem.at[a], recv_sem=rsem.at[a],
                                        device_id=(mx, my, 1 - mc), device_id_type=MESH)


def _swap_start(arrs, after, name):
    na = len(arrs)
    lands = [lax.empty(a.shape, a.dtype) for a in arrs]

    def body(*refs):
        ins, lnd = refs[:na], refs[na:2 * na]
        ssem, rsem, token = refs[2 * na + 1], refs[2 * na + 2], refs[-1]
        for a in range(na):
            _sibling_copy(ins[a], lnd[a], ssem, rsem, a).start()
        token[...] = jnp.zeros_like(token)

    outs = pl.pallas_call(
        body, name=name,
        in_specs=[_HBM_SPEC] * (2 * na) + [pl.BlockSpec(memory_space=pl.ANY)],
        out_specs=[_SEM_SPEC] * 2 + [_HBM_SPEC] * (2 * na) + [pl.BlockSpec(memory_space=pltpu.VMEM)],
        out_shape=[pltpu.SemaphoreType.DMA((na,))] * 2 + [pltpu.HBM(a.shape, a.dtype) for a in list(arrs) + lands]
        + [jax.ShapeDtypeStruct((8, 128), F32)],
        input_output_aliases={i: 2 + i for i in range(2 * na)},
        compiler_params=pltpu.CompilerParams(has_side_effects=_DATAFLOW),
    )(*[pltpu.with_memory_space_constraint(a, pltpu.HBM) for a in list(arrs) + lands], after)
    return (outs[2:2 + na], outs[2 + na:2 + 2 * na], outs[0], outs[1]), outs[-1]


def _swap_wait(state, after, name):
    srcs, lands, ssem, rsem = state
    na = len(srcs)

    def body(*refs):
        ins, lnd = refs[:na], refs[na:2 * na]
        ssem_ref, rsem_ref = refs[2 * na], refs[2 * na + 1]
        for a in range(na):
            cp = _sibling_copy(ins[a], lnd[a], ssem_ref, rsem_ref, a)
            cp.wait_send()
            cp.wait_recv()

    outs = pl.pallas_call(
        body, name=name,
        in_specs=[_HBM_SPEC] * (2 * na) + [_SEM_SPEC, _SEM_SPEC, pl.BlockSpec(memory_space=pl.ANY)],
        out_specs=[_HBM_SPEC] * (2 * na),
        out_shape=[pltpu.HBM(a.shape, a.dtype) for a in list(srcs) + list(lands)],
        input_output_aliases={i: i for i in range(2 * na)},
        compiler_params=pltpu.CompilerParams(has_side_effects=_DATAFLOW),
    )(*srcs, *lands, ssem, rsem, after)
    return outs[:na], outs[na:]


def _allreduce_small(rows_all, rows_sum):
    ra, c = rows_all.shape
    r = rows_sum.shape[0]
    ch = r // N_DEV
    assert ch % 8 == 0 and ch * N_DEV == r

    def body(a_ref, s_ref, all_ref, sum_ref, rbuf, red, sa, rva, sb, rvb, sc, rvc):
        mx, my, mc = _my_place()
        me = 4 * mx + 2 * my + mc
        mine = pl.ds(pl.multiple_of(me * ch, 8), ch)
        all_ref[me] = a_ref[...]
        rbuf[me] = s_ref[mine, :]

        def dev(t):
            tx, ty = _chip_coords(t // 2)
            return (tx, ty, t % 2)

        def everyone_else(fn):
            for t in range(N_DEV):
                @pl.when(t != me)
                def _():
                    fn(t)

        def copy_a(t, slot):
            return pltpu.make_async_remote_copy(src_ref=a_ref, dst_ref=all_ref.at[slot], send_sem=sa.at[t], recv_sem=rva.at[slot],
                                                device_id=dev(t), device_id_type=MESH)

        def copy_b(t, slot):
            return pltpu.make_async_remote_copy(src_ref=s_ref.at[pl.ds(t * ch, ch), :], dst_ref=rbuf.at[slot], send_sem=sb.at[t],
                                                recv_sem=rvb.at[slot], device_id=dev(t), device_id_type=MESH)

        def copy_c(t, chunk_start, slot):
            return pltpu.make_async_remote_copy(src_ref=red, dst_ref=sum_ref.at[pl.ds(chunk_start, ch), :], send_sem=sc.at[t],
                                                recv_sem=rvc.at[slot], device_id=dev(t), device_id_type=MESH)

        everyone_else(lambda t: (copy_a(t, me).start(), copy_b(t, me).start()))
        everyone_else(lambda t: (copy_a(t, t).wait_recv(), copy_b(t, t).wait_recv()))
        everyone_else(lambda t: (copy_a(t, me).wait_send(), copy_b(t, me).wait_send()))
        acc = rbuf[0]
        for t in range(1, N_DEV):
            acc = acc + rbuf[t]
        red[...] = acc
        sum_ref[mine, :] = acc
        everyone_else(lambda t: copy_c(t, pl.multiple_of(me * ch, 8), me).start())
        everyone_else(lambda t: copy_c(t, t * ch, t).wait_recv())
        everyone_else(lambda t: copy_c(t, pl.multiple_of(me * ch, 8), me).wait_send())

    vm = pl.BlockSpec(memory_space=pltpu.VMEM)
    return pl.pallas_call(
        body, name="allreduce_small",
        in_specs=[vm, vm],
        out_specs=[vm, vm],
        out_shape=[jax.ShapeDtypeStruct((N_DEV, ra, c), F32), jax.ShapeDtypeStruct((r, c), F32)],
        scratch_shapes=[pltpu.VMEM((N_DEV, ch, c), F32), pltpu.VMEM((ch, c), F32)] + [pltpu.SemaphoreType.DMA((N_DEV,))] * 6,
        compiler_params=pltpu.CompilerParams(vmem_limit_bytes=_vmem_limit((3 * r + 2 * N_DEV * ra) * c * 4 + (4 << 20))),
    )(rows_all, rows_sum)


def _pack(arrs, row_multiple=8):
    rows, layout, at = [], [], 0
    for a in arrs:
        n = a.size
        nr = -(-n // (8 * SMALL_COLS)) * 8
        flat = a.reshape(-1)
        if nr * SMALL_COLS != n:
            flat = jnp.pad(flat, (0, nr * SMALL_COLS - n))
        rows.append(flat.reshape(nr, SMALL_COLS))
        layout.append((at, nr, a.shape))
        at += nr
    pad = -at % row_multiple
    if pad:
        rows.append(jnp.zeros((pad, SMALL_COLS), F32))
    return jnp.concatenate(rows, axis=0), layout


def _unpack(buf, layout):
    out = []
    for at, nr, shape in layout:
        n = math.prod(shape)
        out.append(buf[at:at + nr].reshape(-1)[:n].reshape(shape))
    return out


SMALL_NAMES = ("mod_b", "mix_pre_g", "mix_post_g", "sgu_norm_g", "sgu_norm_b", "sgu_w", "sgu_b", "conv_b", "conv_norm_g",
               "conv_norm_b", "pool_w", "pool_scale", "branch_g", "ffn_pre_g", "ffn_post_g", "ffn_conv_b")
SHARDED_SMALL = ("conv_w", "ffn_conv_w")
WEIGHT_ORDER = ("mod_w", "mod_b", "mix_pre_g", "mix_post_g", "w_in", "sgu_norm_g", "sgu_norm_b", "sgu_w", "sgu_b", "conv_w",
                "conv_b", "conv_norm_g", "conv_norm_b", "pool_w", "pool_scale", "branch_g", "w_out", "ffn_pre_g", "ffn_post_g",
                "ffn_up", "ffn_conv_w", "ffn_conv_b", "ffn_down")


def _block_diag(blocks):
    n, a, b = blocks.shape
    eye = jnp.eye(n, dtype=blocks.dtype)
    return (eye[:, None, :, None] * blocks[:, :, None, :]).reshape(n * a, n * b)


def _diag_blocks(mat, n):
    a = mat.shape[0] // n
    return jnp.stack([mat[g * a:(g + 1) * a, g * a:(g + 1) * a] for g in range(n)])


def _step(x, c, loss_target, w, m, v):
    nl = w["mod_w"].shape[0]
    s, d = x.shape[1], x.shape[2]
    heads = SGU_WIDTH // HEAD_DIM
    groups = len(POOL_WINDOWS)
    mx, my, _ = _my_place()
    q = 2 * mx + my
    x0 = x.reshape(s, d)
    tgt = loss_target.reshape(s, d)

    nmod = w["mod_w"].shape[2]
    kin = w["w_in"].shape[2]
    inw = kin * N_CHIPS
    f2 = w["ffn_up"].shape[2] * N_CHIPS
    f = f2 // 2

    def wgroups(l):
        return [[jnp.swapaxes(w["w_in"][l], 0, 1).astype(BF16), w["conv_w"][l], w["ffn_conv_w"][l]], [w["w_out"][l].astype(BF16)],
                [w["ffn_up"][l].astype(BF16)], [w["ffn_down"][l].astype(BF16)]]

    gstates = {}
    (gstates[0, 0], gstates[0, 1]), gtoken = _exchange_start(wgroups(0)[:2], False, c, "gather_start_in_0")
    mod4, sc_all = _mod_forward(c + gtoken[0:1, 0:1], w["mod_w"], w["mod_b"].reshape(nl, N_CHIPS, 1, nmod))
    mod = mod4[:, :, 0, :].reshape(nl, N_MOD, 1, d)

    tril = jnp.tril(jnp.ones((CHUNK, CHUNK), bool))
    bd = _block_diag(jnp.ones((heads, HEAD_DIM, HEAD_DIM), BF16))

    def mixer_params(l, conv_w):
        wm = jnp.where(tril[None], w["sgu_w"][l], 0.0)
        pw = _block_diag(w["pool_w"][l])
        return dict(
            bd=bd, ng=w["sgu_norm_g"][l][None], nb=w["sgu_norm_b"][l][None],
            wm=wm.astype(BF16), wmt=jnp.swapaxes(wm, 1, 2).astype(BF16),
            bias=jnp.repeat(w["sgu_b"][l].T, HEAD_DIM, axis=1),
            cw=conv_w, cb=w["conv_b"][l][None], cng=w["conv_norm_g"][l][None], cnb=w["conv_norm_b"][l][None],
            pw=pw.astype(BF16), pwt=pw.T.astype(BF16), ps=w["pool_scale"][l][None], bg=w["branch_g"][l][None])

    saved = []
    xl = x0
    arrived = {0: list(_exchange_wait(gstates[0, 0], False, mod4, "gather_wait_in_0"))}
    arrived[0] += list(_exchange_wait(gstates[0, 1], False, arrived[0][0], "gather_wait_out_0"))
    for l in range(nl):
        sh1, sc1, g1, sh2, sc2, g2 = [mod[l, k] for k in range(N_MOD)]
        gpre1, gpost1 = w["mix_pre_g"][l][None], w["mix_post_g"][l][None]
        gpre2, gpost2 = w["ffn_pre_g"][l][None], w["ffn_post_g"][l][None]
        fcb = w["ffn_conv_b"][l][None]
        sh1_after, bg_after, g1_after, sh2_after, fcb_after = sh1, w["branch_g"][l][None], g1, sh2, fcb
        g_win, g_cw, g_fcw = arrived[l][:3]
        if l == 0:
            (gstates[0, 2],), tok = _exchange_start(wgroups(0)[2:3], False, arrived[0][3], "gather_start_up_0")
            sh1_after = sh1 + tok[0:1, 0:1]
        w_in = g_win.reshape(inw, d)
        conv_w = jnp.transpose(g_cw, (1, 0, 2)).reshape(CONV_K, CONV_WIDTH)
        ffn_cw = jnp.transpose(g_fcw, (1, 0, 2)).reshape(FFN_CONV_K, f2)
        mp = mixer_params(l, conv_w)
        z, h1 = _norm_mod_matmul(xl, gpre1, 1.0 + sc1, sh1_after, w_in[None], f"mix_in_{l}", transposed=True)
        w_out = arrived[l][3].reshape(d, d)
        ycat, cbo, dpool = _mixer_fwd(z, dict(mp, bg=bg_after), f"mixer_fwd_{l}")
        (up,) = _exchange_wait(gstates[l, 2], False, ycat, f"gather_wait_up_{l}")
        if l == 0:
            (gstates[0, 3],), tok = _exchange_start(wgroups(0)[3:4], False, up, "gather_start_down_0")
            g1_after = g1 + tok[0:1, 0:1]
        o, x1 = _matmul_norm_resid(ycat, w_out, xl, g1_after, gpost1, f"mix_out_{l}")
        if l + 1 < nl:
            (gstates[l + 1, 0], gstates[l + 1, 1]), tok = _exchange_start(wgroups(l + 1)[0:2], False, x1,
                                                                        f"gather_start_in_{l + 1}")
            sh2_after = sh2 + tok[0:1, 0:1]
        p, h2 = _norm_mod_matmul(x1, gpre2, 1.0 + sc2, sh2_after, up, f"ffn_in_{l}")
        if l + 1 < nl:
            nxt = _exchange_wait(gstates[l + 1, 0], False, p, f"gather_wait_in_{l + 1}")
            nxt_out = _exchange_wait(gstates[l + 1, 1], False, nxt[0], f"gather_wait_out_{l + 1}")
            arrived[l + 1] = list(nxt) + list(nxt_out)
            (gstates[l + 1, 2], gstates[l + 1, 3]), tok = _exchange_start(wgroups(l + 1)[2:4], False, nxt_out[0],
                                                                        f"gather_start_up_{l + 1}")
            fcb_after = fcb + tok[0:1, 0:1]
        act = _ffn_act_fwd(p, ffn_cw, fcb_after, f"ffn_act_{l}")
        (g_down,) = _exchange_wait(gstates[l, 3], False, act, f"gather_wait_down_{l}")
        down = g_down.reshape(f, d)
        qo, x2 = _matmul_norm_resid(act, down, x1, g2, gpost2, f"ffn_out_{l}")
        saved.append(dict(x=xl, z=z, h1=h1, ycat=ycat, cbo=cbo, dpool=dpool, o=o, x1=x1, p=p, h2=h2, act=act, qo=qo, mp=mp, fcb=fcb,
                          w_in=w_in, w_out=w_out, up=up, down=down, ffn_cw=ffn_cw,
                          mods=(sh1, sc1, g1, sh2, sc2, g2), gains=(gpre1, gpost1, gpre2, gpost2)))
        xl = x2

    dx, loss_row = _loss_head(xl, tgt)

    small = {n: [None] * nl for n in SMALL_NAMES + SHARDED_SMALL}
    dmods = [None] * nl
    tn = f2 // N_CHIPS
    sstates = {}
    token = None
    for l in reversed(range(nl)):
        sv = saved[l]
        sh1, sc1, g1, sh2, sc2, g2 = sv["mods"]
        gpre1, gpost1, gpre2, gpost2 = sv["gains"]
        if token is not None:
            g2 = g2 + token[0:1, 0:1]
        dq, dact, dg2, dgpost2 = _resid_bwd_matmul(dx, sv["qo"], g2, gpost2, sv["down"], f"ffn_out_bwd_{l}")
        g_down = _wgrad(sv["act"], dq, (f, lambda j: 0), (d, lambda j: 0), jax.ShapeDtypeStruct((f, d), BF16),
                        (1, lambda j: (0, 0)), (f, d), f"wgrad_ffn_down_{l}")
        dug, duv, dfwg, dfwv, dfbg, dfbv = _ffn_act_bwd(sv["p"], dact, sv["ffn_cw"], sv["fcb"], f"ffn_act_bwd_{l}")
        dx1, dp, dsh2, dsc2, dgpre2 = _ffn_in_bwd(dug, duv, sv["ffn_cw"], sv["up"], sv["x1"], dx, gpre2, 1.0 + sc2,
                                                  f"ffn_in_bwd_{l}")
        g_up = _wgrad(sv["h2"], dp, (d, lambda j: 0), (tn, lambda j: j), jax.ShapeDtypeStruct((N_CHIPS, d, tn), BF16),
                      (N_CHIPS, lambda j: (j, 0, 0)), (None, d, tn), f"wgrad_ffn_up_{l}")
        (sstates[l, 0],), token = _exchange_start([[g_down.reshape(N_CHIPS, f // N_CHIPS, d), g_up]], True, g_up,
                                                  f"scatter_start_ffn_{l}")
        do, dycat, dg1, dgpost1 = _resid_bwd_matmul(dx1, sv["o"], g1 + token[0:1, 0:1], gpost1, sv["w_out"],
                                                    f"mix_out_bwd_{l}")
        g_out = _wgrad(sv["ycat"], do, (d, lambda j: 0), (d, lambda j: 0), jax.ShapeDtypeStruct((d, d), BF16),
                       (1, lambda j: (0, 0)), (d, d), f"wgrad_w_out_{l}")
        (sstates[l, 1],), token = _exchange_start([[g_out.reshape(N_CHIPS, d // N_CHIPS, d)]], True, g_out,
                                                  f"scatter_start_out_{l}")
        mp_after = dict(sv["mp"], bg=sv["mp"]["bg"] + token[0:1, 0:1])
        (dza, dcb, dd, dbg, dwm, dbias, dng, dnb, dcng, dcnb, dps, dpw) = _mixer_bwd_a(sv["z"], sv["cbo"], sv["dpool"], dycat, mp_after, f"mixer_bwd_a_{l}")
        dx, dz, dsh1, dsc1, dgpre1, dcw, dcbias = _mixer_bwd_b(
            sv["z"], dza, dcb, dd, sv["x"], dx1, gpre1, 1.0 + sc1, sv["mp"]["cw"], sv["w_in"], f"mixer_bwd_b_{l}")
        g_in = _wgrad(dz, sv["h1"], (inw, lambda j: 0), (d, lambda j: 0), jax.ShapeDtypeStruct((inw, d), BF16),
                      (1, lambda j: (0, 0)), (inw, d), f"wgrad_w_in_{l}")
        g_in_parts = g_in.reshape(N_CHIPS, kin, d)
        if l > 0:
            (sstates[l, 2],), token = _exchange_start([[g_in_parts]], True, g_in_parts, f"scatter_start_in_{l}")

        dmods[l] = jnp.concatenate([dsh1, dsc1, dg1, dsh2, dsc2, dg2], axis=0)
        small["mix_pre_g"][l], small["mix_post_g"][l] = dgpre1[0], dgpost1[0]
        small["ffn_pre_g"][l], small["ffn_post_g"][l] = dgpre2[0], dgpost2[0]
        small["sgu_norm_g"][l], small["sgu_norm_b"][l] = dng[0], dnb[0]
        small["sgu_w"][l] = jnp.where(tril[None], dwm, 0.0)
        small["sgu_b"][l] = dbias.reshape(CHUNK, heads, HEAD_DIM).sum(-1).T
        small["conv_b"][l], small["conv_norm_g"][l], small["conv_norm_b"][l] = dcbias[0], dcng[0], dcnb[0]
        small["pool_w"][l], small["pool_scale"][l], small["branch_g"][l] = _diag_blocks(dpw, groups), dps[0], dbg[0]
        small["ffn_conv_b"][l] = jnp.concatenate([dfbg[0], dfbv[0]])
        small["conv_w"][l] = dcw
        small["ffn_conv_w"][l] = jnp.concatenate([dfwg, dfwv], axis=1)

    names = [n for n in SMALL_NAMES if n != "mod_b"] + list(SHARDED_SMALL)
    dmod_rows, _ = _pack([jnp.stack(dmods)])
    packed, layout = _pack([jnp.stack(dmods), loss_row] + [jnp.stack(small[n]) for n in names], 8 * N_DEV)
    gathered, summed = _allreduce_small(dmod_rows, packed)
    (sstates[0, 2],), token = _exchange_start([[g_in_parts]], True, summed, "scatter_start_in_0")
    parts = _unpack(summed, layout)
    loss = parts[1][0, 0]
    gsmall = dict(zip(names, parts[2:]))
    gsmall["mod_b"] = parts[0].reshape(nl, N_MOD * d)
    dmod_all = gathered[:, :nl * N_MOD].reshape(N_DEV, nl, N_MOD * d)
    dmod_mine = jnp.transpose(lax.dynamic_slice_in_dim(dmod_all, q * nmod, nmod, axis=2), (1, 0, 2))

    grads, deltas, new_m, new_v = {}, {}, {}, {}

    def put(name, res):
        grads[name], deltas[name], new_m[name], new_v[name] = res

    recv = dict(w_in=[None] * nl, w_out=[None] * nl, ffn_up=[None] * nl, ffn_down=[None] * nl)
    done = token
    for l in reversed(range(nl)):
        recv["ffn_down"][l], recv["ffn_up"][l] = _exchange_wait(sstates[l, 0], True, done, f"scatter_wait_ffn_{l}")
        (recv["w_out"][l],) = _exchange_wait(sstates[l, 1], True, recv["ffn_up"][l], f"scatter_wait_out_{l}")
        done = recv["w_out"][l]
    big = ("w_out", "ffn_up", "ffn_down", "w_in")
    mine = {n: _reduce4(recv[n], f"reduce4_{n}") for n in big[:3]}
    swap_a, token = _swap_start([mine[n] for n in big[:3]], mine["ffn_down"], "swap_start_a")

    put("mod_w", _modw_adamw(sc_all.T + token[0:1, 0:1], dmod_mine, w["mod_w"], m["mod_w"], v["mod_w"], "adamw_mod_w"))
    done = grads["mod_w"]
    for l in reversed(range(nl)):
        (recv["w_in"][l],) = _exchange_wait(sstates[l, 2], True, done, f"scatter_wait_in_{l}")
        done = recv["w_in"][l]
    mine["w_in"] = _reduce4(recv["w_in"], "reduce4_w_in")
    swap_b, token = _swap_start([mine["w_in"]], mine["w_in"], "swap_start_b")

    gsmall["conv_w"] = lax.dynamic_slice_in_dim(gsmall["conv_w"], q * (CONV_WIDTH // N_CHIPS), CONV_WIDTH // N_CHIPS, axis=2)
    gsmall["ffn_conv_w"] = lax.dynamic_slice_in_dim(gsmall["ffn_conv_w"], q * (f2 // N_CHIPS), f2 // N_CHIPS, axis=2)
    snames = SMALL_NAMES + SHARDED_SMALL
    res = _adamw_many([w[n] for n in snames], [m[n] for n in snames], [v[n] for n in snames], [gsmall[n] for n in snames],
                      "adamw_small")
    for n, (d_, m_, v_) in zip(snames, res):
        put(n, (gsmall[n], d_, m_, v_))

    sent, theirs = _swap_wait(swap_a, deltas["mod_b"], "swap_wait_a")
    sent_b, theirs_b = _swap_wait(swap_b, sent[0], "swap_wait_b")
    for n, a, b in zip(big, list(sent) + list(sent_b), list(theirs) + list(theirs_b)):
        if n == "w_in":
            res = _adamw(*[jnp.swapaxes(t[n], 1, 2) for t in (w, m, v)], [a, b], f"adamw_{n}")
            put(n, [jnp.swapaxes(r_, 1, 2) for r_ in res])
        else:
            put(n, _adamw(w[n], m[n], v[n], [a, b], f"adamw_{n}"))

    return (loss, dx.reshape(1, s, d), *[grads[n] for n in WEIGHT_ORDER], *[deltas[n] for n in WEIGHT_ORDER],
            *[new_m[n] for n in WEIGHT_ORDER], *[new_v[n] for n in WEIGHT_ORDER])


def kernel(x, c, mod_w, mod_b, mix_pre_g, mix_post_g, w_in, sgu_norm_g, sgu_norm_b, sgu_w, sgu_b, conv_w, conv_b, conv_norm_g, conv_norm_b, pool_w, pool_scale, branch_g, w_out, ffn_pre_g, ffn_post_g, ffn_up, ffn_conv_w, ffn_conv_b, ffn_down, loss_target, m_mod_w, m_mod_b, m_mix_pre_g, m_mix_post_g, m_w_in, m_sgu_norm_g, m_sgu_norm_b, m_sgu_w, m_sgu_b, m_conv_w, m_conv_b, m_conv_norm_g, m_conv_norm_b, m_pool_w, m_pool_scale, m_branch_g, m_w_out, m_ffn_pre_g, m_ffn_post_g, m_ffn_up, m_ffn_conv_w, m_ffn_conv_b, m_ffn_down, v_mod_w, v_mod_b, v_mix_pre_g, v_mix_post_g, v_w_in, v_sgu_norm_g, v_sgu_norm_b, v_sgu_w, v_sgu_b, v_conv_w, v_conv_b, v_conv_norm_g, v_conv_norm_b, v_pool_w, v_pool_scale, v_branch_g, v_w_out, v_ffn_pre_g, v_ffn_post_g, v_ffn_up, v_ffn_conv_w, v_ffn_conv_b, v_ffn_down):
    w = dict(mod_w=mod_w, mod_b=mod_b, mix_pre_g=mix_pre_g, mix_post_g=mix_post_g, w_in=w_in, sgu_norm_g=sgu_norm_g,
             sgu_norm_b=sgu_norm_b, sgu_w=sgu_w, sgu_b=sgu_b, conv_w=conv_w, conv_b=conv_b, conv_norm_g=conv_norm_g,
             conv_norm_b=conv_norm_b, pool_w=pool_w, pool_scale=pool_scale, branch_g=branch_g, w_out=w_out,
             ffn_pre_g=ffn_pre_g, ffn_post_g=ffn_post_g, ffn_up=ffn_up, ffn_conv_w=ffn_conv_w, ffn_conv_b=ffn_conv_b,
             ffn_down=ffn_down)
    m = dict(mod_w=m_mod_w, mod_b=m_mod_b, mix_pre_g=m_mix_pre_g, mix_post_g=m_mix_post_g, w_in=m_w_in,
             sgu_norm_g=m_sgu_norm_g, sgu_norm_b=m_sgu_norm_b, sgu_w=m_sgu_w, sgu_b=m_sgu_b, conv_w=m_conv_w,
             conv_b=m_conv_b, conv_norm_g=m_conv_norm_g, conv_norm_b=m_conv_norm_b, pool_w=m_pool_w,
             pool_scale=m_pool_scale, branch_g=m_branch_g, w_out=m_w_out, ffn_pre_g=m_ffn_pre_g, ffn_post_g=m_ffn_post_g,
             ffn_up=m_ffn_up, ffn_conv_w=m_ffn_conv_w, ffn_conv_b=m_ffn_conv_b, ffn_down=m_ffn_down)
    v = dict(mod_w=v_mod_w, mod_b=v_mod_b, mix_pre_g=v_mix_pre_g, mix_post_g=v_mix_post_g, w_in=v_w_in,
             sgu_norm_g=v_sgu_norm_g, sgu_norm_b=v_sgu_norm_b, sgu_w=v_sgu_w, sgu_b=v_sgu_b, conv_w=v_conv_w,
             conv_b=v_conv_b, conv_norm_g=v_conv_norm_g, conv_norm_b=v_conv_norm_b, pool_w=v_pool_w,
             pool_scale=v_pool_scale, branch_g=v_branch_g, w_out=v_w_out, ffn_pre_g=v_ffn_pre_g, ffn_post_g=v_ffn_post_g,
             ffn_up=v_ffn_up, ffn_conv_w=v_ffn_conv_w, ffn_conv_b=v_ffn_conv_b, ffn_down=v_ffn_down)
    return _step(x, c, loss_target, w, m, v)
```

```python
import functools
import math

import jax
import jax.numpy as jnp
from jax import lax
from jax.experimental import pallas as pl
from jax.experimental.pallas import tpu as pltpu

F32 = jnp.float32
BF16 = jnp.bfloat16
MESH = pl.DeviceIdType.MESH

EPS = 1e-6
HEAD_DIM = 64
CHUNK = 128
SGU_WIDTH = 384
CONV_WIDTH = 384
POOL_WIDTH = 256
POOL_WINDOWS = (2, 4, 8, 16)
CONV_K = 31
FFN_CONV_K = 3
N_MOD = 6
N_CHIPS = 4
N_DEV = 8

ADAM_LR = 0.001
ADAM_B1 = 0.9
ADAM_B2 = 0.999
ADAM_EPS = 1e-08
ADAM_WD = 0.01
ADAM_STEP = 10

MIX_HALO = 32
FFN_HALO = 8
FFN_ROWS = 16
FFN_UNROLL = 8
LANES = 128
CONV_ROWS = 32
SMALL_COLS = 1024
VMEM_BYTES_V7X = 64 * 1024 * 1024


def _vmem_limit(estimate_bytes):
    return int(min(max(estimate_bytes, VMEM_BYTES_V7X // 2), VMEM_BYTES_V7X - 8 * 1024 * 1024))


def _row_tile(s, want):
    return want if s % want == 0 else math.gcd(s, want)


def _rsum(v):
    return jnp.sum(v, axis=0, keepdims=True)


def _rmean(v):
    return jnp.mean(v, axis=-1, keepdims=True)


def _gelu(v):
    k = math.sqrt(2.0 / math.pi)
    return 0.5 * v * (1.0 + jnp.tanh(k * (v + 0.044715 * v * v * v)))


def _gelu_grad(v):
    k = math.sqrt(2.0 / math.pi)
    t = jnp.tanh(k * (v + 0.044715 * v * v * v))
    return 0.5 * (1.0 + t) + 0.5 * v * (1.0 - t * t) * (k * (1.0 + 3.0 * 0.044715 * v * v))


def _dot(a, b):
    return jnp.dot(a, b, preferred_element_type=F32)


def _dot_nt(a, b):
    return lax.dot_general(a, b, (((1,), (1,)), ((), ())), preferred_element_type=F32)


def _dot_tn(a, b):
    return lax.dot_general(a, b, (((0,), (0,)), ((), ())), preferred_element_type=F32)


def _group_mean(v, bd):
    hi = v.astype(BF16)
    lo = (v - hi.astype(F32)).astype(BF16)
    return (_dot(hi, bd) + _dot(lo, bd)) * (1.0 / HEAD_DIM)


def _const_spec(shape):
    nd = len(shape)
    return pl.BlockSpec(shape, lambda *_: (0,) * nd)


def _hbm_call(body, **kw):
    call = pl.pallas_call(body, **kw)
    return lambda *args: call(*[pltpu.with_memory_space_constraint(a, pltpu.HBM) for a in args])


def _norm_mod_matmul(x, gain, s1p, shift, w, name, transposed=False):
    s, d = x.shape
    nb = w.shape[0]
    tn = w.shape[1] if transposed else w.shape[2]
    ts = _row_tile(s, 512 if nb * tn <= 2048 else 256)

    def body(x_ref, g_ref, s_ref, b_ref, w_ref, z_ref, h_ref):
        xv = x_ref[...]
        r = lax.rsqrt(_rmean(xv * xv) + EPS)
        h = ((xv * r) * g_ref[...] * s_ref[...] + b_ref[...]).astype(BF16)
        h_ref[...] = h
        for j in range(nb):
            z_ref[:, j * tn:(j + 1) * tn] = _dot_nt(h, w_ref[j]) if transposed else _dot(h, w_ref[j])

    vec = pl.BlockSpec((1, d), lambda i: (0, 0))
    return _hbm_call(
        body, name=name,
        grid=(s // ts,),
        in_specs=[pl.BlockSpec((ts, d), lambda i: (i, 0)), vec, vec, vec,
                  pl.BlockSpec(w.shape, lambda i: (0, 0, 0), pipeline_mode=pl.Buffered(1))],
        out_specs=[pl.BlockSpec((ts, nb * tn), lambda i: (i, 0)), pl.BlockSpec((ts, d), lambda i: (i, 0))],
        out_shape=[jax.ShapeDtypeStruct((s, nb * tn), F32), jax.ShapeDtypeStruct((s, d), BF16)],
        compiler_params=pltpu.CompilerParams(
            dimension_semantics=("arbitrary",),
            vmem_limit_bytes=_vmem_limit(2 * (ts * d * 4 + ts * nb * tn * 4 + ts * d * 2) + nb * d * tn * 2 + 4 * ts * d * 4)),
    )(x, gain, s1p, shift, w)


def _matmul_norm_resid(a, w, xres, gate, gpost, name):
    s, k = a.shape
    d = w.shape[1]
    ts = _row_tile(s, 512)

    def body(a_ref, w_ref, x_ref, gate_ref, gp_ref, o_ref, xn_ref):
        o = _dot(a_ref[...], w_ref[...])
        o_ref[...] = o
        r = lax.rsqrt(_rmean(o * o) + EPS)
        xn_ref[...] = x_ref[...] + gate_ref[...] * ((o * r) * gp_ref[...])

    vec = pl.BlockSpec((1, d), lambda i: (0, 0))
    row = pl.BlockSpec((ts, d), lambda i: (i, 0))
    return _hbm_call(
        body, name=name,
        grid=(s // ts,),
        in_specs=[pl.BlockSpec((ts, k), lambda i: (i, 0)),
                  pl.BlockSpec((k, d), lambda i: (0, 0), pipeline_mode=pl.Buffered(1)), row, vec, vec],
        out_specs=[row, row],
        out_shape=[jax.ShapeDtypeStruct((s, d), F32)] * 2,
        compiler_params=pltpu.CompilerParams(
            dimension_semantics=("arbitrary",),
            vmem_limit_bytes=_vmem_limit(2 * (ts * k * 2 + 3 * ts * d * 4) + k * d * 2 + 4 * ts * d * 4)),
    )(a, w, xres, gate, gpost)


def _wgrad(a, b, acols, bcols, out_struct, out_index, out_block, name):
    s = a.shape[0]
    aw, afn = acols
    bw, bfn = bcols
    ts = _row_tile(s, 2048 if aw * bw <= 2 * 1024 * 1024 else 1024)
    nj = out_index[0]
    oidx = out_index[1]

    def body(a_ref, b_ref, o_ref, acc):
        i = pl.program_id(1)

        @pl.when(i == 0)
        def _():
            acc[...] = jnp.zeros_like(acc)

        acc[...] += _dot_tn(a_ref[...], b_ref[...])

        @pl.when(i == pl.num_programs(1) - 1)
        def _():
            o_ref[...] = acc[...].astype(o_ref.dtype)

    return _hbm_call(
        body, name=name,
        grid=(nj, s // ts),
        in_specs=[pl.BlockSpec((ts, aw), lambda j, i: (i, afn(j))), pl.BlockSpec((ts, bw), lambda j, i: (i, bfn(j)))],
        out_specs=pl.BlockSpec(out_block, lambda j, i: oidx(j)),
        out_shape=out_struct,
        scratch_shapes=[pltpu.VMEM((aw, bw), F32)],
        compiler_params=pltpu.CompilerParams(
            dimension_semantics=("arbitrary", "arbitrary"),
            vmem_limit_bytes=_vmem_limit(2 * (ts * aw * 2 + ts * bw * 2) + 3 * aw * bw * 4 + ts * aw * 4)),
    )(a, b)


def _loss_head(xo, tgt):
    s, d = xo.shape
    ts = _row_tile(s, 512)

    def body(x_ref, t_ref, dx_ref, l_ref, acc):
        i = pl.program_id(0)

        @pl.when(i == 0)
        def _():
            acc[...] = jnp.zeros_like(acc)

        e = x_ref[...] - t_ref[...]
        dx_ref[...] = e * (1.0 / d)
        acc[...] += _rsum(e * e)

        @pl.when(i == pl.num_programs(0) - 1)
        def _():
            tot = jnp.sum(acc[...], axis=-1, keepdims=True) * (0.5 / d)
            l_ref[...] = jnp.broadcast_to(tot, l_ref.shape)

    row = pl.BlockSpec((ts, d), lambda i: (i, 0))
    return _hbm_call(
        body, name="loss_head",
        grid=(s // ts,),
        in_specs=[row, row],
        out_specs=[row, pl.BlockSpec((1, SMALL_COLS), lambda i: (0, 0))],
        out_shape=[jax.ShapeDtypeStruct((s, d), F32), jax.ShapeDtypeStruct((1, SMALL_COLS), F32)],
        scratch_shapes=[pltpu.VMEM((1, d), F32)],
        compiler_params=pltpu.CompilerParams(dimension_semantics=("arbitrary",)),
    )(xo, tgt)


def _resid_bwd_matmul(dxn, o, gate, gpost, w, name):
    s, d = dxn.shape
    k = w.shape[0]
    ts = _row_tile(s, 512)

    def body(dx_ref, o_ref, gate_ref, gp_ref, w_ref, do_ref, da_ref, dgate_ref, dgp_ref):
        i = pl.program_id(0)

        @pl.when(i == 0)
        def _():
            dgate_ref[...] = jnp.zeros_like(dgate_ref)
            dgp_ref[...] = jnp.zeros_like(dgp_ref)

        dx = dx_ref[...]
        o = o_ref[...]
        r = lax.rsqrt(_rmean(o * o) + EPS)
        on = o * r
        dgate_ref[...] += _rsum(dx * (on * gp_ref[...]))
        don = dx * gate_ref[...]
        dgp_ref[...] += _rsum(don * on)
        t = don * gp_ref[...]
        do = (r * (t - on * _rmean(t * on))).astype(BF16)
        do_ref[...] = do
        da_ref[...] = _dot_nt(do, w_ref[...])

    vec = pl.BlockSpec((1, d), lambda i: (0, 0))
    row = pl.BlockSpec((ts, d), lambda i: (i, 0))
    return _hbm_call(
        body, name=name,
        grid=(s // ts,),
        in_specs=[row, row, vec, vec, pl.BlockSpec((k, d), lambda i: (0, 0), pipeline_mode=pl.Buffered(1))],
        out_specs=[row, pl.BlockSpec((ts, k), lambda i: (i, 0)), vec, vec],
        out_shape=[jax.ShapeDtypeStruct((s, d), BF16), jax.ShapeDtypeStruct((s, k), F32),
                   jax.ShapeDtypeStruct((1, d), F32), jax.ShapeDtypeStruct((1, d), F32)],
        compiler_params=pltpu.CompilerParams(
            dimension_semantics=("arbitrary",),
            vmem_limit_bytes=_vmem_limit(2 * (2 * ts * d * 4 + ts * d * 2 + ts * k * 4) + d * k * 2 + 6 * ts * d * 4)),
    )(dxn, o, gate, gpost, w)


def _norm_mod_bwd(dh, xv, gain, s1p, dres):
    r = lax.rsqrt(_rmean(xv * xv) + EPS)
    xn = xv * r
    dshift = _rsum(dh)
    t = dh * xn
    dscale = _rsum(t * gain)
    dgain = _rsum(t * s1p)
    dxn = dh * (gain * s1p)
    dx = r * (dxn - xn * _rmean(dxn * xn)) + dres
    return dx, dshift, dscale, dgain


def _lane_lt(shape, bound):
    return lax.broadcasted_iota(jnp.int32, shape, 1) < bound


def _sgu_forward(z_ref, bd_ref, ng_ref, nb_ref, wm_ref, bias_ref, ts, ya_s, f_s):
    u = _gelu(z_ref[:, 0:SGU_WIDTH])
    v = _gelu(z_ref[:, SGU_WIDTH:2 * SGU_WIDTH])
    bd = bd_ref[...]
    vc = v - _group_mean(v, bd)
    rstd = lax.rsqrt(_group_mean(vc * vc, bd) + EPS)
    vhat = vc * rstd
    vn = (vhat * ng_ref[...] + nb_ref[...]).astype(BF16)
    left = _lane_lt((CHUNK, CHUNK), HEAD_DIM)
    for n in range(ts // CHUNK):
        rows = slice(n * CHUNK, (n + 1) * CHUNK)
        for p in range(SGU_WIDTH // CHUNK):
            cols = slice(p * CHUNK, (p + 1) * CHUNK)
            blk = vn[rows, cols]
            f = jnp.where(left, _dot(wm_ref[2 * p], blk), _dot(wm_ref[2 * p + 1], blk)) + bias_ref[:, cols]
            if f_s is not None:
                f_s[rows, cols] = f
            ya_s[rows, cols] = u[rows, cols] * f
    return u, vhat, rstd, vn


def _shifted_copies(ext, ext8, ts):
    for b in range(1, 8):
        ext8[b - 1] = ext[pl.ds(b, ts + MIX_HALO - 8), :]


def _rows_at(ext, ext8, start, nrows):
    b = start % 8
    return ext[pl.ds(start, nrows), :] if b == 0 else ext8[b - 1, pl.ds(start - b, nrows), :]


def _conv31_forward(z_ref, zh_ref, first, cw_ref, cb_ref, ts, ext_b, ext8, cbs):
    a = z_ref[:, 2 * SGU_WIDTH:2 * SGU_WIDTH + CONV_WIDTH]
    g = z_ref[:, 2 * SGU_WIDTH + CONV_WIDTH:2 * SGU_WIDTH + 2 * CONV_WIDTH]
    ah = zh_ref[:, 2 * SGU_WIDTH:2 * SGU_WIDTH + CONV_WIDTH]
    gh = zh_ref[:, 2 * SGU_WIDTH + CONV_WIDTH:2 * SGU_WIDTH + 2 * CONV_WIDTH]
    ext_b[pl.ds(0, MIX_HALO), :] = jnp.where(first, 0.0, ah * jax.nn.sigmoid(gh))
    ext_b[pl.ds(MIX_HALO, ts), :] = a * jax.nn.sigmoid(g)
    _shifted_copies(ext_b, ext8, ts)
    for r in range(ts // CONV_ROWS):
        acc = jnp.broadcast_to(cb_ref[...], (CONV_ROWS, CONV_WIDTH))
        for k in range(CONV_K):
            acc = acc + cw_ref[k:k + 1, :] * _rows_at(ext_b, ext8, MIX_HALO - (CONV_K - 1) + k + r * CONV_ROWS, CONV_ROWS)
        cbs[pl.ds(r * CONV_ROWS, CONV_ROWS), :] = acc


def _pool_counts(i, ts):
    pos1 = (i * ts + 1 + lax.broadcasted_iota(jnp.int32, (ts, POOL_WIDTH), 0)).astype(F32)
    lane = lax.broadcasted_iota(jnp.int32, (ts, POOL_WIDTH), 1)
    gdim = POOL_WIDTH // len(POOL_WINDOWS)
    win = jnp.where(lane < gdim, float(POOL_WINDOWS[0]),
                    jnp.where(lane < 2 * gdim, float(POOL_WINDOWS[1]),
                              jnp.where(lane < 3 * gdim, float(POOL_WINDOWS[2]), float(POOL_WINDOWS[3]))))
    return jnp.minimum(pos1, win)


def _window_sums(ext, base, ts, sign):
    lane = lax.broadcasted_iota(jnp.int32, (ts, POOL_WIDTH), 1)
    gdim = POOL_WIDTH // len(POOL_WINDOWS)
    run = jnp.zeros((ts, POOL_WIDTH), F32)
    out = jnp.zeros((ts, POOL_WIDTH), F32)
    for m in range(POOL_WINDOWS[-1]):
        run = run + ext[pl.ds(base + sign * m, ts), :]
        for gi, win in enumerate(POOL_WINDOWS):
            if m == win - 1:
                out = jnp.where((lane >= gi * gdim) & (lane < (gi + 1) * gdim), run, out)
    return out


def _pool_forward(z_ref, zh_ref, first, i, ts, ext_c):
    c0 = 2 * SGU_WIDTH + 2 * CONV_WIDTH
    zc = z_ref[:, c0:c0 + POOL_WIDTH]
    ext_c[pl.ds(0, MIX_HALO), :] = jnp.where(first, 0.0, zh_ref[:, c0:c0 + POOL_WIDTH])
    ext_c[pl.ds(MIX_HALO, ts), :] = zc
    sums = _window_sums(ext_c, MIX_HALO, ts, -1)
    return sums / _pool_counts(i, ts) - zc


def _layer_norm_rows(v):
    mu = _rmean(v)
    vc = v - mu
    rstd = lax.rsqrt(_rmean(vc * vc) + EPS)
    return vc * rstd, rstd


def _mixer_specs(s, ts, width):
    nbh = ts // MIX_HALO
    tile = pl.BlockSpec((ts, width), lambda i: (i, 0))
    prev = pl.BlockSpec((MIX_HALO, width), lambda i: (jnp.maximum(i * nbh - 1, 0), 0))
    nxt = pl.BlockSpec((MIX_HALO, width), lambda i: (jnp.minimum((i + 1) * nbh, s // MIX_HALO - 1), 0))
    return tile, prev, nxt


def _mixer_fwd(z, mp, name):
    s, inw = z.shape
    d = SGU_WIDTH + CONV_WIDTH + POOL_WIDTH
    ts = _row_tile(s, 256)

    def body(z_ref, zh_ref, bd_ref, ng_ref, nb_ref, wm_ref, bias_ref, cw_ref, cb_ref, cng_ref, cnb_ref,
             pw_ref, ps_ref, bg_ref, y_ref, cbs, dpool_ref, ya_s, ext_b, ext_c, ext8):
        i = pl.program_id(0)
        first = i == 0
        _sgu_forward(z_ref, bd_ref, ng_ref, nb_ref, wm_ref, bias_ref, ts, ya_s, None)
        ya = ya_s[...]
        ra = lax.rsqrt(_rmean(ya * ya) + EPS)
        y_ref[:, 0:SGU_WIDTH] = ((ya * ra) * bg_ref[:, 0:SGU_WIDTH]).astype(BF16)

        _conv31_forward(z_ref, zh_ref, first, cw_ref, cb_ref, ts, ext_b, ext8, cbs)
        chat, _ = _layer_norm_rows(cbs[...])
        lin = chat * cng_ref[...] + cnb_ref[...]
        yb = lin * jax.nn.sigmoid(lin)
        rb = lax.rsqrt(_rmean(yb * yb) + EPS)
        y_ref[:, SGU_WIDTH:SGU_WIDTH + CONV_WIDTH] = ((yb * rb) * bg_ref[:, SGU_WIDTH:SGU_WIDTH + CONV_WIDTH]).astype(BF16)

        dpool = _pool_forward(z_ref, zh_ref, first, i, ts, ext_c)
        dpool_ref[...] = dpool
        yc = _dot(dpool.astype(BF16), pw_ref[...]) * ps_ref[...]
        rc = lax.rsqrt(_rmean(yc * yc) + EPS)
        y_ref[:, SGU_WIDTH + CONV_WIDTH:d] = ((yc * rc) * bg_ref[:, SGU_WIDTH + CONV_WIDTH:d]).astype(BF16)

    tile, prev, _ = _mixer_specs(s, ts, inw)
    consts = [mp["bd"], mp["ng"], mp["nb"], mp["wm"], mp["bias"], mp["cw"], mp["cb"], mp["cng"], mp["cnb"],
              mp["pw"], mp["ps"], mp["bg"]]
    return _hbm_call(
        body, name=name,
        grid=(s // ts,),
        in_specs=[tile, prev] + [_const_spec(c.shape) for c in consts],
        out_specs=[pl.BlockSpec((ts, d), lambda i: (i, 0)), pl.BlockSpec((ts, CONV_WIDTH), lambda i: (i, 0)),
                   pl.BlockSpec((ts, POOL_WIDTH), lambda i: (i, 0))],
        out_shape=[jax.ShapeDtypeStruct((s, d), BF16), jax.ShapeDtypeStruct((s, CONV_WIDTH), F32),
                   jax.ShapeDtypeStruct((s, POOL_WIDTH), F32)],
        scratch_shapes=[pltpu.VMEM((ts, SGU_WIDTH), F32), pltpu.VMEM((ts + MIX_HALO, CONV_WIDTH), F32),
                        pltpu.VMEM((ts + MIX_HALO, POOL_WIDTH), F32), pltpu.VMEM((7, ts + MIX_HALO - 8, CONV_WIDTH), F32)],
        compiler_params=pltpu.CompilerParams(dimension_semantics=("arbitrary",),
                                             vmem_limit_bytes=_vmem_limit(16 * ts * inw * 4)),
    )(z, z, *consts)


def _mixer_bwd_a(z, cb, dpool, dy, mp, name):
    s, inw = z.shape
    d = SGU_WIDTH + CONV_WIDTH + POOL_WIDTH
    ts = _row_tile(s, 256)
    nchunk = ts // CHUNK

    def rms_bwd(dyn, y, g):
        r = lax.rsqrt(_rmean(y * y) + EPS)
        yn = y * r
        dg = _rsum(dyn * yn)
        t = dyn * g
        return r * (t - yn * _rmean(t * yn)), dg

    def body(z_ref, cbs, dpool_ref, dy_ref, bd_ref, ng_ref, nb_ref, wm_ref, wmt_ref, bias_ref, cng_ref, cnb_ref,
             pw_ref, pwt_ref, ps_ref, bg_ref,
             dza_ref, dcb_ref, dd_ref, dbg_ref, dwm_ref, dbias_ref, dng_ref, dnb_ref, dcng_ref, dcnb_ref, dps_ref, dpw_ref,
             ya_s, f_s, dvn_s):
        i = pl.program_id(0)
        first = i == 0

        @pl.when(first)
        def _():
            for ref in (dwm_ref, dbias_ref, dng_ref, dnb_ref, dcng_ref, dcnb_ref, dps_ref, dpw_ref):
                ref[...] = jnp.zeros_like(ref)

        u, vhat, rstd, vn = _sgu_forward(z_ref, bd_ref, ng_ref, nb_ref, wm_ref, bias_ref, ts, ya_s, f_s)
        dya, dbg_a = rms_bwd(dy_ref[:, 0:SGU_WIDTH], ya_s[...], bg_ref[:, 0:SGU_WIDTH])
        du = dya * f_s[...]
        df = dya * u
        dfb = df.astype(BF16)
        left = _lane_lt((CHUNK, CHUNK), HEAD_DIM)
        zero = jnp.zeros((CHUNK, CHUNK), BF16)
        dbias = jnp.zeros((CHUNK, SGU_WIDTH), F32)
        for n in range(nchunk):
            rows = slice(n * CHUNK, (n + 1) * CHUNK)
            dbias = dbias + df[rows, :]
            for p in range(SGU_WIDTH // CHUNK):
                cols = slice(p * CHUNK, (p + 1) * CHUNK)
                dblk = dfb[rows, cols]
                vblk = vn[rows, cols]
                dwm_ref[2 * p] += _dot_nt(jnp.where(left, dblk, zero), vblk)
                dwm_ref[2 * p + 1] += _dot_nt(jnp.where(left, zero, dblk), vblk)
                dvn_s[rows, cols] = jnp.where(left, _dot(wmt_ref[2 * p], dblk), _dot(wmt_ref[2 * p + 1], dblk))
        dbias_ref[...] += dbias
        dvn = dvn_s[...]
        dng_ref[...] += _rsum(dvn * vhat)
        dnb_ref[...] += _rsum(dvn)
        dvh = dvn * ng_ref[...]
        bd = bd_ref[...]
        dv = rstd * (dvh - _group_mean(dvh, bd) - vhat * _group_mean(dvh * vhat, bd))
        dza_ref[:, 0:SGU_WIDTH] = (du * _gelu_grad(z_ref[:, 0:SGU_WIDTH])).astype(BF16)
        dza_ref[:, SGU_WIDTH:2 * SGU_WIDTH] = (dv * _gelu_grad(z_ref[:, SGU_WIDTH:2 * SGU_WIDTH])).astype(BF16)

        chat, crstd = _layer_norm_rows(cbs[...])
        lin = chat * cng_ref[...] + cnb_ref[...]
        sl = jax.nn.sigmoid(lin)
        dyb, dbg_b = rms_bwd(dy_ref[:, SGU_WIDTH:SGU_WIDTH + CONV_WIDTH], lin * sl, bg_ref[:, SGU_WIDTH:SGU_WIDTH + CONV_WIDTH])
        dlin = dyb * (sl * (1.0 + lin * (1.0 - sl)))
        dcng_ref[...] += _rsum(dlin * chat)
        dcnb_ref[...] += _rsum(dlin)
        dch = dlin * cng_ref[...]
        dcb_ref[...] = crstd * (dch - _rmean(dch) - chat * _rmean(dch * chat))

        dpb = dpool_ref[...].astype(BF16)
        ycp = _dot(dpb, pw_ref[...])
        dyc, dbg_c = rms_bwd(dy_ref[:, SGU_WIDTH + CONV_WIDTH:d], ycp * ps_ref[...], bg_ref[:, SGU_WIDTH + CONV_WIDTH:d])
        dps_ref[...] += _rsum(dyc * ycp)
        dycp = (dyc * ps_ref[...]).astype(BF16)
        dpw_ref[...] += _dot_tn(dpb, dycp)
        dd_ref[...] = _dot(dycp, pwt_ref[...])

        @pl.when(first)
        def _():
            dbg_ref[...] = jnp.zeros_like(dbg_ref)

        dbg_ref[:, 0:SGU_WIDTH] += dbg_a
        dbg_ref[:, SGU_WIDTH:SGU_WIDTH + CONV_WIDTH] += dbg_b
        dbg_ref[:, SGU_WIDTH + CONV_WIDTH:d] += dbg_c

    tile, _, _ = _mixer_specs(s, ts, inw)
    consts = [mp["bd"], mp["ng"], mp["nb"], mp["wm"], mp["wmt"], mp["bias"], mp["cng"], mp["cnb"],
              mp["pw"], mp["pwt"], mp["ps"], mp["bg"]]
    acc_shapes = [(1, d), (2 * (SGU_WIDTH // CHUNK), CHUNK, CHUNK), (CHUNK, SGU_WIDTH), (1, SGU_WIDTH), (1, SGU_WIDTH),
                  (1, CONV_WIDTH), (1, CONV_WIDTH), (1, POOL_WIDTH), (POOL_WIDTH, POOL_WIDTH)]
    return _hbm_call(
        body, name=name,
        grid=(s // ts,),
        in_specs=[tile, pl.BlockSpec((ts, CONV_WIDTH), lambda i: (i, 0)), pl.BlockSpec((ts, POOL_WIDTH), lambda i: (i, 0)),
                  pl.BlockSpec((ts, d), lambda i: (i, 0))] + [_const_spec(c.shape) for c in consts],
        out_specs=[pl.BlockSpec((ts, 2 * SGU_WIDTH), lambda i: (i, 0)), pl.BlockSpec((ts, CONV_WIDTH), lambda i: (i, 0)),
                   pl.BlockSpec((ts, POOL_WIDTH), lambda i: (i, 0))] + [_const_spec(a) for a in acc_shapes],
        out_shape=[jax.ShapeDtypeStruct((s, 2 * SGU_WIDTH), BF16), jax.ShapeDtypeStruct((s, CONV_WIDTH), F32),
                   jax.ShapeDtypeStruct((s, POOL_WIDTH), F32)] + [jax.ShapeDtypeStruct(a, F32) for a in acc_shapes],
        scratch_shapes=[pltpu.VMEM((ts, SGU_WIDTH), F32), pltpu.VMEM((ts, SGU_WIDTH), F32), pltpu.VMEM((ts, SGU_WIDTH), F32)],
        compiler_params=pltpu.CompilerParams(dimension_semantics=("arbitrary",),
                                             vmem_limit_bytes=_vmem_limit(24 * ts * inw * 4)),
    )(z, cb, dpool, dy, *consts)


def _mixer_bwd_b(z, dza, dcb, dd, x, dres, gain, s1p, cw, w, name):
    s, inw = z.shape
    d = x.shape[1]
    ts = _row_tile(s, 256)
    c0 = 2 * SGU_WIDTH
    c1 = c0 + 2 * CONV_WIDTH

    def body(z_ref, zh_ref, dza_ref, dcb_ref, dcbn_ref, dd_ref, ddn_ref, x_ref, dres_ref, g_ref, s_ref, cw_ref, w_ref,
             dx_ref, dz_ref, dsh_ref, dsc_ref, dg_ref, dcw_ref, dcbias_ref, ext_b, ext_n, ext_e, ext8):
        i = pl.program_id(0)
        first = i == 0
        last = i == pl.num_programs(0) - 1

        @pl.when(first)
        def _():
            for ref in (dsh_ref, dsc_ref, dg_ref, dcw_ref, dcbias_ref):
                ref[...] = jnp.zeros_like(ref)

        a = z_ref[:, c0:c0 + CONV_WIDTH]
        sg = jax.nn.sigmoid(z_ref[:, c0 + CONV_WIDTH:c1])
        ah = zh_ref[:, c0:c0 + CONV_WIDTH]
        gh = zh_ref[:, c0 + CONV_WIDTH:c1]
        ext_b[pl.ds(0, MIX_HALO), :] = jnp.where(first, 0.0, ah * jax.nn.sigmoid(gh))
        ext_b[pl.ds(MIX_HALO, ts), :] = a * sg
        _shifted_copies(ext_b, ext8, ts)
        dcbv = dcb_ref[...]
        dcbias_ref[...] += _rsum(dcbv)
        for k in range(CONV_K):
            dcw_ref[k:k + 1, :] += _rsum(dcbv * _rows_at(ext_b, ext8, MIX_HALO - (CONV_K - 1) + k, ts))

        ext_n[pl.ds(0, ts), :] = dcbv
        ext_n[pl.ds(ts, MIX_HALO), :] = jnp.where(last, 0.0, dcbn_ref[...])
        _shifted_copies(ext_n, ext8, ts)
        for r in range(ts // CONV_ROWS):
            acc = jnp.zeros((CONV_ROWS, CONV_WIDTH), F32)
            for k in range(CONV_K):
                acc = acc + cw_ref[k:k + 1, :] * _rows_at(ext_n, ext8, CONV_K - 1 - k + r * CONV_ROWS, CONV_ROWS)
            rows = pl.ds(r * CONV_ROWS, CONV_ROWS)
            ar = z_ref[rows, c0:c0 + CONV_WIDTH]
            sr = jax.nn.sigmoid(z_ref[rows, c0 + CONV_WIDTH:c1])
            dz_ref[rows, c0:c0 + CONV_WIDTH] = (acc * sr).astype(BF16)
            dz_ref[rows, c0 + CONV_WIDTH:c1] = (acc * ar * sr * (1.0 - sr)).astype(BF16)

        ddv = dd_ref[...]
        ext_e[pl.ds(0, ts), :] = ddv / _pool_counts(i, ts)
        nh = (i + 1) * ts + lax.broadcasted_iota(jnp.int32, (MIX_HALO, POOL_WIDTH), 0)
        lane = lax.broadcasted_iota(jnp.int32, (MIX_HALO, POOL_WIDTH), 1)
        gdim = POOL_WIDTH // len(POOL_WINDOWS)
        winh = jnp.where(lane < gdim, float(POOL_WINDOWS[0]),
                         jnp.where(lane < 2 * gdim, float(POOL_WINDOWS[1]),
                                   jnp.where(lane < 3 * gdim, float(POOL_WINDOWS[2]), float(POOL_WINDOWS[3]))))
        cnth = jnp.minimum((nh + 1).astype(F32), winh)
        ext_e[pl.ds(ts, MIX_HALO), :] = jnp.where(last, 0.0, ddn_ref[...] / cnth)
        dz_ref[:, c1:inw] = (_window_sums(ext_e, 0, ts, 1) - ddv).astype(BF16)
        dz_ref[:, 0:c0] = dza_ref[...]

        dh = _dot(dz_ref[...], w_ref[...])
        dx, dsh, dsc, dg = _norm_mod_bwd(dh, x_ref[...], g_ref[...], s_ref[...], dres_ref[...])
        dx_ref[...] = dx
        dsh_ref[...] += dsh
        dsc_ref[...] += dsc
        dg_ref[...] += dg

    tile, prev, _ = _mixer_specs(s, ts, inw)
    _, _, nxt_b = _mixer_specs(s, ts, CONV_WIDTH)
    _, _, nxt_c = _mixer_specs(s, ts, POOL_WIDTH)
    row = pl.BlockSpec((ts, d), lambda i: (i, 0))
    vec = pl.BlockSpec((1, d), lambda i: (0, 0))
    return _hbm_call(
        body, name=name,
        grid=(s // ts,),
        in_specs=[tile, prev, pl.BlockSpec((ts, c0), lambda i: (i, 0)),
                  pl.BlockSpec((ts, CONV_WIDTH), lambda i: (i, 0)), nxt_b,
                  pl.BlockSpec((ts, POOL_WIDTH), lambda i: (i, 0)), nxt_c,
                  row, row, vec, vec, _const_spec(cw.shape),
                  pl.BlockSpec(w.shape, lambda i: (0, 0), pipeline_mode=pl.Buffered(1))],
        out_specs=[row, pl.BlockSpec((ts, inw), lambda i: (i, 0)), vec, vec, vec,
                   _const_spec((CONV_K, CONV_WIDTH)), _const_spec((1, CONV_WIDTH))],
        out_shape=[jax.ShapeDtypeStruct((s, d), F32), jax.ShapeDtypeStruct((s, inw), BF16)]
        + [jax.ShapeDtypeStruct((1, d), F32)] * 3
        + [jax.ShapeDtypeStruct((CONV_K, CONV_WIDTH), F32), jax.ShapeDtypeStruct((1, CONV_WIDTH), F32)],
        scratch_shapes=[pltpu.VMEM((ts + MIX_HALO, CONV_WIDTH), F32), pltpu.VMEM((ts + MIX_HALO, CONV_WIDTH), F32),
                        pltpu.VMEM((ts + MIX_HALO, POOL_WIDTH), F32), pltpu.VMEM((7, ts + MIX_HALO - 8, CONV_WIDTH), F32)],
        compiler_params=pltpu.CompilerParams(dimension_semantics=("arbitrary",),
                                             vmem_limit_bytes=_vmem_limit(18 * ts * inw * 4 + inw * d * 2)),
    )(z, z, dza, dcb, dcb, dd, dd, x, dres, gain, s1p, cw, w)


def _ffn_specs(s, ts, tc, half_blocks):
    nbh = ts // FFN_HALO

    def tile(off):
        return pl.BlockSpec((ts, tc), lambda j, i: (i, j + off))

    def prev(off):
        return pl.BlockSpec((FFN_HALO, tc), lambda j, i: (jnp.maximum(i * nbh - 1, 0), j + off))

    def vec(rows, off):
        return pl.BlockSpec((rows, tc), lambda j, i: (0, j + off))

    return tile, prev, vec


def _rows_before(cur, prev, k):
    row = lax.broadcasted_iota(jnp.int32, cur.shape, 0)
    return jnp.where(row >= k, pltpu.roll(cur, k, 0), pltpu.roll(prev, k, 0))


def _conv3_rows(cur, prev, w_ref, b_ref, cols):
    x1 = _rows_before(cur, prev, 1)
    x2 = _rows_before(cur, prev, 2)
    u = b_ref[:, cols] + w_ref[2:3, cols] * cur + w_ref[1:2, cols] * x1 + w_ref[0:1, cols] * x2
    return u, x2, x1


def _halo_chunk(h_ref, cols, first):
    h = jnp.where(first, 0.0, h_ref[:, cols])
    return jnp.concatenate([h] * (FFN_ROWS // FFN_HALO), axis=0)


def _ffn_act_fwd(p, cw, cb, name):
    s, f2 = p.shape
    f = f2 // 2
    tc = f // 2
    hb = f // tc
    ts = _row_tile(s, 256)

    def body(pg_ref, pgh_ref, pv_ref, pvh_ref, wg_ref, wv_ref, bg_ref, bv_ref, act_ref):
        first = pl.program_id(1) == 0
        for c in range(tc // LANES):
            cols = slice(c * LANES, (c + 1) * LANES)

            def chunk(r, carry, cols=cols):
                pg_prev, pv_prev = carry
                rows = pl.ds(pl.multiple_of(r * FFN_ROWS, FFN_ROWS), FFN_ROWS)
                pg = pg_ref[rows, cols]
                pv = pv_ref[rows, cols]
                ug, _, _ = _conv3_rows(pg, pg_prev, wg_ref, bg_ref, cols)
                uv, _, _ = _conv3_rows(pv, pv_prev, wv_ref, bv_ref, cols)
                act_ref[rows, cols] = (_gelu(ug) * uv).astype(BF16)
                return pg, pv

            def step(r, carry, chunk=chunk):
                for u in range(FFN_UNROLL):
                    carry = chunk(r * FFN_UNROLL + u, carry)
                return carry

            lax.fori_loop(0, ts // (FFN_ROWS * FFN_UNROLL), step,
                          (_halo_chunk(pgh_ref, cols, first), _halo_chunk(pvh_ref, cols, first)))

    tile, prev, vec = _ffn_specs(s, ts, tc, hb)
    return _hbm_call(
        body, name=name,
        grid=(hb, s // ts),
        in_specs=[tile(0), prev(0), tile(hb), prev(hb), vec(FFN_CONV_K, 0), vec(FFN_CONV_K, hb), vec(1, 0), vec(1, hb)],
        out_specs=pl.BlockSpec((ts, tc), lambda j, i: (i, j)),
        out_shape=jax.ShapeDtypeStruct((s, f), BF16),
        compiler_params=pltpu.CompilerParams(dimension_semantics=("arbitrary", "arbitrary"),
                                             vmem_limit_bytes=_vmem_limit(8 * ts * tc * 4)),
    )(p, p, p, p, cw, cw, cb, cb)


def _ffn_act_bwd(p, dact, cw, cb, name):
    s, f2 = p.shape
    f = f2 // 2
    tc = f // 2
    hb = f // tc
    ts = _row_tile(s, 256)

    def body(pg_ref, pgh_ref, pv_ref, pvh_ref, da_ref, wg_ref, wv_ref, bg_ref, bv_ref,
             dug_ref, duv_ref, dwg_ref, dwv_ref, dbg_ref, dbv_ref):
        first = pl.program_id(1) == 0

        @pl.when(first)
        def _():
            for ref in (dwg_ref, dwv_ref, dbg_ref, dbv_ref):
                ref[...] = jnp.zeros_like(ref)

        zero = jnp.zeros((FFN_ROWS, LANES), F32)
        for c in range(tc // LANES):
            cols = slice(c * LANES, (c + 1) * LANES)

            def chunk(r, carry, cols=cols):
                pg_prev, pv_prev, ag0, ag1, ag2, av0, av1, av2, sg, sv = carry
                rows = pl.ds(pl.multiple_of(r * FFN_ROWS, FFN_ROWS), FFN_ROWS)
                pg = pg_ref[rows, cols]
                pv = pv_ref[rows, cols]
                ug, pg2, pg1 = _conv3_rows(pg, pg_prev, wg_ref, bg_ref, cols)
                uv, pv2, pv1 = _conv3_rows(pv, pv_prev, wv_ref, bv_ref, cols)
                da = da_ref[rows, cols]
                dug = da * uv * _gelu_grad(ug)
                duv = da * _gelu(ug)
                dug_ref[rows, cols] = dug
                duv_ref[rows, cols] = duv
                return (pg, pv, ag0 + dug * pg2, ag1 + dug * pg1, ag2 + dug * pg,
                        av0 + duv * pv2, av1 + duv * pv1, av2 + duv * pv, sg + dug, sv + duv)

            def step(r, carry, chunk=chunk):
                for u in range(FFN_UNROLL):
                    carry = chunk(r * FFN_UNROLL + u, carry)
                return carry

            out = lax.fori_loop(0, ts // (FFN_ROWS * FFN_UNROLL), step,
                                (_halo_chunk(pgh_ref, cols, first), _halo_chunk(pvh_ref, cols, first)) + (zero,) * 8)
            for k in range(FFN_CONV_K):
                dwg_ref[k:k + 1, cols] += _rsum(out[2 + k])
                dwv_ref[k:k + 1, cols] += _rsum(out[5 + k])
            dbg_ref[:, cols] += _rsum(out[8])
            dbv_ref[:, cols] += _rsum(out[9])

    tile, prev, vec = _ffn_specs(s, ts, tc, hb)
    half = pl.BlockSpec((ts, tc), lambda j, i: (i, j))
    wacc = pl.BlockSpec((FFN_CONV_K, tc), lambda j, i: (0, j))
    bacc = pl.BlockSpec((1, tc), lambda j, i: (0, j))
    return _hbm_call(
        body, name=name,
        grid=(hb, s // ts),
        in_specs=[tile(0), prev(0), tile(hb), prev(hb), half, vec(FFN_CONV_K, 0), vec(FFN_CONV_K, hb), vec(1, 0), vec(1, hb)],
        out_specs=[half, half, wacc, wacc, bacc, bacc],
        out_shape=[jax.ShapeDtypeStruct((s, f), F32)] * 2 + [jax.ShapeDtypeStruct((FFN_CONV_K, f), F32)] * 2
        + [jax.ShapeDtypeStruct((1, f), F32)] * 2,
        compiler_params=pltpu.CompilerParams(dimension_semantics=("arbitrary", "arbitrary"),
                                             vmem_limit_bytes=_vmem_limit(12 * ts * tc * 4)),
    )(p, p, p, p, dact, cw, cw, cb, cb)


def _ffn_in_bwd(dug, duv, cw, w, x, dres, gain, s1p, name):
    s, f = dug.shape
    d = x.shape[1]
    ts = _row_tile(s, 256)
    tc = w.shape[2]
    assert f % tc == 0 and w.shape[0] * tc == 2 * f
    nbh = ts // FFN_HALO

    def body(dug_ref, dugn_ref, duv_ref, duvn_ref, cw_ref, w_ref, x_ref, dres_ref, g_ref, s_ref,
             dx_ref, dp_ref, dsh_ref, dsc_ref, dg_ref, ext):
        i = pl.program_id(0)
        last = i == pl.num_programs(0) - 1

        @pl.when(i == 0)
        def _():
            for ref in (dsh_ref, dsc_ref, dg_ref):
                ref[...] = jnp.zeros_like(ref)

        dh = jnp.zeros((ts, d), F32)
        for half, (t_ref, n_ref) in enumerate(((dug_ref, dugn_ref), (duv_ref, duvn_ref))):
            for cb in range(f // tc):
                cols = slice(cb * tc, (cb + 1) * tc)
                wcols = slice(half * f + cb * tc, half * f + (cb + 1) * tc)
                ext[pl.ds(0, ts), :] = t_ref[:, cols]
                ext[pl.ds(ts, FFN_HALO), :] = jnp.where(last, 0.0, n_ref[:, cols])
                acc = cw_ref[FFN_CONV_K - 1:FFN_CONV_K, wcols] * t_ref[:, cols]
                for k in range(FFN_CONV_K - 1):
                    acc = acc + cw_ref[k:k + 1, wcols] * ext[pl.ds(FFN_CONV_K - 1 - k, ts), :]
                dpb = acc.astype(BF16)
                dp_ref[:, wcols] = dpb
                dh = dh + _dot_nt(dpb, w_ref[half * (f // tc) + cb])
        dx, dsh, dsc, dg = _norm_mod_bwd(dh, x_ref[...], g_ref[...], s_ref[...], dres_ref[...])
        dx_ref[...] = dx
        dsh_ref[...] += dsh
        dsc_ref[...] += dsc
        dg_ref[...] += dg

    tile = pl.BlockSpec((ts, f), lambda i: (i, 0))
    nxt = pl.BlockSpec((FFN_HALO, f), lambda i: (jnp.minimum((i + 1) * nbh, s // FFN_HALO - 1), 0))
    row = pl.BlockSpec((ts, d), lambda i: (i, 0))
    vec = pl.BlockSpec((1, d), lambda i: (0, 0))
    return _hbm_call(
        body, name=name,
        grid=(s // ts,),
        in_specs=[tile, nxt, tile, nxt, _const_spec(cw.shape),
                  pl.BlockSpec(w.shape, lambda i: (0, 0, 0), pipeline_mode=pl.Buffered(1)), row, row, vec, vec],
        out_specs=[row, pl.BlockSpec((ts, 2 * f), lambda i: (i, 0)), vec, vec, vec],
        out_shape=[jax.ShapeDtypeStruct((s, d), F32), jax.ShapeDtypeStruct((s, 2 * f), BF16)] + [jax.ShapeDtypeStruct((1, d), F32)] * 3,
        scratch_shapes=[pltpu.VMEM((ts + FFN_HALO, tc), F32)],
        compiler_params=pltpu.CompilerParams(
            dimension_semantics=("arbitrary",),
            vmem_limit_bytes=_vmem_limit(4 * ts * f * 4 + 2 * f * d * 2 + 2 * ts * 2 * f * 2 + 12 * ts * d * 4 + 6 * ts * tc * 4)),
    )(dug, dug, duv, duv, cw, w, x, dres, gain, s1p)


def _adamw_math(w, g, m, v):
    m = ADAM_B1 * m + (1.0 - ADAM_B1) * g
    v = ADAM_B2 * v + (1.0 - ADAM_B2) * (g * g)
    m_hat = m / (1.0 - ADAM_B1 ** ADAM_STEP)
    v_hat = v / (1.0 - ADAM_B2 ** ADAM_STEP)
    delta = -ADAM_LR * (m_hat / (jnp.sqrt(v_hat) + ADAM_EPS) + ADAM_WD * w)
    return delta, m, v


def _adam_rows(rows, cols):
    want = max(8, (2 * 1024 * 1024 // (cols * 4)) // 8 * 8)
    tr = min(rows, want)
    while rows % tr:
        tr -= 8
    return tr


def _adamw(w, m, v, g_parts, name):
    shape = w.shape
    nl = shape[0] if w.ndim == 3 else 1
    r, c = shape[-2], shape[-1]
    tr = _adam_rows(r, c)
    ng = len(g_parts)

    def body(*refs):
        w_ref, m_ref, v_ref = refs[0:3]
        g_refs = refs[3:3 + ng]
        g_out, d_out, m_out, v_out = refs[3 + ng:]
        g = g_refs[0][...]
        for gr in g_refs[1:]:
            g = g + gr[...]
        delta, mn, vn = _adamw_math(w_ref[...], g, m_ref[...], v_ref[...])
        g_out[...] = g
        d_out[...] = delta
        m_out[...] = mn
        v_out[...] = vn

    blk = pl.BlockSpec((None, tr, c), lambda l, i: (l, i, 0))
    outs = _hbm_call(
        body, name=name,
        grid=(nl, r // tr),
        in_specs=[blk] * (3 + ng),
        out_specs=[blk] * 4,
        out_shape=[jax.ShapeDtypeStruct((nl, r, c), F32)] * 4,
        compiler_params=pltpu.CompilerParams(dimension_semantics=("arbitrary", "arbitrary"),
                                             vmem_limit_bytes=_vmem_limit(2 * (7 + ng) * tr * max(c, 128) * 4 + (8 << 20))),
    )(*[a.reshape(nl, r, c) for a in (w, m, v, *g_parts)])
    return [o.reshape(shape) for o in outs]


def _adamw_many(ws, ms, vs, gs, name):
    n = len(ws)

    def body(*refs):
        for k in range(n):
            w_ref, m_ref, v_ref, g_ref = refs[k], refs[n + k], refs[2 * n + k], refs[3 * n + k]
            delta, mn, vn = _adamw_math(w_ref[...], g_ref[...], m_ref[...], v_ref[...])
            refs[4 * n + 3 * k][...] = delta
            refs[4 * n + 3 * k + 1][...] = mn
            refs[4 * n + 3 * k + 2][...] = vn

    specs = [_const_spec(a.shape) for a in ws]
    outs = _hbm_call(
        body, name=name,
        grid=(1,),
        in_specs=specs * 4,
        out_specs=[sp for sp in specs for _ in range(3)],
        out_shape=[jax.ShapeDtypeStruct(a.shape, F32) for a in ws for _ in range(3)],
        compiler_params=pltpu.CompilerParams(dimension_semantics=("arbitrary",),
                                             vmem_limit_bytes=_vmem_limit(20 * sum(a.size for a in ws) * 4 + (8 << 20))),
    )(*ws, *ms, *vs, *gs)
    return [tuple(outs[3 * k:3 * k + 3]) for k in range(n)]


def _modw_adamw(sct, dmod, w, m, v, name):
    nl, d, n = w.shape
    tr = _row_tile(d, 128)

    def body(sct_ref, dm_ref, w_ref, m_ref, v_ref, g_out, d_out, m_out, v_out):
        sc = sct_ref[...].astype(BF16).astype(F32)
        dm = dm_ref[...].astype(BF16).astype(F32)
        g = sc[:, 0:1] * dm[0:1, :]
        for b in range(1, N_DEV):
            g = g + sc[:, b:b + 1] * dm[b:b + 1, :]
        delta, mn, vn = _adamw_math(w_ref[...], g, m_ref[...], v_ref[...])
        g_out[...] = g
        d_out[...] = delta
        m_out[...] = mn
        v_out[...] = vn

    blk = pl.BlockSpec((None, tr, n), lambda l, i: (l, i, 0))
    return _hbm_call(
        body, name=name,
        grid=(nl, d // tr),
        in_specs=[pl.BlockSpec((tr, N_DEV), lambda l, i: (i, 0)), pl.BlockSpec((None, N_DEV, n), lambda l, i: (l, 0, 0)),
                  blk, blk, blk],
        out_specs=[blk] * 4,
        out_shape=[jax.ShapeDtypeStruct((nl, d, n), F32)] * 4,
        compiler_params=pltpu.CompilerParams(dimension_semantics=("arbitrary", "arbitrary"),
                                             vmem_limit_bytes=_vmem_limit(2 * 8 * tr * n * 4 + (8 << 20))),
    )(sct, dmod, w, m, v)


def _reduce4(recvs, name):
    nl = len(recvs)
    shape = recvs[0].shape[1:]
    c = shape[-1]
    r = math.prod(shape[:-1])
    tr = _adam_rows(r, c)
    nt = r // tr

    def body(*refs):
        o_ref = refs[nl]
        for l in range(nl):
            @pl.when(pl.program_id(0) == l)
            def _():
                acc = refs[l][0].astype(F32)
                for k in range(1, N_CHIPS):
                    acc = acc + refs[l][k].astype(F32)
                o_ref[...] = acc

    def in_map(l):
        return lambda ll, i: (0, jnp.where(ll < l, 0, jnp.where(ll > l, nt - 1, i)), 0)

    return _hbm_call(
        body, name=name,
        grid=(nl, nt),
        in_specs=[pl.BlockSpec((N_CHIPS, tr, c), in_map(l)) for l in range(nl)],
        out_specs=pl.BlockSpec((None, tr, c), lambda ll, i: (ll, i, 0)),
        out_shape=jax.ShapeDtypeStruct((nl, r, c), F32),
        compiler_params=pltpu.CompilerParams(dimension_semantics=("arbitrary", "arbitrary"),
                                             vmem_limit_bytes=_vmem_limit(2 * 8 * nl * tr * max(c, 128) * 4 + (8 << 20))),
    )(*[rv.reshape(N_CHIPS, r, c) for rv in recvs]).reshape((nl,) + shape)


def _my_place():
    return lax.axis_index("x"), lax.axis_index("y"), lax.axis_index("c")


def _chip_coords(j):
    return j // 2, j % 2


def _mod_forward(c, mod_w, mod_b4):
    nl, d, n = mod_w.shape
    kc = 256

    def body(c_ref, w_ref, b_ref, mod_ref, sc_ref, cbuf, stage, s1, r1, s2, r2):
        mx, my, mc = _my_place()
        me = 4 * mx + 2 * my + mc
        q = 2 * mx + my
        cv = c_ref[...]
        cbuf[me] = jnp.broadcast_to(cv * jax.nn.sigmoid(cv), (8, d))
        sends = []
        for t in range(N_DEV):
            tx, ty = _chip_coords(t // 2)
            cp = pltpu.make_async_remote_copy(src_ref=cbuf.at[me], dst_ref=cbuf.at[me], send_sem=s1.at[t], recv_sem=r1.at[me],
                                              device_id=(tx, ty, t % 2), device_id_type=MESH)

            @pl.when(t != me)
            def _():
                cp.start()

            sends.append((t, cp))
        for t in range(N_DEV):
            @pl.when(t != me)
            def _():
                pltpu.make_async_remote_copy(src_ref=cbuf.at[t], dst_ref=cbuf.at[t], send_sem=s1.at[t], recv_sem=r1.at[t],
                                             device_id=(mx, my, mc), device_id_type=MESH).wait_recv()
        for t, cp in sends:
            @pl.when(t != me)
            def _():
                cp.wait_send()

        row = lax.broadcasted_iota(jnp.int32, (8, d), 0)
        sc_all = jnp.zeros((8, d), F32)
        for t in range(N_DEV):
            sc_all = sc_all + jnp.where(row == t, cbuf[t], 0.0)
        sc_ref[...] = sc_all
        rown = lax.broadcasted_iota(jnp.int32, (8, n), 0)
        for l in range(nl):
            acc = jnp.zeros((8, n), F32)
            for k0 in range(0, d, kc):
                acc = acc + _dot(sc_all[:, k0:k0 + kc].astype(BF16), w_ref[l, k0:k0 + kc, :].astype(BF16))
            acc = acc + b_ref[l, q]
            for j in range(N_CHIPS):
                jx, jy = _chip_coords(j)
                bdest = 4 * jx + 2 * jy + mc
                rowv = jnp.sum(jnp.where(rown == bdest, acc, 0.0), axis=0, keepdims=True)
                stage[j, l] = jnp.broadcast_to(rowv, (8, n))
        sends2 = []
        for j in range(N_CHIPS):
            jx, jy = _chip_coords(j)
            cp = pltpu.make_async_remote_copy(src_ref=stage.at[j], dst_ref=mod_ref.at[:, q], send_sem=s2.at[j], recv_sem=r2.at[q],
                                              device_id=(jx, jy, mc), device_id_type=MESH)

            @pl.when(j != q)
            def _():
                cp.start()

            @pl.when(j == q)
            def _():
                for l in range(nl):
                    mod_ref[l, j] = stage[j, l]

            sends2.append((j, cp))
        for j in range(N_CHIPS):
            @pl.when(j != q)
            def _():
                pltpu.make_async_remote_copy(src_ref=stage.at[j], dst_ref=mod_ref.at[:, j], send_sem=s2.at[j], recv_sem=r2.at[j],
                                             device_id=(mx, my, mc), device_id_type=MESH).wait_recv()
        for j, cp in sends2:
            @pl.when(j != q)
            def _():
                cp.wait_send()

    vm = pl.BlockSpec(memory_space=pltpu.VMEM)
    return pl.pallas_call(
        body, name="mod_forward",
        in_specs=[vm, vm, vm],
        out_specs=[vm, vm],
        out_shape=[jax.ShapeDtypeStruct((nl, N_CHIPS, 8, n), F32), jax.ShapeDtypeStruct((8, d), F32)],
        scratch_shapes=[pltpu.VMEM((N_DEV, 8, d), F32), pltpu.VMEM((N_CHIPS, nl, 8, n), F32),
                        pltpu.SemaphoreType.DMA((N_DEV,)), pltpu.SemaphoreType.DMA((N_DEV,)),
                        pltpu.SemaphoreType.DMA((N_CHIPS,)), pltpu.SemaphoreType.DMA((N_CHIPS,))],
        compiler_params=pltpu.CompilerParams(vmem_limit_bytes=_vmem_limit(2 * nl * d * n * 4 + (8 << 20))),
    )(c, mod_w, mod_b4)


_HBM_SPEC = pl.BlockSpec(memory_space=pltpu.HBM)
_SEM_SPEC = pl.BlockSpec(memory_space=pltpu.SEMAPHORE)
_DATAFLOW = pltpu.SideEffectType.DATAFLOW_SIDE_EFFECTING


def _slot(ref, scatter, j):
    return ref.at[j] if scatter else ref


def _exchange_start(groups, scatter, after, name):
    flat = [a for g in groups for a in g]
    na = len(flat)
    ng = len(groups)
    sizes = [len(g) for g in groups]
    first = [sum(sizes[:g]) for g in range(ng)]
    where = [(g, k) for g in range(ng) for k in range(sizes[g])]
    mx, my, _ = _my_place()
    qo = 2 * mx + my
    lands = []
    for a in flat:
        own = lax.dynamic_index_in_dim(a, qo, 0, keepdims=False) if scatter else a
        lands.append(lax.dynamic_update_index_in_dim(lax.empty((N_CHIPS,) + own.shape, a.dtype), own, qo, 0))

    def body(*refs):
        ins, lnd = refs[:na], refs[na:2 * na]
        ssems, rsems = refs[2 * na + 1:2 * na + 1 + ng], refs[2 * na + 1 + ng:2 * na + 1 + 2 * ng]
        token = refs[-1]
        mx, my, mc = _my_place()
        q = 2 * mx + my
        for j in range(N_CHIPS):
            jx, jy = _chip_coords(j)
            for a in range(na):
                g, k = where[a]

                @pl.when(j != q)
                def _():
                    pltpu.make_async_remote_copy(src_ref=_slot(ins[a], scatter, j), dst_ref=lnd[a].at[q],
                                                 send_sem=ssems[g].at[k * N_CHIPS + j], recv_sem=rsems[g].at[k * N_CHIPS + q],
                                                 device_id=(jx, jy, mc), device_id_type=MESH).start()
        token[...] = jnp.zeros_like(token)

    sem_shapes = [pltpu.SemaphoreType.DMA((n * N_CHIPS,)) for n in sizes]
    outs = pl.pallas_call(
        body, name=name,
        in_specs=[_HBM_SPEC] * (2 * na) + [pl.BlockSpec(memory_space=pl.ANY)],
        out_specs=[_SEM_SPEC] * (2 * ng) + [_HBM_SPEC] * (2 * na) + [pl.BlockSpec(memory_space=pltpu.VMEM)],
        out_shape=sem_shapes + sem_shapes + [pltpu.HBM(a.shape, a.dtype) for a in flat + lands]
        + [jax.ShapeDtypeStruct((8, 128), F32)],
        input_output_aliases={i: 2 * ng + i for i in range(2 * na)},
        compiler_params=pltpu.CompilerParams(has_side_effects=_DATAFLOW),
    )(*[pltpu.with_memory_space_constraint(a, pltpu.HBM) for a in flat + lands], after)
    ssems, rsems = outs[:ng], outs[ng:2 * ng]
    src_thru, land_thru = outs[2 * ng:2 * ng + na], outs[2 * ng + na:2 * ng + 2 * na]
    states = [(src_thru[first[g]:first[g] + sizes[g]], land_thru[first[g]:first[g] + sizes[g]], ssems[g], rsems[g])
              for g in range(ng)]
    return states, outs[-1]


def _exchange_wait(state, scatter, after, name):
    srcs, lands, ssem, rsem = state
    na = len(srcs)

    def body(*refs):
        ins, lnd = refs[:na], refs[na:2 * na]
        ssem_ref, rsem_ref = refs[2 * na], refs[2 * na + 1]
        mx, my, mc = _my_place()
        q = 2 * mx + my
        for j in range(N_CHIPS):
            for a in range(na):
                @pl.when(j != q)
                def _():
                    cp = pltpu.make_async_remote_copy(src_ref=_slot(ins[a], scatter, j), dst_ref=lnd[a].at[j],
                                                      send_sem=ssem_ref.at[a * N_CHIPS + j], recv_sem=rsem_ref.at[a * N_CHIPS + j],
                                                      device_id=(mx, my, mc), device_id_type=MESH)
                    cp.wait_send()
                    cp.wait_recv()

    outs = pl.pallas_call(
        body, name=name,
        in_specs=[_HBM_SPEC] * (2 * na) + [_SEM_SPEC, _SEM_SPEC, pl.BlockSpec(memory_space=pl.ANY)],
        out_specs=[_HBM_SPEC] * (2 * na),
        out_shape=[pltpu.HBM(a.shape, a.dtype) for a in list(srcs) + list(lands)],
        input_output_aliases={i: i for i in range(2 * na)},
        compiler_params=pltpu.CompilerParams(has_side_effects=_DATAFLOW),
    )(*srcs, *lands, ssem, rsem, after)
    return outs[na:]


def _sibling_copy(src, dst, ssem, rsem, a):
    mx, my, mc = _my_place()
    return pltpu.make_async_remote_copy(src_ref=src, dst_ref=dst, send_sem=ssem.at[a], recv_sem=rsem.at[a],
                                        device_id=(mx, my, 1 - mc), device_id_type=MESH)


def _swap_start(arrs, after, name):
    na = len(arrs)
    lands = [lax.empty(a.shape, a.dtype) for a in arrs]

    def body(*refs):
        ins, lnd = refs[:na], refs[na:2 * na]
        ssem, rsem, token = refs[2 * na + 1], refs[2 * na + 2], refs[-1]
        for a in range(na):
            _sibling_copy(ins[a], lnd[a], ssem, rsem, a).start()
        token[...] = jnp.zeros_like(token)

    outs = pl.pallas_call(
        body, name=name,
        in_specs=[_HBM_SPEC] * (2 * na) + [pl.BlockSpec(memory_space=pl.ANY)],
        out_specs=[_SEM_SPEC] * 2 + [_HBM_SPEC] * (2 * na) + [pl.BlockSpec(memory_space=pltpu.VMEM)],
        out_shape=[pltpu.SemaphoreType.DMA((na,))] * 2 + [pltpu.HBM(a.shape, a.dtype) for a in list(arrs) + lands]
        + [jax.ShapeDtypeStruct((8, 128), F32)],
        input_output_aliases={i: 2 + i for i in range(2 * na)},
        compiler_params=pltpu.CompilerParams(has_side_effects=_DATAFLOW),
    )(*[pltpu.with_memory_space_constraint(a, pltpu.HBM) for a in list(arrs) + lands], after)
    return (outs[2:2 + na], outs[2 + na:2 + 2 * na], outs[0], outs[1]), outs[-1]


def _swap_wait(state, after, name):
    srcs, lands, ssem, rsem = state
    na = len(srcs)

    def body(*refs):
        ins, lnd = refs[:na], refs[na:2 * na]
        ssem_ref, rsem_ref = refs[2 * na], refs[2 * na + 1]
        for a in range(na):
            cp = _sibling_copy(ins[a], lnd[a], ssem_ref, rsem_ref, a)
            cp.wait_send()
            cp.wait_recv()

    outs = pl.pallas_call(
        body, name=name,
        in_specs=[_HBM_SPEC] * (2 * na) + [_SEM_SPEC, _SEM_SPEC, pl.BlockSpec(memory_space=pl.ANY)],
        out_specs=[_HBM_SPEC] * (2 * na),
        out_shape=[pltpu.HBM(a.shape, a.dtype) for a in list(srcs) + list(lands)],
        input_output_aliases={i: i for i in range(2 * na)},
        compiler_params=pltpu.CompilerParams(has_side_effects=_DATAFLOW),
    )(*srcs, *lands, ssem, rsem, after)
    return outs[:na], outs[na:]


def _allreduce_small(rows_all, rows_sum):
    ra, c = rows_all.shape
    r = rows_sum.shape[0]
    ch = r // N_DEV
    assert ch % 8 == 0 and ch * N_DEV == r

    def body(a_ref, s_ref, all_ref, sum_ref, rbuf, red, sa, rva, sb, rvb, sc, rvc):
        mx, my, mc = _my_place()
        me = 4 * mx + 2 * my + mc
        mine = pl.ds(pl.multiple_of(me * ch, 8), ch)
        all_ref[me] = a_ref[...]
        rbuf[me] = s_ref[mine, :]

        def dev(t):
            tx, ty = _chip_coords(t // 2)
            return (tx, ty, t % 2)

        def everyone_else(fn):
            for t in range(N_DEV):
                @pl.when(t != me)
                def _():
                    fn(t)

        def copy_a(t, slot):
            return pltpu.make_async_remote_copy(src_ref=a_ref, dst_ref=all_ref.at[slot], send_sem=sa.at[t], recv_sem=rva.at[slot],
                                                device_id=dev(t), device_id_type=MESH)

        def copy_b(t, slot):
            return pltpu.make_async_remote_copy(src_ref=s_ref.at[pl.ds(t * ch, ch), :], dst_ref=rbuf.at[slot], send_sem=sb.at[t],
                                                recv_sem=rvb.at[slot], device_id=dev(t), device_id_type=MESH)

        def copy_c(t, chunk_start, slot):
            return pltpu.make_async_remote_copy(src_ref=red, dst_ref=sum_ref.at[pl.ds(chunk_start, ch), :], send_sem=sc.at[t],
                                                recv_sem=rvc.at[slot], device_id=dev(t), device_id_type=MESH)

        everyone_else(lambda t: (copy_a(t, me).start(), copy_b(t, me).start()))
        everyone_else(lambda t: (copy_a(t, t).wait_recv(), copy_b(t, t).wait_recv()))
        everyone_else(lambda t: (copy_a(t, me).wait_send(), copy_b(t, me).wait_send()))
        acc = rbuf[0]
        for t in range(1, N_DEV):
            acc = acc + rbuf[t]
        red[...] = acc
        sum_ref[mine, :] = acc
        everyone_else(lambda t: copy_c(t, pl.multiple_of(me * ch, 8), me).start())
        everyone_else(lambda t: copy_c(t, t * ch, t).wait_recv())
        everyone_else(lambda t: copy_c(t, pl.multiple_of(me * ch, 8), me).wait_send())

    vm = pl.BlockSpec(memory_space=pltpu.VMEM)
    return pl.pallas_call(
        body, name="allreduce_small",
        in_specs=[vm, vm],
        out_specs=[vm, vm],
        out_shape=[jax.ShapeDtypeStruct((N_DEV, ra, c), F32), jax.ShapeDtypeStruct((r, c), F32)],
        scratch_shapes=[pltpu.VMEM((N_DEV, ch, c), F32), pltpu.VMEM((ch, c), F32)] + [pltpu.SemaphoreType.DMA((N_DEV,))] * 6,
        compiler_params=pltpu.CompilerParams(vmem_limit_bytes=_vmem_limit((3 * r + 2 * N_DEV * ra) * c * 4 + (4 << 20))),
    )(rows_all, rows_sum)


def _pack(arrs, row_multiple=8):
    rows, layout, at = [], [], 0
    for a in arrs:
        n = a.size
        nr = -(-n // (8 * SMALL_COLS)) * 8
        flat = a.reshape(-1)
        if nr * SMALL_COLS != n:
            flat = jnp.pad(flat, (0, nr * SMALL_COLS - n))
        rows.append(flat.reshape(nr, SMALL_COLS))
        layout.append((at, nr, a.shape))
        at += nr
    pad = -at % row_multiple
    if pad:
        rows.append(jnp.zeros((pad, SMALL_COLS), F32))
    return jnp.concatenate(rows, axis=0), layout


def _unpack(buf, layout):
    out = []
    for at, nr, shape in layout:
        n = math.prod(shape)
        out.append(buf[at:at + nr].reshape(-1)[:n].reshape(shape))
    return out


SMALL_NAMES = ("mod_b", "mix_pre_g", "mix_post_g", "sgu_norm_g", "sgu_norm_b", "sgu_w", "sgu_b", "conv_b", "conv_norm_g",
               "conv_norm_b", "pool_w", "pool_scale", "branch_g", "ffn_pre_g", "ffn_post_g", "ffn_conv_b")
SHARDED_SMALL = ("conv_w", "ffn_conv_w")
WEIGHT_ORDER = ("mod_w", "mod_b", "mix_pre_g", "mix_post_g", "w_in", "sgu_norm_g", "sgu_norm_b", "sgu_w", "sgu_b", "conv_w",
                "conv_b", "conv_norm_g", "conv_norm_b", "pool_w", "pool_scale", "branch_g", "w_out", "ffn_pre_g", "ffn_post_g",
                "ffn_up", "ffn_conv_w", "ffn_conv_b", "ffn_down")


def _block_diag(blocks):
    n, a, b = blocks.shape
    eye = jnp.eye(n, dtype=blocks.dtype)
    return (eye[:, None, :, None] * blocks[:, :, None, :]).reshape(n * a, n * b)


def _diag_blocks(mat, n):
    a = mat.shape[0] // n
    return jnp.stack([mat[g * a:(g + 1) * a, g * a:(g + 1) * a] for g in range(n)])


def _step(x, c, loss_target, w, m, v):
    nl = w["mod_w"].shape[0]
    s, d = x.shape[1], x.shape[2]
    heads = SGU_WIDTH // HEAD_DIM
    groups = len(POOL_WINDOWS)
    mx, my, _ = _my_place()
    q = 2 * mx + my
    x0 = x.reshape(s, d)
    tgt = loss_target.reshape(s, d)

    nmod = w["mod_w"].shape[2]
    kin = w["w_in"].shape[2]
    inw = kin * N_CHIPS
    f2 = w["ffn_up"].shape[2] * N_CHIPS
    f = f2 // 2

    def wgroups(l):
        return [[jnp.swapaxes(w["w_in"][l], 0, 1).astype(BF16), w["conv_w"][l], w["ffn_conv_w"][l]], [w["w_out"][l].astype(BF16)],
                [w["ffn_up"][l].astype(BF16)], [w["ffn_down"][l].astype(BF16)]]

    gstates = {}
    (gstates[0, 0], gstates[0, 1]), gtoken = _exchange_start(wgroups(0)[:2], False, c, "gather_start_in_0")
    mod4, sc_all = _mod_forward(c + gtoken[0:1, 0:1], w["mod_w"], w["mod_b"].reshape(nl, N_CHIPS, 1, nmod))
    mod = mod4[:, :, 0, :].reshape(nl, N_MOD, 1, d)

    tril = jnp.tril(jnp.ones((CHUNK, CHUNK), bool))
    bd = _block_diag(jnp.ones((heads, HEAD_DIM, HEAD_DIM), BF16))

    def mixer_params(l, conv_w):
        wm = jnp.where(tril[None], w["sgu_w"][l], 0.0)
        pw = _block_diag(w["pool_w"][l])
        return dict(
            bd=bd, ng=w["sgu_norm_g"][l][None], nb=w["sgu_norm_b"][l][None],
            wm=wm.astype(BF16), wmt=jnp.swapaxes(wm, 1, 2).astype(BF16),
            bias=jnp.repeat(w["sgu_b"][l].T, HEAD_DIM, axis=1),
            cw=conv_w, cb=w["conv_b"][l][None], cng=w["conv_norm_g"][l][None], cnb=w["conv_norm_b"][l][None],
            pw=pw.astype(BF16), pwt=pw.T.astype(BF16), ps=w["pool_scale"][l][None], bg=w["branch_g"][l][None])

    saved = []
    xl = x0
    arrived = {0: list(_exchange_wait(gstates[0, 0], False, mod4, "gather_wait_in_0"))}
    arrived[0] += list(_exchange_wait(gstates[0, 1], False, arrived[0][0], "gather_wait_out_0"))
    for l in range(nl):
        sh1, sc1, g1, sh2, sc2, g2 = [mod[l, k] for k in range(N_MOD)]
        gpre1, gpost1 = w["mix_pre_g"][l][None], w["mix_post_g"][l][None]
        gpre2, gpost2 = w["ffn_pre_g"][l][None], w["ffn_post_g"][l][None]
        fcb = w["ffn_conv_b"][l][None]
        sh1_after, bg_after, g1_after, sh2_after, fcb_after = sh1, w["branch_g"][l][None], g1, sh2, fcb
        g_win, g_cw, g_fcw = arrived[l][:3]
        if l == 0:
            (gstates[0, 2],), tok = _exchange_start(wgroups(0)[2:3], False, arrived[0][3], "gather_start_up_0")
            sh1_after = sh1 + tok[0:1, 0:1]
        w_in = g_win.reshape(inw, d)
        conv_w = jnp.transpose(g_cw, (1, 0, 2)).reshape(CONV_K, CONV_WIDTH)
        ffn_cw = jnp.transpose(g_fcw, (1, 0, 2)).reshape(FFN_CONV_K, f2)
        mp = mixer_params(l, conv_w)
        z, h1 = _norm_mod_matmul(xl, gpre1, 1.0 + sc1, sh1_after, w_in[None], f"mix_in_{l}", transposed=True)
        w_out = arrived[l][3].reshape(d, d)
        ycat, cbo, dpool = _mixer_fwd(z, dict(mp, bg=bg_after), f"mixer_fwd_{l}")
        (up,) = _exchange_wait(gstates[l, 2], False, ycat, f"gather_wait_up_{l}")
        if l == 0:
            (gstates[0, 3],), tok = _exchange_start(wgroups(0)[3:4], False, up, "gather_start_down_0")
            g1_after = g1 + tok[0:1, 0:1]
        o, x1 = _matmul_norm_resid(ycat, w_out, xl, g1_after, gpost1, f"mix_out_{l}")
        if l + 1 < nl:
            (gstates[l + 1, 0], gstates[l + 1, 1]), tok = _exchange_start(wgroups(l + 1)[0:2], False, x1,
                                                                        f"gather_start_in_{l + 1}")
            sh2_after = sh2 + tok[0:1, 0:1]
        p, h2 = _norm_mod_matmul(x1, gpre2, 1.0 + sc2, sh2_after, up, f"ffn_in_{l}")
        if l + 1 < nl:
            nxt = _exchange_wait(gstates[l + 1, 0], False, p, f"gather_wait_in_{l + 1}")
            nxt_out = _exchange_wait(gstates[l + 1, 1], False, nxt[0], f"gather_wait_out_{l + 1}")
            arrived[l + 1] = list(nxt) + list(nxt_out)
            (gstates[l + 1, 2], gstates[l + 1, 3]), tok = _exchange_start(wgroups(l + 1)[2:4], False, nxt_out[0],
                                                                        f"gather_start_up_{l + 1}")
            fcb_after = fcb + tok[0:1, 0:1]
        act = _ffn_act_fwd(p, ffn_cw, fcb_after, f"ffn_act_{l}")
        (g_down,) = _exchange_wait(gstates[l, 3], False, act, f"gather_wait_down_{l}")
        down = g_down.reshape(f, d)
        qo, x2 = _matmul_norm_resid(act, down, x1, g2, gpost2, f"ffn_out_{l}")
        saved.append(dict(x=xl, z=z, h1=h1, ycat=ycat, cbo=cbo, dpool=dpool, o=o, x1=x1, p=p, h2=h2, act=act, qo=qo, mp=mp, fcb=fcb,
                          w_in=w_in, w_out=w_out, up=up, down=down, ffn_cw=ffn_cw,
                          mods=(sh1, sc1, g1, sh2, sc2, g2), gains=(gpre1, gpost1, gpre2, gpost2)))
        xl = x2

    dx, loss_row = _loss_head(xl, tgt)

    small = {n: [None] * nl for n in SMALL_NAMES + SHARDED_SMALL}
    dmods = [None] * nl
    tn = f2 // N_CHIPS
    sstates = {}
    token = None
    for l in reversed(range(nl)):
        sv = saved[l]
        sh1, sc1, g1, sh2, sc2, g2 = sv["mods"]
        gpre1, gpost1, gpre2, gpost2 = sv["gains"]
        if token is not None:
            g2 = g2 + token[0:1, 0:1]
        dq, dact, dg2, dgpost2 = _resid_bwd_matmul(dx, sv["qo"], g2, gpost2, sv["down"], f"ffn_out_bwd_{l}")
        g_down = _wgrad(sv["act"], dq, (f, lambda j: 0), (d, lambda j: 0), jax.ShapeDtypeStruct((f, d), BF16),
                        (1, lambda j: (0, 0)), (f, d), f"wgrad_ffn_down_{l}")
        dug, duv, dfwg, dfwv, dfbg, dfbv = _ffn_act_bwd(sv["p"], dact, sv["ffn_cw"], sv["fcb"], f"ffn_act_bwd_{l}")
        dx1, dp, dsh2, dsc2, dgpre2 = _ffn_in_bwd(dug, duv, sv["ffn_cw"], sv["up"], sv["x1"], dx, gpre2, 1.0 + sc2,
                                                  f"ffn_in_bwd_{l}")
        g_up = _wgrad(sv["h2"], dp, (d, lambda j: 0), (tn, lambda j: j), jax.ShapeDtypeStruct((N_CHIPS, d, tn), BF16),
                      (N_CHIPS, lambda j: (j, 0, 0)), (None, d, tn), f"wgrad_ffn_up_{l}")
        (sstates[l, 0],), token = _exchange_start([[g_down.reshape(N_CHIPS, f // N_CHIPS, d), g_up]], True, g_up,
                                                  f"scatter_start_ffn_{l}")
        do, dycat, dg1, dgpost1 = _resid_bwd_matmul(dx1, sv["o"], g1 + token[0:1, 0:1], gpost1, sv["w_out"],
                                                    f"mix_out_bwd_{l}")
        g_out = _wgrad(sv["ycat"], do, (d, lambda j: 0), (d, lambda j: 0), jax.ShapeDtypeStruct((d, d), BF16),
                       (1, lambda j: (0, 0)), (d, d), f"wgrad_w_out_{l}")
        (sstates[l, 1],), token = _exchange_start([[g_out.reshape(N_CHIPS, d // N_CHIPS, d)]], True, g_out,
                                                  f"scatter_start_out_{l}")
        mp_after = dict(sv["mp"], bg=sv["mp"]["bg"] + token[0:1, 0:1])
        (dza, dcb, dd, dbg, dwm, dbias, dng, dnb, dcng, dcnb, dps, dpw) = _mixer_bwd_a(sv["z"], sv["cbo"], sv["dpool"], dycat, mp_after, f"mixer_bwd_a_{l}")
        dx, dz, dsh1, dsc1, dgpre1, dcw, dcbias = _mixer_bwd_b(
            sv["z"], dza, dcb, dd, sv["x"], dx1, gpre1, 1.0 + sc1, sv["mp"]["cw"], sv["w_in"], f"mixer_bwd_b_{l}")
        g_in = _wgrad(dz, sv["h1"], (inw, lambda j: 0), (d, lambda j: 0), jax.ShapeDtypeStruct((inw, d), BF16),
                      (1, lambda j: (0, 0)), (inw, d), f"wgrad_w_in_{l}")
        g_in_parts = g_in.reshape(N_CHIPS, kin, d)
        if l > 0:
            (sstates[l, 2],), token = _exchange_start([[g_in_parts]], True, g_in_parts, f"scatter_start_in_{l}")

        dmods[l] = jnp.concatenate([dsh1, dsc1, dg1, dsh2, dsc2, dg2], axis=0)
        small["mix_pre_g"][l], small["mix_post_g"][l] = dgpre1[0], dgpost1[0]
        small["ffn_pre_g"][l], small["ffn_post_g"][l] = dgpre2[0], dgpost2[0]
        small["sgu_norm_g"][l], small["sgu_norm_b"][l] = dng[0], dnb[0]
        small["sgu_w"][l] = jnp.where(tril[None], dwm, 0.0)
        small["sgu_b"][l] = dbias.reshape(CHUNK, heads, HEAD_DIM).sum(-1).T
        small["conv_b"][l], small["conv_norm_g"][l], small["conv_norm_b"][l] = dcbias[0], dcng[0], dcnb[0]
        small["pool_w"][l], small["pool_scale"][l], small["branch_g"][l] = _diag_blocks(dpw, groups), dps[0], dbg[0]
        small["ffn_conv_b"][l] = jnp.concatenate([dfbg[0], dfbv[0]])
        small["conv_w"][l] = dcw
        small["ffn_conv_w"][l] = jnp.concatenate([dfwg, dfwv], axis=1)

    names = [n for n in SMALL_NAMES if n != "mod_b"] + list(SHARDED_SMALL)
    dmod_rows, _ = _pack([jnp.stack(dmods)])
    packed, layout = _pack([jnp.stack(dmods), loss_row] + [jnp.stack(small[n]) for n in names], 8 * N_DEV)
    gathered, summed = _allreduce_small(dmod_rows, packed)
    (sstates[0, 2],), token = _exchange_start([[g_in_parts]], True, summed, "scatter_start_in_0")
    parts = _unpack(summed, layout)
    loss = parts[1][0, 0]
    gsmall = dict(zip(names, parts[2:]))
    gsmall["mod_b"] = parts[0].reshape(nl, N_MOD * d)
    dmod_all = gathered[:, :nl * N_MOD].reshape(N_DEV, nl, N_MOD * d)
    dmod_mine = jnp.transpose(lax.dynamic_slice_in_dim(dmod_all, q * nmod, nmod, axis=2), (1, 0, 2))

    grads, deltas, new_m, new_v = {}, {}, {}, {}

    def put(name, res):
        grads[name], deltas[name], new_m[name], new_v[name] = res

    recv = dict(w_in=[None] * nl, w_out=[None] * nl, ffn_up=[None] * nl, ffn_down=[None] * nl)
    done = token
    for l in reversed(range(nl)):
        recv["ffn_down"][l], recv["ffn_up"][l] = _exchange_wait(sstates[l, 0], True, done, f"scatter_wait_ffn_{l}")
        (recv["w_out"][l],) = _exchange_wait(sstates[l, 1], True, recv["ffn_up"][l], f"scatter_wait_out_{l}")
        done = recv["w_out"][l]
    big = ("w_out", "ffn_up", "ffn_down", "w_in")
    mine = {n: _reduce4(recv[n], f"reduce4_{n}") for n in big[:3]}
    swap_a, token = _swap_start([mine[n] for n in big[:3]], mine["ffn_down"], "swap_start_a")

    put("mod_w", _modw_adamw(sc_all.T + token[0:1, 0:1], dmod_mine, w["mod_w"], m["mod_w"], v["mod_w"], "adamw_mod_w"))
    done = grads["mod_w"]
    for l in reversed(range(nl)):
        (recv["w_in"][l],) = _exchange_wait(sstates[l, 2], True, done, f"scatter_wait_in_{l}")
        done = recv["w_in"][l]
    mine["w_in"] = _reduce4(recv["w_in"], "reduce4_w_in")
    swap_b, token = _swap_start([mine["w_in"]], mine["w_in"], "swap_start_b")

    gsmall["conv_w"] = lax.dynamic_slice_in_dim(gsmall["conv_w"], q * (CONV_WIDTH // N_CHIPS), CONV_WIDTH // N_CHIPS, axis=2)
    gsmall["ffn_conv_w"] = lax.dynamic_slice_in_dim(gsmall["ffn_conv_w"], q * (f2 // N_CHIPS), f2 // N_CHIPS, axis=2)
    snames = SMALL_NAMES + SHARDED_SMALL
    res = _adamw_many([w[n] for n in snames], [m[n] for n in snames], [v[n] for n in snames], [gsmall[n] for n in snames],
                      "adamw_small")
    for n, (d_, m_, v_) in zip(snames, res):
        put(n, (gsmall[n], d_, m_, v_))

    sent, theirs = _swap_wait(swap_a, deltas["mod_b"], "swap_wait_a")
    sent_b, theirs_b = _swap_wait(swap_b, sent[0], "swap_wait_b")
    for n, a, b in zip(big, list(sent) + list(sent_b), list(theirs) + list(theirs_b)):
        if n == "w_in":
            res = _adamw(*[jnp.swapaxes(t[n], 1, 2) for t in (w, m, v)], [a, b], f"adamw_{n}")
            put(n, [jnp.swapaxes(r_, 1, 2) for r_ in res])
        else:
            put(n, _adamw(w[n], m[n], v[n], [a, b], f"adamw_{n}"))

    return (loss, dx.reshape(1, s, d), *[grads[n] for n in WEIGHT_ORDER], *[deltas[n] for n in WEIGHT_ORDER],
            *[new_m[n] for n in WEIGHT_ORDER], *[new_v[n] for n in WEIGHT_ORDER])


def kernel(x, c, mod_w, mod_b, mix_pre_g, mix_post_g, w_in, sgu_norm_g, sgu_norm_b, sgu_w, sgu_b, conv_w, conv_b, conv_norm_g, conv_norm_b, pool_w, pool_scale, branch_g, w_out, ffn_pre_g, ffn_post_g, ffn_up, ffn_conv_w, ffn_conv_b, ffn_down, loss_target, m_mod_w, m_mod_b, m_mix_pre_g, m_mix_post_g, m_w_in, m_sgu_norm_g, m_sgu_norm_b, m_sgu_w, m_sgu_b, m_conv_w, m_conv_b, m_conv_norm_g, m_conv_norm_b, m_pool_w, m_pool_scale, m_branch_g, m_w_out, m_ffn_pre_g, m_ffn_post_g, m_ffn_up, m_ffn_conv_w, m_ffn_conv_b, m_ffn_down, v_mod_w, v_mod_b, v_mix_pre_g, v_mix_post_g, v_w_in, v_sgu_norm_g, v_sgu_norm_b, v_sgu_w, v_sgu_b, v_conv_w, v_conv_b, v_conv_norm_g, v_conv_norm_b, v_pool_w, v_pool_scale, v_branch_g, v_w_out, v_ffn_pre_g, v_ffn_post_g, v_ffn_up, v_ffn_conv_w, v_ffn_conv_b, v_ffn_down):
    w = dict(mod_w=mod_w, mod_b=mod_b, mix_pre_g=mix_pre_g, mix_post_g=mix_post_g, w_in=w_in, sgu_norm_g=sgu_norm_g,
             sgu_norm_b=sgu_norm_b, sgu_w=sgu_w, sgu_b=sgu_b, conv_w=conv_w, conv_b=conv_b, conv_norm_g=conv_norm_g,
             conv_norm_b=conv_norm_b, pool_w=pool_w, pool_scale=pool_scale, branch_g=branch_g, w_out=w_out,
             ffn_pre_g=ffn_pre_g, ffn_post_g=ffn_post_g, ffn_up=ffn_up, ffn_conv_w=ffn_conv_w, ffn_conv_b=ffn_conv_b,
             ffn_down=ffn_down)
    m = dict(mod_w=m_mod_w, mod_b=m_mod_b, mix_pre_g=m_mix_pre_g, mix_post_g=m_mix_post_g, w_in=m_w_in,
             sgu_norm_g=m_sgu_norm_g, sgu_norm_b=m_sgu_norm_b, sgu_w=m_sgu_w, sgu_b=m_sgu_b, conv_w=m_conv_w,
             conv_b=m_conv_b, conv_norm_g=m_conv_norm_g, conv_norm_b=m_conv_norm_b, pool_w=m_pool_w,
             pool_scale=m_pool_scale, branch_g=m_branch_g, w_out=m_w_out, ffn_pre_g=m_ffn_pre_g, ffn_post_g=m_ffn_post_g,
             ffn_up=m_ffn_up, ffn_conv_w=m_ffn_conv_w, ffn_conv_b=m_ffn_conv_b, ffn_down=m_ffn_down)
    v = dict(mod_w=v_mod_w, mod_b=v_mod_b, mix_pre_g=v_mix_pre_g, mix_post_g=v_mix_post_g, w_in=v_w_in,
             sgu_norm_g=v_sgu_norm_g, sgu_norm_b=v_sgu_norm_b, sgu_w=v_sgu_w, sgu_b=v_sgu_b, conv_w=v_conv_w,
             conv_b=v_conv_b, conv_norm_g=v_conv_norm_g, conv_norm_b=v_conv_norm_b, pool_w=v_pool_w,
             pool_scale=v_pool_scale, branch_g=v_branch_g, w_out=v_w_out, ffn_pre_g=v_ffn_pre_g, ffn_post_g=v_ffn_post_g,
             ffn_up=v_ffn_up, ffn_conv_w=v_ffn_conv_w, ffn_conv_b=v_ffn_conv_b, ffn_down=v_ffn_down)
    return _step(x, c, loss_target, w, m, v)
```

```python
import functools
import math

import jax
import jax.numpy as jnp
from jax import lax
from jax.experimental import pallas as pl
from jax.experimental.pallas import tpu as pltpu

F32 = jnp.float32
BF16 = jnp.bfloat16
MESH = pl.DeviceIdType.MESH

EPS = 1e-6
HEAD_DIM = 64
CHUNK = 128
SGU_WIDTH = 384
CONV_WIDTH = 384
POOL_WIDTH = 256
POOL_WINDOWS = (2, 4, 8, 16)
CONV_K = 31
FFN_CONV_K = 3
N_MOD = 6
N_CHIPS = 4
N_DEV = 8

ADAM_LR = 0.001
ADAM_B1 = 0.9
ADAM_B2 = 0.999
ADAM_EPS = 1e-08
ADAM_WD = 0.01
ADAM_STEP = 10

MIX_HALO = 32
FFN_HALO = 8
FFN_ROWS = 16
FFN_UNROLL = 8
LANES = 128
CONV_ROWS = 32
SMALL_COLS = 1024
VMEM_BYTES_V7X = 64 * 1024 * 1024


def _vmem_limit(estimate_bytes):
    return int(min(max(estimate_bytes, 16 * 1024 * 1024), VMEM_BYTES_V7X - 8 * 1024 * 1024))


def _row_tile(s, want):
    return want if s % want == 0 else math.gcd(s, want)


def _rsum(v):
    return jnp.sum(v, axis=0, keepdims=True)


def _rmean(v):
    return jnp.mean(v, axis=-1, keepdims=True)


def _gelu(v):
    k = math.sqrt(2.0 / math.pi)
    return 0.5 * v * (1.0 + jnp.tanh(k * (v + 0.044715 * v * v * v)))


def _gelu_grad(v):
    k = math.sqrt(2.0 / math.pi)
    t = jnp.tanh(k * (v + 0.044715 * v * v * v))
    return 0.5 * (1.0 + t) + 0.5 * v * (1.0 - t * t) * (k * (1.0 + 3.0 * 0.044715 * v * v))


def _dot(a, b):
    return jnp.dot(a, b, preferred_element_type=F32)


def _dot_nt(a, b):
    return lax.dot_general(a, b, (((1,), (1,)), ((), ())), preferred_element_type=F32)


def _dot_tn(a, b):
    return lax.dot_general(a, b, (((0,), (0,)), ((), ())), preferred_element_type=F32)


def _group_mean(v, bd):
    hi = v.astype(BF16)
    lo = (v - hi.astype(F32)).astype(BF16)
    return (_dot(hi, bd) + _dot(lo, bd)) * (1.0 / HEAD_DIM)


def _const_spec(shape):
    nd = len(shape)
    return pl.BlockSpec(shape, lambda *_: (0,) * nd)


def _hbm_call(body, **kw):
    call = pl.pallas_call(body, **kw)
    return lambda *args: call(*[pltpu.with_memory_space_constraint(a, pltpu.HBM) for a in args])


def _norm_mod_matmul(x, gain, s1p, shift, w, name, transposed=False):
    s, d = x.shape
    nb = w.shape[0]
    tn = w.shape[1] if transposed else w.shape[2]
    ts = _row_tile(s, 512)

    def body(x_ref, g_ref, s_ref, b_ref, w_ref, z_ref, h_ref):
        xv = x_ref[...]
        r = lax.rsqrt(_rmean(xv * xv) + EPS)
        h = ((xv * r) * g_ref[...] * s_ref[...] + b_ref[...]).astype(BF16)
        h_ref[...] = h
        for j in range(nb):
            z_ref[:, j * tn:(j + 1) * tn] = _dot_nt(h, w_ref[j]) if transposed else _dot(h, w_ref[j])

    vec = pl.BlockSpec((1, d), lambda i: (0, 0))
    return _hbm_call(
        body, name=name,
        grid=(s // ts,),
        in_specs=[pl.BlockSpec((ts, d), lambda i: (i, 0)), vec, vec, vec,
                  pl.BlockSpec(w.shape, lambda i: (0, 0, 0), pipeline_mode=pl.Buffered(1))],
        out_specs=[pl.BlockSpec((ts, nb * tn), lambda i: (i, 0)), pl.BlockSpec((ts, d), lambda i: (i, 0))],
        out_shape=[jax.ShapeDtypeStruct((s, nb * tn), F32), jax.ShapeDtypeStruct((s, d), BF16)],
        compiler_params=pltpu.CompilerParams(
            dimension_semantics=("arbitrary",),
            vmem_limit_bytes=_vmem_limit(2 * (ts * d * 4 + ts * nb * tn * 4 + ts * d * 2) + nb * d * tn * 2 + 4 * ts * d * 4)),
    )(x, gain, s1p, shift, w)


def _matmul_norm_resid(a, w, xres, gate, gpost, name):
    s, k = a.shape
    d = w.shape[1]
    ts = _row_tile(s, 1024)

    def body(a_ref, w_ref, x_ref, gate_ref, gp_ref, o_ref, xn_ref):
        o = _dot(a_ref[...], w_ref[...])
        o_ref[...] = o
        r = lax.rsqrt(_rmean(o * o) + EPS)
        xn_ref[...] = x_ref[...] + gate_ref[...] * ((o * r) * gp_ref[...])

    vec = pl.BlockSpec((1, d), lambda i: (0, 0))
    row = pl.BlockSpec((ts, d), lambda i: (i, 0))
    return _hbm_call(
        body, name=name,
        grid=(s // ts,),
        in_specs=[pl.BlockSpec((ts, k), lambda i: (i, 0)),
                  pl.BlockSpec((k, d), lambda i: (0, 0), pipeline_mode=pl.Buffered(1)), row, vec, vec],
        out_specs=[row, row],
        out_shape=[jax.ShapeDtypeStruct((s, d), F32)] * 2,
        compiler_params=pltpu.CompilerParams(
            dimension_semantics=("arbitrary",),
            vmem_limit_bytes=_vmem_limit(2 * (ts * k * 2 + 3 * ts * d * 4) + k * d * 2 + 4 * ts * d * 4)),
    )(a, w, xres, gate, gpost)


def _wgrad(a, b, acols, bcols, out_struct, out_index, out_block, name):
    s = a.shape[0]
    aw, afn = acols
    bw, bfn = bcols
    ts = _row_tile(s, 2048 if aw * bw <= 2 * 1024 * 1024 else 1024)
    nj = out_index[0]
    oidx = out_index[1]

    def body(a_ref, b_ref, o_ref, acc):
        i = pl.program_id(1)

        @pl.when(i == 0)
        def _():
            acc[...] = jnp.zeros_like(acc)

        acc[...] += _dot_tn(a_ref[...], b_ref[...])

        @pl.when(i == pl.num_programs(1) - 1)
        def _():
            o_ref[...] = acc[...].astype(o_ref.dtype)

    return _hbm_call(
        body, name=name,
        grid=(nj, s // ts),
        in_specs=[pl.BlockSpec((ts, aw), lambda j, i: (i, afn(j))), pl.BlockSpec((ts, bw), lambda j, i: (i, bfn(j)))],
        out_specs=pl.BlockSpec(out_block, lambda j, i: oidx(j)),
        out_shape=out_struct,
        scratch_shapes=[pltpu.VMEM((aw, bw), F32)],
        compiler_params=pltpu.CompilerParams(
            dimension_semantics=("arbitrary", "arbitrary"),
            vmem_limit_bytes=_vmem_limit(2 * (ts * aw * 2 + ts * bw * 2) + 3 * aw * bw * 4 + ts * aw * 4)),
    )(a, b)


def _loss_head(xo, tgt):
    s, d = xo.shape
    ts = _row_tile(s, 512)

    def body(x_ref, t_ref, dx_ref, l_ref, acc):
        i = pl.program_id(0)

        @pl.when(i == 0)
        def _():
            acc[...] = jnp.zeros_like(acc)

        e = x_ref[...] - t_ref[...]
        dx_ref[...] = e * (1.0 / d)
        acc[...] += _rsum(e * e)

        @pl.when(i == pl.num_programs(0) - 1)
        def _():
            tot = jnp.sum(acc[...], axis=-1, keepdims=True) * (0.5 / d)
            l_ref[...] = jnp.broadcast_to(tot, l_ref.shape)

    row = pl.BlockSpec((ts, d), lambda i: (i, 0))
    return _hbm_call(
        body, name="loss_head",
        grid=(s // ts,),
        in_specs=[row, row],
        out_specs=[row, pl.BlockSpec((1, SMALL_COLS), lambda i: (0, 0))],
        out_shape=[jax.ShapeDtypeStruct((s, d), F32), jax.ShapeDtypeStruct((1, SMALL_COLS), F32)],
        scratch_shapes=[pltpu.VMEM((1, d), F32)],
        compiler_params=pltpu.CompilerParams(dimension_semantics=("arbitrary",)),
    )(xo, tgt)


def _resid_bwd_matmul(dxn, o, gate, gpost, w, name):
    s, d = dxn.shape
    k = w.shape[0]
    ts = _row_tile(s, 512)

    def body(dx_ref, o_ref, gate_ref, gp_ref, w_ref, do_ref, da_ref, dgate_ref, dgp_ref):
        i = pl.program_id(0)

        @pl.when(i == 0)
        def _():
            dgate_ref[...] = jnp.zeros_like(dgate_ref)
            dgp_ref[...] = jnp.zeros_like(dgp_ref)

        dx = dx_ref[...]
        o = o_ref[...]
        r = lax.rsqrt(_rmean(o * o) + EPS)
        on = o * r
        dgate_ref[...] += _rsum(dx * (on * gp_ref[...]))
        don = dx * gate_ref[...]
        dgp_ref[...] += _rsum(don * on)
        t = don * gp_ref[...]
        do = (r * (t - on * _rmean(t * on))).astype(BF16)
        do_ref[...] = do
        da_ref[...] = _dot_nt(do, w_ref[...])

    vec = pl.BlockSpec((1, d), lambda i: (0, 0))
    row = pl.BlockSpec((ts, d), lambda i: (i, 0))
    return _hbm_call(
        body, name=name,
        grid=(s // ts,),
        in_specs=[row, row, vec, vec, pl.BlockSpec((k, d), lambda i: (0, 0), pipeline_mode=pl.Buffered(1))],
        out_specs=[row, pl.BlockSpec((ts, k), lambda i: (i, 0)), vec, vec],
        out_shape=[jax.ShapeDtypeStruct((s, d), BF16), jax.ShapeDtypeStruct((s, k), F32),
                   jax.ShapeDtypeStruct((1, d), F32), jax.ShapeDtypeStruct((1, d), F32)],
        compiler_params=pltpu.CompilerParams(
            dimension_semantics=("arbitrary",),
            vmem_limit_bytes=_vmem_limit(2 * (2 * ts * d * 4 + ts * d * 2 + ts * k * 4) + d * k * 2 + 6 * ts * d * 4)),
    )(dxn, o, gate, gpost, w)


def _norm_mod_bwd(dh, xv, gain, s1p, dres):
    r = lax.rsqrt(_rmean(xv * xv) + EPS)
    xn = xv * r
    dshift = _rsum(dh)
    t = dh * xn
    dscale = _rsum(t * gain)
    dgain = _rsum(t * s1p)
    dxn = dh * (gain * s1p)
    dx = r * (dxn - xn * _rmean(dxn * xn)) + dres
    return dx, dshift, dscale, dgain


def _lane_lt(shape, bound):
    return lax.broadcasted_iota(jnp.int32, shape, 1) < bound


def _sgu_forward(z_ref, bd_ref, ng_ref, nb_ref, wm_ref, bias_ref, ts, ya_s, f_s):
    u = _gelu(z_ref[:, 0:SGU_WIDTH])
    v = _gelu(z_ref[:, SGU_WIDTH:2 * SGU_WIDTH])
    bd = bd_ref[...]
    vc = v - _group_mean(v, bd)
    rstd = lax.rsqrt(_group_mean(vc * vc, bd) + EPS)
    vhat = vc * rstd
    vn = (vhat * ng_ref[...] + nb_ref[...]).astype(BF16)
    left = _lane_lt((CHUNK, CHUNK), HEAD_DIM)
    for n in range(ts // CHUNK):
        rows = slice(n * CHUNK, (n + 1) * CHUNK)
        for p in range(SGU_WIDTH // CHUNK):
            cols = slice(p * CHUNK, (p + 1) * CHUNK)
            blk = vn[rows, cols]
            f = jnp.where(left, _dot(wm_ref[2 * p], blk), _dot(wm_ref[2 * p + 1], blk)) + bias_ref[:, cols]
            if f_s is not None:
                f_s[rows, cols] = f
            ya_s[rows, cols] = u[rows, cols] * f
    return u, vhat, rstd, vn


def _shifted_copies(ext, ext8, ts):
    for b in range(1, 8):
        ext8[b - 1] = ext[pl.ds(b, ts + MIX_HALO - 8), :]


def _rows_at(ext, ext8, start, nrows):
    b = start % 8
    return ext[pl.ds(start, nrows), :] if b == 0 else ext8[b - 1, pl.ds(start - b, nrows), :]


def _conv31_forward(z_ref, zh_ref, first, cw_ref, cb_ref, ts, ext_b, ext8, cbs):
    a = z_ref[:, 2 * SGU_WIDTH:2 * SGU_WIDTH + CONV_WIDTH]
    g = z_ref[:, 2 * SGU_WIDTH + CONV_WIDTH:2 * SGU_WIDTH + 2 * CONV_WIDTH]
    ah = zh_ref[:, 2 * SGU_WIDTH:2 * SGU_WIDTH + CONV_WIDTH]
    gh = zh_ref[:, 2 * SGU_WIDTH + CONV_WIDTH:2 * SGU_WIDTH + 2 * CONV_WIDTH]
    ext_b[pl.ds(0, MIX_HALO), :] = jnp.where(first, 0.0, ah * jax.nn.sigmoid(gh))
    ext_b[pl.ds(MIX_HALO, ts), :] = a * jax.nn.sigmoid(g)
    _shifted_copies(ext_b, ext8, ts)
    for r in range(ts // CONV_ROWS):
        acc = jnp.broadcast_to(cb_ref[...], (CONV_ROWS, CONV_WIDTH))
        for k in range(CONV_K):
            acc = acc + cw_ref[k:k + 1, :] * _rows_at(ext_b, ext8, MIX_HALO - (CONV_K - 1) + k + r * CONV_ROWS, CONV_ROWS)
        cbs[pl.ds(r * CONV_ROWS, CONV_ROWS), :] = acc


def _pool_counts(i, ts):
    pos1 = (i * ts + 1 + lax.broadcasted_iota(jnp.int32, (ts, POOL_WIDTH), 0)).astype(F32)
    lane = lax.broadcasted_iota(jnp.int32, (ts, POOL_WIDTH), 1)
    gdim = POOL_WIDTH // len(POOL_WINDOWS)
    win = jnp.where(lane < gdim, float(POOL_WINDOWS[0]),
                    jnp.where(lane < 2 * gdim, float(POOL_WINDOWS[1]),
                              jnp.where(lane < 3 * gdim, float(POOL_WINDOWS[2]), float(POOL_WINDOWS[3]))))
    return jnp.minimum(pos1, win)


def _window_sums(ext, base, ts, sign):
    lane = lax.broadcasted_iota(jnp.int32, (ts, POOL_WIDTH), 1)
    gdim = POOL_WIDTH // len(POOL_WINDOWS)
    run = jnp.zeros((ts, POOL_WIDTH), F32)
    out = jnp.zeros((ts, POOL_WIDTH), F32)
    for m in range(POOL_WINDOWS[-1]):
        run = run + ext[pl.ds(base + sign * m, ts), :]
        for gi, win in enumerate(POOL_WINDOWS):
            if m == win - 1:
                out = jnp.where((lane >= gi * gdim) & (lane < (gi + 1) * gdim), run, out)
    return out


def _pool_forward(z_ref, zh_ref, first, i, ts, ext_c):
    c0 = 2 * SGU_WIDTH + 2 * CONV_WIDTH
    zc = z_ref[:, c0:c0 + POOL_WIDTH]
    ext_c[pl.ds(0, MIX_HALO), :] = jnp.where(first, 0.0, zh_ref[:, c0:c0 + POOL_WIDTH])
    ext_c[pl.ds(MIX_HALO, ts), :] = zc
    sums = _window_sums(ext_c, MIX_HALO, ts, -1)
    return sums / _pool_counts(i, ts) - zc


def _layer_norm_rows(v):
    mu = _rmean(v)
    vc = v - mu
    rstd = lax.rsqrt(_rmean(vc * vc) + EPS)
    return vc * rstd, rstd


def _mixer_specs(s, ts, width):
    nbh = ts // MIX_HALO
    tile = pl.BlockSpec((ts, width), lambda i: (i, 0))
    prev = pl.BlockSpec((MIX_HALO, width), lambda i: (jnp.maximum(i * nbh - 1, 0), 0))
    nxt = pl.BlockSpec((MIX_HALO, width), lambda i: (jnp.minimum((i + 1) * nbh, s // MIX_HALO - 1), 0))
    return tile, prev, nxt


def _mixer_fwd(z, mp, name):
    s, inw = z.shape
    d = SGU_WIDTH + CONV_WIDTH + POOL_WIDTH
    ts = _row_tile(s, 256)

    def body(z_ref, zh_ref, bd_ref, ng_ref, nb_ref, wm_ref, bias_ref, cw_ref, cb_ref, cng_ref, cnb_ref,
             pw_ref, ps_ref, bg_ref, y_ref, cbs, dpool_ref, ya_s, ext_b, ext_c, ext8):
        i = pl.program_id(0)
        first = i == 0
        _sgu_forward(z_ref, bd_ref, ng_ref, nb_ref, wm_ref, bias_ref, ts, ya_s, None)
        ya = ya_s[...]
        ra = lax.rsqrt(_rmean(ya * ya) + EPS)
        y_ref[:, 0:SGU_WIDTH] = ((ya * ra) * bg_ref[:, 0:SGU_WIDTH]).astype(BF16)

        _conv31_forward(z_ref, zh_ref, first, cw_ref, cb_ref, ts, ext_b, ext8, cbs)
        chat, _ = _layer_norm_rows(cbs[...])
        lin = chat * cng_ref[...] + cnb_ref[...]
        yb = lin * jax.nn.sigmoid(lin)
        rb = lax.rsqrt(_rmean(yb * yb) + EPS)
        y_ref[:, SGU_WIDTH:SGU_WIDTH + CONV_WIDTH] = ((yb * rb) * bg_ref[:, SGU_WIDTH:SGU_WIDTH + CONV_WIDTH]).astype(BF16)

        dpool = _pool_forward(z_ref, zh_ref, first, i, ts, ext_c)
        dpool_ref[...] = dpool
        yc = _dot(dpool.astype(BF16), pw_ref[...]) * ps_ref[...]
        rc = lax.rsqrt(_rmean(yc * yc) + EPS)
        y_ref[:, SGU_WIDTH + CONV_WIDTH:d] = ((yc * rc) * bg_ref[:, SGU_WIDTH + CONV_WIDTH:d]).astype(BF16)

    tile, prev, _ = _mixer_specs(s, ts, inw)
    consts = [mp["bd"], mp["ng"], mp["nb"], mp["wm"], mp["bias"], mp["cw"], mp["cb"], mp["cng"], mp["cnb"],
              mp["pw"], mp["ps"], mp["bg"]]
    return _hbm_call(
        body, name=name,
        grid=(s // ts,),
        in_specs=[tile, prev] + [_const_spec(c.shape) for c in consts],
        out_specs=[pl.BlockSpec((ts, d), lambda i: (i, 0)), pl.BlockSpec((ts, CONV_WIDTH), lambda i: (i, 0)),
                   pl.BlockSpec((ts, POOL_WIDTH), lambda i: (i, 0))],
        out_shape=[jax.ShapeDtypeStruct((s, d), BF16), jax.ShapeDtypeStruct((s, CONV_WIDTH), F32),
                   jax.ShapeDtypeStruct((s, POOL_WIDTH), F32)],
        scratch_shapes=[pltpu.VMEM((ts, SGU_WIDTH), F32), pltpu.VMEM((ts + MIX_HALO, CONV_WIDTH), F32),
                        pltpu.VMEM((ts + MIX_HALO, POOL_WIDTH), F32), pltpu.VMEM((7, ts + MIX_HALO - 8, CONV_WIDTH), F32)],
        compiler_params=pltpu.CompilerParams(dimension_semantics=("arbitrary",),
                                             vmem_limit_bytes=_vmem_limit(16 * ts * inw * 4)),
    )(z, z, *consts)


def _mixer_bwd_a(z, cb, dpool, dy, mp, name):
    s, inw = z.shape
    d = SGU_WIDTH + CONV_WIDTH + POOL_WIDTH
    ts = _row_tile(s, 256)
    nchunk = ts // CHUNK

    def rms_bwd(dyn, y, g):
        r = lax.rsqrt(_rmean(y * y) + EPS)
        yn = y * r
        dg = _rsum(dyn * yn)
        t = dyn * g
        return r * (t - yn * _rmean(t * yn)), dg

    def body(z_ref, cbs, dpool_ref, dy_ref, bd_ref, ng_ref, nb_ref, wm_ref, wmt_ref, bias_ref, cng_ref, cnb_ref,
             pw_ref, pwt_ref, ps_ref, bg_ref,
             dza_ref, dcb_ref, dd_ref, dbg_ref, dwm_ref, dbias_ref, dng_ref, dnb_ref, dcng_ref, dcnb_ref, dps_ref, dpw_ref,
             ya_s, f_s, dvn_s):
        i = pl.program_id(0)
        first = i == 0

        @pl.when(first)
        def _():
            for ref in (dwm_ref, dbias_ref, dng_ref, dnb_ref, dcng_ref, dcnb_ref, dps_ref, dpw_ref):
                ref[...] = jnp.zeros_like(ref)

        u, vhat, rstd, vn = _sgu_forward(z_ref, bd_ref, ng_ref, nb_ref, wm_ref, bias_ref, ts, ya_s, f_s)
        dya, dbg_a = rms_bwd(dy_ref[:, 0:SGU_WIDTH], ya_s[...], bg_ref[:, 0:SGU_WIDTH])
        du = dya * f_s[...]
        df = dya * u
        dfb = df.astype(BF16)
        left = _lane_lt((CHUNK, CHUNK), HEAD_DIM)
        zero = jnp.zeros((CHUNK, CHUNK), BF16)
        dbias = jnp.zeros((CHUNK, SGU_WIDTH), F32)
        for n in range(nchunk):
            rows = slice(n * CHUNK, (n + 1) * CHUNK)
            dbias = dbias + df[rows, :]
            for p in range(SGU_WIDTH // CHUNK):
                cols = slice(p * CHUNK, (p + 1) * CHUNK)
                dblk = dfb[rows, cols]
                vblk = vn[rows, cols]
                dwm_ref[2 * p] += _dot_nt(jnp.where(left, dblk, zero), vblk)
                dwm_ref[2 * p + 1] += _dot_nt(jnp.where(left, zero, dblk), vblk)
                dvn_s[rows, cols] = jnp.where(left, _dot(wmt_ref[2 * p], dblk), _dot(wmt_ref[2 * p + 1], dblk))
        dbias_ref[...] += dbias
        dvn = dvn_s[...]
        dng_ref[...] += _rsum(dvn * vhat)
        dnb_ref[...] += _rsum(dvn)
        dvh = dvn * ng_ref[...]
        bd = bd_ref[...]
        dv = rstd * (dvh - _group_mean(dvh, bd) - vhat * _group_mean(dvh * vhat, bd))
        dza_ref[:, 0:SGU_WIDTH] = (du * _gelu_grad(z_ref[:, 0:SGU_WIDTH])).astype(BF16)
        dza_ref[:, SGU_WIDTH:2 * SGU_WIDTH] = (dv * _gelu_grad(z_ref[:, SGU_WIDTH:2 * SGU_WIDTH])).astype(BF16)

        chat, crstd = _layer_norm_rows(cbs[...])
        lin = chat * cng_ref[...] + cnb_ref[...]
        sl = jax.nn.sigmoid(lin)
        dyb, dbg_b = rms_bwd(dy_ref[:, SGU_WIDTH:SGU_WIDTH + CONV_WIDTH], lin * sl, bg_ref[:, SGU_WIDTH:SGU_WIDTH + CONV_WIDTH])
        dlin = dyb * (sl * (1.0 + lin * (1.0 - sl)))
        dcng_ref[...] += _rsum(dlin * chat)
        dcnb_ref[...] += _rsum(dlin)
        dch = dlin * cng_ref[...]
        dcb_ref[...] = crstd * (dch - _rmean(dch) - chat * _rmean(dch * chat))

        dpb = dpool_ref[...].astype(BF16)
        ycp = _dot(dpb, pw_ref[...])
        dyc, dbg_c = rms_bwd(dy_ref[:, SGU_WIDTH + CONV_WIDTH:d], ycp * ps_ref[...], bg_ref[:, SGU_WIDTH + CONV_WIDTH:d])
        dps_ref[...] += _rsum(dyc * ycp)
        dycp = (dyc * ps_ref[...]).astype(BF16)
        dpw_ref[...] += _dot_tn(dpb, dycp)
        dd_ref[...] = _dot(dycp, pwt_ref[...])

        @pl.when(first)
        def _():
            dbg_ref[...] = jnp.zeros_like(dbg_ref)

        dbg_ref[:, 0:SGU_WIDTH] += dbg_a
        dbg_ref[:, SGU_WIDTH:SGU_WIDTH + CONV_WIDTH] += dbg_b
        dbg_ref[:, SGU_WIDTH + CONV_WIDTH:d] += dbg_c

    tile, _, _ = _mixer_specs(s, ts, inw)
    consts = [mp["bd"], mp["ng"], mp["nb"], mp["wm"], mp["wmt"], mp["bias"], mp["cng"], mp["cnb"],
              mp["pw"], mp["pwt"], mp["ps"], mp["bg"]]
    acc_shapes = [(1, d), (2 * (SGU_WIDTH // CHUNK), CHUNK, CHUNK), (CHUNK, SGU_WIDTH), (1, SGU_WIDTH), (1, SGU_WIDTH),
                  (1, CONV_WIDTH), (1, CONV_WIDTH), (1, POOL_WIDTH), (POOL_WIDTH, POOL_WIDTH)]
    return _hbm_call(
        body, name=name,
        grid=(s // ts,),
        in_specs=[tile, pl.BlockSpec((ts, CONV_WIDTH), lambda i: (i, 0)), pl.BlockSpec((ts, POOL_WIDTH), lambda i: (i, 0)),
                  pl.BlockSpec((ts, d), lambda i: (i, 0))] + [_const_spec(c.shape) for c in consts],
        out_specs=[pl.BlockSpec((ts, 2 * SGU_WIDTH), lambda i: (i, 0)), pl.BlockSpec((ts, CONV_WIDTH), lambda i: (i, 0)),
                   pl.BlockSpec((ts, POOL_WIDTH), lambda i: (i, 0))] + [_const_spec(a) for a in acc_shapes],
        out_shape=[jax.ShapeDtypeStruct((s, 2 * SGU_WIDTH), BF16), jax.ShapeDtypeStruct((s, CONV_WIDTH), F32),
                   jax.ShapeDtypeStruct((s, POOL_WIDTH), F32)] + [jax.ShapeDtypeStruct(a, F32) for a in acc_shapes],
        scratch_shapes=[pltpu.VMEM((ts, SGU_WIDTH), F32), pltpu.VMEM((ts, SGU_WIDTH), F32), pltpu.VMEM((ts, SGU_WIDTH), F32)],
        compiler_params=pltpu.CompilerParams(dimension_semantics=("arbitrary",),
                                             vmem_limit_bytes=_vmem_limit(24 * ts * inw * 4)),
    )(z, cb, dpool, dy, *consts)


def _mixer_bwd_b(z, dza, dcb, dd, x, dres, gain, s1p, cw, w, name):
    s, inw = z.shape
    d = x.shape[1]
    ts = _row_tile(s, 256)
    c0 = 2 * SGU_WIDTH
    c1 = c0 + 2 * CONV_WIDTH

    def body(z_ref, zh_ref, dza_ref, dcb_ref, dcbn_ref, dd_ref, ddn_ref, x_ref, dres_ref, g_ref, s_ref, cw_ref, w_ref,
             dx_ref, dz_ref, dsh_ref, dsc_ref, dg_ref, dcw_ref, dcbias_ref, ext_b, ext_n, ext_e, ext8):
        i = pl.program_id(0)
        first = i == 0
        last = i == pl.num_programs(0) - 1

        @pl.when(first)
        def _():
            for ref in (dsh_ref, dsc_ref, dg_ref, dcw_ref, dcbias_ref):
                ref[...] = jnp.zeros_like(ref)

        a = z_ref[:, c0:c0 + CONV_WIDTH]
        sg = jax.nn.sigmoid(z_ref[:, c0 + CONV_WIDTH:c1])
        ah = zh_ref[:, c0:c0 + CONV_WIDTH]
        gh = zh_ref[:, c0 + CONV_WIDTH:c1]
        ext_b[pl.ds(0, MIX_HALO), :] = jnp.where(first, 0.0, ah * jax.nn.sigmoid(gh))
        ext_b[pl.ds(MIX_HALO, ts), :] = a * sg
        _shifted_copies(ext_b, ext8, ts)
        dcbv = dcb_ref[...]
        dcbias_ref[...] += _rsum(dcbv)
        for k in range(CONV_K):
            dcw_ref[k:k + 1, :] += _rsum(dcbv * _rows_at(ext_b, ext8, MIX_HALO - (CONV_K - 1) + k, ts))

        ext_n[pl.ds(0, ts), :] = dcbv
        ext_n[pl.ds(ts, MIX_HALO), :] = jnp.where(last, 0.0, dcbn_ref[...])
        _shifted_copies(ext_n, ext8, ts)
        for r in range(ts // CONV_ROWS):
            acc = jnp.zeros((CONV_ROWS, CONV_WIDTH), F32)
            for k in range(CONV_K):
                acc = acc + cw_ref[k:k + 1, :] * _rows_at(ext_n, ext8, CONV_K - 1 - k + r * CONV_ROWS, CONV_ROWS)
            rows = pl.ds(r * CONV_ROWS, CONV_ROWS)
            ar = z_ref[rows, c0:c0 + CONV_WIDTH]
            sr = jax.nn.sigmoid(z_ref[rows, c0 + CONV_WIDTH:c1])
            dz_ref[rows, c0:c0 + CONV_WIDTH] = (acc * sr).astype(BF16)
            dz_ref[rows, c0 + CONV_WIDTH:c1] = (acc * ar * sr * (1.0 - sr)).astype(BF16)

        ddv = dd_ref[...]
        ext_e[pl.ds(0, ts), :] = ddv / _pool_counts(i, ts)
        nh = (i + 1) * ts + lax.broadcasted_iota(jnp.int32, (MIX_HALO, POOL_WIDTH), 0)
        lane = lax.broadcasted_iota(jnp.int32, (MIX_HALO, POOL_WIDTH), 1)
        gdim = POOL_WIDTH // len(POOL_WINDOWS)
        winh = jnp.where(lane < gdim, float(POOL_WINDOWS[0]),
                         jnp.where(lane < 2 * gdim, float(POOL_WINDOWS[1]),
                                   jnp.where(lane < 3 * gdim, float(POOL_WINDOWS[2]), float(POOL_WINDOWS[3]))))
        cnth = jnp.minimum((nh + 1).astype(F32), winh)
        ext_e[pl.ds(ts, MIX_HALO), :] = jnp.where(last, 0.0, ddn_ref[...] / cnth)
        dz_ref[:, c1:inw] = (_window_sums(ext_e, 0, ts, 1) - ddv).astype(BF16)
        dz_ref[:, 0:c0] = dza_ref[...]

        dh = _dot(dz_ref[...], w_ref[...])
        dx, dsh, dsc, dg = _norm_mod_bwd(dh, x_ref[...], g_ref[...], s_ref[...], dres_ref[...])
        dx_ref[...] = dx
        dsh_ref[...] += dsh
        dsc_ref[...] += dsc
        dg_ref[...] += dg

    tile, prev, _ = _mixer_specs(s, ts, inw)
    _, _, nxt_b = _mixer_specs(s, ts, CONV_WIDTH)
    _, _, nxt_c = _mixer_specs(s, ts, POOL_WIDTH)
    row = pl.BlockSpec((ts, d), lambda i: (i, 0))
    vec = pl.BlockSpec((1, d), lambda i: (0, 0))
    return _hbm_call(
        body, name=name,
        grid=(s // ts,),
        in_specs=[tile, prev, pl.BlockSpec((ts, c0), lambda i: (i, 0)),
                  pl.BlockSpec((ts, CONV_WIDTH), lambda i: (i, 0)), nxt_b,
                  pl.BlockSpec((ts, POOL_WIDTH), lambda i: (i, 0)), nxt_c,
                  row, row, vec, vec, _const_spec(cw.shape),
                  pl.BlockSpec(w.shape, lambda i: (0, 0), pipeline_mode=pl.Buffered(1))],
        out_specs=[row, pl.BlockSpec((ts, inw), lambda i: (i, 0)), vec, vec, vec,
                   _const_spec((CONV_K, CONV_WIDTH)), _const_spec((1, CONV_WIDTH))],
        out_shape=[jax.ShapeDtypeStruct((s, d), F32), jax.ShapeDtypeStruct((s, inw), BF16)]
        + [jax.ShapeDtypeStruct((1, d), F32)] * 3
        + [jax.ShapeDtypeStruct((CONV_K, CONV_WIDTH), F32), jax.ShapeDtypeStruct((1, CONV_WIDTH), F32)],
        scratch_shapes=[pltpu.VMEM((ts + MIX_HALO, CONV_WIDTH), F32), pltpu.VMEM((ts + MIX_HALO, CONV_WIDTH), F32),
                        pltpu.VMEM((ts + MIX_HALO, POOL_WIDTH), F32), pltpu.VMEM((7, ts + MIX_HALO - 8, CONV_WIDTH), F32)],
        compiler_params=pltpu.CompilerParams(dimension_semantics=("arbitrary",),
                                             vmem_limit_bytes=_vmem_limit(18 * ts * inw * 4 + inw * d * 2)),
    )(z, z, dza, dcb, dcb, dd, dd, x, dres, gain, s1p, cw, w)


def _ffn_specs(s, ts, tc, half_blocks):
    nbh = ts // FFN_HALO

    def tile(off):
        return pl.BlockSpec((ts, tc), lambda j, i: (i, j + off))

    def prev(off):
        return pl.BlockSpec((FFN_HALO, tc), lambda j, i: (jnp.maximum(i * nbh - 1, 0), j + off))

    def vec(rows, off):
        return pl.BlockSpec((rows, tc), lambda j, i: (0, j + off))

    return tile, prev, vec


def _rows_before(cur, prev, k):
    row = lax.broadcasted_iota(jnp.int32, cur.shape, 0)
    return jnp.where(row >= k, pltpu.roll(cur, k, 0), pltpu.roll(prev, k, 0))


def _conv3_rows(cur, prev, w_ref, b_ref, cols):
    x1 = _rows_before(cur, prev, 1)
    x2 = _rows_before(cur, prev, 2)
    u = b_ref[:, cols] + w_ref[2:3, cols] * cur + w_ref[1:2, cols] * x1 + w_ref[0:1, cols] * x2
    return u, x2, x1


def _halo_chunk(h_ref, cols, first):
    h = jnp.where(first, 0.0, h_ref[:, cols])
    return jnp.concatenate([h] * (FFN_ROWS // FFN_HALO), axis=0)


def _ffn_act_fwd(p, cw, cb, name):
    s, f2 = p.shape
    f = f2 // 2
    tc = f // 2
    hb = f // tc
    ts = _row_tile(s, 256)

    def body(pg_ref, pgh_ref, pv_ref, pvh_ref, wg_ref, wv_ref, bg_ref, bv_ref, act_ref):
        first = pl.program_id(1) == 0
        for c in range(tc // LANES):
            cols = slice(c * LANES, (c + 1) * LANES)

            def chunk(r, carry, cols=cols):
                pg_prev, pv_prev = carry
                rows = pl.ds(pl.multiple_of(r * FFN_ROWS, FFN_ROWS), FFN_ROWS)
                pg = pg_ref[rows, cols]
                pv = pv_ref[rows, cols]
                ug, _, _ = _conv3_rows(pg, pg_prev, wg_ref, bg_ref, cols)
                uv, _, _ = _conv3_rows(pv, pv_prev, wv_ref, bv_ref, cols)
                act_ref[rows, cols] = (_gelu(ug) * uv).astype(BF16)
                return pg, pv

            def step(r, carry, chunk=chunk):
                for u in range(FFN_UNROLL):
                    carry = chunk(r * FFN_UNROLL + u, carry)
                return carry

            lax.fori_loop(0, ts // (FFN_ROWS * FFN_UNROLL), step,
                          (_halo_chunk(pgh_ref, cols, first), _halo_chunk(pvh_ref, cols, first)))

    tile, prev, vec = _ffn_specs(s, ts, tc, hb)
    return _hbm_call(
        body, name=name,
        grid=(hb, s // ts),
        in_specs=[tile(0), prev(0), tile(hb), prev(hb), vec(FFN_CONV_K, 0), vec(FFN_CONV_K, hb), vec(1, 0), vec(1, hb)],
        out_specs=pl.BlockSpec((ts, tc), lambda j, i: (i, j)),
        out_shape=jax.ShapeDtypeStruct((s, f), BF16),
        compiler_params=pltpu.CompilerParams(dimension_semantics=("arbitrary", "arbitrary"),
                                             vmem_limit_bytes=_vmem_limit(8 * ts * tc * 4)),
    )(p, p, p, p, cw, cw, cb, cb)


def _ffn_act_bwd(p, dact, cw, cb, name):
    s, f2 = p.shape
    f = f2 // 2
    tc = f // 2
    hb = f // tc
    ts = _row_tile(s, 256)

    def body(pg_ref, pgh_ref, pv_ref, pvh_ref, da_ref, wg_ref, wv_ref, bg_ref, bv_ref,
             dug_ref, duv_ref, dwg_ref, dwv_ref, dbg_ref, dbv_ref):
        first = pl.program_id(1) == 0

        @pl.when(first)
        def _():
            for ref in (dwg_ref, dwv_ref, dbg_ref, dbv_ref):
                ref[...] = jnp.zeros_like(ref)

        zero = jnp.zeros((FFN_ROWS, LANES), F32)
        for c in range(tc // LANES):
            cols = slice(c * LANES, (c + 1) * LANES)

            def chunk(r, carry, cols=cols):
                pg_prev, pv_prev, ag0, ag1, ag2, av0, av1, av2, sg, sv = carry
                rows = pl.ds(pl.multiple_of(r * FFN_ROWS, FFN_ROWS), FFN_ROWS)
                pg = pg_ref[rows, cols]
                pv = pv_ref[rows, cols]
                ug, pg2, pg1 = _conv3_rows(pg, pg_prev, wg_ref, bg_ref, cols)
                uv, pv2, pv1 = _conv3_rows(pv, pv_prev, wv_ref, bv_ref, cols)
                da = da_ref[rows, cols]
                dug = da * uv * _gelu_grad(ug)
                duv = da * _gelu(ug)
                dug_ref[rows, cols] = dug
                duv_ref[rows, cols] = duv
                return (pg, pv, ag0 + dug * pg2, ag1 + dug * pg1, ag2 + dug * pg,
                        av0 + duv * pv2, av1 + duv * pv1, av2 + duv * pv, sg + dug, sv + duv)

            def step(r, carry, chunk=chunk):
                for u in range(FFN_UNROLL):
                    carry = chunk(r * FFN_UNROLL + u, carry)
                return carry

            out = lax.fori_loop(0, ts // (FFN_ROWS * FFN_UNROLL), step,
                                (_halo_chunk(pgh_ref, cols, first), _halo_chunk(pvh_ref, cols, first)) + (zero,) * 8)
            for k in range(FFN_CONV_K):
                dwg_ref[k:k + 1, cols] += _rsum(out[2 + k])
                dwv_ref[k:k + 1, cols] += _rsum(out[5 + k])
            dbg_ref[:, cols] += _rsum(out[8])
            dbv_ref[:, cols] += _rsum(out[9])

    tile, prev, vec = _ffn_specs(s, ts, tc, hb)
    half = pl.BlockSpec((ts, tc), lambda j, i: (i, j))
    wacc = pl.BlockSpec((FFN_CONV_K, tc), lambda j, i: (0, j))
    bacc = pl.BlockSpec((1, tc), lambda j, i: (0, j))
    return _hbm_call(
        body, name=name,
        grid=(hb, s // ts),
        in_specs=[tile(0), prev(0), tile(hb), prev(hb), half, vec(FFN_CONV_K, 0), vec(FFN_CONV_K, hb), vec(1, 0), vec(1, hb)],
        out_specs=[half, half, wacc, wacc, bacc, bacc],
        out_shape=[jax.ShapeDtypeStruct((s, f), F32)] * 2 + [jax.ShapeDtypeStruct((FFN_CONV_K, f), F32)] * 2
        + [jax.ShapeDtypeStruct((1, f), F32)] * 2,
        compiler_params=pltpu.CompilerParams(dimension_semantics=("arbitrary", "arbitrary"),
                                             vmem_limit_bytes=_vmem_limit(12 * ts * tc * 4)),
    )(p, p, p, p, dact, cw, cw, cb, cb)


def _ffn_in_bwd(dug, duv, cw, w, x, dres, gain, s1p, name):
    s, f = dug.shape
    d = x.shape[1]
    ts = _row_tile(s, 256)
    tc = w.shape[2]
    assert f % tc == 0 and w.shape[0] * tc == 2 * f
    nbh = ts // FFN_HALO

    def body(dug_ref, dugn_ref, duv_ref, duvn_ref, cw_ref, w_ref, x_ref, dres_ref, g_ref, s_ref,
             dx_ref, dp_ref, dsh_ref, dsc_ref, dg_ref, ext):
        i = pl.program_id(0)
        last = i == pl.num_programs(0) - 1

        @pl.when(i == 0)
        def _():
            for ref in (dsh_ref, dsc_ref, dg_ref):
                ref[...] = jnp.zeros_like(ref)

        dh = jnp.zeros((ts, d), F32)
        for half, (t_ref, n_ref) in enumerate(((dug_ref, dugn_ref), (duv_ref, duvn_ref))):
            for cb in range(f // tc):
                cols = slice(cb * tc, (cb + 1) * tc)
                wcols = slice(half * f + cb * tc, half * f + (cb + 1) * tc)
                ext[pl.ds(0, ts), :] = t_ref[:, cols]
                ext[pl.ds(ts, FFN_HALO), :] = jnp.where(last, 0.0, n_ref[:, cols])
                acc = cw_ref[FFN_CONV_K - 1:FFN_CONV_K, wcols] * t_ref[:, cols]
                for k in range(FFN_CONV_K - 1):
                    acc = acc + cw_ref[k:k + 1, wcols] * ext[pl.ds(FFN_CONV_K - 1 - k, ts), :]
                dpb = acc.astype(BF16)
                dp_ref[:, wcols] = dpb
                dh = dh + _dot_nt(dpb, w_ref[half * (f // tc) + cb])
        dx, dsh, dsc, dg = _norm_mod_bwd(dh, x_ref[...], g_ref[...], s_ref[...], dres_ref[...])
        dx_ref[...] = dx
        dsh_ref[...] += dsh
        dsc_ref[...] += dsc
        dg_ref[...] += dg

    tile = pl.BlockSpec((ts, f), lambda i: (i, 0))
    nxt = pl.BlockSpec((FFN_HALO, f), lambda i: (jnp.minimum((i + 1) * nbh, s // FFN_HALO - 1), 0))
    row = pl.BlockSpec((ts, d), lambda i: (i, 0))
    vec = pl.BlockSpec((1, d), lambda i: (0, 0))
    return _hbm_call(
        body, name=name,
        grid=(s // ts,),
        in_specs=[tile, nxt, tile, nxt, _const_spec(cw.shape),
                  pl.BlockSpec(w.shape, lambda i: (0, 0, 0), pipeline_mode=pl.Buffered(1)), row, row, vec, vec],
        out_specs=[row, pl.BlockSpec((ts, 2 * f), lambda i: (i, 0)), vec, vec, vec],
        out_shape=[jax.ShapeDtypeStruct((s, d), F32), jax.ShapeDtypeStruct((s, 2 * f), BF16)] + [jax.ShapeDtypeStruct((1, d), F32)] * 3,
        scratch_shapes=[pltpu.VMEM((ts + FFN_HALO, tc), F32)],
        compiler_params=pltpu.CompilerParams(
            dimension_semantics=("arbitrary",),
            vmem_limit_bytes=_vmem_limit(4 * ts * f * 4 + 2 * f * d * 2 + 2 * ts * 2 * f * 2 + 12 * ts * d * 4 + 6 * ts * tc * 4)),
    )(dug, dug, duv, duv, cw, w, x, dres, gain, s1p)


def _adamw_math(w, g, m, v):
    m = ADAM_B1 * m + (1.0 - ADAM_B1) * g
    v = ADAM_B2 * v + (1.0 - ADAM_B2) * (g * g)
    m_hat = m / (1.0 - ADAM_B1 ** ADAM_STEP)
    v_hat = v / (1.0 - ADAM_B2 ** ADAM_STEP)
    delta = -ADAM_LR * (m_hat / (jnp.sqrt(v_hat) + ADAM_EPS) + ADAM_WD * w)
    return delta, m, v


def _adam_rows(rows, cols):
    want = max(8, (2 * 1024 * 1024 // (cols * 4)) // 8 * 8)
    tr = min(rows, want)
    while rows % tr:
        tr -= 8
    return tr


def _adamw(w, m, v, g_parts, name):
    shape = w.shape
    nl = shape[0] if w.ndim == 3 else 1
    r, c = shape[-2], shape[-1]
    tr = _adam_rows(r, c)
    ng = len(g_parts)

    def body(*refs):
        w_ref, m_ref, v_ref = refs[0:3]
        g_refs = refs[3:3 + ng]
        g_out, d_out, m_out, v_out = refs[3 + ng:]
        g = g_refs[0][...]
        for gr in g_refs[1:]:
            g = g + gr[...]
        delta, mn, vn = _adamw_math(w_ref[...], g, m_ref[...], v_ref[...])
        g_out[...] = g
        d_out[...] = delta
        m_out[...] = mn
        v_out[...] = vn

    blk = pl.BlockSpec((None, tr, c), lambda l, i: (l, i, 0))
    outs = _hbm_call(
        body, name=name,
        grid=(nl, r // tr),
        in_specs=[blk] * (3 + ng),
        out_specs=[blk] * 4,
        out_shape=[jax.ShapeDtypeStruct((nl, r, c), F32)] * 4,
        compiler_params=pltpu.CompilerParams(dimension_semantics=("arbitrary", "arbitrary"),
                                             vmem_limit_bytes=_vmem_limit(2 * (7 + ng) * tr * max(c, 128) * 4 + (8 << 20))),
    )(*[a.reshape(nl, r, c) for a in (w, m, v, *g_parts)])
    return [o.reshape(shape) for o in outs]


def _adamw_many(ws, ms, vs, gs, name):
    n = len(ws)

    def body(*refs):
        for k in range(n):
            w_ref, m_ref, v_ref, g_ref = refs[k], refs[n + k], refs[2 * n + k], refs[3 * n + k]
            delta, mn, vn = _adamw_math(w_ref[...], g_ref[...], m_ref[...], v_ref[...])
            refs[4 * n + 3 * k][...] = delta
            refs[4 * n + 3 * k + 1][...] = mn
            refs[4 * n + 3 * k + 2][...] = vn

    specs = [_const_spec(a.shape) for a in ws]
    outs = _hbm_call(
        body, name=name,
        grid=(1,),
        in_specs=specs * 4,
        out_specs=[sp for sp in specs for _ in range(3)],
        out_shape=[jax.ShapeDtypeStruct(a.shape, F32) for a in ws for _ in range(3)],
        compiler_params=pltpu.CompilerParams(dimension_semantics=("arbitrary",),
                                             vmem_limit_bytes=_vmem_limit(20 * sum(a.size for a in ws) * 4 + (8 << 20))),
    )(*ws, *ms, *vs, *gs)
    return [tuple(outs[3 * k:3 * k + 3]) for k in range(n)]


def _modw_adamw(sct, dmod, w, m, v, name):
    nl, d, n = w.shape
    tr = _row_tile(d, 128)

    def body(sct_ref, dm_ref, w_ref, m_ref, v_ref, g_out, d_out, m_out, v_out):
        sc = sct_ref[...].astype(BF16).astype(F32)
        dm = dm_ref[...].astype(BF16).astype(F32)
        g = sc[:, 0:1] * dm[0:1, :]
        for b in range(1, N_DEV):
            g = g + sc[:, b:b + 1] * dm[b:b + 1, :]
        delta, mn, vn = _adamw_math(w_ref[...], g, m_ref[...], v_ref[...])
        g_out[...] = g
        d_out[...] = delta
        m_out[...] = mn
        v_out[...] = vn

    blk = pl.BlockSpec((None, tr, n), lambda l, i: (l, i, 0))
    return _hbm_call(
        body, name=name,
        grid=(nl, d // tr),
        in_specs=[pl.BlockSpec((tr, N_DEV), lambda l, i: (i, 0)), pl.BlockSpec((None, N_DEV, n), lambda l, i: (l, 0, 0)),
                  blk, blk, blk],
        out_specs=[blk] * 4,
        out_shape=[jax.ShapeDtypeStruct((nl, d, n), F32)] * 4,
        compiler_params=pltpu.CompilerParams(dimension_semantics=("arbitrary", "arbitrary"),
                                             vmem_limit_bytes=_vmem_limit(2 * 8 * tr * n * 4 + (8 << 20))),
    )(sct, dmod, w, m, v)


def _reduce4(recvs, name):
    nl = len(recvs)
    shape = recvs[0].shape[1:]
    c = shape[-1]
    r = math.prod(shape[:-1])
    tr = _adam_rows(r, c)
    nt = r // tr

    def body(*refs):
        o_ref = refs[nl]
        for l in range(nl):
            @pl.when(pl.program_id(0) == l)
            def _():
                acc = refs[l][0].astype(F32)
                for k in range(1, N_CHIPS):
                    acc = acc + refs[l][k].astype(F32)
                o_ref[...] = acc

    def in_map(l):
        return lambda ll, i: (0, jnp.where(ll < l, 0, jnp.where(ll > l, nt - 1, i)), 0)

    return _hbm_call(
        body, name=name,
        grid=(nl, nt),
        in_specs=[pl.BlockSpec((N_CHIPS, tr, c), in_map(l)) for l in range(nl)],
        out_specs=pl.BlockSpec((None, tr, c), lambda ll, i: (ll, i, 0)),
        out_shape=jax.ShapeDtypeStruct((nl, r, c), F32),
        compiler_params=pltpu.CompilerParams(dimension_semantics=("arbitrary", "arbitrary"),
                                             vmem_limit_bytes=_vmem_limit(2 * 8 * nl * tr * max(c, 128) * 4 + (8 << 20))),
    )(*[rv.reshape(N_CHIPS, r, c) for rv in recvs]).reshape((nl,) + shape)


def _my_place():
    return lax.axis_index("x"), lax.axis_index("y"), lax.axis_index("c")


def _chip_coords(j):
    return j // 2, j % 2


def _mod_forward(c, mod_w, mod_b4):
    nl, d, n = mod_w.shape
    kc = 256

    def body(c_ref, w_ref, b_ref, mod_ref, sc_ref, cbuf, stage, s1, r1, s2, r2):
        mx, my, mc = _my_place()
        me = 4 * mx + 2 * my + mc
        q = 2 * mx + my
        cv = c_ref[...]
        cbuf[me] = jnp.broadcast_to(cv * jax.nn.sigmoid(cv), (8, d))
        sends = []
        for t in range(N_DEV):
            tx, ty = _chip_coords(t // 2)
            cp = pltpu.make_async_remote_copy(src_ref=cbuf.at[me], dst_ref=cbuf.at[me], send_sem=s1.at[t], recv_sem=r1.at[me],
                                              device_id=(tx, ty, t % 2), device_id_type=MESH)

            @pl.when(t != me)
            def _():
                cp.start()

            sends.append((t, cp))
        for t in range(N_DEV):
            @pl.when(t != me)
            def _():
                pltpu.make_async_remote_copy(src_ref=cbuf.at[t], dst_ref=cbuf.at[t], send_sem=s1.at[t], recv_sem=r1.at[t],
                                             device_id=(mx, my, mc), device_id_type=MESH).wait_recv()
        for t, cp in sends:
            @pl.when(t != me)
            def _():
                cp.wait_send()

        row = lax.broadcasted_iota(jnp.int32, (8, d), 0)
        sc_all = jnp.zeros((8, d), F32)
        for t in range(N_DEV):
            sc_all = sc_all + jnp.where(row == t, cbuf[t], 0.0)
        sc_ref[...] = sc_all
        rown = lax.broadcasted_iota(jnp.int32, (8, n), 0)
        for l in range(nl):
            acc = jnp.zeros((8, n), F32)
            for k0 in range(0, d, kc):
                acc = acc + _dot(sc_all[:, k0:k0 + kc].astype(BF16), w_ref[l, k0:k0 + kc, :].astype(BF16))
            acc = acc + b_ref[l, q]
            for j in range(N_CHIPS):
                jx, jy = _chip_coords(j)
                bdest = 4 * jx + 2 * jy + mc
                rowv = jnp.sum(jnp.where(rown == bdest, acc, 0.0), axis=0, keepdims=True)
                stage[j, l] = jnp.broadcast_to(rowv, (8, n))
        sends2 = []
        for j in range(N_CHIPS):
            jx, jy = _chip_coords(j)
            cp = pltpu.make_async_remote_copy(src_ref=stage.at[j], dst_ref=mod_ref.at[:, q], send_sem=s2.at[j], recv_sem=r2.at[q],
                                              device_id=(jx, jy, mc), device_id_type=MESH)

            @pl.when(j != q)
            def _():
                cp.start()

            @pl.when(j == q)
            def _():
                for l in range(nl):
                    mod_ref[l, j] = stage[j, l]

            sends2.append((j, cp))
        for j in range(N_CHIPS):
            @pl.when(j != q)
            def _():
                pltpu.make_async_remote_copy(src_ref=stage.at[j], dst_ref=mod_ref.at[:, j], send_sem=s2.at[j], recv_sem=r2.at[j],
                                             device_id=(mx, my, mc), device_id_type=MESH).wait_recv()
        for j, cp in sends2:
            @pl.when(j != q)
            def _():
                cp.wait_send()

    vm = pl.BlockSpec(memory_space=pltpu.VMEM)
    return pl.pallas_call(
        body, name="mod_forward",
        in_specs=[vm, vm, vm],
        out_specs=[vm, vm],
        out_shape=[jax.ShapeDtypeStruct((nl, N_CHIPS, 8, n), F32), jax.ShapeDtypeStruct((8, d), F32)],
        scratch_shapes=[pltpu.VMEM((N_DEV, 8, d), F32), pltpu.VMEM((N_CHIPS, nl, 8, n), F32),
                        pltpu.SemaphoreType.DMA((N_DEV,)), pltpu.SemaphoreType.DMA((N_DEV,)),
                        pltpu.SemaphoreType.DMA((N_CHIPS,)), pltpu.SemaphoreType.DMA((N_CHIPS,))],
        compiler_params=pltpu.CompilerParams(vmem_limit_bytes=_vmem_limit(2 * nl * d * n * 4 + (8 << 20))),
    )(c, mod_w, mod_b4)


_HBM_SPEC = pl.BlockSpec(memory_space=pltpu.HBM)
_SEM_SPEC = pl.BlockSpec(memory_space=pltpu.SEMAPHORE)
_DATAFLOW = pltpu.SideEffectType.DATAFLOW_SIDE_EFFECTING


def _slot(ref, scatter, j):
    return ref.at[j] if scatter else ref


def _exchange_start(groups, scatter, after, name):
    flat = [a for g in groups for a in g]
    na = len(flat)
    ng = len(groups)
    sizes = [len(g) for g in groups]
    first = [sum(sizes[:g]) for g in range(ng)]
    where = [(g, k) for g in range(ng) for k in range(sizes[g])]
    mx, my, _ = _my_place()
    qo = 2 * mx + my
    lands = []
    for a in flat:
        own = lax.dynamic_index_in_dim(a, qo, 0, keepdims=False) if scatter else a
        lands.append(lax.dynamic_update_index_in_dim(lax.empty((N_CHIPS,) + own.shape, a.dtype), own, qo, 0))

    def body(*refs):
        ins, lnd = refs[:na], refs[na:2 * na]
        ssems, rsems = refs[2 * na + 1:2 * na + 1 + ng], refs[2 * na + 1 + ng:2 * na + 1 + 2 * ng]
        token = refs[-1]
        mx, my, mc = _my_place()
        q = 2 * mx + my
        for j in range(N_CHIPS):
            jx, jy = _chip_coords(j)
            for a in range(na):
                g, k = where[a]

                @pl.when(j != q)
                def _():
                    pltpu.make_async_remote_copy(src_ref=_slot(ins[a], scatter, j), dst_ref=lnd[a].at[q],
                                                 send_sem=ssems[g].at[k * N_CHIPS + j], recv_sem=rsems[g].at[k * N_CHIPS + q],
                                                 device_id=(jx, jy, mc), device_id_type=MESH).start()
        token[...] = jnp.zeros_like(token)

    sem_shapes = [pltpu.SemaphoreType.DMA((n * N_CHIPS,)) for n in sizes]
    outs = pl.pallas_call(
        body, name=name,
        in_specs=[_HBM_SPEC] * (2 * na) + [pl.BlockSpec(memory_space=pl.ANY)],
        out_specs=[_SEM_SPEC] * (2 * ng) + [_HBM_SPEC] * (2 * na) + [pl.BlockSpec(memory_space=pltpu.VMEM)],
        out_shape=sem_shapes + sem_shapes + [pltpu.HBM(a.shape, a.dtype) for a in flat + lands]
        + [jax.ShapeDtypeStruct((8, 128), F32)],
        input_output_aliases={i: 2 * ng + i for i in range(2 * na)},
        compiler_params=pltpu.CompilerParams(has_side_effects=_DATAFLOW),
    )(*[pltpu.with_memory_space_constraint(a, pltpu.HBM) for a in flat + lands], after)
    ssems, rsems = outs[:ng], outs[ng:2 * ng]
    src_thru, land_thru = outs[2 * ng:2 * ng + na], outs[2 * ng + na:2 * ng + 2 * na]
    states = [(src_thru[first[g]:first[g] + sizes[g]], land_thru[first[g]:first[g] + sizes[g]], ssems[g], rsems[g])
              for g in range(ng)]
    return states, outs[-1]


def _exchange_wait(state, scatter, after, name):
    srcs, lands, ssem, rsem = state
    na = len(srcs)

    def body(*refs):
        ins, lnd = refs[:na], refs[na:2 * na]
        ssem_ref, rsem_ref = refs[2 * na], refs[2 * na + 1]
        mx, my, mc = _my_place()
        q = 2 * mx + my
        for j in range(N_CHIPS):
            for a in range(na):
                @pl.when(j != q)
                def _():
                    cp = pltpu.make_async_remote_copy(src_ref=_slot(ins[a], scatter, j), dst_ref=lnd[a].at[j],
                                                      send_sem=ssem_ref.at[a * N_CHIPS + j], recv_sem=rsem_ref.at[a * N_CHIPS + j],
                                                      device_id=(mx, my, mc), device_id_type=MESH)
                    cp.wait_send()
                    cp.wait_recv()

    outs = pl.pallas_call(
        body, name=name,
        in_specs=[_HBM_SPEC] * (2 * na) + [_SEM_SPEC, _SEM_SPEC, pl.BlockSpec(memory_space=pl.ANY)],
        out_specs=[_HBM_SPEC] * (2 * na),
        out_shape=[pltpu.HBM(a.shape, a.dtype) for a in list(srcs) + list(lands)],
        input_output_aliases={i: i for i in range(2 * na)},
        compiler_params=pltpu.CompilerParams(has_side_effects=_DATAFLOW),
    )(*srcs, *lands, ssem, rsem, after)
    return outs[na:]


def _sibling_copy(src, dst, ssem, rsem, a):
    mx, my, mc = _my_place()
    return pltpu.make_async_remote_copy(src_ref=src, dst_ref=dst, send_sem=ssem.at[a], recv_sem=rsem.at[a],
                                        device_id=(mx, my, 1 - mc), device_id_type=MESH)


def _swap_start(arrs, after, name):
    na = len(arrs)
    lands = [lax.empty(a.shape, a.dtype) for a in arrs]

    def body(*refs):
        ins, lnd = refs[:na], refs[na:2 * na]
        ssem, rsem, token = refs[2 * na + 1], refs[2 * na + 2], refs[-1]
        for a in range(na):
            _sibling_copy(ins[a], lnd[a], ssem, rsem, a).start()
        token[...] = jnp.zeros_like(token)

    outs = pl.pallas_call(
        body, name=name,
        in_specs=[_HBM_SPEC] * (2 * na) + [pl.BlockSpec(memory_space=pl.ANY)],
        out_specs=[_SEM_SPEC] * 2 + [_HBM_SPEC] * (2 * na) + [pl.BlockSpec(memory_space=pltpu.VMEM)],
        out_shape=[pltpu.SemaphoreType.DMA((na,))] * 2 + [pltpu.HBM(a.shape, a.dtype) for a in list(arrs) + lands]
        + [jax.ShapeDtypeStruct((8, 128), F32)],
        input_output_aliases={i: 2 + i for i in range(2 * na)},
        compiler_params=pltpu.CompilerParams(has_side_effects=_DATAFLOW),
    )(*[pltpu.with_memory_space_constraint(a, pltpu.HBM) for a in list(arrs) + lands], after)
    return (outs[2:2 + na], outs[2 + na:2 + 2 * na], outs[0], outs[1]), outs[-1]


def _swap_wait(state, after, name):
    srcs, lands, ssem, rsem = state
    na = len(srcs)

    def body(*refs):
        ins, lnd = refs[:na], refs[na:2 * na]
        ssem_ref, rsem_ref = refs[2 * na], refs[2 * na + 1]
        for a in range(na):
            cp = _sibling_copy(ins[a], lnd[a], ssem_ref, rsem_ref, a)
            cp.wait_send()
            cp.wait_recv()

    outs = pl.pallas_call(
        body, name=name,
        in_specs=[_HBM_SPEC] * (2 * na) + [_SEM_SPEC, _SEM_SPEC, pl.BlockSpec(memory_space=pl.ANY)],
        out_specs=[_HBM_SPEC] * (2 * na),
        out_shape=[pltpu.HBM(a.shape, a.dtype) for a in list(srcs) + list(lands)],
        input_output_aliases={i: i for i in range(2 * na)},
        compiler_params=pltpu.CompilerParams(has_side_effects=_DATAFLOW),
    )(*srcs, *lands, ssem, rsem, after)
    return outs[:na], outs[na:]


def _allreduce_small(rows_all, rows_sum):
    ra, c = rows_all.shape
    r = rows_sum.shape[0]
    ch = r // N_DEV
    assert ch % 8 == 0 and ch * N_DEV == r

    def body(a_ref, s_ref, all_ref, sum_ref, rbuf, red, sa, rva, sb, rvb, sc, rvc):
        mx, my, mc = _my_place()
        me = 4 * mx + 2 * my + mc
        mine = pl.ds(pl.multiple_of(me * ch, 8), ch)
        all_ref[me] = a_ref[...]
        rbuf[me] = s_ref[mine, :]

        def dev(t):
            tx, ty = _chip_coords(t // 2)
            return (tx, ty, t % 2)

        def everyone_else(fn):
            for t in range(N_DEV):
                @pl.when(t != me)
                def _():
                    fn(t)

        def copy_a(t, slot):
            return pltpu.make_async_remote_copy(src_ref=a_ref, dst_ref=all_ref.at[slot], send_sem=sa.at[t], recv_sem=rva.at[slot],
                                                device_id=dev(t), device_id_type=MESH)

        def copy_b(t, slot):
            return pltpu.make_async_remote_copy(src_ref=s_ref.at[pl.ds(t * ch, ch), :], dst_ref=rbuf.at[slot], send_sem=sb.at[t],
                                                recv_sem=rvb.at[slot], device_id=dev(t), device_id_type=MESH)

        def copy_c(t, chunk_start, slot):
            return pltpu.make_async_remote_copy(src_ref=red, dst_ref=sum_ref.at[pl.ds(chunk_start, ch), :], send_sem=sc.at[t],
                                                recv_sem=rvc.at[slot], device_id=dev(t), device_id_type=MESH)

        everyone_else(lambda t: (copy_a(t, me).start(), copy_b(t, me).start()))
        everyone_else(lambda t: (copy_a(t, t).wait_recv(), copy_b(t, t).wait_recv()))
        everyone_else(lambda t: (copy_a(t, me).wait_send(), copy_b(t, me).wait_send()))
        acc = rbuf[0]
        for t in range(1, N_DEV):
            acc = acc + rbuf[t]
        red[...] = acc
        sum_ref[mine, :] = acc
        everyone_else(lambda t: copy_c(t, pl.multiple_of(me * ch, 8), me).start())
        everyone_else(lambda t: copy_c(t, t * ch, t).wait_recv())
        everyone_else(lambda t: copy_c(t, pl.multiple_of(me * ch, 8), me).wait_send())

    vm = pl.BlockSpec(memory_space=pltpu.VMEM)
    return pl.pallas_call(
        body, name="allreduce_small",
        in_specs=[vm, vm],
        out_specs=[vm, vm],
        out_shape=[jax.ShapeDtypeStruct((N_DEV, ra, c), F32), jax.ShapeDtypeStruct((r, c), F32)],
        scratch_shapes=[pltpu.VMEM((N_DEV, ch, c), F32), pltpu.VMEM((ch, c), F32)] + [pltpu.SemaphoreType.DMA((N_DEV,))] * 6,
        compiler_params=pltpu.CompilerParams(vmem_limit_bytes=_vmem_limit((3 * r + 2 * N_DEV * ra) * c * 4 + (4 << 20))),
    )(rows_all, rows_sum)


def _pack(arrs, row_multiple=8):
    rows, layout, at = [], [], 0
    for a in arrs:
        n = a.size
        nr = -(-n // (8 * SMALL_COLS)) * 8
        flat = a.reshape(-1)
        if nr * SMALL_COLS != n:
            flat = jnp.pad(flat, (0, nr * SMALL_COLS - n))
        rows.append(flat.reshape(nr, SMALL_COLS))
        layout.append((at, nr, a.shape))
        at += nr
    pad = -at % row_multiple
    if pad:
        rows.append(jnp.zeros((pad, SMALL_COLS), F32))
    return jnp.concatenate(rows, axis=0), layout


def _unpack(buf, layout):
    out = []
    for at, nr, shape in layout:
        n = math.prod(shape)
        out.append(buf[at:at + nr].reshape(-1)[:n].reshape(shape))
    return out


SMALL_NAMES = ("mod_b", "mix_pre_g", "mix_post_g", "sgu_norm_g", "sgu_norm_b", "sgu_w", "sgu_b", "conv_b", "conv_norm_g",
               "conv_norm_b", "pool_w", "pool_scale", "branch_g", "ffn_pre_g", "ffn_post_g", "ffn_conv_b")
SHARDED_SMALL = ("conv_w", "ffn_conv_w")
WEIGHT_ORDER = ("mod_w", "mod_b", "mix_pre_g", "mix_post_g", "w_in", "sgu_norm_g", "sgu_norm_b", "sgu_w", "sgu_b", "conv_w",
                "conv_b", "conv_norm_g", "conv_norm_b", "pool_w", "pool_scale", "branch_g", "w_out", "ffn_pre_g", "ffn_post_g",
                "ffn_up", "ffn_conv_w", "ffn_conv_b", "ffn_down")


def _block_diag(blocks):
    n, a, b = blocks.shape
    eye = jnp.eye(n, dtype=blocks.dtype)
    return (eye[:, None, :, None] * blocks[:, :, None, :]).reshape(n * a, n * b)


def _diag_blocks(mat, n):
    a = mat.shape[0] // n
    return jnp.stack([mat[g * a:(g + 1) * a, g * a:(g + 1) * a] for g in range(n)])


def _step(x, c, loss_target, w, m, v):
    nl = w["mod_w"].shape[0]
    s, d = x.shape[1], x.shape[2]
    heads = SGU_WIDTH // HEAD_DIM
    groups = len(POOL_WINDOWS)
    mx, my, _ = _my_place()
    q = 2 * mx + my
    x0 = x.reshape(s, d)
    tgt = loss_target.reshape(s, d)

    nmod = w["mod_w"].shape[2]
    kin = w["w_in"].shape[2]
    inw = kin * N_CHIPS
    f2 = w["ffn_up"].shape[2] * N_CHIPS
    f = f2 // 2

    def wgroups(l):
        return [[jnp.swapaxes(w["w_in"][l], 0, 1).astype(BF16), w["conv_w"][l], w["ffn_conv_w"][l]], [w["w_out"][l].astype(BF16)],
                [w["ffn_up"][l].astype(BF16)], [w["ffn_down"][l].astype(BF16)]]

    gstates = {}
    (gstates[0, 0], gstates[0, 1]), gtoken = _exchange_start(wgroups(0)[:2], False, c, "gather_start_in_0")
    mod4, sc_all = _mod_forward(c + gtoken[0:1, 0:1], w["mod_w"], w["mod_b"].reshape(nl, N_CHIPS, 1, nmod))
    mod = mod4[:, :, 0, :].reshape(nl, N_MOD, 1, d)

    tril = jnp.tril(jnp.ones((CHUNK, CHUNK), bool))
    bd = _block_diag(jnp.ones((heads, HEAD_DIM, HEAD_DIM), BF16))

    def mixer_params(l, conv_w):
        wm = jnp.where(tril[None], w["sgu_w"][l], 0.0)
        pw = _block_diag(w["pool_w"][l])
        return dict(
            bd=bd, ng=w["sgu_norm_g"][l][None], nb=w["sgu_norm_b"][l][None],
            wm=wm.astype(BF16), wmt=jnp.swapaxes(wm, 1, 2).astype(BF16),
            bias=jnp.repeat(w["sgu_b"][l].T, HEAD_DIM, axis=1),
            cw=conv_w, cb=w["conv_b"][l][None], cng=w["conv_norm_g"][l][None], cnb=w["conv_norm_b"][l][None],
            pw=pw.astype(BF16), pwt=pw.T.astype(BF16), ps=w["pool_scale"][l][None], bg=w["branch_g"][l][None])

    saved = []
    xl = x0
    arrived = {0: list(_exchange_wait(gstates[0, 0], False, mod4, "gather_wait_in_0"))}
    arrived[0] += list(_exchange_wait(gstates[0, 1], False, arrived[0][0], "gather_wait_out_0"))
    for l in range(nl):
        sh1, sc1, g1, sh2, sc2, g2 = [mod[l, k] for k in range(N_MOD)]
        gpre1, gpost1 = w["mix_pre_g"][l][None], w["mix_post_g"][l][None]
        gpre2, gpost2 = w["ffn_pre_g"][l][None], w["ffn_post_g"][l][None]
        fcb = w["ffn_conv_b"][l][None]
        sh1_after, bg_after, g1_after, sh2_after, fcb_after = sh1, w["branch_g"][l][None], g1, sh2, fcb
        g_win, g_cw, g_fcw = arrived[l][:3]
        if l == 0:
            (gstates[0, 2],), tok = _exchange_start(wgroups(0)[2:3], False, arrived[0][3], "gather_start_up_0")
            sh1_after = sh1 + tok[0:1, 0:1]
        w_in = g_win.reshape(inw, d)
        conv_w = jnp.transpose(g_cw, (1, 0, 2)).reshape(CONV_K, CONV_WIDTH)
        ffn_cw = jnp.transpose(g_fcw, (1, 0, 2)).reshape(FFN_CONV_K, f2)
        mp = mixer_params(l, conv_w)
        z, h1 = _norm_mod_matmul(xl, gpre1, 1.0 + sc1, sh1_after, w_in[None], f"mix_in_{l}", transposed=True)
        w_out = arrived[l][3].reshape(d, d)
        ycat, cbo, dpool = _mixer_fwd(z, dict(mp, bg=bg_after), f"mixer_fwd_{l}")
        (up,) = _exchange_wait(gstates[l, 2], False, ycat, f"gather_wait_up_{l}")
        if l == 0:
            (gstates[0, 3],), tok = _exchange_start(wgroups(0)[3:4], False, up, "gather_start_down_0")
            g1_after = g1 + tok[0:1, 0:1]
        o, x1 = _matmul_norm_resid(ycat, w_out, xl, g1_after, gpost1, f"mix_out_{l}")
        if l + 1 < nl:
            (gstates[l + 1, 0], gstates[l + 1, 1]), tok = _exchange_start(wgroups(l + 1)[0:2], False, x1,
                                                                        f"gather_start_in_{l + 1}")
            sh2_after = sh2 + tok[0:1, 0:1]
        p, h2 = _norm_mod_matmul(x1, gpre2, 1.0 + sc2, sh2_after, up, f"ffn_in_{l}")
        if l + 1 < nl:
            nxt = _exchange_wait(gstates[l + 1, 0], False, p, f"gather_wait_in_{l + 1}")
            nxt_out = _exchange_wait(gstates[l + 1, 1], False, nxt[0], f"gather_wait_out_{l + 1}")
            arrived[l + 1] = list(nxt) + list(nxt_out)
            (gstates[l + 1, 2], gstates[l + 1, 3]), tok = _exchange_start(wgroups(l + 1)[2:4], False, nxt_out[0],
                                                                        f"gather_start_up_{l + 1}")
            fcb_after = fcb + tok[0:1, 0:1]
        act = _ffn_act_fwd(p, ffn_cw, fcb_after, f"ffn_act_{l}")
        (g_down,) = _exchange_wait(gstates[l, 3], False, act, f"gather_wait_down_{l}")
        down = g_down.reshape(f, d)
        qo, x2 = _matmul_norm_resid(act, down, x1, g2, gpost2, f"ffn_out_{l}")
        saved.append(dict(x=xl, z=z, h1=h1, ycat=ycat, cbo=cbo, dpool=dpool, o=o, x1=x1, p=p, h2=h2, act=act, qo=qo, mp=mp, fcb=fcb,
                          w_in=w_in, w_out=w_out, up=up, down=down, ffn_cw=ffn_cw,
                          mods=(sh1, sc1, g1, sh2, sc2, g2), gains=(gpre1, gpost1, gpre2, gpost2)))
        xl = x2

    dx, loss_row = _loss_head(xl, tgt)

    small = {n: [None] * nl for n in SMALL_NAMES + SHARDED_SMALL}
    dmods = [None] * nl
    tn = f2 // N_CHIPS
    sstates = {}
    token = None
    for l in reversed(range(nl)):
        sv = saved[l]
        sh1, sc1, g1, sh2, sc2, g2 = sv["mods"]
        gpre1, gpost1, gpre2, gpost2 = sv["gains"]
        if token is not None:
            g2 = g2 + token[0:1, 0:1]
        dq, dact, dg2, dgpost2 = _resid_bwd_matmul(dx, sv["qo"], g2, gpost2, sv["down"], f"ffn_out_bwd_{l}")
        g_down = _wgrad(sv["act"], dq, (f, lambda j: 0), (d, lambda j: 0), jax.ShapeDtypeStruct((f, d), BF16),
                        (1, lambda j: (0, 0)), (f, d), f"wgrad_ffn_down_{l}")
        dug, duv, dfwg, dfwv, dfbg, dfbv = _ffn_act_bwd(sv["p"], dact, sv["ffn_cw"], sv["fcb"], f"ffn_act_bwd_{l}")
        dx1, dp, dsh2, dsc2, dgpre2 = _ffn_in_bwd(dug, duv, sv["ffn_cw"], sv["up"], sv["x1"], dx, gpre2, 1.0 + sc2,
                                                  f"ffn_in_bwd_{l}")
        g_up = _wgrad(sv["h2"], dp, (d, lambda j: 0), (tn, lambda j: j), jax.ShapeDtypeStruct((N_CHIPS, d, tn), BF16),
                      (N_CHIPS, lambda j: (j, 0, 0)), (None, d, tn), f"wgrad_ffn_up_{l}")
        (sstates[l, 0],), token = _exchange_start([[g_down.reshape(N_CHIPS, f // N_CHIPS, d), g_up]], True, g_up,
                                                  f"scatter_start_ffn_{l}")
        do, dycat, dg1, dgpost1 = _resid_bwd_matmul(dx1, sv["o"], g1 + token[0:1, 0:1], gpost1, sv["w_out"],
                                                    f"mix_out_bwd_{l}")
        g_out = _wgrad(sv["ycat"], do, (d, lambda j: 0), (d, lambda j: 0), jax.ShapeDtypeStruct((d, d), BF16),
                       (1, lambda j: (0, 0)), (d, d), f"wgrad_w_out_{l}")
        (sstates[l, 1],), token = _exchange_start([[g_out.reshape(N_CHIPS, d // N_CHIPS, d)]], True, g_out,
                                                  f"scatter_start_out_{l}")
        mp_after = dict(sv["mp"], bg=sv["mp"]["bg"] + token[0:1, 0:1])
        (dza, dcb, dd, dbg, dwm, dbias, dng, dnb, dcng, dcnb, dps, dpw) = _mixer_bwd_a(sv["z"], sv["cbo"], sv["dpool"], dycat, mp_after, f"mixer_bwd_a_{l}")
        dx, dz, dsh1, dsc1, dgpre1, dcw, dcbias = _mixer_bwd_b(
            sv["z"], dza, dcb, dd, sv["x"], dx1, gpre1, 1.0 + sc1, sv["mp"]["cw"], sv["w_in"], f"mixer_bwd_b_{l}")
        g_in = _wgrad(dz, sv["h1"], (inw, lambda j: 0), (d, lambda j: 0), jax.ShapeDtypeStruct((inw, d), BF16),
                      (1, lambda j: (0, 0)), (inw, d), f"wgrad_w_in_{l}")
        g_in_parts = g_in.reshape(N_CHIPS, kin, d)
        if l > 0:
            (sstates[l, 2],), token = _exchange_start([[g_in_parts]], True, g_in_parts, f"scatter_start_in_{l}")

        dmods[l] = jnp.concatenate([dsh1, dsc1, dg1, dsh2, dsc2, dg2], axis=0)
        small["mix_pre_g"][l], small["mix_post_g"][l] = dgpre1[0], dgpost1[0]
        small["ffn_pre_g"][l], small["ffn_post_g"][l] = dgpre2[0], dgpost2[0]
        small["sgu_norm_g"][l], small["sgu_norm_b"][l] = dng[0], dnb[0]
        small["sgu_w"][l] = jnp.where(tril[None], dwm, 0.0)
        small["sgu_b"][l] = dbias.reshape(CHUNK, heads, HEAD_DIM).sum(-1).T
        small["conv_b"][l], small["conv_norm_g"][l], small["conv_norm_b"][l] = dcbias[0], dcng[0], dcnb[0]
        small["pool_w"][l], small["pool_scale"][l], small["branch_g"][l] = _diag_blocks(dpw, groups), dps[0], dbg[0]
        small["ffn_conv_b"][l] = jnp.concatenate([dfbg[0], dfbv[0]])
        small["conv_w"][l] = dcw
        small["ffn_conv_w"][l] = jnp.concatenate([dfwg, dfwv], axis=1)

    names = [n for n in SMALL_NAMES if n != "mod_b"] + list(SHARDED_SMALL)
    dmod_rows, _ = _pack([jnp.stack(dmods)])
    packed, layout = _pack([jnp.stack(dmods), loss_row] + [jnp.stack(small[n]) for n in names], 8 * N_DEV)
    gathered, summed = _allreduce_small(dmod_rows, packed)
    (sstates[0, 2],), token = _exchange_start([[g_in_parts]], True, summed, "scatter_start_in_0")
    parts = _unpack(summed, layout)
    loss = parts[1][0, 0]
    gsmall = dict(zip(names, parts[2:]))
    gsmall["mod_b"] = parts[0].reshape(nl, N_MOD * d)
    dmod_all = gathered[:, :nl * N_MOD].reshape(N_DEV, nl, N_MOD * d)
    dmod_mine = jnp.transpose(lax.dynamic_slice_in_dim(dmod_all, q * nmod, nmod, axis=2), (1, 0, 2))

    grads, deltas, new_m, new_v = {}, {}, {}, {}

    def put(name, res):
        grads[name], deltas[name], new_m[name], new_v[name] = res

    recv = dict(w_in=[None] * nl, w_out=[None] * nl, ffn_up=[None] * nl, ffn_down=[None] * nl)
    done = token
    for l in reversed(range(nl)):
        recv["ffn_down"][l], recv["ffn_up"][l] = _exchange_wait(sstates[l, 0], True, done, f"scatter_wait_ffn_{l}")
        (recv["w_out"][l],) = _exchange_wait(sstates[l, 1], True, recv["ffn_up"][l], f"scatter_wait_out_{l}")
        done = recv["w_out"][l]
    big = ("w_out", "ffn_up", "ffn_down", "w_in")
    mine = {n: _reduce4(recv[n], f"reduce4_{n}") for n in big[:3]}
    swap_a, token = _swap_start([mine[n] for n in big[:3]], mine["ffn_down"], "swap_start_a")

    put("mod_w", _modw_adamw(sc_all.T + token[0:1, 0:1], dmod_mine, w["mod_w"], m["mod_w"], v["mod_w"], "adamw_mod_w"))
    done = grads["mod_w"]
    for l in reversed(range(nl)):
        (recv["w_in"][l],) = _exchange_wait(sstates[l, 2], True, done, f"scatter_wait_in_{l}")
        done = recv["w_in"][l]
    mine["w_in"] = _reduce4(recv["w_in"], "reduce4_w_in")
    swap_b, token = _swap_start([mine["w_in"]], mine["w_in"], "swap_start_b")

    gsmall["conv_w"] = lax.dynamic_slice_in_dim(gsmall["conv_w"], q * (CONV_WIDTH // N_CHIPS), CONV_WIDTH // N_CHIPS, axis=2)
    gsmall["ffn_conv_w"] = lax.dynamic_slice_in_dim(gsmall["ffn_conv_w"], q * (f2 // N_CHIPS), f2 // N_CHIPS, axis=2)
    snames = SMALL_NAMES + SHARDED_SMALL
    res = _adamw_many([w[n] for n in snames], [m[n] for n in snames], [v[n] for n in snames], [gsmall[n] for n in snames],
                      "adamw_small")
    for n, (d_, m_, v_) in zip(snames, res):
        put(n, (gsmall[n], d_, m_, v_))

    sent, theirs = _swap_wait(swap_a, deltas["mod_b"], "swap_wait_a")
    sent_b, theirs_b = _swap_wait(swap_b, sent[0], "swap_wait_b")
    for n, a, b in zip(big, list(sent) + list(sent_b), list(theirs) + list(theirs_b)):
        if n == "w_in":
            res = _adamw(*[jnp.swapaxes(t[n], 1, 2) for t in (w, m, v)], [a, b], f"adamw_{n}")
            put(n, [jnp.swapaxes(r_, 1, 2) for r_ in res])
        else:
            put(n, _adamw(w[n], m[n], v[n], [a, b], f"adamw_{n}"))

    return (loss, dx.reshape(1, s, d), *[grads[n] for n in WEIGHT_ORDER], *[deltas[n] for n in WEIGHT_ORDER],
            *[new_m[n] for n in WEIGHT_ORDER], *[new_v[n] for n in WEIGHT_ORDER])


def kernel(x, c, mod_w, mod_b, mix_pre_g, mix_post_g, w_in, sgu_norm_g, sgu_norm_b, sgu_w, sgu_b, conv_w, conv_b, conv_norm_g, conv_norm_b, pool_w, pool_scale, branch_g, w_out, ffn_pre_g, ffn_post_g, ffn_up, ffn_conv_w, ffn_conv_b, ffn_down, loss_target, m_mod_w, m_mod_b, m_mix_pre_g, m_mix_post_g, m_w_in, m_sgu_norm_g, m_sgu_norm_b, m_sgu_w, m_sgu_b, m_conv_w, m_conv_b, m_conv_norm_g, m_conv_norm_b, m_pool_w, m_pool_scale, m_branch_g, m_w_out, m_ffn_pre_g, m_ffn_post_g, m_ffn_up, m_ffn_conv_w, m_ffn_conv_b, m_ffn_down, v_mod_w, v_mod_b, v_mix_pre_g, v_mix_post_g, v_w_in, v_sgu_norm_g, v_sgu_norm_b, v_sgu_w, v_sgu_b, v_conv_w, v_conv_b, v_conv_norm_g, v_conv_norm_b, v_pool_w, v_pool_scale, v_branch_g, v_w_out, v_ffn_pre_g, v_ffn_post_g, v_ffn_up, v_ffn_conv_w, v_ffn_conv_b, v_ffn_down):
    w = dict(mod_w=mod_w, mod_b=mod_b, mix_pre_g=mix_pre_g, mix_post_g=mix_post_g, w_in=w_in, sgu_norm_g=sgu_norm_g,
             sgu_norm_b=sgu_norm_b, sgu_w=sgu_w, sgu_b=sgu_b, conv_w=conv_w, conv_b=conv_b, conv_norm_g=conv_norm_g,
             conv_norm_b=conv_norm_b, pool_w=pool_w, pool_scale=pool_scale, branch_g=branch_g, w_out=w_out,
             ffn_pre_g=ffn_pre_g, ffn_post_g=ffn_post_g, ffn_up=ffn_up, ffn_conv_w=ffn_conv_w, ffn_conv_b=ffn_conv_b,
             ffn_down=ffn_down)
    m = dict(mod_w=m_mod_w, mod_b=m_mod_b, mix_pre_g=m_mix_pre_g, mix_post_g=m_mix_post_g, w_in=m_w_in,
             sgu_norm_g=m_sgu_norm_g, sgu_norm_b=m_sgu_norm_b, sgu_w=m_sgu_w, sgu_b=m_sgu_b, conv_w=m_conv_w,
             conv_b=m_conv_b, conv_norm_g=m_conv_norm_g, conv_norm_b=m_conv_norm_b, pool_w=m_pool_w,
             pool_scale=m_pool_scale, branch_g=m_branch_g, w_out=m_w_out, ffn_pre_g=m_ffn_pre_g, ffn_post_g=m_ffn_post_g,
             ffn_up=m_ffn_up, ffn_conv_w=m_ffn_conv_w, ffn_conv_b=m_ffn_conv_b, ffn_down=m_ffn_down)
    v = dict(mod_w=v_mod_w, mod_b=v_mod_b, mix_pre_g=v_mix_pre_g, mix_post_g=v_mix_post_g, w_in=v_w_in,
             sgu_norm_g=v_sgu_norm_g, sgu_norm_b=v_sgu_norm_b, sgu_w=v_sgu_w, sgu_b=v_sgu_b, conv_w=v_conv_w,
             conv_b=v_conv_b, conv_norm_g=v_conv_norm_g, conv_norm_b=v_conv_norm_b, pool_w=v_pool_w,
             pool_scale=v_pool_scale, branch_g=v_branch_g, w_out=v_w_out, ffn_pre_g=v_ffn_pre_g, ffn_post_g=v_ffn_post_g,
             ffn_up=v_ffn_up, ffn_conv_w=v_ffn_conv_w, ffn_conv_b=v_ffn_conv_b, ffn_down=v_ffn_down)
    return _step(x, c, loss_target, w, m, v)
```

```python
import functools
import math

import jax
import jax.numpy as jnp
from jax import lax
from jax.experimental import pallas as pl
from jax.experimental.pallas import tpu as pltpu

F32 = jnp.float32
BF16 = jnp.bfloat16
MESH = pl.DeviceIdType.MESH

EPS = 1e-6
HEAD_DIM = 64
CHUNK = 128
SGU_WIDTH = 384
CONV_WIDTH = 384
POOL_WIDTH = 256
POOL_WINDOWS = (2, 4, 8, 16)
CONV_K = 31
FFN_CONV_K = 3
N_MOD = 6
N_CHIPS = 4
N_DEV = 8

ADAM_LR = 0.001
ADAM_B1 = 0.9
ADAM_B2 = 0.999
ADAM_EPS = 1e-08
ADAM_WD = 0.01
ADAM_STEP = 10

MIX_HALO = 32
FFN_HALO = 8
FFN_ROWS = 16
FFN_UNROLL = 8
LANES = 128
CONV_ROWS = 32
SMALL_COLS = 1024
VMEM_BYTES_V7X = 64 * 1024 * 1024


def _vmem_limit(estimate_bytes):
    return int(min(max(estimate_bytes, 16 * 1024 * 1024), VMEM_BYTES_V7X - 8 * 1024 * 1024))


def _row_tile(s, want):
    return want if s % want == 0 else math.gcd(s, want)


def _rsum(v):
    return jnp.sum(v, axis=0, keepdims=True)


def _rmean(v):
    return jnp.mean(v, axis=-1, keepdims=True)


def _gelu(v):
    k = math.sqrt(2.0 / math.pi)
    return 0.5 * v * (1.0 + jnp.tanh(k * (v + 0.044715 * v * v * v)))


def _gelu_grad(v):
    k = math.sqrt(2.0 / math.pi)
    t = jnp.tanh(k * (v + 0.044715 * v * v * v))
    return 0.5 * (1.0 + t) + 0.5 * v * (1.0 - t * t) * (k * (1.0 + 3.0 * 0.044715 * v * v))


def _dot(a, b):
    return jnp.dot(a, b, preferred_element_type=F32)


def _dot_nt(a, b):
    return lax.dot_general(a, b, (((1,), (1,)), ((), ())), preferred_element_type=F32)


def _dot_tn(a, b):
    return lax.dot_general(a, b, (((0,), (0,)), ((), ())), preferred_element_type=F32)


def _group_mean(v, bd):
    hi = v.astype(BF16)
    lo = (v - hi.astype(F32)).astype(BF16)
    return (_dot(hi, bd) + _dot(lo, bd)) * (1.0 / HEAD_DIM)


def _const_spec(shape):
    nd = len(shape)
    return pl.BlockSpec(shape, lambda *_: (0,) * nd)


def _hbm_call(body, **kw):
    call = pl.pallas_call(body, **kw)
    return lambda *args: call(*[pltpu.with_memory_space_constraint(a, pltpu.HBM) for a in args])


def _norm_mod_matmul(x, gain, s1p, shift, w, name, transposed=False):
    s, d = x.shape
    nb = w.shape[0]
    tn = w.shape[1] if transposed else w.shape[2]
    ts = _row_tile(s, 512)

    def body(x_ref, g_ref, s_ref, b_ref, w_ref, z_ref, h_ref):
        xv = x_ref[...]
        r = lax.rsqrt(_rmean(xv * xv) + EPS)
        h = ((xv * r) * g_ref[...] * s_ref[...] + b_ref[...]).astype(BF16)
        h_ref[...] = h
        for j in range(nb):
            z_ref[:, j * tn:(j + 1) * tn] = _dot_nt(h, w_ref[j]) if transposed else _dot(h, w_ref[j])

    vec = pl.BlockSpec((1, d), lambda i: (0, 0))
    return _hbm_call(
        body, name=name,
        grid=(s // ts,),
        in_specs=[pl.BlockSpec((ts, d), lambda i: (i, 0)), vec, vec, vec,
                  pl.BlockSpec(w.shape, lambda i: (0, 0, 0), pipeline_mode=pl.Buffered(1))],
        out_specs=[pl.BlockSpec((ts, nb * tn), lambda i: (i, 0)), pl.BlockSpec((ts, d), lambda i: (i, 0))],
        out_shape=[jax.ShapeDtypeStruct((s, nb * tn), F32), jax.ShapeDtypeStruct((s, d), BF16)],
        compiler_params=pltpu.CompilerParams(
            dimension_semantics=("arbitrary",),
            vmem_limit_bytes=_vmem_limit(2 * (ts * d * 4 + ts * nb * tn * 4 + ts * d * 2) + nb * d * tn * 2 + 4 * ts * d * 4)),
    )(x, gain, s1p, shift, w)


def _matmul_norm_resid(a, w, xres, gate, gpost, name):
    s, k = a.shape
    d = w.shape[1]
    ts = _row_tile(s, 1024)

    def body(a_ref, w_ref, x_ref, gate_ref, gp_ref, o_ref, xn_ref):
        o = _dot(a_ref[...], w_ref[...])
        o_ref[...] = o
        r = lax.rsqrt(_rmean(o * o) + EPS)
        xn_ref[...] = x_ref[...] + gate_ref[...] * ((o * r) * gp_ref[...])

    vec = pl.BlockSpec((1, d), lambda i: (0, 0))
    row = pl.BlockSpec((ts, d), lambda i: (i, 0))
    return _hbm_call(
        body, name=name,
        grid=(s // ts,),
        in_specs=[pl.BlockSpec((ts, k), lambda i: (i, 0)),
                  pl.BlockSpec((k, d), lambda i: (0, 0), pipeline_mode=pl.Buffered(1)), row, vec, vec],
        out_specs=[row, row],
        out_shape=[jax.ShapeDtypeStruct((s, d), F32)] * 2,
        compiler_params=pltpu.CompilerParams(
            dimension_semantics=("arbitrary",),
            vmem_limit_bytes=_vmem_limit(2 * (ts * k * 2 + 3 * ts * d * 4) + k * d * 2 + 4 * ts * d * 4)),
    )(a, w, xres, gate, gpost)


def _matmul_norm_resid_loss(a, w, xres, gate, gpost, tgt, name):
    s, k = a.shape
    d = w.shape[1]
    ts = _row_tile(s, 1024)

    def body(a_ref, w_ref, x_ref, gate_ref, gp_ref, t_ref, o_ref, dx_ref, l_ref, acc):
        i = pl.program_id(0)

        @pl.when(i == 0)
        def _():
            acc[...] = jnp.zeros_like(acc)

        o = _dot(a_ref[...], w_ref[...])
        o_ref[...] = o
        r = lax.rsqrt(_rmean(o * o) + EPS)
        e = (x_ref[...] + gate_ref[...] * ((o * r) * gp_ref[...])) - t_ref[...]
        dx_ref[...] = e * (1.0 / d)
        acc[...] += _rsum(e * e)

        @pl.when(i == pl.num_programs(0) - 1)
        def _():
            tot = jnp.sum(acc[...], axis=-1, keepdims=True) * (0.5 / d)
            l_ref[...] = jnp.broadcast_to(tot, l_ref.shape)

    vec = pl.BlockSpec((1, d), lambda i: (0, 0))
    row = pl.BlockSpec((ts, d), lambda i: (i, 0))
    return _hbm_call(
        body, name=name,
        grid=(s // ts,),
        in_specs=[pl.BlockSpec((ts, k), lambda i: (i, 0)),
                  pl.BlockSpec((k, d), lambda i: (0, 0), pipeline_mode=pl.Buffered(1)), row, vec, vec, row],
        out_specs=[row, row, pl.BlockSpec((1, SMALL_COLS), lambda i: (0, 0))],
        out_shape=[jax.ShapeDtypeStruct((s, d), F32)] * 2 + [jax.ShapeDtypeStruct((1, SMALL_COLS), F32)],
        scratch_shapes=[pltpu.VMEM((1, d), F32)],
        compiler_params=pltpu.CompilerParams(
            dimension_semantics=("arbitrary",),
            vmem_limit_bytes=_vmem_limit(2 * (ts * k * 2 + 4 * ts * d * 4) + k * d * 2 + 4 * ts * d * 4)),
    )(a, w, xres, gate, gpost, tgt)


def _wgrad(a, b, acols, bcols, out_struct, out_index, out_block, name):
    s = a.shape[0]
    aw, afn = acols
    bw, bfn = bcols
    ts = _row_tile(s, 2048 if aw * bw <= 2 * 1024 * 1024 else 1024)
    nj = out_index[0]
    oidx = out_index[1]

    def body(a_ref, b_ref, o_ref, acc):
        i = pl.program_id(1)

        @pl.when(i == 0)
        def _():
            acc[...] = jnp.zeros_like(acc)

        acc[...] += _dot_tn(a_ref[...], b_ref[...])

        @pl.when(i == pl.num_programs(1) - 1)
        def _():
            o_ref[...] = acc[...].astype(o_ref.dtype)

    return _hbm_call(
        body, name=name,
        grid=(nj, s // ts),
        in_specs=[pl.BlockSpec((ts, aw), lambda j, i: (i, afn(j))), pl.BlockSpec((ts, bw), lambda j, i: (i, bfn(j)))],
        out_specs=pl.BlockSpec(out_block, lambda j, i: oidx(j)),
        out_shape=out_struct,
        scratch_shapes=[pltpu.VMEM((aw, bw), F32)],
        compiler_params=pltpu.CompilerParams(
            dimension_semantics=("arbitrary", "arbitrary"),
            vmem_limit_bytes=_vmem_limit(2 * (ts * aw * 2 + ts * bw * 2) + 3 * aw * bw * 4 + ts * aw * 4)),
    )(a, b)


def _loss_head(xo, tgt):
    s, d = xo.shape
    ts = _row_tile(s, 512)

    def body(x_ref, t_ref, dx_ref, l_ref, acc):
        i = pl.program_id(0)

        @pl.when(i == 0)
        def _():
            acc[...] = jnp.zeros_like(acc)

        e = x_ref[...] - t_ref[...]
        dx_ref[...] = e * (1.0 / d)
        acc[...] += _rsum(e * e)

        @pl.when(i == pl.num_programs(0) - 1)
        def _():
            tot = jnp.sum(acc[...], axis=-1, keepdims=True) * (0.5 / d)
            l_ref[...] = jnp.broadcast_to(tot, l_ref.shape)

    row = pl.BlockSpec((ts, d), lambda i: (i, 0))
    return _hbm_call(
        body, name="loss_head",
        grid=(s // ts,),
        in_specs=[row, row],
        out_specs=[row, pl.BlockSpec((1, SMALL_COLS), lambda i: (0, 0))],
        out_shape=[jax.ShapeDtypeStruct((s, d), F32), jax.ShapeDtypeStruct((1, SMALL_COLS), F32)],
        scratch_shapes=[pltpu.VMEM((1, d), F32)],
        compiler_params=pltpu.CompilerParams(dimension_semantics=("arbitrary",)),
    )(xo, tgt)


def _resid_bwd_matmul(dxn, o, gate, gpost, w, name):
    s, d = dxn.shape
    k = w.shape[0]
    ts = _row_tile(s, 512)

    def body(dx_ref, o_ref, gate_ref, gp_ref, w_ref, do_ref, da_ref, dgate_ref, dgp_ref):
        i = pl.program_id(0)

        @pl.when(i == 0)
        def _():
            dgate_ref[...] = jnp.zeros_like(dgate_ref)
            dgp_ref[...] = jnp.zeros_like(dgp_ref)

        dx = dx_ref[...]
        o = o_ref[...]
        r = lax.rsqrt(_rmean(o * o) + EPS)
        on = o * r
        dgate_ref[...] += _rsum(dx * (on * gp_ref[...]))
        don = dx * gate_ref[...]
        dgp_ref[...] += _rsum(don * on)
        t = don * gp_ref[...]
        do = (r * (t - on * _rmean(t * on))).astype(BF16)
        do_ref[...] = do
        da_ref[...] = _dot_nt(do, w_ref[...])

    vec = pl.BlockSpec((1, d), lambda i: (0, 0))
    row = pl.BlockSpec((ts, d), lambda i: (i, 0))
    return _hbm_call(
        body, name=name,
        grid=(s // ts,),
        in_specs=[row, row, vec, vec, pl.BlockSpec((k, d), lambda i: (0, 0), pipeline_mode=pl.Buffered(1))],
        out_specs=[row, pl.BlockSpec((ts, k), lambda i: (i, 0)), vec, vec],
        out_shape=[jax.ShapeDtypeStruct((s, d), BF16), jax.ShapeDtypeStruct((s, k), F32),
                   jax.ShapeDtypeStruct((1, d), F32), jax.ShapeDtypeStruct((1, d), F32)],
        compiler_params=pltpu.CompilerParams(
            dimension_semantics=("arbitrary",),
            vmem_limit_bytes=_vmem_limit(2 * (2 * ts * d * 4 + ts * d * 2 + ts * k * 4) + d * k * 2 + 6 * ts * d * 4)),
    )(dxn, o, gate, gpost, w)


def _norm_mod_bwd(dh, xv, gain, s1p, dres):
    r = lax.rsqrt(_rmean(xv * xv) + EPS)
    xn = xv * r
    dshift = _rsum(dh)
    t = dh * xn
    dscale = _rsum(t * gain)
    dgain = _rsum(t * s1p)
    dxn = dh * (gain * s1p)
    dx = r * (dxn - xn * _rmean(dxn * xn)) + dres
    return dx, dshift, dscale, dgain


def _lane_lt(shape, bound):
    return lax.broadcasted_iota(jnp.int32, shape, 1) < bound


def _sgu_forward(z_ref, bd_ref, ng_ref, nb_ref, wm_ref, bias_ref, ts, ya_s, f_s):
    u = _gelu(z_ref[:, 0:SGU_WIDTH])
    v = _gelu(z_ref[:, SGU_WIDTH:2 * SGU_WIDTH])
    bd = bd_ref[...]
    vc = v - _group_mean(v, bd)
    rstd = lax.rsqrt(_group_mean(vc * vc, bd) + EPS)
    vhat = vc * rstd
    vn = (vhat * ng_ref[...] + nb_ref[...]).astype(BF16)
    left = _lane_lt((CHUNK, CHUNK), HEAD_DIM)
    for n in range(ts // CHUNK):
        rows = slice(n * CHUNK, (n + 1) * CHUNK)
        for p in range(SGU_WIDTH // CHUNK):
            cols = slice(p * CHUNK, (p + 1) * CHUNK)
            blk = vn[rows, cols]
            f = jnp.where(left, _dot(wm_ref[2 * p], blk), _dot(wm_ref[2 * p + 1], blk)) + bias_ref[:, cols]
            if f_s is not None:
                f_s[rows, cols] = f
            ya_s[rows, cols] = u[rows, cols] * f
    return u, vhat, rstd, vn


def _shifted_copies(ext, ext8, ts):
    for b in range(1, 8):
        ext8[b - 1] = ext[pl.ds(b, ts + MIX_HALO - 8), :]


def _rows_at(ext, ext8, start, nrows):
    b = start % 8
    return ext[pl.ds(start, nrows), :] if b == 0 else ext8[b - 1, pl.ds(start - b, nrows), :]


def _conv31_forward(z_ref, zh_ref, first, cw_ref, cb_ref, ts, ext_b, ext8, cbs):
    a = z_ref[:, 2 * SGU_WIDTH:2 * SGU_WIDTH + CONV_WIDTH]
    g = z_ref[:, 2 * SGU_WIDTH + CONV_WIDTH:2 * SGU_WIDTH + 2 * CONV_WIDTH]
    ah = zh_ref[:, 2 * SGU_WIDTH:2 * SGU_WIDTH + CONV_WIDTH]
    gh = zh_ref[:, 2 * SGU_WIDTH + CONV_WIDTH:2 * SGU_WIDTH + 2 * CONV_WIDTH]
    ext_b[pl.ds(0, MIX_HALO), :] = jnp.where(first, 0.0, ah * jax.nn.sigmoid(gh))
    ext_b[pl.ds(MIX_HALO, ts), :] = a * jax.nn.sigmoid(g)
    _shifted_copies(ext_b, ext8, ts)
    for r in range(ts // CONV_ROWS):
        acc = jnp.broadcast_to(cb_ref[...], (CONV_ROWS, CONV_WIDTH))
        for k in range(CONV_K):
            acc = acc + cw_ref[k:k + 1, :] * _rows_at(ext_b, ext8, MIX_HALO - (CONV_K - 1) + k + r * CONV_ROWS, CONV_ROWS)
        cbs[pl.ds(r * CONV_ROWS, CONV_ROWS), :] = acc


def _pool_counts(i, ts):
    pos1 = (i * ts + 1 + lax.broadcasted_iota(jnp.int32, (ts, POOL_WIDTH), 0)).astype(F32)
    lane = lax.broadcasted_iota(jnp.int32, (ts, POOL_WIDTH), 1)
    gdim = POOL_WIDTH // len(POOL_WINDOWS)
    win = jnp.where(lane < gdim, float(POOL_WINDOWS[0]),
                    jnp.where(lane < 2 * gdim, float(POOL_WINDOWS[1]),
                              jnp.where(lane < 3 * gdim, float(POOL_WINDOWS[2]), float(POOL_WINDOWS[3]))))
    return jnp.minimum(pos1, win)


def _window_sums(ext, base, ts, sign):
    lane = lax.broadcasted_iota(jnp.int32, (ts, POOL_WIDTH), 1)
    gdim = POOL_WIDTH // len(POOL_WINDOWS)
    run = jnp.zeros((ts, POOL_WIDTH), F32)
    out = jnp.zeros((ts, POOL_WIDTH), F32)
    for m in range(POOL_WINDOWS[-1]):
        run = run + ext[pl.ds(base + sign * m, ts), :]
        for gi, win in enumerate(POOL_WINDOWS):
            if m == win - 1:
                out = jnp.where((lane >= gi * gdim) & (lane < (gi + 1) * gdim), run, out)
    return out


def _pool_forward(z_ref, zh_ref, first, i, ts, ext_c):
    c0 = 2 * SGU_WIDTH + 2 * CONV_WIDTH
    zc = z_ref[:, c0:c0 + POOL_WIDTH]
    ext_c[pl.ds(0, MIX_HALO), :] = jnp.where(first, 0.0, zh_ref[:, c0:c0 + POOL_WIDTH])
    ext_c[pl.ds(MIX_HALO, ts), :] = zc
    sums = _window_sums(ext_c, MIX_HALO, ts, -1)
    return sums / _pool_counts(i, ts) - zc


def _layer_norm_rows(v):
    mu = _rmean(v)
    vc = v - mu
    rstd = lax.rsqrt(_rmean(vc * vc) + EPS)
    return vc * rstd, rstd


def _mixer_specs(s, ts, width):
    nbh = ts // MIX_HALO
    tile = pl.BlockSpec((ts, width), lambda i: (i, 0))
    prev = pl.BlockSpec((MIX_HALO, width), lambda i: (jnp.maximum(i * nbh - 1, 0), 0))
    nxt = pl.BlockSpec((MIX_HALO, width), lambda i: (jnp.minimum((i + 1) * nbh, s // MIX_HALO - 1), 0))
    return tile, prev, nxt


def _mixer_fwd(z, mp, name):
    s, inw = z.shape
    d = SGU_WIDTH + CONV_WIDTH + POOL_WIDTH
    ts = _row_tile(s, 256)

    def body(z_ref, zh_ref, bd_ref, ng_ref, nb_ref, wm_ref, bias_ref, cw_ref, cb_ref, cng_ref, cnb_ref,
             pw_ref, ps_ref, bg_ref, y_ref, cbs, dpool_ref, ya_s, ext_b, ext_c, ext8):
        i = pl.program_id(0)
        first = i == 0
        _sgu_forward(z_ref, bd_ref, ng_ref, nb_ref, wm_ref, bias_ref, ts, ya_s, None)
        ya = ya_s[...]
        ra = lax.rsqrt(_rmean(ya * ya) + EPS)
        y_ref[:, 0:SGU_WIDTH] = ((ya * ra) * bg_ref[:, 0:SGU_WIDTH]).astype(BF16)

        _conv31_forward(z_ref, zh_ref, first, cw_ref, cb_ref, ts, ext_b, ext8, cbs)
        chat, _ = _layer_norm_rows(cbs[...])
        lin = chat * cng_ref[...] + cnb_ref[...]
        yb = lin * jax.nn.sigmoid(lin)
        rb = lax.rsqrt(_rmean(yb * yb) + EPS)
        y_ref[:, SGU_WIDTH:SGU_WIDTH + CONV_WIDTH] = ((yb * rb) * bg_ref[:, SGU_WIDTH:SGU_WIDTH + CONV_WIDTH]).astype(BF16)

        dpool = _pool_forward(z_ref, zh_ref, first, i, ts, ext_c)
        dpool_ref[...] = dpool
        yc = _dot(dpool.astype(BF16), pw_ref[...]) * ps_ref[...]
        rc = lax.rsqrt(_rmean(yc * yc) + EPS)
        y_ref[:, SGU_WIDTH + CONV_WIDTH:d] = ((yc * rc) * bg_ref[:, SGU_WIDTH + CONV_WIDTH:d]).astype(BF16)

    tile, prev, _ = _mixer_specs(s, ts, inw)
    consts = [mp["bd"], mp["ng"], mp["nb"], mp["wm"], mp["bias"], mp["cw"], mp["cb"], mp["cng"], mp["cnb"],
              mp["pw"], mp["ps"], mp["bg"]]
    return _hbm_call(
        body, name=name,
        grid=(s // ts,),
        in_specs=[tile, prev] + [_const_spec(c.shape) for c in consts],
        out_specs=[pl.BlockSpec((ts, d), lambda i: (i, 0)), pl.BlockSpec((ts, CONV_WIDTH), lambda i: (i, 0)),
                   pl.BlockSpec((ts, POOL_WIDTH), lambda i: (i, 0))],
        out_shape=[jax.ShapeDtypeStruct((s, d), BF16), jax.ShapeDtypeStruct((s, CONV_WIDTH), F32),
                   jax.ShapeDtypeStruct((s, POOL_WIDTH), F32)],
        scratch_shapes=[pltpu.VMEM((ts, SGU_WIDTH), F32), pltpu.VMEM((ts + MIX_HALO, CONV_WIDTH), F32),
                        pltpu.VMEM((ts + MIX_HALO, POOL_WIDTH), F32), pltpu.VMEM((7, ts + MIX_HALO - 8, CONV_WIDTH), F32)],
        compiler_params=pltpu.CompilerParams(dimension_semantics=("arbitrary",),
                                             vmem_limit_bytes=_vmem_limit(16 * ts * inw * 4)),
    )(z, z, *consts)


def _mixer_bwd_a(z, cb, dpool, dy, mp, name):
    s, inw = z.shape
    d = SGU_WIDTH + CONV_WIDTH + POOL_WIDTH
    ts = _row_tile(s, 256)
    nchunk = ts // CHUNK

    def rms_bwd(dyn, y, g):
        r = lax.rsqrt(_rmean(y * y) + EPS)
        yn = y * r
        dg = _rsum(dyn * yn)
        t = dyn * g
        return r * (t - yn * _rmean(t * yn)), dg

    def body(z_ref, cbs, dpool_ref, dy_ref, bd_ref, ng_ref, nb_ref, wm_ref, wmt_ref, bias_ref, cng_ref, cnb_ref,
             pw_ref, pwt_ref, ps_ref, bg_ref,
             dza_ref, dcb_ref, dd_ref, dbg_ref, dwm_ref, dbias_ref, dng_ref, dnb_ref, dcng_ref, dcnb_ref, dps_ref, dpw_ref,
             ya_s, f_s, dvn_s):
        i = pl.program_id(0)
        first = i == 0

        @pl.when(first)
        def _():
            for ref in (dwm_ref, dbias_ref, dng_ref, dnb_ref, dcng_ref, dcnb_ref, dps_ref, dpw_ref):
                ref[...] = jnp.zeros_like(ref)

        u, vhat, rstd, vn = _sgu_forward(z_ref, bd_ref, ng_ref, nb_ref, wm_ref, bias_ref, ts, ya_s, f_s)
        dya, dbg_a = rms_bwd(dy_ref[:, 0:SGU_WIDTH], ya_s[...], bg_ref[:, 0:SGU_WIDTH])
        du = dya * f_s[...]
        df = dya * u
        dfb = df.astype(BF16)
        left = _lane_lt((CHUNK, CHUNK), HEAD_DIM)
        zero = jnp.zeros((CHUNK, CHUNK), BF16)
        dbias = jnp.zeros((CHUNK, SGU_WIDTH), F32)
        for n in range(nchunk):
            rows = slice(n * CHUNK, (n + 1) * CHUNK)
            dbias = dbias + df[rows, :]
            for p in range(SGU_WIDTH // CHUNK):
                cols = slice(p * CHUNK, (p + 1) * CHUNK)
                dblk = dfb[rows, cols]
                vblk = vn[rows, cols]
                dwm_ref[2 * p] += _dot_nt(jnp.where(left, dblk, zero), vblk)
                dwm_ref[2 * p + 1] += _dot_nt(jnp.where(left, zero, dblk), vblk)
                dvn_s[rows, cols] = jnp.where(left, _dot(wmt_ref[2 * p], dblk), _dot(wmt_ref[2 * p + 1], dblk))
        dbias_ref[...] += dbias
        dvn = dvn_s[...]
        dng_ref[...] += _rsum(dvn * vhat)
        dnb_ref[...] += _rsum(dvn)
        dvh = dvn * ng_ref[...]
        bd = bd_ref[...]
        dv = rstd * (dvh - _group_mean(dvh, bd) - vhat * _group_mean(dvh * vhat, bd))
        dza_ref[:, 0:SGU_WIDTH] = (du * _gelu_grad(z_ref[:, 0:SGU_WIDTH])).astype(BF16)
        dza_ref[:, SGU_WIDTH:2 * SGU_WIDTH] = (dv * _gelu_grad(z_ref[:, SGU_WIDTH:2 * SGU_WIDTH])).astype(BF16)

        chat, crstd = _layer_norm_rows(cbs[...])
        lin = chat * cng_ref[...] + cnb_ref[...]
        sl = jax.nn.sigmoid(lin)
        dyb, dbg_b = rms_bwd(dy_ref[:, SGU_WIDTH:SGU_WIDTH + CONV_WIDTH], lin * sl, bg_ref[:, SGU_WIDTH:SGU_WIDTH + CONV_WIDTH])
        dlin = dyb * (sl * (1.0 + lin * (1.0 - sl)))
        dcng_ref[...] += _rsum(dlin * chat)
        dcnb_ref[...] += _rsum(dlin)
        dch = dlin * cng_ref[...]
        dcb_ref[...] = crstd * (dch - _rmean(dch) - chat * _rmean(dch * chat))

        dpb = dpool_ref[...].astype(BF16)
        ycp = _dot(dpb, pw_ref[...])
        dyc, dbg_c = rms_bwd(dy_ref[:, SGU_WIDTH + CONV_WIDTH:d], ycp * ps_ref[...], bg_ref[:, SGU_WIDTH + CONV_WIDTH:d])
        dps_ref[...] += _rsum(dyc * ycp)
        dycp = (dyc * ps_ref[...]).astype(BF16)
        dpw_ref[...] += _dot_tn(dpb, dycp)
        dd_ref[...] = _dot(dycp, pwt_ref[...])

        @pl.when(first)
        def _():
            dbg_ref[...] = jnp.zeros_like(dbg_ref)

        dbg_ref[:, 0:SGU_WIDTH] += dbg_a
        dbg_ref[:, SGU_WIDTH:SGU_WIDTH + CONV_WIDTH] += dbg_b
        dbg_ref[:, SGU_WIDTH + CONV_WIDTH:d] += dbg_c

    tile, _, _ = _mixer_specs(s, ts, inw)
    consts = [mp["bd"], mp["ng"], mp["nb"], mp["wm"], mp["wmt"], mp["bias"], mp["cng"], mp["cnb"],
              mp["pw"], mp["pwt"], mp["ps"], mp["bg"]]
    acc_shapes = [(1, d), (2 * (SGU_WIDTH // CHUNK), CHUNK, CHUNK), (CHUNK, SGU_WIDTH), (1, SGU_WIDTH), (1, SGU_WIDTH),
                  (1, CONV_WIDTH), (1, CONV_WIDTH), (1, POOL_WIDTH), (POOL_WIDTH, POOL_WIDTH)]
    return _hbm_call(
        body, name=name,
        grid=(s // ts,),
        in_specs=[tile, pl.BlockSpec((ts, CONV_WIDTH), lambda i: (i, 0)), pl.BlockSpec((ts, POOL_WIDTH), lambda i: (i, 0)),
                  pl.BlockSpec((ts, d), lambda i: (i, 0))] + [_const_spec(c.shape) for c in consts],
        out_specs=[pl.BlockSpec((ts, 2 * SGU_WIDTH), lambda i: (i, 0)), pl.BlockSpec((ts, CONV_WIDTH), lambda i: (i, 0)),
                   pl.BlockSpec((ts, POOL_WIDTH), lambda i: (i, 0))] + [_const_spec(a) for a in acc_shapes],
        out_shape=[jax.ShapeDtypeStruct((s, 2 * SGU_WIDTH), BF16), jax.ShapeDtypeStruct((s, CONV_WIDTH), F32),
                   jax.ShapeDtypeStruct((s, POOL_WIDTH), F32)] + [jax.ShapeDtypeStruct(a, F32) for a in acc_shapes],
        scratch_shapes=[pltpu.VMEM((ts, SGU_WIDTH), F32), pltpu.VMEM((ts, SGU_WIDTH), F32), pltpu.VMEM((ts, SGU_WIDTH), F32)],
        compiler_params=pltpu.CompilerParams(dimension_semantics=("arbitrary",),
                                             vmem_limit_bytes=_vmem_limit(24 * ts * inw * 4)),
    )(z, cb, dpool, dy, *consts)


def _mixer_bwd_b(z, dza, dcb, dd, x, dres, gain, s1p, cw, w, name):
    s, inw = z.shape
    d = x.shape[1]
    ts = _row_tile(s, 256)
    c0 = 2 * SGU_WIDTH
    c1 = c0 + 2 * CONV_WIDTH

    def body(z_ref, zh_ref, dza_ref, dcb_ref, dcbn_ref, dd_ref, ddn_ref, x_ref, dres_ref, g_ref, s_ref, cw_ref, w_ref,
             dx_ref, dz_ref, dsh_ref, dsc_ref, dg_ref, dcw_ref, dcbias_ref, ext_b, ext_n, ext_e, ext8):
        i = pl.program_id(0)
        first = i == 0
        last = i == pl.num_programs(0) - 1

        @pl.when(first)
        def _():
            for ref in (dsh_ref, dsc_ref, dg_ref, dcw_ref, dcbias_ref):
                ref[...] = jnp.zeros_like(ref)

        a = z_ref[:, c0:c0 + CONV_WIDTH]
        sg = jax.nn.sigmoid(z_ref[:, c0 + CONV_WIDTH:c1])
        ah = zh_ref[:, c0:c0 + CONV_WIDTH]
        gh = zh_ref[:, c0 + CONV_WIDTH:c1]
        ext_b[pl.ds(0, MIX_HALO), :] = jnp.where(first, 0.0, ah * jax.nn.sigmoid(gh))
        ext_b[pl.ds(MIX_HALO, ts), :] = a * sg
        _shifted_copies(ext_b, ext8, ts)
        dcbv = dcb_ref[...]
        dcbias_ref[...] += _rsum(dcbv)
        for k in range(CONV_K):
            dcw_ref[k:k + 1, :] += _rsum(dcbv * _rows_at(ext_b, ext8, MIX_HALO - (CONV_K - 1) + k, ts))

        ext_n[pl.ds(0, ts), :] = dcbv
        ext_n[pl.ds(ts, MIX_HALO), :] = jnp.where(last, 0.0, dcbn_ref[...])
        _shifted_copies(ext_n, ext8, ts)
        for r in range(ts // CONV_ROWS):
            acc = jnp.zeros((CONV_ROWS, CONV_WIDTH), F32)
            for k in range(CONV_K):
                acc = acc + cw_ref[k:k + 1, :] * _rows_at(ext_n, ext8, CONV_K - 1 - k + r * CONV_ROWS, CONV_ROWS)
            rows = pl.ds(r * CONV_ROWS, CONV_ROWS)
            ar = z_ref[rows, c0:c0 + CONV_WIDTH]
            sr = jax.nn.sigmoid(z_ref[rows, c0 + CONV_WIDTH:c1])
            dz_ref[rows, c0:c0 + CONV_WIDTH] = (acc * sr).astype(BF16)
            dz_ref[rows, c0 + CONV_WIDTH:c1] = (acc * ar * sr * (1.0 - sr)).astype(BF16)

        ddv = dd_ref[...]
        ext_e[pl.ds(0, ts), :] = ddv / _pool_counts(i, ts)
        nh = (i + 1) * ts + lax.broadcasted_iota(jnp.int32, (MIX_HALO, POOL_WIDTH), 0)
        lane = lax.broadcasted_iota(jnp.int32, (MIX_HALO, POOL_WIDTH), 1)
        gdim = POOL_WIDTH // len(POOL_WINDOWS)
        winh = jnp.where(lane < gdim, float(POOL_WINDOWS[0]),
                         jnp.where(lane < 2 * gdim, float(POOL_WINDOWS[1]),
                                   jnp.where(lane < 3 * gdim, float(POOL_WINDOWS[2]), float(POOL_WINDOWS[3]))))
        cnth = jnp.minimum((nh + 1).astype(F32), winh)
        ext_e[pl.ds(ts, MIX_HALO), :] = jnp.where(last, 0.0, ddn_ref[...] / cnth)
        dz_ref[:, c1:inw] = (_window_sums(ext_e, 0, ts, 1) - ddv).astype(BF16)
        dz_ref[:, 0:c0] = dza_ref[...]

        dh = _dot(dz_ref[...], w_ref[...])
        dx, dsh, dsc, dg = _norm_mod_bwd(dh, x_ref[...], g_ref[...], s_ref[...], dres_ref[...])
        dx_ref[...] = dx
        dsh_ref[...] += dsh
        dsc_ref[...] += dsc
        dg_ref[...] += dg

    tile, prev, _ = _mixer_specs(s, ts, inw)
    _, _, nxt_b = _mixer_specs(s, ts, CONV_WIDTH)
    _, _, nxt_c = _mixer_specs(s, ts, POOL_WIDTH)
    row = pl.BlockSpec((ts, d), lambda i: (i, 0))
    vec = pl.BlockSpec((1, d), lambda i: (0, 0))
    return _hbm_call(
        body, name=name,
        grid=(s // ts,),
        in_specs=[tile, prev, pl.BlockSpec((ts, c0), lambda i: (i, 0)),
                  pl.BlockSpec((ts, CONV_WIDTH), lambda i: (i, 0)), nxt_b,
                  pl.BlockSpec((ts, POOL_WIDTH), lambda i: (i, 0)), nxt_c,
                  row, row, vec, vec, _const_spec(cw.shape),
                  pl.BlockSpec(w.shape, lambda i: (0, 0), pipeline_mode=pl.Buffered(1))],
        out_specs=[row, pl.BlockSpec((ts, inw), lambda i: (i, 0)), vec, vec, vec,
                   _const_spec((CONV_K, CONV_WIDTH)), _const_spec((1, CONV_WIDTH))],
        out_shape=[jax.ShapeDtypeStruct((s, d), F32), jax.ShapeDtypeStruct((s, inw), BF16)]
        + [jax.ShapeDtypeStruct((1, d), F32)] * 3
        + [jax.ShapeDtypeStruct((CONV_K, CONV_WIDTH), F32), jax.ShapeDtypeStruct((1, CONV_WIDTH), F32)],
        scratch_shapes=[pltpu.VMEM((ts + MIX_HALO, CONV_WIDTH), F32), pltpu.VMEM((ts + MIX_HALO, CONV_WIDTH), F32),
                        pltpu.VMEM((ts + MIX_HALO, POOL_WIDTH), F32), pltpu.VMEM((7, ts + MIX_HALO - 8, CONV_WIDTH), F32)],
        compiler_params=pltpu.CompilerParams(dimension_semantics=("arbitrary",),
                                             vmem_limit_bytes=_vmem_limit(18 * ts * inw * 4 + inw * d * 2)),
    )(z, z, dza, dcb, dcb, dd, dd, x, dres, gain, s1p, cw, w)


def _ffn_specs(s, ts, tc, half_blocks):
    nbh = ts // FFN_HALO

    def tile(off):
        return pl.BlockSpec((ts, tc), lambda j, i: (i, j + off))

    def prev(off):
        return pl.BlockSpec((FFN_HALO, tc), lambda j, i: (jnp.maximum(i * nbh - 1, 0), j + off))

    def vec(rows, off):
        return pl.BlockSpec((rows, tc), lambda j, i: (0, j + off))

    return tile, prev, vec


def _rows_before(cur, prev, k):
    row = lax.broadcasted_iota(jnp.int32, cur.shape, 0)
    return jnp.where(row >= k, pltpu.roll(cur, k, 0), pltpu.roll(prev, k, 0))


def _conv3_rows(cur, prev, w_ref, b_ref, cols):
    x1 = _rows_before(cur, prev, 1)
    x2 = _rows_before(cur, prev, 2)
    u = b_ref[:, cols] + w_ref[2:3, cols] * cur + w_ref[1:2, cols] * x1 + w_ref[0:1, cols] * x2
    return u, x2, x1


def _halo_chunk(h_ref, cols, first):
    h = jnp.where(first, 0.0, h_ref[:, cols])
    return jnp.concatenate([h] * (FFN_ROWS // FFN_HALO), axis=0)


def _ffn_act_fwd(p, cw, cb, name):
    s, f2 = p.shape
    f = f2 // 2
    tc = f // 2
    hb = f // tc
    ts = _row_tile(s, 256)

    def body(pg_ref, pgh_ref, pv_ref, pvh_ref, wg_ref, wv_ref, bg_ref, bv_ref, act_ref):
        first = pl.program_id(1) == 0
        for c in range(tc // LANES):
            cols = slice(c * LANES, (c + 1) * LANES)

            def chunk(r, carry, cols=cols):
                pg_prev, pv_prev = carry
                rows = pl.ds(pl.multiple_of(r * FFN_ROWS, FFN_ROWS), FFN_ROWS)
                pg = pg_ref[rows, cols]
                pv = pv_ref[rows, cols]
                ug, _, _ = _conv3_rows(pg, pg_prev, wg_ref, bg_ref, cols)
                uv, _, _ = _conv3_rows(pv, pv_prev, wv_ref, bv_ref, cols)
                act_ref[rows, cols] = (_gelu(ug) * uv).astype(BF16)
                return pg, pv

            def step(r, carry, chunk=chunk):
                for u in range(FFN_UNROLL):
                    carry = chunk(r * FFN_UNROLL + u, carry)
                return carry

            lax.fori_loop(0, ts // (FFN_ROWS * FFN_UNROLL), step,
                          (_halo_chunk(pgh_ref, cols, first), _halo_chunk(pvh_ref, cols, first)))

    tile, prev, vec = _ffn_specs(s, ts, tc, hb)
    return _hbm_call(
        body, name=name,
        grid=(hb, s // ts),
        in_specs=[tile(0), prev(0), tile(hb), prev(hb), vec(FFN_CONV_K, 0), vec(FFN_CONV_K, hb), vec(1, 0), vec(1, hb)],
        out_specs=pl.BlockSpec((ts, tc), lambda j, i: (i, j)),
        out_shape=jax.ShapeDtypeStruct((s, f), BF16),
        compiler_params=pltpu.CompilerParams(dimension_semantics=("arbitrary", "arbitrary"),
                                             vmem_limit_bytes=_vmem_limit(8 * ts * tc * 4)),
    )(p, p, p, p, cw, cw, cb, cb)


def _ffn_act_bwd(p, dact, cw, cb, name):
    s, f2 = p.shape
    f = f2 // 2
    tc = f // 2
    hb = f // tc
    ts = _row_tile(s, 256)

    def body(pg_ref, pgh_ref, pv_ref, pvh_ref, da_ref, wg_ref, wv_ref, bg_ref, bv_ref,
             dug_ref, duv_ref, dwg_ref, dwv_ref, dbg_ref, dbv_ref):
        first = pl.program_id(1) == 0

        @pl.when(first)
        def _():
            for ref in (dwg_ref, dwv_ref, dbg_ref, dbv_ref):
                ref[...] = jnp.zeros_like(ref)

        zero = jnp.zeros((FFN_ROWS, LANES), F32)
        for c in range(tc // LANES):
            cols = slice(c * LANES, (c + 1) * LANES)

            def chunk(r, carry, cols=cols):
                pg_prev, pv_prev, ag0, ag1, ag2, av0, av1, av2, sg, sv = carry
                rows = pl.ds(pl.multiple_of(r * FFN_ROWS, FFN_ROWS), FFN_ROWS)
                pg = pg_ref[rows, cols]
                pv = pv_ref[rows, cols]
                ug, pg2, pg1 = _conv3_rows(pg, pg_prev, wg_ref, bg_ref, cols)
                uv, pv2, pv1 = _conv3_rows(pv, pv_prev, wv_ref, bv_ref, cols)
                da = da_ref[rows, cols]
                dug = da * uv * _gelu_grad(ug)
                duv = da * _gelu(ug)
                dug_ref[rows, cols] = dug
                duv_ref[rows, cols] = duv
                return (pg, pv, ag0 + dug * pg2, ag1 + dug * pg1, ag2 + dug * pg,
                        av0 + duv * pv2, av1 + duv * pv1, av2 + duv * pv, sg + dug, sv + duv)

            def step(r, carry, chunk=chunk):
                for u in range(FFN_UNROLL):
                    carry = chunk(r * FFN_UNROLL + u, carry)
                return carry

            out = lax.fori_loop(0, ts // (FFN_ROWS * FFN_UNROLL), step,
                                (_halo_chunk(pgh_ref, cols, first), _halo_chunk(pvh_ref, cols, first)) + (zero,) * 8)
            for k in range(FFN_CONV_K):
                dwg_ref[k:k + 1, cols] += _rsum(out[2 + k])
                dwv_ref[k:k + 1, cols] += _rsum(out[5 + k])
            dbg_ref[:, cols] += _rsum(out[8])
            dbv_ref[:, cols] += _rsum(out[9])

    tile, prev, vec = _ffn_specs(s, ts, tc, hb)
    half = pl.BlockSpec((ts, tc), lambda j, i: (i, j))
    wacc = pl.BlockSpec((FFN_CONV_K, tc), lambda j, i: (0, j))
    bacc = pl.BlockSpec((1, tc), lambda j, i: (0, j))
    return _hbm_call(
        body, name=name,
        grid=(hb, s // ts),
        in_specs=[tile(0), prev(0), tile(hb), prev(hb), half, vec(FFN_CONV_K, 0), vec(FFN_CONV_K, hb), vec(1, 0), vec(1, hb)],
        out_specs=[half, half, wacc, wacc, bacc, bacc],
        out_shape=[jax.ShapeDtypeStruct((s, f), F32)] * 2 + [jax.ShapeDtypeStruct((FFN_CONV_K, f), F32)] * 2
        + [jax.ShapeDtypeStruct((1, f), F32)] * 2,
        compiler_params=pltpu.CompilerParams(dimension_semantics=("arbitrary", "arbitrary"),
                                             vmem_limit_bytes=_vmem_limit(12 * ts * tc * 4)),
    )(p, p, p, p, dact, cw, cw, cb, cb)


def _ffn_in_bwd(dug, duv, cw, w, x, dres, gain, s1p, name):
    s, f = dug.shape
    d = x.shape[1]
    ts = _row_tile(s, 256)
    tc = w.shape[2]
    assert f % tc == 0 and w.shape[0] * tc == 2 * f
    nbh = ts // FFN_HALO

    def body(dug_ref, dugn_ref, duv_ref, duvn_ref, cw_ref, w_ref, x_ref, dres_ref, g_ref, s_ref,
             dx_ref, dp_ref, dsh_ref, dsc_ref, dg_ref, ext):
        i = pl.program_id(0)
        last = i == pl.num_programs(0) - 1

        @pl.when(i == 0)
        def _():
            for ref in (dsh_ref, dsc_ref, dg_ref):
                ref[...] = jnp.zeros_like(ref)

        dh = jnp.zeros((ts, d), F32)
        for half, (t_ref, n_ref) in enumerate(((dug_ref, dugn_ref), (duv_ref, duvn_ref))):
            for cb in range(f // tc):
                cols = slice(cb * tc, (cb + 1) * tc)
                wcols = slice(half * f + cb * tc, half * f + (cb + 1) * tc)
                ext[pl.ds(0, ts), :] = t_ref[:, cols]
                ext[pl.ds(ts, FFN_HALO), :] = jnp.where(last, 0.0, n_ref[:, cols])
                acc = cw_ref[FFN_CONV_K - 1:FFN_CONV_K, wcols] * t_ref[:, cols]
                for k in range(FFN_CONV_K - 1):
                    acc = acc + cw_ref[k:k + 1, wcols] * ext[pl.ds(FFN_CONV_K - 1 - k, ts), :]
                dpb = acc.astype(BF16)
                dp_ref[:, wcols] = dpb
                dh = dh + _dot_nt(dpb, w_ref[half * (f // tc) + cb])
        dx, dsh, dsc, dg = _norm_mod_bwd(dh, x_ref[...], g_ref[...], s_ref[...], dres_ref[...])
        dx_ref[...] = dx
        dsh_ref[...] += dsh
        dsc_ref[...] += dsc
        dg_ref[...] += dg

    tile = pl.BlockSpec((ts, f), lambda i: (i, 0))
    nxt = pl.BlockSpec((FFN_HALO, f), lambda i: (jnp.minimum((i + 1) * nbh, s // FFN_HALO - 1), 0))
    row = pl.BlockSpec((ts, d), lambda i: (i, 0))
    vec = pl.BlockSpec((1, d), lambda i: (0, 0))
    return _hbm_call(
        body, name=name,
        grid=(s // ts,),
        in_specs=[tile, nxt, tile, nxt, _const_spec(cw.shape),
                  pl.BlockSpec(w.shape, lambda i: (0, 0, 0), pipeline_mode=pl.Buffered(1)), row, row, vec, vec],
        out_specs=[row, pl.BlockSpec((ts, 2 * f), lambda i: (i, 0)), vec, vec, vec],
        out_shape=[jax.ShapeDtypeStruct((s, d), F32), jax.ShapeDtypeStruct((s, 2 * f), BF16)] + [jax.ShapeDtypeStruct((1, d), F32)] * 3,
        scratch_shapes=[pltpu.VMEM((ts + FFN_HALO, tc), F32)],
        compiler_params=pltpu.CompilerParams(
            dimension_semantics=("arbitrary",),
            vmem_limit_bytes=_vmem_limit(4 * ts * f * 4 + 2 * f * d * 2 + 2 * ts * 2 * f * 2 + 12 * ts * d * 4 + 6 * ts * tc * 4)),
    )(dug, dug, duv, duv, cw, w, x, dres, gain, s1p)


def _adamw_math(w, g, m, v):
    m = ADAM_B1 * m + (1.0 - ADAM_B1) * g
    v = ADAM_B2 * v + (1.0 - ADAM_B2) * (g * g)
    m_hat = m / (1.0 - ADAM_B1 ** ADAM_STEP)
    v_hat = v / (1.0 - ADAM_B2 ** ADAM_STEP)
    delta = -ADAM_LR * (m_hat / (jnp.sqrt(v_hat) + ADAM_EPS) + ADAM_WD * w)
    return delta, m, v


def _adam_rows(rows, cols):
    want = max(8, (2 * 1024 * 1024 // (cols * 4)) // 8 * 8)
    tr = min(rows, want)
    while rows % tr:
        tr -= 8
    return tr


def _adamw(w, m, v, g_parts, name):
    shape = w.shape
    nl = shape[0] if w.ndim == 3 else 1
    r, c = shape[-2], shape[-1]
    tr = _adam_rows(r, c)
    ng = len(g_parts)

    def body(*refs):
        w_ref, m_ref, v_ref = refs[0:3]
        g_refs = refs[3:3 + ng]
        g_out, d_out, m_out, v_out = refs[3 + ng:]
        g = g_refs[0][...]
        for gr in g_refs[1:]:
            g = g + gr[...]
        delta, mn, vn = _adamw_math(w_ref[...], g, m_ref[...], v_ref[...])
        g_out[...] = g
        d_out[...] = delta
        m_out[...] = mn
        v_out[...] = vn

    blk = pl.BlockSpec((None, tr, c), lambda l, i: (l, i, 0))
    outs = _hbm_call(
        body, name=name,
        grid=(nl, r // tr),
        in_specs=[blk] * (3 + ng),
        out_specs=[blk] * 4,
        out_shape=[jax.ShapeDtypeStruct((nl, r, c), F32)] * 4,
        compiler_params=pltpu.CompilerParams(dimension_semantics=("arbitrary", "arbitrary"),
                                             vmem_limit_bytes=_vmem_limit(2 * (7 + ng) * tr * max(c, 128) * 4 + (8 << 20))),
    )(*[a.reshape(nl, r, c) for a in (w, m, v, *g_parts)])
    return [o.reshape(shape) for o in outs]


def _adamw_many(ws, ms, vs, gs, name):
    n = len(ws)

    def body(*refs):
        for k in range(n):
            w_ref, m_ref, v_ref, g_ref = refs[k], refs[n + k], refs[2 * n + k], refs[3 * n + k]
            delta, mn, vn = _adamw_math(w_ref[...], g_ref[...], m_ref[...], v_ref[...])
            refs[4 * n + 3 * k][...] = delta
            refs[4 * n + 3 * k + 1][...] = mn
            refs[4 * n + 3 * k + 2][...] = vn

    specs = [_const_spec(a.shape) for a in ws]
    outs = _hbm_call(
        body, name=name,
        grid=(1,),
        in_specs=specs * 4,
        out_specs=[sp for sp in specs for _ in range(3)],
        out_shape=[jax.ShapeDtypeStruct(a.shape, F32) for a in ws for _ in range(3)],
        compiler_params=pltpu.CompilerParams(dimension_semantics=("arbitrary",),
                                             vmem_limit_bytes=_vmem_limit(20 * sum(a.size for a in ws) * 4 + (8 << 20))),
    )(*ws, *ms, *vs, *gs)
    return [tuple(outs[3 * k:3 * k + 3]) for k in range(n)]


def _modw_adamw(sct, dmod, w, m, v, name):
    nl, d, n = w.shape
    tr = _row_tile(d, 128)

    def body(sct_ref, dm_ref, w_ref, m_ref, v_ref, g_out, d_out, m_out, v_out):
        sc = sct_ref[...].astype(BF16).astype(F32)
        dm = dm_ref[...].astype(BF16).astype(F32)
        g = sc[:, 0:1] * dm[0:1, :]
        for b in range(1, N_DEV):
            g = g + sc[:, b:b + 1] * dm[b:b + 1, :]
        delta, mn, vn = _adamw_math(w_ref[...], g, m_ref[...], v_ref[...])
        g_out[...] = g
        d_out[...] = delta
        m_out[...] = mn
        v_out[...] = vn

    blk = pl.BlockSpec((None, tr, n), lambda l, i: (l, i, 0))
    return _hbm_call(
        body, name=name,
        grid=(nl, d // tr),
        in_specs=[pl.BlockSpec((tr, N_DEV), lambda l, i: (i, 0)), pl.BlockSpec((None, N_DEV, n), lambda l, i: (l, 0, 0)),
                  blk, blk, blk],
        out_specs=[blk] * 4,
        out_shape=[jax.ShapeDtypeStruct((nl, d, n), F32)] * 4,
        compiler_params=pltpu.CompilerParams(dimension_semantics=("arbitrary", "arbitrary"),
                                             vmem_limit_bytes=_vmem_limit(2 * 8 * tr * n * 4 + (8 << 20))),
    )(sct, dmod, w, m, v)


def _reduce4(recvs, name):
    nl = len(recvs)
    shape = recvs[0].shape[1:]
    c = shape[-1]
    r = math.prod(shape[:-1])
    tr = _adam_rows(r, c)
    nt = r // tr

    def body(*refs):
        o_ref = refs[nl]
        for l in range(nl):
            @pl.when(pl.program_id(0) == l)
            def _():
                acc = refs[l][0].astype(F32)
                for k in range(1, N_CHIPS):
                    acc = acc + refs[l][k].astype(F32)
                o_ref[...] = acc

    def in_map(l):
        return lambda ll, i: (0, jnp.where(ll < l, 0, jnp.where(ll > l, nt - 1, i)), 0)

    return _hbm_call(
        body, name=name,
        grid=(nl, nt),
        in_specs=[pl.BlockSpec((N_CHIPS, tr, c), in_map(l)) for l in range(nl)],
        out_specs=pl.BlockSpec((None, tr, c), lambda ll, i: (ll, i, 0)),
        out_shape=jax.ShapeDtypeStruct((nl, r, c), F32),
        compiler_params=pltpu.CompilerParams(dimension_semantics=("arbitrary", "arbitrary"),
                                             vmem_limit_bytes=_vmem_limit(2 * 8 * nl * tr * max(c, 128) * 4 + (8 << 20))),
    )(*[rv.reshape(N_CHIPS, r, c) for rv in recvs]).reshape((nl,) + shape)


def _my_place():
    return lax.axis_index("x"), lax.axis_index("y"), lax.axis_index("c")


def _chip_coords(j):
    return j // 2, j % 2


def _mod_forward(c, mod_w, mod_b4):
    nl, d, n = mod_w.shape
    kc = 256

    def body(c_ref, w_ref, b_ref, mod_ref, sc_ref, cbuf, stage, s1, r1, s2, r2):
        mx, my, mc = _my_place()
        me = 4 * mx + 2 * my + mc
        q = 2 * mx + my
        cv = c_ref[...]
        cbuf[me] = jnp.broadcast_to(cv * jax.nn.sigmoid(cv), (8, d))
        sends = []
        for t in range(N_DEV):
            tx, ty = _chip_coords(t // 2)
            cp = pltpu.make_async_remote_copy(src_ref=cbuf.at[me], dst_ref=cbuf.at[me], send_sem=s1.at[t], recv_sem=r1.at[me],
                                              device_id=(tx, ty, t % 2), device_id_type=MESH)

            @pl.when(t != me)
            def _():
                cp.start()

            sends.append((t, cp))
        for t in range(N_DEV):
            @pl.when(t != me)
            def _():
                pltpu.make_async_remote_copy(src_ref=cbuf.at[t], dst_ref=cbuf.at[t], send_sem=s1.at[t], recv_sem=r1.at[t],
                                             device_id=(mx, my, mc), device_id_type=MESH).wait_recv()
        for t, cp in sends:
            @pl.when(t != me)
            def _():
                cp.wait_send()

        row = lax.broadcasted_iota(jnp.int32, (8, d), 0)
        sc_all = jnp.zeros((8, d), F32)
        for t in range(N_DEV):
            sc_all = sc_all + jnp.where(row == t, cbuf[t], 0.0)
        sc_ref[...] = sc_all
        rown = lax.broadcasted_iota(jnp.int32, (8, n), 0)
        for l in range(nl):
            acc = jnp.zeros((8, n), F32)
            for k0 in range(0, d, kc):
                acc = acc + _dot(sc_all[:, k0:k0 + kc].astype(BF16), w_ref[l, k0:k0 + kc, :].astype(BF16))
            acc = acc + b_ref[l, q]
            for j in range(N_CHIPS):
                jx, jy = _chip_coords(j)
                bdest = 4 * jx + 2 * jy + mc
                rowv = jnp.sum(jnp.where(rown == bdest, acc, 0.0), axis=0, keepdims=True)
                stage[j, l] = jnp.broadcast_to(rowv, (8, n))
        sends2 = []
        for j in range(N_CHIPS):
            jx, jy = _chip_coords(j)
            cp = pltpu.make_async_remote_copy(src_ref=stage.at[j], dst_ref=mod_ref.at[:, q], send_sem=s2.at[j], recv_sem=r2.at[q],
                                              device_id=(jx, jy, mc), device_id_type=MESH)

            @pl.when(j != q)
            def _():
                cp.start()

            @pl.when(j == q)
            def _():
                for l in range(nl):
                    mod_ref[l, j] = stage[j, l]

            sends2.append((j, cp))
        for j in range(N_CHIPS):
            @pl.when(j != q)
            def _():
                pltpu.make_async_remote_copy(src_ref=stage.at[j], dst_ref=mod_ref.at[:, j], send_sem=s2.at[j], recv_sem=r2.at[j],
                                             device_id=(mx, my, mc), device_id_type=MESH).wait_recv()
        for j, cp in sends2:
            @pl.when(j != q)
            def _():
                cp.wait_send()

    vm = pl.BlockSpec(memory_space=pltpu.VMEM)
    return pl.pallas_call(
        body, name="mod_forward",
        in_specs=[vm, vm, vm],
        out_specs=[vm, vm],
        out_shape=[jax.ShapeDtypeStruct((nl, N_CHIPS, 8, n), F32), jax.ShapeDtypeStruct((8, d), F32)],
        scratch_shapes=[pltpu.VMEM((N_DEV, 8, d), F32), pltpu.VMEM((N_CHIPS, nl, 8, n), F32),
                        pltpu.SemaphoreType.DMA((N_DEV,)), pltpu.SemaphoreType.DMA((N_DEV,)),
                        pltpu.SemaphoreType.DMA((N_CHIPS,)), pltpu.SemaphoreType.DMA((N_CHIPS,))],
        compiler_params=pltpu.CompilerParams(vmem_limit_bytes=_vmem_limit(2 * nl * d * n * 4 + (8 << 20))),
    )(c, mod_w, mod_b4)


_HBM_SPEC = pl.BlockSpec(memory_space=pltpu.HBM)
_SEM_SPEC = pl.BlockSpec(memory_space=pltpu.SEMAPHORE)
_DATAFLOW = pltpu.SideEffectType.DATAFLOW_SIDE_EFFECTING


def _slot(ref, scatter, j):
    return ref.at[j] if scatter else ref


def _exchange_start(groups, scatter, after, name):
    flat = [a for g in groups for a in g]
    na = len(flat)
    ng = len(groups)
    sizes = [len(g) for g in groups]
    first = [sum(sizes[:g]) for g in range(ng)]
    where = [(g, k) for g in range(ng) for k in range(sizes[g])]
    mx, my, _ = _my_place()
    qo = 2 * mx + my
    lands = []
    for a in flat:
        own = lax.dynamic_index_in_dim(a, qo, 0, keepdims=False) if scatter else a
        lands.append(lax.dynamic_update_index_in_dim(lax.empty((N_CHIPS,) + own.shape, a.dtype), own, qo, 0))

    def body(*refs):
        ins, lnd = refs[:na], refs[na:2 * na]
        ssems, rsems = refs[2 * na + 1:2 * na + 1 + ng], refs[2 * na + 1 + ng:2 * na + 1 + 2 * ng]
        token = refs[-1]
        mx, my, mc = _my_place()
        q = 2 * mx + my
        for j in range(N_CHIPS):
            jx, jy = _chip_coords(j)
            for a in range(na):
                g, k = where[a]

                @pl.when(j != q)
                def _():
                    pltpu.make_async_remote_copy(src_ref=_slot(ins[a], scatter, j), dst_ref=lnd[a].at[q],
                                                 send_sem=ssems[g].at[k * N_CHIPS + j], recv_sem=rsems[g].at[k * N_CHIPS + q],
                                                 device_id=(jx, jy, mc), device_id_type=MESH).start()
        token[...] = jnp.zeros_like(token)

    sem_shapes = [pltpu.SemaphoreType.DMA((n * N_CHIPS,)) for n in sizes]
    outs = pl.pallas_call(
        body, name=name,
        in_specs=[_HBM_SPEC] * (2 * na) + [pl.BlockSpec(memory_space=pl.ANY)],
        out_specs=[_SEM_SPEC] * (2 * ng) + [_HBM_SPEC] * (2 * na) + [pl.BlockSpec(memory_space=pltpu.VMEM)],
        out_shape=sem_shapes + sem_shapes + [pltpu.HBM(a.shape, a.dtype) for a in flat + lands]
        + [jax.ShapeDtypeStruct((8, 128), F32)],
        input_output_aliases={i: 2 * ng + i for i in range(2 * na)},
        compiler_params=pltpu.CompilerParams(has_side_effects=_DATAFLOW),
    )(*[pltpu.with_memory_space_constraint(a, pltpu.HBM) for a in flat + lands], after)
    ssems, rsems = outs[:ng], outs[ng:2 * ng]
    src_thru, land_thru = outs[2 * ng:2 * ng + na], outs[2 * ng + na:2 * ng + 2 * na]
    states = [(src_thru[first[g]:first[g] + sizes[g]], land_thru[first[g]:first[g] + sizes[g]], ssems[g], rsems[g])
              for g in range(ng)]
    return states, outs[-1]


def _exchange_wait(state, scatter, after, name):
    srcs, lands, ssem, rsem = state
    na = len(srcs)

    def body(*refs):
        ins, lnd = refs[:na], refs[na:2 * na]
        ssem_ref, rsem_ref = refs[2 * na], refs[2 * na + 1]
        mx, my, mc = _my_place()
        q = 2 * mx + my
        for j in range(N_CHIPS):
            for a in range(na):
                @pl.when(j != q)
                def _():
                    cp = pltpu.make_async_remote_copy(src_ref=_slot(ins[a], scatter, j), dst_ref=lnd[a].at[j],
                                                      send_sem=ssem_ref.at[a * N_CHIPS + j], recv_sem=rsem_ref.at[a * N_CHIPS + j],
                                                      device_id=(mx, my, mc), device_id_type=MESH)
                    cp.wait_send()
                    cp.wait_recv()

    outs = pl.pallas_call(
        body, name=name,
        in_specs=[_HBM_SPEC] * (2 * na) + [_SEM_SPEC, _SEM_SPEC, pl.BlockSpec(memory_space=pl.ANY)],
        out_specs=[_HBM_SPEC] * (2 * na),
        out_shape=[pltpu.HBM(a.shape, a.dtype) for a in list(srcs) + list(lands)],
        input_output_aliases={i: i for i in range(2 * na)},
        compiler_params=pltpu.CompilerParams(has_side_effects=_DATAFLOW),
    )(*srcs, *lands, ssem, rsem, after)
    return outs[na:]


def _sibling_copy(src, dst, ssem, rsem, a):
    mx, my, mc = _my_place()
    return pltpu.make_async_remote_copy(src_ref=src, dst_ref=dst, send_sem=ssem.at[a], recv_sem=rsem.at[a],
                                        device_id=(mx, my, 1 - mc), device_id_type=MESH)


def _swap_start(arrs, after, name):
    na = len(arrs)
    lands = [lax.empty(a.shape, a.dtype) for a in arrs]

    def body(*refs):
        ins, lnd = refs[:na], refs[na:2 * na]
        ssem, rsem, token = refs[2 * na + 1], refs[2 * na + 2], refs[-1]
        for a in range(na):
            _sibling_copy(ins[a], lnd[a], ssem, rsem, a).start()
        token[...] = jnp.zeros_like(token)

    outs = pl.pallas_call(
        body, name=name,
        in_specs=[_HBM_SPEC] * (2 * na) + [pl.BlockSpec(memory_space=pl.ANY)],
        out_specs=[_SEM_SPEC] * 2 + [_HBM_SPEC] * (2 * na) + [pl.BlockSpec(memory_space=pltpu.VMEM)],
        out_shape=[pltpu.SemaphoreType.DMA((na,))] * 2 + [pltpu.HBM(a.shape, a.dtype) for a in list(arrs) + lands]
        + [jax.ShapeDtypeStruct((8, 128), F32)],
        input_output_aliases={i: 2 + i for i in range(2 * na)},
        compiler_params=pltpu.CompilerParams(has_side_effects=_DATAFLOW),
    )(*[pltpu.with_memory_space_constraint(a, pltpu.HBM) for a in list(arrs) + lands], after)
    return (outs[2:2 + na], outs[2 + na:2 + 2 * na], outs[0], outs[1]), outs[-1]


def _swap_wait(state, after, name):
    srcs, lands, ssem, rsem = state
    na = len(srcs)

    def body(*refs):
        ins, lnd = refs[:na], refs[na:2 * na]
        ssem_ref, rsem_ref = refs[2 * na], refs[2 * na + 1]
        for a in range(na):
            cp = _sibling_copy(ins[a], lnd[a], ssem_ref, rsem_ref, a)
            cp.wait_send()
            cp.wait_recv()

    outs = pl.pallas_call(
        body, name=name,
        in_specs=[_HBM_SPEC] * (2 * na) + [_SEM_SPEC, _SEM_SPEC, pl.BlockSpec(memory_space=pl.ANY)],
        out_specs=[_HBM_SPEC] * (2 * na),
        out_shape=[pltpu.HBM(a.shape, a.dtype) for a in list(srcs) + list(lands)],
        input_output_aliases={i: i for i in range(2 * na)},
        compiler_params=pltpu.CompilerParams(has_side_effects=_DATAFLOW),
    )(*srcs, *lands, ssem, rsem, after)
    return outs[:na], outs[na:]


def _allreduce_small(rows_all, rows_sum):
    ra, c = rows_all.shape
    r = rows_sum.shape[0]
    ch = r // N_DEV
    assert ch % 8 == 0 and ch * N_DEV == r

    def body(a_ref, s_ref, all_ref, sum_ref, rbuf, red, sa, rva, sb, rvb, sc, rvc):
        mx, my, mc = _my_place()
        me = 4 * mx + 2 * my + mc
        mine = pl.ds(pl.multiple_of(me * ch, 8), ch)
        all_ref[me] = a_ref[...]
        rbuf[me] = s_ref[mine, :]

        def dev(t):
            tx, ty = _chip_coords(t // 2)
            return (tx, ty, t % 2)

        def everyone_else(fn):
            for t in range(N_DEV):
                @pl.when(t != me)
                def _():
                    fn(t)

        def copy_a(t, slot):
            return pltpu.make_async_remote_copy(src_ref=a_ref, dst_ref=all_ref.at[slot], send_sem=sa.at[t], recv_sem=rva.at[slot],
                                                device_id=dev(t), device_id_type=MESH)

        def copy_b(t, slot):
            return pltpu.make_async_remote_copy(src_ref=s_ref.at[pl.ds(t * ch, ch), :], dst_ref=rbuf.at[slot], send_sem=sb.at[t],
                                                recv_sem=rvb.at[slot], device_id=dev(t), device_id_type=MESH)

        def copy_c(t, chunk_start, slot):
            return pltpu.make_async_remote_copy(src_ref=red, dst_ref=sum_ref.at[pl.ds(chunk_start, ch), :], send_sem=sc.at[t],
                                                recv_sem=rvc.at[slot], device_id=dev(t), device_id_type=MESH)

        everyone_else(lambda t: (copy_a(t, me).start(), copy_b(t, me).start()))
        everyone_else(lambda t: (copy_a(t, t).wait_recv(), copy_b(t, t).wait_recv()))
        everyone_else(lambda t: (copy_a(t, me).wait_send(), copy_b(t, me).wait_send()))
        acc = rbuf[0]
        for t in range(1, N_DEV):
            acc = acc + rbuf[t]
        red[...] = acc
        sum_ref[mine, :] = acc
        everyone_else(lambda t: copy_c(t, pl.multiple_of(me * ch, 8), me).start())
        everyone_else(lambda t: copy_c(t, t * ch, t).wait_recv())
        everyone_else(lambda t: copy_c(t, pl.multiple_of(me * ch, 8), me).wait_send())

    vm = pl.BlockSpec(memory_space=pltpu.VMEM)
    return pl.pallas_call(
        body, name="allreduce_small",
        in_specs=[vm, vm],
        out_specs=[vm, vm],
        out_shape=[jax.ShapeDtypeStruct((N_DEV, ra, c), F32), jax.ShapeDtypeStruct((r, c), F32)],
        scratch_shapes=[pltpu.VMEM((N_DEV, ch, c), F32), pltpu.VMEM((ch, c), F32)] + [pltpu.SemaphoreType.DMA((N_DEV,))] * 6,
        compiler_params=pltpu.CompilerParams(vmem_limit_bytes=_vmem_limit((3 * r + 2 * N_DEV * ra) * c * 4 + (4 << 20))),
    )(rows_all, rows_sum)


def _pack(arrs, row_multiple=8):
    rows, layout, at = [], [], 0
    for a in arrs:
        n = a.size
        nr = -(-n // (8 * SMALL_COLS)) * 8
        flat = a.reshape(-1)
        if nr * SMALL_COLS != n:
            flat = jnp.pad(flat, (0, nr * SMALL_COLS - n))
        rows.append(flat.reshape(nr, SMALL_COLS))
        layout.append((at, nr, a.shape))
        at += nr
    pad = -at % row_multiple
    if pad:
        rows.append(jnp.zeros((pad, SMALL_COLS), F32))
    return jnp.concatenate(rows, axis=0), layout


def _unpack(buf, layout):
    out = []
    for at, nr, shape in layout:
        n = math.prod(shape)
        out.append(buf[at:at + nr].reshape(-1)[:n].reshape(shape))
    return out


SMALL_NAMES = ("mod_b", "mix_pre_g", "mix_post_g", "sgu_norm_g", "sgu_norm_b", "sgu_w", "sgu_b", "conv_b", "conv_norm_g",
               "conv_norm_b", "pool_w", "pool_scale", "branch_g", "ffn_pre_g", "ffn_post_g", "ffn_conv_b")
SHARDED_SMALL = ("conv_w", "ffn_conv_w")
WEIGHT_ORDER = ("mod_w", "mod_b", "mix_pre_g", "mix_post_g", "w_in", "sgu_norm_g", "sgu_norm_b", "sgu_w", "sgu_b", "conv_w",
                "conv_b", "conv_norm_g", "conv_norm_b", "pool_w", "pool_scale", "branch_g", "w_out", "ffn_pre_g", "ffn_post_g",
                "ffn_up", "ffn_conv_w", "ffn_conv_b", "ffn_down")


def _block_diag(blocks):
    n, a, b = blocks.shape
    eye = jnp.eye(n, dtype=blocks.dtype)
    return (eye[:, None, :, None] * blocks[:, :, None, :]).reshape(n * a, n * b)


def _diag_blocks(mat, n):
    a = mat.shape[0] // n
    return jnp.stack([mat[g * a:(g + 1) * a, g * a:(g + 1) * a] for g in range(n)])


def _step(x, c, loss_target, w, m, v):
    nl = w["mod_w"].shape[0]
    s, d = x.shape[1], x.shape[2]
    heads = SGU_WIDTH // HEAD_DIM
    groups = len(POOL_WINDOWS)
    mx, my, _ = _my_place()
    q = 2 * mx + my
    x0 = x.reshape(s, d)
    tgt = loss_target.reshape(s, d)

    nmod = w["mod_w"].shape[2]
    kin = w["w_in"].shape[2]
    inw = kin * N_CHIPS
    f2 = w["ffn_up"].shape[2] * N_CHIPS
    f = f2 // 2

    def wgroups(l):
        return [[jnp.swapaxes(w["w_in"][l], 0, 1).astype(BF16), w["conv_w"][l], w["ffn_conv_w"][l]], [w["w_out"][l].astype(BF16)],
                [w["ffn_up"][l].astype(BF16)], [w["ffn_down"][l].astype(BF16)]]

    gstates = {}
    (gstates[0, 0], gstates[0, 1]), gtoken = _exchange_start(wgroups(0)[:2], False, c, "gather_start_in_0")
    mod4, sc_all = _mod_forward(c + gtoken[0:1, 0:1], w["mod_w"], w["mod_b"].reshape(nl, N_CHIPS, 1, nmod))
    mod = mod4[:, :, 0, :].reshape(nl, N_MOD, 1, d)

    tril = jnp.tril(jnp.ones((CHUNK, CHUNK), bool))
    bd = _block_diag(jnp.ones((heads, HEAD_DIM, HEAD_DIM), BF16))

    def mixer_params(l, conv_w):
        wm = jnp.where(tril[None], w["sgu_w"][l], 0.0)
        pw = _block_diag(w["pool_w"][l])
        return dict(
            bd=bd, ng=w["sgu_norm_g"][l][None], nb=w["sgu_norm_b"][l][None],
            wm=wm.astype(BF16), wmt=jnp.swapaxes(wm, 1, 2).astype(BF16),
            bias=jnp.repeat(w["sgu_b"][l].T, HEAD_DIM, axis=1),
            cw=conv_w, cb=w["conv_b"][l][None], cng=w["conv_norm_g"][l][None], cnb=w["conv_norm_b"][l][None],
            pw=pw.astype(BF16), pwt=pw.T.astype(BF16), ps=w["pool_scale"][l][None], bg=w["branch_g"][l][None])

    saved = []
    xl = x0
    arrived = {0: list(_exchange_wait(gstates[0, 0], False, mod4, "gather_wait_in_0"))}
    arrived[0] += list(_exchange_wait(gstates[0, 1], False, arrived[0][0], "gather_wait_out_0"))
    for l in range(nl):
        sh1, sc1, g1, sh2, sc2, g2 = [mod[l, k] for k in range(N_MOD)]
        gpre1, gpost1 = w["mix_pre_g"][l][None], w["mix_post_g"][l][None]
        gpre2, gpost2 = w["ffn_pre_g"][l][None], w["ffn_post_g"][l][None]
        fcb = w["ffn_conv_b"][l][None]
        sh1_after, bg_after, g1_after, sh2_after, fcb_after = sh1, w["branch_g"][l][None], g1, sh2, fcb
        g_win, g_cw, g_fcw = arrived[l][:3]
        if l == 0:
            (gstates[0, 2],), tok = _exchange_start(wgroups(0)[2:3], False, arrived[0][3], "gather_start_up_0")
            sh1_after = sh1 + tok[0:1, 0:1]
        w_in = g_win.reshape(inw, d)
        conv_w = jnp.transpose(g_cw, (1, 0, 2)).reshape(CONV_K, CONV_WIDTH)
        ffn_cw = jnp.transpose(g_fcw, (1, 0, 2)).reshape(FFN_CONV_K, f2)
        mp = mixer_params(l, conv_w)
        z, h1 = _norm_mod_matmul(xl, gpre1, 1.0 + sc1, sh1_after, w_in[None], f"mix_in_{l}", transposed=True)
        w_out = arrived[l][3].reshape(d, d)
        ycat, cbo, dpool = _mixer_fwd(z, dict(mp, bg=bg_after), f"mixer_fwd_{l}")
        (up,) = _exchange_wait(gstates[l, 2], False, ycat, f"gather_wait_up_{l}")
        if l == 0:
            (gstates[0, 3],), tok = _exchange_start(wgroups(0)[3:4], False, up, "gather_start_down_0")
            g1_after = g1 + tok[0:1, 0:1]
        o, x1 = _matmul_norm_resid(ycat, w_out, xl, g1_after, gpost1, f"mix_out_{l}")
        if l + 1 < nl:
            (gstates[l + 1, 0], gstates[l + 1, 1]), tok = _exchange_start(wgroups(l + 1)[0:2], False, x1,
                                                                        f"gather_start_in_{l + 1}")
            sh2_after = sh2 + tok[0:1, 0:1]
        p, h2 = _norm_mod_matmul(x1, gpre2, 1.0 + sc2, sh2_after, up, f"ffn_in_{l}")
        if l + 1 < nl:
            nxt = _exchange_wait(gstates[l + 1, 0], False, p, f"gather_wait_in_{l + 1}")
            nxt_out = _exchange_wait(gstates[l + 1, 1], False, nxt[0], f"gather_wait_out_{l + 1}")
            arrived[l + 1] = list(nxt) + list(nxt_out)
            (gstates[l + 1, 2], gstates[l + 1, 3]), tok = _exchange_start(wgroups(l + 1)[2:4], False, nxt_out[0],
                                                                        f"gather_start_up_{l + 1}")
            fcb_after = fcb + tok[0:1, 0:1]
        act = _ffn_act_fwd(p, ffn_cw, fcb_after, f"ffn_act_{l}")
        (g_down,) = _exchange_wait(gstates[l, 3], False, act, f"gather_wait_down_{l}")
        down = g_down.reshape(f, d)
        if l + 1 < nl:
            qo, x2 = _matmul_norm_resid(act, down, x1, g2, gpost2, f"ffn_out_{l}")
        else:
            qo, dx, loss_row = _matmul_norm_resid_loss(act, down, x1, g2, gpost2, tgt, f"ffn_out_{l}")
            x2 = None
        saved.append(dict(x=xl, z=z, h1=h1, ycat=ycat, cbo=cbo, dpool=dpool, o=o, x1=x1, p=p, h2=h2, act=act, qo=qo, mp=mp, fcb=fcb,
                          w_in=w_in, w_out=w_out, up=up, down=down, ffn_cw=ffn_cw,
                          mods=(sh1, sc1, g1, sh2, sc2, g2), gains=(gpre1, gpost1, gpre2, gpost2)))
        xl = x2


    small = {n: [None] * nl for n in SMALL_NAMES + SHARDED_SMALL}
    dmods = [None] * nl
    tn = f2 // N_CHIPS
    sstates = {}
    token = None
    for l in reversed(range(nl)):
        sv = saved[l]
        sh1, sc1, g1, sh2, sc2, g2 = sv["mods"]
        gpre1, gpost1, gpre2, gpost2 = sv["gains"]
        if token is not None:
            g2 = g2 + token[0:1, 0:1]
        dq, dact, dg2, dgpost2 = _resid_bwd_matmul(dx, sv["qo"], g2, gpost2, sv["down"], f"ffn_out_bwd_{l}")
        g_down = _wgrad(sv["act"], dq, (f, lambda j: 0), (d, lambda j: 0), jax.ShapeDtypeStruct((f, d), BF16),
                        (1, lambda j: (0, 0)), (f, d), f"wgrad_ffn_down_{l}")
        dug, duv, dfwg, dfwv, dfbg, dfbv = _ffn_act_bwd(sv["p"], dact, sv["ffn_cw"], sv["fcb"], f"ffn_act_bwd_{l}")
        dx1, dp, dsh2, dsc2, dgpre2 = _ffn_in_bwd(dug, duv, sv["ffn_cw"], sv["up"], sv["x1"], dx, gpre2, 1.0 + sc2,
                                                  f"ffn_in_bwd_{l}")
        g_up = _wgrad(sv["h2"], dp, (d, lambda j: 0), (tn, lambda j: j), jax.ShapeDtypeStruct((N_CHIPS, d, tn), BF16),
                      (N_CHIPS, lambda j: (j, 0, 0)), (None, d, tn), f"wgrad_ffn_up_{l}")
        (sstates[l, 0],), token = _exchange_start([[g_down.reshape(N_CHIPS, f // N_CHIPS, d), g_up]], True, g_up,
                                                  f"scatter_start_ffn_{l}")
        do, dycat, dg1, dgpost1 = _resid_bwd_matmul(dx1, sv["o"], g1 + token[0:1, 0:1], gpost1, sv["w_out"],
                                                    f"mix_out_bwd_{l}")
        g_out = _wgrad(sv["ycat"], do, (d, lambda j: 0), (d, lambda j: 0), jax.ShapeDtypeStruct((d, d), BF16),
                       (1, lambda j: (0, 0)), (d, d), f"wgrad_w_out_{l}")
        (sstates[l, 1],), token = _exchange_start([[g_out.reshape(N_CHIPS, d // N_CHIPS, d)]], True, g_out,
                                                  f"scatter_start_out_{l}")
        mp_after = dict(sv["mp"], bg=sv["mp"]["bg"] + token[0:1, 0:1])
        (dza, dcb, dd, dbg, dwm, dbias, dng, dnb, dcng, dcnb, dps, dpw) = _mixer_bwd_a(sv["z"], sv["cbo"], sv["dpool"], dycat, mp_after, f"mixer_bwd_a_{l}")
        dx, dz, dsh1, dsc1, dgpre1, dcw, dcbias = _mixer_bwd_b(
            sv["z"], dza, dcb, dd, sv["x"], dx1, gpre1, 1.0 + sc1, sv["mp"]["cw"], sv["w_in"], f"mixer_bwd_b_{l}")
        g_in = _wgrad(dz, sv["h1"], (inw, lambda j: 0), (d, lambda j: 0), jax.ShapeDtypeStruct((inw, d), BF16),
                      (1, lambda j: (0, 0)), (inw, d), f"wgrad_w_in_{l}")
        g_in_parts = g_in.reshape(N_CHIPS, kin, d)
        if l > 0:
            (sstates[l, 2],), token = _exchange_start([[g_in_parts]], True, g_in_parts, f"scatter_start_in_{l}")

        dmods[l] = jnp.concatenate([dsh1, dsc1, dg1, dsh2, dsc2, dg2], axis=0)
        small["mix_pre_g"][l], small["mix_post_g"][l] = dgpre1[0], dgpost1[0]
        small["ffn_pre_g"][l], small["ffn_post_g"][l] = dgpre2[0], dgpost2[0]
        small["sgu_norm_g"][l], small["sgu_norm_b"][l] = dng[0], dnb[0]
        small["sgu_w"][l] = jnp.where(tril[None], dwm, 0.0)
        small["sgu_b"][l] = dbias.reshape(CHUNK, heads, HEAD_DIM).sum(-1).T
        small["conv_b"][l], small["conv_norm_g"][l], small["conv_norm_b"][l] = dcbias[0], dcng[0], dcnb[0]
        small["pool_w"][l], small["pool_scale"][l], small["branch_g"][l] = _diag_blocks(dpw, groups), dps[0], dbg[0]
        small["ffn_conv_b"][l] = jnp.concatenate([dfbg[0], dfbv[0]])
        small["conv_w"][l] = dcw
        small["ffn_conv_w"][l] = jnp.concatenate([dfwg, dfwv], axis=1)

    names = [n for n in SMALL_NAMES if n != "mod_b"] + list(SHARDED_SMALL)
    dmod_rows, _ = _pack([jnp.stack(dmods)])
    packed, layout = _pack([jnp.stack(dmods), loss_row] + [jnp.stack(small[n]) for n in names], 8 * N_DEV)
    gathered, summed = _allreduce_small(dmod_rows, packed)
    (sstates[0, 2],), token = _exchange_start([[g_in_parts]], True, summed, "scatter_start_in_0")
    parts = _unpack(summed, layout)
    loss = parts[1][0, 0]
    gsmall = dict(zip(names, parts[2:]))
    gsmall["mod_b"] = parts[0].reshape(nl, N_MOD * d)
    dmod_all = gathered[:, :nl * N_MOD].reshape(N_DEV, nl, N_MOD * d)
    dmod_mine = jnp.transpose(lax.dynamic_slice_in_dim(dmod_all, q * nmod, nmod, axis=2), (1, 0, 2))

    grads, deltas, new_m, new_v = {}, {}, {}, {}

    def put(name, res):
        grads[name], deltas[name], new_m[name], new_v[name] = res

    recv = dict(w_in=[None] * nl, w_out=[None] * nl, ffn_up=[None] * nl, ffn_down=[None] * nl)
    done = token
    for l in reversed(range(nl)):
        recv["ffn_down"][l], recv["ffn_up"][l] = _exchange_wait(sstates[l, 0], True, done, f"scatter_wait_ffn_{l}")
        (recv["w_out"][l],) = _exchange_wait(sstates[l, 1], True, recv["ffn_up"][l], f"scatter_wait_out_{l}")
        done = recv["w_out"][l]
    big = ("w_out", "ffn_up", "ffn_down", "w_in")
    mine = {n: _reduce4(recv[n], f"reduce4_{n}") for n in big[:3]}
    swap_a, token = _swap_start([mine[n] for n in big[:3]], mine["ffn_down"], "swap_start_a")

    put("mod_w", _modw_adamw(sc_all.T + token[0:1, 0:1], dmod_mine, w["mod_w"], m["mod_w"], v["mod_w"], "adamw_mod_w"))
    done = grads["mod_w"]
    for l in reversed(range(nl)):
        (recv["w_in"][l],) = _exchange_wait(sstates[l, 2], True, done, f"scatter_wait_in_{l}")
        done = recv["w_in"][l]
    mine["w_in"] = _reduce4(recv["w_in"], "reduce4_w_in")
    swap_b, token = _swap_start([mine["w_in"]], mine["w_in"], "swap_start_b")

    gsmall["conv_w"] = lax.dynamic_slice_in_dim(gsmall["conv_w"], q * (CONV_WIDTH // N_CHIPS), CONV_WIDTH // N_CHIPS, axis=2)
    gsmall["ffn_conv_w"] = lax.dynamic_slice_in_dim(gsmall["ffn_conv_w"], q * (f2 // N_CHIPS), f2 // N_CHIPS, axis=2)
    snames = SMALL_NAMES + SHARDED_SMALL
    res = _adamw_many([w[n] for n in snames], [m[n] for n in snames], [v[n] for n in snames], [gsmall[n] for n in snames],
                      "adamw_small")
    for n, (d_, m_, v_) in zip(snames, res):
        put(n, (gsmall[n], d_, m_, v_))

    sent, theirs = _swap_wait(swap_a, deltas["mod_b"], "swap_wait_a")
    sent_b, theirs_b = _swap_wait(swap_b, sent[0], "swap_wait_b")
    for n, a, b in zip(big, list(sent) + list(sent_b), list(theirs) + list(theirs_b)):
        if n == "w_in":
            res = _adamw(*[jnp.swapaxes(t[n], 1, 2) for t in (w, m, v)], [a, b], f"adamw_{n}")
            put(n, [jnp.swapaxes(r_, 1, 2) for r_ in res])
        else:
            put(n, _adamw(w[n], m[n], v[n], [a, b], f"adamw_{n}"))

    return (loss, dx.reshape(1, s, d), *[grads[n] for n in WEIGHT_ORDER], *[deltas[n] for n in WEIGHT_ORDER],
            *[new_m[n] for n in WEIGHT_ORDER], *[new_v[n] for n in WEIGHT_ORDER])


def kernel(x, c, mod_w, mod_b, mix_pre_g, mix_post_g, w_in, sgu_norm_g, sgu_norm_b, sgu_w, sgu_b, conv_w, conv_b, conv_norm_g, conv_norm_b, pool_w, pool_scale, branch_g, w_out, ffn_pre_g, ffn_post_g, ffn_up, ffn_conv_w, ffn_conv_b, ffn_down, loss_target, m_mod_w, m_mod_b, m_mix_pre_g, m_mix_post_g, m_w_in, m_sgu_norm_g, m_sgu_norm_b, m_sgu_w, m_sgu_b, m_conv_w, m_conv_b, m_conv_norm_g, m_conv_norm_b, m_pool_w, m_pool_scale, m_branch_g, m_w_out, m_ffn_pre_g, m_ffn_post_g, m_ffn_up, m_ffn_conv_w, m_ffn_conv_b, m_ffn_down, v_mod_w, v_mod_b, v_mix_pre_g, v_mix_post_g, v_w_in, v_sgu_norm_g, v_sgu_norm_b, v_sgu_w, v_sgu_b, v_conv_w, v_conv_b, v_conv_norm_g, v_conv_norm_b, v_pool_w, v_pool_scale, v_branch_g, v_w_out, v_ffn_pre_g, v_ffn_post_g, v_ffn_up, v_ffn_conv_w, v_ffn_conv_b, v_ffn_down):
    w = dict(mod_w=mod_w, mod_b=mod_b, mix_pre_g=mix_pre_g, mix_post_g=mix_post_g, w_in=w_in, sgu_norm_g=sgu_norm_g,
             sgu_norm_b=sgu_norm_b, sgu_w=sgu_w, sgu_b=sgu_b, conv_w=conv_w, conv_b=conv_b, conv_norm_g=conv_norm_g,
             conv_norm_b=conv_norm_b, pool_w=pool_w, pool_scale=pool_scale, branch_g=branch_g, w_out=w_out,
             ffn_pre_g=ffn_pre_g, ffn_post_g=ffn_post_g, ffn_up=ffn_up, ffn_conv_w=ffn_conv_w, ffn_conv_b=ffn_conv_b,
             ffn_down=ffn_down)
    m = dict(mod_w=m_mod_w, mod_b=m_mod_b, mix_pre_g=m_mix_pre_g, mix_post_g=m_mix_post_g, w_in=m_w_in,
             sgu_norm_g=m_sgu_norm_g, sgu_norm_b=m_sgu_norm_b, sgu_w=m_sgu_w, sgu_b=m_sgu_b, conv_w=m_conv_w,
             conv_b=m_conv_b, conv_norm_g=m_conv_norm_g, conv_norm_b=m_conv_norm_b, pool_w=m_pool_w,
             pool_scale=m_pool_scale, branch_g=m_branch_g, w_out=m_w_out, ffn_pre_g=m_ffn_pre_g, ffn_post_g=m_ffn_post_g,
             ffn_up=m_ffn_up, ffn_conv_w=m_ffn_conv_w, ffn_conv_b=m_ffn_conv_b, ffn_down=m_ffn_down)
    v = dict(mod_w=v_mod_w, mod_b=v_mod_b, mix_pre_g=v_mix_pre_g, mix_post_g=v_mix_post_g, w_in=v_w_in,
             sgu_norm_g=v_sgu_norm_g, sgu_norm_b=v_sgu_norm_b, sgu_w=v_sgu_w, sgu_b=v_sgu_b, conv_w=v_conv_w,
             conv_b=v_conv_b, conv_norm_g=v_conv_norm_g, conv_norm_b=v_conv_norm_b, pool_w=v_pool_w,
             pool_scale=v_pool_scale, branch_g=v_branch_g, w_out=v_w_out, ffn_pre_g=v_ffn_pre_g, ffn_post_g=v_ffn_post_g,
             ffn_up=v_ffn_up, ffn_conv_w=v_ffn_conv_w, ffn_conv_b=v_ffn_conv_b, ffn_down=v_ffn_down)
    return _step(x, c, loss_target, w, m, v)
```
